```python
import jax, jax.numpy as jnp
from jax import lax
import numpy as np

D_MODEL = 1024
BATCH = 16
SEQ = 2048
DEPTH = 4

SB_HEADS = 8
SB_HEAD_DIM = 64
SB_WIDTH = SB_HEADS * SB_HEAD_DIM
SB_BLOCK = 128
DN_HEADS = 4
DN_HEAD_DIM = 128
DN_WIDTH = DN_HEADS * DN_HEAD_DIM
DN_CHUNK = 64
CONV_WIDTH = 4
MIX_WIDTH = SB_WIDTH + DN_WIDTH
IN_WIDTH = 3 * SB_WIDTH + 4 * DN_WIDTH + 2 * DN_HEADS
D_FF = 4 * D_MODEL
N_MOD = 6
EPS = 1e-6

kernel_name = "stickbreak_gdn_hybrid_adaln"


def rmsnorm(x, gain):
    xf = x.astype(jnp.float32)
    y = xf * lax.rsqrt(jnp.mean(xf * xf, axis=-1, keepdims=True) + EPS)
    return (y * gain.astype(jnp.float32)).astype(x.dtype)


def l2norm(x):
    return x * lax.rsqrt(jnp.sum(x * x, axis=-1, keepdims=True) + EPS)


def causal_depthwise_conv(x, w):
    k_w, ch = w.shape
    return lax.conv_general_dilated(
        x, w[:, None, :].astype(x.dtype), window_strides=(1,), padding=((k_w - 1, 0),),
        dimension_numbers=("NWC", "WIO", "NWC"), feature_group_count=ch)


def stick_breaking_attention(q, k, v):
    t_len = q.shape[2]
    scale = SB_HEAD_DIM ** -0.5
    outs = []
    for blk in range(t_len // SB_BLOCK):
        start = blk * SB_BLOCK
        end = start + SB_BLOCK
        qb = q[:, :, start:end]
        kb = k[:, :, :end]
        vb = v[:, :, :end]
        z = jnp.einsum("bhqd,bhkd->bhqk", qb, kb, preferred_element_type=jnp.float32) * scale
        t_idx = start + jnp.arange(SB_BLOCK)[:, None]
        s_idx = jnp.arange(end)[None, :]
        causal = s_idx < t_idx
        log_1m_beta = jnp.where(causal, jax.nn.log_sigmoid(-z), 0.0)
        suffix = lax.cumsum(log_1m_beta, axis=3, reverse=True) - log_1m_beta
        att = jnp.where(causal, jnp.exp(jax.nn.log_sigmoid(z) + suffix), 0.0)
        outs.append(jnp.einsum("bhqk,bhkd->bhqd", att.astype(vb.dtype), vb))
    return jnp.concatenate(outs, axis=2)


def chunk_gated_delta_rule(q, k, v, g, beta):
    b, h, t_len, dk = q.shape
    dv = v.shape[-1]
    n = t_len // DN_CHUNK
    c = DN_CHUNK
    q = q * dk ** -0.5
    qc = q.reshape(b, h, n, c, dk)
    kc = k.reshape(b, h, n, c, dk)
    vc = v.reshape(b, h, n, c, dv)
    bc = beta.reshape(b, h, n, c)
    gc = jnp.cumsum(g.reshape(b, h, n, c), axis=-1)
    tril_incl = jnp.tril(jnp.ones((c, c), dtype=bool))
    tril_strict = jnp.tril(jnp.ones((c, c), dtype=bool), -1)
    diff = jnp.where(tril_incl, gc[..., :, None] - gc[..., None, :], 0.0)
    decay = jnp.where(tril_incl, jnp.exp(diff), 0.0)
    k_beta = kc * bc[..., None]
    v_beta = vc * bc[..., None]
    lower = jnp.where(tril_strict, jnp.einsum("bhnid,bhnjd->bhnij", k_beta, kc) * decay, 0.0)
    eye = jnp.eye(c, dtype=jnp.float32)
    a_mat = lower + eye
    t_mat = lax.linalg.triangular_solve(a_mat, jnp.broadcast_to(eye, a_mat.shape),
                                        left_side=True, lower=True, unit_diagonal=True)
    u = jnp.einsum("bhnij,bhnje->bhnie", t_mat, v_beta)
    w = jnp.einsum("bhnij,bhnjd->bhnid", t_mat, k_beta * jnp.exp(gc)[..., None])
    intra = jnp.where(tril_incl, jnp.einsum("bhnid,bhnjd->bhnij", qc, kc) * decay, 0.0)

    def step(state, inp):
        q_n, k_n, u_n, w_n, g_n, a_n = inp
        v_new = u_n - jnp.einsum("bhcd,bhde->bhce", w_n, state)
        o_n = (jnp.einsum("bhcd,bhde->bhce", q_n * jnp.exp(g_n)[..., None], state)
               + jnp.einsum("bhij,bhje->bhie", a_n, v_new))
        g_last = g_n[..., -1:]
        state = (state * jnp.exp(g_last)[..., None]
                 + jnp.einsum("bhcd,bhce->bhde", k_n * jnp.exp(g_last - g_n)[..., None], v_new))
        return state, o_n

    xs = tuple(jnp.moveaxis(a, 2, 0) for a in (qc, kc, u, w, gc, intra))
    state0 = jnp.zeros((b, h, dk, dv), jnp.float32)
    _, o = lax.scan(step, state0, xs)
    return jnp.moveaxis(o, 0, 2).reshape(b, h, t_len, dv)


def hybrid_mixer(h, w_in, sb_q_norm, sb_k_norm, conv_w, a_log, dt_bias, dn_out_norm, w_out):
    b, t_len, _ = h.shape
    proj = h @ w_in
    cuts = [int(i) for i in np.cumsum([SB_WIDTH, SB_WIDTH, SB_WIDTH, 3 * DN_WIDTH, DN_WIDTH, DN_HEADS])]
    sb_q, sb_k, sb_v, dn_qkv, dn_z, dn_a, dn_b = jnp.split(proj, cuts, axis=-1)

    def sb_heads(a):
        return a.reshape(b, t_len, SB_HEADS, SB_HEAD_DIM)
    q_a = rmsnorm(sb_heads(sb_q), sb_q_norm).transpose(0, 2, 1, 3)
    k_a = rmsnorm(sb_heads(sb_k), sb_k_norm).transpose(0, 2, 1, 3)
    v_a = sb_heads(sb_v).transpose(0, 2, 1, 3)
    o_a = stick_breaking_attention(q_a, k_a, v_a)
    o_a = o_a.transpose(0, 2, 1, 3).reshape(b, t_len, SB_WIDTH)

    qkv = jax.nn.silu(causal_depthwise_conv(dn_qkv, conv_w)).astype(jnp.float32)
    q_b, k_b, v_b = jnp.split(qkv, 3, axis=-1)
    def dn_heads(a):
        return a.reshape(b, t_len, DN_HEADS, DN_HEAD_DIM).transpose(0, 2, 1, 3)
    q_b = l2norm(dn_heads(q_b))
    k_b = l2norm(dn_heads(k_b))
    v_b = dn_heads(v_b)
    beta = jax.nn.sigmoid(dn_b.astype(jnp.float32)).transpose(0, 2, 1)
    g = (-jnp.exp(a_log.astype(jnp.float32))
         * jax.nn.softplus(dn_a.astype(jnp.float32) + dt_bias.astype(jnp.float32))).transpose(0, 2, 1)
    o_b = chunk_gated_delta_rule(q_b, k_b, v_b, g, beta)
    o_b = o_b.transpose(0, 2, 1, 3).astype(h.dtype)
    z = dn_z.reshape(b, t_len, DN_HEADS, DN_HEAD_DIM)
    o_b = (rmsnorm(o_b, dn_out_norm) * jax.nn.silu(z)).reshape(b, t_len, DN_WIDTH)

    return jnp.concatenate([o_a, o_b], axis=-1) @ w_out


def _fwd_setup_inputs(seed: int = 0) -> dict:
    key = jax.random.key(seed)
    ks = jax.random.split(key, 20)
    f32 = jnp.float32
    x = jax.random.normal(ks[0], (BATCH, SEQ, D_MODEL), f32)
    c = jax.random.normal(ks[1], (BATCH, D_MODEL), f32)
    w_ada = jax.random.normal(ks[2], (DEPTH, D_MODEL, N_MOD * D_MODEL), f32) * (0.5 * D_MODEL ** -0.5)
    b_ada = jax.random.normal(ks[3], (DEPTH, N_MOD * D_MODEL), f32) * 0.01
    norm_mix = 1.0 + 0.02 * jax.random.normal(ks[4], (DEPTH, D_MODEL), f32)
    norm_mlp = 1.0 + 0.02 * jax.random.normal(ks[5], (DEPTH, D_MODEL), f32)
    w_in = jax.random.normal(ks[6], (DEPTH, D_MODEL, IN_WIDTH), f32) * D_MODEL ** -0.5
    sb_q_norm = 1.0 + 0.02 * jax.random.normal(ks[7], (DEPTH, SB_HEAD_DIM), f32)
    sb_k_norm = 1.0 + 0.02 * jax.random.normal(ks[8], (DEPTH, SB_HEAD_DIM), f32)
    conv_w = jax.random.normal(ks[9], (DEPTH, CONV_WIDTH, 3 * DN_WIDTH), f32) * CONV_WIDTH ** -0.5
    a_log = jnp.log(jax.random.uniform(ks[10], (DEPTH, DN_HEADS), f32, 1.0, 16.0))
    dt = jnp.exp(jax.random.uniform(ks[11], (DEPTH, DN_HEADS), f32, np.log(1e-3), np.log(1e-1)))
    dt_bias = dt + jnp.log(-jnp.expm1(-dt))
    dn_out_norm = 1.0 + 0.02 * jax.random.normal(ks[12], (DEPTH, DN_HEAD_DIM), f32)
    w_out = jax.random.normal(ks[13], (DEPTH, MIX_WIDTH, D_MODEL), f32) * MIX_WIDTH ** -0.5
    w_ff1 = jax.random.normal(ks[14], (DEPTH, D_MODEL, D_FF), f32) * D_MODEL ** -0.5
    w_ff2 = jax.random.normal(ks[15], (DEPTH, D_FF, D_MODEL), f32) * D_FF ** -0.5
    return {"x": x, "c": c, "w_ada": w_ada, "b_ada": b_ada, "norm_mix": norm_mix,
            "norm_mlp": norm_mlp, "w_in": w_in, "sb_q_norm": sb_q_norm, "sb_k_norm": sb_k_norm,
            "conv_w": conv_w, "a_log": a_log, "dt_bias": dt_bias, "dn_out_norm": dn_out_norm,
            "w_out": w_out, "w_ff1": w_ff1, "w_ff2": w_ff2}


def _fwd_reference(x, c, w_ada, b_ada, norm_mix, norm_mlp, w_in, sb_q_norm, sb_k_norm,
              conv_w, a_log, dt_bias, dn_out_norm, w_out, w_ff1, w_ff2):
    cond = jax.nn.silu(c)
    for l in range(DEPTH):
        mod = cond @ w_ada[l] + b_ada[l]
        sh_a, sc_a, g_a, sh_m, sc_m, g_m = [m[:, None, :] for m in jnp.split(mod, N_MOD, axis=-1)]
        h = rmsnorm(x, norm_mix[l]) * (1.0 + sc_a) + sh_a
        x = x + g_a * hybrid_mixer(h, w_in[l], sb_q_norm[l], sb_k_norm[l], conv_w[l],
                                   a_log[l], dt_bias[l], dn_out_norm[l], w_out[l])
        h = rmsnorm(x, norm_mlp[l]) * (1.0 + sc_m) + sh_m
        x = x + g_m * (jnp.square(jax.nn.relu(h @ w_ff1[l])) @ w_ff2[l])
    return x


import jax as _jax
import jax.numpy as _jnp

TWIN_FORMAT = 'train_step'
FWD_PARAMS = ['x', 'c', 'w_ada', 'b_ada', 'norm_mix', 'norm_mlp', 'w_in', 'sb_q_norm', 'sb_k_norm', 'conv_w', 'a_log', 'dt_bias', 'dn_out_norm', 'w_out', 'w_ff1', 'w_ff2']
TWIN_WEIGHTS = ['w_ada', 'b_ada', 'norm_mix', 'norm_mlp', 'w_in', 'sb_q_norm', 'sb_k_norm', 'conv_w', 'a_log', 'dt_bias', 'dn_out_norm', 'w_out', 'w_ff1', 'w_ff2']
TWIN_DIFF_INPUT = 'x'
TWIN_INPUTS = ['x', 'c', 'w_ada', 'b_ada', 'norm_mix', 'norm_mlp', 'w_in', 'sb_q_norm', 'sb_k_norm', 'conv_w', 'a_log', 'dt_bias', 'dn_out_norm', 'w_out', 'w_ff1', 'w_ff2', 'loss_target', 'm_w_ada', 'm_b_ada', 'm_norm_mix', 'm_norm_mlp', 'm_w_in', 'm_sb_q_norm', 'm_sb_k_norm', 'm_conv_w', 'm_a_log', 'm_dt_bias', 'm_dn_out_norm', 'm_w_out', 'm_w_ff1', 'm_w_ff2', 'v_w_ada', 'v_b_ada', 'v_norm_mix', 'v_norm_mlp', 'v_w_in', 'v_sb_q_norm', 'v_sb_k_norm', 'v_conv_w', 'v_a_log', 'v_dt_bias', 'v_dn_out_norm', 'v_w_out', 'v_w_ff1', 'v_w_ff2']
TWIN_OUTPUTS = ['loss', 'grad_x', 'grad_w_ada', 'grad_b_ada', 'grad_norm_mix', 'grad_norm_mlp', 'grad_w_in', 'grad_sb_q_norm', 'grad_sb_k_norm', 'grad_conv_w', 'grad_a_log', 'grad_dt_bias', 'grad_dn_out_norm', 'grad_w_out', 'grad_w_ff1', 'grad_w_ff2', 'delta_w_ada', 'delta_b_ada', 'delta_norm_mix', 'delta_norm_mlp', 'delta_w_in', 'delta_sb_q_norm', 'delta_sb_k_norm', 'delta_conv_w', 'delta_a_log', 'delta_dt_bias', 'delta_dn_out_norm', 'delta_w_out', 'delta_w_ff1', 'delta_w_ff2', 'new_m_w_ada', 'new_m_b_ada', 'new_m_norm_mix', 'new_m_norm_mlp', 'new_m_w_in', 'new_m_sb_q_norm', 'new_m_sb_k_norm', 'new_m_conv_w', 'new_m_a_log', 'new_m_dt_bias', 'new_m_dn_out_norm', 'new_m_w_out', 'new_m_w_ff1', 'new_m_w_ff2', 'new_v_w_ada', 'new_v_b_ada', 'new_v_norm_mix', 'new_v_norm_mlp', 'new_v_w_in', 'new_v_sb_q_norm', 'new_v_sb_k_norm', 'new_v_conv_w', 'new_v_a_log', 'new_v_dt_bias', 'new_v_dn_out_norm', 'new_v_w_out', 'new_v_w_ff1', 'new_v_w_ff2']
TWIN_LEAF_KINDS = {'loss': 'loss', 'grad_x': 'grad_x', 'grad_w_ada': 'grad_w', 'grad_b_ada': 'grad_w', 'grad_norm_mix': 'grad_w', 'grad_norm_mlp': 'grad_w', 'grad_w_in': 'grad_w', 'grad_sb_q_norm': 'grad_w', 'grad_sb_k_norm': 'grad_w', 'grad_conv_w': 'grad_w', 'grad_a_log': 'grad_w', 'grad_dt_bias': 'grad_w', 'grad_dn_out_norm': 'grad_w', 'grad_w_out': 'grad_w', 'grad_w_ff1': 'grad_w', 'grad_w_ff2': 'grad_w', 'delta_w_ada': 'delta_w', 'delta_b_ada': 'delta_w', 'delta_norm_mix': 'delta_w', 'delta_norm_mlp': 'delta_w', 'delta_w_in': 'delta_w', 'delta_sb_q_norm': 'delta_w', 'delta_sb_k_norm': 'delta_w', 'delta_conv_w': 'delta_w', 'delta_a_log': 'delta_w', 'delta_dt_bias': 'delta_w', 'delta_dn_out_norm': 'delta_w', 'delta_w_out': 'delta_w', 'delta_w_ff1': 'delta_w', 'delta_w_ff2': 'delta_w', 'new_m_w_ada': 'new_m', 'new_m_b_ada': 'new_m', 'new_m_norm_mix': 'new_m', 'new_m_norm_mlp': 'new_m', 'new_m_w_in': 'new_m', 'new_m_sb_q_norm': 'new_m', 'new_m_sb_k_norm': 'new_m', 'new_m_conv_w': 'new_m', 'new_m_a_log': 'new_m', 'new_m_dt_bias': 'new_m', 'new_m_dn_out_norm': 'new_m', 'new_m_w_out': 'new_m', 'new_m_w_ff1': 'new_m', 'new_m_w_ff2': 'new_m', 'new_v_w_ada': 'new_v', 'new_v_b_ada': 'new_v', 'new_v_norm_mix': 'new_v', 'new_v_norm_mlp': 'new_v', 'new_v_w_in': 'new_v', 'new_v_sb_q_norm': 'new_v', 'new_v_sb_k_norm': 'new_v', 'new_v_conv_w': 'new_v', 'new_v_a_log': 'new_v', 'new_v_dt_bias': 'new_v', 'new_v_dn_out_norm': 'new_v', 'new_v_w_out': 'new_v', 'new_v_w_ff1': 'new_v', 'new_v_w_ff2': 'new_v'}


def _forward(args):
    return _fwd_reference(*[args[k] for k in FWD_PARAMS])


def _output_shape():
    out = _jax.eval_shape(lambda: _forward(_fwd_setup_inputs(0)))
    return out.shape, out.dtype

N_MICROBATCH = 1
ADAM_LR = 0.001
ADAM_B1 = 0.9
ADAM_B2 = 0.999
ADAM_EPS = 1e-08
ADAM_WD = 0.01
ADAM_STEP = 10
PER_EXAMPLE_BATCH_AXIS = {'x': 0, 'c': 0, 'loss_target': 0}
SHARED_INPUTS = []
_WEIGHT_DTYPES = {'w_ada': _jnp.float32, 'b_ada': _jnp.float32, 'norm_mix': _jnp.float32, 'norm_mlp': _jnp.float32, 'w_in': _jnp.float32, 'sb_q_norm': _jnp.float32, 'sb_k_norm': _jnp.float32, 'conv_w': _jnp.float32, 'a_log': _jnp.float32, 'dt_bias': _jnp.float32, 'dn_out_norm': _jnp.float32, 'w_out': _jnp.float32, 'w_ff1': _jnp.float32, 'w_ff2': _jnp.float32}
MOMENT_SCALE = {'w_ada': 3.919524e+00, 'b_ada': 8.050055e+00, 'norm_mix': 1.517840e+00, 'norm_mlp': 1.271191e+01, 'w_in': 4.364501e-01, 'sb_q_norm': 1.418862e+00, 'sb_k_norm': 1.423238e+00, 'conv_w': 2.781136e-01, 'a_log': 3.937775e+00, 'dt_bias': 3.760254e+00, 'dn_out_norm': 5.699069e+00, 'w_out': 8.358915e-01, 'w_ff1': 6.428999e-01, 'w_ff2': 2.314249e+00}


def _to_microbatches(a, axis):
    t = _jnp.moveaxis(a, axis, 0)
    t = t.reshape((N_MICROBATCH, t.shape[0] // N_MICROBATCH) + t.shape[1:])
    return _jnp.moveaxis(t, 1, axis + 1)


def setup_inputs(seed: int = 0) -> dict:
    inp = _fwd_setup_inputs(seed)
    key = _jax.random.fold_in(_jax.random.key(seed), 7919)
    shape, _ = _output_shape()
    out = dict(inp)
    out["loss_target"] = _jax.random.normal(_jax.random.fold_in(key, 0), shape, _jnp.float32)
    for i, name in enumerate(TWIN_WEIGHTS):
        w = inp[name].astype(_jnp.float32)
        if MOMENT_SCALE is None:
            s = _jnp.sqrt(_jnp.mean(_jnp.square(w)) + 1e-30)
        else:
            s = MOMENT_SCALE[name]
        km, kv = _jax.random.split(_jax.random.fold_in(key, i + 1))
        out[name] = w
        out["m_" + name] = s * _jax.random.normal(km, w.shape, _jnp.float32)
        out["v_" + name] = (s * s) * _jax.random.uniform(kv, w.shape, _jnp.float32, 0.5, 1.5)
    if N_MICROBATCH > 1:
        for name, axis in PER_EXAMPLE_BATCH_AXIS.items():
            out[name] = _to_microbatches(out[name], axis)
    return {'x': out['x'], 'c': out['c'], 'w_ada': out['w_ada'], 'b_ada': out['b_ada'], 'norm_mix': out['norm_mix'], 'norm_mlp': out['norm_mlp'], 'w_in': out['w_in'], 'sb_q_norm': out['sb_q_norm'], 'sb_k_norm': out['sb_k_norm'], 'conv_w': out['conv_w'], 'a_log': out['a_log'], 'dt_bias': out['dt_bias'], 'dn_out_norm': out['dn_out_norm'], 'w_out': out['w_out'], 'w_ff1': out['w_ff1'], 'w_ff2': out['w_ff2'], 'loss_target': out['loss_target'], 'm_w_ada': out['m_w_ada'], 'm_b_ada': out['m_b_ada'], 'm_norm_mix': out['m_norm_mix'], 'm_norm_mlp': out['m_norm_mlp'], 'm_w_in': out['m_w_in'], 'm_sb_q_norm': out['m_sb_q_norm'], 'm_sb_k_norm': out['m_sb_k_norm'], 'm_conv_w': out['m_conv_w'], 'm_a_log': out['m_a_log'], 'm_dt_bias': out['m_dt_bias'], 'm_dn_out_norm': out['m_dn_out_norm'], 'm_w_out': out['m_w_out'], 'm_w_ff1': out['m_w_ff1'], 'm_w_ff2': out['m_w_ff2'], 'v_w_ada': out['v_w_ada'], 'v_b_ada': out['v_b_ada'], 'v_norm_mix': out['v_norm_mix'], 'v_norm_mlp': out['v_norm_mlp'], 'v_w_in': out['v_w_in'], 'v_sb_q_norm': out['v_sb_q_norm'], 'v_sb_k_norm': out['v_sb_k_norm'], 'v_conv_w': out['v_conv_w'], 'v_a_log': out['v_a_log'], 'v_dt_bias': out['v_dt_bias'], 'v_dn_out_norm': out['v_dn_out_norm'], 'v_w_out': out['v_w_out'], 'v_w_ff1': out['v_w_ff1'], 'v_w_ff2': out['v_w_ff2']}


def _loss(weights, diff, rest, loss_target):
    with _jax.named_scope("forward"):
        args = {**rest, TWIN_DIFF_INPUT: diff, **{k: w.astype(_WEIGHT_DTYPES[k]) for k, w in weights.items()}}
        y = _forward(args)
    with _jax.named_scope("loss_head"):
        err = _jnp.square(y.astype(_jnp.float32) - loss_target)
        return 0.5 * _jnp.sum(_jnp.mean(err, axis=-1)) if err.ndim else 0.5 * err


def _adamw(w, g, m, v):
    m = ADAM_B1 * m + (1.0 - ADAM_B1) * g
    v = ADAM_B2 * v + (1.0 - ADAM_B2) * _jnp.square(g)
    m_hat = m / (1.0 - ADAM_B1 ** ADAM_STEP)
    v_hat = v / (1.0 - ADAM_B2 ** ADAM_STEP)
    delta = -ADAM_LR * (m_hat / (_jnp.sqrt(v_hat) + ADAM_EPS) + ADAM_WD * w)
    return delta, m, v


def reference(x, c, w_ada, b_ada, norm_mix, norm_mlp, w_in, sb_q_norm, sb_k_norm, conv_w, a_log, dt_bias, dn_out_norm, w_out, w_ff1, w_ff2, loss_target, m_w_ada, m_b_ada, m_norm_mix, m_norm_mlp, m_w_in, m_sb_q_norm, m_sb_k_norm, m_conv_w, m_a_log, m_dt_bias, m_dn_out_norm, m_w_out, m_w_ff1, m_w_ff2, v_w_ada, v_b_ada, v_norm_mix, v_norm_mlp, v_w_in, v_sb_q_norm, v_sb_k_norm, v_conv_w, v_a_log, v_dt_bias, v_dn_out_norm, v_w_out, v_w_ff1, v_w_ff2):
    given = dict(x=x, c=c, w_ada=w_ada, b_ada=b_ada, norm_mix=norm_mix, norm_mlp=norm_mlp, w_in=w_in, sb_q_norm=sb_q_norm, sb_k_norm=sb_k_norm, conv_w=conv_w, a_log=a_log, dt_bias=dt_bias, dn_out_norm=dn_out_norm, w_out=w_out, w_ff1=w_ff1, w_ff2=w_ff2, loss_target=loss_target, m_w_ada=m_w_ada, m_b_ada=m_b_ada, m_norm_mix=m_norm_mix, m_norm_mlp=m_norm_mlp, m_w_in=m_w_in, m_sb_q_norm=m_sb_q_norm, m_sb_k_norm=m_sb_k_norm, m_conv_w=m_conv_w, m_a_log=m_a_log, m_dt_bias=m_dt_bias, m_dn_out_norm=m_dn_out_norm, m_w_out=m_w_out, m_w_ff1=m_w_ff1, m_w_ff2=m_w_ff2, v_w_ada=v_w_ada, v_b_ada=v_b_ada, v_norm_mix=v_norm_mix, v_norm_mlp=v_norm_mlp, v_w_in=v_w_in, v_sb_q_norm=v_sb_q_norm, v_sb_k_norm=v_sb_k_norm, v_conv_w=v_conv_w, v_a_log=v_a_log, v_dt_bias=v_dt_bias, v_dn_out_norm=v_dn_out_norm, v_w_out=v_w_out, v_w_ff1=v_w_ff1, v_w_ff2=v_w_ff2)
    weights = {n: given[n] for n in TWIN_WEIGHTS}
    shared = {n: given[n] for n in SHARED_INPUTS}
    per_example = {n: given[n] for n in ['x', 'c']}
    grad_fn = _jax.value_and_grad(_loss, argnums=(0, 1))

    def one_microbatch(ex, loss_target):
        ex = dict(ex)
        diff = ex.pop(TWIN_DIFF_INPUT)
        return grad_fn(weights, diff, {**shared, **ex}, loss_target)

    if N_MICROBATCH == 1:
        loss, (grad_w, grad_x) = one_microbatch(per_example, given["loss_target"])
    else:
        def body(carry, xs):
            loss_sum, grad_sum = carry
            l_k, (gw_k, gx_k) = one_microbatch(xs[0], xs[1])
            with _jax.named_scope("update"):
                return (loss_sum + l_k, _jax.tree.map(_jnp.add, grad_sum, gw_k)), gx_k

        init = (_jnp.zeros((), _jnp.float32), _jax.tree.map(_jnp.zeros_like, weights))
        (loss, grad_w), grad_x = _jax.lax.scan(body, init, (per_example, given["loss_target"]))
    with _jax.named_scope("update"):
        delta_w, new_m, new_v = {}, {}, {}
        for n in TWIN_WEIGHTS:
            delta_w[n], new_m[n], new_v[n] = _adamw(weights[n], grad_w[n], given["m_" + n], given["v_" + n])
    return (loss, grad_x, *[grad_w[n] for n in TWIN_WEIGHTS], *[delta_w[n] for n in TWIN_WEIGHTS],
            *[new_m[n] for n in TWIN_WEIGHTS], *[new_v[n] for n in TWIN_WEIGHTS])
```

```python
import functools
import math

import jax
import jax.numpy as jnp
from jax import lax
from jax.experimental import pallas as pl
from jax.experimental.pallas import tpu as pltpu

F32 = jnp.float32
BF16 = jnp.bfloat16
EPS = 1e-6
N_DEV = 8
AXES = ("x", "y", "c")

SB_HEADS, SB_HD = 8, 64
SB_W = SB_HEADS * SB_HD
SB_BLK = 128
DN_HEADS, DN_HD = 4, 128
DN_W = DN_HEADS * DN_HD
DN_C = 64
CONV_K = 4
IN_W = 3 * SB_W + 4 * DN_W + 2 * DN_HEADS
IN_WP = 3 * SB_W + 4 * DN_W + 128
COL_DNQKV = 3 * SB_W // 128
COL_Z = COL_DNQKV + 3 * DN_W // 128
COL_AB = COL_Z + DN_W // 128

ADAM_LR, ADAM_B1, ADAM_B2, ADAM_EPS, ADAM_WD, ADAM_STEP = 0.001, 0.9, 0.999, 1e-08, 0.01, 10

VMEM_LIMIT = 56 * 1024 * 1024


def _cp(sem):
    return pltpu.CompilerParams(dimension_semantics=sem, vmem_limit_bytes=VMEM_LIMIT)


def _pick(dim, pref):
    return pref if dim % pref == 0 else dim


def _sds(shape, dtype):
    return jax.ShapeDtypeStruct(tuple(shape), dtype)


_NN = (((1,), (0,)), ((), ()))
_NT = (((1,), (1,)), ((), ()))
_TN = (((0,), (0,)), ((), ()))


def _bdot(a, b, dims=_NN):
    return lax.dot_general(a.astype(BF16), b.astype(BF16), dims, preferred_element_type=F32)


def _split(a):
    hi = a.astype(BF16)
    lo = (a - hi.astype(F32)).astype(BF16)
    return hi, lo


def _pdot(a, b, dims=_NN):
    ah, al = _split(a)
    bh, bl = _split(b)
    d = functools.partial(lax.dot_general, dimension_numbers=dims, preferred_element_type=F32)
    return d(ah, bh) + (d(ah, bl) + d(al, bh))


def _sigmoid(x):
    return 1.0 / (1.0 + jnp.exp(-x))


def _softplus(x):
    return jnp.maximum(x, 0.0) + jnp.log(1.0 + jnp.exp(-jnp.abs(x)))


def matmul(a, b, *, mode, name, out_dtypes=(F32,), a_fn=None, a_extras=(), b_fn=None, b_extras=(),
           epi=None, extras=(), tm=512, tn=512, tk=512):
    if mode == "tn":
        K, M = a.shape
    else:
        M, K = a.shape
    N = b.shape[0] if mode == "nt" else b.shape[1]
    tm, tn, tk = _pick(M, tm), _pick(N, tn), _pick(K, tk)
    nk = K // tk
    dims = {"nn": _NN, "nt": _NT, "tn": _TN}[mode]
    a_spec = pl.BlockSpec((tk, tm), lambda i, j, k: (k, i)) if mode == "tn" else pl.BlockSpec((tm, tk), lambda i, j, k: (i, k))
    b_spec = pl.BlockSpec((tn, tk), lambda i, j, k: (j, k)) if mode == "nt" else pl.BlockSpec((tk, tn), lambda i, j, k: (k, j))
    na, nb, ne, no = len(a_extras), len(b_extras), len(extras), len(out_dtypes)

    def body(*refs):
        a_ref, b_ref = refs[0], refs[1]
        ax = refs[2:2 + na]
        bx = refs[2 + na:2 + na + nb]
        ex = refs[2 + na + nb:2 + na + nb + ne]
        outs = refs[2 + na + nb + ne:2 + na + nb + ne + no]
        acc_ref = refs[-1]
        k = pl.program_id(2)

        @pl.when(k == 0)
        def _():
            acc_ref[...] = jnp.zeros_like(acc_ref)

        av = a_ref[...]
        if a_fn is not None:
            av = a_fn(av, *[r[...] for r in ax])
        bv = b_ref[...]
        if b_fn is not None:
            bv = b_fn(bv, *[r[...] for r in bx])
        acc_ref[...] += lax.dot_general(av.astype(BF16), bv.astype(BF16), dims, preferred_element_type=F32)

        @pl.when(k == nk - 1)
        def _():
            res = acc_ref[...]
            res = (res,) if epi is None else epi(res, *[r[...] for r in ex])
            for o_ref, r in zip(outs, res):
                o_ref[...] = r.astype(o_ref.dtype)

    xs = list(a_extras) + list(b_extras) + list(extras)
    return pl.pallas_call(
        body,
        grid=(M // tm, N // tn, nk),
        in_specs=[a_spec, b_spec] + [pl.BlockSpec(bs, im) for _, bs, im in xs],
        out_specs=[pl.BlockSpec((tm, tn), lambda i, j, k: (i, j)) for _ in out_dtypes],
        out_shape=[_sds((M, N), dt) for dt in out_dtypes],
        scratch_shapes=[pltpu.VMEM((tm, tn), F32)],
        compiler_params=_cp(("parallel", "parallel", "arbitrary")),
        name=name,
    )(a, b, *[x for x, _, _ in xs])


def ln_mod_fwd(x, gain, sc, sh, name):
    B, T, D = x.shape
    tt = _pick(T, 512)

    def body(x_ref, g_ref, sc_ref, sh_ref, h_ref):
        xv = x_ref[0]
        r = lax.rsqrt(jnp.mean(xv * xv, axis=-1, keepdims=True) + EPS)
        h = (xv * r * g_ref[...]) * (1.0 + sc_ref[0]) + sh_ref[0]
        h_ref[0] = h.astype(h_ref.dtype)

    return pl.pallas_call(
        body, grid=(B, T // tt),
        in_specs=[pl.BlockSpec((1, tt, D), lambda b, t: (b, t, 0)), pl.BlockSpec((1, D), lambda b, t: (0, 0)),
                  pl.BlockSpec((1, 1, D), lambda b, t: (b, 0, 0)), pl.BlockSpec((1, 1, D), lambda b, t: (b, 0, 0))],
        out_specs=pl.BlockSpec((1, tt, D), lambda b, t: (b, t, 0)),
        out_shape=_sds((B, T, D), BF16),
        compiler_params=_cp(("parallel", "parallel")), name=name,
    )(x, gain, sc, sh)


def ln_mod_bwd(x, gain, sc, dh, dres, name):
    B, T, D = x.shape
    tt = _pick(T, 512)

    def body(x_ref, g_ref, sc_ref, dh_ref, dres_ref, dx_ref, dg_ref, dsc_ref, dsh_ref):
        b, t = pl.program_id(0), pl.program_id(1)
        xv, dhv = x_ref[0], dh_ref[0]
        g, s = g_ref[...], sc_ref[0]
        r = lax.rsqrt(jnp.mean(xv * xv, axis=-1, keepdims=True) + EPS)
        xn = xv * r
        dxn = dhv * (g * (1.0 + s))
        dx_ref[0] = dres_ref[0] + r * (dxn - xn * jnp.mean(dxn * xn, axis=-1, keepdims=True))
        s1 = jnp.sum(dhv * xn, axis=0, keepdims=True)
        s2 = jnp.sum(dhv, axis=0, keepdims=True)

        @pl.when(t == 0)
        def _():
            dsc_ref[0] = jnp.zeros_like(s1)
            dsh_ref[0] = jnp.zeros_like(s1)

        @pl.when((t == 0) & (b == 0))
        def _():
            dg_ref[...] = jnp.zeros_like(s1)

        dsc_ref[0] += s1 * g
        dsh_ref[0] += s2
        dg_ref[...] += s1 * (1.0 + s)

    tile = pl.BlockSpec((1, tt, D), lambda b, t: (b, t, 0))
    row = pl.BlockSpec((1, D), lambda b, t: (0, 0))
    brow = pl.BlockSpec((1, 1, D), lambda b, t: (b, 0, 0))
    return pl.pallas_call(
        body, grid=(B, T // tt),
        in_specs=[tile, row, brow, tile, tile],
        out_specs=[tile, row, brow, brow],
        out_shape=[_sds((B, T, D), F32), _sds((1, D), F32), _sds((B, 1, D), F32), _sds((B, 1, D), F32)],
        compiler_params=_cp(("arbitrary", "arbitrary")), name=name,
    )(x, gain, sc, dh, dres)


def rowsum_prod(a, b, name):
    B, T, D = a.shape
    tt = _pick(T, 512)

    def body(a_ref, b_ref, o_ref):
        @pl.when(pl.program_id(1) == 0)
        def _():
            o_ref[...] = jnp.zeros_like(o_ref)

        o_ref[0] += jnp.sum(a_ref[0] * b_ref[0], axis=0, keepdims=True)

    tile = pl.BlockSpec((1, tt, D), lambda b, t: (b, t, 0))
    return pl.pallas_call(
        body, grid=(B, T // tt), in_specs=[tile, tile],
        out_specs=pl.BlockSpec((1, 1, D), lambda b, t: (b, 0, 0)),
        out_shape=_sds((B, 1, D), F32),
        compiler_params=_cp(("parallel", "arbitrary")), name=name,
    )(a, b)


def loss_grad(y, tgt, name):
    B, T, D = y.shape
    tt = _pick(T, 512)

    def body(y_ref, t_ref, dy_ref, s_ref):
        @pl.when((pl.program_id(0) == 0) & (pl.program_id(1) == 0))
        def _():
            s_ref[...] = jnp.zeros_like(s_ref)

        e = y_ref[0] - t_ref[0]
        dy_ref[0] = e * (1.0 / D)
        s_ref[...] += jnp.sum(e * e, axis=0, keepdims=True)

    tile = pl.BlockSpec((1, tt, D), lambda b, t: (b, t, 0))
    return pl.pallas_call(
        body, grid=(B, T // tt), in_specs=[tile, tile],
        out_specs=[tile, pl.BlockSpec((1, D), lambda b, t: (0, 0))],
        out_shape=[_sds((B, T, D), F32), _sds((1, D), F32)],
        compiler_params=_cp(("arbitrary", "arbitrary")), name=name,
    )(y, tgt)


def _sb_tile(qn, kn, i, j, row_io, col_io):
    z = lax.dot_general(qn, kn, _NT, preferred_element_type=F32)
    mask = (j * SB_BLK + col_io) < (i * SB_BLK + row_io)
    sp = _softplus(z)
    lg = jnp.where(mask, -sp, 0.0)
    return z, mask, sp, lg


def _tri_sum(x, tri):
    hi, lo = _split(x)
    return (lax.dot_general(hi, tri, _NN, preferred_element_type=F32)
            + lax.dot_general(lo, tri, _NN, preferred_element_type=F32))


def sb_attn_fwd(proj3, gq, gk, name):
    B, T, _ = proj3.shape
    NB = T // SB_BLK
    scale = SB_HD ** -0.5

    def body(q_ref, k_ref, v_ref, gq_ref, gk_ref, o_ref, tot_ref, qn_s, kn_s, v_s):
        row_io = lax.broadcasted_iota(jnp.int32, (SB_BLK, SB_BLK), 0)
        col_io = lax.broadcasted_iota(jnp.int32, (SB_BLK, SB_BLK), 1)
        tri = (row_io > col_io).astype(BF16)

        def prep(i, _):
            rows = pl.ds(pl.multiple_of(i * SB_BLK, SB_BLK), SB_BLK)
            for hh in range(2):
                sl = slice(hh * SB_HD, (hh + 1) * SB_HD)
                q = q_ref[0, rows, sl]
                k = k_ref[0, rows, sl]
                qn_s[hh, rows, :] = (q * lax.rsqrt(jnp.mean(q * q, -1, keepdims=True) + EPS) * (gq_ref[...] * scale)).astype(BF16)
                kn_s[hh, rows, :] = (k * lax.rsqrt(jnp.mean(k * k, -1, keepdims=True) + EPS) * gk_ref[...]).astype(BF16)
                v_s[hh, rows, :] = v_ref[0, rows, sl].astype(BF16)
            return 0

        lax.fori_loop(0, NB, prep, 0)

        def qblock(i, _):
            rows = pl.ds(pl.multiple_of(i * SB_BLK, SB_BLK), SB_BLK)
            for hh in range(2):
                qn = qn_s[hh, rows, :]

                def kblock(jj, carry):
                    acc, cs = carry
                    j = i - jj
                    cols = pl.ds(pl.multiple_of(j * SB_BLK, SB_BLK), SB_BLK)
                    z, mask, sp, lg = _sb_tile(qn, kn_s[hh, cols, :], i, j, row_io, col_io)
                    p = (z - sp) + _tri_sum(lg, tri) + cs
                    att = jnp.where(mask, jnp.exp(p), 0.0)
                    acc = acc + lax.dot_general(att.astype(BF16), v_s[hh, cols, :], _NN, preferred_element_type=F32)
                    return acc, cs + jnp.sum(lg, axis=1, keepdims=True)

                acc, cs = lax.fori_loop(0, i + 1, kblock, (jnp.zeros((SB_BLK, SB_HD), F32), jnp.zeros((SB_BLK, 1), F32)))
                o_ref[0, rows, hh * SB_HD:(hh + 1) * SB_HD] = acc.astype(o_ref.dtype)
                tot_ref[0, hh, rows, :] = cs
            return 0

        lax.fori_loop(0, NB, qblock, 0)

    blk = lambda off: pl.BlockSpec((1, T, 128), lambda b, p: (b, 0, off + p))
    grow = pl.BlockSpec((1, SB_HD), lambda b, p: (0, 0))
    return pl.pallas_call(
        body, grid=(B, SB_W // 128),
        in_specs=[blk(0), blk(SB_W // 128), blk(2 * SB_W // 128), grow, grow],
        out_specs=[pl.BlockSpec((1, T, 128), lambda b, p: (b, 0, p)), pl.BlockSpec((1, 2, T, 1), lambda b, p: (b, p, 0, 0))],
        out_shape=[_sds((B, T, SB_W), BF16), _sds((B, SB_HEADS, T, 1), F32)],
        scratch_shapes=[pltpu.VMEM((2, T, SB_HD), BF16)] * 3,
        compiler_params=_cp(("parallel", "parallel")), name=name,
    )(proj3, proj3, proj3, gq, gk)


def sb_attn_bwd(proj3, gq, gk, tot, dmix3, name):
    B, T, _ = proj3.shape
    NB = T // SB_BLK
    scale = SB_HD ** -0.5

    def body(q_ref, k_ref, v_ref, gq_ref, gk_ref, tot_ref, do_ref, dq_ref, dk_ref, dv_ref, dgq_ref, dgk_ref,
             qn_s, kn_s, v_s, do_s, dqn_s, dkn_s, dv_s):
        row_io = lax.broadcasted_iota(jnp.int32, (SB_BLK, SB_BLK), 0)
        col_io = lax.broadcasted_iota(jnp.int32, (SB_BLK, SB_BLK), 1)
        tri = (row_io > col_io).astype(BF16)
        trip = (row_io < col_io).astype(BF16)

        @pl.when((pl.program_id(0) == 0) & (pl.program_id(1) == 0))
        def _():
            dgq_ref[...] = jnp.zeros_like(dgq_ref)
            dgk_ref[...] = jnp.zeros_like(dgk_ref)

        def prep(i, _):
            rows = pl.ds(pl.multiple_of(i * SB_BLK, SB_BLK), SB_BLK)
            for hh in range(2):
                sl = slice(hh * SB_HD, (hh + 1) * SB_HD)
                q = q_ref[0, rows, sl]
                k = k_ref[0, rows, sl]
                qn_s[hh, rows, :] = (q * lax.rsqrt(jnp.mean(q * q, -1, keepdims=True) + EPS) * (gq_ref[...] * scale)).astype(BF16)
                kn_s[hh, rows, :] = (k * lax.rsqrt(jnp.mean(k * k, -1, keepdims=True) + EPS) * gk_ref[...]).astype(BF16)
                v_s[hh, rows, :] = v_ref[0, rows, sl].astype(BF16)
                do_s[hh, rows, :] = do_ref[0, rows, sl].astype(BF16)
            return 0

        lax.fori_loop(0, NB, prep, 0)
        dkn_s[...] = jnp.zeros_like(dkn_s)
        dv_s[...] = jnp.zeros_like(dv_s)

        def qblock(i, _):
            rows = pl.ds(pl.multiple_of(i * SB_BLK, SB_BLK), SB_BLK)
            for hh in range(2):
                qn = qn_s[hh, rows, :]
                dov = do_s[hh, rows, :]
                tot = tot_ref[0, hh, rows, :]

                def kblock(j, carry):
                    dq, cum, cdp = carry
                    cols = pl.ds(pl.multiple_of(j * SB_BLK, SB_BLK), SB_BLK)
                    kn = kn_s[hh, cols, :]
                    vv = v_s[hh, cols, :]
                    z, mask, sp, lg = _sb_tile(qn, kn, i, j, row_io, col_io)
                    cum = cum + jnp.sum(lg, axis=1, keepdims=True)
                    p = (z - sp) + _tri_sum(lg, tri) + (tot - cum)
                    att = jnp.where(mask, jnp.exp(p), 0.0)
                    datt = lax.dot_general(dov, vv, _NT, preferred_element_type=F32)
                    dp = att * datt
                    dlg = _tri_sum(dp, trip) + cdp
                    sig = _sigmoid(z)
                    dz = jnp.where(mask, dp * (1.0 - sig) - dlg * sig, 0.0).astype(BF16)
                    dq = dq + lax.dot_general(dz, kn, _NN, preferred_element_type=F32)
                    dkn_s[hh, cols, :] += lax.dot_general(dz, qn, _TN, preferred_element_type=F32)
                    dv_s[hh, cols, :] += lax.dot_general(att.astype(BF16), dov, _TN, preferred_element_type=F32)
                    return dq, cum, cdp + jnp.sum(dp, axis=1, keepdims=True)

                z1 = jnp.zeros((SB_BLK, 1), F32)
                dq, _, _ = lax.fori_loop(0, i + 1, kblock, (jnp.zeros((SB_BLK, SB_HD), F32), z1, z1))
                dqn_s[hh, rows, :] = dq
            return 0

        lax.fori_loop(0, NB, qblock, 0)

        def fin(i, carry):
            aq, ak = carry
            rows = pl.ds(pl.multiple_of(i * SB_BLK, SB_BLK), SB_BLK)
            for hh in range(2):
                sl = slice(hh * SB_HD, (hh + 1) * SB_HD)
                for src_ref, d_s, g_ref, out_ref, mult, which in ((q_ref, dqn_s, gq_ref, dq_ref, scale, 0), (k_ref, dkn_s, gk_ref, dk_ref, 1.0, 1)):
                    xr = src_ref[0, rows, sl]
                    r = lax.rsqrt(jnp.mean(xr * xr, -1, keepdims=True) + EPS)
                    dy = d_s[hh, rows, :] * mult
                    u = dy * g_ref[...]
                    out_ref[0, rows, sl] = r * u - xr * (r * r * r) * jnp.mean(u * xr, -1, keepdims=True)
                    part = jnp.sum(dy * xr * r, axis=0, keepdims=True)
                    if which == 0:
                        aq = aq + part
                    else:
                        ak = ak + part
                dv_ref[0, rows, sl] = dv_s[hh, rows, :]
            return aq, ak

        z64 = jnp.zeros((1, SB_HD), F32)
        aq, ak = lax.fori_loop(0, NB, fin, (z64, z64))
        dgq_ref[...] += aq
        dgk_ref[...] += ak

    blk = lambda off: pl.BlockSpec((1, T, 128), lambda b, p: (b, 0, off + p))
    grow = pl.BlockSpec((1, SB_HD), lambda b, p: (0, 0))
    return pl.pallas_call(
        body, grid=(B, SB_W // 128),
        in_specs=[blk(0), blk(SB_W // 128), blk(2 * SB_W // 128), grow, grow,
                  pl.BlockSpec((1, 2, T, 1), lambda b, p: (b, p, 0, 0)), blk(0)],
        out_specs=[blk(0), blk(0), blk(0), grow, grow],
        out_shape=[_sds((B, T, SB_W), F32)] * 3 + [_sds((1, SB_HD), F32)] * 2,
        scratch_shapes=[pltpu.VMEM((2, T, SB_HD), BF16)] * 4 + [pltpu.VMEM((2, T, SB_HD), F32)] * 3,
        compiler_params=_cp(("arbitrary", "arbitrary")), name=name,
    )(proj3, proj3, proj3, gq, gk, tot, dmix3)


def _conv_silu(x, w, T):
    t_io = lax.broadcasted_iota(jnp.int32, x.shape, 0)
    xs = [x] + [jnp.where(t_io >= s, pltpu.roll(x, s, 0), 0.0) for s in range(1, CONV_K)]
    y = xs[0] * w[CONV_K - 1:CONV_K, :]
    for s in range(1, CONV_K):
        y = y + xs[s] * w[CONV_K - 1 - s:CONV_K - s, :]
    return y, y * _sigmoid(y), xs


def gdn_pre_fwd(proj3, conv_w, name):
    B, T, _ = proj3.shape
    qs = DN_HD ** -0.5

    def body(x_ref, w_ref, o_ref):
        kind = pl.program_id(1) // DN_HEADS
        _, s, _ = _conv_silu(x_ref[0], w_ref[...], T)
        n = lax.rsqrt(jnp.sum(s * s, axis=-1, keepdims=True) + EPS)
        c = jnp.where(kind == 0, qs, 1.0)
        o_ref[0, 0] = jnp.where(kind < 2, s * (n * c), s)

    return pl.pallas_call(
        body, grid=(B, 3 * DN_HEADS),
        in_specs=[pl.BlockSpec((1, T, 128), lambda b, j: (b, 0, COL_DNQKV + j)), pl.BlockSpec((CONV_K, 128), lambda b, j: (0, j))],
        out_specs=pl.BlockSpec((1, 1, T, 128), lambda b, j: (b, j // DN_HEADS, 0, j % DN_HEADS)),
        out_shape=_sds((B, 3, T, DN_W), F32),
        compiler_params=_cp(("parallel", "parallel")), name=name,
    )(proj3, conv_w)


def gdn_pre_bwd(proj3, conv_w, dqkv, name):
    B, T, _ = proj3.shape
    qs = DN_HD ** -0.5

    def body(x_ref, w_ref, d_ref, dx_ref, dw_ref):
        kind = pl.program_id(1) // DN_HEADS
        w = w_ref[...]
        y, s, xs = _conv_silu(x_ref[0], w, T)
        dout = d_ref[0, 0]
        n = lax.rsqrt(jnp.sum(s * s, axis=-1, keepdims=True) + EPS)
        c = jnp.where(kind == 0, qs, 1.0)
        dsn = c * (n * dout - s * (n * n * n) * jnp.sum(dout * s, axis=-1, keepdims=True))
        ds = jnp.where(kind < 2, dsn, dout)
        sg = _sigmoid(y)
        dy = ds * (sg * (1.0 + y * (1.0 - sg)))
        t_io = lax.broadcasted_iota(jnp.int32, dy.shape, 0)
        dx = dy * w[CONV_K - 1:CONV_K, :]
        dw_ref[0, CONV_K - 1:CONV_K, :] = jnp.sum(dy * xs[0], axis=0, keepdims=True)
        for sft in range(1, CONV_K):
            dx = dx + jnp.where(t_io < T - sft, pltpu.roll(dy, T - sft, 0), 0.0) * w[CONV_K - 1 - sft:CONV_K - sft, :]
            dw_ref[0, CONV_K - 1 - sft:CONV_K - sft, :] = jnp.sum(dy * xs[sft], axis=0, keepdims=True)
        dx_ref[0] = dx

    return pl.pallas_call(
        body, grid=(B, 3 * DN_HEADS),
        in_specs=[pl.BlockSpec((1, T, 128), lambda b, j: (b, 0, COL_DNQKV + j)), pl.BlockSpec((CONV_K, 128), lambda b, j: (0, j)),
                  pl.BlockSpec((1, 1, T, 128), lambda b, j: (b, j // DN_HEADS, 0, j % DN_HEADS))],
        out_specs=[pl.BlockSpec((1, T, 128), lambda b, j: (b, 0, j)), pl.BlockSpec((1, CONV_K, 128), lambda b, j: (b, 0, j))],
        out_shape=[_sds((B, T, 3 * DN_W), F32), _sds((B, CONV_K, 3 * DN_W), F32)],
        compiler_params=_cp(("parallel", "parallel")), name=name,
    )(proj3, conv_w, dqkv)


def gdn_gates_fwd(proj3, alog_row, dtb_row, name):
    B, T, _ = proj3.shape

    def body(x_ref, al_ref, dt_ref, o_ref):
        x = x_ref[0]
        lane = lax.broadcasted_iota(jnp.int32, x.shape, 1)
        g = -jnp.exp(al_ref[...]) * _softplus(x + dt_ref[...])
        o_ref[0] = jnp.where(lane < DN_HEADS, g, jnp.where(lane < 2 * DN_HEADS, _sigmoid(x), 0.0))

    row = pl.BlockSpec((1, 128), lambda b: (0, 0))
    return pl.pallas_call(
        body, grid=(B,),
        in_specs=[pl.BlockSpec((1, T, 128), lambda b: (b, 0, COL_AB)), row, row],
        out_specs=pl.BlockSpec((1, T, 128), lambda b: (b, 0, 0)),
        out_shape=_sds((B, T, 128), F32),
        compiler_params=_cp(("parallel",)), name=name,
    )(proj3, alog_row, dtb_row)


def gdn_gates_bwd(proj3, alog_row, dtb_row, dgates, name):
    B, T, _ = proj3.shape

    def body(x_ref, al_ref, dt_ref, d_ref, dx_ref, dal_ref, ddt_ref):
        @pl.when(pl.program_id(0) == 0)
        def _():
            dal_ref[...] = jnp.zeros_like(dal_ref)
            ddt_ref[...] = jnp.zeros_like(ddt_ref)

        x, d = x_ref[0], d_ref[0]
        lane = lax.broadcasted_iota(jnp.int32, x.shape, 1)
        a = x + dt_ref[...]
        na = -jnp.exp(al_ref[...])
        da = jnp.where(lane < DN_HEADS, d * na * _sigmoid(a), 0.0)
        bt = _sigmoid(x)
        dx_ref[0] = da + jnp.where((lane >= DN_HEADS) & (lane < 2 * DN_HEADS), d * bt * (1.0 - bt), 0.0)
        dal_ref[...] += jnp.sum(jnp.where(lane < DN_HEADS, d * na * _softplus(a), 0.0), axis=0, keepdims=True)
        ddt_ref[...] += jnp.sum(da, axis=0, keepdims=True)

    row = pl.BlockSpec((1, 128), lambda b: (0, 0))
    tile = pl.BlockSpec((1, T, 128), lambda b: (b, 0, 0))
    return pl.pallas_call(
        body, grid=(B,),
        in_specs=[pl.BlockSpec((1, T, 128), lambda b: (b, 0, COL_AB)), row, row, tile],
        out_specs=[tile, row, row],
        out_shape=[_sds((B, T, 128), F32), _sds((1, 128), F32), _sds((1, 128), F32)],
        compiler_params=_cp(("arbitrary",)), name=name,
    )(proj3, alog_row, dtb_row, dgates)


def gdn_post_fwd(ob, proj3, gain, name):
    B, T, _ = ob.shape

    def body(o_ref, z_ref, g_ref, out_ref):
        o, z = o_ref[0], z_ref[0]
        r = lax.rsqrt(jnp.mean(o * o, axis=-1, keepdims=True) + EPS)
        out_ref[0] = ((o * r * g_ref[...]) * (z * _sigmoid(z))).astype(out_ref.dtype)

    tile = pl.BlockSpec((1, T, 128), lambda b, h: (b, 0, h))
    return pl.pallas_call(
        body, grid=(B, DN_HEADS),
        in_specs=[tile, pl.BlockSpec((1, T, 128), lambda b, h: (b, 0, COL_Z + h)), pl.BlockSpec((1, 128), lambda b, h: (0, 0))],
        out_specs=tile, out_shape=_sds((B, T, DN_W), BF16),
        compiler_params=_cp(("parallel", "parallel")), name=name,
    )(ob, proj3, gain)


def gdn_post_bwd(ob, proj3, gain, dmix3, name):
    B, T, _ = ob.shape

    def body(o_ref, z_ref, g_ref, d_ref, do_ref, dz_ref, dg_ref):
        @pl.when((pl.program_id(0) == 0) & (pl.program_id(1) == 0))
        def _():
            dg_ref[...] = jnp.zeros_like(dg_ref)

        o, z, d, g = o_ref[0], z_ref[0], d_ref[0], g_ref[...]
        r = lax.rsqrt(jnp.mean(o * o, axis=-1, keepdims=True) + EPS)
        sg = _sigmoid(z)
        dn = d * (z * sg)
        dz_ref[0] = d * (o * r * g) * (sg * (1.0 + z * (1.0 - sg)))
        dg_ref[...] += jnp.sum(dn * o * r, axis=0, keepdims=True)
        u = dn * g
        do_ref[0] = r * u - o * (r * r * r) * jnp.mean(u * o, axis=-1, keepdims=True)

    tile = pl.BlockSpec((1, T, 128), lambda b, h: (b, 0, h))
    row = pl.BlockSpec((1, 128), lambda b, h: (0, 0))
    return pl.pallas_call(
        body, grid=(B, DN_HEADS),
        in_specs=[tile, pl.BlockSpec((1, T, 128), lambda b, h: (b, 0, COL_Z + h)), row,
                  pl.BlockSpec((1, T, 128), lambda b, h: (b, 0, SB_W // 128 + h))],
        out_specs=[tile, tile, row],
        out_shape=[_sds((B, T, DN_W), F32), _sds((B, T, DN_W), F32), _sds((1, 128), F32)],
        compiler_params=_cp(("arbitrary", "arbitrary")), name=name,
    )(ob, proj3, gain, dmix3)


def _tri_inv(low, ri, ci):
    m = (ri == ci).astype(F32)
    s = 1
    while s < DN_C:
        sh = s.bit_length()
        off = ((ri >> sh) == (ci >> sh)) & ((ri & (2 * s - 1)) >= s) & ((ci & (2 * s - 1)) < s)
        m = m - _pdot(m, _pdot(jnp.where(off, low, 0.0), m))
        s *= 2
    return m


def _chunk_common(q, k, v, gt, h):
    C = DN_C
    ri = lax.broadcasted_iota(jnp.int32, (C, C), 0)
    ci = lax.broadcasted_iota(jnp.int32, (C, C), 1)
    lane = lax.broadcasted_iota(jnp.int32, (C, 128), 1)
    incl, strict = ri >= ci, ri > ci
    g = jnp.sum(jnp.where(lane == h, gt, 0.0), axis=1, keepdims=True)
    beta = jnp.sum(jnp.where(lane == h + DN_HEADS, gt, 0.0), axis=1, keepdims=True)
    ones = jnp.ones((C, 128), F32)
    gc = _pdot(incl.astype(F32), g * ones)[:, :1]
    gcr = _pdot(jnp.ones((C, C), F32), jnp.where(ri == ci, gc, 0.0))
    decay = jnp.where(incl, jnp.exp(jnp.where(incl, gc - gcr, 0.0)), 0.0)
    e = jnp.exp(gc)
    kb, vb = k * beta, v * beta
    kk = _bdot(kb, k, _NT)
    low = jnp.where(strict, kk * decay, 0.0)
    tm = _tri_inv(low, ri, ci)
    kbe = kb * e
    u = _bdot(tm, vb)
    w = _bdot(tm, kbe)
    qk = _bdot(q, k, _NT)
    intra = jnp.where(incl, qk * decay, 0.0)
    gl = gc[C - 1:C, :]
    el = jnp.exp(gl)
    r = jnp.exp(gl - gc)
    return dict(ri=ri, ci=ci, lane=lane, incl=incl, strict=strict, g=g, beta=beta, gc=gc, decay=decay, e=e,
                kb=kb, vb=vb, kk=kk, tm=tm, kbe=kbe, u=u, w=w, qk=qk, intra=intra, el=el, r=r, ones=ones)


def gdn_chunk_fwd(qkv, gates, name):
    B, _, T, _ = qkv.shape
    NC = T // DN_C

    def body(x_ref, gt_ref, o_ref, st_ref, s_s):
        h = pl.program_id(1)
        s_s[...] = jnp.zeros_like(s_s)

        def chunk(n, _):
            rows = pl.ds(pl.multiple_of(n * DN_C, DN_C), DN_C)
            q, k, v = x_ref[0, 0, rows, :], x_ref[0, 1, rows, :], x_ref[0, 2, rows, :]
            c = _chunk_common(q, k, v, gt_ref[0, rows, :], h)
            st = s_s[...]
            st_ref[0, 0, pl.ds(pl.multiple_of(n * DN_HD, DN_HD), DN_HD), :] = st
            v_new = c["u"] - _bdot(c["w"], st)
            o_ref[0, rows, :] = _bdot(q * c["e"], st) + _bdot(c["intra"], v_new)
            s_s[...] = st * c["el"] + _bdot(k * c["r"], v_new, _TN)
            return 0

        lax.fori_loop(0, NC, chunk, 0)

    return pl.pallas_call(
        body, grid=(B, DN_HEADS),
        in_specs=[pl.BlockSpec((1, 3, T, 128), lambda b, h: (b, 0, 0, h)), pl.BlockSpec((1, T, 128), lambda b, h: (b, 0, 0))],
        out_specs=[pl.BlockSpec((1, T, 128), lambda b, h: (b, 0, h)), pl.BlockSpec((1, 1, NC * DN_HD, DN_HD), lambda b, h: (b, h, 0, 0))],
        out_shape=[_sds((B, T, DN_W), F32), _sds((B, DN_HEADS, NC * DN_HD, DN_HD), F32)],
        scratch_shapes=[pltpu.VMEM((DN_HD, DN_HD), F32)],
        compiler_params=_cp(("parallel", "parallel")), name=name,
    )(qkv, gates)


def gdn_chunk_bwd(qkv, gates, states, dob, name):
    B, _, T, _ = qkv.shape
    NC = T // DN_C
    C = DN_C

    def body(x_ref, gt_ref, st_ref, do_ref, dx_ref, dgt_ref, ds_s):
        h = pl.program_id(1)
        ds_s[...] = jnp.zeros_like(ds_s)

        @pl.when(h == 0)
        def _():
            dgt_ref[...] = jnp.zeros_like(dgt_ref)

        def chunk(m, _):
            n = NC - 1 - m
            rows = pl.ds(pl.multiple_of(n * C, C), C)
            q, k, v = x_ref[0, 0, rows, :], x_ref[0, 1, rows, :], x_ref[0, 2, rows, :]
            c = _chunk_common(q, k, v, gt_ref[0, rows, :], h)
            incl, strict, decay, e, r, el, tm = c["incl"], c["strict"], c["decay"], c["e"], c["r"], c["el"], c["tm"]
            st = st_ref[0, 0, pl.ds(pl.multiple_of(n * DN_HD, DN_HD), DN_HD), :]
            do = do_ref[0, rows, :]
            dsn = ds_s[...]
            v_new = c["u"] - _bdot(c["w"], st)
            kr, qe = k * r, q * e
            ds = el * dsn
            del_ = jnp.sum(jnp.sum(dsn * st, axis=1, keepdims=True), axis=0, keepdims=True)
            dkr = _bdot(v_new, dsn, _NT)
            dvn = _bdot(kr, dsn)
            dqe = _bdot(do, st, _NT)
            ds = ds + _bdot(qe, do, _TN)
            dintra = _bdot(do, v_new, _NT)
            dvn = dvn + _bdot(c["intra"], do, _TN)
            dw = -_bdot(dvn, st, _NT)
            ds = ds - _bdot(c["w"], dvn, _TN)
            dqkd = jnp.where(incl, dintra, 0.0)
            dqk = dqkd * decay
            ddecay = dqkd * c["qk"]
            dq = dqe * e + _bdot(dqk, k)
            dk = dkr * r + _bdot(dqk, q, _TN)
            dtm = _bdot(dvn, c["vb"], _NT) + _bdot(dw, c["kbe"], _NT)
            dvb = _bdot(tm, dvn, _TN)
            dkbe = _bdot(tm, dw, _TN)
            dkb = dkbe * e
            de = jnp.sum(dqe * q, axis=1, keepdims=True) + jnp.sum(dkbe * c["kb"], axis=1, keepdims=True)
            da = -_pdot(tm, _pdot(dtm, tm, _NT), _TN)
            dlow = jnp.where(strict, da, 0.0)
            dkk = dlow * decay
            ddecay = ddecay + dlow * c["kk"]
            dkb = dkb + _bdot(dkk, k)
            dk = dk + _bdot(dkk, c["kb"], _TN) + dkb * c["beta"]
            dbeta = jnp.sum(dkb * k, axis=1, keepdims=True) + jnp.sum(dvb * v, axis=1, keepdims=True)
            dv = dvb * c["beta"]
            dd = ddecay * decay
            dgc = jnp.sum(dd, axis=1, keepdims=True) - _pdot(dd, c["ones"], _TN)[:, :1]
            dr = jnp.sum(dkr * k, axis=1, keepdims=True)
            dgc = dgc + de * e - dr * r
            dgl = jnp.sum(dr * r, axis=0, keepdims=True) + del_ * el
            rowc = lax.broadcasted_iota(jnp.int32, (C, 1), 0)
            dgc = dgc + jnp.where(rowc == C - 1, dgl, 0.0)
            dg = _pdot(incl.astype(F32), dgc * c["ones"], _TN)[:, :1]
            dx_ref[0, 0, rows, :] = dq
            dx_ref[0, 1, rows, :] = dk
            dx_ref[0, 2, rows, :] = dv
            lane = c["lane"]
            dgt_ref[0, rows, :] += jnp.where(lane == h, dg, 0.0) + jnp.where(lane == h + DN_HEADS, dbeta, 0.0)
            ds_s[...] = ds
            return 0

        lax.fori_loop(0, NC, chunk, 0)

    return pl.pallas_call(
        body, grid=(B, DN_HEADS),
        in_specs=[pl.BlockSpec((1, 3, T, 128), lambda b, h: (b, 0, 0, h)), pl.BlockSpec((1, T, 128), lambda b, h: (b, 0, 0)),
                  pl.BlockSpec((1, 1, NC * DN_HD, DN_HD), lambda b, h: (b, h, 0, 0)), pl.BlockSpec((1, T, 128), lambda b, h: (b, 0, h))],
        out_specs=[pl.BlockSpec((1, 3, T, 128), lambda b, h: (b, 0, 0, h)), pl.BlockSpec((1, T, 128), lambda b, h: (b, 0, 0))],
        out_shape=[_sds((B, 3, T, DN_W), F32), _sds((B, T, 128), F32)],
        scratch_shapes=[pltpu.VMEM((DN_HD, DN_HD), F32)],
        compiler_params=_cp(("parallel", "arbitrary")), name=name,
    )(qkv, gates, states, dob)


def ada_fwd(c_all, w_ada, b_sl, name):
    L, D, W = w_ada.shape
    NBt = c_all.shape[0]

    def body(c_ref, w_ref, b_ref, o_ref):
        cv = c_ref[...]
        o_ref[0] = _pdot(cv * _sigmoid(cv), w_ref[0]) + b_ref[0]

    return pl.pallas_call(
        body, grid=(L,),
        in_specs=[pl.BlockSpec((NBt, D), lambda l: (0, 0)), pl.BlockSpec((1, D, W), lambda l: (l, 0, 0)), pl.BlockSpec((1, 1, W), lambda l: (l, 0, 0))],
        out_specs=pl.BlockSpec((1, NBt, W), lambda l: (l, 0, 0)),
        out_shape=_sds((L, NBt, W), F32),
        compiler_params=_cp(("parallel",)), name=name,
    )(c_all, w_ada, b_sl)


def ada_bwd(c_all, dmod_cols, name):
    L, NBt, W = dmod_cols.shape
    D = c_all.shape[1]

    def body(c_ref, d_ref, o_ref):
        cv = c_ref[...]
        o_ref[0] = _pdot(cv * _sigmoid(cv), d_ref[0], _TN)

    return pl.pallas_call(
        body, grid=(L,),
        in_specs=[pl.BlockSpec((NBt, D), lambda l: (0, 0)), pl.BlockSpec((1, NBt, W), lambda l: (l, 0, 0))],
        out_specs=pl.BlockSpec((1, D, W), lambda l: (l, 0, 0)),
        out_shape=_sds((L, D, W), F32),
        compiler_params=_cp(("parallel",)), name=name,
    )(c_all, dmod_cols)


def adamw(partials, w, m, v, name):
    P, R, C = partials.shape
    tr = _pick(R, 256)

    def body(p_ref, w_ref, m_ref, v_ref, g_ref, d_ref, nm_ref, nv_ref):
        g = p_ref[0].astype(F32)
        for i in range(1, P):
            g = g + p_ref[i].astype(F32)
        nm = ADAM_B1 * m_ref[...] + (1.0 - ADAM_B1) * g
        nv = ADAM_B2 * v_ref[...] + (1.0 - ADAM_B2) * (g * g)
        m_hat = nm / (1.0 - ADAM_B1 ** ADAM_STEP)
        v_hat = nv / (1.0 - ADAM_B2 ** ADAM_STEP)
        g_ref[...] = g
        d_ref[...] = -ADAM_LR * (m_hat / (jnp.sqrt(v_hat) + ADAM_EPS) + ADAM_WD * w_ref[...])
        nm_ref[...] = nm
        nv_ref[...] = nv

    tile = pl.BlockSpec((tr, C), lambda i: (i, 0))
    return pl.pallas_call(
        body, grid=(R // tr,),
        in_specs=[pl.BlockSpec((P, tr, C), lambda i: (0, i, 0)), tile, tile, tile],
        out_specs=[tile] * 4, out_shape=[_sds((R, C), F32)] * 4,
        compiler_params=_cp(("parallel",)), name=name,
    )(partials, w, m, v)


def _coords():
    return lax.axis_index("x"), lax.axis_index("y"), lax.axis_index("c")


def all_gather(x, name):
    any_spec = pl.BlockSpec(memory_space=pl.ANY)

    def body(x_ref, out_ref, send_sems, recv_sems, local_sem):
        x_, y_, c_ = _coords()
        me, sibling = (x_, y_, c_), (x_, y_, 1 - c_)
        chips = [(1 - x_, y_), (x_, 1 - y_), (1 - x_, 1 - y_)]

        def slot(px, py, pc):
            return out_ref.at[4 * px + 2 * py + pc]

        def copy(k, block, to, src=None):
            return pltpu.make_async_remote_copy(
                src_ref=slot(*block) if src is None else src, dst_ref=slot(*block),
                send_sem=send_sems.at[k], recv_sem=recv_sems.at[k],
                device_id=to, device_id_type=pl.DeviceIdType.MESH)

        mine = pltpu.make_async_copy(x_ref, slot(*me), local_sem)
        mine.start()
        first = [copy(0, me, sibling, src=x_ref)]
        first += [copy(1 + j, me, (*chip, c_), src=x_ref) for j, chip in enumerate(chips)]
        for cp in first:
            cp.start()
        passed = [copy(4 + j, (*chip, c_), sibling) for j, chip in enumerate(chips)]
        for j, chip in enumerate(chips):
            copy(1 + j, (*chip, c_), me).wait_recv()
            passed[j].start()
        copy(0, sibling, me).wait_recv()
        for j, chip in enumerate(chips):
            copy(4 + j, (*chip, 1 - c_), me).wait_recv()
        for cp in first + passed:
            cp.wait_send()
        mine.wait()

    return pl.pallas_call(
        body, out_shape=_sds((N_DEV,) + x.shape, x.dtype),
        in_specs=[any_spec], out_specs=any_spec,
        scratch_shapes=[pltpu.SemaphoreType.DMA((7,)), pltpu.SemaphoreType.DMA((7,)), pltpu.SemaphoreType.DMA],
        name=name,
    )(x)


def all_to_all(x, name):
    any_spec = pl.BlockSpec(memory_space=pl.ANY)

    def body(x_ref, out_ref, send_sems, recv_sems, local_sem):
        x_, y_, c_ = _coords()
        me = 4 * x_ + 2 * y_ + c_
        mine = pltpu.make_async_copy(x_ref.at[me], out_ref.at[me], local_sem)
        mine.start()
        copies = []
        for k in range(1, N_DEV):
            px = 1 - x_ if k & 4 else x_
            py = 1 - y_ if k & 2 else y_
            pc = 1 - c_ if k & 1 else c_
            peer = 4 * px + 2 * py + pc
            copies.append((pltpu.make_async_remote_copy(
                src_ref=x_ref.at[peer], dst_ref=out_ref.at[me],
                send_sem=send_sems.at[k - 1], recv_sem=recv_sems.at[k - 1],
                device_id=(px, py, pc), device_id_type=pl.DeviceIdType.MESH), peer))
        for cp, _ in copies:
            cp.start()
        for k, (cp, peer) in enumerate(copies):
            pltpu.make_async_remote_copy(
                src_ref=x_ref.at[peer], dst_ref=out_ref.at[peer],
                send_sem=send_sems.at[k], recv_sem=recv_sems.at[k],
                device_id=(x_, y_, c_), device_id_type=pl.DeviceIdType.MESH).wait_recv()
        for cp, _ in copies:
            cp.wait_send()
        mine.wait()

    return pl.pallas_call(
        body, out_shape=_sds(x.shape, x.dtype),
        in_specs=[any_spec], out_specs=any_spec,
        scratch_shapes=[pltpu.SemaphoreType.DMA((7,)), pltpu.SemaphoreType.DMA((7,)), pltpu.SemaphoreType.DMA],
        name=name,
    )(x)


def _rows128(a):
    return a.reshape(-1, 128)


def _pad_lanes(a):
    return jnp.pad(a, ((0, 0), (0, 128 - a.shape[1])))


def kernel(x, c, w_ada, b_ada, norm_mix, norm_mlp, w_in, sb_q_norm, sb_k_norm, conv_w, a_log, dt_bias, dn_out_norm, w_out, w_ff1, w_ff2, loss_target, m_w_ada, m_b_ada, m_norm_mix, m_norm_mlp, m_w_in, m_sb_q_norm, m_sb_k_norm, m_conv_w, m_a_log, m_dt_bias, m_dn_out_norm, m_w_out, m_w_ff1, m_w_ff2, v_w_ada, v_b_ada, v_norm_mix, v_norm_mlp, v_w_in, v_sb_q_norm, v_sb_k_norm, v_conv_w, v_a_log, v_dt_bias, v_dn_out_norm, v_w_out, v_w_ff1, v_w_ff2):
    B, T, D = x.shape
    L = w_ada.shape[0]
    N = B * T
    FF = w_ff1.shape[2] * N_DEV
    WA = w_ada.shape[2]
    CS = conv_w.shape[2]
    me = 4 * lax.axis_index("x") + 2 * lax.axis_index("y") + lax.axis_index("c")
    tm = _pick(T, 512)

    win_g = all_gather(w_in.astype(BF16), "comm_gather_w_in")
    W_in = jnp.pad(win_g.transpose(1, 2, 0, 3).reshape(L, D, IN_W), ((0, 0), (0, 0), (0, IN_WP - IN_W)))
    W_out = all_gather(w_out.astype(BF16), "comm_gather_w_out").transpose(1, 0, 2, 3).reshape(L, SB_W + DN_W, D)
    W_1 = all_gather(w_ff1.astype(BF16), "comm_gather_w_ff1").transpose(1, 2, 0, 3).reshape(L, D, FF)
    W_2 = all_gather(w_ff2.astype(BF16), "comm_gather_w_ff2").transpose(1, 0, 2, 3).reshape(L, FF, D)
    conv_full = all_gather(conv_w, "comm_gather_conv").transpose(1, 2, 0, 3).reshape(L, CONV_K, 3 * DN_W)

    c_all = all_gather(c, "comm_gather_c").reshape(N_DEV * B, D)
    b_sl = lax.dynamic_slice_in_dim(b_ada, me * WA, WA, axis=1).reshape(L, 1, WA)
    mod_sh = ada_fwd(c_all, w_ada, b_sl, "ada_fwd")
    mod_g = all_gather(mod_sh, "comm_gather_mod")
    mod = lax.dynamic_slice_in_dim(mod_g, me * B, B, axis=2).transpose(1, 2, 0, 3).reshape(L, B, 6 * D)

    def mod_part(l, i):
        return mod[l, :, i * D:(i + 1) * D].reshape(B, 1, D)

    alog_row = _pad_lanes(a_log).reshape(L, 1, 128)
    dtb_row = _pad_lanes(dt_bias).reshape(L, 1, 128)

    def gate_epi(acc, xv, g):
        return acc, xv + g[0] * acc

    def relu2(a):
        r = jnp.maximum(a, 0.0)
        return r * r

    def times_gate(a, g):
        return a * g[0]

    tile_ij = lambda i, j, k: (i, j)

    saved = []
    xc = x
    for l in range(L):
        sh_a, sc_a, g_a, sh_m, sc_m, g_m = [mod_part(l, i) for i in range(6)]
        h = ln_mod_fwd(xc, norm_mix[l:l + 1], sc_a, sh_a, "ln_mod_fwd")
        proj3 = matmul(h.reshape(N, D), W_in[l], mode="nn", name="mm_proj", tm=256)[0].reshape(B, T, IN_WP)
        o_a, tot = sb_attn_fwd(proj3, sb_q_norm[l:l + 1], sb_k_norm[l:l + 1], "sb_attn_fwd")
        qkv = gdn_pre_fwd(proj3, conv_full[l], "gdn_pre_fwd")
        gates = gdn_gates_fwd(proj3, alog_row[l], dtb_row[l], "gdn_gates_fwd")
        ob, states = gdn_chunk_fwd(qkv, gates, "gdn_chunk_fwd")
        o_b = gdn_post_fwd(ob, proj3, dn_out_norm[l:l + 1], "gdn_post_fwd")
        mix = jnp.concatenate([o_a, o_b], axis=-1)
        y1, x_mid = matmul(
            mix.reshape(N, SB_W + DN_W), W_out[l], mode="nn", name="mm_out", out_dtypes=(F32, F32), tm=tm, epi=gate_epi,
            extras=[(xc.reshape(N, D), (tm, _pick(D, 512)), tile_ij),
                    (g_a, (1, 1, _pick(D, 512)), lambda i, j, k: (i * tm // T, 0, j))])
        x_mid = x_mid.reshape(B, T, D)
        h2 = ln_mod_fwd(x_mid, norm_mlp[l:l + 1], sc_m, sh_m, "ln_mod_fwd")
        u = matmul(h2.reshape(N, D), W_1[l], mode="nn", name="mm_ff1")[0]
        y2, x_out = matmul(
            u, W_2[l], mode="nn", name="mm_ff2", out_dtypes=(F32, F32), tm=tm, a_fn=relu2, epi=gate_epi,
            extras=[(x_mid.reshape(N, D), (tm, _pick(D, 512)), tile_ij),
                    (g_m, (1, 1, _pick(D, 512)), lambda i, j, k: (i * tm // T, 0, j))])
        saved.append(dict(x=xc, h=h, proj3=proj3, tot=tot, qkv=qkv, gates=gates, states=states, ob=ob, mix=mix,
                          y1=y1, x_mid=x_mid, h2=h2, u=u, y2=y2))
        xc = x_out.reshape(B, T, D)

    dx, sq = loss_grad(xc, loss_target, "loss_grad")
    loss = lax.psum((0.5 / D) * jnp.sum(sq), AXES)

    g_win, g_wout, g_w1, g_w2, dmods, smalls = [], [], [], [], [], []
    tk_tok = tm
    for l in reversed(range(L)):
        s = saved[l]
        sh_a, sc_a, g_a, sh_m, sc_m, g_m = [mod_part(l, i) for i in range(6)]
        gate_k = lambda g, blk: (g, (1, 1, blk), lambda i, j, k: (i * tm // T, 0, k))
        gate_tok = lambda g, blk: (g, (1, 1, blk), lambda i, j, k: (k * tk_tok // T, 0, j))
        dx2 = dx.reshape(N, D)
        dg_m = rowsum_prod(dx, s["y2"].reshape(B, T, D), "rowsum_prod")
        du = matmul(dx2, W_2[l], mode="nt", name="mm_ff2_da", out_dtypes=(BF16,), tm=tm, a_fn=times_gate,
                    a_extras=[gate_k(g_m, _pick(D, 512))],
                    epi=lambda acc, uv: (acc * (2.0 * jnp.maximum(uv, 0.0)),),
                    extras=[(s["u"], (tm, _pick(FF, 512)), tile_ij)])[0]
        g_w2.append(matmul(s["u"], dx2, mode="tn", name="mm_ff2_dw", out_dtypes=(BF16,), tk=tk_tok, a_fn=relu2,
                           b_fn=times_gate, b_extras=[gate_tok(g_m, _pick(D, 512))])[0])
        g_w1.append(matmul(s["h2"].reshape(N, D), du, mode="tn", name="mm_ff1_dw", out_dtypes=(BF16,))[0])
        dh2 = matmul(du, W_1[l], mode="nt", name="mm_ff1_da")[0]
        dx_mid, dgn_mlp, dsc_m, dsh_m = ln_mod_bwd(s["x_mid"], norm_mlp[l:l + 1], sc_m, dh2.reshape(B, T, D), dx, "ln_mod_bwd")
        dxm2 = dx_mid.reshape(N, D)
        dg_a = rowsum_prod(dx_mid, s["y1"].reshape(B, T, D), "rowsum_prod")
        dmix3 = matmul(dxm2, W_out[l], mode="nt", name="mm_out_da", tm=tm, a_fn=times_gate,
                       a_extras=[gate_k(g_a, _pick(D, 512))])[0].reshape(B, T, SB_W + DN_W)
        g_wout.append(matmul(s["mix"].reshape(N, SB_W + DN_W), dxm2, mode="tn", name="mm_out_dw", out_dtypes=(BF16,),
                             tk=tk_tok, b_fn=times_gate, b_extras=[gate_tok(g_a, _pick(D, 512))])[0])
        dq_a, dk_a, dv_a, dgq, dgk = sb_attn_bwd(s["proj3"], sb_q_norm[l:l + 1], sb_k_norm[l:l + 1], s["tot"], dmix3, "sb_attn_bwd")
        dob, dz, dgn_dn = gdn_post_bwd(s["ob"], s["proj3"], dn_out_norm[l:l + 1], dmix3, "gdn_post_bwd")
        dqkv, dgates = gdn_chunk_bwd(s["qkv"], s["gates"], s["states"], dob, "gdn_chunk_bwd")
        d_dnqkv, dconv_b = gdn_pre_bwd(s["proj3"], conv_full[l], dqkv, "gdn_pre_bwd")
        d_ab, dalog, ddtb = gdn_gates_bwd(s["proj3"], alog_row[l], dtb_row[l], dgates, "gdn_gates_bwd")
        dproj = jnp.concatenate([dq_a, dk_a, dv_a, d_dnqkv, dz, d_ab], axis=-1).reshape(N, IN_WP)
        g_win.append(matmul(s["h"].reshape(N, D), dproj, mode="tn", name="mm_proj_dw", out_dtypes=(BF16,))[0])
        dh = matmul(dproj, W_in[l], mode="nt", name="mm_proj_da", tk=IN_WP)[0]
        dx, dgn_mix, dsc_a, dsh_a = ln_mod_bwd(s["x"], norm_mix[l:l + 1], sc_a, dh.reshape(B, T, D), dx_mid, "ln_mod_bwd")
        dmods.append(jnp.concatenate([dsh_a, dsc_a, dg_a, dsh_m, dsc_m, dg_m], axis=-1).reshape(B, 6 * D))
        smalls.append(dict(norm_mix=dgn_mix, norm_mlp=dgn_mlp, sbq=dgq, sbk=dgk, alog=dalog, dtb=ddtb, dnorm=dgn_dn,
                           conv=jnp.sum(dconv_b, axis=0)))
    for lst in (g_win, g_wout, g_w1, g_w2, dmods, smalls):
        lst.reverse()
    grad_x = dx

    def shard_cols(g, n):
        return g.reshape(L, g.shape[1], N_DEV, n).transpose(2, 0, 1, 3)

    def shard_rows(g, n):
        return g.reshape(L, N_DEV, n, g.shape[2]).transpose(1, 0, 2, 3)

    def update(parts, w, m, v, name):
        shp = w.shape
        r2 = lambda a: a.reshape(-1, shp[-1])
        outs = adamw(parts.reshape(parts.shape[0], -1, shp[-1]), r2(w), r2(m), r2(v), name)
        return [o.reshape(shp) for o in outs]

    wi = w_in.shape[2]
    p_win = all_to_all(shard_cols(jnp.stack(g_win)[:, :, :IN_W], wi), "comm_scatter_w_in")
    p_wout = all_to_all(shard_rows(jnp.stack(g_wout), w_out.shape[1]), "comm_scatter_w_out")
    p_w1 = all_to_all(shard_cols(jnp.stack(g_w1), w_ff1.shape[2]), "comm_scatter_w_ff1")
    p_w2 = all_to_all(shard_rows(jnp.stack(g_w2), w_ff2.shape[1]), "comm_scatter_w_ff2")
    r_win = update(p_win, w_in, m_w_in, v_w_in, "adamw_w_in")
    r_wout = update(p_wout, w_out, m_w_out, v_w_out, "adamw_w_out")
    r_w1 = update(p_w1, w_ff1, m_w_ff1, v_w_ff1, "adamw_w_ff1")
    r_w2 = update(p_w2, w_ff2, m_w_ff2, v_w_ff2, "adamw_w_ff2")

    dmod_g = all_gather(jnp.stack(dmods), "comm_gather_dmod")
    dmod_all = dmod_g.transpose(1, 0, 2, 3).reshape(L, N_DEV * B, 6 * D)
    g_wada = ada_bwd(c_all, lax.dynamic_slice_in_dim(dmod_all, me * WA, WA, axis=2), "ada_bwd")
    r_wada = update(g_wada[None], w_ada, m_w_ada, v_w_ada, "adamw_w_ada")
    r_bada = update(dmod_g.transpose(0, 2, 1, 3).reshape(N_DEV * B, L, 6 * D), b_ada, m_b_ada, v_b_ada, "adamw_b_ada")

    def pack(f):
        return jnp.concatenate([
            _rows128(f("norm_mix")), _rows128(f("norm_mlp")), _rows128(f("sbq")), _rows128(f("sbk")),
            f("alog"), f("dtb"), f("dnorm"), _rows128(f("conv"))], axis=0)

    names = ["norm_mix", "norm_mlp", "sbq", "sbk", "alog", "dtb", "dnorm"]
    part = pack(lambda n: jnp.concatenate([sm[n] for sm in smalls], axis=0))
    n_rep = part.shape[0] - L * CONV_K * 3 * DN_W // 128
    part_g = all_gather(part, "comm_gather_small")
    params = dict(norm_mix=(norm_mix, m_norm_mix, v_norm_mix), norm_mlp=(norm_mlp, m_norm_mlp, v_norm_mlp),
                  sbq=(sb_q_norm, m_sb_q_norm, v_sb_q_norm), sbk=(sb_k_norm, m_sb_k_norm, v_sb_k_norm),
                  alog=(a_log, m_a_log, v_a_log), dtb=(dt_bias, m_dt_bias, v_dt_bias),
                  dnorm=(dn_out_norm, m_dn_out_norm, v_dn_out_norm))

    def rows_of(n, a):
        return _pad_lanes(a) if n in ("alog", "dtb") else _rows128(a)

    packed = [jnp.concatenate([rows_of(n, params[n][i]) for n in names], axis=0) for i in range(3)]
    r_small = adamw(part_g[:, :n_rep], packed[0], packed[1], packed[2], "adamw_small")
    small_out = {}
    off = 0
    for n in names:
        w0 = params[n][0]
        nr = rows_of(n, w0).shape[0]
        vals = [o[off:off + nr] for o in r_small]
        small_out[n] = [(vv[:, :w0.shape[1]] if n in ("alog", "dtb") else vv.reshape(w0.shape)) for vv in vals]
        off += nr
    conv_parts = part_g[:, n_rep:].reshape(N_DEV, L, CONV_K, 3 * DN_W)
    r_conv = update(lax.dynamic_slice_in_dim(conv_parts, me * CS, CS, axis=3), conv_w, m_conv_w, v_conv_w, "adamw_conv")

    order = [r_wada, r_bada, small_out["norm_mix"], small_out["norm_mlp"], r_win, small_out["sbq"], small_out["sbk"],
             r_conv, small_out["alog"], small_out["dtb"], small_out["dnorm"], r_wout, r_w1, r_w2]
    outs = [loss, grad_x]
    for i in range(4):
        outs += [r[i] for r in order]
    return tuple(outs)
```

```python
import functools
import math

import jax
import jax.numpy as jnp
from jax import lax
from jax.experimental import pallas as pl
from jax.experimental.pallas import tpu as pltpu

F32 = jnp.float32
BF16 = jnp.bfloat16
EPS = 1e-6
N_DEV = 8
AXES = ("x", "y", "c")

SB_HEADS, SB_HD = 8, 64
SB_W = SB_HEADS * SB_HD
SB_BLK = 128
DN_HEADS, DN_HD = 4, 128
DN_W = DN_HEADS * DN_HD
DN_C = 64
CONV_K = 4
IN_W = 3 * SB_W + 4 * DN_W + 2 * DN_HEADS
IN_WP = 3 * SB_W + 4 * DN_W + 128
COL_DNQKV = 3 * SB_W // 128
COL_Z = COL_DNQKV + 3 * DN_W // 128
COL_AB = COL_Z + DN_W // 128

ADAM_LR, ADAM_B1, ADAM_B2, ADAM_EPS, ADAM_WD, ADAM_STEP = 0.001, 0.9, 0.999, 1e-08, 0.01, 10

VMEM_LIMIT = 56 * 1024 * 1024


def _cp(sem):
    return pltpu.CompilerParams(dimension_semantics=sem, vmem_limit_bytes=VMEM_LIMIT)


def _pick(dim, pref):
    return pref if dim % pref == 0 else dim


def _sds(shape, dtype):
    return jax.ShapeDtypeStruct(tuple(shape), dtype)


_NN = (((1,), (0,)), ((), ()))
_NT = (((1,), (1,)), ((), ()))
_TN = (((0,), (0,)), ((), ()))


def _bdot(a, b, dims=_NN):
    return lax.dot_general(a.astype(BF16), b.astype(BF16), dims, preferred_element_type=F32)


def _split(a):
    hi = a.astype(BF16)
    lo = (a - hi.astype(F32)).astype(BF16)
    return hi, lo


def _pdot(a, b, dims=_NN):
    ah, al = _split(a)
    bh, bl = _split(b)
    d = functools.partial(lax.dot_general, dimension_numbers=dims, preferred_element_type=F32)
    return d(ah, bh) + (d(ah, bl) + d(al, bh))


def _sigmoid(x):
    return 1.0 / (1.0 + jnp.exp(-x))


def _softplus(x):
    return jnp.maximum(x, 0.0) + jnp.log(1.0 + jnp.exp(-jnp.abs(x)))


def matmul(a, b, *, mode, name, out_dtypes=(F32,), a_fn=None, a_extras=(), b_fn=None, b_extras=(),
           epi=None, extras=(), tm=1024, tn=1024, tk=1024):
    if mode == "tn":
        K, M = a.shape
    else:
        M, K = a.shape
    N = b.shape[0] if mode == "nt" else b.shape[1]
    tm, tn, tk = _pick(M, tm), _pick(N, tn), _pick(K, tk)
    nk = K // tk
    dims = {"nn": _NN, "nt": _NT, "tn": _TN}[mode]
    a_spec = pl.BlockSpec((tk, tm), lambda i, j, k: (k, i)) if mode == "tn" else pl.BlockSpec((tm, tk), lambda i, j, k: (i, k))
    b_spec = pl.BlockSpec((tn, tk), lambda i, j, k: (j, k)) if mode == "nt" else pl.BlockSpec((tk, tn), lambda i, j, k: (k, j))
    na, nb, ne, no = len(a_extras), len(b_extras), len(extras), len(out_dtypes)

    def body(*refs):
        a_ref, b_ref = refs[0], refs[1]
        ax = refs[2:2 + na]
        bx = refs[2 + na:2 + na + nb]
        ex = refs[2 + na + nb:2 + na + nb + ne]
        outs = refs[2 + na + nb + ne:2 + na + nb + ne + no]
        acc_ref = refs[-1]
        k = pl.program_id(2)

        @pl.when(k == 0)
        def _():
            acc_ref[...] = jnp.zeros_like(acc_ref)

        av = a_ref[...]
        if a_fn is not None:
            av = a_fn(av, *[r[...] for r in ax])
        bv = b_ref[...]
        if b_fn is not None:
            bv = b_fn(bv, *[r[...] for r in bx])
        acc_ref[...] += lax.dot_general(av.astype(BF16), bv.astype(BF16), dims, preferred_element_type=F32)

        @pl.when(k == nk - 1)
        def _():
            res = acc_ref[...]
            res = (res,) if epi is None else epi(res, *[r[...] for r in ex])
            for o_ref, r in zip(outs, res):
                o_ref[...] = r.astype(o_ref.dtype)

    xs = list(a_extras) + list(b_extras) + list(extras)
    return pl.pallas_call(
        body,
        grid=(M // tm, N // tn, nk),
        in_specs=[a_spec, b_spec] + [pl.BlockSpec(bs, im) for _, bs, im in xs],
        out_specs=[pl.BlockSpec((tm, tn), lambda i, j, k: (i, j)) for _ in out_dtypes],
        out_shape=[_sds((M, N), dt) for dt in out_dtypes],
        scratch_shapes=[pltpu.VMEM((tm, tn), F32)],
        compiler_params=_cp(("parallel", "parallel", "arbitrary")),
        name=name,
    )(a, b, *[x for x, _, _ in xs])


def ln_mod_fwd(x, gain, sc, sh, name):
    B, T, D = x.shape
    tt = _pick(T, 512)

    def body(x_ref, g_ref, sc_ref, sh_ref, h_ref):
        xv = x_ref[0]
        r = lax.rsqrt(jnp.mean(xv * xv, axis=-1, keepdims=True) + EPS)
        h = (xv * r * g_ref[...]) * (1.0 + sc_ref[0]) + sh_ref[0]
        h_ref[0] = h.astype(h_ref.dtype)

    return pl.pallas_call(
        body, grid=(B, T // tt),
        in_specs=[pl.BlockSpec((1, tt, D), lambda b, t: (b, t, 0)), pl.BlockSpec((1, D), lambda b, t: (0, 0)),
                  pl.BlockSpec((1, 1, D), lambda b, t: (b, 0, 0)), pl.BlockSpec((1, 1, D), lambda b, t: (b, 0, 0))],
        out_specs=pl.BlockSpec((1, tt, D), lambda b, t: (b, t, 0)),
        out_shape=_sds((B, T, D), BF16),
        compiler_params=_cp(("parallel", "parallel")), name=name,
    )(x, gain, sc, sh)


def ln_mod_bwd(x, gain, sc, dh, dres, name):
    B, T, D = x.shape
    tt = _pick(T, 512)

    def body(x_ref, g_ref, sc_ref, dh_ref, dres_ref, dx_ref, dg_ref, dsc_ref, dsh_ref):
        b, t = pl.program_id(0), pl.program_id(1)
        xv, dhv = x_ref[0], dh_ref[0]
        g, s = g_ref[...], sc_ref[0]
        r = lax.rsqrt(jnp.mean(xv * xv, axis=-1, keepdims=True) + EPS)
        xn = xv * r
        dxn = dhv * (g * (1.0 + s))
        dx_ref[0] = dres_ref[0] + r * (dxn - xn * jnp.mean(dxn * xn, axis=-1, keepdims=True))
        s1 = jnp.sum(dhv * xn, axis=0, keepdims=True)
        s2 = jnp.sum(dhv, axis=0, keepdims=True)

        @pl.when(t == 0)
        def _():
            dsc_ref[0] = jnp.zeros_like(s1)
            dsh_ref[0] = jnp.zeros_like(s1)

        @pl.when((t == 0) & (b == 0))
        def _():
            dg_ref[...] = jnp.zeros_like(s1)

        dsc_ref[0] += s1 * g
        dsh_ref[0] += s2
        dg_ref[...] += s1 * (1.0 + s)

    tile = pl.BlockSpec((1, tt, D), lambda b, t: (b, t, 0))
    row = pl.BlockSpec((1, D), lambda b, t: (0, 0))
    brow = pl.BlockSpec((1, 1, D), lambda b, t: (b, 0, 0))
    return pl.pallas_call(
        body, grid=(B, T // tt),
        in_specs=[tile, row, brow, tile, tile],
        out_specs=[tile, row, brow, brow],
        out_shape=[_sds((B, T, D), F32), _sds((1, D), F32), _sds((B, 1, D), F32), _sds((B, 1, D), F32)],
        compiler_params=_cp(("arbitrary", "arbitrary")), name=name,
    )(x, gain, sc, dh, dres)


def rowsum_prod(a, b, name):
    B, T, D = a.shape
    tt = _pick(T, 512)

    def body(a_ref, b_ref, o_ref):
        @pl.when(pl.program_id(1) == 0)
        def _():
            o_ref[...] = jnp.zeros_like(o_ref)

        o_ref[0] += jnp.sum(a_ref[0] * b_ref[0], axis=0, keepdims=True)

    tile = pl.BlockSpec((1, tt, D), lambda b, t: (b, t, 0))
    return pl.pallas_call(
        body, grid=(B, T // tt), in_specs=[tile, tile],
        out_specs=pl.BlockSpec((1, 1, D), lambda b, t: (b, 0, 0)),
        out_shape=_sds((B, 1, D), F32),
        compiler_params=_cp(("parallel", "arbitrary")), name=name,
    )(a, b)


def loss_grad(y, tgt, name):
    B, T, D = y.shape
    tt = _pick(T, 512)

    def body(y_ref, t_ref, dy_ref, s_ref):
        @pl.when((pl.program_id(0) == 0) & (pl.program_id(1) == 0))
        def _():
            s_ref[...] = jnp.zeros_like(s_ref)

        e = y_ref[0] - t_ref[0]
        dy_ref[0] = e * (1.0 / D)
        s_ref[...] += jnp.sum(e * e, axis=0, keepdims=True)

    tile = pl.BlockSpec((1, tt, D), lambda b, t: (b, t, 0))
    return pl.pallas_call(
        body, grid=(B, T // tt), in_specs=[tile, tile],
        out_specs=[tile, pl.BlockSpec((1, D), lambda b, t: (0, 0))],
        out_shape=[_sds((B, T, D), F32), _sds((1, D), F32)],
        compiler_params=_cp(("arbitrary", "arbitrary")), name=name,
    )(y, tgt)


def _sb_group(nb):
    return 4 if nb % 4 == 0 else (2 if nb % 2 == 0 else 1)


def _tri_sum(x, tri):
    hi, lo = _split(x)
    return (lax.dot_general(hi, tri, _NN, preferred_element_type=F32)
            + lax.dot_general(lo, tri, _NN, preferred_element_type=F32))


def sb_attn_fwd(proj3, gq, gk, name):
    B, T, _ = proj3.shape
    NB = T // SB_BLK
    G = _sb_group(NB)
    KW = G * SB_BLK
    scale = SB_HD ** -0.5

    def body(q_ref, k_ref, v_ref, gq_ref, gk_ref, o_ref, tot_ref, qn_s, kn_s, v_s):
        row_io = lax.broadcasted_iota(jnp.int32, (SB_BLK, SB_BLK), 0)
        col_io = lax.broadcasted_iota(jnp.int32, (SB_BLK, SB_BLK), 1)
        tri = (row_io > col_io).astype(BF16)

        def prep(i, _):
            rows = pl.ds(pl.multiple_of(i * SB_BLK, SB_BLK), SB_BLK)
            for hh in range(2):
                sl = slice(hh * SB_HD, (hh + 1) * SB_HD)
                q = q_ref[0, rows, sl]
                k = k_ref[0, rows, sl]
                qn_s[hh, rows, :] = (q * lax.rsqrt(jnp.mean(q * q, -1, keepdims=True) + EPS) * (gq_ref[...] * scale)).astype(BF16)
                kn_s[hh, rows, :] = (k * lax.rsqrt(jnp.mean(k * k, -1, keepdims=True) + EPS) * gk_ref[...]).astype(BF16)
                v_s[hh, rows, :] = v_ref[0, rows, sl].astype(BF16)
            return 0

        lax.fori_loop(0, NB, prep, 0)

        row_w = lax.broadcasted_iota(jnp.int32, (SB_BLK, KW), 0)
        col_w = lax.broadcasted_iota(jnp.int32, (SB_BLK, KW), 1)

        def qblock(i, _):
            rows = pl.ds(pl.multiple_of(i * SB_BLK, SB_BLK), SB_BLK)
            qn = [qn_s[hh, rows, :] for hh in range(2)]
            nsj = i // G + 1

            def sblock(jj, carry):
                sj = nsj - 1 - jj
                cols = pl.ds(pl.multiple_of(sj * KW, KW), KW)
                mask = (sj * KW + col_w) < (i * SB_BLK + row_w)
                new = []
                for hh in range(2):
                    acc, cs = carry[hh]
                    z = lax.dot_general(qn[hh], kn_s[hh, cols, :], _NT, preferred_element_type=F32)
                    sp = _softplus(z)
                    lg = jnp.where(mask, -sp, 0.0)
                    ls = z - sp
                    ps = []
                    for s in reversed(range(G)):
                        lgs = lg[:, s * SB_BLK:(s + 1) * SB_BLK]
                        ps.append(ls[:, s * SB_BLK:(s + 1) * SB_BLK] + _tri_sum(lgs, tri) + cs)
                        cs = cs + jnp.sum(lgs, axis=1, keepdims=True)
                    p = ps[0] if G == 1 else jnp.concatenate(ps[::-1], axis=1)
                    att = jnp.where(mask, jnp.exp(p), 0.0)
                    acc = acc + lax.dot_general(att.astype(BF16), v_s[hh, cols, :], _NN, preferred_element_type=F32)
                    new.append((acc, cs))
                return tuple(new)

            init = (jnp.zeros((SB_BLK, SB_HD), F32), jnp.zeros((SB_BLK, 1), F32))
            res = lax.fori_loop(0, nsj, sblock, (init, init))
            for hh in range(2):
                o_ref[0, rows, hh * SB_HD:(hh + 1) * SB_HD] = res[hh][0].astype(o_ref.dtype)
                tot_ref[0, hh, rows, :] = res[hh][1]
            return 0

        lax.fori_loop(0, NB, qblock, 0)

    blk = lambda off: pl.BlockSpec((1, T, 128), lambda b, p: (b, 0, off + p))
    grow = pl.BlockSpec((1, SB_HD), lambda b, p: (0, 0))
    return pl.pallas_call(
        body, grid=(B, SB_W // 128),
        in_specs=[blk(0), blk(SB_W // 128), blk(2 * SB_W // 128), grow, grow],
        out_specs=[pl.BlockSpec((1, T, 128), lambda b, p: (b, 0, p)), pl.BlockSpec((1, 2, T, 1), lambda b, p: (b, p, 0, 0))],
        out_shape=[_sds((B, T, SB_W), BF16), _sds((B, SB_HEADS, T, 1), F32)],
        scratch_shapes=[pltpu.VMEM((2, T, SB_HD), BF16)] * 3,
        compiler_params=_cp(("parallel", "parallel")), name=name,
    )(proj3, proj3, proj3, gq, gk)


def sb_attn_bwd(proj3, gq, gk, tot, dmix3, name):
    B, T, _ = proj3.shape
    NB = T // SB_BLK
    G = _sb_group(NB)
    KW = G * SB_BLK
    scale = SB_HD ** -0.5

    def body(q_ref, k_ref, v_ref, gq_ref, gk_ref, tot_ref, do_ref, dq_ref, dk_ref, dv_ref, dgq_ref, dgk_ref,
             qn_s, kn_s, v_s, do_s, dqn_s, dkn_s, dv_s):
        row_io = lax.broadcasted_iota(jnp.int32, (SB_BLK, SB_BLK), 0)
        col_io = lax.broadcasted_iota(jnp.int32, (SB_BLK, SB_BLK), 1)
        tri = (row_io > col_io).astype(BF16)
        trip = (row_io < col_io).astype(BF16)

        @pl.when((pl.program_id(0) == 0) & (pl.program_id(1) == 0))
        def _():
            dgq_ref[...] = jnp.zeros_like(dgq_ref)
            dgk_ref[...] = jnp.zeros_like(dgk_ref)

        def prep(i, _):
            rows = pl.ds(pl.multiple_of(i * SB_BLK, SB_BLK), SB_BLK)
            for hh in range(2):
                sl = slice(hh * SB_HD, (hh + 1) * SB_HD)
                q = q_ref[0, rows, sl]
                k = k_ref[0, rows, sl]
                qn_s[hh, rows, :] = (q * lax.rsqrt(jnp.mean(q * q, -1, keepdims=True) + EPS) * (gq_ref[...] * scale)).astype(BF16)
                kn_s[hh, rows, :] = (k * lax.rsqrt(jnp.mean(k * k, -1, keepdims=True) + EPS) * gk_ref[...]).astype(BF16)
                v_s[hh, rows, :] = v_ref[0, rows, sl].astype(BF16)
                do_s[hh, rows, :] = do_ref[0, rows, sl].astype(BF16)
            return 0

        lax.fori_loop(0, NB, prep, 0)
        dkn_s[...] = jnp.zeros_like(dkn_s)
        dv_s[...] = jnp.zeros_like(dv_s)

        row_w = lax.broadcasted_iota(jnp.int32, (SB_BLK, KW), 0)
        col_w = lax.broadcasted_iota(jnp.int32, (SB_BLK, KW), 1)

        def qblock(i, _):
            rows = pl.ds(pl.multiple_of(i * SB_BLK, SB_BLK), SB_BLK)
            qn = [qn_s[hh, rows, :] for hh in range(2)]
            dov = [do_s[hh, rows, :] for hh in range(2)]
            tot = [tot_ref[0, hh, rows, :] for hh in range(2)]

            def sblock(sj, carry):
                cols = pl.ds(pl.multiple_of(sj * KW, KW), KW)
                mask = (sj * KW + col_w) < (i * SB_BLK + row_w)
                new = []
                for hh in range(2):
                    dq, cum, cdp = carry[hh]
                    kn = kn_s[hh, cols, :]
                    z = lax.dot_general(qn[hh], kn, _NT, preferred_element_type=F32)
                    sp = _softplus(z)
                    lg = jnp.where(mask, -sp, 0.0)
                    ls = z - sp
                    ps = []
                    for s in range(G):
                        lgs = lg[:, s * SB_BLK:(s + 1) * SB_BLK]
                        cum = cum + jnp.sum(lgs, axis=1, keepdims=True)
                        ps.append(ls[:, s * SB_BLK:(s + 1) * SB_BLK] + _tri_sum(lgs, tri) + (tot[hh] - cum))
                    p = ps[0] if G == 1 else jnp.concatenate(ps, axis=1)
                    att = jnp.where(mask, jnp.exp(p), 0.0)
                    dp = att * lax.dot_general(dov[hh], v_s[hh, cols, :], _NT, preferred_element_type=F32)
                    dls = []
                    for s in range(G):
                        dps = dp[:, s * SB_BLK:(s + 1) * SB_BLK]
                        dls.append(_tri_sum(dps, trip) + cdp)
                        cdp = cdp + jnp.sum(dps, axis=1, keepdims=True)
                    dlg = dls[0] if G == 1 else jnp.concatenate(dls, axis=1)
                    sig = jnp.exp(ls)
                    dz = jnp.where(mask, dp * (1.0 - sig) - dlg * sig, 0.0).astype(BF16)
                    dq = dq + lax.dot_general(dz, kn, _NN, preferred_element_type=F32)
                    dkn_s[hh, cols, :] += lax.dot_general(dz, qn[hh], _TN, preferred_element_type=F32)
                    dv_s[hh, cols, :] += lax.dot_general(att.astype(BF16), dov[hh], _TN, preferred_element_type=F32)
                    new.append((dq, cum, cdp))
                return tuple(new)

            z1 = jnp.zeros((SB_BLK, 1), F32)
            init = (jnp.zeros((SB_BLK, SB_HD), F32), z1, z1)
            res = lax.fori_loop(0, i // G + 1, sblock, (init, init))
            for hh in range(2):
                dqn_s[hh, rows, :] = res[hh][0]
            return 0

        lax.fori_loop(0, NB, qblock, 0)

        def fin(i, carry):
            aq, ak = carry
            rows = pl.ds(pl.multiple_of(i * SB_BLK, SB_BLK), SB_BLK)
            for hh in range(2):
                sl = slice(hh * SB_HD, (hh + 1) * SB_HD)
                for src_ref, d_s, g_ref, out_ref, mult, which in ((q_ref, dqn_s, gq_ref, dq_ref, scale, 0), (k_ref, dkn_s, gk_ref, dk_ref, 1.0, 1)):
                    xr = src_ref[0, rows, sl]
                    r = lax.rsqrt(jnp.mean(xr * xr, -1, keepdims=True) + EPS)
                    dy = d_s[hh, rows, :] * mult
                    u = dy * g_ref[...]
                    out_ref[0, rows, sl] = r * u - xr * (r * r * r) * jnp.mean(u * xr, -1, keepdims=True)
                    part = jnp.sum(dy * xr * r, axis=0, keepdims=True)
                    if which == 0:
                        aq = aq + part
                    else:
                        ak = ak + part
                dv_ref[0, rows, sl] = dv_s[hh, rows, :]
            return aq, ak

        z64 = jnp.zeros((1, SB_HD), F32)
        aq, ak = lax.fori_loop(0, NB, fin, (z64, z64))
        dgq_ref[...] += aq
        dgk_ref[...] += ak

    blk = lambda off: pl.BlockSpec((1, T, 128), lambda b, p: (b, 0, off + p))
    grow = pl.BlockSpec((1, SB_HD), lambda b, p: (0, 0))
    return pl.pallas_call(
        body, grid=(B, SB_W // 128),
        in_specs=[blk(0), blk(SB_W // 128), blk(2 * SB_W // 128), grow, grow,
                  pl.BlockSpec((1, 2, T, 1), lambda b, p: (b, p, 0, 0)), blk(0)],
        out_specs=[blk(0), blk(0), blk(0), grow, grow],
        out_shape=[_sds((B, T, SB_W), F32)] * 3 + [_sds((1, SB_HD), F32)] * 2,
        scratch_shapes=[pltpu.VMEM((2, T, SB_HD), BF16)] * 4 + [pltpu.VMEM((2, T, SB_HD), F32)] * 3,
        compiler_params=_cp(("arbitrary", "arbitrary")), name=name,
    )(proj3, proj3, proj3, gq, gk, tot, dmix3)


def _conv_silu(x, w, T):
    t_io = lax.broadcasted_iota(jnp.int32, x.shape, 0)
    xs = [x] + [jnp.where(t_io >= s, pltpu.roll(x, s, 0), 0.0) for s in range(1, CONV_K)]
    y = xs[0] * w[CONV_K - 1:CONV_K, :]
    for s in range(1, CONV_K):
        y = y + xs[s] * w[CONV_K - 1 - s:CONV_K - s, :]
    return y, y * _sigmoid(y), xs


def gdn_pre_fwd(proj3, conv_w, name):
    B, T, _ = proj3.shape
    qs = DN_HD ** -0.5

    def body(x_ref, w_ref, o_ref):
        kind = pl.program_id(1) // DN_HEADS
        _, s, _ = _conv_silu(x_ref[0], w_ref[...], T)
        n = lax.rsqrt(jnp.sum(s * s, axis=-1, keepdims=True) + EPS)
        c = jnp.where(kind == 0, qs, 1.0)
        o_ref[0, 0] = jnp.where(kind < 2, s * (n * c), s)

    return pl.pallas_call(
        body, grid=(B, 3 * DN_HEADS),
        in_specs=[pl.BlockSpec((1, T, 128), lambda b, j: (b, 0, COL_DNQKV + j)), pl.BlockSpec((CONV_K, 128), lambda b, j: (0, j))],
        out_specs=pl.BlockSpec((1, 1, T, 128), lambda b, j: (b, j // DN_HEADS, 0, j % DN_HEADS)),
        out_shape=_sds((B, 3, T, DN_W), F32),
        compiler_params=_cp(("parallel", "parallel")), name=name,
    )(proj3, conv_w)


def gdn_pre_bwd(proj3, conv_w, dqkv, name):
    B, T, _ = proj3.shape
    qs = DN_HD ** -0.5

    def body(x_ref, w_ref, d_ref, dx_ref, dw_ref):
        kind = pl.program_id(1) // DN_HEADS
        w = w_ref[...]
        y, s, xs = _conv_silu(x_ref[0], w, T)
        dout = d_ref[0, 0]
        n = lax.rsqrt(jnp.sum(s * s, axis=-1, keepdims=True) + EPS)
        c = jnp.where(kind == 0, qs, 1.0)
        dsn = c * (n * dout - s * (n * n * n) * jnp.sum(dout * s, axis=-1, keepdims=True))
        ds = jnp.where(kind < 2, dsn, dout)
        sg = _sigmoid(y)
        dy = ds * (sg * (1.0 + y * (1.0 - sg)))
        t_io = lax.broadcasted_iota(jnp.int32, dy.shape, 0)
        dx = dy * w[CONV_K - 1:CONV_K, :]
        dw_ref[0, CONV_K - 1:CONV_K, :] = jnp.sum(dy * xs[0], axis=0, keepdims=True)
        for sft in range(1, CONV_K):
            dx = dx + jnp.where(t_io < T - sft, pltpu.roll(dy, T - sft, 0), 0.0) * w[CONV_K - 1 - sft:CONV_K - sft, :]
            dw_ref[0, CONV_K - 1 - sft:CONV_K - sft, :] = jnp.sum(dy * xs[sft], axis=0, keepdims=True)
        dx_ref[0] = dx

    return pl.pallas_call(
        body, grid=(B, 3 * DN_HEADS),
        in_specs=[pl.BlockSpec((1, T, 128), lambda b, j: (b, 0, COL_DNQKV + j)), pl.BlockSpec((CONV_K, 128), lambda b, j: (0, j)),
                  pl.BlockSpec((1, 1, T, 128), lambda b, j: (b, j // DN_HEADS, 0, j % DN_HEADS))],
        out_specs=[pl.BlockSpec((1, T, 128), lambda b, j: (b, 0, j)), pl.BlockSpec((1, CONV_K, 128), lambda b, j: (b, 0, j))],
        out_shape=[_sds((B, T, 3 * DN_W), F32), _sds((B, CONV_K, 3 * DN_W), F32)],
        compiler_params=_cp(("parallel", "parallel")), name=name,
    )(proj3, conv_w, dqkv)


def gdn_gates_fwd(proj3, alog_row, dtb_row, name):
    B, T, _ = proj3.shape

    def body(x_ref, al_ref, dt_ref, o_ref):
        x = x_ref[0]
        lane = lax.broadcasted_iota(jnp.int32, x.shape, 1)
        g = -jnp.exp(al_ref[...]) * _softplus(x + dt_ref[...])
        o_ref[0] = jnp.where(lane < DN_HEADS, g, jnp.where(lane < 2 * DN_HEADS, _sigmoid(x), 0.0))

    row = pl.BlockSpec((1, 128), lambda b: (0, 0))
    return pl.pallas_call(
        body, grid=(B,),
        in_specs=[pl.BlockSpec((1, T, 128), lambda b: (b, 0, COL_AB)), row, row],
        out_specs=pl.BlockSpec((1, T, 128), lambda b: (b, 0, 0)),
        out_shape=_sds((B, T, 128), F32),
        compiler_params=_cp(("parallel",)), name=name,
    )(proj3, alog_row, dtb_row)


def gdn_gates_bwd(proj3, alog_row, dtb_row, dgates, name):
    B, T, _ = proj3.shape

    def body(x_ref, al_ref, dt_ref, d_ref, dx_ref, dal_ref, ddt_ref):
        @pl.when(pl.program_id(0) == 0)
        def _():
            dal_ref[...] = jnp.zeros_like(dal_ref)
            ddt_ref[...] = jnp.zeros_like(ddt_ref)

        x, d = x_ref[0], d_ref[0]
        lane = lax.broadcasted_iota(jnp.int32, x.shape, 1)
        a = x + dt_ref[...]
        na = -jnp.exp(al_ref[...])
        da = jnp.where(lane < DN_HEADS, d * na * _sigmoid(a), 0.0)
        bt = _sigmoid(x)
        dx_ref[0] = da + jnp.where((lane >= DN_HEADS) & (lane < 2 * DN_HEADS), d * bt * (1.0 - bt), 0.0)
        dal_ref[...] += jnp.sum(jnp.where(lane < DN_HEADS, d * na * _softplus(a), 0.0), axis=0, keepdims=True)
        ddt_ref[...] += jnp.sum(da, axis=0, keepdims=True)

    row = pl.BlockSpec((1, 128), lambda b: (0, 0))
    tile = pl.BlockSpec((1, T, 128), lambda b: (b, 0, 0))
    return pl.pallas_call(
        body, grid=(B,),
        in_specs=[pl.BlockSpec((1, T, 128), lambda b: (b, 0, COL_AB)), row, row, tile],
        out_specs=[tile, row, row],
        out_shape=[_sds((B, T, 128), F32), _sds((1, 128), F32), _sds((1, 128), F32)],
        compiler_params=_cp(("arbitrary",)), name=name,
    )(proj3, alog_row, dtb_row, dgates)


def gdn_post_fwd(ob, proj3, gain, name):
    B, T, _ = ob.shape

    def body(o_ref, z_ref, g_ref, out_ref):
        o, z = o_ref[0], z_ref[0]
        r = lax.rsqrt(jnp.mean(o * o, axis=-1, keepdims=True) + EPS)
        out_ref[0] = ((o * r * g_ref[...]) * (z * _sigmoid(z))).astype(out_ref.dtype)

    tile = pl.BlockSpec((1, T, 128), lambda b, h: (b, 0, h))
    return pl.pallas_call(
        body, grid=(B, DN_HEADS),
        in_specs=[tile, pl.BlockSpec((1, T, 128), lambda b, h: (b, 0, COL_Z + h)), pl.BlockSpec((1, 128), lambda b, h: (0, 0))],
        out_specs=tile, out_shape=_sds((B, T, DN_W), BF16),
        compiler_params=_cp(("parallel", "parallel")), name=name,
    )(ob, proj3, gain)


def gdn_post_bwd(ob, proj3, gain, dmix3, name):
    B, T, _ = ob.shape

    def body(o_ref, z_ref, g_ref, d_ref, do_ref, dz_ref, dg_ref):
        @pl.when((pl.program_id(0) == 0) & (pl.program_id(1) == 0))
        def _():
            dg_ref[...] = jnp.zeros_like(dg_ref)

        o, z, d, g = o_ref[0], z_ref[0], d_ref[0], g_ref[...]
        r = lax.rsqrt(jnp.mean(o * o, axis=-1, keepdims=True) + EPS)
        sg = _sigmoid(z)
        dn = d * (z * sg)
        dz_ref[0] = d * (o * r * g) * (sg * (1.0 + z * (1.0 - sg)))
        dg_ref[...] += jnp.sum(dn * o * r, axis=0, keepdims=True)
        u = dn * g
        do_ref[0] = r * u - o * (r * r * r) * jnp.mean(u * o, axis=-1, keepdims=True)

    tile = pl.BlockSpec((1, T, 128), lambda b, h: (b, 0, h))
    row = pl.BlockSpec((1, 128), lambda b, h: (0, 0))
    return pl.pallas_call(
        body, grid=(B, DN_HEADS),
        in_specs=[tile, pl.BlockSpec((1, T, 128), lambda b, h: (b, 0, COL_Z + h)), row,
                  pl.BlockSpec((1, T, 128), lambda b, h: (b, 0, SB_W // 128 + h))],
        out_specs=[tile, tile, row],
        out_shape=[_sds((B, T, DN_W), F32), _sds((B, T, DN_W), F32), _sds((1, 128), F32)],
        compiler_params=_cp(("arbitrary", "arbitrary")), name=name,
    )(ob, proj3, gain, dmix3)


def _tri_inv(low, ri, ci):
    m = jnp.broadcast_to((ri == ci).astype(F32), low.shape)
    s = 1
    while s < DN_C:
        sh = s.bit_length()
        off = ((ri >> sh) == (ci >> sh)) & ((ri & (2 * s - 1)) >= s) & ((ci & (2 * s - 1)) < s)
        m = m - _pdot(m, _pdot(jnp.where(off, low, 0.0), m, _BNN), _BNN)
        s *= 2
    return m


_BNN = (((2,), (1,)), ((0,), (0,)))
_BNT = (((2,), (2,)), ((0,), (0,)))
_BTN = (((1,), (1,)), ((0,), (0,)))
DN_G = 4


def _chunk_common(q, k, v, gt, h, tm=None):
    C = DN_C
    G = q.shape[0]
    ri = lax.broadcasted_iota(jnp.int32, (C, C), 0)
    ci = lax.broadcasted_iota(jnp.int32, (C, C), 1)
    lane = lax.broadcasted_iota(jnp.int32, (C, 128), 1)
    incl, strict = ri >= ci, ri > ci
    g = jnp.sum(jnp.where(lane == h, gt, 0.0), axis=2, keepdims=True)
    beta = jnp.sum(jnp.where(lane == h + DN_HEADS, gt, 0.0), axis=2, keepdims=True)
    ones = jnp.ones((G, C, 128), F32)
    inclf = jnp.broadcast_to(incl.astype(F32), (G, C, C))
    gc = _pdot(inclf, g * ones, _BNN)[:, :, :1]
    gcr = _pdot(jnp.ones((G, C, C), F32), jnp.where(ri == ci, gc, 0.0), _BNN)
    decay = jnp.where(incl, jnp.exp(jnp.where(incl, gc - gcr, 0.0)), 0.0)
    e = jnp.exp(gc)
    kb, vb = k * beta, v * beta
    kk = _bdot(kb, k, _BNT)
    if tm is None:
        tm = _tri_inv(jnp.where(strict, kk * decay, 0.0), ri, ci)
    kbe = kb * e
    u = _bdot(tm, vb, _BNN)
    w = _bdot(tm, kbe, _BNN)
    qk = _bdot(q, k, _BNT)
    intra = jnp.where(incl, qk * decay, 0.0)
    gl = gc[:, C - 1:C, :]
    el = jnp.exp(gl)
    r = jnp.exp(gl - gc)
    return dict(lane=lane, incl=incl, inclf=inclf, strict=strict, beta=beta, decay=decay, e=e,
                kb=kb, vb=vb, kk=kk, tm=tm, kbe=kbe, u=u, w=w, qk=qk, intra=intra, el=el, r=r, ones=ones)


def gdn_chunk_fwd(qkv, gates, name):
    B, _, T, _ = qkv.shape
    NC = T // DN_C

    G = DN_G if NC % DN_G == 0 else 1
    GC = G * DN_C

    def body(x_ref, gt_ref, o_ref, st_ref, s_s, u_s, w_s, qe_s, kr_s, in_s, el_s):
        h = pl.program_id(1)

        def group(gi, _):
            rows = pl.ds(pl.multiple_of(gi * GC, GC), GC)
            q, k, v = [x_ref[0, i, rows, :].reshape(G, DN_C, DN_HD) for i in range(3)]
            c = _chunk_common(q, k, v, gt_ref[0, rows, :].reshape(G, DN_C, 128), h)
            u_s[rows, :] = c["u"].reshape(GC, DN_HD)
            w_s[rows, :] = c["w"].reshape(GC, DN_HD)
            qe_s[rows, :] = (q * c["e"]).reshape(GC, DN_HD)
            kr_s[rows, :] = (k * c["r"]).reshape(GC, DN_HD)
            in_s[rows, :] = c["intra"].reshape(GC, DN_C)
            el_s[pl.ds(gi * G, G), :, :] = c["el"] * jnp.ones((G, 1, 128), F32)
            return 0

        lax.fori_loop(0, NC // G, group, 0)
        s_s[...] = jnp.zeros_like(s_s)

        def chunk(n, _):
            rows = pl.ds(pl.multiple_of(n * DN_C, DN_C), DN_C)
            st = s_s[...]
            st_ref[0, 0, pl.ds(pl.multiple_of(n * DN_HD, DN_HD), DN_HD), :] = st
            v_new = u_s[rows, :] - _bdot(w_s[rows, :], st)
            o_ref[0, rows, :] = _bdot(qe_s[rows, :], st) + _bdot(in_s[rows, :], v_new)
            s_s[...] = st * el_s[n] + _bdot(kr_s[rows, :], v_new, _TN)
            return 0

        lax.fori_loop(0, NC, chunk, 0)

    return pl.pallas_call(
        body, grid=(B, DN_HEADS),
        in_specs=[pl.BlockSpec((1, 3, T, 128), lambda b, h: (b, 0, 0, h)), pl.BlockSpec((1, T, 128), lambda b, h: (b, 0, 0))],
        out_specs=[pl.BlockSpec((1, T, 128), lambda b, h: (b, 0, h)), pl.BlockSpec((1, 1, NC * DN_HD, DN_HD), lambda b, h: (b, h, 0, 0))],
        out_shape=[_sds((B, T, DN_W), F32), _sds((B, DN_HEADS, NC * DN_HD, DN_HD), F32)],
        scratch_shapes=[pltpu.VMEM((DN_HD, DN_HD), F32)] + [pltpu.VMEM((T, DN_HD), F32)] * 4
        + [pltpu.VMEM((T, DN_C), F32), pltpu.VMEM((NC, 1, 128), F32)],
        compiler_params=_cp(("parallel", "parallel")), name=name,
    )(qkv, gates)


def gdn_chunk_bwd(qkv, gates, states, dob, name):
    B, _, T, _ = qkv.shape
    NC = T // DN_C
    C = DN_C

    G = DN_G if NC % DN_G == 0 else 1
    GC = G * C

    def body(x_ref, gt_ref, st_ref, do_ref, dx_ref, dgt_ref, ds_s, w_s, qe_s, kr_s, in_s, tm_s, el_s, dvn_s, dsa_s):
        h = pl.program_id(1)

        @pl.when(h == 0)
        def _():
            dgt_ref[...] = jnp.zeros_like(dgt_ref)

        def load(gi):
            rows = pl.ds(pl.multiple_of(gi * GC, GC), GC)
            q, k, v = [x_ref[0, i, rows, :].reshape(G, C, DN_HD) for i in range(3)]
            return rows, q, k, v, gt_ref[0, rows, :].reshape(G, C, 128)

        def group_a(gi, _):
            rows, q, k, v, gt = load(gi)
            c = _chunk_common(q, k, v, gt, h)
            w_s[rows, :] = c["w"].reshape(GC, DN_HD)
            qe_s[rows, :] = (q * c["e"]).reshape(GC, DN_HD)
            kr_s[rows, :] = (k * c["r"]).reshape(GC, DN_HD)
            in_s[rows, :] = c["intra"].reshape(GC, C)
            tm_s[rows, :] = c["tm"].reshape(GC, C)
            el_s[pl.ds(gi * G, G), :, :] = c["el"] * jnp.ones((G, 1, 128), F32)
            return 0

        lax.fori_loop(0, NC // G, group_a, 0)
        ds_s[...] = jnp.zeros_like(ds_s)

        def chunk(m, _):
            n = NC - 1 - m
            rows = pl.ds(pl.multiple_of(n * C, C), C)
            do = do_ref[0, rows, :]
            dsn = ds_s[...]
            dsa_s[pl.ds(pl.multiple_of(n * DN_HD, DN_HD), DN_HD), :] = dsn
            dvn = _bdot(kr_s[rows, :], dsn) + _bdot(in_s[rows, :], do, _TN)
            dvn_s[rows, :] = dvn
            ds_s[...] = dsn * el_s[n] + _bdot(qe_s[rows, :], do, _TN) - _bdot(w_s[rows, :], dvn, _TN)
            return 0

        lax.fori_loop(0, NC, chunk, 0)

        def group_c(gi, _):
            rows, q, k, v, gt = load(gi)
            c = _chunk_common(q, k, v, gt, h, tm=tm_s[rows, :].reshape(G, C, C))
            incl, strict, decay, e, r, el, tm = c["incl"], c["strict"], c["decay"], c["e"], c["r"], c["el"], c["tm"]
            srow = pl.ds(pl.multiple_of(gi * G * DN_HD, G * DN_HD), G * DN_HD)
            st = st_ref[0, 0, srow, :].reshape(G, DN_HD, DN_HD)
            dsn = dsa_s[srow, :].reshape(G, DN_HD, DN_HD)
            do = do_ref[0, rows, :].reshape(G, C, DN_HD)
            dvn = dvn_s[rows, :].reshape(G, C, DN_HD)
            v_new = c["u"] - _bdot(c["w"], st, _BNN)
            del_ = jnp.sum(jnp.sum(dsn * st, axis=2, keepdims=True), axis=1, keepdims=True)
            dkr = _bdot(v_new, dsn, _BNT)
            dqe = _bdot(do, st, _BNT)
            dintra = _bdot(do, v_new, _BNT)
            dw = -_bdot(dvn, st, _BNT)
            dqkd = jnp.where(incl, dintra, 0.0)
            dqk = dqkd * decay
            ddecay = dqkd * c["qk"]
            dq = dqe * e + _bdot(dqk, k, _BNN)
            dk = dkr * r + _bdot(dqk, q, _BTN)
            dtm = _bdot(dvn, c["vb"], _BNT) + _bdot(dw, c["kbe"], _BNT)
            dvb = _bdot(tm, dvn, _BTN)
            dkbe = _bdot(tm, dw, _BTN)
            dkb = dkbe * e
            de = jnp.sum(dqe * q, axis=2, keepdims=True) + jnp.sum(dkbe * c["kb"], axis=2, keepdims=True)
            da = -_pdot(tm, _pdot(dtm, tm, _BNT), _BTN)
            dlow = jnp.where(strict, da, 0.0)
            dkk = dlow * decay
            ddecay = ddecay + dlow * c["kk"]
            dkb = dkb + _bdot(dkk, k, _BNN)
            dk = dk + _bdot(dkk, c["kb"], _BTN) + dkb * c["beta"]
            dbeta = jnp.sum(dkb * k, axis=2, keepdims=True) + jnp.sum(dvb * v, axis=2, keepdims=True)
            dv = dvb * c["beta"]
            dd = ddecay * decay
            dgc = jnp.sum(dd, axis=2, keepdims=True) - _pdot(dd, c["ones"], _BTN)[:, :, :1]
            dr = jnp.sum(dkr * k, axis=2, keepdims=True)
            dgc = dgc + de * e - dr * r
            dgl = jnp.sum(dr * r, axis=1, keepdims=True) + del_ * el
            rowc = lax.broadcasted_iota(jnp.int32, (C, 1), 0)
            dgc = dgc + jnp.where(rowc == C - 1, dgl, 0.0)
            dg = _pdot(c["inclf"], dgc * c["ones"], _BTN)[:, :, :1]
            dx_ref[0, 0, rows, :] = dq.reshape(GC, DN_HD)
            dx_ref[0, 1, rows, :] = dk.reshape(GC, DN_HD)
            dx_ref[0, 2, rows, :] = dv.reshape(GC, DN_HD)
            lane = c["lane"]
            dgt_ref[0, rows, :] += (jnp.where(lane == h, dg, 0.0) + jnp.where(lane == h + DN_HEADS, dbeta, 0.0)).reshape(GC, 128)
            return 0

        lax.fori_loop(0, NC // G, group_c, 0)

    return pl.pallas_call(
        body, grid=(B, DN_HEADS),
        in_specs=[pl.BlockSpec((1, 3, T, 128), lambda b, h: (b, 0, 0, h)), pl.BlockSpec((1, T, 128), lambda b, h: (b, 0, 0)),
                  pl.BlockSpec((1, 1, NC * DN_HD, DN_HD), lambda b, h: (b, h, 0, 0)), pl.BlockSpec((1, T, 128), lambda b, h: (b, 0, h))],
        out_specs=[pl.BlockSpec((1, 3, T, 128), lambda b, h: (b, 0, 0, h)), pl.BlockSpec((1, T, 128), lambda b, h: (b, 0, 0))],
        out_shape=[_sds((B, 3, T, DN_W), F32), _sds((B, T, 128), F32)],
        scratch_shapes=[pltpu.VMEM((DN_HD, DN_HD), F32)] + [pltpu.VMEM((T, DN_HD), F32)] * 3
        + [pltpu.VMEM((T, C), F32)] * 2 + [pltpu.VMEM((NC, 1, 128), F32), pltpu.VMEM((T, DN_HD), F32), pltpu.VMEM((NC * DN_HD, DN_HD), F32)],
        compiler_params=_cp(("parallel", "arbitrary")), name=name,
    )(qkv, gates, states, dob)


def ada_fwd(c_all, w_ada, b_sl, name):
    L, D, W = w_ada.shape
    NBt = c_all.shape[0]

    def body(c_ref, w_ref, b_ref, o_ref):
        cv = c_ref[...]
        o_ref[0] = _pdot(cv * _sigmoid(cv), w_ref[0]) + b_ref[0]

    return pl.pallas_call(
        body, grid=(L,),
        in_specs=[pl.BlockSpec((NBt, D), lambda l: (0, 0)), pl.BlockSpec((1, D, W), lambda l: (l, 0, 0)), pl.BlockSpec((1, 1, W), lambda l: (l, 0, 0))],
        out_specs=pl.BlockSpec((1, NBt, W), lambda l: (l, 0, 0)),
        out_shape=_sds((L, NBt, W), F32),
        compiler_params=_cp(("parallel",)), name=name,
    )(c_all, w_ada, b_sl)


def ada_bwd(c_all, dmod_cols, name):
    L, NBt, W = dmod_cols.shape
    D = c_all.shape[1]

    def body(c_ref, d_ref, o_ref):
        cv = c_ref[...]
        o_ref[0] = _pdot(cv * _sigmoid(cv), d_ref[0], _TN)

    return pl.pallas_call(
        body, grid=(L,),
        in_specs=[pl.BlockSpec((NBt, D), lambda l: (0, 0)), pl.BlockSpec((1, NBt, W), lambda l: (l, 0, 0))],
        out_specs=pl.BlockSpec((1, D, W), lambda l: (l, 0, 0)),
        out_shape=_sds((L, D, W), F32),
        compiler_params=_cp(("parallel",)), name=name,
    )(c_all, dmod_cols)


def adamw(partials, w, m, v, name):
    P, R, C = partials.shape
    tr = _pick(R, 256)

    def body(p_ref, w_ref, m_ref, v_ref, g_ref, d_ref, nm_ref, nv_ref):
        g = p_ref[0].astype(F32)
        for i in range(1, P):
            g = g + p_ref[i].astype(F32)
        nm = ADAM_B1 * m_ref[...] + (1.0 - ADAM_B1) * g
        nv = ADAM_B2 * v_ref[...] + (1.0 - ADAM_B2) * (g * g)
        m_hat = nm / (1.0 - ADAM_B1 ** ADAM_STEP)
        v_hat = nv / (1.0 - ADAM_B2 ** ADAM_STEP)
        g_ref[...] = g
        d_ref[...] = -ADAM_LR * (m_hat / (jnp.sqrt(v_hat) + ADAM_EPS) + ADAM_WD * w_ref[...])
        nm_ref[...] = nm
        nv_ref[...] = nv

    tile = pl.BlockSpec((tr, C), lambda i: (i, 0))
    return pl.pallas_call(
        body, grid=(R // tr,),
        in_specs=[pl.BlockSpec((P, tr, C), lambda i: (0, i, 0)), tile, tile, tile],
        out_specs=[tile] * 4, out_shape=[_sds((R, C), F32)] * 4,
        compiler_params=_cp(("parallel",)), name=name,
    )(partials, w, m, v)


def _coords():
    return lax.axis_index("x"), lax.axis_index("y"), lax.axis_index("c")


def all_gather(x, name):
    any_spec = pl.BlockSpec(memory_space=pl.ANY)

    def body(x_ref, out_ref, send_sems, recv_sems, local_sem):
        x_, y_, c_ = _coords()
        me, sibling = (x_, y_, c_), (x_, y_, 1 - c_)
        chips = [(1 - x_, y_), (x_, 1 - y_), (1 - x_, 1 - y_)]

        def slot(px, py, pc):
            return out_ref.at[4 * px + 2 * py + pc]

        def copy(k, block, to, src=None):
            return pltpu.make_async_remote_copy(
                src_ref=slot(*block) if src is None else src, dst_ref=slot(*block),
                send_sem=send_sems.at[k], recv_sem=recv_sems.at[k],
                device_id=to, device_id_type=pl.DeviceIdType.MESH)

        mine = pltpu.make_async_copy(x_ref, slot(*me), local_sem)
        mine.start()
        first = [copy(0, me, sibling, src=x_ref)]
        first += [copy(1 + j, me, (*chip, c_), src=x_ref) for j, chip in enumerate(chips)]
        for cp in first:
            cp.start()
        passed = [copy(4 + j, (*chip, c_), sibling) for j, chip in enumerate(chips)]
        for j, chip in enumerate(chips):
            copy(1 + j, (*chip, c_), me).wait_recv()
            passed[j].start()
        copy(0, sibling, me).wait_recv()
        for j, chip in enumerate(chips):
            copy(4 + j, (*chip, 1 - c_), me).wait_recv()
        for cp in first + passed:
            cp.wait_send()
        mine.wait()

    return pl.pallas_call(
        body, out_shape=_sds((N_DEV,) + x.shape, x.dtype),
        in_specs=[any_spec], out_specs=any_spec,
        scratch_shapes=[pltpu.SemaphoreType.DMA((7,)), pltpu.SemaphoreType.DMA((7,)), pltpu.SemaphoreType.DMA],
        name=name,
    )(x)


def all_to_all(x, name):
    any_spec = pl.BlockSpec(memory_space=pl.ANY)

    def body(x_ref, out_ref, send_sems, recv_sems, local_sem):
        x_, y_, c_ = _coords()
        me = 4 * x_ + 2 * y_ + c_
        mine = pltpu.make_async_copy(x_ref.at[me], out_ref.at[me], local_sem)
        mine.start()
        copies = []
        for k in range(1, N_DEV):
            px = 1 - x_ if k & 4 else x_
            py = 1 - y_ if k & 2 else y_
            pc = 1 - c_ if k & 1 else c_
            peer = 4 * px + 2 * py + pc
            copies.append((pltpu.make_async_remote_copy(
                src_ref=x_ref.at[peer], dst_ref=out_ref.at[me],
                send_sem=send_sems.at[k - 1], recv_sem=recv_sems.at[k - 1],
                device_id=(px, py, pc), device_id_type=pl.DeviceIdType.MESH), peer))
        for cp, _ in copies:
            cp.start()
        for k, (cp, peer) in enumerate(copies):
            pltpu.make_async_remote_copy(
                src_ref=x_ref.at[peer], dst_ref=out_ref.at[peer],
                send_sem=send_sems.at[k], recv_sem=recv_sems.at[k],
                device_id=(x_, y_, c_), device_id_type=pl.DeviceIdType.MESH).wait_recv()
        for cp, _ in copies:
            cp.wait_send()
        mine.wait()

    return pl.pallas_call(
        body, out_shape=_sds(x.shape, x.dtype),
        in_specs=[any_spec], out_specs=any_spec,
        scratch_shapes=[pltpu.SemaphoreType.DMA((7,)), pltpu.SemaphoreType.DMA((7,)), pltpu.SemaphoreType.DMA],
        name=name,
    )(x)


def _rows128(a):
    return a.reshape(-1, 128)


def _pad_lanes(a):
    return jnp.pad(a, ((0, 0), (0, 128 - a.shape[1])))


def kernel(x, c, w_ada, b_ada, norm_mix, norm_mlp, w_in, sb_q_norm, sb_k_norm, conv_w, a_log, dt_bias, dn_out_norm, w_out, w_ff1, w_ff2, loss_target, m_w_ada, m_b_ada, m_norm_mix, m_norm_mlp, m_w_in, m_sb_q_norm, m_sb_k_norm, m_conv_w, m_a_log, m_dt_bias, m_dn_out_norm, m_w_out, m_w_ff1, m_w_ff2, v_w_ada, v_b_ada, v_norm_mix, v_norm_mlp, v_w_in, v_sb_q_norm, v_sb_k_norm, v_conv_w, v_a_log, v_dt_bias, v_dn_out_norm, v_w_out, v_w_ff1, v_w_ff2):
    B, T, D = x.shape
    L = w_ada.shape[0]
    N = B * T
    FF = w_ff1.shape[2] * N_DEV
    WA = w_ada.shape[2]
    CS = conv_w.shape[2]
    me = 4 * lax.axis_index("x") + 2 * lax.axis_index("y") + lax.axis_index("c")
    tm = _pick(T, 1024)

    win_g = all_gather(w_in.astype(BF16), "comm_gather_w_in")
    W_in = jnp.pad(win_g.transpose(1, 2, 0, 3).reshape(L, D, IN_W), ((0, 0), (0, 0), (0, IN_WP - IN_W)))
    W_out = all_gather(w_out.astype(BF16), "comm_gather_w_out").transpose(1, 0, 2, 3).reshape(L, SB_W + DN_W, D)
    W_1 = all_gather(w_ff1.astype(BF16), "comm_gather_w_ff1").transpose(1, 2, 0, 3).reshape(L, D, FF)
    W_2 = all_gather(w_ff2.astype(BF16), "comm_gather_w_ff2").transpose(1, 0, 2, 3).reshape(L, FF, D)
    conv_full = all_gather(conv_w, "comm_gather_conv").transpose(1, 2, 0, 3).reshape(L, CONV_K, 3 * DN_W)

    c_all = all_gather(c, "comm_gather_c").reshape(N_DEV * B, D)
    b_sl = lax.dynamic_slice_in_dim(b_ada, me * WA, WA, axis=1).reshape(L, 1, WA)
    mod_sh = ada_fwd(c_all, w_ada, b_sl, "ada_fwd")
    mod_g = all_gather(mod_sh, "comm_gather_mod")
    mod = lax.dynamic_slice_in_dim(mod_g, me * B, B, axis=2).transpose(1, 2, 0, 3).reshape(L, B, 6 * D)

    def mod_part(l, i):
        return mod[l, :, i * D:(i + 1) * D].reshape(B, 1, D)

    alog_row = _pad_lanes(a_log).reshape(L, 1, 128)
    dtb_row = _pad_lanes(dt_bias).reshape(L, 1, 128)

    def gate_epi(acc, xv, g):
        return acc, xv + g[0] * acc

    def relu2(a):
        r = jnp.maximum(a, 0.0)
        return r * r

    def times_gate(a, g):
        return a * g[0]

    tile_ij = lambda i, j, k: (i, j)

    saved = []
    xc = x
    for l in range(L):
        sh_a, sc_a, g_a, sh_m, sc_m, g_m = [mod_part(l, i) for i in range(6)]
        h = ln_mod_fwd(xc, norm_mix[l:l + 1], sc_a, sh_a, "ln_mod_fwd")
        proj3 = matmul(h.reshape(N, D), W_in[l], mode="nn", name="mm_proj", tm=256)[0].reshape(B, T, IN_WP)
        o_a, tot = sb_attn_fwd(proj3, sb_q_norm[l:l + 1], sb_k_norm[l:l + 1], "sb_attn_fwd")
        qkv = gdn_pre_fwd(proj3, conv_full[l], "gdn_pre_fwd")
        gates = gdn_gates_fwd(proj3, alog_row[l], dtb_row[l], "gdn_gates_fwd")
        ob, states = gdn_chunk_fwd(qkv, gates, "gdn_chunk_fwd")
        o_b = gdn_post_fwd(ob, proj3, dn_out_norm[l:l + 1], "gdn_post_fwd")
        mix = jnp.concatenate([o_a, o_b], axis=-1)
        y1, x_mid = matmul(
            mix.reshape(N, SB_W + DN_W), W_out[l], mode="nn", name="mm_out", out_dtypes=(F32, F32), tm=tm, epi=gate_epi,
            extras=[(xc.reshape(N, D), (tm, _pick(D, 1024)), tile_ij),
                    (g_a, (1, 1, _pick(D, 1024)), lambda i, j, k: (i * tm // T, 0, j))])
        x_mid = x_mid.reshape(B, T, D)
        h2 = ln_mod_fwd(x_mid, norm_mlp[l:l + 1], sc_m, sh_m, "ln_mod_fwd")
        u = matmul(h2.reshape(N, D), W_1[l], mode="nn", name="mm_ff1")[0]
        y2, x_out = matmul(
            u, W_2[l], mode="nn", name="mm_ff2", out_dtypes=(F32, F32), tm=tm, a_fn=relu2, epi=gate_epi,
            extras=[(x_mid.reshape(N, D), (tm, _pick(D, 1024)), tile_ij),
                    (g_m, (1, 1, _pick(D, 1024)), lambda i, j, k: (i * tm // T, 0, j))])
        saved.append(dict(x=xc, h=h, proj3=proj3, tot=tot, qkv=qkv, gates=gates, states=states, ob=ob, mix=mix,
                          y1=y1, x_mid=x_mid, h2=h2, u=u, y2=y2))
        xc = x_out.reshape(B, T, D)

    dx, sq = loss_grad(xc, loss_target, "loss_grad")
    loss = lax.psum((0.5 / D) * jnp.sum(sq), AXES)

    g_win, g_wout, g_w1, g_w2, dmods, smalls = [], [], [], [], [], []
    tk_tok = tm
    for l in reversed(range(L)):
        s = saved[l]
        sh_a, sc_a, g_a, sh_m, sc_m, g_m = [mod_part(l, i) for i in range(6)]
        gate_k = lambda g, blk: (g, (1, 1, blk), lambda i, j, k: (i * tm // T, 0, k))
        gate_tok = lambda g, blk: (g, (1, 1, blk), lambda i, j, k: (k * tk_tok // T, 0, j))
        dx2 = dx.reshape(N, D)
        dg_m = rowsum_prod(dx, s["y2"].reshape(B, T, D), "rowsum_prod")
        du = matmul(dx2, W_2[l], mode="nt", name="mm_ff2_da", out_dtypes=(BF16,), tm=tm, a_fn=times_gate,
                    a_extras=[gate_k(g_m, _pick(D, 1024))],
                    epi=lambda acc, uv: (acc * (2.0 * jnp.maximum(uv, 0.0)),),
                    extras=[(s["u"], (tm, _pick(FF, 1024)), tile_ij)])[0]
        g_w2.append(matmul(s["u"], dx2, mode="tn", name="mm_ff2_dw", out_dtypes=(BF16,), tk=tk_tok, a_fn=relu2,
                           b_fn=times_gate, b_extras=[gate_tok(g_m, _pick(D, 1024))])[0])
        g_w1.append(matmul(s["h2"].reshape(N, D), du, mode="tn", name="mm_ff1_dw", out_dtypes=(BF16,))[0])
        dh2 = matmul(du, W_1[l], mode="nt", name="mm_ff1_da")[0]
        dx_mid, dgn_mlp, dsc_m, dsh_m = ln_mod_bwd(s["x_mid"], norm_mlp[l:l + 1], sc_m, dh2.reshape(B, T, D), dx, "ln_mod_bwd")
        dxm2 = dx_mid.reshape(N, D)
        dg_a = rowsum_prod(dx_mid, s["y1"].reshape(B, T, D), "rowsum_prod")
        dmix3 = matmul(dxm2, W_out[l], mode="nt", name="mm_out_da", tm=tm, a_fn=times_gate,
                       a_extras=[gate_k(g_a, _pick(D, 1024))])[0].reshape(B, T, SB_W + DN_W)
        g_wout.append(matmul(s["mix"].reshape(N, SB_W + DN_W), dxm2, mode="tn", name="mm_out_dw", out_dtypes=(BF16,),
                             tk=tk_tok, b_fn=times_gate, b_extras=[gate_tok(g_a, _pick(D, 1024))])[0])
        dq_a, dk_a, dv_a, dgq, dgk = sb_attn_bwd(s["proj3"], sb_q_norm[l:l + 1], sb_k_norm[l:l + 1], s["tot"], dmix3, "sb_attn_bwd")
        dob, dz, dgn_dn = gdn_post_bwd(s["ob"], s["proj3"], dn_out_norm[l:l + 1], dmix3, "gdn_post_bwd")
        dqkv, dgates = gdn_chunk_bwd(s["qkv"], s["gates"], s["states"], dob, "gdn_chunk_bwd")
        d_dnqkv, dconv_b = gdn_pre_bwd(s["proj3"], conv_full[l], dqkv, "gdn_pre_bwd")
        d_ab, dalog, ddtb = gdn_gates_bwd(s["proj3"], alog_row[l], dtb_row[l], dgates, "gdn_gates_bwd")
        dproj = jnp.concatenate([dq_a, dk_a, dv_a, d_dnqkv, dz, d_ab], axis=-1).reshape(N, IN_WP)
        g_win.append(matmul(s["h"].reshape(N, D), dproj, mode="tn", name="mm_proj_dw", out_dtypes=(BF16,), tm=512, tk=512)[0])
        dh = matmul(dproj, W_in[l], mode="nt", name="mm_proj_da", tm=512, tk=IN_WP)[0]
        dx, dgn_mix, dsc_a, dsh_a = ln_mod_bwd(s["x"], norm_mix[l:l + 1], sc_a, dh.reshape(B, T, D), dx_mid, "ln_mod_bwd")
        dmods.append(jnp.concatenate([dsh_a, dsc_a, dg_a, dsh_m, dsc_m, dg_m], axis=-1).reshape(B, 6 * D))
        smalls.append(dict(norm_mix=dgn_mix, norm_mlp=dgn_mlp, sbq=dgq, sbk=dgk, alog=dalog, dtb=ddtb, dnorm=dgn_dn,
                           conv=jnp.sum(dconv_b, axis=0)))
    for lst in (g_win, g_wout, g_w1, g_w2, dmods, smalls):
        lst.reverse()
    grad_x = dx

    def shard_cols(g, n):
        return g.reshape(L, g.shape[1], N_DEV, n).transpose(2, 0, 1, 3)

    def shard_rows(g, n):
        return g.reshape(L, N_DEV, n, g.shape[2]).transpose(1, 0, 2, 3)

    def update(parts, w, m, v, name):
        shp = w.shape
        r2 = lambda a: a.reshape(-1, shp[-1])
        outs = adamw(parts.reshape(parts.shape[0], -1, shp[-1]), r2(w), r2(m), r2(v), name)
        return [o.reshape(shp) for o in outs]

    wi = w_in.shape[2]
    p_win = all_to_all(shard_cols(jnp.stack(g_win)[:, :, :IN_W], wi), "comm_scatter_w_in")
    p_wout = all_to_all(shard_rows(jnp.stack(g_wout), w_out.shape[1]), "comm_scatter_w_out")
    p_w1 = all_to_all(shard_cols(jnp.stack(g_w1), w_ff1.shape[2]), "comm_scatter_w_ff1")
    p_w2 = all_to_all(shard_rows(jnp.stack(g_w2), w_ff2.shape[1]), "comm_scatter_w_ff2")
    r_win = update(p_win, w_in, m_w_in, v_w_in, "adamw_w_in")
    r_wout = update(p_wout, w_out, m_w_out, v_w_out, "adamw_w_out")
    r_w1 = update(p_w1, w_ff1, m_w_ff1, v_w_ff1, "adamw_w_ff1")
    r_w2 = update(p_w2, w_ff2, m_w_ff2, v_w_ff2, "adamw_w_ff2")

    dmod_g = all_gather(jnp.stack(dmods), "comm_gather_dmod")
    dmod_all = dmod_g.transpose(1, 0, 2, 3).reshape(L, N_DEV * B, 6 * D)
    g_wada = ada_bwd(c_all, lax.dynamic_slice_in_dim(dmod_all, me * WA, WA, axis=2), "ada_bwd")
    r_wada = update(g_wada[None], w_ada, m_w_ada, v_w_ada, "adamw_w_ada")
    r_bada = update(dmod_g.transpose(0, 2, 1, 3).reshape(N_DEV * B, L, 6 * D), b_ada, m_b_ada, v_b_ada, "adamw_b_ada")

    def pack(f):
        return jnp.concatenate([
            _rows128(f("norm_mix")), _rows128(f("norm_mlp")), _rows128(f("sbq")), _rows128(f("sbk")),
            f("alog"), f("dtb"), f("dnorm"), _rows128(f("conv"))], axis=0)

    names = ["norm_mix", "norm_mlp", "sbq", "sbk", "alog", "dtb", "dnorm"]
    part = pack(lambda n: jnp.concatenate([sm[n] for sm in smalls], axis=0))
    n_rep = part.shape[0] - L * CONV_K * 3 * DN_W // 128
    part_g = all_gather(part, "comm_gather_small")
    params = dict(norm_mix=(norm_mix, m_norm_mix, v_norm_mix), norm_mlp=(norm_mlp, m_norm_mlp, v_norm_mlp),
                  sbq=(sb_q_norm, m_sb_q_norm, v_sb_q_norm), sbk=(sb_k_norm, m_sb_k_norm, v_sb_k_norm),
                  alog=(a_log, m_a_log, v_a_log), dtb=(dt_bias, m_dt_bias, v_dt_bias),
                  dnorm=(dn_out_norm, m_dn_out_norm, v_dn_out_norm))

    def rows_of(n, a):
        return _pad_lanes(a) if n in ("alog", "dtb") else _rows128(a)

    packed = [jnp.concatenate([rows_of(n, params[n][i]) for n in names], axis=0) for i in range(3)]
    r_small = adamw(part_g[:, :n_rep], packed[0], packed[1], packed[2], "adamw_small")
    small_out = {}
    off = 0
    for n in names:
        w0 = params[n][0]
        nr = rows_of(n, w0).shape[0]
        vals = [o[off:off + nr] for o in r_small]
        small_out[n] = [(vv[:, :w0.shape[1]] if n in ("alog", "dtb") else vv.reshape(w0.shape)) for vv in vals]
        off += nr
    conv_parts = part_g[:, n_rep:].reshape(N_DEV, L, CONV_K, 3 * DN_W)
    r_conv = update(lax.dynamic_slice_in_dim(conv_parts, me * CS, CS, axis=3), conv_w, m_conv_w, v_conv_w, "adamw_conv")

    order = [r_wada, r_bada, small_out["norm_mix"], small_out["norm_mlp"], r_win, small_out["sbq"], small_out["sbk"],
             r_conv, small_out["alog"], small_out["dtb"], small_out["dnorm"], r_wout, r_w1, r_w2]
    outs = [loss, grad_x]
    for i in range(4):
        outs += [r[i] for r in order]
    return tuple(outs)
```

```python
import functools
import math

import jax
import jax.numpy as jnp
from jax import lax
from jax.experimental import pallas as pl
from jax.experimental.pallas import tpu as pltpu

F32 = jnp.float32
BF16 = jnp.bfloat16
EPS = 1e-6
N_DEV = 8
AXES = ("x", "y", "c")

SB_HEADS, SB_HD = 8, 64
SB_W = SB_HEADS * SB_HD
SB_BLK = 128
DN_HEADS, DN_HD = 4, 128
DN_W = DN_HEADS * DN_HD
DN_C = 64
CONV_K = 4
IN_W = 3 * SB_W + 4 * DN_W + 2 * DN_HEADS
IN_WP = 3 * SB_W + 4 * DN_W + 128
COL_DNQKV = 3 * SB_W // 128
COL_Z = COL_DNQKV + 3 * DN_W // 128
COL_AB = COL_Z + DN_W // 128

ADAM_LR, ADAM_B1, ADAM_B2, ADAM_EPS, ADAM_WD, ADAM_STEP = 0.001, 0.9, 0.999, 1e-08, 0.01, 10

VMEM_LIMIT = 56 * 1024 * 1024


def _cp(sem):
    return pltpu.CompilerParams(dimension_semantics=sem, vmem_limit_bytes=VMEM_LIMIT)


def _pick(dim, pref):
    return pref if dim % pref == 0 else dim


def _sds(shape, dtype):
    return jax.ShapeDtypeStruct(tuple(shape), dtype)


_NN = (((1,), (0,)), ((), ()))
_NT = (((1,), (1,)), ((), ()))
_TN = (((0,), (0,)), ((), ()))


def _bdot(a, b, dims=_NN):
    return lax.dot_general(a.astype(BF16), b.astype(BF16), dims, preferred_element_type=F32)


def _split(a):
    hi = a.astype(BF16)
    lo = (a - hi.astype(F32)).astype(BF16)
    return hi, lo


def _pdot(a, b, dims=_NN):
    ah, al = _split(a)
    bh, bl = _split(b)
    d = functools.partial(lax.dot_general, dimension_numbers=dims, preferred_element_type=F32)
    return d(ah, bh) + (d(ah, bl) + d(al, bh))


def _sigmoid(x):
    return 1.0 / (1.0 + jnp.exp(-x))


def _softplus(x):
    return jnp.maximum(x, 0.0) + jnp.log(1.0 + jnp.exp(-jnp.abs(x)))


def matmul(a, b, *, mode, name, out_dtypes=(F32,), a_fn=None, a_extras=(), b_fn=None, b_extras=(),
           epi=None, extras=(), tm=1024, tn=1024, tk=1024):
    if mode == "tn":
        K, M = a.shape
    else:
        M, K = a.shape
    N = b.shape[0] if mode == "nt" else b.shape[1]
    tm, tn, tk = _pick(M, tm), _pick(N, tn), _pick(K, tk)
    nk = K // tk
    dims = {"nn": _NN, "nt": _NT, "tn": _TN}[mode]
    a_spec = pl.BlockSpec((tk, tm), lambda i, j, k: (k, i)) if mode == "tn" else pl.BlockSpec((tm, tk), lambda i, j, k: (i, k))
    b_spec = pl.BlockSpec((tn, tk), lambda i, j, k: (j, k)) if mode == "nt" else pl.BlockSpec((tk, tn), lambda i, j, k: (k, j))
    na, nb, ne, no = len(a_extras), len(b_extras), len(extras), len(out_dtypes)

    def body(*refs):
        a_ref, b_ref = refs[0], refs[1]
        ax = refs[2:2 + na]
        bx = refs[2 + na:2 + na + nb]
        ex = refs[2 + na + nb:2 + na + nb + ne]
        outs = refs[2 + na + nb + ne:2 + na + nb + ne + no]
        acc_ref = refs[-1]
        k = pl.program_id(2)

        @pl.when(k == 0)
        def _():
            acc_ref[...] = jnp.zeros_like(acc_ref)

        av = a_ref[...]
        if a_fn is not None:
            av = a_fn(av, *[r[...] for r in ax])
        bv = b_ref[...]
        if b_fn is not None:
            bv = b_fn(bv, *[r[...] for r in bx])
        acc_ref[...] += lax.dot_general(av.astype(BF16), bv.astype(BF16), dims, preferred_element_type=F32)

        @pl.when(k == nk - 1)
        def _():
            res = acc_ref[...]
            res = (res,) if epi is None else epi(res, *[r[...] for r in ex])
            for o_ref, r in zip(outs, res):
                o_ref[...] = r.astype(o_ref.dtype)

    xs = list(a_extras) + list(b_extras) + list(extras)
    return pl.pallas_call(
        body,
        grid=(M // tm, N // tn, nk),
        in_specs=[a_spec, b_spec] + [pl.BlockSpec(bs, im) for _, bs, im in xs],
        out_specs=[pl.BlockSpec((tm, tn), lambda i, j, k: (i, j)) for _ in out_dtypes],
        out_shape=[_sds((M, N), dt) for dt in out_dtypes],
        scratch_shapes=[pltpu.VMEM((tm, tn), F32)],
        compiler_params=_cp(("parallel", "parallel", "arbitrary")),
        name=name,
    )(a, b, *[x for x, _, _ in xs])


def ln_mod_fwd(x, gain, sc, sh, name):
    B, T, D = x.shape
    tt = _pick(T, 512)

    def body(x_ref, g_ref, sc_ref, sh_ref, h_ref):
        xv = x_ref[0]
        r = lax.rsqrt(jnp.mean(xv * xv, axis=-1, keepdims=True) + EPS)
        h = (xv * r * g_ref[...]) * (1.0 + sc_ref[0]) + sh_ref[0]
        h_ref[0] = h.astype(h_ref.dtype)

    return pl.pallas_call(
        body, grid=(B, T // tt),
        in_specs=[pl.BlockSpec((1, tt, D), lambda b, t: (b, t, 0)), pl.BlockSpec((1, D), lambda b, t: (0, 0)),
                  pl.BlockSpec((1, 1, D), lambda b, t: (b, 0, 0)), pl.BlockSpec((1, 1, D), lambda b, t: (b, 0, 0))],
        out_specs=pl.BlockSpec((1, tt, D), lambda b, t: (b, t, 0)),
        out_shape=_sds((B, T, D), BF16),
        compiler_params=_cp(("parallel", "parallel")), name=name,
    )(x, gain, sc, sh)


def ln_mod_bwd(x, gain, sc, dh, dres, name):
    B, T, D = x.shape
    tt = _pick(T, 512)

    def body(x_ref, g_ref, sc_ref, dh_ref, dres_ref, dx_ref, dg_ref, dsc_ref, dsh_ref):
        b, t = pl.program_id(0), pl.program_id(1)
        xv, dhv = x_ref[0], dh_ref[0]
        g, s = g_ref[...], sc_ref[0]
        r = lax.rsqrt(jnp.mean(xv * xv, axis=-1, keepdims=True) + EPS)
        xn = xv * r
        dxn = dhv * (g * (1.0 + s))
        dx_ref[0] = dres_ref[0] + r * (dxn - xn * jnp.mean(dxn * xn, axis=-1, keepdims=True))
        s1 = jnp.sum(dhv * xn, axis=0, keepdims=True)
        s2 = jnp.sum(dhv, axis=0, keepdims=True)

        @pl.when(t == 0)
        def _():
            dsc_ref[0] = jnp.zeros_like(s1)
            dsh_ref[0] = jnp.zeros_like(s1)

        @pl.when((t == 0) & (b == 0))
        def _():
            dg_ref[...] = jnp.zeros_like(s1)

        dsc_ref[0] += s1 * g
        dsh_ref[0] += s2
        dg_ref[...] += s1 * (1.0 + s)

    tile = pl.BlockSpec((1, tt, D), lambda b, t: (b, t, 0))
    row = pl.BlockSpec((1, D), lambda b, t: (0, 0))
    brow = pl.BlockSpec((1, 1, D), lambda b, t: (b, 0, 0))
    return pl.pallas_call(
        body, grid=(B, T // tt),
        in_specs=[tile, row, brow, tile, tile],
        out_specs=[tile, row, brow, brow],
        out_shape=[_sds((B, T, D), F32), _sds((1, D), F32), _sds((B, 1, D), F32), _sds((B, 1, D), F32)],
        compiler_params=_cp(("arbitrary", "arbitrary")), name=name,
    )(x, gain, sc, dh, dres)


def rowsum_prod(a, b, name):
    B, T, D = a.shape
    tt = _pick(T, 512)

    def body(a_ref, b_ref, o_ref):
        @pl.when(pl.program_id(1) == 0)
        def _():
            o_ref[...] = jnp.zeros_like(o_ref)

        o_ref[0] += jnp.sum(a_ref[0] * b_ref[0], axis=0, keepdims=True)

    tile = pl.BlockSpec((1, tt, D), lambda b, t: (b, t, 0))
    return pl.pallas_call(
        body, grid=(B, T // tt), in_specs=[tile, tile],
        out_specs=pl.BlockSpec((1, 1, D), lambda b, t: (b, 0, 0)),
        out_shape=_sds((B, 1, D), F32),
        compiler_params=_cp(("parallel", "arbitrary")), name=name,
    )(a, b)


def loss_grad(y, tgt, name):
    B, T, D = y.shape
    tt = _pick(T, 512)

    def body(y_ref, t_ref, dy_ref, s_ref):
        @pl.when((pl.program_id(0) == 0) & (pl.program_id(1) == 0))
        def _():
            s_ref[...] = jnp.zeros_like(s_ref)

        e = y_ref[0] - t_ref[0]
        dy_ref[0] = e * (1.0 / D)
        s_ref[...] += jnp.sum(e * e, axis=0, keepdims=True)

    tile = pl.BlockSpec((1, tt, D), lambda b, t: (b, t, 0))
    return pl.pallas_call(
        body, grid=(B, T // tt), in_specs=[tile, tile],
        out_specs=[tile, pl.BlockSpec((1, D), lambda b, t: (0, 0))],
        out_shape=[_sds((B, T, D), F32), _sds((1, D), F32)],
        compiler_params=_cp(("arbitrary", "arbitrary")), name=name,
    )(y, tgt)


def _sb_group(nb):
    return 4 if nb % 4 == 0 else (2 if nb % 2 == 0 else 1)


def _tri_sum(x, tri2):
    hi, lo = _split(x)
    return lax.dot_general(jnp.concatenate([hi, lo], axis=1), tri2, _NN, preferred_element_type=F32)


def _tri2(cond):
    t = cond.astype(BF16)
    return jnp.concatenate([t, t], axis=0)


def sb_attn_fwd(proj3, gq, gk, name):
    B, T, _ = proj3.shape
    NB = T // SB_BLK
    G = _sb_group(NB)
    KW = G * SB_BLK
    scale = SB_HD ** -0.5

    def body(q_ref, k_ref, v_ref, gq_ref, gk_ref, o_ref, tot_ref, qn_s, kn_s, v_s):
        row_io = lax.broadcasted_iota(jnp.int32, (SB_BLK, SB_BLK), 0)
        col_io = lax.broadcasted_iota(jnp.int32, (SB_BLK, SB_BLK), 1)
        tri = _tri2(row_io > col_io)

        def prep(i, _):
            rows = pl.ds(pl.multiple_of(i * SB_BLK, SB_BLK), SB_BLK)
            for hh in range(2):
                sl = slice(hh * SB_HD, (hh + 1) * SB_HD)
                q = q_ref[0, rows, sl]
                k = k_ref[0, rows, sl]
                qn_s[hh, rows, :] = (q * lax.rsqrt(jnp.mean(q * q, -1, keepdims=True) + EPS) * (gq_ref[...] * scale)).astype(BF16)
                kn_s[hh, rows, :] = (k * lax.rsqrt(jnp.mean(k * k, -1, keepdims=True) + EPS) * gk_ref[...]).astype(BF16)
                v_s[hh, rows, :] = v_ref[0, rows, sl].astype(BF16)
            return 0

        lax.fori_loop(0, NB, prep, 0)

        diff_w = (lax.broadcasted_iota(jnp.int32, (SB_BLK, KW), 1)
                  - lax.broadcasted_iota(jnp.int32, (SB_BLK, KW), 0))

        def qblock(i, _):
            rows = pl.ds(pl.multiple_of(i * SB_BLK, SB_BLK), SB_BLK)
            qn = [qn_s[hh, rows, :] for hh in range(2)]
            nsj = i // G + 1

            def sblock(sj, carry, masked):
                cols = pl.ds(pl.multiple_of(sj * KW, KW), KW)
                mask = diff_w < (i % G) * SB_BLK
                new = []
                for hh in range(2):
                    acc, cs = carry[hh]
                    z = lax.dot_general(qn[hh], kn_s[hh, cols, :], _NT, preferred_element_type=F32)
                    sp = _softplus(z)
                    lg = jnp.where(mask, -sp, 0.0) if masked else -sp
                    ls = z - sp
                    ps = []
                    for s in reversed(range(G)):
                        lgs = lg[:, s * SB_BLK:(s + 1) * SB_BLK]
                        ts = _tri_sum(lgs, tri)
                        ps.append(ls[:, s * SB_BLK:(s + 1) * SB_BLK] + ts + cs)
                        cs = cs + (ts[:, :1] + lgs[:, :1])
                    p = ps[0] if G == 1 else jnp.concatenate(ps[::-1], axis=1)
                    att = jnp.exp(p)
                    if masked:
                        att = jnp.where(mask, att, 0.0)
                    acc = acc + lax.dot_general(att.astype(BF16), v_s[hh, cols, :], _NN, preferred_element_type=F32)
                    new.append((acc, cs))
                return tuple(new)

            init = (jnp.zeros((SB_BLK, SB_HD), F32), jnp.zeros((SB_BLK, 1), F32))
            res = sblock(nsj - 1, (init, init), True)
            res = lax.fori_loop(0, nsj - 1, lambda jj, c: sblock(nsj - 2 - jj, c, False), res)
            for hh in range(2):
                o_ref[0, rows, hh * SB_HD:(hh + 1) * SB_HD] = res[hh][0].astype(o_ref.dtype)
                tot_ref[0, hh, rows, :] = res[hh][1]
            return 0

        lax.fori_loop(0, NB, qblock, 0)

    blk = lambda off: pl.BlockSpec((1, T, 128), lambda b, p: (b, 0, off + p))
    grow = pl.BlockSpec((1, SB_HD), lambda b, p: (0, 0))
    return pl.pallas_call(
        body, grid=(B, SB_W // 128),
        in_specs=[blk(0), blk(SB_W // 128), blk(2 * SB_W // 128), grow, grow],
        out_specs=[pl.BlockSpec((1, T, 128), lambda b, p: (b, 0, p)), pl.BlockSpec((1, 2, T, 1), lambda b, p: (b, p, 0, 0))],
        out_shape=[_sds((B, T, SB_W), BF16), _sds((B, SB_HEADS, T, 1), F32)],
        scratch_shapes=[pltpu.VMEM((2, T, SB_HD), BF16)] * 3,
        compiler_params=_cp(("parallel", "parallel")), name=name,
    )(proj3, proj3, proj3, gq, gk)


def sb_attn_bwd(proj3, gq, gk, tot, dmix3, name):
    B, T, _ = proj3.shape
    NB = T // SB_BLK
    G = _sb_group(NB)
    KW = G * SB_BLK
    scale = SB_HD ** -0.5

    def body(q_ref, k_ref, v_ref, gq_ref, gk_ref, tot_ref, do_ref, dq_ref, dk_ref, dv_ref, dgq_ref, dgk_ref,
             qn_s, kn_s, v_s, do_s, dqn_s, dkn_s, dv_s):
        row_io = lax.broadcasted_iota(jnp.int32, (SB_BLK, SB_BLK), 0)
        col_io = lax.broadcasted_iota(jnp.int32, (SB_BLK, SB_BLK), 1)
        tri = _tri2(row_io > col_io)
        trip = _tri2(row_io < col_io)

        @pl.when((pl.program_id(0) == 0) & (pl.program_id(1) == 0))
        def _():
            dgq_ref[...] = jnp.zeros_like(dgq_ref)
            dgk_ref[...] = jnp.zeros_like(dgk_ref)

        def prep(i, _):
            rows = pl.ds(pl.multiple_of(i * SB_BLK, SB_BLK), SB_BLK)
            for hh in range(2):
                sl = slice(hh * SB_HD, (hh + 1) * SB_HD)
                q = q_ref[0, rows, sl]
                k = k_ref[0, rows, sl]
                qn_s[hh, rows, :] = (q * lax.rsqrt(jnp.mean(q * q, -1, keepdims=True) + EPS) * (gq_ref[...] * scale)).astype(BF16)
                kn_s[hh, rows, :] = (k * lax.rsqrt(jnp.mean(k * k, -1, keepdims=True) + EPS) * gk_ref[...]).astype(BF16)
                v_s[hh, rows, :] = v_ref[0, rows, sl].astype(BF16)
                do_s[hh, rows, :] = do_ref[0, rows, sl].astype(BF16)
            return 0

        lax.fori_loop(0, NB, prep, 0)
        dkn_s[...] = jnp.zeros_like(dkn_s)
        dv_s[...] = jnp.zeros_like(dv_s)

        diff_w = (lax.broadcasted_iota(jnp.int32, (SB_BLK, KW), 1)
                  - lax.broadcasted_iota(jnp.int32, (SB_BLK, KW), 0))

        def qblock(i, _):
            rows = pl.ds(pl.multiple_of(i * SB_BLK, SB_BLK), SB_BLK)
            qn = [qn_s[hh, rows, :] for hh in range(2)]
            dov = [do_s[hh, rows, :] for hh in range(2)]
            tot = [tot_ref[0, hh, rows, :] for hh in range(2)]

            def sblock(sj, carry, masked):
                cols = pl.ds(pl.multiple_of(sj * KW, KW), KW)
                mask = diff_w < (i % G) * SB_BLK
                new = []
                for hh in range(2):
                    dq, cum, cdp = carry[hh]
                    kn = kn_s[hh, cols, :]
                    z = lax.dot_general(qn[hh], kn, _NT, preferred_element_type=F32)
                    sp = _softplus(z)
                    lg = jnp.where(mask, -sp, 0.0) if masked else -sp
                    ls = z - sp
                    ps = []
                    for s in range(G):
                        lgs = lg[:, s * SB_BLK:(s + 1) * SB_BLK]
                        ts = _tri_sum(lgs, tri)
                        cum = cum + (ts[:, :1] + lgs[:, :1])
                        ps.append(ls[:, s * SB_BLK:(s + 1) * SB_BLK] + ts + (tot[hh] - cum))
                    p = ps[0] if G == 1 else jnp.concatenate(ps, axis=1)
                    att = jnp.exp(p)
                    if masked:
                        att = jnp.where(mask, att, 0.0)
                    dp = att * lax.dot_general(dov[hh], v_s[hh, cols, :], _NT, preferred_element_type=F32)
                    dls = []
                    for s in range(G):
                        dps = dp[:, s * SB_BLK:(s + 1) * SB_BLK]
                        tp = _tri_sum(dps, trip)
                        dls.append(tp + cdp)
                        cdp = cdp + (tp[:, SB_BLK - 1:] + dps[:, SB_BLK - 1:])
                    dlg = dls[0] if G == 1 else jnp.concatenate(dls, axis=1)
                    dz = dp - jnp.exp(ls) * (dp + dlg)
                    if masked:
                        dz = jnp.where(mask, dz, 0.0)
                    dz = dz.astype(BF16)
                    dq = dq + lax.dot_general(dz, kn, _NN, preferred_element_type=F32)
                    dkn_s[hh, cols, :] += lax.dot_general(dz, qn[hh], _TN, preferred_element_type=F32)
                    dv_s[hh, cols, :] += lax.dot_general(att.astype(BF16), dov[hh], _TN, preferred_element_type=F32)
                    new.append((dq, cum, cdp))
                return tuple(new)

            z1 = jnp.zeros((SB_BLK, 1), F32)
            init = (jnp.zeros((SB_BLK, SB_HD), F32), z1, z1)
            nsj = i // G + 1
            res = lax.fori_loop(0, nsj - 1, lambda sj, c: sblock(sj, c, False), (init, init))
            res = sblock(nsj - 1, res, True)
            for hh in range(2):
                dqn_s[hh, rows, :] = res[hh][0]
            return 0

        lax.fori_loop(0, NB, qblock, 0)

        def fin(i, carry):
            aq, ak = carry
            rows = pl.ds(pl.multiple_of(i * SB_BLK, SB_BLK), SB_BLK)
            for hh in range(2):
                sl = slice(hh * SB_HD, (hh + 1) * SB_HD)
                for src_ref, d_s, g_ref, out_ref, mult, which in ((q_ref, dqn_s, gq_ref, dq_ref, scale, 0), (k_ref, dkn_s, gk_ref, dk_ref, 1.0, 1)):
                    xr = src_ref[0, rows, sl]
                    r = lax.rsqrt(jnp.mean(xr * xr, -1, keepdims=True) + EPS)
                    dy = d_s[hh, rows, :] * mult
                    u = dy * g_ref[...]
                    out_ref[0, rows, sl] = r * u - xr * (r * r * r) * jnp.mean(u * xr, -1, keepdims=True)
                    part = jnp.sum(dy * xr * r, axis=0, keepdims=True)
                    if which == 0:
                        aq = aq + part
                    else:
                        ak = ak + part
                dv_ref[0, rows, sl] = dv_s[hh, rows, :]
            return aq, ak

        z64 = jnp.zeros((1, SB_HD), F32)
        aq, ak = lax.fori_loop(0, NB, fin, (z64, z64))
        dgq_ref[...] += aq
        dgk_ref[...] += ak

    blk = lambda off: pl.BlockSpec((1, T, 128), lambda b, p: (b, 0, off + p))
    grow = pl.BlockSpec((1, SB_HD), lambda b, p: (0, 0))
    return pl.pallas_call(
        body, grid=(B, SB_W // 128),
        in_specs=[blk(0), blk(SB_W // 128), blk(2 * SB_W // 128), grow, grow,
                  pl.BlockSpec((1, 2, T, 1), lambda b, p: (b, p, 0, 0)), blk(0)],
        out_specs=[blk(0), blk(0), blk(0), grow, grow],
        out_shape=[_sds((B, T, SB_W), F32)] * 3 + [_sds((1, SB_HD), F32)] * 2,
        scratch_shapes=[pltpu.VMEM((2, T, SB_HD), BF16)] * 4 + [pltpu.VMEM((2, T, SB_HD), F32)] * 3,
        compiler_params=_cp(("arbitrary", "arbitrary")), name=name,
    )(proj3, proj3, proj3, gq, gk, tot, dmix3)


def _conv_silu(x, w, T):
    t_io = lax.broadcasted_iota(jnp.int32, x.shape, 0)
    xs = [x] + [jnp.where(t_io >= s, pltpu.roll(x, s, 0), 0.0) for s in range(1, CONV_K)]
    y = xs[0] * w[CONV_K - 1:CONV_K, :]
    for s in range(1, CONV_K):
        y = y + xs[s] * w[CONV_K - 1 - s:CONV_K - s, :]
    return y, y * _sigmoid(y), xs


def gdn_pre_fwd(proj3, conv_w, name):
    B, T, _ = proj3.shape
    qs = DN_HD ** -0.5

    def body(x_ref, w_ref, o_ref):
        kind = pl.program_id(1) // DN_HEADS
        _, s, _ = _conv_silu(x_ref[0], w_ref[...], T)
        n = lax.rsqrt(jnp.sum(s * s, axis=-1, keepdims=True) + EPS)
        c = jnp.where(kind == 0, qs, 1.0)
        o_ref[0, 0] = jnp.where(kind < 2, s * (n * c), s)

    return pl.pallas_call(
        body, grid=(B, 3 * DN_HEADS),
        in_specs=[pl.BlockSpec((1, T, 128), lambda b, j: (b, 0, COL_DNQKV + j)), pl.BlockSpec((CONV_K, 128), lambda b, j: (0, j))],
        out_specs=pl.BlockSpec((1, 1, T, 128), lambda b, j: (b, j // DN_HEADS, 0, j % DN_HEADS)),
        out_shape=_sds((B, 3, T, DN_W), F32),
        compiler_params=_cp(("parallel", "parallel")), name=name,
    )(proj3, conv_w)


def gdn_pre_bwd(proj3, conv_w, dqkv, name):
    B, T, _ = proj3.shape
    qs = DN_HD ** -0.5

    def body(x_ref, w_ref, d_ref, dx_ref, dw_ref):
        kind = pl.program_id(1) // DN_HEADS
        w = w_ref[...]
        y, s, xs = _conv_silu(x_ref[0], w, T)
        dout = d_ref[0, 0]
        n = lax.rsqrt(jnp.sum(s * s, axis=-1, keepdims=True) + EPS)
        c = jnp.where(kind == 0, qs, 1.0)
        dsn = c * (n * dout - s * (n * n * n) * jnp.sum(dout * s, axis=-1, keepdims=True))
        ds = jnp.where(kind < 2, dsn, dout)
        sg = _sigmoid(y)
        dy = ds * (sg * (1.0 + y * (1.0 - sg)))
        t_io = lax.broadcasted_iota(jnp.int32, dy.shape, 0)
        dx = dy * w[CONV_K - 1:CONV_K, :]
        dw_ref[0, CONV_K - 1:CONV_K, :] = jnp.sum(dy * xs[0], axis=0, keepdims=True)
        for sft in range(1, CONV_K):
            dx = dx + jnp.where(t_io < T - sft, pltpu.roll(dy, T - sft, 0), 0.0) * w[CONV_K - 1 - sft:CONV_K - sft, :]
            dw_ref[0, CONV_K - 1 - sft:CONV_K - sft, :] = jnp.sum(dy * xs[sft], axis=0, keepdims=True)
        dx_ref[0] = dx

    return pl.pallas_call(
        body, grid=(B, 3 * DN_HEADS),
        in_specs=[pl.BlockSpec((1, T, 128), lambda b, j: (b, 0, COL_DNQKV + j)), pl.BlockSpec((CONV_K, 128), lambda b, j: (0, j)),
                  pl.BlockSpec((1, 1, T, 128), lambda b, j: (b, j // DN_HEADS, 0, j % DN_HEADS))],
        out_specs=[pl.BlockSpec((1, T, 128), lambda b, j: (b, 0, j)), pl.BlockSpec((1, CONV_K, 128), lambda b, j: (b, 0, j))],
        out_shape=[_sds((B, T, 3 * DN_W), F32), _sds((B, CONV_K, 3 * DN_W), F32)],
        compiler_params=_cp(("parallel", "parallel")), name=name,
    )(proj3, conv_w, dqkv)


def gdn_gates_fwd(proj3, alog_row, dtb_row, name):
    B, T, _ = proj3.shape

    def body(x_ref, al_ref, dt_ref, o_ref):
        x = x_ref[0]
        lane = lax.broadcasted_iota(jnp.int32, x.shape, 1)
        g = -jnp.exp(al_ref[...]) * _softplus(x + dt_ref[...])
        o_ref[0] = jnp.where(lane < DN_HEADS, g, jnp.where(lane < 2 * DN_HEADS, _sigmoid(x), 0.0))

    row = pl.BlockSpec((1, 128), lambda b: (0, 0))
    return pl.pallas_call(
        body, grid=(B,),
        in_specs=[pl.BlockSpec((1, T, 128), lambda b: (b, 0, COL_AB)), row, row],
        out_specs=pl.BlockSpec((1, T, 128), lambda b: (b, 0, 0)),
        out_shape=_sds((B, T, 128), F32),
        compiler_params=_cp(("parallel",)), name=name,
    )(proj3, alog_row, dtb_row)


def gdn_gates_bwd(proj3, alog_row, dtb_row, dgates, name):
    B, T, _ = proj3.shape

    def body(x_ref, al_ref, dt_ref, d_ref, dx_ref, dal_ref, ddt_ref):
        @pl.when(pl.program_id(0) == 0)
        def _():
            dal_ref[...] = jnp.zeros_like(dal_ref)
            ddt_ref[...] = jnp.zeros_like(ddt_ref)

        x, d = x_ref[0], d_ref[0]
        lane = lax.broadcasted_iota(jnp.int32, x.shape, 1)
        a = x + dt_ref[...]
        na = -jnp.exp(al_ref[...])
        da = jnp.where(lane < DN_HEADS, d * na * _sigmoid(a), 0.0)
        bt = _sigmoid(x)
        dx_ref[0] = da + jnp.where((lane >= DN_HEADS) & (lane < 2 * DN_HEADS), d * bt * (1.0 - bt), 0.0)
        dal_ref[...] += jnp.sum(jnp.where(lane < DN_HEADS, d * na * _softplus(a), 0.0), axis=0, keepdims=True)
        ddt_ref[...] += jnp.sum(da, axis=0, keepdims=True)

    row = pl.BlockSpec((1, 128), lambda b: (0, 0))
    tile = pl.BlockSpec((1, T, 128), lambda b: (b, 0, 0))
    return pl.pallas_call(
        body, grid=(B,),
        in_specs=[pl.BlockSpec((1, T, 128), lambda b: (b, 0, COL_AB)), row, row, tile],
        out_specs=[tile, row, row],
        out_shape=[_sds((B, T, 128), F32), _sds((1, 128), F32), _sds((1, 128), F32)],
        compiler_params=_cp(("arbitrary",)), name=name,
    )(proj3, alog_row, dtb_row, dgates)


def gdn_post_fwd(ob, proj3, gain, name):
    B, T, _ = ob.shape

    def body(o_ref, z_ref, g_ref, out_ref):
        o, z = o_ref[0], z_ref[0]
        r = lax.rsqrt(jnp.mean(o * o, axis=-1, keepdims=True) + EPS)
        out_ref[0] = ((o * r * g_ref[...]) * (z * _sigmoid(z))).astype(out_ref.dtype)

    tile = pl.BlockSpec((1, T, 128), lambda b, h: (b, 0, h))
    return pl.pallas_call(
        body, grid=(B, DN_HEADS),
        in_specs=[tile, pl.BlockSpec((1, T, 128), lambda b, h: (b, 0, COL_Z + h)), pl.BlockSpec((1, 128), lambda b, h: (0, 0))],
        out_specs=tile, out_shape=_sds((B, T, DN_W), BF16),
        compiler_params=_cp(("parallel", "parallel")), name=name,
    )(ob, proj3, gain)


def gdn_post_bwd(ob, proj3, gain, dmix3, name):
    B, T, _ = ob.shape

    def body(o_ref, z_ref, g_ref, d_ref, do_ref, dz_ref, dg_ref):
        @pl.when((pl.program_id(0) == 0) & (pl.program_id(1) == 0))
        def _():
            dg_ref[...] = jnp.zeros_like(dg_ref)

        o, z, d, g = o_ref[0], z_ref[0], d_ref[0], g_ref[...]
        r = lax.rsqrt(jnp.mean(o * o, axis=-1, keepdims=True) + EPS)
        sg = _sigmoid(z)
        dn = d * (z * sg)
        dz_ref[0] = d * (o * r * g) * (sg * (1.0 + z * (1.0 - sg)))
        dg_ref[...] += jnp.sum(dn * o * r, axis=0, keepdims=True)
        u = dn * g
        do_ref[0] = r * u - o * (r * r * r) * jnp.mean(u * o, axis=-1, keepdims=True)

    tile = pl.BlockSpec((1, T, 128), lambda b, h: (b, 0, h))
    row = pl.BlockSpec((1, 128), lambda b, h: (0, 0))
    return pl.pallas_call(
        body, grid=(B, DN_HEADS),
        in_specs=[tile, pl.BlockSpec((1, T, 128), lambda b, h: (b, 0, COL_Z + h)), row,
                  pl.BlockSpec((1, T, 128), lambda b, h: (b, 0, SB_W // 128 + h))],
        out_specs=[tile, tile, row],
        out_shape=[_sds((B, T, DN_W), F32), _sds((B, T, DN_W), F32), _sds((1, 128), F32)],
        compiler_params=_cp(("arbitrary", "arbitrary")), name=name,
    )(ob, proj3, gain, dmix3)


def _tri_inv(low, ri, ci):
    m = (ri == ci).astype(F32) - jnp.where(((ri >> 1) == (ci >> 1)) & (ri > ci), low, 0.0)
    s = 2
    while s < DN_C:
        sh = s.bit_length()
        off = ((ri >> sh) == (ci >> sh)) & ((ri & (2 * s - 1)) >= s) & ((ci & (2 * s - 1)) < s)
        m = m - _pdot(m, _pdot(jnp.where(off, low, 0.0), m, _BNN), _BNN)
        s *= 2
    return m


_BNN = (((2,), (1,)), ((0,), (0,)))
_BNT = (((2,), (2,)), ((0,), (0,)))
_BTN = (((1,), (1,)), ((0,), (0,)))
DN_G = 4


def _chunk_common(q, k, v, gt, h, tm=None):
    C = DN_C
    G = q.shape[0]
    ri = lax.broadcasted_iota(jnp.int32, (C, C), 0)
    ci = lax.broadcasted_iota(jnp.int32, (C, C), 1)
    lane = lax.broadcasted_iota(jnp.int32, (C, 128), 1)
    incl, strict = ri >= ci, ri > ci
    g = jnp.sum(jnp.where(lane == h, gt, 0.0), axis=2, keepdims=True)
    beta = jnp.sum(jnp.where(lane == h + DN_HEADS, gt, 0.0), axis=2, keepdims=True)
    ones = jnp.ones((G, C, 128), F32)
    inclf = jnp.broadcast_to(incl.astype(F32), (G, C, C))
    gc = _pdot(inclf, g * ones, _BNN)[:, :, :1]
    gcr = _pdot(jnp.ones((G, C, C), F32), jnp.where(ri == ci, gc, 0.0), _BNN)
    decay = jnp.where(incl, jnp.exp(jnp.where(incl, gc - gcr, 0.0)), 0.0)
    e = jnp.exp(gc)
    kb, vb = k * beta, v * beta
    kk = _bdot(kb, k, _BNT)
    if tm is None:
        tm = _tri_inv(jnp.where(strict, kk * decay, 0.0), ri, ci)
    kbe = kb * e
    u = _bdot(tm, vb, _BNN)
    w = _bdot(tm, kbe, _BNN)
    qk = _bdot(q, k, _BNT)
    intra = jnp.where(incl, qk * decay, 0.0)
    gl = gc[:, C - 1:C, :]
    el = jnp.exp(gl)
    r = jnp.exp(gl - gc)
    return dict(lane=lane, incl=incl, inclf=inclf, strict=strict, beta=beta, decay=decay, e=e,
                kb=kb, vb=vb, kk=kk, tm=tm, kbe=kbe, u=u, w=w, qk=qk, intra=intra, el=el, r=r, ones=ones)


def gdn_chunk_fwd(qkv, gates, name):
    B, _, T, _ = qkv.shape
    NC = T // DN_C

    G = DN_G if NC % DN_G == 0 else 1
    GC = G * DN_C

    def body(x_ref, gt_ref, o_ref, st_ref, tm_ref, s_s, u_s, w_s, qe_s, kr_s, in_s, el_s):
        h = pl.program_id(1)

        def group(gi, _):
            rows = pl.ds(pl.multiple_of(gi * GC, GC), GC)
            q, k, v = [x_ref[0, i, rows, :].reshape(G, DN_C, DN_HD) for i in range(3)]
            c = _chunk_common(q, k, v, gt_ref[0, rows, :].reshape(G, DN_C, 128), h)
            tm_ref[0, 0, rows, :] = c["tm"].reshape(GC, DN_C)
            u_s[rows, :] = c["u"].reshape(GC, DN_HD)
            w_s[rows, :] = c["w"].reshape(GC, DN_HD)
            qe_s[rows, :] = (q * c["e"]).reshape(GC, DN_HD)
            kr_s[rows, :] = (k * c["r"]).reshape(GC, DN_HD)
            in_s[rows, :] = c["intra"].reshape(GC, DN_C)
            el_s[pl.ds(gi * G, G), :, :] = c["el"] * jnp.ones((G, 1, 128), F32)
            return 0

        lax.fori_loop(0, NC // G, group, 0)
        s_s[...] = jnp.zeros_like(s_s)

        def chunk(n, _):
            rows = pl.ds(pl.multiple_of(n * DN_C, DN_C), DN_C)
            st = s_s[...]
            st_ref[0, 0, pl.ds(pl.multiple_of(n * DN_HD, DN_HD), DN_HD), :] = st
            v_new = u_s[rows, :] - _bdot(w_s[rows, :], st)
            o_ref[0, rows, :] = _bdot(qe_s[rows, :], st) + _bdot(in_s[rows, :], v_new)
            s_s[...] = st * el_s[n] + _bdot(kr_s[rows, :], v_new, _TN)
            return 0

        lax.fori_loop(0, NC, chunk, 0)

    return pl.pallas_call(
        body, grid=(B, DN_HEADS),
        in_specs=[pl.BlockSpec((1, 3, T, 128), lambda b, h: (b, 0, 0, h)), pl.BlockSpec((1, T, 128), lambda b, h: (b, 0, 0))],
        out_specs=[pl.BlockSpec((1, T, 128), lambda b, h: (b, 0, h)), pl.BlockSpec((1, 1, NC * DN_HD, DN_HD), lambda b, h: (b, h, 0, 0)),
                   pl.BlockSpec((1, 1, T, DN_C), lambda b, h: (b, h, 0, 0))],
        out_shape=[_sds((B, T, DN_W), F32), _sds((B, DN_HEADS, NC * DN_HD, DN_HD), F32), _sds((B, DN_HEADS, T, DN_C), F32)],
        scratch_shapes=[pltpu.VMEM((DN_HD, DN_HD), F32)] + [pltpu.VMEM((T, DN_HD), F32)] * 4
        + [pltpu.VMEM((T, DN_C), F32), pltpu.VMEM((NC, 1, 128), F32)],
        compiler_params=_cp(("parallel", "parallel")), name=name,
    )(qkv, gates)


def gdn_chunk_bwd(qkv, gates, states, tms, dob, name):
    B, _, T, _ = qkv.shape
    NC = T // DN_C
    C = DN_C

    G = DN_G if NC % DN_G == 0 else 1
    GC = G * C

    def body(x_ref, gt_ref, st_ref, tm_ref, do_ref, dx_ref, dgt_ref, ds_s, w_s, qe_s, kr_s, in_s, el_s, dvn_s, dsa_s):
        h = pl.program_id(1)

        @pl.when(h == 0)
        def _():
            dgt_ref[...] = jnp.zeros_like(dgt_ref)

        def load(gi):
            rows = pl.ds(pl.multiple_of(gi * GC, GC), GC)
            q, k, v = [x_ref[0, i, rows, :].reshape(G, C, DN_HD) for i in range(3)]
            return rows, q, k, v, gt_ref[0, rows, :].reshape(G, C, 128), tm_ref[0, 0, rows, :].reshape(G, C, C)

        def group_a(gi, _):
            rows, q, k, v, gt, tm = load(gi)
            c = _chunk_common(q, k, v, gt, h, tm=tm)
            w_s[rows, :] = c["w"].reshape(GC, DN_HD)
            qe_s[rows, :] = (q * c["e"]).reshape(GC, DN_HD)
            kr_s[rows, :] = (k * c["r"]).reshape(GC, DN_HD)
            in_s[rows, :] = c["intra"].reshape(GC, C)
            el_s[pl.ds(gi * G, G), :, :] = c["el"] * jnp.ones((G, 1, 128), F32)
            return 0

        lax.fori_loop(0, NC // G, group_a, 0)
        ds_s[...] = jnp.zeros_like(ds_s)

        def chunk(m, _):
            n = NC - 1 - m
            rows = pl.ds(pl.multiple_of(n * C, C), C)
            do = do_ref[0, rows, :]
            dsn = ds_s[...]
            dsa_s[pl.ds(pl.multiple_of(n * DN_HD, DN_HD), DN_HD), :] = dsn
            dvn = _bdot(kr_s[rows, :], dsn) + _bdot(in_s[rows, :], do, _TN)
            dvn_s[rows, :] = dvn
            ds_s[...] = dsn * el_s[n] + _bdot(qe_s[rows, :], do, _TN) - _bdot(w_s[rows, :], dvn, _TN)
            return 0

        lax.fori_loop(0, NC, chunk, 0)

        def group_c(gi, _):
            rows, q, k, v, gt, tm = load(gi)
            c = _chunk_common(q, k, v, gt, h, tm=tm)
            incl, strict, decay, e, r, el, tm = c["incl"], c["strict"], c["decay"], c["e"], c["r"], c["el"], c["tm"]
            srow = pl.ds(pl.multiple_of(gi * G * DN_HD, G * DN_HD), G * DN_HD)
            st = st_ref[0, 0, srow, :].reshape(G, DN_HD, DN_HD)
            dsn = dsa_s[srow, :].reshape(G, DN_HD, DN_HD)
            do = do_ref[0, rows, :].reshape(G, C, DN_HD)
            dvn = dvn_s[rows, :].reshape(G, C, DN_HD)
            v_new = c["u"] - _bdot(c["w"], st, _BNN)
            del_ = jnp.sum(jnp.sum(dsn * st, axis=2, keepdims=True), axis=1, keepdims=True)
            dkr = _bdot(v_new, dsn, _BNT)
            dqe = _bdot(do, st, _BNT)
            dintra = _bdot(do, v_new, _BNT)
            dw = -_bdot(dvn, st, _BNT)
            dqkd = jnp.where(incl, dintra, 0.0)
            dqk = dqkd * decay
            ddecay = dqkd * c["qk"]
            dq = dqe * e + _bdot(dqk, k, _BNN)
            dk = dkr * r + _bdot(dqk, q, _BTN)
            dtm = _bdot(dvn, c["vb"], _BNT) + _bdot(dw, c["kbe"], _BNT)
            dvb = _bdot(tm, dvn, _BTN)
            dkbe = _bdot(tm, dw, _BTN)
            dkb = dkbe * e
            de = jnp.sum(dqe * q, axis=2, keepdims=True) + jnp.sum(dkbe * c["kb"], axis=2, keepdims=True)
            da = -_pdot(tm, _pdot(dtm, tm, _BNT), _BTN)
            dlow = jnp.where(strict, da, 0.0)
            dkk = dlow * decay
            ddecay = ddecay + dlow * c["kk"]
            dkb = dkb + _bdot(dkk, k, _BNN)
            dk = dk + _bdot(dkk, c["kb"], _BTN) + dkb * c["beta"]
            dbeta = jnp.sum(dkb * k, axis=2, keepdims=True) + jnp.sum(dvb * v, axis=2, keepdims=True)
            dv = dvb * c["beta"]
            dd = ddecay * decay
            dgc = jnp.sum(dd, axis=2, keepdims=True) - _pdot(dd, c["ones"], _BTN)[:, :, :1]
            dr = jnp.sum(dkr * k, axis=2, keepdims=True)
            dgc = dgc + de * e - dr * r
            dgl = jnp.sum(dr * r, axis=1, keepdims=True) + del_ * el
            rowc = lax.broadcasted_iota(jnp.int32, (C, 1), 0)
            dgc = dgc + jnp.where(rowc == C - 1, dgl, 0.0)
            dg = _pdot(c["inclf"], dgc * c["ones"], _BTN)[:, :, :1]
            dx_ref[0, 0, rows, :] = dq.reshape(GC, DN_HD)
            dx_ref[0, 1, rows, :] = dk.reshape(GC, DN_HD)
            dx_ref[0, 2, rows, :] = dv.reshape(GC, DN_HD)
            lane = c["lane"]
            dgt_ref[0, rows, :] += (jnp.where(lane == h, dg, 0.0) + jnp.where(lane == h + DN_HEADS, dbeta, 0.0)).reshape(GC, 128)
            return 0

        lax.fori_loop(0, NC // G, group_c, 0)

    return pl.pallas_call(
        body, grid=(B, DN_HEADS),
        in_specs=[pl.BlockSpec((1, 3, T, 128), lambda b, h: (b, 0, 0, h)), pl.BlockSpec((1, T, 128), lambda b, h: (b, 0, 0)),
                  pl.BlockSpec((1, 1, NC * DN_HD, DN_HD), lambda b, h: (b, h, 0, 0)), pl.BlockSpec((1, 1, T, C), lambda b, h: (b, h, 0, 0)),
                  pl.BlockSpec((1, T, 128), lambda b, h: (b, 0, h))],
        out_specs=[pl.BlockSpec((1, 3, T, 128), lambda b, h: (b, 0, 0, h)), pl.BlockSpec((1, T, 128), lambda b, h: (b, 0, 0))],
        out_shape=[_sds((B, 3, T, DN_W), F32), _sds((B, T, 128), F32)],
        scratch_shapes=[pltpu.VMEM((DN_HD, DN_HD), F32)] + [pltpu.VMEM((T, DN_HD), F32)] * 3
        + [pltpu.VMEM((T, C), F32), pltpu.VMEM((NC, 1, 128), F32), pltpu.VMEM((T, DN_HD), F32), pltpu.VMEM((NC * DN_HD, DN_HD), F32)],
        compiler_params=_cp(("parallel", "arbitrary")), name=name,
    )(qkv, gates, states, tms, dob)


def ada_fwd(c_all, w_ada, b_sl, name):
    L, D, W = w_ada.shape
    NBt = c_all.shape[0]

    def body(c_ref, w_ref, b_ref, o_ref):
        cv = c_ref[...]
        o_ref[0] = _pdot(cv * _sigmoid(cv), w_ref[0]) + b_ref[0]

    return pl.pallas_call(
        body, grid=(L,),
        in_specs=[pl.BlockSpec((NBt, D), lambda l: (0, 0)), pl.BlockSpec((1, D, W), lambda l: (l, 0, 0)), pl.BlockSpec((1, 1, W), lambda l: (l, 0, 0))],
        out_specs=pl.BlockSpec((1, NBt, W), lambda l: (l, 0, 0)),
        out_shape=_sds((L, NBt, W), F32),
        compiler_params=_cp(("parallel",)), name=name,
    )(c_all, w_ada, b_sl)


def ada_bwd(c_all, dmod_cols, name):
    L, NBt, W = dmod_cols.shape
    D = c_all.shape[1]

    def body(c_ref, d_ref, o_ref):
        cv = c_ref[...]
        o_ref[0] = _pdot(cv * _sigmoid(cv), d_ref[0], _TN)

    return pl.pallas_call(
        body, grid=(L,),
        in_specs=[pl.BlockSpec((NBt, D), lambda l: (0, 0)), pl.BlockSpec((1, NBt, W), lambda l: (l, 0, 0))],
        out_specs=pl.BlockSpec((1, D, W), lambda l: (l, 0, 0)),
        out_shape=_sds((L, D, W), F32),
        compiler_params=_cp(("parallel",)), name=name,
    )(c_all, dmod_cols)


def adamw(partials, w, m, v, name):
    P, R, C = partials.shape
    tr = _pick(R, 256)

    def body(p_ref, w_ref, m_ref, v_ref, g_ref, d_ref, nm_ref, nv_ref):
        g = p_ref[0].astype(F32)
        for i in range(1, P):
            g = g + p_ref[i].astype(F32)
        nm = ADAM_B1 * m_ref[...] + (1.0 - ADAM_B1) * g
        nv = ADAM_B2 * v_ref[...] + (1.0 - ADAM_B2) * (g * g)
        m_hat = nm / (1.0 - ADAM_B1 ** ADAM_STEP)
        v_hat = nv / (1.0 - ADAM_B2 ** ADAM_STEP)
        g_ref[...] = g
        d_ref[...] = -ADAM_LR * (m_hat / (jnp.sqrt(v_hat) + ADAM_EPS) + ADAM_WD * w_ref[...])
        nm_ref[...] = nm
        nv_ref[...] = nv

    tile = pl.BlockSpec((tr, C), lambda i: (i, 0))
    return pl.pallas_call(
        body, grid=(R // tr,),
        in_specs=[pl.BlockSpec((P, tr, C), lambda i: (0, i, 0)), tile, tile, tile],
        out_specs=[tile] * 4, out_shape=[_sds((R, C), F32)] * 4,
        compiler_params=_cp(("parallel",)), name=name,
    )(partials, w, m, v)


def _coords():
    return lax.axis_index("x"), lax.axis_index("y"), lax.axis_index("c")


def all_gather(x, name):
    any_spec = pl.BlockSpec(memory_space=pl.ANY)

    def body(x_ref, out_ref, send_sems, recv_sems, local_sem):
        x_, y_, c_ = _coords()
        me, sibling = (x_, y_, c_), (x_, y_, 1 - c_)
        chips = [(1 - x_, y_), (x_, 1 - y_), (1 - x_, 1 - y_)]

        def slot(px, py, pc):
            return out_ref.at[4 * px + 2 * py + pc]

        def copy(k, block, to, src=None):
            return pltpu.make_async_remote_copy(
                src_ref=slot(*block) if src is None else src, dst_ref=slot(*block),
                send_sem=send_sems.at[k], recv_sem=recv_sems.at[k],
                device_id=to, device_id_type=pl.DeviceIdType.MESH)

        mine = pltpu.make_async_copy(x_ref, slot(*me), local_sem)
        mine.start()
        first = [copy(0, me, sibling, src=x_ref)]
        first += [copy(1 + j, me, (*chip, c_), src=x_ref) for j, chip in enumerate(chips)]
        for cp in first:
            cp.start()
        passed = [copy(4 + j, (*chip, c_), sibling) for j, chip in enumerate(chips)]
        for j, chip in enumerate(chips):
            copy(1 + j, (*chip, c_), me).wait_recv()
            passed[j].start()
        copy(0, sibling, me).wait_recv()
        for j, chip in enumerate(chips):
            copy(4 + j, (*chip, 1 - c_), me).wait_recv()
        for cp in first + passed:
            cp.wait_send()
        mine.wait()

    return pl.pallas_call(
        body, out_shape=_sds((N_DEV,) + x.shape, x.dtype),
        in_specs=[any_spec], out_specs=any_spec,
        scratch_shapes=[pltpu.SemaphoreType.DMA((7,)), pltpu.SemaphoreType.DMA((7,)), pltpu.SemaphoreType.DMA],
        name=name,
    )(x)


def all_to_all(x, name):
    any_spec = pl.BlockSpec(memory_space=pl.ANY)

    def body(x_ref, out_ref, send_sems, recv_sems, local_sem):
        x_, y_, c_ = _coords()
        me = 4 * x_ + 2 * y_ + c_
        mine = pltpu.make_async_copy(x_ref.at[me], out_ref.at[me], local_sem)
        mine.start()
        copies = []
        for k in range(1, N_DEV):
            px = 1 - x_ if k & 4 else x_
            py = 1 - y_ if k & 2 else y_
            pc = 1 - c_ if k & 1 else c_
            peer = 4 * px + 2 * py + pc
            copies.append((pltpu.make_async_remote_copy(
                src_ref=x_ref.at[peer], dst_ref=out_ref.at[me],
                send_sem=send_sems.at[k - 1], recv_sem=recv_sems.at[k - 1],
                device_id=(px, py, pc), device_id_type=pl.DeviceIdType.MESH), peer))
        for cp, _ in copies:
            cp.start()
        for k, (cp, peer) in enumerate(copies):
            pltpu.make_async_remote_copy(
                src_ref=x_ref.at[peer], dst_ref=out_ref.at[peer],
                send_sem=send_sems.at[k], recv_sem=recv_sems.at[k],
                device_id=(x_, y_, c_), device_id_type=pl.DeviceIdType.MESH).wait_recv()
        for cp, _ in copies:
            cp.wait_send()
        mine.wait()

    return pl.pallas_call(
        body, out_shape=_sds(x.shape, x.dtype),
        in_specs=[any_spec], out_specs=any_spec,
        scratch_shapes=[pltpu.SemaphoreType.DMA((7,)), pltpu.SemaphoreType.DMA((7,)), pltpu.SemaphoreType.DMA],
        name=name,
    )(x)


def _rows128(a):
    return a.reshape(-1, 128)


def _pad_lanes(a):
    return jnp.pad(a, ((0, 0), (0, 128 - a.shape[1])))


def kernel(x, c, w_ada, b_ada, norm_mix, norm_mlp, w_in, sb_q_norm, sb_k_norm, conv_w, a_log, dt_bias, dn_out_norm, w_out, w_ff1, w_ff2, loss_target, m_w_ada, m_b_ada, m_norm_mix, m_norm_mlp, m_w_in, m_sb_q_norm, m_sb_k_norm, m_conv_w, m_a_log, m_dt_bias, m_dn_out_norm, m_w_out, m_w_ff1, m_w_ff2, v_w_ada, v_b_ada, v_norm_mix, v_norm_mlp, v_w_in, v_sb_q_norm, v_sb_k_norm, v_conv_w, v_a_log, v_dt_bias, v_dn_out_norm, v_w_out, v_w_ff1, v_w_ff2):
    B, T, D = x.shape
    L = w_ada.shape[0]
    N = B * T
    FF = w_ff1.shape[2] * N_DEV
    WA = w_ada.shape[2]
    CS = conv_w.shape[2]
    me = 4 * lax.axis_index("x") + 2 * lax.axis_index("y") + lax.axis_index("c")
    tm = _pick(T, 1024)

    win_g = all_gather(w_in.astype(BF16), "comm_gather_w_in")
    W_in = jnp.pad(win_g.transpose(1, 2, 0, 3).reshape(L, D, IN_W), ((0, 0), (0, 0), (0, IN_WP - IN_W)))
    W_out = all_gather(w_out.astype(BF16), "comm_gather_w_out").transpose(1, 0, 2, 3).reshape(L, SB_W + DN_W, D)
    W_1 = all_gather(w_ff1.astype(BF16), "comm_gather_w_ff1").transpose(1, 2, 0, 3).reshape(L, D, FF)
    W_2 = all_gather(w_ff2.astype(BF16), "comm_gather_w_ff2").transpose(1, 0, 2, 3).reshape(L, FF, D)
    conv_full = all_gather(conv_w, "comm_gather_conv").transpose(1, 2, 0, 3).reshape(L, CONV_K, 3 * DN_W)

    c_all = all_gather(c, "comm_gather_c").reshape(N_DEV * B, D)
    b_sl = lax.dynamic_slice_in_dim(b_ada, me * WA, WA, axis=1).reshape(L, 1, WA)
    mod_sh = ada_fwd(c_all, w_ada, b_sl, "ada_fwd")
    mod_g = all_gather(mod_sh, "comm_gather_mod")
    mod = lax.dynamic_slice_in_dim(mod_g, me * B, B, axis=2).transpose(1, 2, 0, 3).reshape(L, B, 6 * D)

    def mod_part(l, i):
        return mod[l, :, i * D:(i + 1) * D].reshape(B, 1, D)

    alog_row = _pad_lanes(a_log).reshape(L, 1, 128)
    dtb_row = _pad_lanes(dt_bias).reshape(L, 1, 128)

    def gate_epi(acc, xv, g):
        return acc, xv + g[0] * acc

    def relu2(a):
        r = jnp.maximum(a, 0.0)
        return r * r

    def times_gate(a, g):
        return a * g[0]

    tile_ij = lambda i, j, k: (i, j)

    saved = []
    xc = x
    for l in range(L):
        sh_a, sc_a, g_a, sh_m, sc_m, g_m = [mod_part(l, i) for i in range(6)]
        h = ln_mod_fwd(xc, norm_mix[l:l + 1], sc_a, sh_a, "ln_mod_fwd")
        proj3 = matmul(h.reshape(N, D), W_in[l], mode="nn", name="mm_proj", tm=256)[0].reshape(B, T, IN_WP)
        o_a, tot = sb_attn_fwd(proj3, sb_q_norm[l:l + 1], sb_k_norm[l:l + 1], "sb_attn_fwd")
        qkv = gdn_pre_fwd(proj3, conv_full[l], "gdn_pre_fwd")
        gates = gdn_gates_fwd(proj3, alog_row[l], dtb_row[l], "gdn_gates_fwd")
        ob, states, tms = gdn_chunk_fwd(qkv, gates, "gdn_chunk_fwd")
        o_b = gdn_post_fwd(ob, proj3, dn_out_norm[l:l + 1], "gdn_post_fwd")
        mix = jnp.concatenate([o_a, o_b], axis=-1)
        y1, x_mid = matmul(
            mix.reshape(N, SB_W + DN_W), W_out[l], mode="nn", name="mm_out", out_dtypes=(F32, F32), tm=tm, epi=gate_epi,
            extras=[(xc.reshape(N, D), (tm, _pick(D, 1024)), tile_ij),
                    (g_a, (1, 1, _pick(D, 1024)), lambda i, j, k: (i * tm // T, 0, j))])
        x_mid = x_mid.reshape(B, T, D)
        h2 = ln_mod_fwd(x_mid, norm_mlp[l:l + 1], sc_m, sh_m, "ln_mod_fwd")
        u = matmul(h2.reshape(N, D), W_1[l], mode="nn", name="mm_ff1")[0]
        y2, x_out = matmul(
            u, W_2[l], mode="nn", name="mm_ff2", out_dtypes=(F32, F32), tm=tm, a_fn=relu2, epi=gate_epi,
            extras=[(x_mid.reshape(N, D), (tm, _pick(D, 1024)), tile_ij),
                    (g_m, (1, 1, _pick(D, 1024)), lambda i, j, k: (i * tm // T, 0, j))])
        saved.append(dict(x=xc, h=h, proj3=proj3, tot=tot, qkv=qkv, gates=gates, states=states, tms=tms, ob=ob, mix=mix,
                          y1=y1, x_mid=x_mid, h2=h2, u=u, y2=y2))
        xc = x_out.reshape(B, T, D)

    dx, sq = loss_grad(xc, loss_target, "loss_grad")
    loss = lax.psum((0.5 / D) * jnp.sum(sq), AXES)

    g_win, g_wout, g_w1, g_w2, dmods, smalls = [], [], [], [], [], []
    tk_tok = tm
    for l in reversed(range(L)):
        s = saved[l]
        sh_a, sc_a, g_a, sh_m, sc_m, g_m = [mod_part(l, i) for i in range(6)]
        gate_k = lambda g, blk: (g, (1, 1, blk), lambda i, j, k: (i * tm // T, 0, k))
        gate_tok = lambda g, blk: (g, (1, 1, blk), lambda i, j, k: (k * tk_tok // T, 0, j))
        dx2 = dx.reshape(N, D)
        dg_m = rowsum_prod(dx, s["y2"].reshape(B, T, D), "rowsum_prod")
        du = matmul(dx2, W_2[l], mode="nt", name="mm_ff2_da", out_dtypes=(BF16,), tm=tm, a_fn=times_gate,
                    a_extras=[gate_k(g_m, _pick(D, 1024))],
                    epi=lambda acc, uv: (acc * (2.0 * jnp.maximum(uv, 0.0)),),
                    extras=[(s["u"], (tm, _pick(FF, 1024)), tile_ij)])[0]
        g_w2.append(matmul(s["u"], dx2, mode="tn", name="mm_ff2_dw", out_dtypes=(BF16,), tk=tk_tok, a_fn=relu2,
                           b_fn=times_gate, b_extras=[gate_tok(g_m, _pick(D, 1024))])[0])
        g_w1.append(matmul(s["h2"].reshape(N, D), du, mode="tn", name="mm_ff1_dw", out_dtypes=(BF16,))[0])
        dh2 = matmul(du, W_1[l], mode="nt", name="mm_ff1_da")[0]
        dx_mid, dgn_mlp, dsc_m, dsh_m = ln_mod_bwd(s["x_mid"], norm_mlp[l:l + 1], sc_m, dh2.reshape(B, T, D), dx, "ln_mod_bwd")
        dxm2 = dx_mid.reshape(N, D)
        dg_a = rowsum_prod(dx_mid, s["y1"].reshape(B, T, D), "rowsum_prod")
        dmix3 = matmul(dxm2, W_out[l], mode="nt", name="mm_out_da", tm=tm, a_fn=times_gate,
                       a_extras=[gate_k(g_a, _pick(D, 1024))])[0].reshape(B, T, SB_W + DN_W)
        g_wout.append(matmul(s["mix"].reshape(N, SB_W + DN_W), dxm2, mode="tn", name="mm_out_dw", out_dtypes=(BF16,),
                             tk=tk_tok, b_fn=times_gate, b_extras=[gate_tok(g_a, _pick(D, 1024))])[0])
        dq_a, dk_a, dv_a, dgq, dgk = sb_attn_bwd(s["proj3"], sb_q_norm[l:l + 1], sb_k_norm[l:l + 1], s["tot"], dmix3, "sb_attn_bwd")
        dob, dz, dgn_dn = gdn_post_bwd(s["ob"], s["proj3"], dn_out_norm[l:l + 1], dmix3, "gdn_post_bwd")
        dqkv, dgates = gdn_chunk_bwd(s["qkv"], s["gates"], s["states"], s["tms"], dob, "gdn_chunk_bwd")
        d_dnqkv, dconv_b = gdn_pre_bwd(s["proj3"], conv_full[l], dqkv, "gdn_pre_bwd")
        d_ab, dalog, ddtb = gdn_gates_bwd(s["proj3"], alog_row[l], dtb_row[l], dgates, "gdn_gates_bwd")
        dproj = jnp.concatenate([dq_a, dk_a, dv_a, d_dnqkv, dz, d_ab], axis=-1).reshape(N, IN_WP)
        g_win.append(matmul(s["h"].reshape(N, D), dproj, mode="tn", name="mm_proj_dw", out_dtypes=(BF16,), tm=512, tk=512)[0])
        dh = matmul(dproj, W_in[l], mode="nt", name="mm_proj_da", tm=512, tk=IN_WP)[0]
        dx, dgn_mix, dsc_a, dsh_a = ln_mod_bwd(s["x"], norm_mix[l:l + 1], sc_a, dh.reshape(B, T, D), dx_mid, "ln_mod_bwd")
        dmods.append(jnp.concatenate([dsh_a, dsc_a, dg_a, dsh_m, dsc_m, dg_m], axis=-1).reshape(B, 6 * D))
        smalls.append(dict(norm_mix=dgn_mix, norm_mlp=dgn_mlp, sbq=dgq, sbk=dgk, alog=dalog, dtb=ddtb, dnorm=dgn_dn,
                           conv=jnp.sum(dconv_b, axis=0)))
    for lst in (g_win, g_wout, g_w1, g_w2, dmods, smalls):
        lst.reverse()
    grad_x = dx

    def shard_cols(g, n):
        return g.reshape(L, g.shape[1], N_DEV, n).transpose(2, 0, 1, 3)

    def shard_rows(g, n):
        return g.reshape(L, N_DEV, n, g.shape[2]).transpose(1, 0, 2, 3)

    def update(parts, w, m, v, name):
        shp = w.shape
        r2 = lambda a: a.reshape(-1, shp[-1])
        outs = adamw(parts.reshape(parts.shape[0], -1, shp[-1]), r2(w), r2(m), r2(v), name)
        return [o.reshape(shp) for o in outs]

    wi = w_in.shape[2]
    p_win = all_to_all(shard_cols(jnp.stack(g_win)[:, :, :IN_W], wi), "comm_scatter_w_in")
    p_wout = all_to_all(shard_rows(jnp.stack(g_wout), w_out.shape[1]), "comm_scatter_w_out")
    p_w1 = all_to_all(shard_cols(jnp.stack(g_w1), w_ff1.shape[2]), "comm_scatter_w_ff1")
    p_w2 = all_to_all(shard_rows(jnp.stack(g_w2), w_ff2.shape[1]), "comm_scatter_w_ff2")
    r_win = update(p_win, w_in, m_w_in, v_w_in, "adamw_w_in")
    r_wout = update(p_wout, w_out, m_w_out, v_w_out, "adamw_w_out")
    r_w1 = update(p_w1, w_ff1, m_w_ff1, v_w_ff1, "adamw_w_ff1")
    r_w2 = update(p_w2, w_ff2, m_w_ff2, v_w_ff2, "adamw_w_ff2")

    dmod_g = all_gather(jnp.stack(dmods), "comm_gather_dmod")
    dmod_all = dmod_g.transpose(1, 0, 2, 3).reshape(L, N_DEV * B, 6 * D)
    g_wada = ada_bwd(c_all, lax.dynamic_slice_in_dim(dmod_all, me * WA, WA, axis=2), "ada_bwd")
    r_wada = update(g_wada[None], w_ada, m_w_ada, v_w_ada, "adamw_w_ada")
    r_bada = update(dmod_g.transpose(0, 2, 1, 3).reshape(N_DEV * B, L, 6 * D), b_ada, m_b_ada, v_b_ada, "adamw_b_ada")

    def pack(f):
        return jnp.concatenate([
            _rows128(f("norm_mix")), _rows128(f("norm_mlp")), _rows128(f("sbq")), _rows128(f("sbk")),
            f("alog"), f("dtb"), f("dnorm"), _rows128(f("conv"))], axis=0)

    names = ["norm_mix", "norm_mlp", "sbq", "sbk", "alog", "dtb", "dnorm"]
    part = pack(lambda n: jnp.concatenate([sm[n] for sm in smalls], axis=0))
    n_rep = part.shape[0] - L * CONV_K * 3 * DN_W // 128
    part_g = all_gather(part, "comm_gather_small")
    params = dict(norm_mix=(norm_mix, m_norm_mix, v_norm_mix), norm_mlp=(norm_mlp, m_norm_mlp, v_norm_mlp),
                  sbq=(sb_q_norm, m_sb_q_norm, v_sb_q_norm), sbk=(sb_k_norm, m_sb_k_norm, v_sb_k_norm),
                  alog=(a_log, m_a_log, v_a_log), dtb=(dt_bias, m_dt_bias, v_dt_bias),
                  dnorm=(dn_out_norm, m_dn_out_norm, v_dn_out_norm))

    def rows_of(n, a):
        return _pad_lanes(a) if n in ("alog", "dtb") else _rows128(a)

    packed = [jnp.concatenate([rows_of(n, params[n][i]) for n in names], axis=0) for i in range(3)]
    r_small = adamw(part_g[:, :n_rep], packed[0], packed[1], packed[2], "adamw_small")
    small_out = {}
    off = 0
    for n in names:
        w0 = params[n][0]
        nr = rows_of(n, w0).shape[0]
        vals = [o[off:off + nr] for o in r_small]
        small_out[n] = [(vv[:, :w0.shape[1]] if n in ("alog", "dtb") else vv.reshape(w0.shape)) for vv in vals]
        off += nr
    conv_parts = part_g[:, n_rep:].reshape(N_DEV, L, CONV_K, 3 * DN_W)
    r_conv = update(lax.dynamic_slice_in_dim(conv_parts, me * CS, CS, axis=3), conv_w, m_conv_w, v_conv_w, "adamw_conv")

    order = [r_wada, r_bada, small_out["norm_mix"], small_out["norm_mlp"], r_win, small_out["sbq"], small_out["sbk"],
             r_conv, small_out["alog"], small_out["dtb"], small_out["dnorm"], r_wout, r_w1, r_w2]
    outs = [loss, grad_x]
    for i in range(4):
        outs += [r[i] for r in order]
    return tuple(outs)
```

```python
import functools
import math

import jax
import jax.numpy as jnp
from jax import lax
from jax.experimental import pallas as pl
from jax.experimental.pallas import tpu as pltpu

F32 = jnp.float32
BF16 = jnp.bfloat16
EPS = 1e-6
N_DEV = 8
AXES = ("x", "y", "c")

SB_HEADS, SB_HD = 8, 64
SB_W = SB_HEADS * SB_HD
SB_BLK = 128
DN_HEADS, DN_HD = 4, 128
DN_W = DN_HEADS * DN_HD
DN_C = 64
CONV_K = 4
IN_W = 3 * SB_W + 4 * DN_W + 2 * DN_HEADS
IN_WP = 3 * SB_W + 4 * DN_W + 128
COL_DNQKV = 3 * SB_W // 128
COL_Z = COL_DNQKV + 3 * DN_W // 128
COL_AB = COL_Z + DN_W // 128

ADAM_LR, ADAM_B1, ADAM_B2, ADAM_EPS, ADAM_WD, ADAM_STEP = 0.001, 0.9, 0.999, 1e-08, 0.01, 10

VMEM_LIMIT = 56 * 1024 * 1024


def _cp(sem):
    return pltpu.CompilerParams(dimension_semantics=sem, vmem_limit_bytes=VMEM_LIMIT)


def _pick(dim, pref):
    return pref if dim % pref == 0 else dim


def _sds(shape, dtype):
    return jax.ShapeDtypeStruct(tuple(shape), dtype)


_NN = (((1,), (0,)), ((), ()))
_NT = (((1,), (1,)), ((), ()))
_TN = (((0,), (0,)), ((), ()))


def _bdot(a, b, dims=_NN):
    return lax.dot_general(a.astype(BF16), b.astype(BF16), dims, preferred_element_type=F32)


def _split(a):
    hi = a.astype(BF16)
    lo = (a - hi.astype(F32)).astype(BF16)
    return hi, lo


def _pdot(a, b, dims=_NN):
    ah, al = _split(a)
    bh, bl = _split(b)
    d = functools.partial(lax.dot_general, dimension_numbers=dims, preferred_element_type=F32)
    return d(ah, bh) + (d(ah, bl) + d(al, bh))


def _sigmoid(x):
    return 1.0 / (1.0 + jnp.exp(-x))


def _softplus(x):
    return jnp.maximum(x, 0.0) + jnp.log(1.0 + jnp.exp(-jnp.abs(x)))


def matmul(a, b, *, mode, name, out_dtypes=(F32,), a_fn=None, a_extras=(), b_fn=None, b_extras=(),
           epi=None, extras=(), tm=1024, tn=1024, tk=1024):
    if mode == "tn":
        K, M = a.shape
    else:
        M, K = a.shape
    N = b.shape[0] if mode == "nt" else b.shape[1]
    tm, tn, tk = _pick(M, tm), _pick(N, tn), _pick(K, tk)
    nk = K // tk
    dims = {"nn": _NN, "nt": _NT, "tn": _TN}[mode]
    a_spec = pl.BlockSpec((tk, tm), lambda i, j, k: (k, i)) if mode == "tn" else pl.BlockSpec((tm, tk), lambda i, j, k: (i, k))
    b_spec = pl.BlockSpec((tn, tk), lambda i, j, k: (j, k)) if mode == "nt" else pl.BlockSpec((tk, tn), lambda i, j, k: (k, j))
    na, nb, ne, no = len(a_extras), len(b_extras), len(extras), len(out_dtypes)

    def body(*refs):
        a_ref, b_ref = refs[0], refs[1]
        ax = refs[2:2 + na]
        bx = refs[2 + na:2 + na + nb]
        ex = refs[2 + na + nb:2 + na + nb + ne]
        outs = refs[2 + na + nb + ne:2 + na + nb + ne + no]
        acc_ref = refs[-1]
        k = pl.program_id(2)

        @pl.when(k == 0)
        def _():
            acc_ref[...] = jnp.zeros_like(acc_ref)

        av = a_ref[...]
        if a_fn is not None:
            av = a_fn(av, *[r[...] for r in ax])
        bv = b_ref[...]
        if b_fn is not None:
            bv = b_fn(bv, *[r[...] for r in bx])
        acc_ref[...] += lax.dot_general(av.astype(BF16), bv.astype(BF16), dims, preferred_element_type=F32)

        @pl.when(k == nk - 1)
        def _():
            res = acc_ref[...]
            res = (res,) if epi is None else epi(res, *[r[...] for r in ex])
            for o_ref, r in zip(outs, res):
                o_ref[...] = r.astype(o_ref.dtype)

    xs = list(a_extras) + list(b_extras) + list(extras)
    return pl.pallas_call(
        body,
        grid=(M // tm, N // tn, nk),
        in_specs=[a_spec, b_spec] + [pl.BlockSpec(bs, im) for _, bs, im in xs],
        out_specs=[pl.BlockSpec((tm, tn), lambda i, j, k: (i, j)) for _ in out_dtypes],
        out_shape=[_sds((M, N), dt) for dt in out_dtypes],
        scratch_shapes=[pltpu.VMEM((tm, tn), F32)],
        compiler_params=_cp(("parallel", "parallel", "arbitrary")),
        name=name,
    )(a, b, *[x for x, _, _ in xs])


def ln_mod_fwd(x, gain, sc, sh, name):
    B, T, D = x.shape
    tt = _pick(T, 512)

    def body(x_ref, g_ref, sc_ref, sh_ref, h_ref):
        xv = x_ref[0]
        r = lax.rsqrt(jnp.mean(xv * xv, axis=-1, keepdims=True) + EPS)
        h = (xv * r * g_ref[...]) * (1.0 + sc_ref[0]) + sh_ref[0]
        h_ref[0] = h.astype(h_ref.dtype)

    return pl.pallas_call(
        body, grid=(B, T // tt),
        in_specs=[pl.BlockSpec((1, tt, D), lambda b, t: (b, t, 0)), pl.BlockSpec((1, D), lambda b, t: (0, 0)),
                  pl.BlockSpec((1, 1, D), lambda b, t: (b, 0, 0)), pl.BlockSpec((1, 1, D), lambda b, t: (b, 0, 0))],
        out_specs=pl.BlockSpec((1, tt, D), lambda b, t: (b, t, 0)),
        out_shape=_sds((B, T, D), BF16),
        compiler_params=_cp(("parallel", "parallel")), name=name,
    )(x, gain, sc, sh)


def ln_mod_bwd(x, gain, sc, dh, dres, name):
    B, T, D = x.shape
    tt = _pick(T, 512)

    def body(x_ref, g_ref, sc_ref, dh_ref, dres_ref, dx_ref, dg_ref, dsc_ref, dsh_ref):
        b, t = pl.program_id(0), pl.program_id(1)
        xv, dhv = x_ref[0], dh_ref[0]
        g, s = g_ref[...], sc_ref[0]
        r = lax.rsqrt(jnp.mean(xv * xv, axis=-1, keepdims=True) + EPS)
        xn = xv * r
        dxn = dhv * (g * (1.0 + s))
        dx_ref[0] = dres_ref[0] + r * (dxn - xn * jnp.mean(dxn * xn, axis=-1, keepdims=True))
        s1 = jnp.sum(dhv * xn, axis=0, keepdims=True)
        s2 = jnp.sum(dhv, axis=0, keepdims=True)

        @pl.when(t == 0)
        def _():
            dsc_ref[0] = jnp.zeros_like(s1)
            dsh_ref[0] = jnp.zeros_like(s1)

        @pl.when((t == 0) & (b == 0))
        def _():
            dg_ref[...] = jnp.zeros_like(s1)

        dsc_ref[0] += s1 * g
        dsh_ref[0] += s2
        dg_ref[...] += s1 * (1.0 + s)

    tile = pl.BlockSpec((1, tt, D), lambda b, t: (b, t, 0))
    row = pl.BlockSpec((1, D), lambda b, t: (0, 0))
    brow = pl.BlockSpec((1, 1, D), lambda b, t: (b, 0, 0))
    return pl.pallas_call(
        body, grid=(B, T // tt),
        in_specs=[tile, row, brow, tile, tile],
        out_specs=[tile, row, brow, brow],
        out_shape=[_sds((B, T, D), F32), _sds((1, D), F32), _sds((B, 1, D), F32), _sds((B, 1, D), F32)],
        compiler_params=_cp(("arbitrary", "arbitrary")), name=name,
    )(x, gain, sc, dh, dres)


def rowsum_prod(a, b, name):
    B, T, D = a.shape
    tt = _pick(T, 512)

    def body(a_ref, b_ref, o_ref):
        @pl.when(pl.program_id(1) == 0)
        def _():
            o_ref[...] = jnp.zeros_like(o_ref)

        o_ref[0] += jnp.sum(a_ref[0] * b_ref[0], axis=0, keepdims=True)

    tile = pl.BlockSpec((1, tt, D), lambda b, t: (b, t, 0))
    return pl.pallas_call(
        body, grid=(B, T // tt), in_specs=[tile, tile],
        out_specs=pl.BlockSpec((1, 1, D), lambda b, t: (b, 0, 0)),
        out_shape=_sds((B, 1, D), F32),
        compiler_params=_cp(("parallel", "arbitrary")), name=name,
    )(a, b)


def loss_grad(y, tgt, name):
    B, T, D = y.shape
    tt = _pick(T, 512)

    def body(y_ref, t_ref, dy_ref, s_ref):
        @pl.when((pl.program_id(0) == 0) & (pl.program_id(1) == 0))
        def _():
            s_ref[...] = jnp.zeros_like(s_ref)

        e = y_ref[0] - t_ref[0]
        dy_ref[0] = e * (1.0 / D)
        s_ref[...] += jnp.sum(e * e, axis=0, keepdims=True)

    tile = pl.BlockSpec((1, tt, D), lambda b, t: (b, t, 0))
    return pl.pallas_call(
        body, grid=(B, T // tt), in_specs=[tile, tile],
        out_specs=[tile, pl.BlockSpec((1, D), lambda b, t: (0, 0))],
        out_shape=[_sds((B, T, D), F32), _sds((1, D), F32)],
        compiler_params=_cp(("arbitrary", "arbitrary")), name=name,
    )(y, tgt)


def _sb_group(nb):
    return 4 if nb % 4 == 0 else (2 if nb % 2 == 0 else 1)


def _tri_sum(x, tri2):
    hi, lo = _split(x)
    return lax.dot_general(jnp.concatenate([hi, lo], axis=1), tri2, _NN, preferred_element_type=F32)


def _tri2(cond):
    t = cond.astype(BF16)
    return jnp.concatenate([t, t], axis=0)


def sb_attn_fwd(proj3, gq, gk, name):
    B, T, _ = proj3.shape
    NB = T // SB_BLK
    G = _sb_group(NB)
    KW = G * SB_BLK
    scale = SB_HD ** -0.5

    def body(q_ref, k_ref, v_ref, gq_ref, gk_ref, o_ref, tot_ref, qn_s, kn_s, v_s):
        row_io = lax.broadcasted_iota(jnp.int32, (SB_BLK, SB_BLK), 0)
        col_io = lax.broadcasted_iota(jnp.int32, (SB_BLK, SB_BLK), 1)
        tri = _tri2(row_io > col_io)

        def prep(i, _):
            rows = pl.ds(pl.multiple_of(i * SB_BLK, SB_BLK), SB_BLK)
            for hh in range(2):
                sl = slice(hh * SB_HD, (hh + 1) * SB_HD)
                q = q_ref[0, rows, sl]
                k = k_ref[0, rows, sl]
                qn_s[hh, rows, :] = (q * lax.rsqrt(jnp.mean(q * q, -1, keepdims=True) + EPS) * (gq_ref[...] * scale)).astype(BF16)
                kn_s[hh, rows, :] = (k * lax.rsqrt(jnp.mean(k * k, -1, keepdims=True) + EPS) * gk_ref[...]).astype(BF16)
                v_s[hh, rows, :] = v_ref[0, rows, sl].astype(BF16)
            return 0

        lax.fori_loop(0, NB, prep, 0)

        diff_w = (lax.broadcasted_iota(jnp.int32, (SB_BLK, KW), 1)
                  - lax.broadcasted_iota(jnp.int32, (SB_BLK, KW), 0))

        def qblock(i, _):
            rows = pl.ds(pl.multiple_of(i * SB_BLK, SB_BLK), SB_BLK)
            qn = [qn_s[hh, rows, :] for hh in range(2)]
            nsj = i // G + 1

            def sblock(sj, carry, masked):
                cols = pl.ds(pl.multiple_of(sj * KW, KW), KW)
                mask = diff_w < (i % G) * SB_BLK
                zs = [lax.dot_general(qn[hh], kn_s[hh, cols, :], _NT, preferred_element_type=F32) for hh in range(2)]
                lgs, lss = [], []
                for hh in range(2):
                    sp = _softplus(zs[hh])
                    lgs.append(jnp.where(mask, -sp, 0.0) if masked else -sp)
                    lss.append(zs[hh] - sp)
                blocks = [lgs[hh][:, s * SB_BLK:(s + 1) * SB_BLK] for hh in range(2) for s in range(G)]
                ts_all = _tri_sum(jnp.concatenate(blocks, axis=0), tri)
                atts, css = [], []
                for hh in range(2):
                    cs = carry[hh][1]
                    ps = []
                    for s in reversed(range(G)):
                        n = hh * G + s
                        ts = ts_all[n * SB_BLK:(n + 1) * SB_BLK]
                        ps.append(lss[hh][:, s * SB_BLK:(s + 1) * SB_BLK] + ts + cs)
                        cs = cs + (ts[:, :1] + blocks[n][:, :1])
                    p = ps[0] if G == 1 else jnp.concatenate(ps[::-1], axis=1)
                    att = jnp.exp(p)
                    atts.append((jnp.where(mask, att, 0.0) if masked else att).astype(BF16))
                    css.append(cs)
                return tuple((carry[hh][0] + lax.dot_general(atts[hh], v_s[hh, cols, :], _NN, preferred_element_type=F32), css[hh])
                             for hh in range(2))

            init = (jnp.zeros((SB_BLK, SB_HD), F32), jnp.zeros((SB_BLK, 1), F32))
            res = sblock(nsj - 1, (init, init), True)
            res = lax.fori_loop(0, nsj - 1, lambda jj, c: sblock(nsj - 2 - jj, c, False), res)
            for hh in range(2):
                o_ref[0, rows, hh * SB_HD:(hh + 1) * SB_HD] = res[hh][0].astype(o_ref.dtype)
                tot_ref[0, hh, rows, :] = res[hh][1]
            return 0

        lax.fori_loop(0, NB, qblock, 0)

    blk = lambda off: pl.BlockSpec((1, T, 128), lambda b, p: (b, 0, off + p))
    grow = pl.BlockSpec((1, SB_HD), lambda b, p: (0, 0))
    return pl.pallas_call(
        body, grid=(B, SB_W // 128),
        in_specs=[blk(0), blk(SB_W // 128), blk(2 * SB_W // 128), grow, grow],
        out_specs=[pl.BlockSpec((1, T, 128), lambda b, p: (b, 0, p)), pl.BlockSpec((1, 2, T, 1), lambda b, p: (b, p, 0, 0))],
        out_shape=[_sds((B, T, SB_W), BF16), _sds((B, SB_HEADS, T, 1), F32)],
        scratch_shapes=[pltpu.VMEM((2, T, SB_HD), BF16)] * 3,
        compiler_params=_cp(("parallel", "parallel")), name=name,
    )(proj3, proj3, proj3, gq, gk)


def sb_attn_bwd(proj3, gq, gk, tot, dmix3, name):
    B, T, _ = proj3.shape
    NB = T // SB_BLK
    G = _sb_group(NB)
    KW = G * SB_BLK
    scale = SB_HD ** -0.5

    def body(q_ref, k_ref, v_ref, gq_ref, gk_ref, tot_ref, do_ref, dq_ref, dk_ref, dv_ref, dgq_ref, dgk_ref,
             qn_s, kn_s, v_s, do_s, dqn_s, dkn_s, dv_s):
        row_io = lax.broadcasted_iota(jnp.int32, (SB_BLK, SB_BLK), 0)
        col_io = lax.broadcasted_iota(jnp.int32, (SB_BLK, SB_BLK), 1)
        tri = _tri2(row_io > col_io)
        trip = _tri2(row_io < col_io)

        @pl.when((pl.program_id(0) == 0) & (pl.program_id(1) == 0))
        def _():
            dgq_ref[...] = jnp.zeros_like(dgq_ref)
            dgk_ref[...] = jnp.zeros_like(dgk_ref)

        def prep(i, _):
            rows = pl.ds(pl.multiple_of(i * SB_BLK, SB_BLK), SB_BLK)
            for hh in range(2):
                sl = slice(hh * SB_HD, (hh + 1) * SB_HD)
                q = q_ref[0, rows, sl]
                k = k_ref[0, rows, sl]
                qn_s[hh, rows, :] = (q * lax.rsqrt(jnp.mean(q * q, -1, keepdims=True) + EPS) * (gq_ref[...] * scale)).astype(BF16)
                kn_s[hh, rows, :] = (k * lax.rsqrt(jnp.mean(k * k, -1, keepdims=True) + EPS) * gk_ref[...]).astype(BF16)
                v_s[hh, rows, :] = v_ref[0, rows, sl].astype(BF16)
                do_s[hh, rows, :] = do_ref[0, rows, sl].astype(BF16)
            return 0

        lax.fori_loop(0, NB, prep, 0)
        dkn_s[...] = jnp.zeros_like(dkn_s)
        dv_s[...] = jnp.zeros_like(dv_s)

        diff_w = (lax.broadcasted_iota(jnp.int32, (SB_BLK, KW), 1)
                  - lax.broadcasted_iota(jnp.int32, (SB_BLK, KW), 0))

        def qblock(i, _):
            rows = pl.ds(pl.multiple_of(i * SB_BLK, SB_BLK), SB_BLK)
            qn = [qn_s[hh, rows, :] for hh in range(2)]
            dov = [do_s[hh, rows, :] for hh in range(2)]
            tot = [tot_ref[0, hh, rows, :] for hh in range(2)]

            def sblock(sj, carry, masked):
                cols = pl.ds(pl.multiple_of(sj * KW, KW), KW)
                mask = diff_w < (i % G) * SB_BLK
                hs = range(2)
                kns = [kn_s[hh, cols, :] for hh in hs]
                zs = [lax.dot_general(qn[hh], kns[hh], _NT, preferred_element_type=F32) for hh in hs]
                datts = [lax.dot_general(dov[hh], v_s[hh, cols, :], _NT, preferred_element_type=F32) for hh in hs]
                lgs, lss = [], []
                for hh in hs:
                    sp = _softplus(zs[hh])
                    lgs.append(jnp.where(mask, -sp, 0.0) if masked else -sp)
                    lss.append(zs[hh] - sp)
                blocks = [lgs[hh][:, s * SB_BLK:(s + 1) * SB_BLK] for hh in hs for s in range(G)]
                ts_all = _tri_sum(jnp.concatenate(blocks, axis=0), tri)
                atts, dps, cums = [], [], []
                for hh in hs:
                    cum = carry[hh][1]
                    ps = []
                    for s in range(G):
                        n = hh * G + s
                        ts = ts_all[n * SB_BLK:(n + 1) * SB_BLK]
                        cum = cum + (ts[:, :1] + blocks[n][:, :1])
                        ps.append(lss[hh][:, s * SB_BLK:(s + 1) * SB_BLK] + ts + (tot[hh] - cum))
                    p = ps[0] if G == 1 else jnp.concatenate(ps, axis=1)
                    att = jnp.exp(p)
                    att = jnp.where(mask, att, 0.0) if masked else att
                    atts.append(att.astype(BF16))
                    dps.append(att * datts[hh])
                    cums.append(cum)
                dblocks = [dps[hh][:, s * SB_BLK:(s + 1) * SB_BLK] for hh in hs for s in range(G)]
                tp_all = _tri_sum(jnp.concatenate(dblocks, axis=0), trip)
                dzs, cdps = [], []
                for hh in hs:
                    cdp = carry[hh][2]
                    dls = []
                    for s in range(G):
                        n = hh * G + s
                        tp = tp_all[n * SB_BLK:(n + 1) * SB_BLK]
                        dls.append(tp + cdp)
                        cdp = cdp + (tp[:, SB_BLK - 1:] + dblocks[n][:, SB_BLK - 1:])
                    dlg = dls[0] if G == 1 else jnp.concatenate(dls, axis=1)
                    dz = dps[hh] - jnp.exp(lss[hh]) * (dps[hh] + dlg)
                    dzs.append((jnp.where(mask, dz, 0.0) if masked else dz).astype(BF16))
                    cdps.append(cdp)
                new = []
                for hh in hs:
                    dq = carry[hh][0] + lax.dot_general(dzs[hh], kns[hh], _NN, preferred_element_type=F32)
                    dkn_s[hh, cols, :] += lax.dot_general(dzs[hh], qn[hh], _TN, preferred_element_type=F32)
                    dv_s[hh, cols, :] += lax.dot_general(atts[hh], dov[hh], _TN, preferred_element_type=F32)
                    new.append((dq, cums[hh], cdps[hh]))
                return tuple(new)

            z1 = jnp.zeros((SB_BLK, 1), F32)
            init = (jnp.zeros((SB_BLK, SB_HD), F32), z1, z1)
            nsj = i // G + 1
            res = lax.fori_loop(0, nsj - 1, lambda sj, c: sblock(sj, c, False), (init, init))
            res = sblock(nsj - 1, res, True)
            for hh in range(2):
                dqn_s[hh, rows, :] = res[hh][0]
            return 0

        lax.fori_loop(0, NB, qblock, 0)

        def fin(i, carry):
            aq, ak = carry
            rows = pl.ds(pl.multiple_of(i * SB_BLK, SB_BLK), SB_BLK)
            for hh in range(2):
                sl = slice(hh * SB_HD, (hh + 1) * SB_HD)
                for src_ref, d_s, g_ref, out_ref, mult, which in ((q_ref, dqn_s, gq_ref, dq_ref, scale, 0), (k_ref, dkn_s, gk_ref, dk_ref, 1.0, 1)):
                    xr = src_ref[0, rows, sl]
                    r = lax.rsqrt(jnp.mean(xr * xr, -1, keepdims=True) + EPS)
                    dy = d_s[hh, rows, :] * mult
                    u = dy * g_ref[...]
                    out_ref[0, rows, sl] = r * u - xr * (r * r * r) * jnp.mean(u * xr, -1, keepdims=True)
                    part = jnp.sum(dy * xr * r, axis=0, keepdims=True)
                    if which == 0:
                        aq = aq + part
                    else:
                        ak = ak + part
                dv_ref[0, rows, sl] = dv_s[hh, rows, :]
            return aq, ak

        z64 = jnp.zeros((1, SB_HD), F32)
        aq, ak = lax.fori_loop(0, NB, fin, (z64, z64))
        dgq_ref[...] += aq
        dgk_ref[...] += ak

    blk = lambda off: pl.BlockSpec((1, T, 128), lambda b, p: (b, 0, off + p))
    grow = pl.BlockSpec((1, SB_HD), lambda b, p: (0, 0))
    return pl.pallas_call(
        body, grid=(B, SB_W // 128),
        in_specs=[blk(0), blk(SB_W // 128), blk(2 * SB_W // 128), grow, grow,
                  pl.BlockSpec((1, 2, T, 1), lambda b, p: (b, p, 0, 0)), blk(0)],
        out_specs=[blk(0), blk(0), blk(0), grow, grow],
        out_shape=[_sds((B, T, SB_W), F32)] * 3 + [_sds((1, SB_HD), F32)] * 2,
        scratch_shapes=[pltpu.VMEM((2, T, SB_HD), BF16)] * 4 + [pltpu.VMEM((2, T, SB_HD), F32)] * 3,
        compiler_params=_cp(("arbitrary", "arbitrary")), name=name,
    )(proj3, proj3, proj3, gq, gk, tot, dmix3)


def _conv_silu(x, w, T):
    t_io = lax.broadcasted_iota(jnp.int32, x.shape, 0)
    xs = [x] + [jnp.where(t_io >= s, pltpu.roll(x, s, 0), 0.0) for s in range(1, CONV_K)]
    y = xs[0] * w[CONV_K - 1:CONV_K, :]
    for s in range(1, CONV_K):
        y = y + xs[s] * w[CONV_K - 1 - s:CONV_K - s, :]
    return y, y * _sigmoid(y), xs


def gdn_pre_fwd(proj3, conv_w, name):
    B, T, _ = proj3.shape
    qs = DN_HD ** -0.5

    def body(x_ref, w_ref, o_ref):
        kind = pl.program_id(1) // DN_HEADS
        _, s, _ = _conv_silu(x_ref[0], w_ref[...], T)
        n = lax.rsqrt(jnp.sum(s * s, axis=-1, keepdims=True) + EPS)
        c = jnp.where(kind == 0, qs, 1.0)
        o_ref[0, 0] = jnp.where(kind < 2, s * (n * c), s)

    return pl.pallas_call(
        body, grid=(B, 3 * DN_HEADS),
        in_specs=[pl.BlockSpec((1, T, 128), lambda b, j: (b, 0, COL_DNQKV + j)), pl.BlockSpec((CONV_K, 128), lambda b, j: (0, j))],
        out_specs=pl.BlockSpec((1, 1, T, 128), lambda b, j: (b, j // DN_HEADS, 0, j % DN_HEADS)),
        out_shape=_sds((B, 3, T, DN_W), F32),
        compiler_params=_cp(("parallel", "parallel")), name=name,
    )(proj3, conv_w)


def gdn_pre_bwd(proj3, conv_w, dqkv, name):
    B, T, _ = proj3.shape
    qs = DN_HD ** -0.5

    def body(x_ref, w_ref, d_ref, dx_ref, dw_ref):
        kind = pl.program_id(1) // DN_HEADS
        w = w_ref[...]
        y, s, xs = _conv_silu(x_ref[0], w, T)
        dout = d_ref[0, 0]
        n = lax.rsqrt(jnp.sum(s * s, axis=-1, keepdims=True) + EPS)
        c = jnp.where(kind == 0, qs, 1.0)
        dsn = c * (n * dout - s * (n * n * n) * jnp.sum(dout * s, axis=-1, keepdims=True))
        ds = jnp.where(kind < 2, dsn, dout)
        sg = _sigmoid(y)
        dy = ds * (sg * (1.0 + y * (1.0 - sg)))
        t_io = lax.broadcasted_iota(jnp.int32, dy.shape, 0)
        dx = dy * w[CONV_K - 1:CONV_K, :]
        dw_ref[0, CONV_K - 1:CONV_K, :] = jnp.sum(dy * xs[0], axis=0, keepdims=True)
        for sft in range(1, CONV_K):
            dx = dx + jnp.where(t_io < T - sft, pltpu.roll(dy, T - sft, 0), 0.0) * w[CONV_K - 1 - sft:CONV_K - sft, :]
            dw_ref[0, CONV_K - 1 - sft:CONV_K - sft, :] = jnp.sum(dy * xs[sft], axis=0, keepdims=True)
        dx_ref[0] = dx

    return pl.pallas_call(
        body, grid=(B, 3 * DN_HEADS),
        in_specs=[pl.BlockSpec((1, T, 128), lambda b, j: (b, 0, COL_DNQKV + j)), pl.BlockSpec((CONV_K, 128), lambda b, j: (0, j)),
                  pl.BlockSpec((1, 1, T, 128), lambda b, j: (b, j // DN_HEADS, 0, j % DN_HEADS))],
        out_specs=[pl.BlockSpec((1, T, 128), lambda b, j: (b, 0, j)), pl.BlockSpec((1, CONV_K, 128), lambda b, j: (b, 0, j))],
        out_shape=[_sds((B, T, 3 * DN_W), F32), _sds((B, CONV_K, 3 * DN_W), F32)],
        compiler_params=_cp(("parallel", "parallel")), name=name,
    )(proj3, conv_w, dqkv)


def gdn_gates_fwd(proj3, alog_row, dtb_row, name):
    B, T, _ = proj3.shape

    def body(x_ref, al_ref, dt_ref, o_ref):
        x = x_ref[0]
        lane = lax.broadcasted_iota(jnp.int32, x.shape, 1)
        g = -jnp.exp(al_ref[...]) * _softplus(x + dt_ref[...])
        o_ref[0] = jnp.where(lane < DN_HEADS, g, jnp.where(lane < 2 * DN_HEADS, _sigmoid(x), 0.0))

    row = pl.BlockSpec((1, 128), lambda b: (0, 0))
    return pl.pallas_call(
        body, grid=(B,),
        in_specs=[pl.BlockSpec((1, T, 128), lambda b: (b, 0, COL_AB)), row, row],
        out_specs=pl.BlockSpec((1, T, 128), lambda b: (b, 0, 0)),
        out_shape=_sds((B, T, 128), F32),
        compiler_params=_cp(("parallel",)), name=name,
    )(proj3, alog_row, dtb_row)


def gdn_gates_bwd(proj3, alog_row, dtb_row, dgates, name):
    B, T, _ = proj3.shape

    def body(x_ref, al_ref, dt_ref, d_ref, dx_ref, dal_ref, ddt_ref):
        @pl.when(pl.program_id(0) == 0)
        def _():
            dal_ref[...] = jnp.zeros_like(dal_ref)
            ddt_ref[...] = jnp.zeros_like(ddt_ref)

        x, d = x_ref[0], d_ref[0]
        lane = lax.broadcasted_iota(jnp.int32, x.shape, 1)
        a = x + dt_ref[...]
        na = -jnp.exp(al_ref[...])
        da = jnp.where(lane < DN_HEADS, d * na * _sigmoid(a), 0.0)
        bt = _sigmoid(x)
        dx_ref[0] = da + jnp.where((lane >= DN_HEADS) & (lane < 2 * DN_HEADS), d * bt * (1.0 - bt), 0.0)
        dal_ref[...] += jnp.sum(jnp.where(lane < DN_HEADS, d * na * _softplus(a), 0.0), axis=0, keepdims=True)
        ddt_ref[...] += jnp.sum(da, axis=0, keepdims=True)

    row = pl.BlockSpec((1, 128), lambda b: (0, 0))
    tile = pl.BlockSpec((1, T, 128), lambda b: (b, 0, 0))
    return pl.pallas_call(
        body, grid=(B,),
        in_specs=[pl.BlockSpec((1, T, 128), lambda b: (b, 0, COL_AB)), row, row, tile],
        out_specs=[tile, row, row],
        out_shape=[_sds((B, T, 128), F32), _sds((1, 128), F32), _sds((1, 128), F32)],
        compiler_params=_cp(("arbitrary",)), name=name,
    )(proj3, alog_row, dtb_row, dgates)


def gdn_post_fwd(ob, proj3, gain, name):
    B, T, _ = ob.shape

    def body(o_ref, z_ref, g_ref, out_ref):
        o, z = o_ref[0], z_ref[0]
        r = lax.rsqrt(jnp.mean(o * o, axis=-1, keepdims=True) + EPS)
        out_ref[0] = ((o * r * g_ref[...]) * (z * _sigmoid(z))).astype(out_ref.dtype)

    tile = pl.BlockSpec((1, T, 128), lambda b, h: (b, 0, h))
    return pl.pallas_call(
        body, grid=(B, DN_HEADS),
        in_specs=[tile, pl.BlockSpec((1, T, 128), lambda b, h: (b, 0, COL_Z + h)), pl.BlockSpec((1, 128), lambda b, h: (0, 0))],
        out_specs=tile, out_shape=_sds((B, T, DN_W), BF16),
        compiler_params=_cp(("parallel", "parallel")), name=name,
    )(ob, proj3, gain)


def gdn_post_bwd(ob, proj3, gain, dmix3, name):
    B, T, _ = ob.shape

    def body(o_ref, z_ref, g_ref, d_ref, do_ref, dz_ref, dg_ref):
        @pl.when((pl.program_id(0) == 0) & (pl.program_id(1) == 0))
        def _():
            dg_ref[...] = jnp.zeros_like(dg_ref)

        o, z, d, g = o_ref[0], z_ref[0], d_ref[0], g_ref[...]
        r = lax.rsqrt(jnp.mean(o * o, axis=-1, keepdims=True) + EPS)
        sg = _sigmoid(z)
        dn = d * (z * sg)
        dz_ref[0] = d * (o * r * g) * (sg * (1.0 + z * (1.0 - sg)))
        dg_ref[...] += jnp.sum(dn * o * r, axis=0, keepdims=True)
        u = dn * g
        do_ref[0] = r * u - o * (r * r * r) * jnp.mean(u * o, axis=-1, keepdims=True)

    tile = pl.BlockSpec((1, T, 128), lambda b, h: (b, 0, h))
    row = pl.BlockSpec((1, 128), lambda b, h: (0, 0))
    return pl.pallas_call(
        body, grid=(B, DN_HEADS),
        in_specs=[tile, pl.BlockSpec((1, T, 128), lambda b, h: (b, 0, COL_Z + h)), row,
                  pl.BlockSpec((1, T, 128), lambda b, h: (b, 0, SB_W // 128 + h))],
        out_specs=[tile, tile, row],
        out_shape=[_sds((B, T, DN_W), F32), _sds((B, T, DN_W), F32), _sds((1, 128), F32)],
        compiler_params=_cp(("arbitrary", "arbitrary")), name=name,
    )(ob, proj3, gain, dmix3)


def _tri_inv(low, ri, ci):
    m = (ri == ci).astype(F32) - jnp.where(((ri >> 1) == (ci >> 1)) & (ri > ci), low, 0.0)
    s = 2
    while s < DN_C:
        sh = s.bit_length()
        off = ((ri >> sh) == (ci >> sh)) & ((ri & (2 * s - 1)) >= s) & ((ci & (2 * s - 1)) < s)
        m = m - _pdot(m, _pdot(jnp.where(off, low, 0.0), m, _BNN), _BNN)
        s *= 2
    return m


_BNN = (((2,), (1,)), ((0,), (0,)))
_BNT = (((2,), (2,)), ((0,), (0,)))
_BTN = (((1,), (1,)), ((0,), (0,)))
DN_G = 8


def _chunk_common(q, k, v, gt, h, tm=None):
    C = DN_C
    G = q.shape[0]
    ri = lax.broadcasted_iota(jnp.int32, (C, C), 0)
    ci = lax.broadcasted_iota(jnp.int32, (C, C), 1)
    lane = lax.broadcasted_iota(jnp.int32, (C, 128), 1)
    incl, strict = ri >= ci, ri > ci
    g = jnp.sum(jnp.where(lane == h, gt, 0.0), axis=2, keepdims=True)
    beta = jnp.sum(jnp.where(lane == h + DN_HEADS, gt, 0.0), axis=2, keepdims=True)
    ones = jnp.ones((G, C, 128), F32)
    inclf = jnp.broadcast_to(incl.astype(F32), (G, C, C))
    gc = _pdot(inclf, g * ones, _BNN)[:, :, :1]
    gcr = _pdot(jnp.ones((G, C, C), F32), jnp.where(ri == ci, gc, 0.0), _BNN)
    decay = jnp.where(incl, jnp.exp(jnp.where(incl, gc - gcr, 0.0)), 0.0)
    e = jnp.exp(gc)
    kb, vb = k * beta, v * beta
    kk = _bdot(kb, k, _BNT)
    if tm is None:
        tm = _tri_inv(jnp.where(strict, kk * decay, 0.0), ri, ci)
    kbe = kb * e
    u = _bdot(tm, vb, _BNN)
    w = _bdot(tm, kbe, _BNN)
    qk = _bdot(q, k, _BNT)
    intra = jnp.where(incl, qk * decay, 0.0)
    gl = gc[:, C - 1:C, :]
    el = jnp.exp(gl)
    r = jnp.exp(gl - gc)
    return dict(lane=lane, incl=incl, inclf=inclf, strict=strict, beta=beta, decay=decay, e=e,
                kb=kb, vb=vb, kk=kk, tm=tm, kbe=kbe, u=u, w=w, qk=qk, intra=intra, el=el, r=r, ones=ones)


def gdn_chunk_fwd(qkv, gates, name):
    B, _, T, _ = qkv.shape
    NC = T // DN_C

    G = DN_G if NC % DN_G == 0 else 1
    GC = G * DN_C

    GS = G * DN_HD

    def body(x_ref, gt_ref, o_ref, st_ref, tm_ref, s_s, p_s, b_s, qp_s, el_s):
        h = pl.program_id(1)

        def group_a(gi, _):
            rows = pl.ds(pl.multiple_of(gi * GC, GC), GC)
            srow = pl.ds(pl.multiple_of(gi * GS, GS), GS)
            q, k, v = [x_ref[0, i, rows, :].reshape(G, DN_C, DN_HD) for i in range(3)]
            c = _chunk_common(q, k, v, gt_ref[0, rows, :].reshape(G, DN_C, 128), h)
            kr = k * c["r"]
            tm_ref[0, 0, rows, :] = c["tm"].reshape(GC, DN_C)
            p_s[srow, :] = _bdot(kr, c["w"], _BTN).reshape(GS, DN_HD)
            b_s[srow, :] = _bdot(kr, c["u"], _BTN).reshape(GS, DN_HD)
            qp_s[rows, :] = (q * c["e"] - _bdot(c["intra"], c["w"], _BNN)).reshape(GC, DN_HD)
            o_ref[0, rows, :] = _bdot(c["intra"], c["u"], _BNN).reshape(GC, DN_HD)
            el_s[pl.ds(gi * G, G), :, :] = c["el"] * jnp.ones((G, 1, 128), F32)
            return 0

        lax.fori_loop(0, NC // G, group_a, 0)
        s_s[...] = jnp.zeros_like(s_s)

        def chunk(n, _):
            srow = pl.ds(pl.multiple_of(n * DN_HD, DN_HD), DN_HD)
            st = s_s[...]
            st_ref[0, 0, srow, :] = st
            s_s[...] = (st * el_s[n] + b_s[srow, :]) - _bdot(p_s[srow, :], st)
            return 0

        lax.fori_loop(0, NC, chunk, 0)

        def group_c(gi, _):
            rows = pl.ds(pl.multiple_of(gi * GC, GC), GC)
            srow = pl.ds(pl.multiple_of(gi * GS, GS), GS)
            st = st_ref[0, 0, srow, :].reshape(G, DN_HD, DN_HD)
            o_ref[0, rows, :] += _bdot(qp_s[rows, :].reshape(G, DN_C, DN_HD), st, _BNN).reshape(GC, DN_HD)
            return 0

        lax.fori_loop(0, NC // G, group_c, 0)

    return pl.pallas_call(
        body, grid=(B, DN_HEADS),
        in_specs=[pl.BlockSpec((1, 3, T, 128), lambda b, h: (b, 0, 0, h)), pl.BlockSpec((1, T, 128), lambda b, h: (b, 0, 0))],
        out_specs=[pl.BlockSpec((1, T, 128), lambda b, h: (b, 0, h)), pl.BlockSpec((1, 1, NC * DN_HD, DN_HD), lambda b, h: (b, h, 0, 0)),
                   pl.BlockSpec((1, 1, T, DN_C), lambda b, h: (b, h, 0, 0))],
        out_shape=[_sds((B, T, DN_W), F32), _sds((B, DN_HEADS, NC * DN_HD, DN_HD), F32), _sds((B, DN_HEADS, T, DN_C), F32)],
        scratch_shapes=[pltpu.VMEM((DN_HD, DN_HD), F32)] + [pltpu.VMEM((NC * DN_HD, DN_HD), F32)] * 2
        + [pltpu.VMEM((T, DN_HD), F32), pltpu.VMEM((NC, 1, 128), F32)],
        compiler_params=_cp(("parallel", "parallel")), name=name,
    )(qkv, gates)


def gdn_chunk_bwd(qkv, gates, states, tms, dob, name):
    B, _, T, _ = qkv.shape
    NC = T // DN_C
    C = DN_C

    G = DN_G if NC % DN_G == 0 else 1
    GC = G * C

    GS = G * DN_HD

    def body(x_ref, gt_ref, st_ref, tm_ref, do_ref, dx_ref, dgt_ref, ds_s, p_s, r_s, el_s, dsa_s):
        h = pl.program_id(1)

        @pl.when(h == 0)
        def _():
            dgt_ref[...] = jnp.zeros_like(dgt_ref)

        def load(gi):
            rows = pl.ds(pl.multiple_of(gi * GC, GC), GC)
            q, k, v = [x_ref[0, i, rows, :].reshape(G, C, DN_HD) for i in range(3)]
            return rows, q, k, v, gt_ref[0, rows, :].reshape(G, C, 128), tm_ref[0, 0, rows, :].reshape(G, C, C)

        def group_a(gi, _):
            rows, q, k, v, gt, tm = load(gi)
            srow = pl.ds(pl.multiple_of(gi * GS, GS), GS)
            c = _chunk_common(q, k, v, gt, h, tm=tm)
            qp = q * c["e"] - _bdot(c["intra"], c["w"], _BNN)
            p_s[srow, :] = _bdot(k * c["r"], c["w"], _BTN).reshape(GS, DN_HD)
            r_s[srow, :] = _bdot(qp, do_ref[0, rows, :].reshape(G, C, DN_HD), _BTN).reshape(GS, DN_HD)
            el_s[pl.ds(gi * G, G), :, :] = c["el"] * jnp.ones((G, 1, 128), F32)
            return 0

        lax.fori_loop(0, NC // G, group_a, 0)
        ds_s[...] = jnp.zeros_like(ds_s)

        def chunk(m, _):
            n = NC - 1 - m
            srow = pl.ds(pl.multiple_of(n * DN_HD, DN_HD), DN_HD)
            dsn = ds_s[...]
            dsa_s[srow, :] = dsn
            ds_s[...] = (dsn * el_s[n] + r_s[srow, :]) - _bdot(p_s[srow, :], dsn, _TN)
            return 0

        lax.fori_loop(0, NC, chunk, 0)

        def group_c(gi, _):
            rows, q, k, v, gt, tm = load(gi)
            c = _chunk_common(q, k, v, gt, h, tm=tm)
            incl, strict, decay, e, r, el, tm = c["incl"], c["strict"], c["decay"], c["e"], c["r"], c["el"], c["tm"]
            srow = pl.ds(pl.multiple_of(gi * GS, GS), GS)
            st = st_ref[0, 0, srow, :].reshape(G, DN_HD, DN_HD)
            dsn = dsa_s[srow, :].reshape(G, DN_HD, DN_HD)
            do = do_ref[0, rows, :].reshape(G, C, DN_HD)
            dvn = _bdot(k * r, dsn, _BNN) + _bdot(c["intra"], do, _BTN)
            v_new = c["u"] - _bdot(c["w"], st, _BNN)
            del_ = jnp.sum(jnp.sum(dsn * st, axis=2, keepdims=True), axis=1, keepdims=True)
            dkr = _bdot(v_new, dsn, _BNT)
            dqe = _bdot(do, st, _BNT)
            dintra = _bdot(do, v_new, _BNT)
            dw = -_bdot(dvn, st, _BNT)
            dqkd = jnp.where(incl, dintra, 0.0)
            dqk = dqkd * decay
            ddecay = dqkd * c["qk"]
            dq = dqe * e + _bdot(dqk, k, _BNN)
            dk = dkr * r + _bdot(dqk, q, _BTN)
            dtm = _bdot(dvn, c["vb"], _BNT) + _bdot(dw, c["kbe"], _BNT)
            dvb = _bdot(tm, dvn, _BTN)
            dkbe = _bdot(tm, dw, _BTN)
            dkb = dkbe * e
            de = jnp.sum(dqe * q, axis=2, keepdims=True) + jnp.sum(dkbe * c["kb"], axis=2, keepdims=True)
            da = -_pdot(tm, _pdot(dtm, tm, _BNT), _BTN)
            dlow = jnp.where(strict, da, 0.0)
            dkk = dlow * decay
            ddecay = ddecay + dlow * c["kk"]
            dkb = dkb + _bdot(dkk, k, _BNN)
            dk = dk + _bdot(dkk, c["kb"], _BTN) + dkb * c["beta"]
            dbeta = jnp.sum(dkb * k, axis=2, keepdims=True) + jnp.sum(dvb * v, axis=2, keepdims=True)
            dv = dvb * c["beta"]
            dd = ddecay * decay
            dgc = jnp.sum(dd, axis=2, keepdims=True) - _pdot(dd, c["ones"], _BTN)[:, :, :1]
            dr = jnp.sum(dkr * k, axis=2, keepdims=True)
            dgc = dgc + de * e - dr * r
            dgl = jnp.sum(dr * r, axis=1, keepdims=True) + del_ * el
            rowc = lax.broadcasted_iota(jnp.int32, (C, 1), 0)
            dgc = dgc + jnp.where(rowc == C - 1, dgl, 0.0)
            dg = _pdot(c["inclf"], dgc * c["ones"], _BTN)[:, :, :1]
            dx_ref[0, 0, rows, :] = dq.reshape(GC, DN_HD)
            dx_ref[0, 1, rows, :] = dk.reshape(GC, DN_HD)
            dx_ref[0, 2, rows, :] = dv.reshape(GC, DN_HD)
            lane = c["lane"]
            dgt_ref[0, rows, :] += (jnp.where(lane == h, dg, 0.0) + jnp.where(lane == h + DN_HEADS, dbeta, 0.0)).reshape(GC, 128)
            return 0

        lax.fori_loop(0, NC // G, group_c, 0)

    return pl.pallas_call(
        body, grid=(B, DN_HEADS),
        in_specs=[pl.BlockSpec((1, 3, T, 128), lambda b, h: (b, 0, 0, h)), pl.BlockSpec((1, T, 128), lambda b, h: (b, 0, 0)),
                  pl.BlockSpec((1, 1, NC * DN_HD, DN_HD), lambda b, h: (b, h, 0, 0)), pl.BlockSpec((1, 1, T, C), lambda b, h: (b, h, 0, 0)),
                  pl.BlockSpec((1, T, 128), lambda b, h: (b, 0, h))],
        out_specs=[pl.BlockSpec((1, 3, T, 128), lambda b, h: (b, 0, 0, h)), pl.BlockSpec((1, T, 128), lambda b, h: (b, 0, 0))],
        out_shape=[_sds((B, 3, T, DN_W), F32), _sds((B, T, 128), F32)],
        scratch_shapes=[pltpu.VMEM((DN_HD, DN_HD), F32)] + [pltpu.VMEM((NC * DN_HD, DN_HD), F32)] * 2
        + [pltpu.VMEM((NC, 1, 128), F32), pltpu.VMEM((NC * DN_HD, DN_HD), F32)],
        compiler_params=_cp(("parallel", "arbitrary")), name=name,
    )(qkv, gates, states, tms, dob)


def ada_fwd(c_all, w_ada, b_sl, name):
    L, D, W = w_ada.shape
    NBt = c_all.shape[0]

    def body(c_ref, w_ref, b_ref, o_ref):
        cv = c_ref[...]
        o_ref[0] = _pdot(cv * _sigmoid(cv), w_ref[0]) + b_ref[0]

    return pl.pallas_call(
        body, grid=(L,),
        in_specs=[pl.BlockSpec((NBt, D), lambda l: (0, 0)), pl.BlockSpec((1, D, W), lambda l: (l, 0, 0)), pl.BlockSpec((1, 1, W), lambda l: (l, 0, 0))],
        out_specs=pl.BlockSpec((1, NBt, W), lambda l: (l, 0, 0)),
        out_shape=_sds((L, NBt, W), F32),
        compiler_params=_cp(("parallel",)), name=name,
    )(c_all, w_ada, b_sl)


def ada_bwd(c_all, dmod_cols, name):
    L, NBt, W = dmod_cols.shape
    D = c_all.shape[1]

    def body(c_ref, d_ref, o_ref):
        cv = c_ref[...]
        o_ref[0] = _pdot(cv * _sigmoid(cv), d_ref[0], _TN)

    return pl.pallas_call(
        body, grid=(L,),
        in_specs=[pl.BlockSpec((NBt, D), lambda l: (0, 0)), pl.BlockSpec((1, NBt, W), lambda l: (l, 0, 0))],
        out_specs=pl.BlockSpec((1, D, W), lambda l: (l, 0, 0)),
        out_shape=_sds((L, D, W), F32),
        compiler_params=_cp(("parallel",)), name=name,
    )(c_all, dmod_cols)


def adamw(partials, w, m, v, name):
    P, R, C = partials.shape
    tr = _pick(R, 256)

    def body(p_ref, w_ref, m_ref, v_ref, g_ref, d_ref, nm_ref, nv_ref):
        g = p_ref[0].astype(F32)
        for i in range(1, P):
            g = g + p_ref[i].astype(F32)
        nm = ADAM_B1 * m_ref[...] + (1.0 - ADAM_B1) * g
        nv = ADAM_B2 * v_ref[...] + (1.0 - ADAM_B2) * (g * g)
        m_hat = nm / (1.0 - ADAM_B1 ** ADAM_STEP)
        v_hat = nv / (1.0 - ADAM_B2 ** ADAM_STEP)
        g_ref[...] = g
        d_ref[...] = -ADAM_LR * (m_hat / (jnp.sqrt(v_hat) + ADAM_EPS) + ADAM_WD * w_ref[...])
        nm_ref[...] = nm
        nv_ref[...] = nv

    tile = pl.BlockSpec((tr, C), lambda i: (i, 0))
    return pl.pallas_call(
        body, grid=(R // tr,),
        in_specs=[pl.BlockSpec((P, tr, C), lambda i: (0, i, 0)), tile, tile, tile],
        out_specs=[tile] * 4, out_shape=[_sds((R, C), F32)] * 4,
        compiler_params=_cp(("parallel",)), name=name,
    )(partials, w, m, v)


def _coords():
    return lax.axis_index("x"), lax.axis_index("y"), lax.axis_index("c")


def all_gather(x, name):
    any_spec = pl.BlockSpec(memory_space=pl.ANY)

    def body(x_ref, out_ref, send_sems, recv_sems, local_sem):
        x_, y_, c_ = _coords()
        me, sibling = (x_, y_, c_), (x_, y_, 1 - c_)
        chips = [(1 - x_, y_), (x_, 1 - y_), (1 - x_, 1 - y_)]

        def slot(px, py, pc):
            return out_ref.at[4 * px + 2 * py + pc]

        def copy(k, block, to, src=None):
            return pltpu.make_async_remote_copy(
                src_ref=slot(*block) if src is None else src, dst_ref=slot(*block),
                send_sem=send_sems.at[k], recv_sem=recv_sems.at[k],
                device_id=to, device_id_type=pl.DeviceIdType.MESH)

        mine = pltpu.make_async_copy(x_ref, slot(*me), local_sem)
        mine.start()
        first = [copy(0, me, sibling, src=x_ref)]
        first += [copy(1 + j, me, (*chip, c_), src=x_ref) for j, chip in enumerate(chips)]
        for cp in first:
            cp.start()
        passed = [copy(4 + j, (*chip, c_), sibling) for j, chip in enumerate(chips)]
        for j, chip in enumerate(chips):
            copy(1 + j, (*chip, c_), me).wait_recv()
            passed[j].start()
        copy(0, sibling, me).wait_recv()
        for j, chip in enumerate(chips):
            copy(4 + j, (*chip, 1 - c_), me).wait_recv()
        for cp in first + passed:
            cp.wait_send()
        mine.wait()

    return pl.pallas_call(
        body, out_shape=_sds((N_DEV,) + x.shape, x.dtype),
        in_specs=[any_spec], out_specs=any_spec,
        scratch_shapes=[pltpu.SemaphoreType.DMA((7,)), pltpu.SemaphoreType.DMA((7,)), pltpu.SemaphoreType.DMA],
        name=name,
    )(x)


def all_to_all(x, name):
    any_spec = pl.BlockSpec(memory_space=pl.ANY)

    def body(x_ref, out_ref, send_sems, recv_sems, local_sem):
        x_, y_, c_ = _coords()
        me = 4 * x_ + 2 * y_ + c_
        mine = pltpu.make_async_copy(x_ref.at[me], out_ref.at[me], local_sem)
        mine.start()
        copies = []
        for k in range(1, N_DEV):
            px = 1 - x_ if k & 4 else x_
            py = 1 - y_ if k & 2 else y_
            pc = 1 - c_ if k & 1 else c_
            peer = 4 * px + 2 * py + pc
            copies.append((pltpu.make_async_remote_copy(
                src_ref=x_ref.at[peer], dst_ref=out_ref.at[me],
                send_sem=send_sems.at[k - 1], recv_sem=recv_sems.at[k - 1],
                device_id=(px, py, pc), device_id_type=pl.DeviceIdType.MESH), peer))
        for cp, _ in copies:
            cp.start()
        for k, (cp, peer) in enumerate(copies):
            pltpu.make_async_remote_copy(
                src_ref=x_ref.at[peer], dst_ref=out_ref.at[peer],
                send_sem=send_sems.at[k], recv_sem=recv_sems.at[k],
                device_id=(x_, y_, c_), device_id_type=pl.DeviceIdType.MESH).wait_recv()
        for cp, _ in copies:
            cp.wait_send()
        mine.wait()

    return pl.pallas_call(
        body, out_shape=_sds(x.shape, x.dtype),
        in_specs=[any_spec], out_specs=any_spec,
        scratch_shapes=[pltpu.SemaphoreType.DMA((7,)), pltpu.SemaphoreType.DMA((7,)), pltpu.SemaphoreType.DMA],
        name=name,
    )(x)


def _rows128(a):
    return a.reshape(-1, 128)


def _pad_lanes(a):
    return jnp.pad(a, ((0, 0), (0, 128 - a.shape[1])))


def kernel(x, c, w_ada, b_ada, norm_mix, norm_mlp, w_in, sb_q_norm, sb_k_norm, conv_w, a_log, dt_bias, dn_out_norm, w_out, w_ff1, w_ff2, loss_target, m_w_ada, m_b_ada, m_norm_mix, m_norm_mlp, m_w_in, m_sb_q_norm, m_sb_k_norm, m_conv_w, m_a_log, m_dt_bias, m_dn_out_norm, m_w_out, m_w_ff1, m_w_ff2, v_w_ada, v_b_ada, v_norm_mix, v_norm_mlp, v_w_in, v_sb_q_norm, v_sb_k_norm, v_conv_w, v_a_log, v_dt_bias, v_dn_out_norm, v_w_out, v_w_ff1, v_w_ff2):
    B, T, D = x.shape
    L = w_ada.shape[0]
    N = B * T
    FF = w_ff1.shape[2] * N_DEV
    WA = w_ada.shape[2]
    CS = conv_w.shape[2]
    me = 4 * lax.axis_index("x") + 2 * lax.axis_index("y") + lax.axis_index("c")
    tm = _pick(T, 1024)

    win_g = all_gather(w_in.astype(BF16), "comm_gather_w_in")
    W_in = jnp.pad(win_g.transpose(1, 2, 0, 3).reshape(L, D, IN_W), ((0, 0), (0, 0), (0, IN_WP - IN_W)))
    W_out = all_gather(w_out.astype(BF16), "comm_gather_w_out").transpose(1, 0, 2, 3).reshape(L, SB_W + DN_W, D)
    W_1 = all_gather(w_ff1.astype(BF16), "comm_gather_w_ff1").transpose(1, 2, 0, 3).reshape(L, D, FF)
    W_2 = all_gather(w_ff2.astype(BF16), "comm_gather_w_ff2").transpose(1, 0, 2, 3).reshape(L, FF, D)
    conv_full = all_gather(conv_w, "comm_gather_conv").transpose(1, 2, 0, 3).reshape(L, CONV_K, 3 * DN_W)

    c_all = all_gather(c, "comm_gather_c").reshape(N_DEV * B, D)
    b_sl = lax.dynamic_slice_in_dim(b_ada, me * WA, WA, axis=1).reshape(L, 1, WA)
    mod_sh = ada_fwd(c_all, w_ada, b_sl, "ada_fwd")
    mod_g = all_gather(mod_sh, "comm_gather_mod")
    mod = lax.dynamic_slice_in_dim(mod_g, me * B, B, axis=2).transpose(1, 2, 0, 3).reshape(L, B, 6 * D)

    def mod_part(l, i):
        return mod[l, :, i * D:(i + 1) * D].reshape(B, 1, D)

    alog_row = _pad_lanes(a_log).reshape(L, 1, 128)
    dtb_row = _pad_lanes(dt_bias).reshape(L, 1, 128)

    def gate_epi(acc, xv, g):
        return acc, xv + g[0] * acc

    def relu2(a):
        r = jnp.maximum(a, 0.0)
        return r * r

    def times_gate(a, g):
        return a * g[0]

    tile_ij = lambda i, j, k: (i, j)

    saved = []
    xc = x
    for l in range(L):
        sh_a, sc_a, g_a, sh_m, sc_m, g_m = [mod_part(l, i) for i in range(6)]
        h = ln_mod_fwd(xc, norm_mix[l:l + 1], sc_a, sh_a, "ln_mod_fwd")
        proj3 = matmul(h.reshape(N, D), W_in[l], mode="nn", name="mm_proj", tm=256)[0].reshape(B, T, IN_WP)
        o_a, tot = sb_attn_fwd(proj3, sb_q_norm[l:l + 1], sb_k_norm[l:l + 1], "sb_attn_fwd")
        qkv = gdn_pre_fwd(proj3, conv_full[l], "gdn_pre_fwd")
        gates = gdn_gates_fwd(proj3, alog_row[l], dtb_row[l], "gdn_gates_fwd")
        ob, states, tms = gdn_chunk_fwd(qkv, gates, "gdn_chunk_fwd")
        o_b = gdn_post_fwd(ob, proj3, dn_out_norm[l:l + 1], "gdn_post_fwd")
        mix = jnp.concatenate([o_a, o_b], axis=-1)
        y1, x_mid = matmul(
            mix.reshape(N, SB_W + DN_W), W_out[l], mode="nn", name="mm_out", out_dtypes=(F32, F32), tm=tm, epi=gate_epi,
            extras=[(xc.reshape(N, D), (tm, _pick(D, 1024)), tile_ij),
                    (g_a, (1, 1, _pick(D, 1024)), lambda i, j, k: (i * tm // T, 0, j))])
        x_mid = x_mid.reshape(B, T, D)
        h2 = ln_mod_fwd(x_mid, norm_mlp[l:l + 1], sc_m, sh_m, "ln_mod_fwd")
        u = matmul(h2.reshape(N, D), W_1[l], mode="nn", name="mm_ff1")[0]
        y2, x_out = matmul(
            u, W_2[l], mode="nn", name="mm_ff2", out_dtypes=(F32, F32), tm=tm, a_fn=relu2, epi=gate_epi,
            extras=[(x_mid.reshape(N, D), (tm, _pick(D, 1024)), tile_ij),
                    (g_m, (1, 1, _pick(D, 1024)), lambda i, j, k: (i * tm // T, 0, j))])
        saved.append(dict(x=xc, h=h, proj3=proj3, tot=tot, qkv=qkv, gates=gates, states=states, tms=tms, ob=ob, mix=mix,
                          y1=y1, x_mid=x_mid, h2=h2, u=u, y2=y2))
        xc = x_out.reshape(B, T, D)

    dx, sq = loss_grad(xc, loss_target, "loss_grad")
    loss = lax.psum((0.5 / D) * jnp.sum(sq), AXES)

    g_win, g_wout, g_w1, g_w2, dmods, smalls = [], [], [], [], [], []
    tk_tok = tm
    for l in reversed(range(L)):
        s = saved[l]
        sh_a, sc_a, g_a, sh_m, sc_m, g_m = [mod_part(l, i) for i in range(6)]
        gate_k = lambda g, blk: (g, (1, 1, blk), lambda i, j, k: (i * tm // T, 0, k))
        gate_tok = lambda g, blk: (g, (1, 1, blk), lambda i, j, k: (k * tk_tok // T, 0, j))
        dx2 = dx.reshape(N, D)
        dg_m = rowsum_prod(dx, s["y2"].reshape(B, T, D), "rowsum_prod")
        du = matmul(dx2, W_2[l], mode="nt", name="mm_ff2_da", out_dtypes=(BF16,), tm=tm, a_fn=times_gate,
                    a_extras=[gate_k(g_m, _pick(D, 1024))],
                    epi=lambda acc, uv: (acc * (2.0 * jnp.maximum(uv, 0.0)),),
                    extras=[(s["u"], (tm, _pick(FF, 1024)), tile_ij)])[0]
        g_w2.append(matmul(s["u"], dx2, mode="tn", name="mm_ff2_dw", out_dtypes=(BF16,), tk=tk_tok, a_fn=relu2,
                           b_fn=times_gate, b_extras=[gate_tok(g_m, _pick(D, 1024))])[0])
        g_w1.append(matmul(s["h2"].reshape(N, D), du, mode="tn", name="mm_ff1_dw", out_dtypes=(BF16,))[0])
        dh2 = matmul(du, W_1[l], mode="nt", name="mm_ff1_da")[0]
        dx_mid, dgn_mlp, dsc_m, dsh_m = ln_mod_bwd(s["x_mid"], norm_mlp[l:l + 1], sc_m, dh2.reshape(B, T, D), dx, "ln_mod_bwd")
        dxm2 = dx_mid.reshape(N, D)
        dg_a = rowsum_prod(dx_mid, s["y1"].reshape(B, T, D), "rowsum_prod")
        dmix3 = matmul(dxm2, W_out[l], mode="nt", name="mm_out_da", tm=tm, a_fn=times_gate,
                       a_extras=[gate_k(g_a, _pick(D, 1024))])[0].reshape(B, T, SB_W + DN_W)
        g_wout.append(matmul(s["mix"].reshape(N, SB_W + DN_W), dxm2, mode="tn", name="mm_out_dw", out_dtypes=(BF16,),
                             tk=tk_tok, b_fn=times_gate, b_extras=[gate_tok(g_a, _pick(D, 1024))])[0])
        dq_a, dk_a, dv_a, dgq, dgk = sb_attn_bwd(s["proj3"], sb_q_norm[l:l + 1], sb_k_norm[l:l + 1], s["tot"], dmix3, "sb_attn_bwd")
        dob, dz, dgn_dn = gdn_post_bwd(s["ob"], s["proj3"], dn_out_norm[l:l + 1], dmix3, "gdn_post_bwd")
        dqkv, dgates = gdn_chunk_bwd(s["qkv"], s["gates"], s["states"], s["tms"], dob, "gdn_chunk_bwd")
        d_dnqkv, dconv_b = gdn_pre_bwd(s["proj3"], conv_full[l], dqkv, "gdn_pre_bwd")
        d_ab, dalog, ddtb = gdn_gates_bwd(s["proj3"], alog_row[l], dtb_row[l], dgates, "gdn_gates_bwd")
        dproj = jnp.concatenate([dq_a, dk_a, dv_a, d_dnqkv, dz, d_ab], axis=-1).reshape(N, IN_WP)
        g_win.append(matmul(s["h"].reshape(N, D), dproj, mode="tn", name="mm_proj_dw", out_dtypes=(BF16,), tm=512, tk=512)[0])
        dh = matmul(dproj, W_in[l], mode="nt", name="mm_proj_da", tm=512, tk=IN_WP)[0]
        dx, dgn_mix, dsc_a, dsh_a = ln_mod_bwd(s["x"], norm_mix[l:l + 1], sc_a, dh.reshape(B, T, D), dx_mid, "ln_mod_bwd")
        dmods.append(jnp.concatenate([dsh_a, dsc_a, dg_a, dsh_m, dsc_m, dg_m], axis=-1).reshape(B, 6 * D))
        smalls.append(dict(norm_mix=dgn_mix, norm_mlp=dgn_mlp, sbq=dgq, sbk=dgk, alog=dalog, dtb=ddtb, dnorm=dgn_dn,
                           conv=jnp.sum(dconv_b, axis=0)))
    for lst in (g_win, g_wout, g_w1, g_w2, dmods, smalls):
        lst.reverse()
    grad_x = dx

    def shard_cols(g, n):
        return g.reshape(L, g.shape[1], N_DEV, n).transpose(2, 0, 1, 3)

    def shard_rows(g, n):
        return g.reshape(L, N_DEV, n, g.shape[2]).transpose(1, 0, 2, 3)

    def update(parts, w, m, v, name):
        shp = w.shape
        r2 = lambda a: a.reshape(-1, shp[-1])
        outs = adamw(parts.reshape(parts.shape[0], -1, shp[-1]), r2(w), r2(m), r2(v), name)
        return [o.reshape(shp) for o in outs]

    wi = w_in.shape[2]
    p_win = all_to_all(shard_cols(jnp.stack(g_win)[:, :, :IN_W], wi), "comm_scatter_w_in")
    p_wout = all_to_all(shard_rows(jnp.stack(g_wout), w_out.shape[1]), "comm_scatter_w_out")
    p_w1 = all_to_all(shard_cols(jnp.stack(g_w1), w_ff1.shape[2]), "comm_scatter_w_ff1")
    p_w2 = all_to_all(shard_rows(jnp.stack(g_w2), w_ff2.shape[1]), "comm_scatter_w_ff2")
    r_win = update(p_win, w_in, m_w_in, v_w_in, "adamw_w_in")
    r_wout = update(p_wout, w_out, m_w_out, v_w_out, "adamw_w_out")
    r_w1 = update(p_w1, w_ff1, m_w_ff1, v_w_ff1, "adamw_w_ff1")
    r_w2 = update(p_w2, w_ff2, m_w_ff2, v_w_ff2, "adamw_w_ff2")

    dmod_g = all_gather(jnp.stack(dmods), "comm_gather_dmod")
    dmod_all = dmod_g.transpose(1, 0, 2, 3).reshape(L, N_DEV * B, 6 * D)
    g_wada = ada_bwd(c_all, lax.dynamic_slice_in_dim(dmod_all, me * WA, WA, axis=2), "ada_bwd")
    r_wada = update(g_wada[None], w_ada, m_w_ada, v_w_ada, "adamw_w_ada")
    r_bada = update(dmod_g.transpose(0, 2, 1, 3).reshape(N_DEV * B, L, 6 * D), b_ada, m_b_ada, v_b_ada, "adamw_b_ada")

    def pack(f):
        return jnp.concatenate([
            _rows128(f("norm_mix")), _rows128(f("norm_mlp")), _rows128(f("sbq")), _rows128(f("sbk")),
            f("alog"), f("dtb"), f("dnorm"), _rows128(f("conv"))], axis=0)

    names = ["norm_mix", "norm_mlp", "sbq", "sbk", "alog", "dtb", "dnorm"]
    part = pack(lambda n: jnp.concatenate([sm[n] for sm in smalls], axis=0))
    n_rep = part.shape[0] - L * CONV_K * 3 * DN_W // 128
    part_g = all_gather(part, "comm_gather_small")
    params = dict(norm_mix=(norm_mix, m_norm_mix, v_norm_mix), norm_mlp=(norm_mlp, m_norm_mlp, v_norm_mlp),
                  sbq=(sb_q_norm, m_sb_q_norm, v_sb_q_norm), sbk=(sb_k_norm, m_sb_k_norm, v_sb_k_norm),
                  alog=(a_log, m_a_log, v_a_log), dtb=(dt_bias, m_dt_bias, v_dt_bias),
                  dnorm=(dn_out_norm, m_dn_out_norm, v_dn_out_norm))

    def rows_of(n, a):
        return _pad_lanes(a) if n in ("alog", "dtb") else _rows128(a)

    packed = [jnp.concatenate([rows_of(n, params[n][i]) for n in names], axis=0) for i in range(3)]
    r_small = adamw(part_g[:, :n_rep], packed[0], packed[1], packed[2], "adamw_small")
    small_out = {}
    off = 0
    for n in names:
        w0 = params[n][0]
        nr = rows_of(n, w0).shape[0]
        vals = [o[off:off + nr] for o in r_small]
        small_out[n] = [(vv[:, :w0.shape[1]] if n in ("alog", "dtb") else vv.reshape(w0.shape)) for vv in vals]
        off += nr
    conv_parts = part_g[:, n_rep:].reshape(N_DEV, L, CONV_K, 3 * DN_W)
    r_conv = update(lax.dynamic_slice_in_dim(conv_parts, me * CS, CS, axis=3), conv_w, m_conv_w, v_conv_w, "adamw_conv")

    order = [r_wada, r_bada, small_out["norm_mix"], small_out["norm_mlp"], r_win, small_out["sbq"], small_out["sbk"],
             r_conv, small_out["alog"], small_out["dtb"], small_out["dnorm"], r_wout, r_w1, r_w2]
    outs = [loss, grad_x]
    for i in range(4):
        outs += [r[i] for r in order]
    return tuple(outs)
```

```python
import functools
import math

import jax
import jax.numpy as jnp
from jax import lax
from jax.experimental import pallas as pl
from jax.experimental.pallas import tpu as pltpu

F32 = jnp.float32
BF16 = jnp.bfloat16
EPS = 1e-6
N_DEV = 8
AXES = ("x", "y", "c")

SB_HEADS, SB_HD = 8, 64
SB_W = SB_HEADS * SB_HD
SB_BLK = 128
DN_HEADS, DN_HD = 4, 128
DN_W = DN_HEADS * DN_HD
DN_C = 64
CONV_K = 4
IN_W = 3 * SB_W + 4 * DN_W + 2 * DN_HEADS
IN_WP = 3 * SB_W + 4 * DN_W + 128
COL_DNQKV = 3 * SB_W // 128
COL_Z = COL_DNQKV + 3 * DN_W // 128
COL_AB = COL_Z + DN_W // 128

ADAM_LR, ADAM_B1, ADAM_B2, ADAM_EPS, ADAM_WD, ADAM_STEP = 0.001, 0.9, 0.999, 1e-08, 0.01, 10

VMEM_LIMIT = 56 * 1024 * 1024


def _cp(sem):
    return pltpu.CompilerParams(dimension_semantics=sem, vmem_limit_bytes=VMEM_LIMIT)


def _pick(dim, pref):
    return pref if dim % pref == 0 else dim


def _sds(shape, dtype):
    return jax.ShapeDtypeStruct(tuple(shape), dtype)


_NN = (((1,), (0,)), ((), ()))
_NT = (((1,), (1,)), ((), ()))
_TN = (((0,), (0,)), ((), ()))


def _bdot(a, b, dims=_NN):
    return lax.dot_general(a.astype(BF16), b.astype(BF16), dims, preferred_element_type=F32)


def _split(a):
    hi = a.astype(BF16)
    lo = (a - hi.astype(F32)).astype(BF16)
    return hi, lo


def _pdot(a, b, dims=_NN):
    ah, al = _split(a)
    bh, bl = _split(b)
    d = functools.partial(lax.dot_general, dimension_numbers=dims, preferred_element_type=F32)
    return d(ah, bh) + (d(ah, bl) + d(al, bh))


def _sigmoid(x):
    return 1.0 / (1.0 + jnp.exp(-x))


def _softplus(x):
    return jnp.maximum(x, 0.0) + jnp.log(1.0 + jnp.exp(-jnp.abs(x)))


def matmul(a, b, *, mode, name, out_dtypes=(F32,), a_fn=None, a_extras=(), b_fn=None, b_extras=(),
           epi=None, extras=(), tm=1024, tn=1024, tk=1024):
    if mode == "tn":
        K, M = a.shape
    else:
        M, K = a.shape
    N = b.shape[0] if mode == "nt" else b.shape[1]
    tm, tn, tk = _pick(M, tm), _pick(N, tn), _pick(K, tk)
    nk = K // tk
    dims = {"nn": _NN, "nt": _NT, "tn": _TN}[mode]
    a_spec = pl.BlockSpec((tk, tm), lambda i, j, k: (k, i)) if mode == "tn" else pl.BlockSpec((tm, tk), lambda i, j, k: (i, k))
    b_spec = pl.BlockSpec((tn, tk), lambda i, j, k: (j, k)) if mode == "nt" else pl.BlockSpec((tk, tn), lambda i, j, k: (k, j))
    na, nb, ne, no = len(a_extras), len(b_extras), len(extras), len(out_dtypes)

    def body(*refs):
        a_ref, b_ref = refs[0], refs[1]
        ax = refs[2:2 + na]
        bx = refs[2 + na:2 + na + nb]
        ex = refs[2 + na + nb:2 + na + nb + ne]
        outs = refs[2 + na + nb + ne:2 + na + nb + ne + no]
        acc_ref = refs[-1]
        k = pl.program_id(2)

        @pl.when(k == 0)
        def _():
            acc_ref[...] = jnp.zeros_like(acc_ref)

        av = a_ref[...]
        if a_fn is not None:
            av = a_fn(av, *[r[...] for r in ax])
        bv = b_ref[...]
        if b_fn is not None:
            bv = b_fn(bv, *[r[...] for r in bx])
        acc_ref[...] += lax.dot_general(av.astype(BF16), bv.astype(BF16), dims, preferred_element_type=F32)

        @pl.when(k == nk - 1)
        def _():
            res = acc_ref[...]
            res = (res,) if epi is None else epi(res, *[r[...] for r in ex])
            for o_ref, r in zip(outs, res):
                o_ref[...] = r.astype(o_ref.dtype)

    xs = list(a_extras) + list(b_extras) + list(extras)
    return pl.pallas_call(
        body,
        grid=(M // tm, N // tn, nk),
        in_specs=[a_spec, b_spec] + [pl.BlockSpec(bs, im) for _, bs, im in xs],
        out_specs=[pl.BlockSpec((tm, tn), lambda i, j, k: (i, j)) for _ in out_dtypes],
        out_shape=[_sds((M, N), dt) for dt in out_dtypes],
        scratch_shapes=[pltpu.VMEM((tm, tn), F32)],
        compiler_params=_cp(("parallel", "parallel", "arbitrary")),
        name=name,
    )(a, b, *[x for x, _, _ in xs])


def ln_mod_fwd(x, gain, sc, sh, name):
    B, T, D = x.shape
    tt = _pick(T, 512)

    def body(x_ref, g_ref, sc_ref, sh_ref, h_ref):
        xv = x_ref[0]
        r = lax.rsqrt(jnp.mean(xv * xv, axis=-1, keepdims=True) + EPS)
        h = (xv * r * g_ref[...]) * (1.0 + sc_ref[0]) + sh_ref[0]
        h_ref[0] = h.astype(h_ref.dtype)

    return pl.pallas_call(
        body, grid=(B, T // tt),
        in_specs=[pl.BlockSpec((1, tt, D), lambda b, t: (b, t, 0)), pl.BlockSpec((1, D), lambda b, t: (0, 0)),
                  pl.BlockSpec((1, 1, D), lambda b, t: (b, 0, 0)), pl.BlockSpec((1, 1, D), lambda b, t: (b, 0, 0))],
        out_specs=pl.BlockSpec((1, tt, D), lambda b, t: (b, t, 0)),
        out_shape=_sds((B, T, D), BF16),
        compiler_params=_cp(("parallel", "parallel")), name=name,
    )(x, gain, sc, sh)


def ln_mod_bwd(x, gain, sc, dh, dres, name):
    B, T, D = x.shape
    tt = _pick(T, 512)

    def body(x_ref, g_ref, sc_ref, dh_ref, dres_ref, dx_ref, dg_ref, dsc_ref, dsh_ref):
        b, t = pl.program_id(0), pl.program_id(1)
        xv, dhv = x_ref[0], dh_ref[0]
        g, s = g_ref[...], sc_ref[0]
        r = lax.rsqrt(jnp.mean(xv * xv, axis=-1, keepdims=True) + EPS)
        xn = xv * r
        dxn = dhv * (g * (1.0 + s))
        dx_ref[0] = dres_ref[0] + r * (dxn - xn * jnp.mean(dxn * xn, axis=-1, keepdims=True))
        s1 = jnp.sum(dhv * xn, axis=0, keepdims=True)
        s2 = jnp.sum(dhv, axis=0, keepdims=True)

        @pl.when(t == 0)
        def _():
            dsc_ref[0] = jnp.zeros_like(s1)
            dsh_ref[0] = jnp.zeros_like(s1)

        @pl.when((t == 0) & (b == 0))
        def _():
            dg_ref[...] = jnp.zeros_like(s1)

        dsc_ref[0] += s1 * g
        dsh_ref[0] += s2
        dg_ref[...] += s1 * (1.0 + s)

    tile = pl.BlockSpec((1, tt, D), lambda b, t: (b, t, 0))
    row = pl.BlockSpec((1, D), lambda b, t: (0, 0))
    brow = pl.BlockSpec((1, 1, D), lambda b, t: (b, 0, 0))
    return pl.pallas_call(
        body, grid=(B, T // tt),
        in_specs=[tile, row, brow, tile, tile],
        out_specs=[tile, row, brow, brow],
        out_shape=[_sds((B, T, D), F32), _sds((1, D), F32), _sds((B, 1, D), F32), _sds((B, 1, D), F32)],
        compiler_params=_cp(("arbitrary", "arbitrary")), name=name,
    )(x, gain, sc, dh, dres)


def rowsum_prod(a, b, name):
    B, T, D = a.shape
    tt = _pick(T, 512)

    def body(a_ref, b_ref, o_ref):
        @pl.when(pl.program_id(1) == 0)
        def _():
            o_ref[...] = jnp.zeros_like(o_ref)

        o_ref[0] += jnp.sum(a_ref[0] * b_ref[0], axis=0, keepdims=True)

    tile = pl.BlockSpec((1, tt, D), lambda b, t: (b, t, 0))
    return pl.pallas_call(
        body, grid=(B, T // tt), in_specs=[tile, tile],
        out_specs=pl.BlockSpec((1, 1, D), lambda b, t: (b, 0, 0)),
        out_shape=_sds((B, 1, D), F32),
        compiler_params=_cp(("parallel", "arbitrary")), name=name,
    )(a, b)


def loss_grad(y, tgt, name):
    B, T, D = y.shape
    tt = _pick(T, 512)

    def body(y_ref, t_ref, dy_ref, s_ref):
        @pl.when((pl.program_id(0) == 0) & (pl.program_id(1) == 0))
        def _():
            s_ref[...] = jnp.zeros_like(s_ref)

        e = y_ref[0] - t_ref[0]
        dy_ref[0] = e * (1.0 / D)
        s_ref[...] += jnp.sum(e * e, axis=0, keepdims=True)

    tile = pl.BlockSpec((1, tt, D), lambda b, t: (b, t, 0))
    return pl.pallas_call(
        body, grid=(B, T // tt), in_specs=[tile, tile],
        out_specs=[tile, pl.BlockSpec((1, D), lambda b, t: (0, 0))],
        out_shape=[_sds((B, T, D), F32), _sds((1, D), F32)],
        compiler_params=_cp(("arbitrary", "arbitrary")), name=name,
    )(y, tgt)


def _sb_group(nb):
    return 4 if nb % 4 == 0 else (2 if nb % 2 == 0 else 1)


def _tri_sum(x, tri2):
    hi, lo = _split(x)
    return lax.dot_general(jnp.concatenate([hi, lo], axis=1), tri2, _NN, preferred_element_type=F32)


def _tri2(cond):
    t = cond.astype(BF16)
    return jnp.concatenate([t, t], axis=0)


def sb_attn_fwd(proj3, gq, gk, name, exch=None):
    B, T, _ = proj3.shape
    NB = T // SB_BLK
    G = _sb_group(NB)
    KW = G * SB_BLK
    scale = SB_HD ** -0.5

    def body(q_ref, k_ref, v_ref, gq_ref, gk_ref, o_ref, tot_ref, qn_s, kn_s, v_s):
        row_io = lax.broadcasted_iota(jnp.int32, (SB_BLK, SB_BLK), 0)
        col_io = lax.broadcasted_iota(jnp.int32, (SB_BLK, SB_BLK), 1)
        tri = _tri2(row_io > col_io)

        def prep(i, _):
            rows = pl.ds(pl.multiple_of(i * SB_BLK, SB_BLK), SB_BLK)
            for hh in range(2):
                sl = slice(hh * SB_HD, (hh + 1) * SB_HD)
                q = q_ref[0, rows, sl]
                k = k_ref[0, rows, sl]
                qn_s[hh, rows, :] = (q * lax.rsqrt(jnp.mean(q * q, -1, keepdims=True) + EPS) * (gq_ref[...] * scale)).astype(BF16)
                kn_s[hh, rows, :] = (k * lax.rsqrt(jnp.mean(k * k, -1, keepdims=True) + EPS) * gk_ref[...]).astype(BF16)
                v_s[hh, rows, :] = v_ref[0, rows, sl].astype(BF16)
            return 0

        lax.fori_loop(0, NB, prep, 0)

        diff_w = (lax.broadcasted_iota(jnp.int32, (SB_BLK, KW), 1)
                  - lax.broadcasted_iota(jnp.int32, (SB_BLK, KW), 0))

        def qblock(i, _):
            rows = pl.ds(pl.multiple_of(i * SB_BLK, SB_BLK), SB_BLK)
            qn = [qn_s[hh, rows, :] for hh in range(2)]
            nsj = i // G + 1

            def sblock(sj, carry, masked):
                cols = pl.ds(pl.multiple_of(sj * KW, KW), KW)
                mask = diff_w < (i % G) * SB_BLK
                zs = [lax.dot_general(qn[hh], kn_s[hh, cols, :], _NT, preferred_element_type=F32) for hh in range(2)]
                lgs, lss = [], []
                for hh in range(2):
                    sp = _softplus(zs[hh])
                    lgs.append(jnp.where(mask, -sp, 0.0) if masked else -sp)
                    lss.append(zs[hh] - sp)
                blocks = [lgs[hh][:, s * SB_BLK:(s + 1) * SB_BLK] for hh in range(2) for s in range(G)]
                ts_all = _tri_sum(jnp.concatenate(blocks, axis=0), tri)
                atts, css = [], []
                for hh in range(2):
                    cs = carry[hh][1]
                    ps = []
                    for s in reversed(range(G)):
                        n = hh * G + s
                        ts = ts_all[n * SB_BLK:(n + 1) * SB_BLK]
                        ps.append(lss[hh][:, s * SB_BLK:(s + 1) * SB_BLK] + ts + cs)
                        cs = cs + (ts[:, :1] + blocks[n][:, :1])
                    p = ps[0] if G == 1 else jnp.concatenate(ps[::-1], axis=1)
                    att = jnp.exp(p)
                    atts.append((jnp.where(mask, att, 0.0) if masked else att).astype(BF16))
                    css.append(cs)
                return tuple((carry[hh][0] + lax.dot_general(atts[hh], v_s[hh, cols, :], _NN, preferred_element_type=F32), css[hh])
                             for hh in range(2))

            init = (jnp.zeros((SB_BLK, SB_HD), F32), jnp.zeros((SB_BLK, 1), F32))
            res = sblock(nsj - 1, (init, init), True)
            res = lax.fori_loop(0, nsj - 1, lambda jj, c: sblock(nsj - 2 - jj, c, False), res)
            for hh in range(2):
                o_ref[0, rows, hh * SB_HD:(hh + 1) * SB_HD] = res[hh][0].astype(o_ref.dtype)
                tot_ref[0, hh, rows, :] = res[hh][1]
            return 0

        lax.fori_loop(0, NB, qblock, 0)

    blk = lambda off: pl.BlockSpec((1, T, 128), lambda b, p: (b, 0, off + p))
    grow = pl.BlockSpec((1, SB_HD), lambda b, p: (0, 0))
    return _hosted_call(
        body, grid=(B, SB_W // 128),
        in_specs=[blk(0), blk(SB_W // 128), blk(2 * SB_W // 128), grow, grow],
        out_specs=[pl.BlockSpec((1, T, 128), lambda b, p: (b, 0, p)), pl.BlockSpec((1, 2, T, 1), lambda b, p: (b, p, 0, 0))],
        out_shape=[_sds((B, T, SB_W), BF16), _sds((B, SB_HEADS, T, 1), F32)],
        scratch_shapes=[pltpu.VMEM((2, T, SB_HD), BF16)] * 3,
        name=name, args=(proj3, proj3, proj3, gq, gk), exch=exch)


def sb_attn_bwd(proj3, gq, gk, tot, dmix3, name, exch=None):
    B, T, _ = proj3.shape
    NB = T // SB_BLK
    G = _sb_group(NB)
    KW = G * SB_BLK
    scale = SB_HD ** -0.5

    def body(q_ref, k_ref, v_ref, gq_ref, gk_ref, tot_ref, do_ref, dq_ref, dk_ref, dv_ref, dgq_ref, dgk_ref,
             qn_s, kn_s, v_s, do_s, dqn_s, dkn_s, dv_s):
        row_io = lax.broadcasted_iota(jnp.int32, (SB_BLK, SB_BLK), 0)
        col_io = lax.broadcasted_iota(jnp.int32, (SB_BLK, SB_BLK), 1)
        tri = _tri2(row_io > col_io)
        trip = _tri2(row_io < col_io)

        @pl.when((pl.program_id(0) == 0) & (pl.program_id(1) == 0))
        def _():
            dgq_ref[...] = jnp.zeros_like(dgq_ref)
            dgk_ref[...] = jnp.zeros_like(dgk_ref)

        def prep(i, _):
            rows = pl.ds(pl.multiple_of(i * SB_BLK, SB_BLK), SB_BLK)
            for hh in range(2):
                sl = slice(hh * SB_HD, (hh + 1) * SB_HD)
                q = q_ref[0, rows, sl]
                k = k_ref[0, rows, sl]
                qn_s[hh, rows, :] = (q * lax.rsqrt(jnp.mean(q * q, -1, keepdims=True) + EPS) * (gq_ref[...] * scale)).astype(BF16)
                kn_s[hh, rows, :] = (k * lax.rsqrt(jnp.mean(k * k, -1, keepdims=True) + EPS) * gk_ref[...]).astype(BF16)
                v_s[hh, rows, :] = v_ref[0, rows, sl].astype(BF16)
                do_s[hh, rows, :] = do_ref[0, rows, sl].astype(BF16)
            return 0

        lax.fori_loop(0, NB, prep, 0)
        dkn_s[...] = jnp.zeros_like(dkn_s)
        dv_s[...] = jnp.zeros_like(dv_s)

        diff_w = (lax.broadcasted_iota(jnp.int32, (SB_BLK, KW), 1)
                  - lax.broadcasted_iota(jnp.int32, (SB_BLK, KW), 0))

        def qblock(i, _):
            rows = pl.ds(pl.multiple_of(i * SB_BLK, SB_BLK), SB_BLK)
            qn = [qn_s[hh, rows, :] for hh in range(2)]
            dov = [do_s[hh, rows, :] for hh in range(2)]
            tot = [tot_ref[0, hh, rows, :] for hh in range(2)]

            def sblock(sj, carry, masked):
                cols = pl.ds(pl.multiple_of(sj * KW, KW), KW)
                mask = diff_w < (i % G) * SB_BLK
                hs = range(2)
                kns = [kn_s[hh, cols, :] for hh in hs]
                zs = [lax.dot_general(qn[hh], kns[hh], _NT, preferred_element_type=F32) for hh in hs]
                datts = [lax.dot_general(dov[hh], v_s[hh, cols, :], _NT, preferred_element_type=F32) for hh in hs]
                lgs, lss = [], []
                for hh in hs:
                    sp = _softplus(zs[hh])
                    lgs.append(jnp.where(mask, -sp, 0.0) if masked else -sp)
                    lss.append(zs[hh] - sp)
                blocks = [lgs[hh][:, s * SB_BLK:(s + 1) * SB_BLK] for hh in hs for s in range(G)]
                ts_all = _tri_sum(jnp.concatenate(blocks, axis=0), tri)
                atts, dps, cums = [], [], []
                for hh in hs:
                    cum = carry[hh][1]
                    ps = []
                    for s in range(G):
                        n = hh * G + s
                        ts = ts_all[n * SB_BLK:(n + 1) * SB_BLK]
                        cum = cum + (ts[:, :1] + blocks[n][:, :1])
                        ps.append(lss[hh][:, s * SB_BLK:(s + 1) * SB_BLK] + ts + (tot[hh] - cum))
                    p = ps[0] if G == 1 else jnp.concatenate(ps, axis=1)
                    att = jnp.exp(p)
                    att = jnp.where(mask, att, 0.0) if masked else att
                    atts.append(att.astype(BF16))
                    dps.append(att * datts[hh])
                    cums.append(cum)
                dblocks = [dps[hh][:, s * SB_BLK:(s + 1) * SB_BLK] for hh in hs for s in range(G)]
                tp_all = _tri_sum(jnp.concatenate(dblocks, axis=0), trip)
                dzs, cdps = [], []
                for hh in hs:
                    cdp = carry[hh][2]
                    dls = []
                    for s in range(G):
                        n = hh * G + s
                        tp = tp_all[n * SB_BLK:(n + 1) * SB_BLK]
                        dls.append(tp + cdp)
                        cdp = cdp + (tp[:, SB_BLK - 1:] + dblocks[n][:, SB_BLK - 1:])
                    dlg = dls[0] if G == 1 else jnp.concatenate(dls, axis=1)
                    dz = dps[hh] - jnp.exp(lss[hh]) * (dps[hh] + dlg)
                    dzs.append((jnp.where(mask, dz, 0.0) if masked else dz).astype(BF16))
                    cdps.append(cdp)
                new = []
                for hh in hs:
                    dq = carry[hh][0] + lax.dot_general(dzs[hh], kns[hh], _NN, preferred_element_type=F32)
                    dkn_s[hh, cols, :] += lax.dot_general(dzs[hh], qn[hh], _TN, preferred_element_type=F32)
                    dv_s[hh, cols, :] += lax.dot_general(atts[hh], dov[hh], _TN, preferred_element_type=F32)
                    new.append((dq, cums[hh], cdps[hh]))
                return tuple(new)

            z1 = jnp.zeros((SB_BLK, 1), F32)
            init = (jnp.zeros((SB_BLK, SB_HD), F32), z1, z1)
            nsj = i // G + 1
            res = lax.fori_loop(0, nsj - 1, lambda sj, c: sblock(sj, c, False), (init, init))
            res = sblock(nsj - 1, res, True)
            for hh in range(2):
                dqn_s[hh, rows, :] = res[hh][0]
            return 0

        lax.fori_loop(0, NB, qblock, 0)

        def fin(i, carry):
            aq, ak = carry
            rows = pl.ds(pl.multiple_of(i * SB_BLK, SB_BLK), SB_BLK)
            for hh in range(2):
                sl = slice(hh * SB_HD, (hh + 1) * SB_HD)
                for src_ref, d_s, g_ref, out_ref, mult, which in ((q_ref, dqn_s, gq_ref, dq_ref, scale, 0), (k_ref, dkn_s, gk_ref, dk_ref, 1.0, 1)):
                    xr = src_ref[0, rows, sl]
                    r = lax.rsqrt(jnp.mean(xr * xr, -1, keepdims=True) + EPS)
                    dy = d_s[hh, rows, :] * mult
                    u = dy * g_ref[...]
                    out_ref[0, rows, sl] = r * u - xr * (r * r * r) * jnp.mean(u * xr, -1, keepdims=True)
                    part = jnp.sum(dy * xr * r, axis=0, keepdims=True)
                    if which == 0:
                        aq = aq + part
                    else:
                        ak = ak + part
                dv_ref[0, rows, sl] = dv_s[hh, rows, :]
            return aq, ak

        z64 = jnp.zeros((1, SB_HD), F32)
        aq, ak = lax.fori_loop(0, NB, fin, (z64, z64))
        dgq_ref[...] += aq
        dgk_ref[...] += ak

    blk = lambda off: pl.BlockSpec((1, T, 128), lambda b, p: (b, 0, off + p))
    grow = pl.BlockSpec((1, SB_HD), lambda b, p: (0, 0))
    return _hosted_call(
        body, grid=(B, SB_W // 128),
        in_specs=[blk(0), blk(SB_W // 128), blk(2 * SB_W // 128), grow, grow,
                  pl.BlockSpec((1, 2, T, 1), lambda b, p: (b, p, 0, 0)), blk(0)],
        out_specs=[blk(0), blk(0), blk(0), grow, grow],
        out_shape=[_sds((B, T, SB_W), F32)] * 3 + [_sds((1, SB_HD), F32)] * 2,
        scratch_shapes=[pltpu.VMEM((2, T, SB_HD), BF16)] * 4 + [pltpu.VMEM((2, T, SB_HD), F32)] * 3,
        name=name, args=(proj3, proj3, proj3, gq, gk, tot, dmix3), exch=exch)


def _exch_copies(kind, src_refs, land_refs, sems):
    x_, y_, c_ = _coords()
    me = 4 * x_ + 2 * y_ + c_
    local, go, arrive = [], [], []
    for a, (src, land) in enumerate(zip(src_refs, land_refs)):
        send, recv, loc = sems[3 * a:3 * a + 3]
        local.append(pltpu.make_async_copy(src if kind == "gather" else src.at[me], land.at[me], loc))
        for k in range(1, N_DEV):
            px = 1 - x_ if k & 4 else x_
            py = 1 - y_ if k & 2 else y_
            pc = 1 - c_ if k & 1 else c_
            peer = 4 * px + 2 * py + pc
            out = src if kind == "gather" else src.at[peer]
            mk = functools.partial(pltpu.make_async_remote_copy, send_sem=send.at[k - 1], recv_sem=recv.at[k - 1],
                                   device_id=(px, py, pc), device_id_type=pl.DeviceIdType.MESH)
            go.append(mk(src_ref=out, dst_ref=land.at[me]))
            arrive.append(mk(src_ref=out, dst_ref=land.at[peer]))
    return local, go, arrive


def _hosted_call(body, *, grid, in_specs, out_specs, out_shape, scratch_shapes, name, args, exch=None):
    sem = ("arbitrary",) * len(grid)
    if exch is None:
        return pl.pallas_call(body, grid=grid, in_specs=in_specs, out_specs=out_specs, out_shape=out_shape,
                              scratch_shapes=scratch_shapes, compiler_params=_cp(sem), name=name)(*args)
    kind, srcs = exch
    ns, n_in, n_out, n_scr = len(srcs), len(in_specs), len(out_specs), len(scratch_shapes)
    lands = [_sds((N_DEV,) + s.shape if kind == "gather" else s.shape, s.dtype) for s in srcs]

    def wrapped(*refs):
        ins, src_refs = refs[:n_in], refs[n_in:n_in + ns]
        outs, land_refs = refs[n_in + ns:n_in + ns + n_out], refs[n_in + ns + n_out:n_in + 2 * ns + n_out]
        scr, sems = refs[n_in + 2 * ns + n_out:n_in + 2 * ns + n_out + n_scr], refs[n_in + 2 * ns + n_out + n_scr:]
        ids = [pl.program_id(d) for d in range(len(grid))]
        first = functools.reduce(lambda a, b: a & b, [i == 0 for i in ids])
        last = functools.reduce(lambda a, b: a & b, [i == g - 1 for i, g in zip(ids, grid)])

        @pl.when(first)
        def _():
            local, go, _ = _exch_copies(kind, src_refs, land_refs, sems)
            for cp in local + go:
                cp.start()

        body(*ins, *outs, *scr)

        @pl.when(last)
        def _():
            local, go, arrive = _exch_copies(kind, src_refs, land_refs, sems)
            for cp in arrive:
                cp.wait_recv()
            for cp in go:
                cp.wait_send()
            for cp in local:
                cp.wait()

    any_spec = pl.BlockSpec(memory_space=pl.ANY)
    sems = [pltpu.SemaphoreType.DMA((N_DEV - 1,)), pltpu.SemaphoreType.DMA((N_DEV - 1,)), pltpu.SemaphoreType.DMA] * ns
    return pl.pallas_call(
        wrapped, grid=grid, in_specs=list(in_specs) + [any_spec] * ns, out_specs=list(out_specs) + [any_spec] * ns,
        out_shape=list(out_shape) + lands, scratch_shapes=list(scratch_shapes) + sems,
        compiler_params=_cp(sem), name=name)(*args, *srcs)


def _conv_silu(x, w, T):
    t_io = lax.broadcasted_iota(jnp.int32, x.shape, 0)
    xs = [x] + [jnp.where(t_io >= s, pltpu.roll(x, s, 0), 0.0) for s in range(1, CONV_K)]
    y = xs[0] * w[CONV_K - 1:CONV_K, :]
    for s in range(1, CONV_K):
        y = y + xs[s] * w[CONV_K - 1 - s:CONV_K - s, :]
    return y, y * _sigmoid(y), xs


def gdn_pre_fwd(proj3, conv_w, name):
    B, T, _ = proj3.shape
    qs = DN_HD ** -0.5

    def body(x_ref, w_ref, o_ref):
        kind = pl.program_id(1) // DN_HEADS
        _, s, _ = _conv_silu(x_ref[0], w_ref[...], T)
        n = lax.rsqrt(jnp.sum(s * s, axis=-1, keepdims=True) + EPS)
        c = jnp.where(kind == 0, qs, 1.0)
        o_ref[0, 0] = jnp.where(kind < 2, s * (n * c), s)

    return pl.pallas_call(
        body, grid=(B, 3 * DN_HEADS),
        in_specs=[pl.BlockSpec((1, T, 128), lambda b, j: (b, 0, COL_DNQKV + j)), pl.BlockSpec((CONV_K, 128), lambda b, j: (0, j))],
        out_specs=pl.BlockSpec((1, 1, T, 128), lambda b, j: (b, j // DN_HEADS, 0, j % DN_HEADS)),
        out_shape=_sds((B, 3, T, DN_W), F32),
        compiler_params=_cp(("parallel", "parallel")), name=name,
    )(proj3, conv_w)


def gdn_pre_bwd(proj3, conv_w, dqkv, name):
    B, T, _ = proj3.shape
    qs = DN_HD ** -0.5

    def body(x_ref, w_ref, d_ref, dx_ref, dw_ref):
        kind = pl.program_id(1) // DN_HEADS
        w = w_ref[...]
        y, s, xs = _conv_silu(x_ref[0], w, T)
        dout = d_ref[0, 0]
        n = lax.rsqrt(jnp.sum(s * s, axis=-1, keepdims=True) + EPS)
        c = jnp.where(kind == 0, qs, 1.0)
        dsn = c * (n * dout - s * (n * n * n) * jnp.sum(dout * s, axis=-1, keepdims=True))
        ds = jnp.where(kind < 2, dsn, dout)
        sg = _sigmoid(y)
        dy = ds * (sg * (1.0 + y * (1.0 - sg)))
        t_io = lax.broadcasted_iota(jnp.int32, dy.shape, 0)
        dx = dy * w[CONV_K - 1:CONV_K, :]
        dw_ref[0, CONV_K - 1:CONV_K, :] = jnp.sum(dy * xs[0], axis=0, keepdims=True)
        for sft in range(1, CONV_K):
            dx = dx + jnp.where(t_io < T - sft, pltpu.roll(dy, T - sft, 0), 0.0) * w[CONV_K - 1 - sft:CONV_K - sft, :]
            dw_ref[0, CONV_K - 1 - sft:CONV_K - sft, :] = jnp.sum(dy * xs[sft], axis=0, keepdims=True)
        dx_ref[0] = dx

    return pl.pallas_call(
        body, grid=(B, 3 * DN_HEADS),
        in_specs=[pl.BlockSpec((1, T, 128), lambda b, j: (b, 0, COL_DNQKV + j)), pl.BlockSpec((CONV_K, 128), lambda b, j: (0, j)),
                  pl.BlockSpec((1, 1, T, 128), lambda b, j: (b, j // DN_HEADS, 0, j % DN_HEADS))],
        out_specs=[pl.BlockSpec((1, T, 128), lambda b, j: (b, 0, j)), pl.BlockSpec((1, CONV_K, 128), lambda b, j: (b, 0, j))],
        out_shape=[_sds((B, T, 3 * DN_W), F32), _sds((B, CONV_K, 3 * DN_W), F32)],
        compiler_params=_cp(("parallel", "parallel")), name=name,
    )(proj3, conv_w, dqkv)


def gdn_gates_fwd(proj3, alog_row, dtb_row, name):
    B, T, _ = proj3.shape

    def body(x_ref, al_ref, dt_ref, o_ref):
        x = x_ref[0]
        lane = lax.broadcasted_iota(jnp.int32, x.shape, 1)
        g = -jnp.exp(al_ref[...]) * _softplus(x + dt_ref[...])
        o_ref[0] = jnp.where(lane < DN_HEADS, g, jnp.where(lane < 2 * DN_HEADS, _sigmoid(x), 0.0))

    row = pl.BlockSpec((1, 128), lambda b: (0, 0))
    return pl.pallas_call(
        body, grid=(B,),
        in_specs=[pl.BlockSpec((1, T, 128), lambda b: (b, 0, COL_AB)), row, row],
        out_specs=pl.BlockSpec((1, T, 128), lambda b: (b, 0, 0)),
        out_shape=_sds((B, T, 128), F32),
        compiler_params=_cp(("parallel",)), name=name,
    )(proj3, alog_row, dtb_row)


def gdn_gates_bwd(proj3, alog_row, dtb_row, dgates, name):
    B, T, _ = proj3.shape

    def body(x_ref, al_ref, dt_ref, d_ref, dx_ref, dal_ref, ddt_ref):
        @pl.when(pl.program_id(0) == 0)
        def _():
            dal_ref[...] = jnp.zeros_like(dal_ref)
            ddt_ref[...] = jnp.zeros_like(ddt_ref)

        x, d = x_ref[0], d_ref[0]
        lane = lax.broadcasted_iota(jnp.int32, x.shape, 1)
        a = x + dt_ref[...]
        na = -jnp.exp(al_ref[...])
        da = jnp.where(lane < DN_HEADS, d * na * _sigmoid(a), 0.0)
        bt = _sigmoid(x)
        dx_ref[0] = da + jnp.where((lane >= DN_HEADS) & (lane < 2 * DN_HEADS), d * bt * (1.0 - bt), 0.0)
        dal_ref[...] += jnp.sum(jnp.where(lane < DN_HEADS, d * na * _softplus(a), 0.0), axis=0, keepdims=True)
        ddt_ref[...] += jnp.sum(da, axis=0, keepdims=True)

    row = pl.BlockSpec((1, 128), lambda b: (0, 0))
    tile = pl.BlockSpec((1, T, 128), lambda b: (b, 0, 0))
    return pl.pallas_call(
        body, grid=(B,),
        in_specs=[pl.BlockSpec((1, T, 128), lambda b: (b, 0, COL_AB)), row, row, tile],
        out_specs=[tile, row, row],
        out_shape=[_sds((B, T, 128), F32), _sds((1, 128), F32), _sds((1, 128), F32)],
        compiler_params=_cp(("arbitrary",)), name=name,
    )(proj3, alog_row, dtb_row, dgates)


def gdn_post_fwd(ob, proj3, gain, name):
    B, T, _ = ob.shape

    def body(o_ref, z_ref, g_ref, out_ref):
        o, z = o_ref[0], z_ref[0]
        r = lax.rsqrt(jnp.mean(o * o, axis=-1, keepdims=True) + EPS)
        out_ref[0] = ((o * r * g_ref[...]) * (z * _sigmoid(z))).astype(out_ref.dtype)

    tile = pl.BlockSpec((1, T, 128), lambda b, h: (b, 0, h))
    return pl.pallas_call(
        body, grid=(B, DN_HEADS),
        in_specs=[tile, pl.BlockSpec((1, T, 128), lambda b, h: (b, 0, COL_Z + h)), pl.BlockSpec((1, 128), lambda b, h: (0, 0))],
        out_specs=tile, out_shape=_sds((B, T, DN_W), BF16),
        compiler_params=_cp(("parallel", "parallel")), name=name,
    )(ob, proj3, gain)


def gdn_post_bwd(ob, proj3, gain, dmix3, name):
    B, T, _ = ob.shape

    def body(o_ref, z_ref, g_ref, d_ref, do_ref, dz_ref, dg_ref):
        @pl.when((pl.program_id(0) == 0) & (pl.program_id(1) == 0))
        def _():
            dg_ref[...] = jnp.zeros_like(dg_ref)

        o, z, d, g = o_ref[0], z_ref[0], d_ref[0], g_ref[...]
        r = lax.rsqrt(jnp.mean(o * o, axis=-1, keepdims=True) + EPS)
        sg = _sigmoid(z)
        dn = d * (z * sg)
        dz_ref[0] = d * (o * r * g) * (sg * (1.0 + z * (1.0 - sg)))
        dg_ref[...] += jnp.sum(dn * o * r, axis=0, keepdims=True)
        u = dn * g
        do_ref[0] = r * u - o * (r * r * r) * jnp.mean(u * o, axis=-1, keepdims=True)

    tile = pl.BlockSpec((1, T, 128), lambda b, h: (b, 0, h))
    row = pl.BlockSpec((1, 128), lambda b, h: (0, 0))
    return pl.pallas_call(
        body, grid=(B, DN_HEADS),
        in_specs=[tile, pl.BlockSpec((1, T, 128), lambda b, h: (b, 0, COL_Z + h)), row,
                  pl.BlockSpec((1, T, 128), lambda b, h: (b, 0, SB_W // 128 + h))],
        out_specs=[tile, tile, row],
        out_shape=[_sds((B, T, DN_W), F32), _sds((B, T, DN_W), F32), _sds((1, 128), F32)],
        compiler_params=_cp(("arbitrary", "arbitrary")), name=name,
    )(ob, proj3, gain, dmix3)


def _tri_inv(low, ri, ci):
    m = (ri == ci).astype(F32) - jnp.where(((ri >> 1) == (ci >> 1)) & (ri > ci), low, 0.0)
    s = 2
    while s < DN_C:
        sh = s.bit_length()
        off = ((ri >> sh) == (ci >> sh)) & ((ri & (2 * s - 1)) >= s) & ((ci & (2 * s - 1)) < s)
        m = m - _pdot(m, _pdot(jnp.where(off, low, 0.0), m, _BNN), _BNN)
        s *= 2
    return m


_BNN = (((2,), (1,)), ((0,), (0,)))
_BNT = (((2,), (2,)), ((0,), (0,)))
_BTN = (((1,), (1,)), ((0,), (0,)))
DN_G = 8


def _chunk_common(q, k, v, gt, h, tm=None):
    C = DN_C
    G = q.shape[0]
    ri = lax.broadcasted_iota(jnp.int32, (C, C), 0)
    ci = lax.broadcasted_iota(jnp.int32, (C, C), 1)
    lane = lax.broadcasted_iota(jnp.int32, (C, 128), 1)
    incl, strict = ri >= ci, ri > ci
    g = jnp.sum(jnp.where(lane == h, gt, 0.0), axis=2, keepdims=True)
    beta = jnp.sum(jnp.where(lane == h + DN_HEADS, gt, 0.0), axis=2, keepdims=True)
    ones = jnp.ones((G, C, 128), F32)
    inclf = jnp.broadcast_to(incl.astype(F32), (G, C, C))
    gc = _pdot(inclf, g * ones, _BNN)[:, :, :1]
    gcr = _pdot(jnp.ones((G, C, C), F32), jnp.where(ri == ci, gc, 0.0), _BNN)
    decay = jnp.where(incl, jnp.exp(jnp.where(incl, gc - gcr, 0.0)), 0.0)
    e = jnp.exp(gc)
    kb, vb = k * beta, v * beta
    kk = _bdot(kb, k, _BNT)
    if tm is None:
        tm = _tri_inv(jnp.where(strict, kk * decay, 0.0), ri, ci)
    kbe = kb * e
    u = _bdot(tm, vb, _BNN)
    w = _bdot(tm, kbe, _BNN)
    qk = _bdot(q, k, _BNT)
    intra = jnp.where(incl, qk * decay, 0.0)
    gl = gc[:, C - 1:C, :]
    el = jnp.exp(gl)
    r = jnp.exp(gl - gc)
    return dict(lane=lane, incl=incl, inclf=inclf, strict=strict, beta=beta, decay=decay, e=e,
                kb=kb, vb=vb, kk=kk, tm=tm, kbe=kbe, u=u, w=w, qk=qk, intra=intra, el=el, r=r, ones=ones)


def gdn_chunk_fwd(qkv, gates, name):
    B, _, T, _ = qkv.shape
    NC = T // DN_C

    G = DN_G if NC % DN_G == 0 else 1
    GC = G * DN_C

    GS = G * DN_HD

    def body(x_ref, gt_ref, o_ref, st_ref, tm_ref, s_s, p_s, b_s, qp_s, el_s):
        h = pl.program_id(1)

        def group_a(gi, _):
            rows = pl.ds(pl.multiple_of(gi * GC, GC), GC)
            srow = pl.ds(pl.multiple_of(gi * GS, GS), GS)
            q, k, v = [x_ref[0, i, rows, :].reshape(G, DN_C, DN_HD) for i in range(3)]
            c = _chunk_common(q, k, v, gt_ref[0, rows, :].reshape(G, DN_C, 128), h)
            kr = k * c["r"]
            tm_ref[0, 0, rows, :] = c["tm"].reshape(GC, DN_C)
            p_s[srow, :] = _bdot(kr, c["w"], _BTN).reshape(GS, DN_HD)
            b_s[srow, :] = _bdot(kr, c["u"], _BTN).reshape(GS, DN_HD)
            qp_s[rows, :] = (q * c["e"] - _bdot(c["intra"], c["w"], _BNN)).reshape(GC, DN_HD)
            o_ref[0, rows, :] = _bdot(c["intra"], c["u"], _BNN).reshape(GC, DN_HD)
            el_s[pl.ds(gi * G, G), :, :] = c["el"] * jnp.ones((G, 1, 128), F32)
            return 0

        lax.fori_loop(0, NC // G, group_a, 0)
        s_s[...] = jnp.zeros_like(s_s)

        def chunk(n, _):
            srow = pl.ds(pl.multiple_of(n * DN_HD, DN_HD), DN_HD)
            st = s_s[...]
            st_ref[0, 0, srow, :] = st
            s_s[...] = (st * el_s[n] + b_s[srow, :]) - _bdot(p_s[srow, :], st)
            return 0

        lax.fori_loop(0, NC, chunk, 0)

        def group_c(gi, _):
            rows = pl.ds(pl.multiple_of(gi * GC, GC), GC)
            srow = pl.ds(pl.multiple_of(gi * GS, GS), GS)
            st = st_ref[0, 0, srow, :].reshape(G, DN_HD, DN_HD)
            o_ref[0, rows, :] += _bdot(qp_s[rows, :].reshape(G, DN_C, DN_HD), st, _BNN).reshape(GC, DN_HD)
            return 0

        lax.fori_loop(0, NC // G, group_c, 0)

    return pl.pallas_call(
        body, grid=(B, DN_HEADS),
        in_specs=[pl.BlockSpec((1, 3, T, 128), lambda b, h: (b, 0, 0, h)), pl.BlockSpec((1, T, 128), lambda b, h: (b, 0, 0))],
        out_specs=[pl.BlockSpec((1, T, 128), lambda b, h: (b, 0, h)), pl.BlockSpec((1, 1, NC * DN_HD, DN_HD), lambda b, h: (b, h, 0, 0)),
                   pl.BlockSpec((1, 1, T, DN_C), lambda b, h: (b, h, 0, 0))],
        out_shape=[_sds((B, T, DN_W), F32), _sds((B, DN_HEADS, NC * DN_HD, DN_HD), F32), _sds((B, DN_HEADS, T, DN_C), F32)],
        scratch_shapes=[pltpu.VMEM((DN_HD, DN_HD), F32)] + [pltpu.VMEM((NC * DN_HD, DN_HD), F32)] * 2
        + [pltpu.VMEM((T, DN_HD), F32), pltpu.VMEM((NC, 1, 128), F32)],
        compiler_params=_cp(("parallel", "parallel")), name=name,
    )(qkv, gates)


def gdn_chunk_bwd(qkv, gates, states, tms, dob, name):
    B, _, T, _ = qkv.shape
    NC = T // DN_C
    C = DN_C

    G = DN_G if NC % DN_G == 0 else 1
    GC = G * C

    GS = G * DN_HD

    def body(x_ref, gt_ref, st_ref, tm_ref, do_ref, dx_ref, dgt_ref, ds_s, p_s, r_s, el_s, dsa_s):
        h = pl.program_id(1)

        @pl.when(h == 0)
        def _():
            dgt_ref[...] = jnp.zeros_like(dgt_ref)

        def load(gi):
            rows = pl.ds(pl.multiple_of(gi * GC, GC), GC)
            q, k, v = [x_ref[0, i, rows, :].reshape(G, C, DN_HD) for i in range(3)]
            return rows, q, k, v, gt_ref[0, rows, :].reshape(G, C, 128), tm_ref[0, 0, rows, :].reshape(G, C, C)

        def group_a(gi, _):
            rows, q, k, v, gt, tm = load(gi)
            srow = pl.ds(pl.multiple_of(gi * GS, GS), GS)
            c = _chunk_common(q, k, v, gt, h, tm=tm)
            qp = q * c["e"] - _bdot(c["intra"], c["w"], _BNN)
            p_s[srow, :] = _bdot(k * c["r"], c["w"], _BTN).reshape(GS, DN_HD)
            r_s[srow, :] = _bdot(qp, do_ref[0, rows, :].reshape(G, C, DN_HD), _BTN).reshape(GS, DN_HD)
            el_s[pl.ds(gi * G, G), :, :] = c["el"] * jnp.ones((G, 1, 128), F32)
            return 0

        lax.fori_loop(0, NC // G, group_a, 0)
        ds_s[...] = jnp.zeros_like(ds_s)

        def chunk(m, _):
            n = NC - 1 - m
            srow = pl.ds(pl.multiple_of(n * DN_HD, DN_HD), DN_HD)
            dsn = ds_s[...]
            dsa_s[srow, :] = dsn
            ds_s[...] = (dsn * el_s[n] + r_s[srow, :]) - _bdot(p_s[srow, :], dsn, _TN)
            return 0

        lax.fori_loop(0, NC, chunk, 0)

        def group_c(gi, _):
            rows, q, k, v, gt, tm = load(gi)
            c = _chunk_common(q, k, v, gt, h, tm=tm)
            incl, strict, decay, e, r, el, tm = c["incl"], c["strict"], c["decay"], c["e"], c["r"], c["el"], c["tm"]
            srow = pl.ds(pl.multiple_of(gi * GS, GS), GS)
            st = st_ref[0, 0, srow, :].reshape(G, DN_HD, DN_HD)
            dsn = dsa_s[srow, :].reshape(G, DN_HD, DN_HD)
            do = do_ref[0, rows, :].reshape(G, C, DN_HD)
            dvn = _bdot(k * r, dsn, _BNN) + _bdot(c["intra"], do, _BTN)
            v_new = c["u"] - _bdot(c["w"], st, _BNN)
            del_ = jnp.sum(jnp.sum(dsn * st, axis=2, keepdims=True), axis=1, keepdims=True)
            dkr = _bdot(v_new, dsn, _BNT)
            dqe = _bdot(do, st, _BNT)
            dintra = _bdot(do, v_new, _BNT)
            dw = -_bdot(dvn, st, _BNT)
            dqkd = jnp.where(incl, dintra, 0.0)
            dqk = dqkd * decay
            ddecay = dqkd * c["qk"]
            dq = dqe * e + _bdot(dqk, k, _BNN)
            dk = dkr * r + _bdot(dqk, q, _BTN)
            dtm = _bdot(dvn, c["vb"], _BNT) + _bdot(dw, c["kbe"], _BNT)
            dvb = _bdot(tm, dvn, _BTN)
            dkbe = _bdot(tm, dw, _BTN)
            dkb = dkbe * e
            de = jnp.sum(dqe * q, axis=2, keepdims=True) + jnp.sum(dkbe * c["kb"], axis=2, keepdims=True)
            da = -_pdot(tm, _pdot(dtm, tm, _BNT), _BTN)
            dlow = jnp.where(strict, da, 0.0)
            dkk = dlow * decay
            ddecay = ddecay + dlow * c["kk"]
            dkb = dkb + _bdot(dkk, k, _BNN)
            dk = dk + _bdot(dkk, c["kb"], _BTN) + dkb * c["beta"]
            dbeta = jnp.sum(dkb * k, axis=2, keepdims=True) + jnp.sum(dvb * v, axis=2, keepdims=True)
            dv = dvb * c["beta"]
            dd = ddecay * decay
            dgc = jnp.sum(dd, axis=2, keepdims=True) - _pdot(dd, c["ones"], _BTN)[:, :, :1]
            dr = jnp.sum(dkr * k, axis=2, keepdims=True)
            dgc = dgc + de * e - dr * r
            dgl = jnp.sum(dr * r, axis=1, keepdims=True) + del_ * el
            rowc = lax.broadcasted_iota(jnp.int32, (C, 1), 0)
            dgc = dgc + jnp.where(rowc == C - 1, dgl, 0.0)
            dg = _pdot(c["inclf"], dgc * c["ones"], _BTN)[:, :, :1]
            dx_ref[0, 0, rows, :] = dq.reshape(GC, DN_HD)
            dx_ref[0, 1, rows, :] = dk.reshape(GC, DN_HD)
            dx_ref[0, 2, rows, :] = dv.reshape(GC, DN_HD)
            lane = c["lane"]
            dgt_ref[0, rows, :] += (jnp.where(lane == h, dg, 0.0) + jnp.where(lane == h + DN_HEADS, dbeta, 0.0)).reshape(GC, 128)
            return 0

        lax.fori_loop(0, NC // G, group_c, 0)

    return pl.pallas_call(
        body, grid=(B, DN_HEADS),
        in_specs=[pl.BlockSpec((1, 3, T, 128), lambda b, h: (b, 0, 0, h)), pl.BlockSpec((1, T, 128), lambda b, h: (b, 0, 0)),
                  pl.BlockSpec((1, 1, NC * DN_HD, DN_HD), lambda b, h: (b, h, 0, 0)), pl.BlockSpec((1, 1, T, C), lambda b, h: (b, h, 0, 0)),
                  pl.BlockSpec((1, T, 128), lambda b, h: (b, 0, h))],
        out_specs=[pl.BlockSpec((1, 3, T, 128), lambda b, h: (b, 0, 0, h)), pl.BlockSpec((1, T, 128), lambda b, h: (b, 0, 0))],
        out_shape=[_sds((B, 3, T, DN_W), F32), _sds((B, T, 128), F32)],
        scratch_shapes=[pltpu.VMEM((DN_HD, DN_HD), F32)] + [pltpu.VMEM((NC * DN_HD, DN_HD), F32)] * 2
        + [pltpu.VMEM((NC, 1, 128), F32), pltpu.VMEM((NC * DN_HD, DN_HD), F32)],
        compiler_params=_cp(("parallel", "arbitrary")), name=name,
    )(qkv, gates, states, tms, dob)


def ada_fwd(c_all, w_ada, b_sl, name):
    L, D, W = w_ada.shape
    NBt = c_all.shape[0]

    def body(c_ref, w_ref, b_ref, o_ref):
        cv = c_ref[...]
        o_ref[0] = _pdot(cv * _sigmoid(cv), w_ref[0]) + b_ref[0]

    return pl.pallas_call(
        body, grid=(L,),
        in_specs=[pl.BlockSpec((NBt, D), lambda l: (0, 0)), pl.BlockSpec((1, D, W), lambda l: (l, 0, 0)), pl.BlockSpec((1, 1, W), lambda l: (l, 0, 0))],
        out_specs=pl.BlockSpec((1, NBt, W), lambda l: (l, 0, 0)),
        out_shape=_sds((L, NBt, W), F32),
        compiler_params=_cp(("parallel",)), name=name,
    )(c_all, w_ada, b_sl)


def ada_bwd(c_all, dmod_cols, name):
    L, NBt, W = dmod_cols.shape
    D = c_all.shape[1]

    def body(c_ref, d_ref, o_ref):
        cv = c_ref[...]
        o_ref[0] = _pdot(cv * _sigmoid(cv), d_ref[0], _TN)

    return pl.pallas_call(
        body, grid=(L,),
        in_specs=[pl.BlockSpec((NBt, D), lambda l: (0, 0)), pl.BlockSpec((1, NBt, W), lambda l: (l, 0, 0))],
        out_specs=pl.BlockSpec((1, D, W), lambda l: (l, 0, 0)),
        out_shape=_sds((L, D, W), F32),
        compiler_params=_cp(("parallel",)), name=name,
    )(c_all, dmod_cols)


def adamw(partials, w, m, v, name):
    P, R, C = partials.shape
    tr = _pick(R, 256)

    def body(p_ref, w_ref, m_ref, v_ref, g_ref, d_ref, nm_ref, nv_ref):
        g = p_ref[0].astype(F32)
        for i in range(1, P):
            g = g + p_ref[i].astype(F32)
        nm = ADAM_B1 * m_ref[...] + (1.0 - ADAM_B1) * g
        nv = ADAM_B2 * v_ref[...] + (1.0 - ADAM_B2) * (g * g)
        m_hat = nm / (1.0 - ADAM_B1 ** ADAM_STEP)
        v_hat = nv / (1.0 - ADAM_B2 ** ADAM_STEP)
        g_ref[...] = g
        d_ref[...] = -ADAM_LR * (m_hat / (jnp.sqrt(v_hat) + ADAM_EPS) + ADAM_WD * w_ref[...])
        nm_ref[...] = nm
        nv_ref[...] = nv

    tile = pl.BlockSpec((tr, C), lambda i: (i, 0))
    return pl.pallas_call(
        body, grid=(R // tr,),
        in_specs=[pl.BlockSpec((P, tr, C), lambda i: (0, i, 0)), tile, tile, tile],
        out_specs=[tile] * 4, out_shape=[_sds((R, C), F32)] * 4,
        compiler_params=_cp(("parallel",)), name=name,
    )(partials, w, m, v)


def _coords():
    return lax.axis_index("x"), lax.axis_index("y"), lax.axis_index("c")


def all_gather(x, name):
    any_spec = pl.BlockSpec(memory_space=pl.ANY)

    def body(x_ref, out_ref, send_sems, recv_sems, local_sem):
        x_, y_, c_ = _coords()
        me, sibling = (x_, y_, c_), (x_, y_, 1 - c_)
        chips = [(1 - x_, y_), (x_, 1 - y_), (1 - x_, 1 - y_)]

        def slot(px, py, pc):
            return out_ref.at[4 * px + 2 * py + pc]

        def copy(k, block, to, src=None):
            return pltpu.make_async_remote_copy(
                src_ref=slot(*block) if src is None else src, dst_ref=slot(*block),
                send_sem=send_sems.at[k], recv_sem=recv_sems.at[k],
                device_id=to, device_id_type=pl.DeviceIdType.MESH)

        mine = pltpu.make_async_copy(x_ref, slot(*me), local_sem)
        mine.start()
        first = [copy(0, me, sibling, src=x_ref)]
        first += [copy(1 + j, me, (*chip, c_), src=x_ref) for j, chip in enumerate(chips)]
        for cp in first:
            cp.start()
        passed = [copy(4 + j, (*chip, c_), sibling) for j, chip in enumerate(chips)]
        for j, chip in enumerate(chips):
            copy(1 + j, (*chip, c_), me).wait_recv()
            passed[j].start()
        copy(0, sibling, me).wait_recv()
        for j, chip in enumerate(chips):
            copy(4 + j, (*chip, 1 - c_), me).wait_recv()
        for cp in first + passed:
            cp.wait_send()
        mine.wait()

    return pl.pallas_call(
        body, out_shape=_sds((N_DEV,) + x.shape, x.dtype),
        in_specs=[any_spec], out_specs=any_spec,
        scratch_shapes=[pltpu.SemaphoreType.DMA((7,)), pltpu.SemaphoreType.DMA((7,)), pltpu.SemaphoreType.DMA],
        name=name,
    )(x)


def all_to_all(x, name):
    any_spec = pl.BlockSpec(memory_space=pl.ANY)

    def body(x_ref, out_ref, send_sems, recv_sems, local_sem):
        x_, y_, c_ = _coords()
        me = 4 * x_ + 2 * y_ + c_
        mine = pltpu.make_async_copy(x_ref.at[me], out_ref.at[me], local_sem)
        mine.start()
        copies = []
        for k in range(1, N_DEV):
            px = 1 - x_ if k & 4 else x_
            py = 1 - y_ if k & 2 else y_
            pc = 1 - c_ if k & 1 else c_
            peer = 4 * px + 2 * py + pc
            copies.append((pltpu.make_async_remote_copy(
                src_ref=x_ref.at[peer], dst_ref=out_ref.at[me],
                send_sem=send_sems.at[k - 1], recv_sem=recv_sems.at[k - 1],
                device_id=(px, py, pc), device_id_type=pl.DeviceIdType.MESH), peer))
        for cp, _ in copies:
            cp.start()
        for k, (cp, peer) in enumerate(copies):
            pltpu.make_async_remote_copy(
                src_ref=x_ref.at[peer], dst_ref=out_ref.at[peer],
                send_sem=send_sems.at[k], recv_sem=recv_sems.at[k],
                device_id=(x_, y_, c_), device_id_type=pl.DeviceIdType.MESH).wait_recv()
        for cp, _ in copies:
            cp.wait_send()
        mine.wait()

    return pl.pallas_call(
        body, out_shape=_sds(x.shape, x.dtype),
        in_specs=[any_spec], out_specs=any_spec,
        scratch_shapes=[pltpu.SemaphoreType.DMA((7,)), pltpu.SemaphoreType.DMA((7,)), pltpu.SemaphoreType.DMA],
        name=name,
    )(x)


def _rows128(a):
    return a.reshape(-1, 128)


def _pad_lanes(a):
    return jnp.pad(a, ((0, 0), (0, 128 - a.shape[1])))


def kernel(x, c, w_ada, b_ada, norm_mix, norm_mlp, w_in, sb_q_norm, sb_k_norm, conv_w, a_log, dt_bias, dn_out_norm, w_out, w_ff1, w_ff2, loss_target, m_w_ada, m_b_ada, m_norm_mix, m_norm_mlp, m_w_in, m_sb_q_norm, m_sb_k_norm, m_conv_w, m_a_log, m_dt_bias, m_dn_out_norm, m_w_out, m_w_ff1, m_w_ff2, v_w_ada, v_b_ada, v_norm_mix, v_norm_mlp, v_w_in, v_sb_q_norm, v_sb_k_norm, v_conv_w, v_a_log, v_dt_bias, v_dn_out_norm, v_w_out, v_w_ff1, v_w_ff2):
    B, T, D = x.shape
    L = w_ada.shape[0]
    N = B * T
    FF = w_ff1.shape[2] * N_DEV
    WA = w_ada.shape[2]
    CS = conv_w.shape[2]
    me = 4 * lax.axis_index("x") + 2 * lax.axis_index("y") + lax.axis_index("c")
    tm = _pick(T, 1024)

    wb = [w.astype(BF16) for w in (w_in, w_out, w_ff1, w_ff2)]

    def assemble(lands):
        win_g, wout_g, w1_g, w2_g = lands
        return (jnp.pad(win_g.transpose(1, 0, 2).reshape(D, IN_W), ((0, 0), (0, IN_WP - IN_W))),
                wout_g.reshape(SB_W + DN_W, D), w1_g.transpose(1, 0, 2).reshape(D, FF), w2_g.reshape(FF, D))

    weights = [assemble([all_gather(w[0], "comm_gather_" + n) for w, n in zip(wb, ("w_in", "w_out", "w_ff1", "w_ff2"))])]
    conv_full = all_gather(conv_w, "comm_gather_conv").transpose(1, 2, 0, 3).reshape(L, CONV_K, 3 * DN_W)

    c_all = all_gather(c, "comm_gather_c").reshape(N_DEV * B, D)
    b_sl = lax.dynamic_slice_in_dim(b_ada, me * WA, WA, axis=1).reshape(L, 1, WA)
    mod_sh = ada_fwd(c_all, w_ada, b_sl, "ada_fwd")
    mod_g = all_gather(mod_sh, "comm_gather_mod")
    mod = lax.dynamic_slice_in_dim(mod_g, me * B, B, axis=2).transpose(1, 2, 0, 3).reshape(L, B, 6 * D)

    def mod_part(l, i):
        return mod[l, :, i * D:(i + 1) * D].reshape(B, 1, D)

    alog_row = _pad_lanes(a_log).reshape(L, 1, 128)
    dtb_row = _pad_lanes(dt_bias).reshape(L, 1, 128)

    def gate_epi(acc, xv, g):
        return acc, xv + g[0] * acc

    def relu2(a):
        r = jnp.maximum(a, 0.0)
        return r * r

    def times_gate(a, g):
        return a * g[0]

    tile_ij = lambda i, j, k: (i, j)

    saved = []
    xc = x
    for l in range(L):
        sh_a, sc_a, g_a, sh_m, sc_m, g_m = [mod_part(l, i) for i in range(6)]
        h = ln_mod_fwd(xc, norm_mix[l:l + 1], sc_a, sh_a, "ln_mod_fwd")
        W_in_l, W_out_l, W_1_l, W_2_l = weights[l]
        proj3 = matmul(h.reshape(N, D), W_in_l, mode="nn", name="mm_proj", tm=256)[0].reshape(B, T, IN_WP)
        if l + 1 < L:
            o_a, tot, *lands = sb_attn_fwd(proj3, sb_q_norm[l:l + 1], sb_k_norm[l:l + 1], "sb_attn_fwd_gather",
                                           exch=("gather", [w[l + 1] for w in wb]))
            weights.append(assemble(lands))
        else:
            o_a, tot = sb_attn_fwd(proj3, sb_q_norm[l:l + 1], sb_k_norm[l:l + 1], "sb_attn_fwd")
        qkv = gdn_pre_fwd(proj3, conv_full[l], "gdn_pre_fwd")
        gates = gdn_gates_fwd(proj3, alog_row[l], dtb_row[l], "gdn_gates_fwd")
        ob, states, tms = gdn_chunk_fwd(qkv, gates, "gdn_chunk_fwd")
        o_b = gdn_post_fwd(ob, proj3, dn_out_norm[l:l + 1], "gdn_post_fwd")
        mix = jnp.concatenate([o_a, o_b], axis=-1)
        y1, x_mid = matmul(
            mix.reshape(N, SB_W + DN_W), W_out_l, mode="nn", name="mm_out", out_dtypes=(F32, F32), tm=tm, epi=gate_epi,
            extras=[(xc.reshape(N, D), (tm, _pick(D, 1024)), tile_ij),
                    (g_a, (1, 1, _pick(D, 1024)), lambda i, j, k: (i * tm // T, 0, j))])
        x_mid = x_mid.reshape(B, T, D)
        h2 = ln_mod_fwd(x_mid, norm_mlp[l:l + 1], sc_m, sh_m, "ln_mod_fwd")
        u = matmul(h2.reshape(N, D), W_1_l, mode="nn", name="mm_ff1")[0]
        y2, x_out = matmul(
            u, W_2_l, mode="nn", name="mm_ff2", out_dtypes=(F32, F32), tm=tm, a_fn=relu2, epi=gate_epi,
            extras=[(x_mid.reshape(N, D), (tm, _pick(D, 1024)), tile_ij),
                    (g_m, (1, 1, _pick(D, 1024)), lambda i, j, k: (i * tm // T, 0, j))])
        saved.append(dict(x=xc, h=h, proj3=proj3, tot=tot, qkv=qkv, gates=gates, states=states, tms=tms, ob=ob, mix=mix,
                          y1=y1, x_mid=x_mid, h2=h2, u=u, y2=y2))
        xc = x_out.reshape(B, T, D)

    dx, sq = loss_grad(xc, loss_target, "loss_grad")
    loss = lax.psum((0.5 / D) * jnp.sum(sq), AXES)

    g_win, g_wout, g_w1, g_w2, dmods, smalls, parts = [], [], [], [], [], [], []
    pending = None
    tk_tok = tm
    wi = w_in.shape[2]

    def shard_layer(gin, gout, g1, g2):
        return [gin[:, :IN_W].reshape(D, N_DEV, wi).transpose(1, 0, 2), gout.reshape(N_DEV, w_out.shape[1], D),
                g1.reshape(D, N_DEV, w_ff1.shape[2]).transpose(1, 0, 2), g2.reshape(N_DEV, w_ff2.shape[1], D)]

    for l in reversed(range(L)):
        s = saved[l]
        W_in_l, W_out_l, W_1_l, W_2_l = weights[l]
        sh_a, sc_a, g_a, sh_m, sc_m, g_m = [mod_part(l, i) for i in range(6)]
        gate_k = lambda g, blk: (g, (1, 1, blk), lambda i, j, k: (i * tm // T, 0, k))
        gate_tok = lambda g, blk: (g, (1, 1, blk), lambda i, j, k: (k * tk_tok // T, 0, j))
        dx2 = dx.reshape(N, D)
        dg_m = rowsum_prod(dx, s["y2"].reshape(B, T, D), "rowsum_prod")
        du = matmul(dx2, W_2_l, mode="nt", name="mm_ff2_da", out_dtypes=(BF16,), tm=tm, a_fn=times_gate,
                    a_extras=[gate_k(g_m, _pick(D, 1024))],
                    epi=lambda acc, uv: (acc * (2.0 * jnp.maximum(uv, 0.0)),),
                    extras=[(s["u"], (tm, _pick(FF, 1024)), tile_ij)])[0]
        g_w2.append(matmul(s["u"], dx2, mode="tn", name="mm_ff2_dw", out_dtypes=(BF16,), tk=tk_tok, a_fn=relu2,
                           b_fn=times_gate, b_extras=[gate_tok(g_m, _pick(D, 1024))])[0])
        g_w1.append(matmul(s["h2"].reshape(N, D), du, mode="tn", name="mm_ff1_dw", out_dtypes=(BF16,))[0])
        dh2 = matmul(du, W_1_l, mode="nt", name="mm_ff1_da")[0]
        dx_mid, dgn_mlp, dsc_m, dsh_m = ln_mod_bwd(s["x_mid"], norm_mlp[l:l + 1], sc_m, dh2.reshape(B, T, D), dx, "ln_mod_bwd")
        dxm2 = dx_mid.reshape(N, D)
        dg_a = rowsum_prod(dx_mid, s["y1"].reshape(B, T, D), "rowsum_prod")
        dmix3 = matmul(dxm2, W_out_l, mode="nt", name="mm_out_da", tm=tm, a_fn=times_gate,
                       a_extras=[gate_k(g_a, _pick(D, 1024))])[0].reshape(B, T, SB_W + DN_W)
        g_wout.append(matmul(s["mix"].reshape(N, SB_W + DN_W), dxm2, mode="tn", name="mm_out_dw", out_dtypes=(BF16,),
                             tk=tk_tok, b_fn=times_gate, b_extras=[gate_tok(g_a, _pick(D, 1024))])[0])
        if pending is not None:
            dq_a, dk_a, dv_a, dgq, dgk, *lands = sb_attn_bwd(
                s["proj3"], sb_q_norm[l:l + 1], sb_k_norm[l:l + 1], s["tot"], dmix3, "sb_attn_bwd_scatter",
                exch=("scatter", pending))
            parts.append(lands)
        else:
            dq_a, dk_a, dv_a, dgq, dgk = sb_attn_bwd(s["proj3"], sb_q_norm[l:l + 1], sb_k_norm[l:l + 1], s["tot"], dmix3, "sb_attn_bwd")
        dob, dz, dgn_dn = gdn_post_bwd(s["ob"], s["proj3"], dn_out_norm[l:l + 1], dmix3, "gdn_post_bwd")
        dqkv, dgates = gdn_chunk_bwd(s["qkv"], s["gates"], s["states"], s["tms"], dob, "gdn_chunk_bwd")
        d_dnqkv, dconv_b = gdn_pre_bwd(s["proj3"], conv_full[l], dqkv, "gdn_pre_bwd")
        d_ab, dalog, ddtb = gdn_gates_bwd(s["proj3"], alog_row[l], dtb_row[l], dgates, "gdn_gates_bwd")
        dproj = jnp.concatenate([dq_a, dk_a, dv_a, d_dnqkv, dz, d_ab], axis=-1).reshape(N, IN_WP)
        g_win.append(matmul(s["h"].reshape(N, D), dproj, mode="tn", name="mm_proj_dw", out_dtypes=(BF16,), tm=512, tk=512)[0])
        dh = matmul(dproj, W_in_l, mode="nt", name="mm_proj_da", tm=512, tk=IN_WP)[0]
        dx, dgn_mix, dsc_a, dsh_a = ln_mod_bwd(s["x"], norm_mix[l:l + 1], sc_a, dh.reshape(B, T, D), dx_mid, "ln_mod_bwd")
        pending = shard_layer(g_win[-1], g_wout[-1], g_w1[-1], g_w2[-1])
        dmods.append(jnp.concatenate([dsh_a, dsc_a, dg_a, dsh_m, dsc_m, dg_m], axis=-1).reshape(B, 6 * D))
        smalls.append(dict(norm_mix=dgn_mix, norm_mlp=dgn_mlp, sbq=dgq, sbk=dgk, alog=dalog, dtb=ddtb, dnorm=dgn_dn,
                           conv=jnp.sum(dconv_b, axis=0)))
    parts.append([all_to_all(a, "comm_scatter_" + n) for a, n in zip(pending, ("w_in", "w_out", "w_ff1", "w_ff2"))])
    for lst in (dmods, smalls, parts):
        lst.reverse()
    grad_x = dx

    def update(parts, w, m, v, name):
        shp = w.shape
        r2 = lambda a: a.reshape(-1, shp[-1])
        outs = adamw(parts.reshape(parts.shape[0], -1, shp[-1]), r2(w), r2(m), r2(v), name)
        return [o.reshape(shp) for o in outs]

    p_win, p_wout, p_w1, p_w2 = [jnp.stack([parts[l][i] for l in range(L)], axis=1) for i in range(4)]
    r_win = update(p_win, w_in, m_w_in, v_w_in, "adamw_w_in")
    r_wout = update(p_wout, w_out, m_w_out, v_w_out, "adamw_w_out")
    r_w1 = update(p_w1, w_ff1, m_w_ff1, v_w_ff1, "adamw_w_ff1")
    r_w2 = update(p_w2, w_ff2, m_w_ff2, v_w_ff2, "adamw_w_ff2")

    dmod_g = all_gather(jnp.stack(dmods), "comm_gather_dmod")
    dmod_all = dmod_g.transpose(1, 0, 2, 3).reshape(L, N_DEV * B, 6 * D)
    g_wada = ada_bwd(c_all, lax.dynamic_slice_in_dim(dmod_all, me * WA, WA, axis=2), "ada_bwd")
    r_wada = update(g_wada[None], w_ada, m_w_ada, v_w_ada, "adamw_w_ada")
    r_bada = update(dmod_g.transpose(0, 2, 1, 3).reshape(N_DEV * B, L, 6 * D), b_ada, m_b_ada, v_b_ada, "adamw_b_ada")

    def pack(f):
        return jnp.concatenate([
            _rows128(f("norm_mix")), _rows128(f("norm_mlp")), _rows128(f("sbq")), _rows128(f("sbk")),
            f("alog"), f("dtb"), f("dnorm"), _rows128(f("conv"))], axis=0)

    names = ["norm_mix", "norm_mlp", "sbq", "sbk", "alog", "dtb", "dnorm"]
    part = pack(lambda n: jnp.concatenate([sm[n] for sm in smalls], axis=0))
    n_rep = part.shape[0] - L * CONV_K * 3 * DN_W // 128
    part_g = all_gather(part, "comm_gather_small")
    params = dict(norm_mix=(norm_mix, m_norm_mix, v_norm_mix), norm_mlp=(norm_mlp, m_norm_mlp, v_norm_mlp),
                  sbq=(sb_q_norm, m_sb_q_norm, v_sb_q_norm), sbk=(sb_k_norm, m_sb_k_norm, v_sb_k_norm),
                  alog=(a_log, m_a_log, v_a_log), dtb=(dt_bias, m_dt_bias, v_dt_bias),
                  dnorm=(dn_out_norm, m_dn_out_norm, v_dn_out_norm))

    def rows_of(n, a):
        return _pad_lanes(a) if n in ("alog", "dtb") else _rows128(a)

    packed = [jnp.concatenate([rows_of(n, params[n][i]) for n in names], axis=0) for i in range(3)]
    r_small = adamw(part_g[:, :n_rep], packed[0], packed[1], packed[2], "adamw_small")
    small_out = {}
    off = 0
    for n in names:
        w0 = params[n][0]
        nr = rows_of(n, w0).shape[0]
        vals = [o[off:off + nr] for o in r_small]
        small_out[n] = [(vv[:, :w0.shape[1]] if n in ("alog", "dtb") else vv.reshape(w0.shape)) for vv in vals]
        off += nr
    conv_parts = part_g[:, n_rep:].reshape(N_DEV, L, CONV_K, 3 * DN_W)
    r_conv = update(lax.dynamic_slice_in_dim(conv_parts, me * CS, CS, axis=3), conv_w, m_conv_w, v_conv_w, "adamw_conv")

    order = [r_wada, r_bada, small_out["norm_mix"], small_out["norm_mlp"], r_win, small_out["sbq"], small_out["sbk"],
             r_conv, small_out["alog"], small_out["dtb"], small_out["dnorm"], r_wout, r_w1, r_w2]
    outs = [loss, grad_x]
    for i in range(4):
        outs += [r[i] for r in order]
    return tuple(outs)
```

```python
import functools
import math

import jax
import jax.numpy as jnp
from jax import lax
from jax.experimental import pallas as pl
from jax.experimental.pallas import tpu as pltpu

F32 = jnp.float32
BF16 = jnp.bfloat16
EPS = 1e-6
N_DEV = 8
AXES = ("x", "y", "c")

SB_HEADS, SB_HD = 8, 64
SB_W = SB_HEADS * SB_HD
SB_BLK = 128
DN_HEADS, DN_HD = 4, 128
DN_W = DN_HEADS * DN_HD
DN_C = 64
CONV_K = 4
IN_W = 3 * SB_W + 4 * DN_W + 2 * DN_HEADS
IN_WP = 3 * SB_W + 4 * DN_W + 128
COL_DNQKV = 3 * SB_W // 128
COL_Z = COL_DNQKV + 3 * DN_W // 128
COL_AB = COL_Z + DN_W // 128

ADAM_LR, ADAM_B1, ADAM_B2, ADAM_EPS, ADAM_WD, ADAM_STEP = 0.001, 0.9, 0.999, 1e-08, 0.01, 10

VMEM_LIMIT = 56 * 1024 * 1024


def _cp(sem):
    return pltpu.CompilerParams(dimension_semantics=sem, vmem_limit_bytes=VMEM_LIMIT)


def _pick(dim, pref):
    return pref if dim % pref == 0 else dim


def _sds(shape, dtype):
    return jax.ShapeDtypeStruct(tuple(shape), dtype)


_NN = (((1,), (0,)), ((), ()))
_NT = (((1,), (1,)), ((), ()))
_TN = (((0,), (0,)), ((), ()))


def _bdot(a, b, dims=_NN):
    return lax.dot_general(a.astype(BF16), b.astype(BF16), dims, preferred_element_type=F32)


def _split(a):
    hi = a.astype(BF16)
    lo = (a - hi.astype(F32)).astype(BF16)
    return hi, lo


def _pdot(a, b, dims=_NN):
    ah, al = _split(a)
    bh, bl = _split(b)
    d = functools.partial(lax.dot_general, dimension_numbers=dims, preferred_element_type=F32)
    return d(ah, bh) + (d(ah, bl) + d(al, bh))


def _sigmoid(x):
    return 1.0 / (1.0 + jnp.exp(-x))


def _softplus(x):
    return jnp.maximum(x, 0.0) + jnp.log(1.0 + jnp.exp(-jnp.abs(x)))


def matmul(a, b, *, mode, name, out_dtypes=(F32,), a_fn=None, a_extras=(), b_fn=None, b_extras=(),
           epi=None, extras=(), tm=1024, tn=1024, tk=1024):
    if mode == "tn":
        K, M = a.shape
    else:
        M, K = a.shape
    N = b.shape[0] if mode == "nt" else b.shape[1]
    tm, tn, tk = _pick(M, tm), _pick(N, tn), _pick(K, tk)
    nk = K // tk
    dims = {"nn": _NN, "nt": _NT, "tn": _TN}[mode]
    a_spec = pl.BlockSpec((tk, tm), lambda i, j, k: (k, i)) if mode == "tn" else pl.BlockSpec((tm, tk), lambda i, j, k: (i, k))
    b_spec = pl.BlockSpec((tn, tk), lambda i, j, k: (j, k)) if mode == "nt" else pl.BlockSpec((tk, tn), lambda i, j, k: (k, j))
    na, nb, ne, no = len(a_extras), len(b_extras), len(extras), len(out_dtypes)

    def body(*refs):
        a_ref, b_ref = refs[0], refs[1]
        ax = refs[2:2 + na]
        bx = refs[2 + na:2 + na + nb]
        ex = refs[2 + na + nb:2 + na + nb + ne]
        outs = refs[2 + na + nb + ne:2 + na + nb + ne + no]
        acc_ref = refs[-1]
        k = pl.program_id(2)

        @pl.when(k == 0)
        def _():
            acc_ref[...] = jnp.zeros_like(acc_ref)

        av = a_ref[...]
        if a_fn is not None:
            av = a_fn(av, *[r[...] for r in ax])
        bv = b_ref[...]
        if b_fn is not None:
            bv = b_fn(bv, *[r[...] for r in bx])
        acc_ref[...] += lax.dot_general(av.astype(BF16), bv.astype(BF16), dims, preferred_element_type=F32)

        @pl.when(k == nk - 1)
        def _():
            res = acc_ref[...]
            res = (res,) if epi is None else epi(res, *[r[...] for r in ex])
            for o_ref, r in zip(outs, res):
                o_ref[...] = r.astype(o_ref.dtype)

    xs = list(a_extras) + list(b_extras) + list(extras)
    return pl.pallas_call(
        body,
        grid=(M // tm, N // tn, nk),
        in_specs=[a_spec, b_spec] + [pl.BlockSpec(bs, im) for _, bs, im in xs],
        out_specs=[pl.BlockSpec((tm, tn), lambda i, j, k: (i, j)) for _ in out_dtypes],
        out_shape=[_sds((M, N), dt) for dt in out_dtypes],
        scratch_shapes=[pltpu.VMEM((tm, tn), F32)],
        compiler_params=_cp(("parallel", "parallel", "arbitrary")),
        name=name,
    )(a, b, *[x for x, _, _ in xs])


def proj_bwd_input(pieces, w, name, tm=512, tn=1024):
    N, D = pieces[0].shape[0], w.shape[0]
    widths = [p.shape[1] for p in pieces]
    offs = [sum(widths[:i]) for i in range(len(widths))]
    tm, tn = _pick(N, tm), _pick(D, tn)

    def body(*refs):
        w_ref, o_ref = refs[len(pieces)], refs[len(pieces) + 1]
        acc = None
        for p_ref, off, wd in zip(refs, offs, widths):
            t = lax.dot_general(p_ref[...].astype(BF16), w_ref[:, off:off + wd], _NT, preferred_element_type=F32)
            acc = t if acc is None else acc + t
        o_ref[...] = acc

    return pl.pallas_call(
        body, grid=(N // tm, D // tn),
        in_specs=[pl.BlockSpec((tm, wd), lambda i, j: (i, 0)) for wd in widths] + [pl.BlockSpec((tn, sum(widths)), lambda i, j: (j, 0))],
        out_specs=pl.BlockSpec((tm, tn), lambda i, j: (i, j)), out_shape=_sds((N, D), F32),
        compiler_params=_cp(("parallel", "parallel")), name=name,
    )(*pieces, w)


def proj_bwd_weight(h, pieces, name, tm=512, tk=512):
    N, D = h.shape
    widths = [p.shape[1] for p in pieces]
    offs = [sum(widths[:i]) for i in range(len(widths))]
    tm, tk = _pick(D, tm), _pick(N, tk)
    nk = N // tk

    def body(*refs):
        h_ref, o_ref, acc_ref = refs[0], refs[len(pieces) + 1], refs[len(pieces) + 2]
        k = pl.program_id(1)

        @pl.when(k == 0)
        def _():
            acc_ref[...] = jnp.zeros_like(acc_ref)

        hv = h_ref[...]
        for p_ref, off, wd in zip(refs[1:], offs, widths):
            acc_ref[:, off:off + wd] += lax.dot_general(hv, p_ref[...].astype(BF16), _TN, preferred_element_type=F32)

        @pl.when(k == nk - 1)
        def _():
            o_ref[...] = acc_ref[...].astype(o_ref.dtype)

    return pl.pallas_call(
        body, grid=(D // tm, nk),
        in_specs=[pl.BlockSpec((tk, tm), lambda i, k: (k, i))] + [pl.BlockSpec((tk, wd), lambda i, k: (k, 0)) for wd in widths],
        out_specs=pl.BlockSpec((tm, sum(widths)), lambda i, k: (i, 0)), out_shape=_sds((D, sum(widths)), BF16),
        scratch_shapes=[pltpu.VMEM((tm, sum(widths)), F32)],
        compiler_params=_cp(("parallel", "arbitrary")), name=name,
    )(h, *pieces)


def ln_mod_fwd(x, gain, sc, sh, name):
    B, T, D = x.shape
    tt = _pick(T, 512)

    def body(x_ref, g_ref, sc_ref, sh_ref, h_ref):
        xv = x_ref[0]
        r = lax.rsqrt(jnp.mean(xv * xv, axis=-1, keepdims=True) + EPS)
        h = (xv * r * g_ref[...]) * (1.0 + sc_ref[0]) + sh_ref[0]
        h_ref[0] = h.astype(h_ref.dtype)

    return pl.pallas_call(
        body, grid=(B, T // tt),
        in_specs=[pl.BlockSpec((1, tt, D), lambda b, t: (b, t, 0)), pl.BlockSpec((1, D), lambda b, t: (0, 0)),
                  pl.BlockSpec((1, 1, D), lambda b, t: (b, 0, 0)), pl.BlockSpec((1, 1, D), lambda b, t: (b, 0, 0))],
        out_specs=pl.BlockSpec((1, tt, D), lambda b, t: (b, t, 0)),
        out_shape=_sds((B, T, D), BF16),
        compiler_params=_cp(("parallel", "parallel")), name=name,
    )(x, gain, sc, sh)


def ln_mod_bwd(x, gain, sc, dh, dres, name):
    B, T, D = x.shape
    tt = _pick(T, 512)

    def body(x_ref, g_ref, sc_ref, dh_ref, dres_ref, dx_ref, dg_ref, dsc_ref, dsh_ref):
        b, t = pl.program_id(0), pl.program_id(1)
        xv, dhv = x_ref[0], dh_ref[0]
        g, s = g_ref[...], sc_ref[0]
        r = lax.rsqrt(jnp.mean(xv * xv, axis=-1, keepdims=True) + EPS)
        xn = xv * r
        dxn = dhv * (g * (1.0 + s))
        dx_ref[0] = dres_ref[0] + r * (dxn - xn * jnp.mean(dxn * xn, axis=-1, keepdims=True))
        s1 = jnp.sum(dhv * xn, axis=0, keepdims=True)
        s2 = jnp.sum(dhv, axis=0, keepdims=True)

        @pl.when(t == 0)
        def _():
            dsc_ref[0] = jnp.zeros_like(s1)
            dsh_ref[0] = jnp.zeros_like(s1)

        @pl.when((t == 0) & (b == 0))
        def _():
            dg_ref[...] = jnp.zeros_like(s1)

        dsc_ref[0] += s1 * g
        dsh_ref[0] += s2
        dg_ref[...] += s1 * (1.0 + s)

    tile = pl.BlockSpec((1, tt, D), lambda b, t: (b, t, 0))
    row = pl.BlockSpec((1, D), lambda b, t: (0, 0))
    brow = pl.BlockSpec((1, 1, D), lambda b, t: (b, 0, 0))
    return pl.pallas_call(
        body, grid=(B, T // tt),
        in_specs=[tile, row, brow, tile, tile],
        out_specs=[tile, row, brow, brow],
        out_shape=[_sds((B, T, D), F32), _sds((1, D), F32), _sds((B, 1, D), F32), _sds((B, 1, D), F32)],
        compiler_params=_cp(("arbitrary", "arbitrary")), name=name,
    )(x, gain, sc, dh, dres)


def rowsum_prod(a, b, name):
    B, T, D = a.shape
    tt = _pick(T, 512)

    def body(a_ref, b_ref, o_ref):
        @pl.when(pl.program_id(1) == 0)
        def _():
            o_ref[...] = jnp.zeros_like(o_ref)

        o_ref[0] += jnp.sum(a_ref[0] * b_ref[0], axis=0, keepdims=True)

    tile = pl.BlockSpec((1, tt, D), lambda b, t: (b, t, 0))
    return pl.pallas_call(
        body, grid=(B, T // tt), in_specs=[tile, tile],
        out_specs=pl.BlockSpec((1, 1, D), lambda b, t: (b, 0, 0)),
        out_shape=_sds((B, 1, D), F32),
        compiler_params=_cp(("parallel", "arbitrary")), name=name,
    )(a, b)


def loss_grad(y, tgt, name):
    B, T, D = y.shape
    tt = _pick(T, 512)

    def body(y_ref, t_ref, dy_ref, s_ref):
        @pl.when((pl.program_id(0) == 0) & (pl.program_id(1) == 0))
        def _():
            s_ref[...] = jnp.zeros_like(s_ref)

        e = y_ref[0] - t_ref[0]
        dy_ref[0] = e * (1.0 / D)
        s_ref[...] += jnp.sum(e * e, axis=0, keepdims=True)

    tile = pl.BlockSpec((1, tt, D), lambda b, t: (b, t, 0))
    return pl.pallas_call(
        body, grid=(B, T // tt), in_specs=[tile, tile],
        out_specs=[tile, pl.BlockSpec((1, D), lambda b, t: (0, 0))],
        out_shape=[_sds((B, T, D), F32), _sds((1, D), F32)],
        compiler_params=_cp(("arbitrary", "arbitrary")), name=name,
    )(y, tgt)


def _sb_group(nb):
    return 4 if nb % 4 == 0 else (2 if nb % 2 == 0 else 1)


def _sb_qrows(t, kw):
    return 256 if (t % 256 == 0 and kw % 256 == 0) else SB_BLK


def _tri_sum(x, tri2):
    hi, lo = _split(x)
    return lax.dot_general(jnp.concatenate([hi, lo], axis=1), tri2, _NN, preferred_element_type=F32)


def _tri2(cond):
    t = cond.astype(BF16)
    return jnp.concatenate([t, t], axis=0)


def sb_attn_fwd(proj3, gq, gk, name, exch=None):
    B, T, _ = proj3.shape
    NB = T // SB_BLK
    G = _sb_group(NB)
    KW = G * SB_BLK
    QB = _sb_qrows(T, KW)
    scale = SB_HD ** -0.5

    def body(q_ref, k_ref, v_ref, gq_ref, gk_ref, o_ref, tot_ref, qn_s, kn_s, v_s):
        row_io = lax.broadcasted_iota(jnp.int32, (SB_BLK, SB_BLK), 0)
        col_io = lax.broadcasted_iota(jnp.int32, (SB_BLK, SB_BLK), 1)
        tri = _tri2(row_io > col_io)

        def prep(i, _):
            rows = pl.ds(pl.multiple_of(i * SB_BLK, SB_BLK), SB_BLK)
            for hh in range(2):
                sl = slice(hh * SB_HD, (hh + 1) * SB_HD)
                q = q_ref[0, rows, sl]
                k = k_ref[0, rows, sl]
                qn_s[hh, rows, :] = (q * lax.rsqrt(jnp.mean(q * q, -1, keepdims=True) + EPS) * (gq_ref[...] * scale)).astype(BF16)
                kn_s[hh, rows, :] = (k * lax.rsqrt(jnp.mean(k * k, -1, keepdims=True) + EPS) * gk_ref[...]).astype(BF16)
                v_s[hh, rows, :] = v_ref[0, rows, sl].astype(BF16)
            return 0

        lax.fori_loop(0, NB, prep, 0)

        diff_w = (lax.broadcasted_iota(jnp.int32, (QB, KW), 1)
                  - lax.broadcasted_iota(jnp.int32, (QB, KW), 0))

        def qblock(i, _):
            rows = pl.ds(pl.multiple_of(i * QB, QB), QB)
            qn = [qn_s[hh, rows, :] for hh in range(2)]
            nsj = ((i + 1) * QB - 1) // KW + 1

            def sblock(sj, carry, masked):
                cols = pl.ds(pl.multiple_of(sj * KW, KW), KW)
                mask = diff_w < i * QB - (nsj - 1) * KW
                zs = [lax.dot_general(qn[hh], kn_s[hh, cols, :], _NT, preferred_element_type=F32) for hh in range(2)]
                lgs, lss = [], []
                for hh in range(2):
                    sp = _softplus(zs[hh])
                    lgs.append(jnp.where(mask, -sp, 0.0) if masked else -sp)
                    lss.append(zs[hh] - sp)
                blocks = [lgs[hh][:, s * SB_BLK:(s + 1) * SB_BLK] for hh in range(2) for s in range(G)]
                ts_all = _tri_sum(jnp.concatenate(blocks, axis=0), tri)
                atts, css = [], []
                for hh in range(2):
                    cs = carry[hh][1]
                    ps = []
                    for s in reversed(range(G)):
                        n = hh * G + s
                        ts = ts_all[n * QB:(n + 1) * QB]
                        ps.append(lss[hh][:, s * SB_BLK:(s + 1) * SB_BLK] + ts + cs)
                        cs = cs + (ts[:, :1] + blocks[n][:, :1])
                    p = ps[0] if G == 1 else jnp.concatenate(ps[::-1], axis=1)
                    att = jnp.exp(p)
                    atts.append((jnp.where(mask, att, 0.0) if masked else att).astype(BF16))
                    css.append(cs)
                return tuple((carry[hh][0] + lax.dot_general(atts[hh], v_s[hh, cols, :], _NN, preferred_element_type=F32), css[hh])
                             for hh in range(2))

            init = (jnp.zeros((QB, SB_HD), F32), jnp.zeros((QB, 1), F32))
            res = sblock(nsj - 1, (init, init), True)
            res = lax.fori_loop(0, nsj - 1, lambda jj, c: sblock(nsj - 2 - jj, c, False), res)
            for hh in range(2):
                o_ref[0, rows, hh * SB_HD:(hh + 1) * SB_HD] = res[hh][0].astype(o_ref.dtype)
                tot_ref[0, hh, rows, :] = res[hh][1]
            return 0

        lax.fori_loop(0, T // QB, qblock, 0)

    blk = lambda off: pl.BlockSpec((1, T, 128), lambda b, p: (b, 0, off + p))
    grow = pl.BlockSpec((1, SB_HD), lambda b, p: (0, 0))
    return _hosted_call(
        body, grid=(B, SB_W // 128),
        in_specs=[blk(0), blk(SB_W // 128), blk(2 * SB_W // 128), grow, grow],
        out_specs=[pl.BlockSpec((1, T, 128), lambda b, p: (b, 0, p)), pl.BlockSpec((1, 2, T, 1), lambda b, p: (b, p, 0, 0))],
        out_shape=[_sds((B, T, SB_W), BF16), _sds((B, SB_HEADS, T, 1), F32)],
        scratch_shapes=[pltpu.VMEM((2, T, SB_HD), BF16)] * 3,
        name=name, args=(proj3, proj3, proj3, gq, gk), exch=exch)


def sb_attn_bwd(proj3, gq, gk, tot, dmix3, name, exch=None):
    B, T, _ = proj3.shape
    NB = T // SB_BLK
    G = _sb_group(NB)
    KW = G * SB_BLK
    QB = _sb_qrows(T, KW)
    scale = SB_HD ** -0.5

    def body(q_ref, k_ref, v_ref, gq_ref, gk_ref, tot_ref, do_ref, dq_ref, dk_ref, dv_ref, dgq_ref, dgk_ref,
             qn_s, kn_s, v_s, do_s, dqn_s, dkn_s, dv_s):
        row_io = lax.broadcasted_iota(jnp.int32, (SB_BLK, SB_BLK), 0)
        col_io = lax.broadcasted_iota(jnp.int32, (SB_BLK, SB_BLK), 1)
        tri = _tri2(row_io > col_io)
        trip = _tri2(row_io < col_io)

        @pl.when((pl.program_id(0) == 0) & (pl.program_id(1) == 0))
        def _():
            dgq_ref[...] = jnp.zeros_like(dgq_ref)
            dgk_ref[...] = jnp.zeros_like(dgk_ref)

        def prep(i, _):
            rows = pl.ds(pl.multiple_of(i * SB_BLK, SB_BLK), SB_BLK)
            for hh in range(2):
                sl = slice(hh * SB_HD, (hh + 1) * SB_HD)
                q = q_ref[0, rows, sl]
                k = k_ref[0, rows, sl]
                qn_s[hh, rows, :] = (q * lax.rsqrt(jnp.mean(q * q, -1, keepdims=True) + EPS) * (gq_ref[...] * scale)).astype(BF16)
                kn_s[hh, rows, :] = (k * lax.rsqrt(jnp.mean(k * k, -1, keepdims=True) + EPS) * gk_ref[...]).astype(BF16)
                v_s[hh, rows, :] = v_ref[0, rows, sl].astype(BF16)
                do_s[hh, rows, :] = do_ref[0, rows, sl].astype(BF16)
            return 0

        lax.fori_loop(0, NB, prep, 0)
        dkn_s[...] = jnp.zeros_like(dkn_s)
        dv_s[...] = jnp.zeros_like(dv_s)

        diff_w = (lax.broadcasted_iota(jnp.int32, (QB, KW), 1)
                  - lax.broadcasted_iota(jnp.int32, (QB, KW), 0))

        def qblock(i, _):
            rows = pl.ds(pl.multiple_of(i * QB, QB), QB)
            qn = [qn_s[hh, rows, :] for hh in range(2)]
            dov = [do_s[hh, rows, :] for hh in range(2)]
            tot = [tot_ref[0, hh, rows, :] for hh in range(2)]
            nsj = ((i + 1) * QB - 1) // KW + 1

            def sblock(sj, carry, masked):
                cols = pl.ds(pl.multiple_of(sj * KW, KW), KW)
                mask = diff_w < i * QB - (nsj - 1) * KW
                hs = range(2)
                kns = [kn_s[hh, cols, :] for hh in hs]
                zs = [lax.dot_general(qn[hh], kns[hh], _NT, preferred_element_type=F32) for hh in hs]
                datts = [lax.dot_general(dov[hh], v_s[hh, cols, :], _NT, preferred_element_type=F32) for hh in hs]
                lgs, lss = [], []
                for hh in hs:
                    sp = _softplus(zs[hh])
                    lgs.append(jnp.where(mask, -sp, 0.0) if masked else -sp)
                    lss.append(zs[hh] - sp)
                blocks = [lgs[hh][:, s * SB_BLK:(s + 1) * SB_BLK] for hh in hs for s in range(G)]
                ts_all = _tri_sum(jnp.concatenate(blocks, axis=0), tri)
                atts, dps, cums = [], [], []
                for hh in hs:
                    cum = carry[hh][1]
                    ps = []
                    for s in range(G):
                        n = hh * G + s
                        ts = ts_all[n * QB:(n + 1) * QB]
                        cum = cum + (ts[:, :1] + blocks[n][:, :1])
                        ps.append(lss[hh][:, s * SB_BLK:(s + 1) * SB_BLK] + ts + (tot[hh] - cum))
                    p = ps[0] if G == 1 else jnp.concatenate(ps, axis=1)
                    att = jnp.exp(p)
                    att = jnp.where(mask, att, 0.0) if masked else att
                    atts.append(att.astype(BF16))
                    dps.append(att * datts[hh])
                    cums.append(cum)
                dblocks = [dps[hh][:, s * SB_BLK:(s + 1) * SB_BLK] for hh in hs for s in range(G)]
                tp_all = _tri_sum(jnp.concatenate(dblocks, axis=0), trip)
                dzs, cdps = [], []
                for hh in hs:
                    cdp = carry[hh][2]
                    dls = []
                    for s in range(G):
                        n = hh * G + s
                        tp = tp_all[n * QB:(n + 1) * QB]
                        dls.append(tp + cdp)
                        cdp = cdp + (tp[:, SB_BLK - 1:] + dblocks[n][:, SB_BLK - 1:])
                    dlg = dls[0] if G == 1 else jnp.concatenate(dls, axis=1)
                    dz = dps[hh] - jnp.exp(lss[hh]) * (dps[hh] + dlg)
                    dzs.append((jnp.where(mask, dz, 0.0) if masked else dz).astype(BF16))
                    cdps.append(cdp)
                new = []
                for hh in hs:
                    dq = carry[hh][0] + lax.dot_general(dzs[hh], kns[hh], _NN, preferred_element_type=F32)
                    dkn_s[hh, cols, :] += lax.dot_general(dzs[hh], qn[hh], _TN, preferred_element_type=F32)
                    dv_s[hh, cols, :] += lax.dot_general(atts[hh], dov[hh], _TN, preferred_element_type=F32)
                    new.append((dq, cums[hh], cdps[hh]))
                return tuple(new)

            z1 = jnp.zeros((QB, 1), F32)
            init = (jnp.zeros((QB, SB_HD), F32), z1, z1)
            res = lax.fori_loop(0, nsj - 1, lambda sj, c: sblock(sj, c, False), (init, init))
            res = sblock(nsj - 1, res, True)
            for hh in range(2):
                dqn_s[hh, rows, :] = res[hh][0]
            return 0

        lax.fori_loop(0, T // QB, qblock, 0)

        def fin(i, carry):
            aq, ak = carry
            rows = pl.ds(pl.multiple_of(i * SB_BLK, SB_BLK), SB_BLK)
            for hh in range(2):
                sl = slice(hh * SB_HD, (hh + 1) * SB_HD)
                for src_ref, d_s, g_ref, out_ref, mult, which in ((q_ref, dqn_s, gq_ref, dq_ref, scale, 0), (k_ref, dkn_s, gk_ref, dk_ref, 1.0, 1)):
                    xr = src_ref[0, rows, sl]
                    r = lax.rsqrt(jnp.mean(xr * xr, -1, keepdims=True) + EPS)
                    dy = d_s[hh, rows, :] * mult
                    u = dy * g_ref[...]
                    out_ref[0, rows, sl] = r * u - xr * (r * r * r) * jnp.mean(u * xr, -1, keepdims=True)
                    part = jnp.sum(dy * xr * r, axis=0, keepdims=True)
                    if which == 0:
                        aq = aq + part
                    else:
                        ak = ak + part
                dv_ref[0, rows, sl] = dv_s[hh, rows, :]
            return aq, ak

        z64 = jnp.zeros((1, SB_HD), F32)
        aq, ak = lax.fori_loop(0, NB, fin, (z64, z64))
        dgq_ref[...] += aq
        dgk_ref[...] += ak

    blk = lambda off: pl.BlockSpec((1, T, 128), lambda b, p: (b, 0, off + p))
    grow = pl.BlockSpec((1, SB_HD), lambda b, p: (0, 0))
    return _hosted_call(
        body, grid=(B, SB_W // 128),
        in_specs=[blk(0), blk(SB_W // 128), blk(2 * SB_W // 128), grow, grow,
                  pl.BlockSpec((1, 2, T, 1), lambda b, p: (b, p, 0, 0)), blk(0)],
        out_specs=[blk(0), blk(0), blk(0), grow, grow],
        out_shape=[_sds((B, T, SB_W), F32)] * 3 + [_sds((1, SB_HD), F32)] * 2,
        scratch_shapes=[pltpu.VMEM((2, T, SB_HD), BF16)] * 4 + [pltpu.VMEM((2, T, SB_HD), F32)] * 3,
        name=name, args=(proj3, proj3, proj3, gq, gk, tot, dmix3), exch=exch)


def _exch_copies(kind, src_refs, land_refs, sems, with_arrivals=True):
    x_, y_, c_ = _coords()
    me = 4 * x_ + 2 * y_ + c_
    local, go, arrive = [], [], []
    for a, (src, land) in enumerate(zip(src_refs, land_refs)):
        send, recv, loc = sems[3 * a:3 * a + 3]
        local.append(pltpu.make_async_copy(src if kind == "gather" else src.at[me], land.at[me], loc))
        for k in range(1, N_DEV):
            px = 1 - x_ if k & 4 else x_
            py = 1 - y_ if k & 2 else y_
            pc = 1 - c_ if k & 1 else c_
            peer = 4 * px + 2 * py + pc
            out = src if kind == "gather" else src.at[peer]
            mk = functools.partial(pltpu.make_async_remote_copy, send_sem=send.at[k - 1], recv_sem=recv.at[k - 1],
                                   device_id=(px, py, pc), device_id_type=pl.DeviceIdType.MESH)
            go.append(mk(src_ref=out, dst_ref=land.at[me]))
            if with_arrivals:
                arrive.append(mk(src_ref=out, dst_ref=land.at[peer]))
    return local, go, arrive


def _hosted_call(body, *, grid, in_specs, out_specs, out_shape, scratch_shapes, name, args, exch=None):
    sem = ("arbitrary",) * len(grid)
    if exch is None:
        return pl.pallas_call(body, grid=grid, in_specs=in_specs, out_specs=out_specs, out_shape=out_shape,
                              scratch_shapes=scratch_shapes, compiler_params=_cp(sem), name=name)(*args)
    kind, srcs = exch
    ns, n_in, n_out, n_scr = len(srcs), len(in_specs), len(out_specs), len(scratch_shapes)
    lands = [_sds((N_DEV,) + s.shape if kind == "gather" else s.shape, s.dtype) for s in srcs]

    def wrapped(*refs):
        ins, src_refs = refs[:n_in], refs[n_in:n_in + ns]
        outs, land_refs = refs[n_in + ns:n_in + ns + n_out], refs[n_in + ns + n_out:n_in + 2 * ns + n_out]
        scr, sems = refs[n_in + 2 * ns + n_out:n_in + 2 * ns + n_out + n_scr], refs[n_in + 2 * ns + n_out + n_scr:]
        ids = [pl.program_id(d) for d in range(len(grid))]
        first = functools.reduce(lambda a, b: a & b, [i == 0 for i in ids])
        last = functools.reduce(lambda a, b: a & b, [i == g - 1 for i, g in zip(ids, grid)])

        @pl.when(first)
        def _():
            local, go, _ = _exch_copies(kind, src_refs, land_refs, sems, with_arrivals=False)
            for cp in local + go:
                cp.start()

        body(*ins, *outs, *scr)

        @pl.when(last)
        def _():
            local, go, arrive = _exch_copies(kind, src_refs, land_refs, sems)
            for cp in arrive:
                cp.wait_recv()
            for cp in go:
                cp.wait_send()
            for cp in local:
                cp.wait()

    any_spec = pl.BlockSpec(memory_space=pl.ANY)
    sems = [pltpu.SemaphoreType.DMA((N_DEV - 1,)), pltpu.SemaphoreType.DMA((N_DEV - 1,)), pltpu.SemaphoreType.DMA] * ns
    return pl.pallas_call(
        wrapped, grid=grid, in_specs=list(in_specs) + [any_spec] * ns, out_specs=list(out_specs) + [any_spec] * ns,
        out_shape=list(out_shape) + lands, scratch_shapes=list(scratch_shapes) + sems,
        compiler_params=_cp(sem), name=name)(*args, *srcs)


def _conv_silu(x, w, T):
    t_io = lax.broadcasted_iota(jnp.int32, x.shape, 0)
    xs = [x] + [jnp.where(t_io >= s, pltpu.roll(x, s, 0), 0.0) for s in range(1, CONV_K)]
    y = xs[0] * w[CONV_K - 1:CONV_K, :]
    for s in range(1, CONV_K):
        y = y + xs[s] * w[CONV_K - 1 - s:CONV_K - s, :]
    return y, y * _sigmoid(y), xs


def gdn_pre_fwd(proj3, conv_w, name):
    B, T, _ = proj3.shape
    qs = DN_HD ** -0.5

    def body(x_ref, w_ref, o_ref):
        kind = pl.program_id(1) // DN_HEADS
        _, s, _ = _conv_silu(x_ref[0], w_ref[...], T)
        n = lax.rsqrt(jnp.sum(s * s, axis=-1, keepdims=True) + EPS)
        c = jnp.where(kind == 0, qs, 1.0)
        o_ref[0, 0] = jnp.where(kind < 2, s * (n * c), s)

    return pl.pallas_call(
        body, grid=(B, 3 * DN_HEADS),
        in_specs=[pl.BlockSpec((1, T, 128), lambda b, j: (b, 0, COL_DNQKV + j)), pl.BlockSpec((CONV_K, 128), lambda b, j: (0, j))],
        out_specs=pl.BlockSpec((1, 1, T, 128), lambda b, j: (b, j // DN_HEADS, 0, j % DN_HEADS)),
        out_shape=_sds((B, 3, T, DN_W), F32),
        compiler_params=_cp(("parallel", "parallel")), name=name,
    )(proj3, conv_w)


def gdn_pre_bwd(proj3, conv_w, dqkv, name):
    B, T, _ = proj3.shape
    qs = DN_HD ** -0.5

    def body(x_ref, w_ref, d_ref, dx_ref, dw_ref):
        kind = pl.program_id(1) // DN_HEADS
        w = w_ref[...]
        y, s, xs = _conv_silu(x_ref[0], w, T)
        dout = d_ref[0, 0]
        n = lax.rsqrt(jnp.sum(s * s, axis=-1, keepdims=True) + EPS)
        c = jnp.where(kind == 0, qs, 1.0)
        dsn = c * (n * dout - s * (n * n * n) * jnp.sum(dout * s, axis=-1, keepdims=True))
        ds = jnp.where(kind < 2, dsn, dout)
        sg = _sigmoid(y)
        dy = ds * (sg * (1.0 + y * (1.0 - sg)))
        t_io = lax.broadcasted_iota(jnp.int32, dy.shape, 0)
        dx = dy * w[CONV_K - 1:CONV_K, :]
        dw_ref[0, CONV_K - 1:CONV_K, :] = jnp.sum(dy * xs[0], axis=0, keepdims=True)
        for sft in range(1, CONV_K):
            dx = dx + jnp.where(t_io < T - sft, pltpu.roll(dy, T - sft, 0), 0.0) * w[CONV_K - 1 - sft:CONV_K - sft, :]
            dw_ref[0, CONV_K - 1 - sft:CONV_K - sft, :] = jnp.sum(dy * xs[sft], axis=0, keepdims=True)
        dx_ref[0] = dx

    return pl.pallas_call(
        body, grid=(B, 3 * DN_HEADS),
        in_specs=[pl.BlockSpec((1, T, 128), lambda b, j: (b, 0, COL_DNQKV + j)), pl.BlockSpec((CONV_K, 128), lambda b, j: (0, j)),
                  pl.BlockSpec((1, 1, T, 128), lambda b, j: (b, j // DN_HEADS, 0, j % DN_HEADS))],
        out_specs=[pl.BlockSpec((1, T, 128), lambda b, j: (b, 0, j)), pl.BlockSpec((1, CONV_K, 128), lambda b, j: (b, 0, j))],
        out_shape=[_sds((B, T, 3 * DN_W), F32), _sds((B, CONV_K, 3 * DN_W), F32)],
        compiler_params=_cp(("parallel", "parallel")), name=name,
    )(proj3, conv_w, dqkv)


def gdn_gates_fwd(proj3, alog_row, dtb_row, name):
    B, T, _ = proj3.shape

    def body(x_ref, al_ref, dt_ref, o_ref):
        x = x_ref[0]
        lane = lax.broadcasted_iota(jnp.int32, x.shape, 1)
        g = -jnp.exp(al_ref[...]) * _softplus(x + dt_ref[...])
        o_ref[0] = jnp.where(lane < DN_HEADS, g, jnp.where(lane < 2 * DN_HEADS, _sigmoid(x), 0.0))

    row = pl.BlockSpec((1, 128), lambda b: (0, 0))
    return pl.pallas_call(
        body, grid=(B,),
        in_specs=[pl.BlockSpec((1, T, 128), lambda b: (b, 0, COL_AB)), row, row],
        out_specs=pl.BlockSpec((1, T, 128), lambda b: (b, 0, 0)),
        out_shape=_sds((B, T, 128), F32),
        compiler_params=_cp(("parallel",)), name=name,
    )(proj3, alog_row, dtb_row)


def gdn_gates_bwd(proj3, alog_row, dtb_row, dgates, name):
    B, T, _ = proj3.shape

    def body(x_ref, al_ref, dt_ref, d_ref, dx_ref, dal_ref, ddt_ref):
        @pl.when(pl.program_id(0) == 0)
        def _():
            dal_ref[...] = jnp.zeros_like(dal_ref)
            ddt_ref[...] = jnp.zeros_like(ddt_ref)

        x, d = x_ref[0], d_ref[0]
        lane = lax.broadcasted_iota(jnp.int32, x.shape, 1)
        a = x + dt_ref[...]
        na = -jnp.exp(al_ref[...])
        da = jnp.where(lane < DN_HEADS, d * na * _sigmoid(a), 0.0)
        bt = _sigmoid(x)
        dx_ref[0] = da + jnp.where((lane >= DN_HEADS) & (lane < 2 * DN_HEADS), d * bt * (1.0 - bt), 0.0)
        dal_ref[...] += jnp.sum(jnp.where(lane < DN_HEADS, d * na * _softplus(a), 0.0), axis=0, keepdims=True)
        ddt_ref[...] += jnp.sum(da, axis=0, keepdims=True)

    row = pl.BlockSpec((1, 128), lambda b: (0, 0))
    tile = pl.BlockSpec((1, T, 128), lambda b: (b, 0, 0))
    return pl.pallas_call(
        body, grid=(B,),
        in_specs=[pl.BlockSpec((1, T, 128), lambda b: (b, 0, COL_AB)), row, row, tile],
        out_specs=[tile, row, row],
        out_shape=[_sds((B, T, 128), F32), _sds((1, 128), F32), _sds((1, 128), F32)],
        compiler_params=_cp(("arbitrary",)), name=name,
    )(proj3, alog_row, dtb_row, dgates)


def gdn_post_fwd(ob, proj3, gain, name):
    B, T, _ = ob.shape

    def body(o_ref, z_ref, g_ref, out_ref):
        o, z = o_ref[0], z_ref[0]
        r = lax.rsqrt(jnp.mean(o * o, axis=-1, keepdims=True) + EPS)
        out_ref[0] = ((o * r * g_ref[...]) * (z * _sigmoid(z))).astype(out_ref.dtype)

    tile = pl.BlockSpec((1, T, 128), lambda b, h: (b, 0, h))
    return pl.pallas_call(
        body, grid=(B, DN_HEADS),
        in_specs=[tile, pl.BlockSpec((1, T, 128), lambda b, h: (b, 0, COL_Z + h)), pl.BlockSpec((1, 128), lambda b, h: (0, 0))],
        out_specs=tile, out_shape=_sds((B, T, DN_W), BF16),
        compiler_params=_cp(("parallel", "parallel")), name=name,
    )(ob, proj3, gain)


def gdn_post_bwd(ob, proj3, gain, dmix3, name):
    B, T, _ = ob.shape

    def body(o_ref, z_ref, g_ref, d_ref, do_ref, dz_ref, dg_ref):
        @pl.when((pl.program_id(0) == 0) & (pl.program_id(1) == 0))
        def _():
            dg_ref[...] = jnp.zeros_like(dg_ref)

        o, z, d, g = o_ref[0], z_ref[0], d_ref[0], g_ref[...]
        r = lax.rsqrt(jnp.mean(o * o, axis=-1, keepdims=True) + EPS)
        sg = _sigmoid(z)
        dn = d * (z * sg)
        dz_ref[0] = d * (o * r * g) * (sg * (1.0 + z * (1.0 - sg)))
        dg_ref[...] += jnp.sum(dn * o * r, axis=0, keepdims=True)
        u = dn * g
        do_ref[0] = r * u - o * (r * r * r) * jnp.mean(u * o, axis=-1, keepdims=True)

    tile = pl.BlockSpec((1, T, 128), lambda b, h: (b, 0, h))
    row = pl.BlockSpec((1, 128), lambda b, h: (0, 0))
    return pl.pallas_call(
        body, grid=(B, DN_HEADS),
        in_specs=[tile, pl.BlockSpec((1, T, 128), lambda b, h: (b, 0, COL_Z + h)), row,
                  pl.BlockSpec((1, T, 128), lambda b, h: (b, 0, SB_W // 128 + h))],
        out_specs=[tile, tile, row],
        out_shape=[_sds((B, T, DN_W), F32), _sds((B, T, DN_W), F32), _sds((1, 128), F32)],
        compiler_params=_cp(("arbitrary", "arbitrary")), name=name,
    )(ob, proj3, gain, dmix3)


def _tri_inv(low, ri, ci):
    m = (ri == ci).astype(F32) - jnp.where(((ri >> 1) == (ci >> 1)) & (ri > ci), low, 0.0)
    s = 2
    while s < DN_C:
        sh = s.bit_length()
        off = ((ri >> sh) == (ci >> sh)) & ((ri & (2 * s - 1)) >= s) & ((ci & (2 * s - 1)) < s)
        m = m - _pdot(m, _pdot(jnp.where(off, low, 0.0), m, _BNN), _BNN)
        s *= 2
    return m


_BNN = (((2,), (1,)), ((0,), (0,)))
_BNT = (((2,), (2,)), ((0,), (0,)))
_BTN = (((1,), (1,)), ((0,), (0,)))
DN_G = 8


def _chunk_common(q, k, v, gt, h, tm=None):
    C = DN_C
    G = q.shape[0]
    ri = lax.broadcasted_iota(jnp.int32, (C, C), 0)
    ci = lax.broadcasted_iota(jnp.int32, (C, C), 1)
    lane = lax.broadcasted_iota(jnp.int32, (C, 128), 1)
    incl, strict = ri >= ci, ri > ci
    g = jnp.sum(jnp.where(lane == h, gt, 0.0), axis=2, keepdims=True)
    beta = jnp.sum(jnp.where(lane == h + DN_HEADS, gt, 0.0), axis=2, keepdims=True)
    ones = jnp.ones((G, C, 128), F32)
    inclf = jnp.broadcast_to(incl.astype(F32), (G, C, C))
    gc = _pdot(inclf, g * ones, _BNN)[:, :, :1]
    gcr = _pdot(jnp.ones((G, C, C), F32), jnp.where(ri == ci, gc, 0.0), _BNN)
    decay = jnp.where(incl, jnp.exp(jnp.where(incl, gc - gcr, 0.0)), 0.0)
    e = jnp.exp(gc)
    kb, vb = k * beta, v * beta
    kk = _bdot(kb, k, _BNT)
    if tm is None:
        tm = _tri_inv(jnp.where(strict, kk * decay, 0.0), ri, ci)
    kbe = kb * e
    u = _bdot(tm, vb, _BNN)
    w = _bdot(tm, kbe, _BNN)
    qk = _bdot(q, k, _BNT)
    intra = jnp.where(incl, qk * decay, 0.0)
    gl = gc[:, C - 1:C, :]
    el = jnp.exp(gl)
    r = jnp.exp(gl - gc)
    return dict(lane=lane, incl=incl, inclf=inclf, strict=strict, beta=beta, decay=decay, e=e,
                kb=kb, vb=vb, kk=kk, tm=tm, kbe=kbe, u=u, w=w, qk=qk, intra=intra, el=el, r=r, ones=ones)


def gdn_chunk_fwd(qkv, gates, name):
    B, _, T, _ = qkv.shape
    NC = T // DN_C

    G = DN_G if NC % DN_G == 0 else 1
    GC = G * DN_C

    GS = G * DN_HD

    def body(x_ref, gt_ref, o_ref, st_ref, tm_ref, s_s, p_s, b_s, qp_s, el_s):
        h = pl.program_id(1)

        def group_a(gi, _):
            rows = pl.ds(pl.multiple_of(gi * GC, GC), GC)
            srow = pl.ds(pl.multiple_of(gi * GS, GS), GS)
            q, k, v = [x_ref[0, i, rows, :].reshape(G, DN_C, DN_HD) for i in range(3)]
            c = _chunk_common(q, k, v, gt_ref[0, rows, :].reshape(G, DN_C, 128), h)
            kr = k * c["r"]
            tm_ref[0, 0, rows, :] = c["tm"].reshape(GC, DN_C)
            p_s[srow, :] = _bdot(kr, c["w"], _BTN).reshape(GS, DN_HD)
            b_s[srow, :] = _bdot(kr, c["u"], _BTN).reshape(GS, DN_HD)
            qp_s[rows, :] = (q * c["e"] - _bdot(c["intra"], c["w"], _BNN)).reshape(GC, DN_HD)
            o_ref[0, rows, :] = _bdot(c["intra"], c["u"], _BNN).reshape(GC, DN_HD)
            el_s[pl.ds(gi * G, G), :, :] = c["el"] * jnp.ones((G, 1, 128), F32)
            return 0

        lax.fori_loop(0, NC // G, group_a, 0)
        s_s[...] = jnp.zeros_like(s_s)

        def chunk(n, _):
            srow = pl.ds(pl.multiple_of(n * DN_HD, DN_HD), DN_HD)
            st = s_s[...]
            st_ref[0, 0, srow, :] = st
            s_s[...] = (st * el_s[n] + b_s[srow, :]) - _bdot(p_s[srow, :], st)
            return 0

        lax.fori_loop(0, NC, chunk, 0)

        def group_c(gi, _):
            rows = pl.ds(pl.multiple_of(gi * GC, GC), GC)
            srow = pl.ds(pl.multiple_of(gi * GS, GS), GS)
            st = st_ref[0, 0, srow, :].reshape(G, DN_HD, DN_HD)
            o_ref[0, rows, :] += _bdot(qp_s[rows, :].reshape(G, DN_C, DN_HD), st, _BNN).reshape(GC, DN_HD)
            return 0

        lax.fori_loop(0, NC // G, group_c, 0)

    return pl.pallas_call(
        body, grid=(B, DN_HEADS),
        in_specs=[pl.BlockSpec((1, 3, T, 128), lambda b, h: (b, 0, 0, h)), pl.BlockSpec((1, T, 128), lambda b, h: (b, 0, 0))],
        out_specs=[pl.BlockSpec((1, T, 128), lambda b, h: (b, 0, h)), pl.BlockSpec((1, 1, NC * DN_HD, DN_HD), lambda b, h: (b, h, 0, 0)),
                   pl.BlockSpec((1, 1, T, DN_C), lambda b, h: (b, h, 0, 0))],
        out_shape=[_sds((B, T, DN_W), F32), _sds((B, DN_HEADS, NC * DN_HD, DN_HD), F32), _sds((B, DN_HEADS, T, DN_C), F32)],
        scratch_shapes=[pltpu.VMEM((DN_HD, DN_HD), F32)] + [pltpu.VMEM((NC * DN_HD, DN_HD), F32)] * 2
        + [pltpu.VMEM((T, DN_HD), F32), pltpu.VMEM((NC, 1, 128), F32)],
        compiler_params=_cp(("parallel", "parallel")), name=name,
    )(qkv, gates)


def gdn_chunk_bwd(qkv, gates, states, tms, dob, name):
    B, _, T, _ = qkv.shape
    NC = T // DN_C
    C = DN_C

    G = DN_G if NC % DN_G == 0 else 1
    GC = G * C

    GS = G * DN_HD

    def body(x_ref, gt_ref, st_ref, tm_ref, do_ref, dx_ref, dgt_ref, ds_s, p_s, r_s, el_s, dsa_s):
        h = pl.program_id(1)

        @pl.when(h == 0)
        def _():
            dgt_ref[...] = jnp.zeros_like(dgt_ref)

        def load(gi):
            rows = pl.ds(pl.multiple_of(gi * GC, GC), GC)
            q, k, v = [x_ref[0, i, rows, :].reshape(G, C, DN_HD) for i in range(3)]
            return rows, q, k, v, gt_ref[0, rows, :].reshape(G, C, 128), tm_ref[0, 0, rows, :].reshape(G, C, C)

        def group_a(gi, _):
            rows, q, k, v, gt, tm = load(gi)
            srow = pl.ds(pl.multiple_of(gi * GS, GS), GS)
            c = _chunk_common(q, k, v, gt, h, tm=tm)
            qp = q * c["e"] - _bdot(c["intra"], c["w"], _BNN)
            p_s[srow, :] = _bdot(k * c["r"], c["w"], _BTN).reshape(GS, DN_HD)
            r_s[srow, :] = _bdot(qp, do_ref[0, rows, :].reshape(G, C, DN_HD), _BTN).reshape(GS, DN_HD)
            el_s[pl.ds(gi * G, G), :, :] = c["el"] * jnp.ones((G, 1, 128), F32)
            return 0

        lax.fori_loop(0, NC // G, group_a, 0)
        ds_s[...] = jnp.zeros_like(ds_s)

        def chunk(m, _):
            n = NC - 1 - m
            srow = pl.ds(pl.multiple_of(n * DN_HD, DN_HD), DN_HD)
            dsn = ds_s[...]
            dsa_s[srow, :] = dsn
            ds_s[...] = (dsn * el_s[n] + r_s[srow, :]) - _bdot(p_s[srow, :], dsn, _TN)
            return 0

        lax.fori_loop(0, NC, chunk, 0)

        def group_c(gi, _):
            rows, q, k, v, gt, tm = load(gi)
            c = _chunk_common(q, k, v, gt, h, tm=tm)
            incl, strict, decay, e, r, el, tm = c["incl"], c["strict"], c["decay"], c["e"], c["r"], c["el"], c["tm"]
            srow = pl.ds(pl.multiple_of(gi * GS, GS), GS)
            st = st_ref[0, 0, srow, :].reshape(G, DN_HD, DN_HD)
            dsn = dsa_s[srow, :].reshape(G, DN_HD, DN_HD)
            do = do_ref[0, rows, :].reshape(G, C, DN_HD)
            dvn = _bdot(k * r, dsn, _BNN) + _bdot(c["intra"], do, _BTN)
            v_new = c["u"] - _bdot(c["w"], st, _BNN)
            del_ = jnp.sum(jnp.sum(dsn * st, axis=2, keepdims=True), axis=1, keepdims=True)
            dkr = _bdot(v_new, dsn, _BNT)
            dqe = _bdot(do, st, _BNT)
            dintra = _bdot(do, v_new, _BNT)
            dw = -_bdot(dvn, st, _BNT)
            dqkd = jnp.where(incl, dintra, 0.0)
            dqk = dqkd * decay
            ddecay = dqkd * c["qk"]
            dq = dqe * e + _bdot(dqk, k, _BNN)
            dk = dkr * r + _bdot(dqk, q, _BTN)
            dtm = _bdot(dvn, c["vb"], _BNT) + _bdot(dw, c["kbe"], _BNT)
            dvb = _bdot(tm, dvn, _BTN)
            dkbe = _bdot(tm, dw, _BTN)
            dkb = dkbe * e
            de = jnp.sum(dqe * q, axis=2, keepdims=True) + jnp.sum(dkbe * c["kb"], axis=2, keepdims=True)
            da = -_pdot(tm, _pdot(dtm, tm, _BNT), _BTN)
            dlow = jnp.where(strict, da, 0.0)
            dkk = dlow * decay
            ddecay = ddecay + dlow * c["kk"]
            dkb = dkb + _bdot(dkk, k, _BNN)
            dk = dk + _bdot(dkk, c["kb"], _BTN) + dkb * c["beta"]
            dbeta = jnp.sum(dkb * k, axis=2, keepdims=True) + jnp.sum(dvb * v, axis=2, keepdims=True)
            dv = dvb * c["beta"]
            dd = ddecay * decay
            dgc = jnp.sum(dd, axis=2, keepdims=True) - _pdot(dd, c["ones"], _BTN)[:, :, :1]
            dr = jnp.sum(dkr * k, axis=2, keepdims=True)
            dgc = dgc + de * e - dr * r
            dgl = jnp.sum(dr * r, axis=1, keepdims=True) + del_ * el
            rowc = lax.broadcasted_iota(jnp.int32, (C, 1), 0)
            dgc = dgc + jnp.where(rowc == C - 1, dgl, 0.0)
            dg = _pdot(c["inclf"], dgc * c["ones"], _BTN)[:, :, :1]
            dx_ref[0, 0, rows, :] = dq.reshape(GC, DN_HD)
            dx_ref[0, 1, rows, :] = dk.reshape(GC, DN_HD)
            dx_ref[0, 2, rows, :] = dv.reshape(GC, DN_HD)
            lane = c["lane"]
            dgt_ref[0, rows, :] += (jnp.where(lane == h, dg, 0.0) + jnp.where(lane == h + DN_HEADS, dbeta, 0.0)).reshape(GC, 128)
            return 0

        lax.fori_loop(0, NC // G, group_c, 0)

    return pl.pallas_call(
        body, grid=(B, DN_HEADS),
        in_specs=[pl.BlockSpec((1, 3, T, 128), lambda b, h: (b, 0, 0, h)), pl.BlockSpec((1, T, 128), lambda b, h: (b, 0, 0)),
                  pl.BlockSpec((1, 1, NC * DN_HD, DN_HD), lambda b, h: (b, h, 0, 0)), pl.BlockSpec((1, 1, T, C), lambda b, h: (b, h, 0, 0)),
                  pl.BlockSpec((1, T, 128), lambda b, h: (b, 0, h))],
        out_specs=[pl.BlockSpec((1, 3, T, 128), lambda b, h: (b, 0, 0, h)), pl.BlockSpec((1, T, 128), lambda b, h: (b, 0, 0))],
        out_shape=[_sds((B, 3, T, DN_W), F32), _sds((B, T, 128), F32)],
        scratch_shapes=[pltpu.VMEM((DN_HD, DN_HD), F32)] + [pltpu.VMEM((NC * DN_HD, DN_HD), F32)] * 2
        + [pltpu.VMEM((NC, 1, 128), F32), pltpu.VMEM((NC * DN_HD, DN_HD), F32)],
        compiler_params=_cp(("parallel", "arbitrary")), name=name,
    )(qkv, gates, states, tms, dob)


def ada_fwd(c_all, w_ada, b_sl, name):
    L, D, W = w_ada.shape
    NBt = c_all.shape[0]

    def body(c_ref, w_ref, b_ref, o_ref):
        cv = c_ref[...]
        o_ref[0] = _pdot(cv * _sigmoid(cv), w_ref[0]) + b_ref[0]

    return pl.pallas_call(
        body, grid=(L,),
        in_specs=[pl.BlockSpec((NBt, D), lambda l: (0, 0)), pl.BlockSpec((1, D, W), lambda l: (l, 0, 0)), pl.BlockSpec((1, 1, W), lambda l: (l, 0, 0))],
        out_specs=pl.BlockSpec((1, NBt, W), lambda l: (l, 0, 0)),
        out_shape=_sds((L, NBt, W), F32),
        compiler_params=_cp(("parallel",)), name=name,
    )(c_all, w_ada, b_sl)


def ada_bwd(c_all, dmod_cols, name):
    L, NBt, W = dmod_cols.shape
    D = c_all.shape[1]

    def body(c_ref, d_ref, o_ref):
        cv = c_ref[...]
        o_ref[0] = _pdot(cv * _sigmoid(cv), d_ref[0], _TN)

    return pl.pallas_call(
        body, grid=(L,),
        in_specs=[pl.BlockSpec((NBt, D), lambda l: (0, 0)), pl.BlockSpec((1, NBt, W), lambda l: (l, 0, 0))],
        out_specs=pl.BlockSpec((1, D, W), lambda l: (l, 0, 0)),
        out_shape=_sds((L, D, W), F32),
        compiler_params=_cp(("parallel",)), name=name,
    )(c_all, dmod_cols)


def adamw(partials, w, m, v, name):
    P, R, C = partials.shape
    tr = _pick(R, 256)

    def body(p_ref, w_ref, m_ref, v_ref, g_ref, d_ref, nm_ref, nv_ref):
        g = p_ref[0].astype(F32)
        for i in range(1, P):
            g = g + p_ref[i].astype(F32)
        nm = ADAM_B1 * m_ref[...] + (1.0 - ADAM_B1) * g
        nv = ADAM_B2 * v_ref[...] + (1.0 - ADAM_B2) * (g * g)
        m_hat = nm / (1.0 - ADAM_B1 ** ADAM_STEP)
        v_hat = nv / (1.0 - ADAM_B2 ** ADAM_STEP)
        g_ref[...] = g
        d_ref[...] = -ADAM_LR * (m_hat / (jnp.sqrt(v_hat) + ADAM_EPS) + ADAM_WD * w_ref[...])
        nm_ref[...] = nm
        nv_ref[...] = nv

    tile = pl.BlockSpec((tr, C), lambda i: (i, 0))
    return pl.pallas_call(
        body, grid=(R // tr,),
        in_specs=[pl.BlockSpec((P, tr, C), lambda i: (0, i, 0)), tile, tile, tile],
        out_specs=[tile] * 4, out_shape=[_sds((R, C), F32)] * 4,
        compiler_params=_cp(("parallel",)), name=name,
    )(partials, w, m, v)


def _coords():
    return lax.axis_index("x"), lax.axis_index("y"), lax.axis_index("c")


def all_gather(x, name):
    any_spec = pl.BlockSpec(memory_space=pl.ANY)

    def body(x_ref, out_ref, send_sems, recv_sems, local_sem):
        x_, y_, c_ = _coords()
        me, sibling = (x_, y_, c_), (x_, y_, 1 - c_)
        chips = [(1 - x_, y_), (x_, 1 - y_), (1 - x_, 1 - y_)]

        def slot(px, py, pc):
            return out_ref.at[4 * px + 2 * py + pc]

        def copy(k, block, to, src=None):
            return pltpu.make_async_remote_copy(
                src_ref=slot(*block) if src is None else src, dst_ref=slot(*block),
                send_sem=send_sems.at[k], recv_sem=recv_sems.at[k],
                device_id=to, device_id_type=pl.DeviceIdType.MESH)

        mine = pltpu.make_async_copy(x_ref, slot(*me), local_sem)
        mine.start()
        first = [copy(0, me, sibling, src=x_ref)]
        first += [copy(1 + j, me, (*chip, c_), src=x_ref) for j, chip in enumerate(chips)]
        for cp in first:
            cp.start()
        passed = [copy(4 + j, (*chip, c_), sibling) for j, chip in enumerate(chips)]
        for j, chip in enumerate(chips):
            copy(1 + j, (*chip, c_), me).wait_recv()
            passed[j].start()
        copy(0, sibling, me).wait_recv()
        for j, chip in enumerate(chips):
            copy(4 + j, (*chip, 1 - c_), me).wait_recv()
        for cp in first + passed:
            cp.wait_send()
        mine.wait()

    return pl.pallas_call(
        body, out_shape=_sds((N_DEV,) + x.shape, x.dtype),
        in_specs=[any_spec], out_specs=any_spec,
        scratch_shapes=[pltpu.SemaphoreType.DMA((7,)), pltpu.SemaphoreType.DMA((7,)), pltpu.SemaphoreType.DMA],
        name=name,
    )(x)


def all_to_all(x, name):
    any_spec = pl.BlockSpec(memory_space=pl.ANY)

    def body(x_ref, out_ref, send_sems, recv_sems, local_sem):
        x_, y_, c_ = _coords()
        me = 4 * x_ + 2 * y_ + c_
        mine = pltpu.make_async_copy(x_ref.at[me], out_ref.at[me], local_sem)
        mine.start()
        copies = []
        for k in range(1, N_DEV):
            px = 1 - x_ if k & 4 else x_
            py = 1 - y_ if k & 2 else y_
            pc = 1 - c_ if k & 1 else c_
            peer = 4 * px + 2 * py + pc
            copies.append((pltpu.make_async_remote_copy(
                src_ref=x_ref.at[peer], dst_ref=out_ref.at[me],
                send_sem=send_sems.at[k - 1], recv_sem=recv_sems.at[k - 1],
                device_id=(px, py, pc), device_id_type=pl.DeviceIdType.MESH), peer))
        for cp, _ in copies:
            cp.start()
        for k, (cp, peer) in enumerate(copies):
            pltpu.make_async_remote_copy(
                src_ref=x_ref.at[peer], dst_ref=out_ref.at[peer],
                send_sem=send_sems.at[k], recv_sem=recv_sems.at[k],
                device_id=(x_, y_, c_), device_id_type=pl.DeviceIdType.MESH).wait_recv()
        for cp, _ in copies:
            cp.wait_send()
        mine.wait()

    return pl.pallas_call(
        body, out_shape=_sds(x.shape, x.dtype),
        in_specs=[any_spec], out_specs=any_spec,
        scratch_shapes=[pltpu.SemaphoreType.DMA((7,)), pltpu.SemaphoreType.DMA((7,)), pltpu.SemaphoreType.DMA],
        name=name,
    )(x)


def _rows128(a):
    return a.reshape(-1, 128)


def _pad_lanes(a):
    return jnp.pad(a, ((0, 0), (0, 128 - a.shape[1])))


def kernel(x, c, w_ada, b_ada, norm_mix, norm_mlp, w_in, sb_q_norm, sb_k_norm, conv_w, a_log, dt_bias, dn_out_norm, w_out, w_ff1, w_ff2, loss_target, m_w_ada, m_b_ada, m_norm_mix, m_norm_mlp, m_w_in, m_sb_q_norm, m_sb_k_norm, m_conv_w, m_a_log, m_dt_bias, m_dn_out_norm, m_w_out, m_w_ff1, m_w_ff2, v_w_ada, v_b_ada, v_norm_mix, v_norm_mlp, v_w_in, v_sb_q_norm, v_sb_k_norm, v_conv_w, v_a_log, v_dt_bias, v_dn_out_norm, v_w_out, v_w_ff1, v_w_ff2):
    B, T, D = x.shape
    L = w_ada.shape[0]
    N = B * T
    FF = w_ff1.shape[2] * N_DEV
    WA = w_ada.shape[2]
    CS = conv_w.shape[2]
    me = 4 * lax.axis_index("x") + 2 * lax.axis_index("y") + lax.axis_index("c")
    tm = _pick(T, 1024)

    wb = [w.astype(BF16) for w in (w_in, w_out, w_ff1, w_ff2)]

    def assemble(lands):
        win_g, wout_g, w1_g, w2_g = lands
        return (jnp.pad(win_g.transpose(1, 0, 2).reshape(D, IN_W), ((0, 0), (0, IN_WP - IN_W))),
                wout_g.reshape(SB_W + DN_W, D), w1_g.transpose(1, 0, 2).reshape(D, FF), w2_g.reshape(FF, D))

    weights = [assemble([all_gather(w[0], "comm_gather_" + n) for w, n in zip(wb, ("w_in", "w_out", "w_ff1", "w_ff2"))])]
    conv_full = all_gather(conv_w, "comm_gather_conv").transpose(1, 2, 0, 3).reshape(L, CONV_K, 3 * DN_W)

    c_all = all_gather(c, "comm_gather_c").reshape(N_DEV * B, D)
    b_sl = lax.dynamic_slice_in_dim(b_ada, me * WA, WA, axis=1).reshape(L, 1, WA)
    mod_sh = ada_fwd(c_all, w_ada, b_sl, "ada_fwd")
    mod_g = all_gather(mod_sh, "comm_gather_mod")
    mod = lax.dynamic_slice_in_dim(mod_g, me * B, B, axis=2).transpose(1, 2, 0, 3).reshape(L, B, 6 * D)

    def mod_part(l, i):
        return mod[l, :, i * D:(i + 1) * D].reshape(B, 1, D)

    alog_row = _pad_lanes(a_log).reshape(L, 1, 128)
    dtb_row = _pad_lanes(dt_bias).reshape(L, 1, 128)

    def gate_epi(acc, xv, g):
        return acc, xv + g[0] * acc

    def relu2(a):
        r = jnp.maximum(a, 0.0)
        return r * r

    def times_gate(a, g):
        return a * g[0]

    tile_ij = lambda i, j, k: (i, j)

    saved = []
    xc = x
    for l in range(L):
        sh_a, sc_a, g_a, sh_m, sc_m, g_m = [mod_part(l, i) for i in range(6)]
        h = ln_mod_fwd(xc, norm_mix[l:l + 1], sc_a, sh_a, "ln_mod_fwd")
        W_in_l, W_out_l, W_1_l, W_2_l = weights[l]
        proj3 = matmul(h.reshape(N, D), W_in_l, mode="nn", name="mm_proj", tm=256)[0].reshape(B, T, IN_WP)
        if l + 1 < L:
            o_a, tot, *lands = sb_attn_fwd(proj3, sb_q_norm[l:l + 1], sb_k_norm[l:l + 1], "sb_attn_fwd_gather",
                                           exch=("gather", [w[l + 1] for w in wb]))
            weights.append(assemble(lands))
        else:
            o_a, tot = sb_attn_fwd(proj3, sb_q_norm[l:l + 1], sb_k_norm[l:l + 1], "sb_attn_fwd")
        qkv = gdn_pre_fwd(proj3, conv_full[l], "gdn_pre_fwd")
        gates = gdn_gates_fwd(proj3, alog_row[l], dtb_row[l], "gdn_gates_fwd")
        ob, states, tms = gdn_chunk_fwd(qkv, gates, "gdn_chunk_fwd")
        o_b = gdn_post_fwd(ob, proj3, dn_out_norm[l:l + 1], "gdn_post_fwd")
        mix = jnp.concatenate([o_a, o_b], axis=-1)
        y1, x_mid = matmul(
            mix.reshape(N, SB_W + DN_W), W_out_l, mode="nn", name="mm_out", out_dtypes=(F32, F32), tm=tm, epi=gate_epi,
            extras=[(xc.reshape(N, D), (tm, _pick(D, 1024)), tile_ij),
                    (g_a, (1, 1, _pick(D, 1024)), lambda i, j, k: (i * tm // T, 0, j))])
        x_mid = x_mid.reshape(B, T, D)
        h2 = ln_mod_fwd(x_mid, norm_mlp[l:l + 1], sc_m, sh_m, "ln_mod_fwd")
        u = matmul(h2.reshape(N, D), W_1_l, mode="nn", name="mm_ff1")[0]
        y2, x_out = matmul(
            u, W_2_l, mode="nn", name="mm_ff2", out_dtypes=(F32, F32), tm=tm, a_fn=relu2, epi=gate_epi,
            extras=[(x_mid.reshape(N, D), (tm, _pick(D, 1024)), tile_ij),
                    (g_m, (1, 1, _pick(D, 1024)), lambda i, j, k: (i * tm // T, 0, j))])
        saved.append(dict(x=xc, h=h, proj3=proj3, tot=tot, qkv=qkv, gates=gates, states=states, tms=tms, ob=ob, mix=mix,
                          y1=y1, x_mid=x_mid, h2=h2, u=u, y2=y2))
        xc = x_out.reshape(B, T, D)

    dx, sq = loss_grad(xc, loss_target, "loss_grad")
    loss = lax.psum((0.5 / D) * jnp.sum(sq), AXES)

    g_win, g_wout, g_w1, g_w2, dmods, smalls, parts = [], [], [], [], [], [], []
    pending = None
    tk_tok = tm
    wi = w_in.shape[2]

    def shard_layer(gin, gout, g1, g2):
        return [None if gin is None else gin[:, :IN_W].reshape(D, N_DEV, wi).transpose(1, 0, 2), gout.reshape(N_DEV, w_out.shape[1], D),
                g1.reshape(D, N_DEV, w_ff1.shape[2]).transpose(1, 0, 2), g2.reshape(N_DEV, w_ff2.shape[1], D)]

    for l in reversed(range(L)):
        s = saved[l]
        W_in_l, W_out_l, W_1_l, W_2_l = weights[l]
        sh_a, sc_a, g_a, sh_m, sc_m, g_m = [mod_part(l, i) for i in range(6)]
        gate_k = lambda g, blk: (g, (1, 1, blk), lambda i, j, k: (i * tm // T, 0, k))
        gate_tok = lambda g, blk: (g, (1, 1, blk), lambda i, j, k: (k * tk_tok // T, 0, j))
        dx2 = dx.reshape(N, D)
        dg_m = rowsum_prod(dx, s["y2"].reshape(B, T, D), "rowsum_prod")
        du = matmul(dx2, W_2_l, mode="nt", name="mm_ff2_da", out_dtypes=(BF16,), tm=tm, a_fn=times_gate,
                    a_extras=[gate_k(g_m, _pick(D, 1024))],
                    epi=lambda acc, uv: (acc * (2.0 * jnp.maximum(uv, 0.0)),),
                    extras=[(s["u"], (tm, _pick(FF, 1024)), tile_ij)])[0]
        g_w2.append(matmul(s["u"], dx2, mode="tn", name="mm_ff2_dw", out_dtypes=(BF16,), tk=tk_tok, a_fn=relu2,
                           b_fn=times_gate, b_extras=[gate_tok(g_m, _pick(D, 1024))])[0])
        g_w1.append(matmul(s["h2"].reshape(N, D), du, mode="tn", name="mm_ff1_dw", out_dtypes=(BF16,))[0])
        dh2 = matmul(du, W_1_l, mode="nt", name="mm_ff1_da")[0]
        dx_mid, dgn_mlp, dsc_m, dsh_m = ln_mod_bwd(s["x_mid"], norm_mlp[l:l + 1], sc_m, dh2.reshape(B, T, D), dx, "ln_mod_bwd")
        dxm2 = dx_mid.reshape(N, D)
        dg_a = rowsum_prod(dx_mid, s["y1"].reshape(B, T, D), "rowsum_prod")
        dmix3 = matmul(dxm2, W_out_l, mode="nt", name="mm_out_da", tm=tm, a_fn=times_gate,
                       a_extras=[gate_k(g_a, _pick(D, 1024))])[0].reshape(B, T, SB_W + DN_W)
        g_wout.append(matmul(s["mix"].reshape(N, SB_W + DN_W), dxm2, mode="tn", name="mm_out_dw", out_dtypes=(BF16,),
                             tk=tk_tok, b_fn=times_gate, b_extras=[gate_tok(g_a, _pick(D, 1024))])[0])
        srcs = (pending or []) + (shard_layer(None, g_wout[-1], g_w1[-1], g_w2[-1])[1:] if l == 0 else [])
        if srcs:
            dq_a, dk_a, dv_a, dgq, dgk, *lands = sb_attn_bwd(
                s["proj3"], sb_q_norm[l:l + 1], sb_k_norm[l:l + 1], s["tot"], dmix3,
                "sb_attn_bwd_scatter" if l else "sb_attn_bwd_scatter0", exch=("scatter", srcs))
            if pending:
                parts.append(lands[:4])
            lands0 = lands[-3:]
        else:
            dq_a, dk_a, dv_a, dgq, dgk = sb_attn_bwd(s["proj3"], sb_q_norm[l:l + 1], sb_k_norm[l:l + 1], s["tot"], dmix3, "sb_attn_bwd")
        dob, dz, dgn_dn = gdn_post_bwd(s["ob"], s["proj3"], dn_out_norm[l:l + 1], dmix3, "gdn_post_bwd")
        dqkv, dgates = gdn_chunk_bwd(s["qkv"], s["gates"], s["states"], s["tms"], dob, "gdn_chunk_bwd")
        d_dnqkv, dconv_b = gdn_pre_bwd(s["proj3"], conv_full[l], dqkv, "gdn_pre_bwd")
        d_ab, dalog, ddtb = gdn_gates_bwd(s["proj3"], alog_row[l], dtb_row[l], dgates, "gdn_gates_bwd")
        dproj = [a.reshape(N, a.shape[-1]) for a in (dq_a, dk_a, dv_a, d_dnqkv, dz, d_ab)]
        g_win.append(proj_bwd_weight(s["h"].reshape(N, D), dproj, "mm_proj_dw"))
        dh = proj_bwd_input(dproj, W_in_l, "mm_proj_da")
        dx, dgn_mix, dsc_a, dsh_a = ln_mod_bwd(s["x"], norm_mix[l:l + 1], sc_a, dh.reshape(B, T, D), dx_mid, "ln_mod_bwd")
        pending = shard_layer(g_win[-1], g_wout[-1], g_w1[-1], g_w2[-1])
        dmods.append(jnp.concatenate([dsh_a, dsc_a, dg_a, dsh_m, dsc_m, dg_m], axis=-1).reshape(B, 6 * D))
        smalls.append(dict(norm_mix=dgn_mix, norm_mlp=dgn_mlp, sbq=dgq, sbk=dgk, alog=dalog, dtb=ddtb, dnorm=dgn_dn,
                           conv=jnp.sum(dconv_b, axis=0)))
    parts.append([all_to_all(pending[0], "comm_scatter_w_in")] + lands0)
    for lst in (dmods, smalls, parts):
        lst.reverse()
    grad_x = dx

    def update(parts, w, m, v, name):
        shp = w.shape
        r2 = lambda a: a.reshape(-1, shp[-1])
        outs = adamw(parts.reshape(parts.shape[0], -1, shp[-1]), r2(w), r2(m), r2(v), name)
        return [o.reshape(shp) for o in outs]

    p_win, p_wout, p_w1, p_w2 = [jnp.stack([parts[l][i] for l in range(L)], axis=1) for i in range(4)]
    r_win = update(p_win, w_in, m_w_in, v_w_in, "adamw_w_in")
    r_wout = update(p_wout, w_out, m_w_out, v_w_out, "adamw_w_out")
    r_w1 = update(p_w1, w_ff1, m_w_ff1, v_w_ff1, "adamw_w_ff1")
    r_w2 = update(p_w2, w_ff2, m_w_ff2, v_w_ff2, "adamw_w_ff2")

    dmod_g = all_gather(jnp.stack(dmods), "comm_gather_dmod")
    dmod_all = dmod_g.transpose(1, 0, 2, 3).reshape(L, N_DEV * B, 6 * D)
    g_wada = ada_bwd(c_all, lax.dynamic_slice_in_dim(dmod_all, me * WA, WA, axis=2), "ada_bwd")
    r_wada = update(g_wada[None], w_ada, m_w_ada, v_w_ada, "adamw_w_ada")
    r_bada = update(dmod_g.transpose(0, 2, 1, 3).reshape(N_DEV * B, L, 6 * D), b_ada, m_b_ada, v_b_ada, "adamw_b_ada")

    def pack(f):
        return jnp.concatenate([
            _rows128(f("norm_mix")), _rows128(f("norm_mlp")), _rows128(f("sbq")), _rows128(f("sbk")),
            f("alog"), f("dtb"), f("dnorm"), _rows128(f("conv"))], axis=0)

    names = ["norm_mix", "norm_mlp", "sbq", "sbk", "alog", "dtb", "dnorm"]
    part = pack(lambda n: jnp.concatenate([sm[n] for sm in smalls], axis=0))
    n_rep = part.shape[0] - L * CONV_K * 3 * DN_W // 128
    part_g = all_gather(part, "comm_gather_small")
    params = dict(norm_mix=(norm_mix, m_norm_mix, v_norm_mix), norm_mlp=(norm_mlp, m_norm_mlp, v_norm_mlp),
                  sbq=(sb_q_norm, m_sb_q_norm, v_sb_q_norm), sbk=(sb_k_norm, m_sb_k_norm, v_sb_k_norm),
                  alog=(a_log, m_a_log, v_a_log), dtb=(dt_bias, m_dt_bias, v_dt_bias),
                  dnorm=(dn_out_norm, m_dn_out_norm, v_dn_out_norm))

    def rows_of(n, a):
        return _pad_lanes(a) if n in ("alog", "dtb") else _rows128(a)

    packed = [jnp.concatenate([rows_of(n, params[n][i]) for n in names], axis=0) for i in range(3)]
    r_small = adamw(part_g[:, :n_rep], packed[0], packed[1], packed[2], "adamw_small")
    small_out = {}
    off = 0
    for n in names:
        w0 = params[n][0]
        nr = rows_of(n, w0).shape[0]
        vals = [o[off:off + nr] for o in r_small]
        small_out[n] = [(vv[:, :w0.shape[1]] if n in ("alog", "dtb") else vv.reshape(w0.shape)) for vv in vals]
        off += nr
    conv_parts = part_g[:, n_rep:].reshape(N_DEV, L, CONV_K, 3 * DN_W)
    r_conv = update(lax.dynamic_slice_in_dim(conv_parts, me * CS, CS, axis=3), conv_w, m_conv_w, v_conv_w, "adamw_conv")

    order = [r_wada, r_bada, small_out["norm_mix"], small_out["norm_mlp"], r_win, small_out["sbq"], small_out["sbk"],
             r_conv, small_out["alog"], small_out["dtb"], small_out["dnorm"], r_wout, r_w1, r_w2]
    outs = [loss, grad_x]
    for i in range(4):
        outs += [r[i] for r in order]
    return tuple(outs)
```

```python
import functools
import math

import jax
import jax.numpy as jnp
from jax import lax
from jax.experimental import pallas as pl
from jax.experimental.pallas import tpu as pltpu

F32 = jnp.float32
BF16 = jnp.bfloat16
EPS = 1e-6
N_DEV = 8
AXES = ("x", "y", "c")

SB_HEADS, SB_HD = 8, 64
SB_W = SB_HEADS * SB_HD
SB_BLK = 128
DN_HEADS, DN_HD = 4, 128
DN_W = DN_HEADS * DN_HD
DN_C = 64
CONV_K = 4
IN_W = 3 * SB_W + 4 * DN_W + 2 * DN_HEADS
IN_WP = 3 * SB_W + 4 * DN_W + 128
COL_DNQKV = 3 * SB_W // 128
COL_Z = COL_DNQKV + 3 * DN_W // 128
COL_AB = COL_Z + DN_W // 128

ADAM_LR, ADAM_B1, ADAM_B2, ADAM_EPS, ADAM_WD, ADAM_STEP = 0.001, 0.9, 0.999, 1e-08, 0.01, 10

VMEM_LIMIT = 56 * 1024 * 1024


def _cp(sem):
    return pltpu.CompilerParams(dimension_semantics=sem, vmem_limit_bytes=VMEM_LIMIT)


def _pick(dim, pref):
    return pref if dim % pref == 0 else dim


def _sds(shape, dtype):
    return jax.ShapeDtypeStruct(tuple(shape), dtype)


_NN = (((1,), (0,)), ((), ()))
_NT = (((1,), (1,)), ((), ()))
_TN = (((0,), (0,)), ((), ()))


def _bdot(a, b, dims=_NN):
    return lax.dot_general(a.astype(BF16), b.astype(BF16), dims, preferred_element_type=F32)


def _split(a):
    hi = a.astype(BF16)
    lo = (a - hi.astype(F32)).astype(BF16)
    return hi, lo


def _pdot(a, b, dims=_NN):
    ah, al = _split(a)
    bh, bl = _split(b)
    d = functools.partial(lax.dot_general, dimension_numbers=dims, preferred_element_type=F32)
    return d(ah, bh) + (d(ah, bl) + d(al, bh))


def _sigmoid(x):
    return 1.0 / (1.0 + jnp.exp(-x))


def _softplus(x):
    return jnp.maximum(x, 0.0) + jnp.log(1.0 + jnp.exp(-jnp.abs(x)))


def matmul(a, b, *, mode, name, out_dtypes=(F32,), a_fn=None, a_extras=(), b_fn=None, b_extras=(),
           epi=None, extras=(), tm=1024, tn=1024, tk=1024):
    if mode == "tn":
        K, M = a.shape
    else:
        M, K = a.shape
    N = b.shape[0] if mode == "nt" else b.shape[1]
    tm, tn, tk = _pick(M, tm), _pick(N, tn), _pick(K, tk)
    nk = K // tk
    dims = {"nn": _NN, "nt": _NT, "tn": _TN}[mode]
    a_spec = pl.BlockSpec((tk, tm), lambda i, j, k: (k, i)) if mode == "tn" else pl.BlockSpec((tm, tk), lambda i, j, k: (i, k))
    b_spec = pl.BlockSpec((tn, tk), lambda i, j, k: (j, k)) if mode == "nt" else pl.BlockSpec((tk, tn), lambda i, j, k: (k, j))
    na, nb, ne, no = len(a_extras), len(b_extras), len(extras), len(out_dtypes)

    def body(*refs):
        a_ref, b_ref = refs[0], refs[1]
        ax = refs[2:2 + na]
        bx = refs[2 + na:2 + na + nb]
        ex = refs[2 + na + nb:2 + na + nb + ne]
        outs = refs[2 + na + nb + ne:2 + na + nb + ne + no]
        acc_ref = refs[-1]
        k = pl.program_id(2)

        @pl.when(k == 0)
        def _():
            acc_ref[...] = jnp.zeros_like(acc_ref)

        av = a_ref[...]
        if a_fn is not None:
            av = a_fn(av, *[r[...] for r in ax])
        bv = b_ref[...]
        if b_fn is not None:
            bv = b_fn(bv, *[r[...] for r in bx])
        acc_ref[...] += lax.dot_general(av.astype(BF16), bv.astype(BF16), dims, preferred_element_type=F32)

        @pl.when(k == nk - 1)
        def _():
            res = acc_ref[...]
            res = (res,) if epi is None else epi(res, *[r[...] for r in ex])
            for o_ref, r in zip(outs, res):
                o_ref[...] = r.astype(o_ref.dtype)

    xs = list(a_extras) + list(b_extras) + list(extras)
    return pl.pallas_call(
        body,
        grid=(M // tm, N // tn, nk),
        in_specs=[a_spec, b_spec] + [pl.BlockSpec(bs, im) for _, bs, im in xs],
        out_specs=[pl.BlockSpec((tm, tn), lambda i, j, k: (i, j)) for _ in out_dtypes],
        out_shape=[_sds((M, N), dt) for dt in out_dtypes],
        scratch_shapes=[pltpu.VMEM((tm, tn), F32)],
        compiler_params=_cp(("parallel", "parallel", "arbitrary")),
        name=name,
    )(a, b, *[x for x, _, _ in xs])


def proj_bwd_input(pieces, w, name, tm=512, tn=1024):
    N, D = pieces[0].shape[0], w.shape[0]
    widths = [p.shape[1] for p in pieces]
    offs = [sum(widths[:i]) for i in range(len(widths))]
    tm, tn = _pick(N, tm), _pick(D, tn)

    def body(*refs):
        w_ref, o_ref = refs[len(pieces)], refs[len(pieces) + 1]
        acc = None
        for p_ref, off, wd in zip(refs, offs, widths):
            t = lax.dot_general(p_ref[...].astype(BF16), w_ref[:, off:off + wd], _NT, preferred_element_type=F32)
            acc = t if acc is None else acc + t
        o_ref[...] = acc

    return pl.pallas_call(
        body, grid=(N // tm, D // tn),
        in_specs=[pl.BlockSpec((tm, wd), lambda i, j: (i, 0)) for wd in widths] + [pl.BlockSpec((tn, sum(widths)), lambda i, j: (j, 0))],
        out_specs=pl.BlockSpec((tm, tn), lambda i, j: (i, j)), out_shape=_sds((N, D), F32),
        compiler_params=_cp(("parallel", "parallel")), name=name,
    )(*pieces, w)


def proj_bwd_weight(h, pieces, name, tm=512, tk=512):
    N, D = h.shape
    widths = [p.shape[1] for p in pieces]
    offs = [sum(widths[:i]) for i in range(len(widths))]
    tm, tk = _pick(D, tm), _pick(N, tk)
    nk = N // tk

    def body(*refs):
        h_ref, o_ref, acc_ref = refs[0], refs[len(pieces) + 1], refs[len(pieces) + 2]
        k = pl.program_id(1)

        @pl.when(k == 0)
        def _():
            acc_ref[...] = jnp.zeros_like(acc_ref)

        hv = h_ref[...]
        for p_ref, off, wd in zip(refs[1:], offs, widths):
            acc_ref[:, off:off + wd] += lax.dot_general(hv, p_ref[...].astype(BF16), _TN, preferred_element_type=F32)

        @pl.when(k == nk - 1)
        def _():
            o_ref[...] = acc_ref[...].astype(o_ref.dtype)

    return pl.pallas_call(
        body, grid=(D // tm, nk),
        in_specs=[pl.BlockSpec((tk, tm), lambda i, k: (k, i))] + [pl.BlockSpec((tk, wd), lambda i, k: (k, 0)) for wd in widths],
        out_specs=pl.BlockSpec((tm, sum(widths)), lambda i, k: (i, 0)), out_shape=_sds((D, sum(widths)), BF16),
        scratch_shapes=[pltpu.VMEM((tm, sum(widths)), F32)],
        compiler_params=_cp(("parallel", "arbitrary")), name=name,
    )(h, *pieces)


def ln_mod_fwd(x, gain, sc, sh, name):
    B, T, D = x.shape
    tt = _pick(T, 512)

    def body(x_ref, g_ref, sc_ref, sh_ref, h_ref):
        xv = x_ref[0]
        r = lax.rsqrt(jnp.mean(xv * xv, axis=-1, keepdims=True) + EPS)
        h = (xv * r * g_ref[...]) * (1.0 + sc_ref[0]) + sh_ref[0]
        h_ref[0] = h.astype(h_ref.dtype)

    return pl.pallas_call(
        body, grid=(B, T // tt),
        in_specs=[pl.BlockSpec((1, tt, D), lambda b, t: (b, t, 0)), pl.BlockSpec((1, D), lambda b, t: (0, 0)),
                  pl.BlockSpec((1, 1, D), lambda b, t: (b, 0, 0)), pl.BlockSpec((1, 1, D), lambda b, t: (b, 0, 0))],
        out_specs=pl.BlockSpec((1, tt, D), lambda b, t: (b, t, 0)),
        out_shape=_sds((B, T, D), BF16),
        compiler_params=_cp(("parallel", "parallel")), name=name,
    )(x, gain, sc, sh)


def ln_mod_bwd(x, gain, sc, dh, dres, name):
    B, T, D = x.shape
    tt = _pick(T, 512)

    def body(x_ref, g_ref, sc_ref, dh_ref, dres_ref, dx_ref, dg_ref, dsc_ref, dsh_ref):
        b, t = pl.program_id(0), pl.program_id(1)
        xv, dhv = x_ref[0], dh_ref[0]
        g, s = g_ref[...], sc_ref[0]
        r = lax.rsqrt(jnp.mean(xv * xv, axis=-1, keepdims=True) + EPS)
        xn = xv * r
        dxn = dhv * (g * (1.0 + s))
        dx_ref[0] = dres_ref[0] + r * (dxn - xn * jnp.mean(dxn * xn, axis=-1, keepdims=True))
        s1 = jnp.sum(dhv * xn, axis=0, keepdims=True)
        s2 = jnp.sum(dhv, axis=0, keepdims=True)

        @pl.when(t == 0)
        def _():
            dsc_ref[0] = jnp.zeros_like(s1)
            dsh_ref[0] = jnp.zeros_like(s1)

        @pl.when((t == 0) & (b == 0))
        def _():
            dg_ref[...] = jnp.zeros_like(s1)

        dsc_ref[0] += s1 * g
        dsh_ref[0] += s2
        dg_ref[...] += s1 * (1.0 + s)

    tile = pl.BlockSpec((1, tt, D), lambda b, t: (b, t, 0))
    row = pl.BlockSpec((1, D), lambda b, t: (0, 0))
    brow = pl.BlockSpec((1, 1, D), lambda b, t: (b, 0, 0))
    return pl.pallas_call(
        body, grid=(B, T // tt),
        in_specs=[tile, row, brow, tile, tile],
        out_specs=[tile, row, brow, brow],
        out_shape=[_sds((B, T, D), F32), _sds((1, D), F32), _sds((B, 1, D), F32), _sds((B, 1, D), F32)],
        compiler_params=_cp(("arbitrary", "arbitrary")), name=name,
    )(x, gain, sc, dh, dres)


def rowsum_prod(a, b, name):
    B, T, D = a.shape
    tt = _pick(T, 512)

    def body(a_ref, b_ref, o_ref):
        @pl.when(pl.program_id(1) == 0)
        def _():
            o_ref[...] = jnp.zeros_like(o_ref)

        o_ref[0] += jnp.sum(a_ref[0] * b_ref[0], axis=0, keepdims=True)

    tile = pl.BlockSpec((1, tt, D), lambda b, t: (b, t, 0))
    return pl.pallas_call(
        body, grid=(B, T // tt), in_specs=[tile, tile],
        out_specs=pl.BlockSpec((1, 1, D), lambda b, t: (b, 0, 0)),
        out_shape=_sds((B, 1, D), F32),
        compiler_params=_cp(("parallel", "arbitrary")), name=name,
    )(a, b)


def loss_grad(y, tgt, name):
    B, T, D = y.shape
    tt = _pick(T, 512)

    def body(y_ref, t_ref, dy_ref, s_ref):
        @pl.when((pl.program_id(0) == 0) & (pl.program_id(1) == 0))
        def _():
            s_ref[...] = jnp.zeros_like(s_ref)

        e = y_ref[0] - t_ref[0]
        dy_ref[0] = e * (1.0 / D)
        s_ref[...] += jnp.sum(e * e, axis=0, keepdims=True)

    tile = pl.BlockSpec((1, tt, D), lambda b, t: (b, t, 0))
    return pl.pallas_call(
        body, grid=(B, T // tt), in_specs=[tile, tile],
        out_specs=[tile, pl.BlockSpec((1, D), lambda b, t: (0, 0))],
        out_shape=[_sds((B, T, D), F32), _sds((1, D), F32)],
        compiler_params=_cp(("arbitrary", "arbitrary")), name=name,
    )(y, tgt)


def _sb_group(nb):
    return 4 if nb % 4 == 0 else (2 if nb % 2 == 0 else 1)


def _sb_qrows(t, kw):
    return 256 if (t % 256 == 0 and kw % 256 == 0) else SB_BLK


def _tri_sum(x, tri):
    return lax.dot_general(x.astype(BF16), tri, _NN, preferred_element_type=F32)


def _tri2(cond):
    return cond.astype(BF16)


def sb_attn_fwd(proj3, gq, gk, name, exch=None):
    B, T, _ = proj3.shape
    NB = T // SB_BLK
    G = _sb_group(NB)
    KW = G * SB_BLK
    QB = _sb_qrows(T, KW)
    scale = SB_HD ** -0.5

    def body(q_ref, k_ref, v_ref, gq_ref, gk_ref, o_ref, tot_ref, qn_s, kn_s, v_s):
        row_io = lax.broadcasted_iota(jnp.int32, (SB_BLK, SB_BLK), 0)
        col_io = lax.broadcasted_iota(jnp.int32, (SB_BLK, SB_BLK), 1)
        tri = _tri2(row_io > col_io)

        def prep(i, _):
            rows = pl.ds(pl.multiple_of(i * SB_BLK, SB_BLK), SB_BLK)
            for hh in range(2):
                sl = slice(hh * SB_HD, (hh + 1) * SB_HD)
                q = q_ref[0, rows, sl]
                k = k_ref[0, rows, sl]
                qn_s[hh, rows, :] = (q * lax.rsqrt(jnp.mean(q * q, -1, keepdims=True) + EPS) * (gq_ref[...] * scale)).astype(BF16)
                kn_s[hh, rows, :] = (k * lax.rsqrt(jnp.mean(k * k, -1, keepdims=True) + EPS) * gk_ref[...]).astype(BF16)
                v_s[hh, rows, :] = v_ref[0, rows, sl].astype(BF16)
            return 0

        lax.fori_loop(0, NB, prep, 0)

        diff_w = (lax.broadcasted_iota(jnp.int32, (QB, KW), 1)
                  - lax.broadcasted_iota(jnp.int32, (QB, KW), 0))

        def qblock(i, _):
            rows = pl.ds(pl.multiple_of(i * QB, QB), QB)
            qn = [qn_s[hh, rows, :] for hh in range(2)]
            nsj = ((i + 1) * QB - 1) // KW + 1

            def sblock(sj, carry, masked):
                cols = pl.ds(pl.multiple_of(sj * KW, KW), KW)
                mask = diff_w < i * QB - (nsj - 1) * KW
                zs = [lax.dot_general(qn[hh], kn_s[hh, cols, :], _NT, preferred_element_type=F32) for hh in range(2)]
                lgs, lss = [], []
                for hh in range(2):
                    sp = _softplus(zs[hh])
                    lgs.append(jnp.where(mask, -sp, 0.0) if masked else -sp)
                    lss.append(zs[hh] - sp)
                blocks = [lgs[hh][:, s * SB_BLK:(s + 1) * SB_BLK] for hh in range(2) for s in range(G)]
                ts_all = _tri_sum(jnp.concatenate(blocks, axis=0), tri)
                atts, css = [], []
                for hh in range(2):
                    cs = carry[hh][1]
                    ps = []
                    for s in reversed(range(G)):
                        n = hh * G + s
                        ts = ts_all[n * QB:(n + 1) * QB]
                        ps.append(lss[hh][:, s * SB_BLK:(s + 1) * SB_BLK] + ts + cs)
                        cs = cs + (ts[:, :1] + blocks[n][:, :1])
                    p = ps[0] if G == 1 else jnp.concatenate(ps[::-1], axis=1)
                    att = jnp.exp(p)
                    atts.append((jnp.where(mask, att, 0.0) if masked else att).astype(BF16))
                    css.append(cs)
                return tuple((carry[hh][0] + lax.dot_general(atts[hh], v_s[hh, cols, :], _NN, preferred_element_type=F32), css[hh])
                             for hh in range(2))

            init = (jnp.zeros((QB, SB_HD), F32), jnp.zeros((QB, 1), F32))
            res = sblock(nsj - 1, (init, init), True)
            res = lax.fori_loop(0, nsj - 1, lambda jj, c: sblock(nsj - 2 - jj, c, False), res)
            for hh in range(2):
                o_ref[0, rows, hh * SB_HD:(hh + 1) * SB_HD] = res[hh][0].astype(o_ref.dtype)
                tot_ref[0, hh, rows, :] = res[hh][1]
            return 0

        lax.fori_loop(0, T // QB, qblock, 0)

    blk = lambda off: pl.BlockSpec((1, T, 128), lambda b, p: (b, 0, off + p))
    grow = pl.BlockSpec((1, SB_HD), lambda b, p: (0, 0))
    return _hosted_call(
        body, grid=(B, SB_W // 128),
        in_specs=[blk(0), blk(SB_W // 128), blk(2 * SB_W // 128), grow, grow],
        out_specs=[pl.BlockSpec((1, T, 128), lambda b, p: (b, 0, p)), pl.BlockSpec((1, 2, T, 1), lambda b, p: (b, p, 0, 0))],
        out_shape=[_sds((B, T, SB_W), BF16), _sds((B, SB_HEADS, T, 1), F32)],
        scratch_shapes=[pltpu.VMEM((2, T, SB_HD), BF16)] * 3,
        name=name, args=(proj3, proj3, proj3, gq, gk), exch=exch)


def sb_attn_bwd(proj3, gq, gk, tot, dmix3, name, exch=None):
    B, T, _ = proj3.shape
    NB = T // SB_BLK
    G = _sb_group(NB)
    KW = G * SB_BLK
    QB = _sb_qrows(T, KW)
    scale = SB_HD ** -0.5

    def body(q_ref, k_ref, v_ref, gq_ref, gk_ref, tot_ref, do_ref, dq_ref, dk_ref, dv_ref, dgq_ref, dgk_ref,
             qn_s, kn_s, v_s, do_s, dqn_s, dkn_s, dv_s):
        row_io = lax.broadcasted_iota(jnp.int32, (SB_BLK, SB_BLK), 0)
        col_io = lax.broadcasted_iota(jnp.int32, (SB_BLK, SB_BLK), 1)
        tri = _tri2(row_io > col_io)
        trip = _tri2(row_io < col_io)

        @pl.when((pl.program_id(0) == 0) & (pl.program_id(1) == 0))
        def _():
            dgq_ref[...] = jnp.zeros_like(dgq_ref)
            dgk_ref[...] = jnp.zeros_like(dgk_ref)

        def prep(i, _):
            rows = pl.ds(pl.multiple_of(i * SB_BLK, SB_BLK), SB_BLK)
            for hh in range(2):
                sl = slice(hh * SB_HD, (hh + 1) * SB_HD)
                q = q_ref[0, rows, sl]
                k = k_ref[0, rows, sl]
                qn_s[hh, rows, :] = (q * lax.rsqrt(jnp.mean(q * q, -1, keepdims=True) + EPS) * (gq_ref[...] * scale)).astype(BF16)
                kn_s[hh, rows, :] = (k * lax.rsqrt(jnp.mean(k * k, -1, keepdims=True) + EPS) * gk_ref[...]).astype(BF16)
                v_s[hh, rows, :] = v_ref[0, rows, sl].astype(BF16)
                do_s[hh, rows, :] = do_ref[0, rows, sl].astype(BF16)
            return 0

        lax.fori_loop(0, NB, prep, 0)
        dkn_s[...] = jnp.zeros_like(dkn_s)
        dv_s[...] = jnp.zeros_like(dv_s)

        diff_w = (lax.broadcasted_iota(jnp.int32, (QB, KW), 1)
                  - lax.broadcasted_iota(jnp.int32, (QB, KW), 0))

        def qblock(i, _):
            rows = pl.ds(pl.multiple_of(i * QB, QB), QB)
            qn = [qn_s[hh, rows, :] for hh in range(2)]
            dov = [do_s[hh, rows, :] for hh in range(2)]
            tot = [tot_ref[0, hh, rows, :] for hh in range(2)]
            nsj = ((i + 1) * QB - 1) // KW + 1

            def sblock(sj, carry, masked):
                cols = pl.ds(pl.multiple_of(sj * KW, KW), KW)
                mask = diff_w < i * QB - (nsj - 1) * KW
                hs = range(2)
                kns = [kn_s[hh, cols, :] for hh in hs]
                zs = [lax.dot_general(qn[hh], kns[hh], _NT, preferred_element_type=F32) for hh in hs]
                datts = [lax.dot_general(dov[hh], v_s[hh, cols, :], _NT, preferred_element_type=F32) for hh in hs]
                lgs, lss = [], []
                for hh in hs:
                    sp = _softplus(zs[hh])
                    lgs.append(jnp.where(mask, -sp, 0.0) if masked else -sp)
                    lss.append(zs[hh] - sp)
                blocks = [lgs[hh][:, s * SB_BLK:(s + 1) * SB_BLK] for hh in hs for s in range(G)]
                ts_all = _tri_sum(jnp.concatenate(blocks, axis=0), tri)
                atts, dps, cums = [], [], []
                for hh in hs:
                    cum = carry[hh][1]
                    ps = []
                    for s in range(G):
                        n = hh * G + s
                        ts = ts_all[n * QB:(n + 1) * QB]
                        cum = cum + (ts[:, :1] + blocks[n][:, :1])
                        ps.append(lss[hh][:, s * SB_BLK:(s + 1) * SB_BLK] + ts + (tot[hh] - cum))
                    p = ps[0] if G == 1 else jnp.concatenate(ps, axis=1)
                    att = jnp.exp(p)
                    att = jnp.where(mask, att, 0.0) if masked else att
                    atts.append(att.astype(BF16))
                    dps.append(att * datts[hh])
                    cums.append(cum)
                dblocks = [dps[hh][:, s * SB_BLK:(s + 1) * SB_BLK] for hh in hs for s in range(G)]
                tp_all = _tri_sum(jnp.concatenate(dblocks, axis=0), trip)
                dzs, cdps = [], []
                for hh in hs:
                    cdp = carry[hh][2]
                    dls = []
                    for s in range(G):
                        n = hh * G + s
                        tp = tp_all[n * QB:(n + 1) * QB]
                        dls.append(tp + cdp)
                        cdp = cdp + (tp[:, SB_BLK - 1:] + dblocks[n][:, SB_BLK - 1:])
                    dlg = dls[0] if G == 1 else jnp.concatenate(dls, axis=1)
                    dz = dps[hh] - jnp.exp(lss[hh]) * (dps[hh] + dlg)
                    dzs.append((jnp.where(mask, dz, 0.0) if masked else dz).astype(BF16))
                    cdps.append(cdp)
                new = []
                for hh in hs:
                    dq = carry[hh][0] + lax.dot_general(dzs[hh], kns[hh], _NN, preferred_element_type=F32)
                    dkn_s[hh, cols, :] += lax.dot_general(dzs[hh], qn[hh], _TN, preferred_element_type=F32)
                    dv_s[hh, cols, :] += lax.dot_general(atts[hh], dov[hh], _TN, preferred_element_type=F32)
                    new.append((dq, cums[hh], cdps[hh]))
                return tuple(new)

            z1 = jnp.zeros((QB, 1), F32)
            init = (jnp.zeros((QB, SB_HD), F32), z1, z1)
            res = lax.fori_loop(0, nsj - 1, lambda sj, c: sblock(sj, c, False), (init, init))
            res = sblock(nsj - 1, res, True)
            for hh in range(2):
                dqn_s[hh, rows, :] = res[hh][0]
            return 0

        lax.fori_loop(0, T // QB, qblock, 0)

        def fin(i, carry):
            aq, ak = carry
            rows = pl.ds(pl.multiple_of(i * SB_BLK, SB_BLK), SB_BLK)
            for hh in range(2):
                sl = slice(hh * SB_HD, (hh + 1) * SB_HD)
                for src_ref, d_s, g_ref, out_ref, mult, which in ((q_ref, dqn_s, gq_ref, dq_ref, scale, 0), (k_ref, dkn_s, gk_ref, dk_ref, 1.0, 1)):
                    xr = src_ref[0, rows, sl]
                    r = lax.rsqrt(jnp.mean(xr * xr, -1, keepdims=True) + EPS)
                    dy = d_s[hh, rows, :] * mult
                    u = dy * g_ref[...]
                    out_ref[0, rows, sl] = r * u - xr * (r * r * r) * jnp.mean(u * xr, -1, keepdims=True)
                    part = jnp.sum(dy * xr * r, axis=0, keepdims=True)
                    if which == 0:
                        aq = aq + part
                    else:
                        ak = ak + part
                dv_ref[0, rows, sl] = dv_s[hh, rows, :]
            return aq, ak

        z64 = jnp.zeros((1, SB_HD), F32)
        aq, ak = lax.fori_loop(0, NB, fin, (z64, z64))
        dgq_ref[...] += aq
        dgk_ref[...] += ak

    blk = lambda off: pl.BlockSpec((1, T, 128), lambda b, p: (b, 0, off + p))
    grow = pl.BlockSpec((1, SB_HD), lambda b, p: (0, 0))
    return _hosted_call(
        body, grid=(B, SB_W // 128),
        in_specs=[blk(0), blk(SB_W // 128), blk(2 * SB_W // 128), grow, grow,
                  pl.BlockSpec((1, 2, T, 1), lambda b, p: (b, p, 0, 0)), blk(0)],
        out_specs=[blk(0), blk(0), blk(0), grow, grow],
        out_shape=[_sds((B, T, SB_W), F32)] * 3 + [_sds((1, SB_HD), F32)] * 2,
        scratch_shapes=[pltpu.VMEM((2, T, SB_HD), BF16)] * 4 + [pltpu.VMEM((2, T, SB_HD), F32)] * 3,
        name=name, args=(proj3, proj3, proj3, gq, gk, tot, dmix3), exch=exch)


def _exch_copies(kind, src_refs, land_refs, sems, with_arrivals=True):
    x_, y_, c_ = _coords()
    me = 4 * x_ + 2 * y_ + c_
    local, go, arrive = [], [], []
    for a, (src, land) in enumerate(zip(src_refs, land_refs)):
        send, recv, loc = sems[3 * a:3 * a + 3]
        local.append(pltpu.make_async_copy(src if kind == "gather" else src.at[me], land.at[me], loc))
        for k in range(1, N_DEV):
            px = 1 - x_ if k & 4 else x_
            py = 1 - y_ if k & 2 else y_
            pc = 1 - c_ if k & 1 else c_
            peer = 4 * px + 2 * py + pc
            out = src if kind == "gather" else src.at[peer]
            mk = functools.partial(pltpu.make_async_remote_copy, send_sem=send.at[k - 1], recv_sem=recv.at[k - 1],
                                   device_id=(px, py, pc), device_id_type=pl.DeviceIdType.MESH)
            go.append(mk(src_ref=out, dst_ref=land.at[me]))
            if with_arrivals:
                arrive.append(mk(src_ref=out, dst_ref=land.at[peer]))
    return local, go, arrive


def _hosted_call(body, *, grid, in_specs, out_specs, out_shape, scratch_shapes, name, args, exch=None):
    sem = ("arbitrary",) * len(grid)
    if exch is None:
        return pl.pallas_call(body, grid=grid, in_specs=in_specs, out_specs=out_specs, out_shape=out_shape,
                              scratch_shapes=scratch_shapes, compiler_params=_cp(sem), name=name)(*args)
    kind, srcs = exch
    ns, n_in, n_out, n_scr = len(srcs), len(in_specs), len(out_specs), len(scratch_shapes)
    lands = [_sds((N_DEV,) + s.shape if kind == "gather" else s.shape, s.dtype) for s in srcs]

    def wrapped(*refs):
        ins, src_refs = refs[:n_in], refs[n_in:n_in + ns]
        outs, land_refs = refs[n_in + ns:n_in + ns + n_out], refs[n_in + ns + n_out:n_in + 2 * ns + n_out]
        scr, sems = refs[n_in + 2 * ns + n_out:n_in + 2 * ns + n_out + n_scr], refs[n_in + 2 * ns + n_out + n_scr:]
        ids = [pl.program_id(d) for d in range(len(grid))]
        first = functools.reduce(lambda a, b: a & b, [i == 0 for i in ids])
        last = functools.reduce(lambda a, b: a & b, [i == g - 1 for i, g in zip(ids, grid)])

        @pl.when(first)
        def _():
            local, go, _ = _exch_copies(kind, src_refs, land_refs, sems, with_arrivals=False)
            for cp in local + go:
                cp.start()

        body(*ins, *outs, *scr)

        @pl.when(last)
        def _():
            local, go, arrive = _exch_copies(kind, src_refs, land_refs, sems)
            for cp in arrive:
                cp.wait_recv()
            for cp in go:
                cp.wait_send()
            for cp in local:
                cp.wait()

    any_spec = pl.BlockSpec(memory_space=pl.ANY)
    sems = [pltpu.SemaphoreType.DMA((N_DEV - 1,)), pltpu.SemaphoreType.DMA((N_DEV - 1,)), pltpu.SemaphoreType.DMA] * ns
    return pl.pallas_call(
        wrapped, grid=grid, in_specs=list(in_specs) + [any_spec] * ns, out_specs=list(out_specs) + [any_spec] * ns,
        out_shape=list(out_shape) + lands, scratch_shapes=list(scratch_shapes) + sems,
        compiler_params=_cp(sem), name=name)(*args, *srcs)


def _conv_silu(x, w, T):
    t_io = lax.broadcasted_iota(jnp.int32, x.shape, 0)
    xs = [x] + [jnp.where(t_io >= s, pltpu.roll(x, s, 0), 0.0) for s in range(1, CONV_K)]
    y = xs[0] * w[CONV_K - 1:CONV_K, :]
    for s in range(1, CONV_K):
        y = y + xs[s] * w[CONV_K - 1 - s:CONV_K - s, :]
    return y, y * _sigmoid(y), xs


def gdn_pre_fwd(proj3, conv_w, name):
    B, T, _ = proj3.shape
    qs = DN_HD ** -0.5

    def body(x_ref, w_ref, o_ref):
        kind = pl.program_id(1) // DN_HEADS
        _, s, _ = _conv_silu(x_ref[0], w_ref[...], T)
        n = lax.rsqrt(jnp.sum(s * s, axis=-1, keepdims=True) + EPS)
        c = jnp.where(kind == 0, qs, 1.0)
        o_ref[0, 0] = jnp.where(kind < 2, s * (n * c), s)

    return pl.pallas_call(
        body, grid=(B, 3 * DN_HEADS),
        in_specs=[pl.BlockSpec((1, T, 128), lambda b, j: (b, 0, COL_DNQKV + j)), pl.BlockSpec((CONV_K, 128), lambda b, j: (0, j))],
        out_specs=pl.BlockSpec((1, 1, T, 128), lambda b, j: (b, j // DN_HEADS, 0, j % DN_HEADS)),
        out_shape=_sds((B, 3, T, DN_W), F32),
        compiler_params=_cp(("parallel", "parallel")), name=name,
    )(proj3, conv_w)


def gdn_pre_bwd(proj3, conv_w, dqkv, name):
    B, T, _ = proj3.shape
    qs = DN_HD ** -0.5

    def body(x_ref, w_ref, d_ref, dx_ref, dw_ref):
        kind = pl.program_id(1) // DN_HEADS
        w = w_ref[...]
        y, s, xs = _conv_silu(x_ref[0], w, T)
        dout = d_ref[0, 0]
        n = lax.rsqrt(jnp.sum(s * s, axis=-1, keepdims=True) + EPS)
        c = jnp.where(kind == 0, qs, 1.0)
        dsn = c * (n * dout - s * (n * n * n) * jnp.sum(dout * s, axis=-1, keepdims=True))
        ds = jnp.where(kind < 2, dsn, dout)
        sg = _sigmoid(y)
        dy = ds * (sg * (1.0 + y * (1.0 - sg)))
        t_io = lax.broadcasted_iota(jnp.int32, dy.shape, 0)
        dx = dy * w[CONV_K - 1:CONV_K, :]
        dw_ref[0, CONV_K - 1:CONV_K, :] = jnp.sum(dy * xs[0], axis=0, keepdims=True)
        for sft in range(1, CONV_K):
            dx = dx + jnp.where(t_io < T - sft, pltpu.roll(dy, T - sft, 0), 0.0) * w[CONV_K - 1 - sft:CONV_K - sft, :]
            dw_ref[0, CONV_K - 1 - sft:CONV_K - sft, :] = jnp.sum(dy * xs[sft], axis=0, keepdims=True)
        dx_ref[0] = dx

    return pl.pallas_call(
        body, grid=(B, 3 * DN_HEADS),
        in_specs=[pl.BlockSpec((1, T, 128), lambda b, j: (b, 0, COL_DNQKV + j)), pl.BlockSpec((CONV_K, 128), lambda b, j: (0, j)),
                  pl.BlockSpec((1, 1, T, 128), lambda b, j: (b, j // DN_HEADS, 0, j % DN_HEADS))],
        out_specs=[pl.BlockSpec((1, T, 128), lambda b, j: (b, 0, j)), pl.BlockSpec((1, CONV_K, 128), lambda b, j: (b, 0, j))],
        out_shape=[_sds((B, T, 3 * DN_W), F32), _sds((B, CONV_K, 3 * DN_W), F32)],
        compiler_params=_cp(("parallel", "parallel")), name=name,
    )(proj3, conv_w, dqkv)


def gdn_gates_fwd(proj3, alog_row, dtb_row, name):
    B, T, _ = proj3.shape

    def body(x_ref, al_ref, dt_ref, o_ref):
        x = x_ref[0]
        lane = lax.broadcasted_iota(jnp.int32, x.shape, 1)
        g = -jnp.exp(al_ref[...]) * _softplus(x + dt_ref[...])
        o_ref[0] = jnp.where(lane < DN_HEADS, g, jnp.where(lane < 2 * DN_HEADS, _sigmoid(x), 0.0))

    row = pl.BlockSpec((1, 128), lambda b: (0, 0))
    return pl.pallas_call(
        body, grid=(B,),
        in_specs=[pl.BlockSpec((1, T, 128), lambda b: (b, 0, COL_AB)), row, row],
        out_specs=pl.BlockSpec((1, T, 128), lambda b: (b, 0, 0)),
        out_shape=_sds((B, T, 128), F32),
        compiler_params=_cp(("parallel",)), name=name,
    )(proj3, alog_row, dtb_row)


def gdn_gates_bwd(proj3, alog_row, dtb_row, dgates, name):
    B, T, _ = proj3.shape

    def body(x_ref, al_ref, dt_ref, d_ref, dx_ref, dal_ref, ddt_ref):
        @pl.when(pl.program_id(0) == 0)
        def _():
            dal_ref[...] = jnp.zeros_like(dal_ref)
            ddt_ref[...] = jnp.zeros_like(ddt_ref)

        x, d = x_ref[0], d_ref[0]
        lane = lax.broadcasted_iota(jnp.int32, x.shape, 1)
        a = x + dt_ref[...]
        na = -jnp.exp(al_ref[...])
        da = jnp.where(lane < DN_HEADS, d * na * _sigmoid(a), 0.0)
        bt = _sigmoid(x)
        dx_ref[0] = da + jnp.where((lane >= DN_HEADS) & (lane < 2 * DN_HEADS), d * bt * (1.0 - bt), 0.0)
        dal_ref[...] += jnp.sum(jnp.where(lane < DN_HEADS, d * na * _softplus(a), 0.0), axis=0, keepdims=True)
        ddt_ref[...] += jnp.sum(da, axis=0, keepdims=True)

    row = pl.BlockSpec((1, 128), lambda b: (0, 0))
    tile = pl.BlockSpec((1, T, 128), lambda b: (b, 0, 0))
    return pl.pallas_call(
        body, grid=(B,),
        in_specs=[pl.BlockSpec((1, T, 128), lambda b: (b, 0, COL_AB)), row, row, tile],
        out_specs=[tile, row, row],
        out_shape=[_sds((B, T, 128), F32), _sds((1, 128), F32), _sds((1, 128), F32)],
        compiler_params=_cp(("arbitrary",)), name=name,
    )(proj3, alog_row, dtb_row, dgates)


def gdn_post_fwd(ob, proj3, gain, name):
    B, T, _ = ob.shape

    def body(o_ref, z_ref, g_ref, out_ref):
        o, z = o_ref[0], z_ref[0]
        r = lax.rsqrt(jnp.mean(o * o, axis=-1, keepdims=True) + EPS)
        out_ref[0] = ((o * r * g_ref[...]) * (z * _sigmoid(z))).astype(out_ref.dtype)

    tile = pl.BlockSpec((1, T, 128), lambda b, h: (b, 0, h))
    return pl.pallas_call(
        body, grid=(B, DN_HEADS),
        in_specs=[tile, pl.BlockSpec((1, T, 128), lambda b, h: (b, 0, COL_Z + h)), pl.BlockSpec((1, 128), lambda b, h: (0, 0))],
        out_specs=tile, out_shape=_sds((B, T, DN_W), BF16),
        compiler_params=_cp(("parallel", "parallel")), name=name,
    )(ob, proj3, gain)


def gdn_post_bwd(ob, proj3, gain, dmix3, name):
    B, T, _ = ob.shape

    def body(o_ref, z_ref, g_ref, d_ref, do_ref, dz_ref, dg_ref):
        @pl.when((pl.program_id(0) == 0) & (pl.program_id(1) == 0))
        def _():
            dg_ref[...] = jnp.zeros_like(dg_ref)

        o, z, d, g = o_ref[0], z_ref[0], d_ref[0], g_ref[...]
        r = lax.rsqrt(jnp.mean(o * o, axis=-1, keepdims=True) + EPS)
        sg = _sigmoid(z)
        dn = d * (z * sg)
        dz_ref[0] = d * (o * r * g) * (sg * (1.0 + z * (1.0 - sg)))
        dg_ref[...] += jnp.sum(dn * o * r, axis=0, keepdims=True)
        u = dn * g
        do_ref[0] = r * u - o * (r * r * r) * jnp.mean(u * o, axis=-1, keepdims=True)

    tile = pl.BlockSpec((1, T, 128), lambda b, h: (b, 0, h))
    row = pl.BlockSpec((1, 128), lambda b, h: (0, 0))
    return pl.pallas_call(
        body, grid=(B, DN_HEADS),
        in_specs=[tile, pl.BlockSpec((1, T, 128), lambda b, h: (b, 0, COL_Z + h)), row,
                  pl.BlockSpec((1, T, 128), lambda b, h: (b, 0, SB_W // 128 + h))],
        out_specs=[tile, tile, row],
        out_shape=[_sds((B, T, DN_W), F32), _sds((B, T, DN_W), F32), _sds((1, 128), F32)],
        compiler_params=_cp(("arbitrary", "arbitrary")), name=name,
    )(ob, proj3, gain, dmix3)


def _tri_inv(low, ri, ci):
    m = (ri == ci).astype(F32) - jnp.where(((ri >> 1) == (ci >> 1)) & (ri > ci), low, 0.0)
    s = 2
    while s < DN_C:
        sh = s.bit_length()
        off = ((ri >> sh) == (ci >> sh)) & ((ri & (2 * s - 1)) >= s) & ((ci & (2 * s - 1)) < s)
        m = m - _pdot(m, _pdot(jnp.where(off, low, 0.0), m, _BNN), _BNN)
        s *= 2
    return m


_BNN = (((2,), (1,)), ((0,), (0,)))
_BNT = (((2,), (2,)), ((0,), (0,)))
_BTN = (((1,), (1,)), ((0,), (0,)))
DN_G = 16


def _chunk_common(q, k, v, gt, h, tm=None):
    C = DN_C
    G = q.shape[0]
    ri = lax.broadcasted_iota(jnp.int32, (C, C), 0)
    ci = lax.broadcasted_iota(jnp.int32, (C, C), 1)
    lane = lax.broadcasted_iota(jnp.int32, (C, 128), 1)
    incl, strict = ri >= ci, ri > ci
    g = jnp.sum(jnp.where(lane == h, gt, 0.0), axis=2, keepdims=True)
    beta = jnp.sum(jnp.where(lane == h + DN_HEADS, gt, 0.0), axis=2, keepdims=True)
    ones = jnp.ones((G, C, 128), F32)
    inclf = jnp.broadcast_to(incl.astype(F32), (G, C, C))
    gc = _pdot(inclf, g * ones, _BNN)[:, :, :1]
    gcr = _pdot(jnp.ones((G, C, C), F32), jnp.where(ri == ci, gc, 0.0), _BNN)
    decay = jnp.where(incl, jnp.exp(jnp.where(incl, gc - gcr, 0.0)), 0.0)
    e = jnp.exp(gc)
    kb, vb = k * beta, v * beta
    kk = _bdot(kb, k, _BNT)
    if tm is None:
        tm = _tri_inv(jnp.where(strict, kk * decay, 0.0), ri, ci)
    kbe = kb * e
    u = _bdot(tm, vb, _BNN)
    w = _bdot(tm, kbe, _BNN)
    qk = _bdot(q, k, _BNT)
    intra = jnp.where(incl, qk * decay, 0.0)
    gl = gc[:, C - 1:C, :]
    el = jnp.exp(gl)
    r = jnp.exp(gl - gc)
    return dict(lane=lane, incl=incl, inclf=inclf, strict=strict, beta=beta, decay=decay, e=e,
                kb=kb, vb=vb, kk=kk, tm=tm, kbe=kbe, u=u, w=w, qk=qk, intra=intra, el=el, r=r, ones=ones)


def gdn_chunk_fwd(qkv, gates, name):
    B, _, T, _ = qkv.shape
    NC = T // DN_C

    G = DN_G if NC % DN_G == 0 else 1
    GC = G * DN_C

    GS = G * DN_HD

    def body(x_ref, gt_ref, o_ref, st_ref, tm_ref, s_s, p_s, b_s, qp_s, el_s):
        h = pl.program_id(1)

        def group_a(gi, _):
            rows = pl.ds(pl.multiple_of(gi * GC, GC), GC)
            srow = pl.ds(pl.multiple_of(gi * GS, GS), GS)
            q, k, v = [x_ref[0, i, rows, :].reshape(G, DN_C, DN_HD) for i in range(3)]
            c = _chunk_common(q, k, v, gt_ref[0, rows, :].reshape(G, DN_C, 128), h)
            kr = k * c["r"]
            tm_ref[0, 0, rows, :] = c["tm"].reshape(GC, DN_C)
            p_s[srow, :] = _bdot(kr, c["w"], _BTN).reshape(GS, DN_HD)
            b_s[srow, :] = _bdot(kr, c["u"], _BTN).reshape(GS, DN_HD)
            qp_s[rows, :] = (q * c["e"] - _bdot(c["intra"], c["w"], _BNN)).reshape(GC, DN_HD)
            o_ref[0, rows, :] = _bdot(c["intra"], c["u"], _BNN).reshape(GC, DN_HD)
            el_s[pl.ds(gi * G, G), :, :] = c["el"] * jnp.ones((G, 1, 128), F32)
            return 0

        lax.fori_loop(0, NC // G, group_a, 0)
        s_s[...] = jnp.zeros_like(s_s)

        def chunk(n, _):
            srow = pl.ds(pl.multiple_of(n * DN_HD, DN_HD), DN_HD)
            st = s_s[...]
            st_ref[0, 0, srow, :] = st
            s_s[...] = (st * el_s[n] + b_s[srow, :]) - _bdot(p_s[srow, :], st)
            return 0

        lax.fori_loop(0, NC, chunk, 0)

        def group_c(gi, _):
            rows = pl.ds(pl.multiple_of(gi * GC, GC), GC)
            srow = pl.ds(pl.multiple_of(gi * GS, GS), GS)
            st = st_ref[0, 0, srow, :].reshape(G, DN_HD, DN_HD)
            o_ref[0, rows, :] += _bdot(qp_s[rows, :].reshape(G, DN_C, DN_HD), st, _BNN).reshape(GC, DN_HD)
            return 0

        lax.fori_loop(0, NC // G, group_c, 0)

    return pl.pallas_call(
        body, grid=(B, DN_HEADS),
        in_specs=[pl.BlockSpec((1, 3, T, 128), lambda b, h: (b, 0, 0, h)), pl.BlockSpec((1, T, 128), lambda b, h: (b, 0, 0))],
        out_specs=[pl.BlockSpec((1, T, 128), lambda b, h: (b, 0, h)), pl.BlockSpec((1, 1, NC * DN_HD, DN_HD), lambda b, h: (b, h, 0, 0)),
                   pl.BlockSpec((1, 1, T, DN_C), lambda b, h: (b, h, 0, 0))],
        out_shape=[_sds((B, T, DN_W), F32), _sds((B, DN_HEADS, NC * DN_HD, DN_HD), F32), _sds((B, DN_HEADS, T, DN_C), F32)],
        scratch_shapes=[pltpu.VMEM((DN_HD, DN_HD), F32)] + [pltpu.VMEM((NC * DN_HD, DN_HD), F32)] * 2
        + [pltpu.VMEM((T, DN_HD), F32), pltpu.VMEM((NC, 1, 128), F32)],
        compiler_params=_cp(("parallel", "parallel")), name=name,
    )(qkv, gates)


def gdn_chunk_bwd(qkv, gates, states, tms, dob, name):
    B, _, T, _ = qkv.shape
    NC = T // DN_C
    C = DN_C

    G = DN_G if NC % DN_G == 0 else 1
    GC = G * C

    GS = G * DN_HD

    def body(x_ref, gt_ref, st_ref, tm_ref, do_ref, dx_ref, dgt_ref, ds_s, p_s, r_s, el_s, dsa_s):
        h = pl.program_id(1)

        @pl.when(h == 0)
        def _():
            dgt_ref[...] = jnp.zeros_like(dgt_ref)

        def load(gi):
            rows = pl.ds(pl.multiple_of(gi * GC, GC), GC)
            q, k, v = [x_ref[0, i, rows, :].reshape(G, C, DN_HD) for i in range(3)]
            return rows, q, k, v, gt_ref[0, rows, :].reshape(G, C, 128), tm_ref[0, 0, rows, :].reshape(G, C, C)

        def group_a(gi, _):
            rows, q, k, v, gt, tm = load(gi)
            srow = pl.ds(pl.multiple_of(gi * GS, GS), GS)
            c = _chunk_common(q, k, v, gt, h, tm=tm)
            qp = q * c["e"] - _bdot(c["intra"], c["w"], _BNN)
            p_s[srow, :] = _bdot(k * c["r"], c["w"], _BTN).reshape(GS, DN_HD)
            r_s[srow, :] = _bdot(qp, do_ref[0, rows, :].reshape(G, C, DN_HD), _BTN).reshape(GS, DN_HD)
            el_s[pl.ds(gi * G, G), :, :] = c["el"] * jnp.ones((G, 1, 128), F32)
            return 0

        lax.fori_loop(0, NC // G, group_a, 0)
        ds_s[...] = jnp.zeros_like(ds_s)

        def chunk(m, _):
            n = NC - 1 - m
            srow = pl.ds(pl.multiple_of(n * DN_HD, DN_HD), DN_HD)
            dsn = ds_s[...]
            dsa_s[srow, :] = dsn
            ds_s[...] = (dsn * el_s[n] + r_s[srow, :]) - _bdot(p_s[srow, :], dsn, _TN)
            return 0

        lax.fori_loop(0, NC, chunk, 0)

        def group_c(gi, _):
            rows, q, k, v, gt, tm = load(gi)
            c = _chunk_common(q, k, v, gt, h, tm=tm)
            incl, strict, decay, e, r, el, tm = c["incl"], c["strict"], c["decay"], c["e"], c["r"], c["el"], c["tm"]
            srow = pl.ds(pl.multiple_of(gi * GS, GS), GS)
            st = st_ref[0, 0, srow, :].reshape(G, DN_HD, DN_HD)
            dsn = dsa_s[srow, :].reshape(G, DN_HD, DN_HD)
            do = do_ref[0, rows, :].reshape(G, C, DN_HD)
            dvn = _bdot(k * r, dsn, _BNN) + _bdot(c["intra"], do, _BTN)
            v_new = c["u"] - _bdot(c["w"], st, _BNN)
            del_ = jnp.sum(jnp.sum(dsn * st, axis=2, keepdims=True), axis=1, keepdims=True)
            dkr = _bdot(v_new, dsn, _BNT)
            dqe = _bdot(do, st, _BNT)
            dintra = _bdot(do, v_new, _BNT)
            dw = -_bdot(dvn, st, _BNT)
            dqkd = jnp.where(incl, dintra, 0.0)
            dqk = dqkd * decay
            ddecay = dqkd * c["qk"]
            dq = dqe * e + _bdot(dqk, k, _BNN)
            dk = dkr * r + _bdot(dqk, q, _BTN)
            dtm = _bdot(dvn, c["vb"], _BNT) + _bdot(dw, c["kbe"], _BNT)
            dvb = _bdot(tm, dvn, _BTN)
            dkbe = _bdot(tm, dw, _BTN)
            dkb = dkbe * e
            de = jnp.sum(dqe * q, axis=2, keepdims=True) + jnp.sum(dkbe * c["kb"], axis=2, keepdims=True)
            da = -_pdot(tm, _pdot(dtm, tm, _BNT), _BTN)
            dlow = jnp.where(strict, da, 0.0)
            dkk = dlow * decay
            ddecay = ddecay + dlow * c["kk"]
            dkb = dkb + _bdot(dkk, k, _BNN)
            dk = dk + _bdot(dkk, c["kb"], _BTN) + dkb * c["beta"]
            dbeta = jnp.sum(dkb * k, axis=2, keepdims=True) + jnp.sum(dvb * v, axis=2, keepdims=True)
            dv = dvb * c["beta"]
            dd = ddecay * decay
            dgc = jnp.sum(dd, axis=2, keepdims=True) - _pdot(dd, c["ones"], _BTN)[:, :, :1]
            dr = jnp.sum(dkr * k, axis=2, keepdims=True)
            dgc = dgc + de * e - dr * r
            dgl = jnp.sum(dr * r, axis=1, keepdims=True) + del_ * el
            rowc = lax.broadcasted_iota(jnp.int32, (C, 1), 0)
            dgc = dgc + jnp.where(rowc == C - 1, dgl, 0.0)
            dg = _pdot(c["inclf"], dgc * c["ones"], _BTN)[:, :, :1]
            dx_ref[0, 0, rows, :] = dq.reshape(GC, DN_HD)
            dx_ref[0, 1, rows, :] = dk.reshape(GC, DN_HD)
            dx_ref[0, 2, rows, :] = dv.reshape(GC, DN_HD)
            lane = c["lane"]
            dgt_ref[0, rows, :] += (jnp.where(lane == h, dg, 0.0) + jnp.where(lane == h + DN_HEADS, dbeta, 0.0)).reshape(GC, 128)
            return 0

        lax.fori_loop(0, NC // G, group_c, 0)

    return pl.pallas_call(
        body, grid=(B, DN_HEADS),
        in_specs=[pl.BlockSpec((1, 3, T, 128), lambda b, h: (b, 0, 0, h)), pl.BlockSpec((1, T, 128), lambda b, h: (b, 0, 0)),
                  pl.BlockSpec((1, 1, NC * DN_HD, DN_HD), lambda b, h: (b, h, 0, 0)), pl.BlockSpec((1, 1, T, C), lambda b, h: (b, h, 0, 0)),
                  pl.BlockSpec((1, T, 128), lambda b, h: (b, 0, h))],
        out_specs=[pl.BlockSpec((1, 3, T, 128), lambda b, h: (b, 0, 0, h)), pl.BlockSpec((1, T, 128), lambda b, h: (b, 0, 0))],
        out_shape=[_sds((B, 3, T, DN_W), F32), _sds((B, T, 128), F32)],
        scratch_shapes=[pltpu.VMEM((DN_HD, DN_HD), F32)] + [pltpu.VMEM((NC * DN_HD, DN_HD), F32)] * 2
        + [pltpu.VMEM((NC, 1, 128), F32), pltpu.VMEM((NC * DN_HD, DN_HD), F32)],
        compiler_params=_cp(("parallel", "arbitrary")), name=name,
    )(qkv, gates, states, tms, dob)


def ada_fwd(c_all, w_ada, b_sl, name):
    L, D, W = w_ada.shape
    NBt = c_all.shape[0]

    def body(c_ref, w_ref, b_ref, o_ref):
        cv = c_ref[...]
        o_ref[0] = _pdot(cv * _sigmoid(cv), w_ref[0]) + b_ref[0]

    return pl.pallas_call(
        body, grid=(L,),
        in_specs=[pl.BlockSpec((NBt, D), lambda l: (0, 0)), pl.BlockSpec((1, D, W), lambda l: (l, 0, 0)), pl.BlockSpec((1, 1, W), lambda l: (l, 0, 0))],
        out_specs=pl.BlockSpec((1, NBt, W), lambda l: (l, 0, 0)),
        out_shape=_sds((L, NBt, W), F32),
        compiler_params=_cp(("parallel",)), name=name,
    )(c_all, w_ada, b_sl)


def ada_bwd(c_all, dmod_cols, name):
    L, NBt, W = dmod_cols.shape
    D = c_all.shape[1]

    def body(c_ref, d_ref, o_ref):
        cv = c_ref[...]
        o_ref[0] = _pdot(cv * _sigmoid(cv), d_ref[0], _TN)

    return pl.pallas_call(
        body, grid=(L,),
        in_specs=[pl.BlockSpec((NBt, D), lambda l: (0, 0)), pl.BlockSpec((1, NBt, W), lambda l: (l, 0, 0))],
        out_specs=pl.BlockSpec((1, D, W), lambda l: (l, 0, 0)),
        out_shape=_sds((L, D, W), F32),
        compiler_params=_cp(("parallel",)), name=name,
    )(c_all, dmod_cols)


def adamw(partials, w, m, v, name):
    P, R, C = partials.shape
    tr = _pick(R, 256)

    def body(p_ref, w_ref, m_ref, v_ref, g_ref, d_ref, nm_ref, nv_ref):
        g = p_ref[0].astype(F32)
        for i in range(1, P):
            g = g + p_ref[i].astype(F32)
        nm = ADAM_B1 * m_ref[...] + (1.0 - ADAM_B1) * g
        nv = ADAM_B2 * v_ref[...] + (1.0 - ADAM_B2) * (g * g)
        m_hat = nm / (1.0 - ADAM_B1 ** ADAM_STEP)
        v_hat = nv / (1.0 - ADAM_B2 ** ADAM_STEP)
        g_ref[...] = g
        d_ref[...] = -ADAM_LR * (m_hat / (jnp.sqrt(v_hat) + ADAM_EPS) + ADAM_WD * w_ref[...])
        nm_ref[...] = nm
        nv_ref[...] = nv

    tile = pl.BlockSpec((tr, C), lambda i: (i, 0))
    return pl.pallas_call(
        body, grid=(R // tr,),
        in_specs=[pl.BlockSpec((P, tr, C), lambda i: (0, i, 0)), tile, tile, tile],
        out_specs=[tile] * 4, out_shape=[_sds((R, C), F32)] * 4,
        compiler_params=_cp(("parallel",)), name=name,
    )(partials, w, m, v)


def _coords():
    return lax.axis_index("x"), lax.axis_index("y"), lax.axis_index("c")


def all_gather(x, name):
    return all_gather_many([x], name)[0]


def all_gather_many(xs, name):
    any_spec = pl.BlockSpec(memory_space=pl.ANY)
    n = len(xs)

    def body(*refs):
        x_refs, out_refs = refs[:n], refs[n:2 * n]
        send_sems, recv_sems, local_sems = refs[2 * n:]
        x_, y_, c_ = _coords()
        me, sibling = (x_, y_, c_), (x_, y_, 1 - c_)
        chips = [(1 - x_, y_), (x_, 1 - y_), (1 - x_, 1 - y_)]

        def copy(a, k, block, to, own=False):
            px, py, pc = block
            rows = out_refs[a].at[4 * px + 2 * py + pc]
            return pltpu.make_async_remote_copy(
                src_ref=x_refs[a] if own else rows, dst_ref=rows,
                send_sem=send_sems.at[7 * a + k], recv_sem=recv_sems.at[7 * a + k],
                device_id=to, device_id_type=pl.DeviceIdType.MESH)

        arrays = range(n)
        mine = [pltpu.make_async_copy(x_refs[a], out_refs[a].at[4 * x_ + 2 * y_ + c_], local_sems.at[a]) for a in arrays]
        first = [copy(a, 0, me, sibling, own=True) for a in arrays]
        first += [copy(a, 1 + j, me, (*chip, c_), own=True) for a in arrays for j, chip in enumerate(chips)]
        for cp in mine + first:
            cp.start()
        passed = []
        for a in arrays:
            for j, chip in enumerate(chips):
                copy(a, 1 + j, (*chip, c_), me).wait_recv()
                passed.append(copy(a, 4 + j, (*chip, c_), sibling))
                passed[-1].start()
        for a in arrays:
            copy(a, 0, sibling, me).wait_recv()
            for j, chip in enumerate(chips):
                copy(a, 4 + j, (*chip, 1 - c_), me).wait_recv()
        for cp in first + passed:
            cp.wait_send()
        for cp in mine:
            cp.wait()

    return pl.pallas_call(
        body, out_shape=[_sds((N_DEV,) + x.shape, x.dtype) for x in xs],
        in_specs=[any_spec] * n, out_specs=[any_spec] * n,
        scratch_shapes=[pltpu.SemaphoreType.DMA((7 * n,)), pltpu.SemaphoreType.DMA((7 * n,)), pltpu.SemaphoreType.DMA((n,))],
        name=name,
    )(*xs)


def all_to_all(x, name):
    any_spec = pl.BlockSpec(memory_space=pl.ANY)

    def body(x_ref, out_ref, send_sems, recv_sems, local_sem):
        x_, y_, c_ = _coords()
        me = 4 * x_ + 2 * y_ + c_
        mine = pltpu.make_async_copy(x_ref.at[me], out_ref.at[me], local_sem)
        mine.start()
        copies = []
        for k in range(1, N_DEV):
            px = 1 - x_ if k & 4 else x_
            py = 1 - y_ if k & 2 else y_
            pc = 1 - c_ if k & 1 else c_
            peer = 4 * px + 2 * py + pc
            copies.append((pltpu.make_async_remote_copy(
                src_ref=x_ref.at[peer], dst_ref=out_ref.at[me],
                send_sem=send_sems.at[k - 1], recv_sem=recv_sems.at[k - 1],
                device_id=(px, py, pc), device_id_type=pl.DeviceIdType.MESH), peer))
        for cp, _ in copies:
            cp.start()
        for k, (cp, peer) in enumerate(copies):
            pltpu.make_async_remote_copy(
                src_ref=x_ref.at[peer], dst_ref=out_ref.at[peer],
                send_sem=send_sems.at[k], recv_sem=recv_sems.at[k],
                device_id=(x_, y_, c_), device_id_type=pl.DeviceIdType.MESH).wait_recv()
        for cp, _ in copies:
            cp.wait_send()
        mine.wait()

    return pl.pallas_call(
        body, out_shape=_sds(x.shape, x.dtype),
        in_specs=[any_spec], out_specs=any_spec,
        scratch_shapes=[pltpu.SemaphoreType.DMA((7,)), pltpu.SemaphoreType.DMA((7,)), pltpu.SemaphoreType.DMA],
        name=name,
    )(x)


def _rows128(a):
    return a.reshape(-1, 128)


def _pad_lanes(a):
    return jnp.pad(a, ((0, 0), (0, 128 - a.shape[1])))


def kernel(x, c, w_ada, b_ada, norm_mix, norm_mlp, w_in, sb_q_norm, sb_k_norm, conv_w, a_log, dt_bias, dn_out_norm, w_out, w_ff1, w_ff2, loss_target, m_w_ada, m_b_ada, m_norm_mix, m_norm_mlp, m_w_in, m_sb_q_norm, m_sb_k_norm, m_conv_w, m_a_log, m_dt_bias, m_dn_out_norm, m_w_out, m_w_ff1, m_w_ff2, v_w_ada, v_b_ada, v_norm_mix, v_norm_mlp, v_w_in, v_sb_q_norm, v_sb_k_norm, v_conv_w, v_a_log, v_dt_bias, v_dn_out_norm, v_w_out, v_w_ff1, v_w_ff2):
    B, T, D = x.shape
    L = w_ada.shape[0]
    N = B * T
    FF = w_ff1.shape[2] * N_DEV
    WA = w_ada.shape[2]
    CS = conv_w.shape[2]
    me = 4 * lax.axis_index("x") + 2 * lax.axis_index("y") + lax.axis_index("c")
    tm = _pick(T, 1024)

    wb = [w.astype(BF16) for w in (w_in, w_out, w_ff1, w_ff2)]

    def assemble(lands):
        win_g, wout_g, w1_g, w2_g = lands
        return (jnp.pad(win_g.transpose(1, 0, 2).reshape(D, IN_W), ((0, 0), (0, IN_WP - IN_W))),
                wout_g.reshape(SB_W + DN_W, D), w1_g.transpose(1, 0, 2).reshape(D, FF), w2_g.reshape(FF, D))

    *lands0, conv_g, c_g = all_gather_many([w[0] for w in wb] + [conv_w, c], "comm_gather_first")
    weights = [assemble(lands0)]
    conv_full = conv_g.transpose(1, 2, 0, 3).reshape(L, CONV_K, 3 * DN_W)

    c_all = c_g.reshape(N_DEV * B, D)
    b_sl = lax.dynamic_slice_in_dim(b_ada, me * WA, WA, axis=1).reshape(L, 1, WA)
    mod_sh = ada_fwd(c_all, w_ada, b_sl, "ada_fwd")
    mod_g = all_gather(mod_sh, "comm_gather_mod")
    mod = lax.dynamic_slice_in_dim(mod_g, me * B, B, axis=2).transpose(1, 2, 0, 3).reshape(L, B, 6 * D)

    def mod_part(l, i):
        return mod[l, :, i * D:(i + 1) * D].reshape(B, 1, D)

    alog_row = _pad_lanes(a_log).reshape(L, 1, 128)
    dtb_row = _pad_lanes(dt_bias).reshape(L, 1, 128)

    def gate_epi(acc, xv, g):
        return acc, xv + g[0] * acc

    def relu2(a):
        r = jnp.maximum(a, 0.0)
        return r * r

    def times_gate(a, g):
        return a * g[0]

    tile_ij = lambda i, j, k: (i, j)

    saved = []
    xc = x
    for l in range(L):
        sh_a, sc_a, g_a, sh_m, sc_m, g_m = [mod_part(l, i) for i in range(6)]
        h = ln_mod_fwd(xc, norm_mix[l:l + 1], sc_a, sh_a, "ln_mod_fwd")
        W_in_l, W_out_l, W_1_l, W_2_l = weights[l]
        proj3 = matmul(h.reshape(N, D), W_in_l, mode="nn", name="mm_proj", tm=256)[0].reshape(B, T, IN_WP)
        if l + 1 < L:
            o_a, tot, *lands = sb_attn_fwd(proj3, sb_q_norm[l:l + 1], sb_k_norm[l:l + 1], "sb_attn_fwd_gather",
                                           exch=("gather", [w[l + 1] for w in wb]))
            weights.append(assemble(lands))
        else:
            o_a, tot = sb_attn_fwd(proj3, sb_q_norm[l:l + 1], sb_k_norm[l:l + 1], "sb_attn_fwd")
        qkv = gdn_pre_fwd(proj3, conv_full[l], "gdn_pre_fwd")
        gates = gdn_gates_fwd(proj3, alog_row[l], dtb_row[l], "gdn_gates_fwd")
        ob, states, tms = gdn_chunk_fwd(qkv, gates, "gdn_chunk_fwd")
        o_b = gdn_post_fwd(ob, proj3, dn_out_norm[l:l + 1], "gdn_post_fwd")
        mix = jnp.concatenate([o_a, o_b], axis=-1)
        y1, x_mid = matmul(
            mix.reshape(N, SB_W + DN_W), W_out_l, mode="nn", name="mm_out", out_dtypes=(F32, F32), tm=tm, epi=gate_epi,
            extras=[(xc.reshape(N, D), (tm, _pick(D, 1024)), tile_ij),
                    (g_a, (1, 1, _pick(D, 1024)), lambda i, j, k: (i * tm // T, 0, j))])
        x_mid = x_mid.reshape(B, T, D)
        h2 = ln_mod_fwd(x_mid, norm_mlp[l:l + 1], sc_m, sh_m, "ln_mod_fwd")
        u = matmul(h2.reshape(N, D), W_1_l, mode="nn", name="mm_ff1")[0]
        y2, x_out = matmul(
            u, W_2_l, mode="nn", name="mm_ff2", out_dtypes=(F32, F32), tm=tm, a_fn=relu2, epi=gate_epi,
            extras=[(x_mid.reshape(N, D), (tm, _pick(D, 1024)), tile_ij),
                    (g_m, (1, 1, _pick(D, 1024)), lambda i, j, k: (i * tm // T, 0, j))])
        saved.append(dict(x=xc, h=h, proj3=proj3, tot=tot, qkv=qkv, gates=gates, states=states, tms=tms, ob=ob, mix=mix,
                          y1=y1, x_mid=x_mid, h2=h2, u=u, y2=y2))
        xc = x_out.reshape(B, T, D)

    dx, sq = loss_grad(xc, loss_target, "loss_grad")
    loss = lax.psum((0.5 / D) * jnp.sum(sq), AXES)

    g_win, g_wout, g_w1, g_w2, dmods, smalls, parts = [], [], [], [], [], [], []
    pending = None
    tk_tok = tm
    wi = w_in.shape[2]

    def shard_layer(gin, gout, g1, g2):
        return [None if gin is None else gin[:, :IN_W].reshape(D, N_DEV, wi).transpose(1, 0, 2), gout.reshape(N_DEV, w_out.shape[1], D),
                g1.reshape(D, N_DEV, w_ff1.shape[2]).transpose(1, 0, 2), g2.reshape(N_DEV, w_ff2.shape[1], D)]

    for l in reversed(range(L)):
        s = saved[l]
        W_in_l, W_out_l, W_1_l, W_2_l = weights[l]
        sh_a, sc_a, g_a, sh_m, sc_m, g_m = [mod_part(l, i) for i in range(6)]
        gate_k = lambda g, blk: (g, (1, 1, blk), lambda i, j, k: (i * tm // T, 0, k))
        gate_tok = lambda g, blk: (g, (1, 1, blk), lambda i, j, k: (k * tk_tok // T, 0, j))
        dx2 = dx.reshape(N, D)
        dg_m = rowsum_prod(dx, s["y2"].reshape(B, T, D), "rowsum_prod")
        du = matmul(dx2, W_2_l, mode="nt", name="mm_ff2_da", out_dtypes=(BF16,), tm=tm, a_fn=times_gate,
                    a_extras=[gate_k(g_m, _pick(D, 1024))],
                    epi=lambda acc, uv: (acc * (2.0 * jnp.maximum(uv, 0.0)),),
                    extras=[(s["u"], (tm, _pick(FF, 1024)), tile_ij)])[0]
        g_w2.append(matmul(s["u"], dx2, mode="tn", name="mm_ff2_dw", out_dtypes=(BF16,), tk=tk_tok, a_fn=relu2,
                           b_fn=times_gate, b_extras=[gate_tok(g_m, _pick(D, 1024))])[0])
        g_w1.append(matmul(s["h2"].reshape(N, D), du, mode="tn", name="mm_ff1_dw", out_dtypes=(BF16,))[0])
        dh2 = matmul(du, W_1_l, mode="nt", name="mm_ff1_da")[0]
        dx_mid, dgn_mlp, dsc_m, dsh_m = ln_mod_bwd(s["x_mid"], norm_mlp[l:l + 1], sc_m, dh2.reshape(B, T, D), dx, "ln_mod_bwd")
        dxm2 = dx_mid.reshape(N, D)
        dg_a = rowsum_prod(dx_mid, s["y1"].reshape(B, T, D), "rowsum_prod")
        dmix3 = matmul(dxm2, W_out_l, mode="nt", name="mm_out_da", tm=tm, a_fn=times_gate,
                       a_extras=[gate_k(g_a, _pick(D, 1024))])[0].reshape(B, T, SB_W + DN_W)
        g_wout.append(matmul(s["mix"].reshape(N, SB_W + DN_W), dxm2, mode="tn", name="mm_out_dw", out_dtypes=(BF16,),
                             tk=tk_tok, b_fn=times_gate, b_extras=[gate_tok(g_a, _pick(D, 1024))])[0])
        srcs = (pending or []) + (shard_layer(None, g_wout[-1], g_w1[-1], g_w2[-1])[1:] if l == 0 else [])
        if srcs:
            dq_a, dk_a, dv_a, dgq, dgk, *lands = sb_attn_bwd(
                s["proj3"], sb_q_norm[l:l + 1], sb_k_norm[l:l + 1], s["tot"], dmix3,
                "sb_attn_bwd_scatter" if l else "sb_attn_bwd_scatter0", exch=("scatter", srcs))
            if pending:
                parts.append(lands[:4])
            lands0 = lands[-3:]
        else:
            dq_a, dk_a, dv_a, dgq, dgk = sb_attn_bwd(s["proj3"], sb_q_norm[l:l + 1], sb_k_norm[l:l + 1], s["tot"], dmix3, "sb_attn_bwd")
        dob, dz, dgn_dn = gdn_post_bwd(s["ob"], s["proj3"], dn_out_norm[l:l + 1], dmix3, "gdn_post_bwd")
        dqkv, dgates = gdn_chunk_bwd(s["qkv"], s["gates"], s["states"], s["tms"], dob, "gdn_chunk_bwd")
        d_dnqkv, dconv_b = gdn_pre_bwd(s["proj3"], conv_full[l], dqkv, "gdn_pre_bwd")
        d_ab, dalog, ddtb = gdn_gates_bwd(s["proj3"], alog_row[l], dtb_row[l], dgates, "gdn_gates_bwd")
        dproj = [a.reshape(N, a.shape[-1]) for a in (dq_a, dk_a, dv_a, d_dnqkv, dz, d_ab)]
        g_win.append(proj_bwd_weight(s["h"].reshape(N, D), dproj, "mm_proj_dw"))
        dh = proj_bwd_input(dproj, W_in_l, "mm_proj_da")
        dx, dgn_mix, dsc_a, dsh_a = ln_mod_bwd(s["x"], norm_mix[l:l + 1], sc_a, dh.reshape(B, T, D), dx_mid, "ln_mod_bwd")
        pending = shard_layer(g_win[-1], g_wout[-1], g_w1[-1], g_w2[-1])
        dmods.append(jnp.concatenate([dsh_a, dsc_a, dg_a, dsh_m, dsc_m, dg_m], axis=-1).reshape(B, 6 * D))
        smalls.append(dict(norm_mix=dgn_mix, norm_mlp=dgn_mlp, sbq=dgq, sbk=dgk, alog=dalog, dtb=ddtb, dnorm=dgn_dn,
                           conv=jnp.sum(dconv_b, axis=0)))
    parts.append([all_to_all(pending[0], "comm_scatter_w_in")] + lands0)
    for lst in (dmods, smalls, parts):
        lst.reverse()
    grad_x = dx

    def update(parts, w, m, v, name):
        shp = w.shape
        r2 = lambda a: a.reshape(-1, shp[-1])
        outs = adamw(parts.reshape(parts.shape[0], -1, shp[-1]), r2(w), r2(m), r2(v), name)
        return [o.reshape(shp) for o in outs]

    p_win, p_wout, p_w1, p_w2 = [jnp.stack([parts[l][i] for l in range(L)], axis=1) for i in range(4)]
    r_win = update(p_win, w_in, m_w_in, v_w_in, "adamw_w_in")
    r_wout = update(p_wout, w_out, m_w_out, v_w_out, "adamw_w_out")
    r_w1 = update(p_w1, w_ff1, m_w_ff1, v_w_ff1, "adamw_w_ff1")
    r_w2 = update(p_w2, w_ff2, m_w_ff2, v_w_ff2, "adamw_w_ff2")

    def pack(f):
        return jnp.concatenate([
            _rows128(f("norm_mix")), _rows128(f("norm_mlp")), _rows128(f("sbq")), _rows128(f("sbk")),
            f("alog"), f("dtb"), f("dnorm"), _rows128(f("conv"))], axis=0)

    part = pack(lambda n: jnp.concatenate([sm[n] for sm in smalls], axis=0))
    dmod_g, part_g = all_gather_many([jnp.stack(dmods), part], "comm_gather_last")

    dmod_all = dmod_g.transpose(1, 0, 2, 3).reshape(L, N_DEV * B, 6 * D)
    g_wada = ada_bwd(c_all, lax.dynamic_slice_in_dim(dmod_all, me * WA, WA, axis=2), "ada_bwd")
    r_wada = update(g_wada[None], w_ada, m_w_ada, v_w_ada, "adamw_w_ada")
    r_bada = update(dmod_g.transpose(0, 2, 1, 3).reshape(N_DEV * B, L, 6 * D), b_ada, m_b_ada, v_b_ada, "adamw_b_ada")

    names = ["norm_mix", "norm_mlp", "sbq", "sbk", "alog", "dtb", "dnorm"]
    n_rep = part.shape[0] - L * CONV_K * 3 * DN_W // 128
    params = dict(norm_mix=(norm_mix, m_norm_mix, v_norm_mix), norm_mlp=(norm_mlp, m_norm_mlp, v_norm_mlp),
                  sbq=(sb_q_norm, m_sb_q_norm, v_sb_q_norm), sbk=(sb_k_norm, m_sb_k_norm, v_sb_k_norm),
                  alog=(a_log, m_a_log, v_a_log), dtb=(dt_bias, m_dt_bias, v_dt_bias),
                  dnorm=(dn_out_norm, m_dn_out_norm, v_dn_out_norm))

    def rows_of(n, a):
        return _pad_lanes(a) if n in ("alog", "dtb") else _rows128(a)

    packed = [jnp.concatenate([rows_of(n, params[n][i]) for n in names], axis=0) for i in range(3)]
    r_small = adamw(part_g[:, :n_rep], packed[0], packed[1], packed[2], "adamw_small")
    small_out = {}
    off = 0
    for n in names:
        w0 = params[n][0]
        nr = rows_of(n, w0).shape[0]
        vals = [o[off:off + nr] for o in r_small]
        small_out[n] = [(vv[:, :w0.shape[1]] if n in ("alog", "dtb") else vv.reshape(w0.shape)) for vv in vals]
        off += nr
    conv_parts = part_g[:, n_rep:].reshape(N_DEV, L, CONV_K, 3 * DN_W)
    r_conv = update(lax.dynamic_slice_in_dim(conv_parts, me * CS, CS, axis=3), conv_w, m_conv_w, v_conv_w, "adamw_conv")

    order = [r_wada, r_bada, small_out["norm_mix"], small_out["norm_mlp"], r_win, small_out["sbq"], small_out["sbk"],
             r_conv, small_out["alog"], small_out["dtb"], small_out["dnorm"], r_wout, r_w1, r_w2]
    outs = [loss, grad_x]
    for i in range(4):
        outs += [r[i] for r in order]
    return tuple(outs)
```

```python
import functools
import math

import jax
import jax.numpy as jnp
from jax import lax
from jax.experimental import pallas as pl
from jax.experimental.pallas import tpu as pltpu

F32 = jnp.float32
BF16 = jnp.bfloat16
EPS = 1e-6
N_DEV = 8
AXES = ("x", "y", "c")

SB_HEADS, SB_HD = 8, 64
SB_W = SB_HEADS * SB_HD
SB_BLK = 128
DN_HEADS, DN_HD = 4, 128
DN_W = DN_HEADS * DN_HD
DN_C = 64
CONV_K = 4
IN_W = 3 * SB_W + 4 * DN_W + 2 * DN_HEADS
IN_WP = 3 * SB_W + 4 * DN_W + 128
COL_DNQKV = 3 * SB_W // 128
COL_Z = COL_DNQKV + 3 * DN_W // 128
COL_AB = COL_Z + DN_W // 128

ADAM_LR, ADAM_B1, ADAM_B2, ADAM_EPS, ADAM_WD, ADAM_STEP = 0.001, 0.9, 0.999, 1e-08, 0.01, 10

VMEM_LIMIT = 56 * 1024 * 1024


def _cp(sem):
    return pltpu.CompilerParams(dimension_semantics=sem, vmem_limit_bytes=VMEM_LIMIT)


def _pick(dim, pref):
    return pref if dim % pref == 0 else dim


def _sds(shape, dtype):
    return jax.ShapeDtypeStruct(tuple(shape), dtype)


_NN = (((1,), (0,)), ((), ()))
_NT = (((1,), (1,)), ((), ()))
_TN = (((0,), (0,)), ((), ()))


def _bdot(a, b, dims=_NN):
    return lax.dot_general(a.astype(BF16), b.astype(BF16), dims, preferred_element_type=F32)


def _split(a):
    hi = a.astype(BF16)
    lo = (a - hi.astype(F32)).astype(BF16)
    return hi, lo


def _pdot(a, b, dims=_NN):
    ah, al = _split(a)
    bh, bl = _split(b)
    d = functools.partial(lax.dot_general, dimension_numbers=dims, preferred_element_type=F32)
    return d(ah, bh) + (d(ah, bl) + d(al, bh))


def _sigmoid(x):
    return 1.0 / (1.0 + jnp.exp(-x))


def _softplus(x):
    return jnp.maximum(x, 0.0) + jnp.log(1.0 + jnp.exp(-jnp.abs(x)))


def matmul(a, b, *, mode, name, out_dtypes=(F32,), a_fn=None, a_extras=(), b_fn=None, b_extras=(),
           epi=None, extras=(), tm=1024, tn=1024, tk=1024):
    if mode == "tn":
        K, M = a.shape
    else:
        M, K = a.shape
    N = b.shape[0] if mode == "nt" else b.shape[1]
    tm, tn, tk = _pick(M, tm), _pick(N, tn), _pick(K, tk)
    nk = K // tk
    dims = {"nn": _NN, "nt": _NT, "tn": _TN}[mode]
    a_spec = pl.BlockSpec((tk, tm), lambda i, j, k: (k, i)) if mode == "tn" else pl.BlockSpec((tm, tk), lambda i, j, k: (i, k))
    b_spec = pl.BlockSpec((tn, tk), lambda i, j, k: (j, k)) if mode == "nt" else pl.BlockSpec((tk, tn), lambda i, j, k: (k, j))
    na, nb, ne, no = len(a_extras), len(b_extras), len(extras), len(out_dtypes)

    def body(*refs):
        a_ref, b_ref = refs[0], refs[1]
        ax = refs[2:2 + na]
        bx = refs[2 + na:2 + na + nb]
        ex = refs[2 + na + nb:2 + na + nb + ne]
        outs = refs[2 + na + nb + ne:2 + na + nb + ne + no]
        acc_ref = refs[-1]
        k = pl.program_id(2)

        @pl.when(k == 0)
        def _():
            acc_ref[...] = jnp.zeros_like(acc_ref)

        av = a_ref[...]
        if a_fn is not None:
            av = a_fn(av, *[r[...] for r in ax])
        bv = b_ref[...]
        if b_fn is not None:
            bv = b_fn(bv, *[r[...] for r in bx])
        acc_ref[...] += lax.dot_general(av.astype(BF16), bv.astype(BF16), dims, preferred_element_type=F32)

        @pl.when(k == nk - 1)
        def _():
            res = acc_ref[...]
            res = (res,) if epi is None else epi(res, *[r[...] for r in ex])
            for o_ref, r in zip(outs, res):
                o_ref[...] = r.astype(o_ref.dtype)

    xs = list(a_extras) + list(b_extras) + list(extras)
    return pl.pallas_call(
        body,
        grid=(M // tm, N // tn, nk),
        in_specs=[a_spec, b_spec] + [pl.BlockSpec(bs, im) for _, bs, im in xs],
        out_specs=[pl.BlockSpec((tm, tn), lambda i, j, k: (i, j)) for _ in out_dtypes],
        out_shape=[_sds((M, N), dt) for dt in out_dtypes],
        scratch_shapes=[pltpu.VMEM((tm, tn), F32)],
        compiler_params=_cp(("parallel", "parallel", "arbitrary")),
        name=name,
    )(a, b, *[x for x, _, _ in xs])


def proj_bwd_input(pieces, w, name, tm=512, tn=1024):
    N, D = pieces[0].shape[0], w.shape[0]
    widths = [p.shape[1] for p in pieces]
    offs = [sum(widths[:i]) for i in range(len(widths))]
    tm, tn = _pick(N, tm), _pick(D, tn)

    def body(*refs):
        w_ref, o_ref = refs[len(pieces)], refs[len(pieces) + 1]
        acc = None
        for p_ref, off, wd in zip(refs, offs, widths):
            t = lax.dot_general(p_ref[...].astype(BF16), w_ref[:, off:off + wd], _NT, preferred_element_type=F32)
            acc = t if acc is None else acc + t
        o_ref[...] = acc

    return pl.pallas_call(
        body, grid=(N // tm, D // tn),
        in_specs=[pl.BlockSpec((tm, wd), lambda i, j: (i, 0)) for wd in widths] + [pl.BlockSpec((tn, sum(widths)), lambda i, j: (j, 0))],
        out_specs=pl.BlockSpec((tm, tn), lambda i, j: (i, j)), out_shape=_sds((N, D), F32),
        compiler_params=_cp(("parallel", "parallel")), name=name,
    )(*pieces, w)


def proj_bwd_weight(h, pieces, name, tm=512, tk=512):
    N, D = h.shape
    widths = [p.shape[1] for p in pieces]
    offs = [sum(widths[:i]) for i in range(len(widths))]
    tm, tk = _pick(D, tm), _pick(N, tk)
    nk = N // tk

    def body(*refs):
        h_ref, o_ref, acc_ref = refs[0], refs[len(pieces) + 1], refs[len(pieces) + 2]
        k = pl.program_id(1)

        @pl.when(k == 0)
        def _():
            acc_ref[...] = jnp.zeros_like(acc_ref)

        hv = h_ref[...]
        for p_ref, off, wd in zip(refs[1:], offs, widths):
            acc_ref[:, off:off + wd] += lax.dot_general(hv, p_ref[...].astype(BF16), _TN, preferred_element_type=F32)

        @pl.when(k == nk - 1)
        def _():
            o_ref[...] = acc_ref[...].astype(o_ref.dtype)

    return pl.pallas_call(
        body, grid=(D // tm, nk),
        in_specs=[pl.BlockSpec((tk, tm), lambda i, k: (k, i))] + [pl.BlockSpec((tk, wd), lambda i, k: (k, 0)) for wd in widths],
        out_specs=pl.BlockSpec((tm, sum(widths)), lambda i, k: (i, 0)), out_shape=_sds((D, sum(widths)), BF16),
        scratch_shapes=[pltpu.VMEM((tm, sum(widths)), F32)],
        compiler_params=_cp(("parallel", "arbitrary")), name=name,
    )(h, *pieces)


def ln_mod_fwd(x, gain, sc, sh, name):
    B, T, D = x.shape
    tt = _pick(T, 512)

    def body(x_ref, g_ref, sc_ref, sh_ref, h_ref):
        xv = x_ref[0]
        r = lax.rsqrt(jnp.mean(xv * xv, axis=-1, keepdims=True) + EPS)
        h = (xv * r * g_ref[...]) * (1.0 + sc_ref[0]) + sh_ref[0]
        h_ref[0] = h.astype(h_ref.dtype)

    return pl.pallas_call(
        body, grid=(B, T // tt),
        in_specs=[pl.BlockSpec((1, tt, D), lambda b, t: (b, t, 0)), pl.BlockSpec((1, D), lambda b, t: (0, 0)),
                  pl.BlockSpec((1, 1, D), lambda b, t: (b, 0, 0)), pl.BlockSpec((1, 1, D), lambda b, t: (b, 0, 0))],
        out_specs=pl.BlockSpec((1, tt, D), lambda b, t: (b, t, 0)),
        out_shape=_sds((B, T, D), BF16),
        compiler_params=_cp(("parallel", "parallel")), name=name,
    )(x, gain, sc, sh)


def ln_mod_bwd(x, gain, sc, dh, dres, name):
    B, T, D = x.shape
    tt = _pick(T, 512)

    def body(x_ref, g_ref, sc_ref, dh_ref, dres_ref, dx_ref, dg_ref, dsc_ref, dsh_ref):
        b, t = pl.program_id(0), pl.program_id(1)
        xv, dhv = x_ref[0], dh_ref[0]
        g, s = g_ref[...], sc_ref[0]
        r = lax.rsqrt(jnp.mean(xv * xv, axis=-1, keepdims=True) + EPS)
        xn = xv * r
        dxn = dhv * (g * (1.0 + s))
        dx_ref[0] = dres_ref[0] + r * (dxn - xn * jnp.mean(dxn * xn, axis=-1, keepdims=True))
        s1 = jnp.sum(dhv * xn, axis=0, keepdims=True)
        s2 = jnp.sum(dhv, axis=0, keepdims=True)

        @pl.when(t == 0)
        def _():
            dsc_ref[0] = jnp.zeros_like(s1)
            dsh_ref[0] = jnp.zeros_like(s1)

        @pl.when((t == 0) & (b == 0))
        def _():
            dg_ref[...] = jnp.zeros_like(s1)

        dsc_ref[0] += s1 * g
        dsh_ref[0] += s2
        dg_ref[...] += s1 * (1.0 + s)

    tile = pl.BlockSpec((1, tt, D), lambda b, t: (b, t, 0))
    row = pl.BlockSpec((1, D), lambda b, t: (0, 0))
    brow = pl.BlockSpec((1, 1, D), lambda b, t: (b, 0, 0))
    return pl.pallas_call(
        body, grid=(B, T // tt),
        in_specs=[tile, row, brow, tile, tile],
        out_specs=[tile, row, brow, brow],
        out_shape=[_sds((B, T, D), F32), _sds((1, D), F32), _sds((B, 1, D), F32), _sds((B, 1, D), F32)],
        compiler_params=_cp(("arbitrary", "arbitrary")), name=name,
    )(x, gain, sc, dh, dres)


def rowsum_prod(a, b, name):
    B, T, D = a.shape
    tt = _pick(T, 512)

    def body(a_ref, b_ref, o_ref):
        @pl.when(pl.program_id(1) == 0)
        def _():
            o_ref[...] = jnp.zeros_like(o_ref)

        o_ref[0] += jnp.sum(a_ref[0] * b_ref[0], axis=0, keepdims=True)

    tile = pl.BlockSpec((1, tt, D), lambda b, t: (b, t, 0))
    return pl.pallas_call(
        body, grid=(B, T // tt), in_specs=[tile, tile],
        out_specs=pl.BlockSpec((1, 1, D), lambda b, t: (b, 0, 0)),
        out_shape=_sds((B, 1, D), F32),
        compiler_params=_cp(("parallel", "arbitrary")), name=name,
    )(a, b)


def loss_grad(y, tgt, name):
    B, T, D = y.shape
    tt = _pick(T, 512)

    def body(y_ref, t_ref, dy_ref, s_ref):
        @pl.when((pl.program_id(0) == 0) & (pl.program_id(1) == 0))
        def _():
            s_ref[...] = jnp.zeros_like(s_ref)

        e = y_ref[0] - t_ref[0]
        dy_ref[0] = e * (1.0 / D)
        s_ref[...] += jnp.sum(e * e, axis=0, keepdims=True)

    tile = pl.BlockSpec((1, tt, D), lambda b, t: (b, t, 0))
    return pl.pallas_call(
        body, grid=(B, T // tt), in_specs=[tile, tile],
        out_specs=[tile, pl.BlockSpec((1, D), lambda b, t: (0, 0))],
        out_shape=[_sds((B, T, D), F32), _sds((1, D), F32)],
        compiler_params=_cp(("arbitrary", "arbitrary")), name=name,
    )(y, tgt)


def _sb_group(nb):
    return 4 if nb % 4 == 0 else (2 if nb % 2 == 0 else 1)


def _sb_qrows(t, kw):
    return 256 if (t % 256 == 0 and kw % 256 == 0) else SB_BLK


def _tri_sum(x, tri):
    return lax.dot_general(x.astype(BF16), tri, _NN, preferred_element_type=F32)


def _tri2(cond):
    return cond.astype(BF16)


def sb_attn_fwd(proj3, gq, gk, name, exch=None):
    B, T, _ = proj3.shape
    NB = T // SB_BLK
    G = _sb_group(NB)
    KW = G * SB_BLK
    QB = _sb_qrows(T, KW)
    scale = SB_HD ** -0.5

    def body(q_ref, k_ref, v_ref, gq_ref, gk_ref, o_ref, tot_ref, qn_s, kn_s, v_s):
        row_io = lax.broadcasted_iota(jnp.int32, (SB_BLK, SB_BLK), 0)
        col_io = lax.broadcasted_iota(jnp.int32, (SB_BLK, SB_BLK), 1)
        tri = _tri2(row_io > col_io)

        def prep(i, _):
            rows = pl.ds(pl.multiple_of(i * SB_BLK, SB_BLK), SB_BLK)
            for hh in range(2):
                sl = slice(hh * SB_HD, (hh + 1) * SB_HD)
                q = q_ref[0, rows, sl]
                k = k_ref[0, rows, sl]
                qn_s[hh, rows, :] = (q * lax.rsqrt(jnp.mean(q * q, -1, keepdims=True) + EPS) * (gq_ref[...] * scale)).astype(BF16)
                kn_s[hh, rows, :] = (k * lax.rsqrt(jnp.mean(k * k, -1, keepdims=True) + EPS) * gk_ref[...]).astype(BF16)
                v_s[hh, rows, :] = v_ref[0, rows, sl].astype(BF16)
            return 0

        lax.fori_loop(0, NB, prep, 0)

        diff_w = (lax.broadcasted_iota(jnp.int32, (QB, KW), 1)
                  - lax.broadcasted_iota(jnp.int32, (QB, KW), 0))

        def qblock(i, _):
            rows = pl.ds(pl.multiple_of(i * QB, QB), QB)
            qn = [qn_s[hh, rows, :] for hh in range(2)]
            nsj = ((i + 1) * QB - 1) // KW + 1

            def sblock(sj, carry, masked):
                cols = pl.ds(pl.multiple_of(sj * KW, KW), KW)
                mask = diff_w < i * QB - (nsj - 1) * KW
                zs = [lax.dot_general(qn[hh], kn_s[hh, cols, :], _NT, preferred_element_type=F32) for hh in range(2)]
                lgs, lss = [], []
                for hh in range(2):
                    sp = _softplus(zs[hh])
                    lgs.append(jnp.where(mask, -sp, 0.0) if masked else -sp)
                    lss.append(zs[hh] - sp)
                blocks = [lgs[hh][:, s * SB_BLK:(s + 1) * SB_BLK] for hh in range(2) for s in range(G)]
                ts_all = _tri_sum(jnp.concatenate(blocks, axis=0), tri)
                atts, css = [], []
                for hh in range(2):
                    cs = carry[hh][1]
                    ps = []
                    for s in reversed(range(G)):
                        n = hh * G + s
                        ts = ts_all[n * QB:(n + 1) * QB]
                        ps.append(lss[hh][:, s * SB_BLK:(s + 1) * SB_BLK] + ts + cs)
                        cs = cs + (ts[:, :1] + blocks[n][:, :1])
                    p = ps[0] if G == 1 else jnp.concatenate(ps[::-1], axis=1)
                    att = jnp.exp(p)
                    atts.append((jnp.where(mask, att, 0.0) if masked else att).astype(BF16))
                    css.append(cs)
                return tuple((carry[hh][0] + lax.dot_general(atts[hh], v_s[hh, cols, :], _NN, preferred_element_type=F32), css[hh])
                             for hh in range(2))

            init = (jnp.zeros((QB, SB_HD), F32), jnp.zeros((QB, 1), F32))
            res = sblock(nsj - 1, (init, init), True)
            res = lax.fori_loop(0, nsj - 1, lambda jj, c: sblock(nsj - 2 - jj, c, False), res)
            for hh in range(2):
                o_ref[0, rows, hh * SB_HD:(hh + 1) * SB_HD] = res[hh][0].astype(o_ref.dtype)
                tot_ref[0, hh, rows, :] = res[hh][1]
            return 0

        lax.fori_loop(0, T // QB, qblock, 0)

    blk = lambda off: pl.BlockSpec((1, T, 128), lambda b, p: (b, 0, off + p))
    grow = pl.BlockSpec((1, SB_HD), lambda b, p: (0, 0))
    return _hosted_call(
        body, grid=(B, SB_W // 128),
        in_specs=[blk(0), blk(SB_W // 128), blk(2 * SB_W // 128), grow, grow],
        out_specs=[pl.BlockSpec((1, T, 128), lambda b, p: (b, 0, p)), pl.BlockSpec((1, 2, T, 1), lambda b, p: (b, p, 0, 0))],
        out_shape=[_sds((B, T, SB_W), BF16), _sds((B, SB_HEADS, T, 1), F32)],
        scratch_shapes=[pltpu.VMEM((2, T, SB_HD), BF16)] * 3,
        name=name, args=(proj3, proj3, proj3, gq, gk), exch=exch)


def sb_attn_bwd(proj3, gq, gk, tot, dmix3, name, exch=None):
    B, T, _ = proj3.shape
    NB = T // SB_BLK
    G = _sb_group(NB)
    KW = G * SB_BLK
    QB = _sb_qrows(T, KW)
    scale = SB_HD ** -0.5

    def body(q_ref, k_ref, v_ref, gq_ref, gk_ref, tot_ref, do_ref, dq_ref, dk_ref, dv_ref, dgq_ref, dgk_ref,
             qn_s, kn_s, v_s, do_s, dqn_s, dkn_s, dv_s):
        row_io = lax.broadcasted_iota(jnp.int32, (SB_BLK, SB_BLK), 0)
        col_io = lax.broadcasted_iota(jnp.int32, (SB_BLK, SB_BLK), 1)
        tri = _tri2(row_io > col_io)
        trip = _tri2(row_io < col_io)

        @pl.when((pl.program_id(0) == 0) & (pl.program_id(1) == 0))
        def _():
            dgq_ref[...] = jnp.zeros_like(dgq_ref)
            dgk_ref[...] = jnp.zeros_like(dgk_ref)

        def prep(i, _):
            rows = pl.ds(pl.multiple_of(i * SB_BLK, SB_BLK), SB_BLK)
            for hh in range(2):
                sl = slice(hh * SB_HD, (hh + 1) * SB_HD)
                q = q_ref[0, rows, sl]
                k = k_ref[0, rows, sl]
                qn_s[hh, rows, :] = (q * lax.rsqrt(jnp.mean(q * q, -1, keepdims=True) + EPS) * (gq_ref[...] * scale)).astype(BF16)
                kn_s[hh, rows, :] = (k * lax.rsqrt(jnp.mean(k * k, -1, keepdims=True) + EPS) * gk_ref[...]).astype(BF16)
                v_s[hh, rows, :] = v_ref[0, rows, sl].astype(BF16)
                do_s[hh, rows, :] = do_ref[0, rows, sl].astype(BF16)
            return 0

        lax.fori_loop(0, NB, prep, 0)
        dkn_s[...] = jnp.zeros_like(dkn_s)
        dv_s[...] = jnp.zeros_like(dv_s)

        diff_w = (lax.broadcasted_iota(jnp.int32, (QB, KW), 1)
                  - lax.broadcasted_iota(jnp.int32, (QB, KW), 0))

        def qblock(i, _):
            rows = pl.ds(pl.multiple_of(i * QB, QB), QB)
            qn = [qn_s[hh, rows, :] for hh in range(2)]
            dov = [do_s[hh, rows, :] for hh in range(2)]
            tot = [tot_ref[0, hh, rows, :] for hh in range(2)]
            nsj = ((i + 1) * QB - 1) // KW + 1
            qnT = [qn[hh].astype(F32).T.astype(BF16) for hh in range(2)]
            doT = [dov[hh].astype(F32).T.astype(BF16) for hh in range(2)]

            def sblock(sj, carry, masked):
                cols = pl.ds(pl.multiple_of(sj * KW, KW), KW)
                mask = diff_w < i * QB - (nsj - 1) * KW
                hs = range(2)
                kns = [kn_s[hh, cols, :] for hh in hs]
                zs = [lax.dot_general(qn[hh], kns[hh], _NT, preferred_element_type=F32) for hh in hs]
                datts = [lax.dot_general(dov[hh], v_s[hh, cols, :], _NT, preferred_element_type=F32) for hh in hs]
                lgs, lss = [], []
                for hh in hs:
                    sp = _softplus(zs[hh])
                    lgs.append(jnp.where(mask, -sp, 0.0) if masked else -sp)
                    lss.append(zs[hh] - sp)
                blocks = [lgs[hh][:, s * SB_BLK:(s + 1) * SB_BLK] for hh in hs for s in range(G)]
                ts_all = _tri_sum(jnp.concatenate(blocks, axis=0), tri)
                atts, dps, cums = [], [], []
                for hh in hs:
                    cum = carry[hh][1]
                    ps = []
                    for s in range(G):
                        n = hh * G + s
                        ts = ts_all[n * QB:(n + 1) * QB]
                        cum = cum + (ts[:, :1] + blocks[n][:, :1])
                        ps.append(lss[hh][:, s * SB_BLK:(s + 1) * SB_BLK] + ts + (tot[hh] - cum))
                    p = ps[0] if G == 1 else jnp.concatenate(ps, axis=1)
                    att = jnp.exp(p)
                    att = jnp.where(mask, att, 0.0) if masked else att
                    atts.append(att.astype(BF16))
                    dps.append(att * datts[hh])
                    cums.append(cum)
                dblocks = [dps[hh][:, s * SB_BLK:(s + 1) * SB_BLK] for hh in hs for s in range(G)]
                tp_all = _tri_sum(jnp.concatenate(dblocks, axis=0), trip)
                dzs, cdps = [], []
                for hh in hs:
                    cdp = carry[hh][2]
                    dls = []
                    for s in range(G):
                        n = hh * G + s
                        tp = tp_all[n * QB:(n + 1) * QB]
                        dls.append(tp + cdp)
                        cdp = cdp + (tp[:, SB_BLK - 1:] + dblocks[n][:, SB_BLK - 1:])
                    dlg = dls[0] if G == 1 else jnp.concatenate(dls, axis=1)
                    dz = dps[hh] - jnp.exp(lss[hh]) * (dps[hh] + dlg)
                    dzs.append((jnp.where(mask, dz, 0.0) if masked else dz).astype(BF16))
                    cdps.append(cdp)
                new = []
                for hh in hs:
                    dq = carry[hh][0] + lax.dot_general(dzs[hh], kns[hh], _NN, preferred_element_type=F32)
                    dkn_s[hh, :, cols] += lax.dot_general(qnT[hh], dzs[hh], _NN, preferred_element_type=F32)
                    dv_s[hh, :, cols] += lax.dot_general(doT[hh], atts[hh], _NN, preferred_element_type=F32)
                    new.append((dq, cums[hh], cdps[hh]))
                return tuple(new)

            z1 = jnp.zeros((QB, 1), F32)
            init = (jnp.zeros((QB, SB_HD), F32), z1, z1)
            res = lax.fori_loop(0, nsj - 1, lambda sj, c: sblock(sj, c, False), (init, init))
            res = sblock(nsj - 1, res, True)
            for hh in range(2):
                dqn_s[hh, rows, :] = res[hh][0]
            return 0

        lax.fori_loop(0, T // QB, qblock, 0)

        def fin(i, carry):
            aq, ak = carry
            rows = pl.ds(pl.multiple_of(i * SB_BLK, SB_BLK), SB_BLK)
            for hh in range(2):
                sl = slice(hh * SB_HD, (hh + 1) * SB_HD)
                for src_ref, g_ref, out_ref, mult, which in ((q_ref, gq_ref, dq_ref, scale, 0), (k_ref, gk_ref, dk_ref, 1.0, 1)):
                    xr = src_ref[0, rows, sl]
                    r = lax.rsqrt(jnp.mean(xr * xr, -1, keepdims=True) + EPS)
                    dy = (dqn_s[hh, rows, :] if which == 0 else dkn_s[hh, :, rows].T) * mult
                    u = dy * g_ref[...]
                    out_ref[0, rows, sl] = r * u - xr * (r * r * r) * jnp.mean(u * xr, -1, keepdims=True)
                    part = jnp.sum(dy * xr * r, axis=0, keepdims=True)
                    if which == 0:
                        aq = aq + part
                    else:
                        ak = ak + part
                dv_ref[0, rows, sl] = dv_s[hh, :, rows].T
            return aq, ak

        z64 = jnp.zeros((1, SB_HD), F32)
        aq, ak = lax.fori_loop(0, NB, fin, (z64, z64))
        dgq_ref[...] += aq
        dgk_ref[...] += ak

    blk = lambda off: pl.BlockSpec((1, T, 128), lambda b, p: (b, 0, off + p))
    grow = pl.BlockSpec((1, SB_HD), lambda b, p: (0, 0))
    return _hosted_call(
        body, grid=(B, SB_W // 128),
        in_specs=[blk(0), blk(SB_W // 128), blk(2 * SB_W // 128), grow, grow,
                  pl.BlockSpec((1, 2, T, 1), lambda b, p: (b, p, 0, 0)), blk(0)],
        out_specs=[blk(0), blk(0), blk(0), grow, grow],
        out_shape=[_sds((B, T, SB_W), F32)] * 3 + [_sds((1, SB_HD), F32)] * 2,
        scratch_shapes=[pltpu.VMEM((2, T, SB_HD), BF16)] * 4 + [pltpu.VMEM((2, T, SB_HD), F32)] + [pltpu.VMEM((2, SB_HD, T), F32)] * 2,
        name=name, args=(proj3, proj3, proj3, gq, gk, tot, dmix3), exch=exch)


def _exch_copies(kind, src_refs, land_refs, sems, with_arrivals=True):
    x_, y_, c_ = _coords()
    me = 4 * x_ + 2 * y_ + c_
    local, go, arrive = [], [], []
    for a, (src, land) in enumerate(zip(src_refs, land_refs)):
        send, recv, loc = sems[3 * a:3 * a + 3]
        local.append(pltpu.make_async_copy(src if kind == "gather" else src.at[me], land.at[me], loc))
        for k in range(1, N_DEV):
            px = 1 - x_ if k & 4 else x_
            py = 1 - y_ if k & 2 else y_
            pc = 1 - c_ if k & 1 else c_
            peer = 4 * px + 2 * py + pc
            out = src if kind == "gather" else src.at[peer]
            mk = functools.partial(pltpu.make_async_remote_copy, send_sem=send.at[k - 1], recv_sem=recv.at[k - 1],
                                   device_id=(px, py, pc), device_id_type=pl.DeviceIdType.MESH)
            go.append(mk(src_ref=out, dst_ref=land.at[me]))
            if with_arrivals:
                arrive.append(mk(src_ref=out, dst_ref=land.at[peer]))
    return local, go, arrive


def _hosted_call(body, *, grid, in_specs, out_specs, out_shape, scratch_shapes, name, args, exch=None):
    sem = ("arbitrary",) * len(grid)
    if exch is None:
        return pl.pallas_call(body, grid=grid, in_specs=in_specs, out_specs=out_specs, out_shape=out_shape,
                              scratch_shapes=scratch_shapes, compiler_params=_cp(sem), name=name)(*args)
    kind, srcs = exch
    ns, n_in, n_out, n_scr = len(srcs), len(in_specs), len(out_specs), len(scratch_shapes)
    lands = [_sds((N_DEV,) + s.shape if kind == "gather" else s.shape, s.dtype) for s in srcs]

    def wrapped(*refs):
        ins, src_refs = refs[:n_in], refs[n_in:n_in + ns]
        outs, land_refs = refs[n_in + ns:n_in + ns + n_out], refs[n_in + ns + n_out:n_in + 2 * ns + n_out]
        scr, sems = refs[n_in + 2 * ns + n_out:n_in + 2 * ns + n_out + n_scr], refs[n_in + 2 * ns + n_out + n_scr:]
        ids = [pl.program_id(d) for d in range(len(grid))]
        first = functools.reduce(lambda a, b: a & b, [i == 0 for i in ids])
        last = functools.reduce(lambda a, b: a & b, [i == g - 1 for i, g in zip(ids, grid)])

        @pl.when(first)
        def _():
            local, go, _ = _exch_copies(kind, src_refs, land_refs, sems, with_arrivals=False)
            for cp in local + go:
                cp.start()

        body(*ins, *outs, *scr)

        @pl.when(last)
        def _():
            local, go, arrive = _exch_copies(kind, src_refs, land_refs, sems)
            for cp in arrive:
                cp.wait_recv()
            for cp in go:
                cp.wait_send()
            for cp in local:
                cp.wait()

    any_spec = pl.BlockSpec(memory_space=pl.ANY)
    sems = [pltpu.SemaphoreType.DMA((N_DEV - 1,)), pltpu.SemaphoreType.DMA((N_DEV - 1,)), pltpu.SemaphoreType.DMA] * ns
    return pl.pallas_call(
        wrapped, grid=grid, in_specs=list(in_specs) + [any_spec] * ns, out_specs=list(out_specs) + [any_spec] * ns,
        out_shape=list(out_shape) + lands, scratch_shapes=list(scratch_shapes) + sems,
        compiler_params=_cp(sem), name=name)(*args, *srcs)


def _conv_silu(x, w, T):
    t_io = lax.broadcasted_iota(jnp.int32, x.shape, 0)
    xs = [x] + [jnp.where(t_io >= s, pltpu.roll(x, s, 0), 0.0) for s in range(1, CONV_K)]
    y = xs[0] * w[CONV_K - 1:CONV_K, :]
    for s in range(1, CONV_K):
        y = y + xs[s] * w[CONV_K - 1 - s:CONV_K - s, :]
    return y, y * _sigmoid(y), xs


def gdn_pre_fwd(proj3, conv_w, name):
    B, T, _ = proj3.shape
    qs = DN_HD ** -0.5

    def body(x_ref, w_ref, o_ref):
        kind = pl.program_id(1) // DN_HEADS
        _, s, _ = _conv_silu(x_ref[0], w_ref[...], T)
        n = lax.rsqrt(jnp.sum(s * s, axis=-1, keepdims=True) + EPS)
        c = jnp.where(kind == 0, qs, 1.0)
        o_ref[0, 0] = jnp.where(kind < 2, s * (n * c), s)

    return pl.pallas_call(
        body, grid=(B, 3 * DN_HEADS),
        in_specs=[pl.BlockSpec((1, T, 128), lambda b, j: (b, 0, COL_DNQKV + j)), pl.BlockSpec((CONV_K, 128), lambda b, j: (0, j))],
        out_specs=pl.BlockSpec((1, 1, T, 128), lambda b, j: (b, j // DN_HEADS, 0, j % DN_HEADS)),
        out_shape=_sds((B, 3, T, DN_W), F32),
        compiler_params=_cp(("parallel", "parallel")), name=name,
    )(proj3, conv_w)


def gdn_pre_bwd(proj3, conv_w, dqkv, name):
    B, T, _ = proj3.shape
    qs = DN_HD ** -0.5

    def body(x_ref, w_ref, d_ref, dx_ref, dw_ref):
        kind = pl.program_id(1) // DN_HEADS
        w = w_ref[...]
        y, s, xs = _conv_silu(x_ref[0], w, T)
        dout = d_ref[0, 0]
        n = lax.rsqrt(jnp.sum(s * s, axis=-1, keepdims=True) + EPS)
        c = jnp.where(kind == 0, qs, 1.0)
        dsn = c * (n * dout - s * (n * n * n) * jnp.sum(dout * s, axis=-1, keepdims=True))
        ds = jnp.where(kind < 2, dsn, dout)
        sg = _sigmoid(y)
        dy = ds * (sg * (1.0 + y * (1.0 - sg)))
        t_io = lax.broadcasted_iota(jnp.int32, dy.shape, 0)
        dx = dy * w[CONV_K - 1:CONV_K, :]
        dw_ref[0, CONV_K - 1:CONV_K, :] = jnp.sum(dy * xs[0], axis=0, keepdims=True)
        for sft in range(1, CONV_K):
            dx = dx + jnp.where(t_io < T - sft, pltpu.roll(dy, T - sft, 0), 0.0) * w[CONV_K - 1 - sft:CONV_K - sft, :]
            dw_ref[0, CONV_K - 1 - sft:CONV_K - sft, :] = jnp.sum(dy * xs[sft], axis=0, keepdims=True)
        dx_ref[0] = dx

    return pl.pallas_call(
        body, grid=(B, 3 * DN_HEADS),
        in_specs=[pl.BlockSpec((1, T, 128), lambda b, j: (b, 0, COL_DNQKV + j)), pl.BlockSpec((CONV_K, 128), lambda b, j: (0, j)),
                  pl.BlockSpec((1, 1, T, 128), lambda b, j: (b, j // DN_HEADS, 0, j % DN_HEADS))],
        out_specs=[pl.BlockSpec((1, T, 128), lambda b, j: (b, 0, j)), pl.BlockSpec((1, CONV_K, 128), lambda b, j: (b, 0, j))],
        out_shape=[_sds((B, T, 3 * DN_W), F32), _sds((B, CONV_K, 3 * DN_W), F32)],
        compiler_params=_cp(("parallel", "parallel")), name=name,
    )(proj3, conv_w, dqkv)


def gdn_gates_fwd(proj3, alog_row, dtb_row, name):
    B, T, _ = proj3.shape

    def body(x_ref, al_ref, dt_ref, o_ref):
        x = x_ref[0]
        lane = lax.broadcasted_iota(jnp.int32, x.shape, 1)
        g = -jnp.exp(al_ref[...]) * _softplus(x + dt_ref[...])
        o_ref[0] = jnp.where(lane < DN_HEADS, g, jnp.where(lane < 2 * DN_HEADS, _sigmoid(x), 0.0))

    row = pl.BlockSpec((1, 128), lambda b: (0, 0))
    return pl.pallas_call(
        body, grid=(B,),
        in_specs=[pl.BlockSpec((1, T, 128), lambda b: (b, 0, COL_AB)), row, row],
        out_specs=pl.BlockSpec((1, T, 128), lambda b: (b, 0, 0)),
        out_shape=_sds((B, T, 128), F32),
        compiler_params=_cp(("parallel",)), name=name,
    )(proj3, alog_row, dtb_row)


def gdn_gates_bwd(proj3, alog_row, dtb_row, dgates, name):
    B, T, _ = proj3.shape

    def body(x_ref, al_ref, dt_ref, d_ref, dx_ref, dal_ref, ddt_ref):
        @pl.when(pl.program_id(0) == 0)
        def _():
            dal_ref[...] = jnp.zeros_like(dal_ref)
            ddt_ref[...] = jnp.zeros_like(ddt_ref)

        x, d = x_ref[0], d_ref[0]
        lane = lax.broadcasted_iota(jnp.int32, x.shape, 1)
        a = x + dt_ref[...]
        na = -jnp.exp(al_ref[...])
        da = jnp.where(lane < DN_HEADS, d * na * _sigmoid(a), 0.0)
        bt = _sigmoid(x)
        dx_ref[0] = da + jnp.where((lane >= DN_HEADS) & (lane < 2 * DN_HEADS), d * bt * (1.0 - bt), 0.0)
        dal_ref[...] += jnp.sum(jnp.where(lane < DN_HEADS, d * na * _softplus(a), 0.0), axis=0, keepdims=True)
        ddt_ref[...] += jnp.sum(da, axis=0, keepdims=True)

    row = pl.BlockSpec((1, 128), lambda b: (0, 0))
    tile = pl.BlockSpec((1, T, 128), lambda b: (b, 0, 0))
    return pl.pallas_call(
        body, grid=(B,),
        in_specs=[pl.BlockSpec((1, T, 128), lambda b: (b, 0, COL_AB)), row, row, tile],
        out_specs=[tile, row, row],
        out_shape=[_sds((B, T, 128), F32), _sds((1, 128), F32), _sds((1, 128), F32)],
        compiler_params=_cp(("arbitrary",)), name=name,
    )(proj3, alog_row, dtb_row, dgates)


def gdn_post_fwd(ob, proj3, gain, name):
    B, T, _ = ob.shape

    def body(o_ref, z_ref, g_ref, out_ref):
        o, z = o_ref[0], z_ref[0]
        r = lax.rsqrt(jnp.mean(o * o, axis=-1, keepdims=True) + EPS)
        out_ref[0] = ((o * r * g_ref[...]) * (z * _sigmoid(z))).astype(out_ref.dtype)

    tile = pl.BlockSpec((1, T, 128), lambda b, h: (b, 0, h))
    return pl.pallas_call(
        body, grid=(B, DN_HEADS),
        in_specs=[tile, pl.BlockSpec((1, T, 128), lambda b, h: (b, 0, COL_Z + h)), pl.BlockSpec((1, 128), lambda b, h: (0, 0))],
        out_specs=tile, out_shape=_sds((B, T, DN_W), BF16),
        compiler_params=_cp(("parallel", "parallel")), name=name,
    )(ob, proj3, gain)


def gdn_post_bwd(ob, proj3, gain, dmix3, name):
    B, T, _ = ob.shape

    def body(o_ref, z_ref, g_ref, d_ref, do_ref, dz_ref, dg_ref):
        @pl.when((pl.program_id(0) == 0) & (pl.program_id(1) == 0))
        def _():
            dg_ref[...] = jnp.zeros_like(dg_ref)

        o, z, d, g = o_ref[0], z_ref[0], d_ref[0], g_ref[...]
        r = lax.rsqrt(jnp.mean(o * o, axis=-1, keepdims=True) + EPS)
        sg = _sigmoid(z)
        dn = d * (z * sg)
        dz_ref[0] = d * (o * r * g) * (sg * (1.0 + z * (1.0 - sg)))
        dg_ref[...] += jnp.sum(dn * o * r, axis=0, keepdims=True)
        u = dn * g
        do_ref[0] = r * u - o * (r * r * r) * jnp.mean(u * o, axis=-1, keepdims=True)

    tile = pl.BlockSpec((1, T, 128), lambda b, h: (b, 0, h))
    row = pl.BlockSpec((1, 128), lambda b, h: (0, 0))
    return pl.pallas_call(
        body, grid=(B, DN_HEADS),
        in_specs=[tile, pl.BlockSpec((1, T, 128), lambda b, h: (b, 0, COL_Z + h)), row,
                  pl.BlockSpec((1, T, 128), lambda b, h: (b, 0, SB_W // 128 + h))],
        out_specs=[tile, tile, row],
        out_shape=[_sds((B, T, DN_W), F32), _sds((B, T, DN_W), F32), _sds((1, 128), F32)],
        compiler_params=_cp(("arbitrary", "arbitrary")), name=name,
    )(ob, proj3, gain, dmix3)


def _tri_inv(low, ri, ci):
    m = (ri == ci).astype(F32) - jnp.where(((ri >> 1) == (ci >> 1)) & (ri > ci), low, 0.0)
    s = 2
    while s < DN_C:
        sh = s.bit_length()
        off = ((ri >> sh) == (ci >> sh)) & ((ri & (2 * s - 1)) >= s) & ((ci & (2 * s - 1)) < s)
        m = m - _pdot(m, _pdot(jnp.where(off, low, 0.0), m, _BNN), _BNN)
        s *= 2
    return m


_BNN = (((2,), (1,)), ((0,), (0,)))
_BNT = (((2,), (2,)), ((0,), (0,)))
_BTN = (((1,), (1,)), ((0,), (0,)))
DN_G = 16


def _chunk_common(q, k, v, gt, h, tm=None):
    C = DN_C
    G = q.shape[0]
    ri = lax.broadcasted_iota(jnp.int32, (C, C), 0)
    ci = lax.broadcasted_iota(jnp.int32, (C, C), 1)
    lane = lax.broadcasted_iota(jnp.int32, (C, 128), 1)
    incl, strict = ri >= ci, ri > ci
    g = jnp.sum(jnp.where(lane == h, gt, 0.0), axis=2, keepdims=True)
    beta = jnp.sum(jnp.where(lane == h + DN_HEADS, gt, 0.0), axis=2, keepdims=True)
    ones = jnp.ones((G, C, 128), F32)
    inclf = jnp.broadcast_to(incl.astype(F32), (G, C, C))
    gc = _pdot(inclf, g * ones, _BNN)[:, :, :1]
    gcr = _pdot(jnp.ones((G, C, C), F32), jnp.where(ri == ci, gc, 0.0), _BNN)
    decay = jnp.where(incl, jnp.exp(jnp.where(incl, gc - gcr, 0.0)), 0.0)
    e = jnp.exp(gc)
    kb, vb = k * beta, v * beta
    kk = _bdot(kb, k, _BNT)
    if tm is None:
        tm = _tri_inv(jnp.where(strict, kk * decay, 0.0), ri, ci)
    kbe = kb * e
    u = _bdot(tm, vb, _BNN)
    w = _bdot(tm, kbe, _BNN)
    qk = _bdot(q, k, _BNT)
    intra = jnp.where(incl, qk * decay, 0.0)
    gl = gc[:, C - 1:C, :]
    el = jnp.exp(gl)
    r = jnp.exp(gl - gc)
    return dict(lane=lane, incl=incl, inclf=inclf, strict=strict, beta=beta, decay=decay, e=e,
                kb=kb, vb=vb, kk=kk, tm=tm, kbe=kbe, u=u, w=w, qk=qk, intra=intra, el=el, r=r, ones=ones)


def gdn_chunk_fwd(qkv, gates, name):
    B, _, T, _ = qkv.shape
    NC = T // DN_C

    G = DN_G if NC % DN_G == 0 else 1
    GC = G * DN_C

    GS = G * DN_HD

    def body(x_ref, gt_ref, o_ref, st_ref, tm_ref, s_s, p_s, b_s, qp_s, el_s):
        h = pl.program_id(1)

        def group_a(gi, _):
            rows = pl.ds(pl.multiple_of(gi * GC, GC), GC)
            srow = pl.ds(pl.multiple_of(gi * GS, GS), GS)
            q, k, v = [x_ref[0, i, rows, :].reshape(G, DN_C, DN_HD) for i in range(3)]
            c = _chunk_common(q, k, v, gt_ref[0, rows, :].reshape(G, DN_C, 128), h)
            kr = k * c["r"]
            tm_ref[0, 0, rows, :] = c["tm"].reshape(GC, DN_C)
            p_s[srow, :] = _bdot(kr, c["w"], _BTN).reshape(GS, DN_HD)
            b_s[srow, :] = _bdot(kr, c["u"], _BTN).reshape(GS, DN_HD)
            qp_s[rows, :] = (q * c["e"] - _bdot(c["intra"], c["w"], _BNN)).reshape(GC, DN_HD)
            o_ref[0, rows, :] = _bdot(c["intra"], c["u"], _BNN).reshape(GC, DN_HD)
            el_s[pl.ds(gi * G, G), :, :] = c["el"] * jnp.ones((G, 1, 128), F32)
            return 0

        lax.fori_loop(0, NC // G, group_a, 0)
        s_s[...] = jnp.zeros_like(s_s)

        def chunk(n, _):
            srow = pl.ds(pl.multiple_of(n * DN_HD, DN_HD), DN_HD)
            st = s_s[...]
            st_ref[0, 0, srow, :] = st
            s_s[...] = (st * el_s[n] + b_s[srow, :]) - _bdot(p_s[srow, :], st)
            return 0

        lax.fori_loop(0, NC, chunk, 0)

        def group_c(gi, _):
            rows = pl.ds(pl.multiple_of(gi * GC, GC), GC)
            srow = pl.ds(pl.multiple_of(gi * GS, GS), GS)
            st = st_ref[0, 0, srow, :].reshape(G, DN_HD, DN_HD)
            o_ref[0, rows, :] += _bdot(qp_s[rows, :].reshape(G, DN_C, DN_HD), st, _BNN).reshape(GC, DN_HD)
            return 0

        lax.fori_loop(0, NC // G, group_c, 0)

    return pl.pallas_call(
        body, grid=(B, DN_HEADS),
        in_specs=[pl.BlockSpec((1, 3, T, 128), lambda b, h: (b, 0, 0, h)), pl.BlockSpec((1, T, 128), lambda b, h: (b, 0, 0))],
        out_specs=[pl.BlockSpec((1, T, 128), lambda b, h: (b, 0, h)), pl.BlockSpec((1, 1, NC * DN_HD, DN_HD), lambda b, h: (b, h, 0, 0)),
                   pl.BlockSpec((1, 1, T, DN_C), lambda b, h: (b, h, 0, 0))],
        out_shape=[_sds((B, T, DN_W), F32), _sds((B, DN_HEADS, NC * DN_HD, DN_HD), F32), _sds((B, DN_HEADS, T, DN_C), F32)],
        scratch_shapes=[pltpu.VMEM((DN_HD, DN_HD), F32)] + [pltpu.VMEM((NC * DN_HD, DN_HD), F32)] * 2
        + [pltpu.VMEM((T, DN_HD), F32), pltpu.VMEM((NC, 1, 128), F32)],
        compiler_params=_cp(("parallel", "parallel")), name=name,
    )(qkv, gates)


def gdn_chunk_bwd(qkv, gates, states, tms, dob, name):
    B, _, T, _ = qkv.shape
    NC = T // DN_C
    C = DN_C

    G = DN_G if NC % DN_G == 0 else 1
    GC = G * C

    GS = G * DN_HD

    def body(x_ref, gt_ref, st_ref, tm_ref, do_ref, dx_ref, dgt_ref, ds_s, p_s, r_s, el_s, dsa_s):
        h = pl.program_id(1)

        @pl.when(h == 0)
        def _():
            dgt_ref[...] = jnp.zeros_like(dgt_ref)

        def load(gi):
            rows = pl.ds(pl.multiple_of(gi * GC, GC), GC)
            q, k, v = [x_ref[0, i, rows, :].reshape(G, C, DN_HD) for i in range(3)]
            return rows, q, k, v, gt_ref[0, rows, :].reshape(G, C, 128), tm_ref[0, 0, rows, :].reshape(G, C, C)

        def group_a(gi, _):
            rows, q, k, v, gt, tm = load(gi)
            srow = pl.ds(pl.multiple_of(gi * GS, GS), GS)
            c = _chunk_common(q, k, v, gt, h, tm=tm)
            qp = q * c["e"] - _bdot(c["intra"], c["w"], _BNN)
            p_s[srow, :] = _bdot(k * c["r"], c["w"], _BTN).reshape(GS, DN_HD)
            r_s[srow, :] = _bdot(qp, do_ref[0, rows, :].reshape(G, C, DN_HD), _BTN).reshape(GS, DN_HD)
            el_s[pl.ds(gi * G, G), :, :] = c["el"] * jnp.ones((G, 1, 128), F32)
            return 0

        lax.fori_loop(0, NC // G, group_a, 0)
        ds_s[...] = jnp.zeros_like(ds_s)

        def chunk(m, _):
            n = NC - 1 - m
            srow = pl.ds(pl.multiple_of(n * DN_HD, DN_HD), DN_HD)
            dsn = ds_s[...]
            dsa_s[srow, :] = dsn
            ds_s[...] = (dsn * el_s[n] + r_s[srow, :]) - _bdot(p_s[srow, :], dsn, _TN)
            return 0

        lax.fori_loop(0, NC, chunk, 0)

        def group_c(gi, _):
            rows, q, k, v, gt, tm = load(gi)
            c = _chunk_common(q, k, v, gt, h, tm=tm)
            incl, strict, decay, e, r, el, tm = c["incl"], c["strict"], c["decay"], c["e"], c["r"], c["el"], c["tm"]
            srow = pl.ds(pl.multiple_of(gi * GS, GS), GS)
            st = st_ref[0, 0, srow, :].reshape(G, DN_HD, DN_HD)
            dsn = dsa_s[srow, :].reshape(G, DN_HD, DN_HD)
            do = do_ref[0, rows, :].reshape(G, C, DN_HD)
            dvn = _bdot(k * r, dsn, _BNN) + _bdot(c["intra"], do, _BTN)
            v_new = c["u"] - _bdot(c["w"], st, _BNN)
            del_ = jnp.sum(jnp.sum(dsn * st, axis=2, keepdims=True), axis=1, keepdims=True)
            dkr = _bdot(v_new, dsn, _BNT)
            dqe = _bdot(do, st, _BNT)
            dintra = _bdot(do, v_new, _BNT)
            dw = -_bdot(dvn, st, _BNT)
            dqkd = jnp.where(incl, dintra, 0.0)
            dqk = dqkd * decay
            ddecay = dqkd * c["qk"]
            dq = dqe * e + _bdot(dqk, k, _BNN)
            dk = dkr * r + _bdot(dqk, q, _BTN)
            dtm = _bdot(dvn, c["vb"], _BNT) + _bdot(dw, c["kbe"], _BNT)
            dvb = _bdot(tm, dvn, _BTN)
            dkbe = _bdot(tm, dw, _BTN)
            dkb = dkbe * e
            de = jnp.sum(dqe * q, axis=2, keepdims=True) + jnp.sum(dkbe * c["kb"], axis=2, keepdims=True)
            da = -_pdot(tm, _pdot(dtm, tm, _BNT), _BTN)
            dlow = jnp.where(strict, da, 0.0)
            dkk = dlow * decay
            ddecay = ddecay + dlow * c["kk"]
            dkb = dkb + _bdot(dkk, k, _BNN)
            dk = dk + _bdot(dkk, c["kb"], _BTN) + dkb * c["beta"]
            dbeta = jnp.sum(dkb * k, axis=2, keepdims=True) + jnp.sum(dvb * v, axis=2, keepdims=True)
            dv = dvb * c["beta"]
            dd = ddecay * decay
            dgc = jnp.sum(dd, axis=2, keepdims=True) - _pdot(dd, c["ones"], _BTN)[:, :, :1]
            dr = jnp.sum(dkr * k, axis=2, keepdims=True)
            dgc = dgc + de * e - dr * r
            dgl = jnp.sum(dr * r, axis=1, keepdims=True) + del_ * el
            rowc = lax.broadcasted_iota(jnp.int32, (C, 1), 0)
            dgc = dgc + jnp.where(rowc == C - 1, dgl, 0.0)
            dg = _pdot(c["inclf"], dgc * c["ones"], _BTN)[:, :, :1]
            dx_ref[0, 0, rows, :] = dq.reshape(GC, DN_HD)
            dx_ref[0, 1, rows, :] = dk.reshape(GC, DN_HD)
            dx_ref[0, 2, rows, :] = dv.reshape(GC, DN_HD)
            lane = c["lane"]
            dgt_ref[0, rows, :] += (jnp.where(lane == h, dg, 0.0) + jnp.where(lane == h + DN_HEADS, dbeta, 0.0)).reshape(GC, 128)
            return 0

        lax.fori_loop(0, NC // G, group_c, 0)

    return pl.pallas_call(
        body, grid=(B, DN_HEADS),
        in_specs=[pl.BlockSpec((1, 3, T, 128), lambda b, h: (b, 0, 0, h)), pl.BlockSpec((1, T, 128), lambda b, h: (b, 0, 0)),
                  pl.BlockSpec((1, 1, NC * DN_HD, DN_HD), lambda b, h: (b, h, 0, 0)), pl.BlockSpec((1, 1, T, C), lambda b, h: (b, h, 0, 0)),
                  pl.BlockSpec((1, T, 128), lambda b, h: (b, 0, h))],
        out_specs=[pl.BlockSpec((1, 3, T, 128), lambda b, h: (b, 0, 0, h)), pl.BlockSpec((1, T, 128), lambda b, h: (b, 0, 0))],
        out_shape=[_sds((B, 3, T, DN_W), F32), _sds((B, T, 128), F32)],
        scratch_shapes=[pltpu.VMEM((DN_HD, DN_HD), F32)] + [pltpu.VMEM((NC * DN_HD, DN_HD), F32)] * 2
        + [pltpu.VMEM((NC, 1, 128), F32), pltpu.VMEM((NC * DN_HD, DN_HD), F32)],
        compiler_params=_cp(("parallel", "arbitrary")), name=name,
    )(qkv, gates, states, tms, dob)


def ada_fwd(c_all, w_ada, b_sl, name):
    L, D, W = w_ada.shape
    NBt = c_all.shape[0]

    def body(c_ref, w_ref, b_ref, o_ref):
        cv = c_ref[...]
        o_ref[0] = _pdot(cv * _sigmoid(cv), w_ref[0]) + b_ref[0]

    return pl.pallas_call(
        body, grid=(L,),
        in_specs=[pl.BlockSpec((NBt, D), lambda l: (0, 0)), pl.BlockSpec((1, D, W), lambda l: (l, 0, 0)), pl.BlockSpec((1, 1, W), lambda l: (l, 0, 0))],
        out_specs=pl.BlockSpec((1, NBt, W), lambda l: (l, 0, 0)),
        out_shape=_sds((L, NBt, W), F32),
        compiler_params=_cp(("parallel",)), name=name,
    )(c_all, w_ada, b_sl)


def ada_bwd(c_all, dmod_cols, name):
    L, NBt, W = dmod_cols.shape
    D = c_all.shape[1]

    def body(c_ref, d_ref, o_ref):
        cv = c_ref[...]
        o_ref[0] = _pdot(cv * _sigmoid(cv), d_ref[0], _TN)

    return pl.pallas_call(
        body, grid=(L,),
        in_specs=[pl.BlockSpec((NBt, D), lambda l: (0, 0)), pl.BlockSpec((1, NBt, W), lambda l: (l, 0, 0))],
        out_specs=pl.BlockSpec((1, D, W), lambda l: (l, 0, 0)),
        out_shape=_sds((L, D, W), F32),
        compiler_params=_cp(("parallel",)), name=name,
    )(c_all, dmod_cols)


def adamw(partials, w, m, v, name):
    L, R, C = w.shape
    per_layer = isinstance(partials, (list, tuple))
    plist = list(partials) if per_layer else [partials]
    P = plist[0].shape[0]
    tr = _pick(R, 256)

    def body(*refs):
        p_refs = refs[:len(plist)]
        w_ref, m_ref, v_ref, g_ref, d_ref, nm_ref, nv_ref = refs[len(plist):]

        def run(read):
            g = read(0).astype(F32)
            for i in range(1, P):
                g = g + read(i).astype(F32)
            nm = ADAM_B1 * m_ref[0] + (1.0 - ADAM_B1) * g
            nv = ADAM_B2 * v_ref[0] + (1.0 - ADAM_B2) * (g * g)
            m_hat = nm / (1.0 - ADAM_B1 ** ADAM_STEP)
            v_hat = nv / (1.0 - ADAM_B2 ** ADAM_STEP)
            g_ref[0] = g
            d_ref[0] = -ADAM_LR * (m_hat / (jnp.sqrt(v_hat) + ADAM_EPS) + ADAM_WD * w_ref[0])
            nm_ref[0] = nm
            nv_ref[0] = nv

        if per_layer:
            for l in range(L):
                @pl.when(pl.program_id(0) == l)
                def _(l=l):
                    run(lambda i: p_refs[l][i])
        else:
            run(lambda i: p_refs[0][i, 0])

    tile = pl.BlockSpec((1, tr, C), lambda l, i: (l, i, 0))
    if per_layer:
        p_specs = [pl.BlockSpec((P, tr, C), lambda l, i, k=k: (0, jnp.where(l == k, i, 0), 0)) for k in range(L)]
    else:
        p_specs = [pl.BlockSpec((P, 1, tr, C), lambda l, i: (0, l, i, 0))]
    return pl.pallas_call(
        body, grid=(L, R // tr),
        in_specs=p_specs + [tile, tile, tile],
        out_specs=[tile] * 4, out_shape=[_sds((L, R, C), F32)] * 4,
        compiler_params=_cp(("arbitrary", "arbitrary")), name=name,
    )(*plist, w, m, v)


def _coords():
    return lax.axis_index("x"), lax.axis_index("y"), lax.axis_index("c")


def all_gather(x, name):
    return all_gather_many([x], name)[0]


def all_gather_many(xs, name):
    any_spec = pl.BlockSpec(memory_space=pl.ANY)
    n = len(xs)

    def body(*refs):
        x_refs, out_refs = refs[:n], refs[n:2 * n]
        send_sems, recv_sems, local_sems = refs[2 * n:]
        x_, y_, c_ = _coords()
        me, sibling = (x_, y_, c_), (x_, y_, 1 - c_)
        chips = [(1 - x_, y_), (x_, 1 - y_), (1 - x_, 1 - y_)]

        def copy(a, k, block, to, own=False):
            px, py, pc = block
            rows = out_refs[a].at[4 * px + 2 * py + pc]
            return pltpu.make_async_remote_copy(
                src_ref=x_refs[a] if own else rows, dst_ref=rows,
                send_sem=send_sems.at[7 * a + k], recv_sem=recv_sems.at[7 * a + k],
                device_id=to, device_id_type=pl.DeviceIdType.MESH)

        arrays = range(n)
        mine = [pltpu.make_async_copy(x_refs[a], out_refs[a].at[4 * x_ + 2 * y_ + c_], local_sems.at[a]) for a in arrays]
        first = [copy(a, 0, me, sibling, own=True) for a in arrays]
        first += [copy(a, 1 + j, me, (*chip, c_), own=True) for a in arrays for j, chip in enumerate(chips)]
        for cp in mine + first:
            cp.start()
        passed = []
        for a in arrays:
            for j, chip in enumerate(chips):
                copy(a, 1 + j, (*chip, c_), me).wait_recv()
                passed.append(copy(a, 4 + j, (*chip, c_), sibling))
                passed[-1].start()
        for a in arrays:
            copy(a, 0, sibling, me).wait_recv()
            for j, chip in enumerate(chips):
                copy(a, 4 + j, (*chip, 1 - c_), me).wait_recv()
        for cp in first + passed:
            cp.wait_send()
        for cp in mine:
            cp.wait()

    return pl.pallas_call(
        body, out_shape=[_sds((N_DEV,) + x.shape, x.dtype) for x in xs],
        in_specs=[any_spec] * n, out_specs=[any_spec] * n,
        scratch_shapes=[pltpu.SemaphoreType.DMA((7 * n,)), pltpu.SemaphoreType.DMA((7 * n,)), pltpu.SemaphoreType.DMA((n,))],
        name=name,
    )(*xs)


def all_to_all(x, name):
    any_spec = pl.BlockSpec(memory_space=pl.ANY)

    def body(x_ref, out_ref, send_sems, recv_sems, local_sem):
        x_, y_, c_ = _coords()
        me = 4 * x_ + 2 * y_ + c_
        mine = pltpu.make_async_copy(x_ref.at[me], out_ref.at[me], local_sem)
        mine.start()
        copies = []
        for k in range(1, N_DEV):
            px = 1 - x_ if k & 4 else x_
            py = 1 - y_ if k & 2 else y_
            pc = 1 - c_ if k & 1 else c_
            peer = 4 * px + 2 * py + pc
            copies.append((pltpu.make_async_remote_copy(
                src_ref=x_ref.at[peer], dst_ref=out_ref.at[me],
                send_sem=send_sems.at[k - 1], recv_sem=recv_sems.at[k - 1],
                device_id=(px, py, pc), device_id_type=pl.DeviceIdType.MESH), peer))
        for cp, _ in copies:
            cp.start()
        for k, (cp, peer) in enumerate(copies):
            pltpu.make_async_remote_copy(
                src_ref=x_ref.at[peer], dst_ref=out_ref.at[peer],
                send_sem=send_sems.at[k], recv_sem=recv_sems.at[k],
                device_id=(x_, y_, c_), device_id_type=pl.DeviceIdType.MESH).wait_recv()
        for cp, _ in copies:
            cp.wait_send()
        mine.wait()

    return pl.pallas_call(
        body, out_shape=_sds(x.shape, x.dtype),
        in_specs=[any_spec], out_specs=any_spec,
        scratch_shapes=[pltpu.SemaphoreType.DMA((7,)), pltpu.SemaphoreType.DMA((7,)), pltpu.SemaphoreType.DMA],
        name=name,
    )(x)


def _rows128(a):
    return a.reshape(-1, 128)


def _pad_lanes(a):
    return jnp.pad(a, ((0, 0), (0, 128 - a.shape[1])))


def kernel(x, c, w_ada, b_ada, norm_mix, norm_mlp, w_in, sb_q_norm, sb_k_norm, conv_w, a_log, dt_bias, dn_out_norm, w_out, w_ff1, w_ff2, loss_target, m_w_ada, m_b_ada, m_norm_mix, m_norm_mlp, m_w_in, m_sb_q_norm, m_sb_k_norm, m_conv_w, m_a_log, m_dt_bias, m_dn_out_norm, m_w_out, m_w_ff1, m_w_ff2, v_w_ada, v_b_ada, v_norm_mix, v_norm_mlp, v_w_in, v_sb_q_norm, v_sb_k_norm, v_conv_w, v_a_log, v_dt_bias, v_dn_out_norm, v_w_out, v_w_ff1, v_w_ff2):
    B, T, D = x.shape
    L = w_ada.shape[0]
    N = B * T
    FF = w_ff1.shape[2] * N_DEV
    WA = w_ada.shape[2]
    CS = conv_w.shape[2]
    me = 4 * lax.axis_index("x") + 2 * lax.axis_index("y") + lax.axis_index("c")
    tm = _pick(T, 1024)

    wb = [w.astype(BF16) for w in (w_in, w_out, w_ff1, w_ff2)]

    def assemble(lands):
        win_g, wout_g, w1_g, w2_g = lands
        return (jnp.pad(win_g.transpose(1, 0, 2).reshape(D, IN_W), ((0, 0), (0, IN_WP - IN_W))),
                wout_g.reshape(SB_W + DN_W, D), w1_g.transpose(1, 0, 2).reshape(D, FF), w2_g.reshape(FF, D))

    *lands0, conv_g, c_g = all_gather_many([w[0] for w in wb] + [conv_w, c], "comm_gather_first")
    weights = [assemble(lands0)]
    conv_full = conv_g.transpose(1, 2, 0, 3).reshape(L, CONV_K, 3 * DN_W)

    c_all = c_g.reshape(N_DEV * B, D)
    b_sl = lax.dynamic_slice_in_dim(b_ada, me * WA, WA, axis=1).reshape(L, 1, WA)
    mod_sh = ada_fwd(c_all, w_ada, b_sl, "ada_fwd")
    mod_g = all_gather(mod_sh, "comm_gather_mod")
    mod = lax.dynamic_slice_in_dim(mod_g, me * B, B, axis=2).transpose(1, 2, 0, 3).reshape(L, B, 6 * D)

    def mod_part(l, i):
        return mod[l, :, i * D:(i + 1) * D].reshape(B, 1, D)

    alog_row = _pad_lanes(a_log).reshape(L, 1, 128)
    dtb_row = _pad_lanes(dt_bias).reshape(L, 1, 128)

    def gate_epi(acc, xv, g):
        return acc, xv + g[0] * acc

    def relu2(a):
        r = jnp.maximum(a.astype(F32), 0.0)
        return r * r

    def times_gate(a, g):
        return a * g[0]

    tile_ij = lambda i, j, k: (i, j)

    saved = []
    xc = x
    for l in range(L):
        sh_a, sc_a, g_a, sh_m, sc_m, g_m = [mod_part(l, i) for i in range(6)]
        h = ln_mod_fwd(xc, norm_mix[l:l + 1], sc_a, sh_a, "ln_mod_fwd")
        W_in_l, W_out_l, W_1_l, W_2_l = weights[l]
        proj3 = matmul(h.reshape(N, D), W_in_l, mode="nn", name="mm_proj", tm=256)[0].reshape(B, T, IN_WP)
        if l + 1 < L:
            o_a, tot, *lands = sb_attn_fwd(proj3, sb_q_norm[l:l + 1], sb_k_norm[l:l + 1], "sb_attn_fwd_gather",
                                           exch=("gather", [w[l + 1] for w in wb]))
            weights.append(assemble(lands))
        else:
            o_a, tot = sb_attn_fwd(proj3, sb_q_norm[l:l + 1], sb_k_norm[l:l + 1], "sb_attn_fwd")
        qkv = gdn_pre_fwd(proj3, conv_full[l], "gdn_pre_fwd")
        gates = gdn_gates_fwd(proj3, alog_row[l], dtb_row[l], "gdn_gates_fwd")
        ob, states, tms = gdn_chunk_fwd(qkv, gates, "gdn_chunk_fwd")
        o_b = gdn_post_fwd(ob, proj3, dn_out_norm[l:l + 1], "gdn_post_fwd")
        mix = jnp.concatenate([o_a, o_b], axis=-1)
        y1, x_mid = matmul(
            mix.reshape(N, SB_W + DN_W), W_out_l, mode="nn", name="mm_out", out_dtypes=(BF16, F32), tm=tm, epi=gate_epi,
            extras=[(xc.reshape(N, D), (tm, _pick(D, 1024)), tile_ij),
                    (g_a, (1, 1, _pick(D, 1024)), lambda i, j, k: (i * tm // T, 0, j))])
        x_mid = x_mid.reshape(B, T, D)
        h2 = ln_mod_fwd(x_mid, norm_mlp[l:l + 1], sc_m, sh_m, "ln_mod_fwd")
        u = matmul(h2.reshape(N, D), W_1_l, mode="nn", name="mm_ff1", out_dtypes=(BF16,))[0]
        y2, x_out = matmul(
            u, W_2_l, mode="nn", name="mm_ff2", out_dtypes=(BF16, F32), tm=tm, a_fn=relu2, epi=gate_epi,
            extras=[(x_mid.reshape(N, D), (tm, _pick(D, 1024)), tile_ij),
                    (g_m, (1, 1, _pick(D, 1024)), lambda i, j, k: (i * tm // T, 0, j))])
        saved.append(dict(x=xc, h=h, proj3=proj3, tot=tot, qkv=qkv, gates=gates, states=states, tms=tms, ob=ob, mix=mix,
                          y1=y1, x_mid=x_mid, h2=h2, u=u, y2=y2))
        xc = x_out.reshape(B, T, D)

    dx, sq = loss_grad(xc, loss_target, "loss_grad")
    loss = lax.psum((0.5 / D) * jnp.sum(sq), AXES)

    g_win, g_wout, g_w1, g_w2, dmods, smalls, parts = [], [], [], [], [], [], []
    pending = None
    tk_tok = tm
    wi = w_in.shape[2]

    def shard_layer(gin, gout, g1, g2):
        return [None if gin is None else gin[:, :IN_W].reshape(D, N_DEV, wi).transpose(1, 0, 2), gout.reshape(N_DEV, w_out.shape[1], D),
                g1.reshape(D, N_DEV, w_ff1.shape[2]).transpose(1, 0, 2), g2.reshape(N_DEV, w_ff2.shape[1], D)]

    for l in reversed(range(L)):
        s = saved[l]
        W_in_l, W_out_l, W_1_l, W_2_l = weights[l]
        sh_a, sc_a, g_a, sh_m, sc_m, g_m = [mod_part(l, i) for i in range(6)]
        gate_k = lambda g, blk: (g, (1, 1, blk), lambda i, j, k: (i * tm // T, 0, k))
        gate_tok = lambda g, blk: (g, (1, 1, blk), lambda i, j, k: (k * tk_tok // T, 0, j))
        dx2 = dx.reshape(N, D)
        dg_m = rowsum_prod(dx, s["y2"].reshape(B, T, D), "rowsum_prod")
        du = matmul(dx2, W_2_l, mode="nt", name="mm_ff2_da", out_dtypes=(BF16,), tm=tm, a_fn=times_gate,
                    a_extras=[gate_k(g_m, _pick(D, 1024))],
                    epi=lambda acc, uv: (acc * (2.0 * jnp.maximum(uv, 0.0)),),
                    extras=[(s["u"], (tm, _pick(FF, 1024)), tile_ij)])[0]
        g_w2.append(matmul(s["u"], dx2, mode="tn", name="mm_ff2_dw", out_dtypes=(BF16,), tk=tk_tok, a_fn=relu2,
                           b_fn=times_gate, b_extras=[gate_tok(g_m, _pick(D, 1024))])[0])
        g_w1.append(matmul(s["h2"].reshape(N, D), du, mode="tn", name="mm_ff1_dw", out_dtypes=(BF16,))[0])
        dh2 = matmul(du, W_1_l, mode="nt", name="mm_ff1_da")[0]
        dx_mid, dgn_mlp, dsc_m, dsh_m = ln_mod_bwd(s["x_mid"], norm_mlp[l:l + 1], sc_m, dh2.reshape(B, T, D), dx, "ln_mod_bwd")
        dxm2 = dx_mid.reshape(N, D)
        dg_a = rowsum_prod(dx_mid, s["y1"].reshape(B, T, D), "rowsum_prod")
        dmix3 = matmul(dxm2, W_out_l, mode="nt", name="mm_out_da", tm=tm, a_fn=times_gate,
                       a_extras=[gate_k(g_a, _pick(D, 1024))])[0].reshape(B, T, SB_W + DN_W)
        g_wout.append(matmul(s["mix"].reshape(N, SB_W + DN_W), dxm2, mode="tn", name="mm_out_dw", out_dtypes=(BF16,),
                             tk=tk_tok, b_fn=times_gate, b_extras=[gate_tok(g_a, _pick(D, 1024))])[0])
        srcs = (pending or []) + (shard_layer(None, g_wout[-1], g_w1[-1], g_w2[-1])[1:] if l == 0 else [])
        if srcs:
            dq_a, dk_a, dv_a, dgq, dgk, *lands = sb_attn_bwd(
                s["proj3"], sb_q_norm[l:l + 1], sb_k_norm[l:l + 1], s["tot"], dmix3,
                "sb_attn_bwd_scatter" if l else "sb_attn_bwd_scatter0", exch=("scatter", srcs))
            if pending:
                parts.append(lands[:4])
            lands0 = lands[-3:]
        else:
            dq_a, dk_a, dv_a, dgq, dgk = sb_attn_bwd(s["proj3"], sb_q_norm[l:l + 1], sb_k_norm[l:l + 1], s["tot"], dmix3, "sb_attn_bwd")
        dob, dz, dgn_dn = gdn_post_bwd(s["ob"], s["proj3"], dn_out_norm[l:l + 1], dmix3, "gdn_post_bwd")
        dqkv, dgates = gdn_chunk_bwd(s["qkv"], s["gates"], s["states"], s["tms"], dob, "gdn_chunk_bwd")
        d_dnqkv, dconv_b = gdn_pre_bwd(s["proj3"], conv_full[l], dqkv, "gdn_pre_bwd")
        d_ab, dalog, ddtb = gdn_gates_bwd(s["proj3"], alog_row[l], dtb_row[l], dgates, "gdn_gates_bwd")
        dproj = [a.reshape(N, a.shape[-1]) for a in (dq_a, dk_a, dv_a, d_dnqkv, dz, d_ab)]
        g_win.append(proj_bwd_weight(s["h"].reshape(N, D), dproj, "mm_proj_dw"))
        dh = proj_bwd_input(dproj, W_in_l, "mm_proj_da")
        dx, dgn_mix, dsc_a, dsh_a = ln_mod_bwd(s["x"], norm_mix[l:l + 1], sc_a, dh.reshape(B, T, D), dx_mid, "ln_mod_bwd")
        pending = shard_layer(g_win[-1], g_wout[-1], g_w1[-1], g_w2[-1])
        dmods.append(jnp.concatenate([dsh_a, dsc_a, dg_a, dsh_m, dsc_m, dg_m], axis=-1).reshape(B, 6 * D))
        smalls.append(dict(norm_mix=dgn_mix, norm_mlp=dgn_mlp, sbq=dgq, sbk=dgk, alog=dalog, dtb=ddtb, dnorm=dgn_dn,
                           conv=jnp.sum(dconv_b, axis=0)))
    parts.append([all_to_all(pending[0], "comm_scatter_w_in")] + lands0)
    for lst in (dmods, smalls, parts):
        lst.reverse()
    grad_x = dx

    def update2d(parts, w, m, v, name):
        return [o[0] for o in adamw(parts[:, None], w[None], m[None], v[None], name)]

    p_win, p_wout, p_w1, p_w2 = [[parts[l][i] for l in range(L)] for i in range(4)]
    r_win = adamw(p_win, w_in, m_w_in, v_w_in, "adamw_w_in")
    r_wout = adamw(p_wout, w_out, m_w_out, v_w_out, "adamw_w_out")
    r_w1 = adamw(p_w1, w_ff1, m_w_ff1, v_w_ff1, "adamw_w_ff1")
    r_w2 = adamw(p_w2, w_ff2, m_w_ff2, v_w_ff2, "adamw_w_ff2")

    def pack(f):
        return jnp.concatenate([
            _rows128(f("norm_mix")), _rows128(f("norm_mlp")), _rows128(f("sbq")), _rows128(f("sbk")),
            f("alog"), f("dtb"), f("dnorm"), _rows128(f("conv"))], axis=0)

    part = pack(lambda n: jnp.concatenate([sm[n] for sm in smalls], axis=0))
    dmod_g, part_g = all_gather_many([jnp.stack(dmods), part], "comm_gather_last")

    dmod_all = dmod_g.transpose(1, 0, 2, 3).reshape(L, N_DEV * B, 6 * D)
    g_wada = ada_bwd(c_all, lax.dynamic_slice_in_dim(dmod_all, me * WA, WA, axis=2), "ada_bwd")
    r_wada = adamw(g_wada[None], w_ada, m_w_ada, v_w_ada, "adamw_w_ada")
    r_bada = update2d(dmod_g.transpose(0, 2, 1, 3).reshape(N_DEV * B, L, 6 * D), b_ada, m_b_ada, v_b_ada, "adamw_b_ada")

    names = ["norm_mix", "norm_mlp", "sbq", "sbk", "alog", "dtb", "dnorm"]
    n_rep = part.shape[0] - L * CONV_K * 3 * DN_W // 128
    params = dict(norm_mix=(norm_mix, m_norm_mix, v_norm_mix), norm_mlp=(norm_mlp, m_norm_mlp, v_norm_mlp),
                  sbq=(sb_q_norm, m_sb_q_norm, v_sb_q_norm), sbk=(sb_k_norm, m_sb_k_norm, v_sb_k_norm),
                  alog=(a_log, m_a_log, v_a_log), dtb=(dt_bias, m_dt_bias, v_dt_bias),
                  dnorm=(dn_out_norm, m_dn_out_norm, v_dn_out_norm))

    def rows_of(n, a):
        return _pad_lanes(a) if n in ("alog", "dtb") else _rows128(a)

    packed = [jnp.concatenate([rows_of(n, params[n][i]) for n in names], axis=0) for i in range(3)]
    r_small = update2d(part_g[:, :n_rep], packed[0], packed[1], packed[2], "adamw_small")
    small_out = {}
    off = 0
    for n in names:
        w0 = params[n][0]
        nr = rows_of(n, w0).shape[0]
        vals = [o[off:off + nr] for o in r_small]
        small_out[n] = [(vv[:, :w0.shape[1]] if n in ("alog", "dtb") else vv.reshape(w0.shape)) for vv in vals]
        off += nr
    conv_parts = part_g[:, n_rep:].reshape(N_DEV, L, CONV_K, 3 * DN_W)
    r_conv = adamw(lax.dynamic_slice_in_dim(conv_parts, me * CS, CS, axis=3), conv_w, m_conv_w, v_conv_w, "adamw_conv")

    order = [r_wada, r_bada, small_out["norm_mix"], small_out["norm_mlp"], r_win, small_out["sbq"], small_out["sbk"],
             r_conv, small_out["alog"], small_out["dtb"], small_out["dnorm"], r_wout, r_w1, r_w2]
    outs = [loss, grad_x]
    for i in range(4):
        outs += [r[i] for r in order]
    return tuple(outs)
```

```python
import functools
import math

import jax
import jax.numpy as jnp
from jax import lax
from jax.experimental import pallas as pl
from jax.experimental.pallas import tpu as pltpu

F32 = jnp.float32
BF16 = jnp.bfloat16
EPS = 1e-6
N_DEV = 8
AXES = ("x", "y", "c")

SB_HEADS, SB_HD = 8, 64
SB_W = SB_HEADS * SB_HD
SB_BLK = 128
DN_HEADS, DN_HD = 4, 128
DN_W = DN_HEADS * DN_HD
DN_C = 64
CONV_K = 4
IN_W = 3 * SB_W + 4 * DN_W + 2 * DN_HEADS
IN_WP = 3 * SB_W + 4 * DN_W + 128
COL_DNQKV = 3 * SB_W // 128
COL_Z = COL_DNQKV + 3 * DN_W // 128
COL_AB = COL_Z + DN_W // 128

ADAM_LR, ADAM_B1, ADAM_B2, ADAM_EPS, ADAM_WD, ADAM_STEP = 0.001, 0.9, 0.999, 1e-08, 0.01, 10

VMEM_LIMIT = 56 * 1024 * 1024


def _cp(sem):
    return pltpu.CompilerParams(dimension_semantics=sem, vmem_limit_bytes=VMEM_LIMIT)


def _pick(dim, pref):
    return pref if dim % pref == 0 else dim


def _sds(shape, dtype):
    return jax.ShapeDtypeStruct(tuple(shape), dtype)


_NN = (((1,), (0,)), ((), ()))
_NT = (((1,), (1,)), ((), ()))
_TN = (((0,), (0,)), ((), ()))


def _bdot(a, b, dims=_NN):
    return lax.dot_general(a.astype(BF16), b.astype(BF16), dims, preferred_element_type=F32)


def _split(a):
    hi = a.astype(BF16)
    lo = (a - hi.astype(F32)).astype(BF16)
    return hi, lo


def _pdot(a, b, dims=_NN):
    ah, al = _split(a)
    bh, bl = _split(b)
    d = functools.partial(lax.dot_general, dimension_numbers=dims, preferred_element_type=F32)
    return d(ah, bh) + (d(ah, bl) + d(al, bh))


def _sigmoid(x):
    return 1.0 / (1.0 + jnp.exp(-x))


def _softplus(x):
    return jnp.maximum(x, 0.0) + jnp.log(1.0 + jnp.exp(-jnp.abs(x)))


def matmul(a, b, *, mode, name, out_dtypes=(F32,), a_fn=None, a_extras=(), b_fn=None, b_extras=(),
           epi=None, extras=(), tm=1024, tn=1024, tk=1024):
    if mode == "tn":
        K, M = a.shape
    else:
        M, K = a.shape
    N = b.shape[0] if mode == "nt" else b.shape[1]
    tm, tn, tk = _pick(M, tm), _pick(N, tn), _pick(K, tk)
    nk = K // tk
    dims = {"nn": _NN, "nt": _NT, "tn": _TN}[mode]
    a_spec = pl.BlockSpec((tk, tm), lambda i, j, k: (k, i)) if mode == "tn" else pl.BlockSpec((tm, tk), lambda i, j, k: (i, k))
    b_spec = pl.BlockSpec((tn, tk), lambda i, j, k: (j, k)) if mode == "nt" else pl.BlockSpec((tk, tn), lambda i, j, k: (k, j))
    na, nb, ne, no = len(a_extras), len(b_extras), len(extras), len(out_dtypes)

    def body(*refs):
        a_ref, b_ref = refs[0], refs[1]
        ax = refs[2:2 + na]
        bx = refs[2 + na:2 + na + nb]
        ex = refs[2 + na + nb:2 + na + nb + ne]
        outs = refs[2 + na + nb + ne:2 + na + nb + ne + no]
        acc_ref = refs[-1]
        k = pl.program_id(2)

        @pl.when(k == 0)
        def _():
            acc_ref[...] = jnp.zeros_like(acc_ref)

        av = a_ref[...]
        if a_fn is not None:
            av = a_fn(av, *[r[...] for r in ax])
        bv = b_ref[...]
        if b_fn is not None:
            bv = b_fn(bv, *[r[...] for r in bx])
        acc_ref[...] += lax.dot_general(av.astype(BF16), bv.astype(BF16), dims, preferred_element_type=F32)

        @pl.when(k == nk - 1)
        def _():
            res = acc_ref[...]
            res = (res,) if epi is None else epi(res, *[r[...] for r in ex])
            for o_ref, r in zip(outs, res):
                o_ref[...] = r.astype(o_ref.dtype)

    xs = list(a_extras) + list(b_extras) + list(extras)
    return pl.pallas_call(
        body,
        grid=(M // tm, N // tn, nk),
        in_specs=[a_spec, b_spec] + [pl.BlockSpec(bs, im) for _, bs, im in xs],
        out_specs=[pl.BlockSpec((tm, tn), lambda i, j, k: (i, j)) for _ in out_dtypes],
        out_shape=[_sds((M, N), dt) for dt in out_dtypes],
        scratch_shapes=[pltpu.VMEM((tm, tn), F32)],
        compiler_params=_cp(("parallel", "parallel", "arbitrary")),
        name=name,
    )(a, b, *[x for x, _, _ in xs])


def proj_bwd_input(pieces, w, name, tm=512, tn=1024):
    N, D = pieces[0].shape[0], w.shape[0]
    widths = [p.shape[1] for p in pieces]
    offs = [sum(widths[:i]) for i in range(len(widths))]
    tm, tn = _pick(N, tm), _pick(D, tn)

    def body(*refs):
        w_ref, o_ref = refs[len(pieces)], refs[len(pieces) + 1]
        acc = None
        for p_ref, off, wd in zip(refs, offs, widths):
            t = lax.dot_general(p_ref[...].astype(BF16), w_ref[:, off:off + wd], _NT, preferred_element_type=F32)
            acc = t if acc is None else acc + t
        o_ref[...] = acc

    return pl.pallas_call(
        body, grid=(N // tm, D // tn),
        in_specs=[pl.BlockSpec((tm, wd), lambda i, j: (i, 0)) for wd in widths] + [pl.BlockSpec((tn, sum(widths)), lambda i, j: (j, 0))],
        out_specs=pl.BlockSpec((tm, tn), lambda i, j: (i, j)), out_shape=_sds((N, D), F32),
        compiler_params=_cp(("parallel", "parallel")), name=name,
    )(*pieces, w)


def proj_bwd_weight(h, pieces, name, tm=512, tk=512):
    N, D = h.shape
    widths = [p.shape[1] for p in pieces]
    offs = [sum(widths[:i]) for i in range(len(widths))]
    tm, tk = _pick(D, tm), _pick(N, tk)
    nk = N // tk

    def body(*refs):
        h_ref, o_ref, acc_ref = refs[0], refs[len(pieces) + 1], refs[len(pieces) + 2]
        k = pl.program_id(1)

        @pl.when(k == 0)
        def _():
            acc_ref[...] = jnp.zeros_like(acc_ref)

        hv = h_ref[...]
        for p_ref, off, wd in zip(refs[1:], offs, widths):
            acc_ref[:, off:off + wd] += lax.dot_general(hv, p_ref[...].astype(BF16), _TN, preferred_element_type=F32)

        @pl.when(k == nk - 1)
        def _():
            o_ref[...] = acc_ref[...].astype(o_ref.dtype)

    return pl.pallas_call(
        body, grid=(D // tm, nk),
        in_specs=[pl.BlockSpec((tk, tm), lambda i, k: (k, i))] + [pl.BlockSpec((tk, wd), lambda i, k: (k, 0)) for wd in widths],
        out_specs=pl.BlockSpec((tm, sum(widths)), lambda i, k: (i, 0)), out_shape=_sds((D, sum(widths)), BF16),
        scratch_shapes=[pltpu.VMEM((tm, sum(widths)), F32)],
        compiler_params=_cp(("parallel", "arbitrary")), name=name,
    )(h, *pieces)


def ln_mod_fwd(x, gain, sc, sh, name):
    B, T, D = x.shape
    tt = _pick(T, 512)

    def body(x_ref, g_ref, sc_ref, sh_ref, h_ref):
        xv = x_ref[0]
        r = lax.rsqrt(jnp.mean(xv * xv, axis=-1, keepdims=True) + EPS)
        h = (xv * r * g_ref[...]) * (1.0 + sc_ref[0]) + sh_ref[0]
        h_ref[0] = h.astype(h_ref.dtype)

    return pl.pallas_call(
        body, grid=(B, T // tt),
        in_specs=[pl.BlockSpec((1, tt, D), lambda b, t: (b, t, 0)), pl.BlockSpec((1, D), lambda b, t: (0, 0)),
                  pl.BlockSpec((1, 1, D), lambda b, t: (b, 0, 0)), pl.BlockSpec((1, 1, D), lambda b, t: (b, 0, 0))],
        out_specs=pl.BlockSpec((1, tt, D), lambda b, t: (b, t, 0)),
        out_shape=_sds((B, T, D), BF16),
        compiler_params=_cp(("parallel", "parallel")), name=name,
    )(x, gain, sc, sh)


def ln_mod_bwd(x, gain, sc, dh, dres, name):
    B, T, D = x.shape
    tt = _pick(T, 512)

    def body(x_ref, g_ref, sc_ref, dh_ref, dres_ref, dx_ref, dg_ref, dsc_ref, dsh_ref):
        b, t = pl.program_id(0), pl.program_id(1)
        xv, dhv = x_ref[0], dh_ref[0]
        g, s = g_ref[...], sc_ref[0]
        r = lax.rsqrt(jnp.mean(xv * xv, axis=-1, keepdims=True) + EPS)
        xn = xv * r
        dxn = dhv * (g * (1.0 + s))
        dx_ref[0] = dres_ref[0] + r * (dxn - xn * jnp.mean(dxn * xn, axis=-1, keepdims=True))
        s1 = jnp.sum(dhv * xn, axis=0, keepdims=True)
        s2 = jnp.sum(dhv, axis=0, keepdims=True)

        @pl.when(t == 0)
        def _():
            dsc_ref[0] = jnp.zeros_like(s1)
            dsh_ref[0] = jnp.zeros_like(s1)

        @pl.when((t == 0) & (b == 0))
        def _():
            dg_ref[...] = jnp.zeros_like(s1)

        dsc_ref[0] += s1 * g
        dsh_ref[0] += s2
        dg_ref[...] += s1 * (1.0 + s)

    tile = pl.BlockSpec((1, tt, D), lambda b, t: (b, t, 0))
    row = pl.BlockSpec((1, D), lambda b, t: (0, 0))
    brow = pl.BlockSpec((1, 1, D), lambda b, t: (b, 0, 0))
    return pl.pallas_call(
        body, grid=(B, T // tt),
        in_specs=[tile, row, brow, tile, tile],
        out_specs=[tile, row, brow, brow],
        out_shape=[_sds((B, T, D), F32), _sds((1, D), F32), _sds((B, 1, D), F32), _sds((B, 1, D), F32)],
        compiler_params=_cp(("arbitrary", "arbitrary")), name=name,
    )(x, gain, sc, dh, dres)


def rowsum_prod(a, b, name):
    B, T, D = a.shape
    tt = _pick(T, 512)

    def body(a_ref, b_ref, o_ref):
        @pl.when(pl.program_id(1) == 0)
        def _():
            o_ref[...] = jnp.zeros_like(o_ref)

        o_ref[0] += jnp.sum(a_ref[0] * b_ref[0], axis=0, keepdims=True)

    tile = pl.BlockSpec((1, tt, D), lambda b, t: (b, t, 0))
    return pl.pallas_call(
        body, grid=(B, T // tt), in_specs=[tile, tile],
        out_specs=pl.BlockSpec((1, 1, D), lambda b, t: (b, 0, 0)),
        out_shape=_sds((B, 1, D), F32),
        compiler_params=_cp(("parallel", "arbitrary")), name=name,
    )(a, b)


def loss_grad(y, tgt, name):
    B, T, D = y.shape
    tt = _pick(T, 512)

    def body(y_ref, t_ref, dy_ref, s_ref):
        @pl.when((pl.program_id(0) == 0) & (pl.program_id(1) == 0))
        def _():
            s_ref[...] = jnp.zeros_like(s_ref)

        e = y_ref[0] - t_ref[0]
        dy_ref[0] = e * (1.0 / D)
        s_ref[...] += jnp.sum(e * e, axis=0, keepdims=True)

    tile = pl.BlockSpec((1, tt, D), lambda b, t: (b, t, 0))
    return pl.pallas_call(
        body, grid=(B, T // tt), in_specs=[tile, tile],
        out_specs=[tile, pl.BlockSpec((1, D), lambda b, t: (0, 0))],
        out_shape=[_sds((B, T, D), F32), _sds((1, D), F32)],
        compiler_params=_cp(("arbitrary", "arbitrary")), name=name,
    )(y, tgt)


def _sb_group(nb):
    return 4 if nb % 4 == 0 else (2 if nb % 2 == 0 else 1)


def _sb_qrows(t, kw):
    return 256 if (t % 256 == 0 and kw % 256 == 0) else SB_BLK


def _diag_step(sblock, sj, carry, thr, qb, kw):
    half = kw // 2
    if half % SB_BLK or half < qb:
        return sblock(sj, carry, True)
    return lax.cond(thr + qb <= half, lambda c: sblock(sj, c, True, half), lambda c: sblock(sj, c, True), carry)


def _tri_sum(x, tri):
    return lax.dot_general(x.astype(BF16), tri, _NN, preferred_element_type=F32)


def _tri2(cond):
    return cond.astype(BF16)


def sb_attn_fwd(proj3, gq, gk, name, exch=None):
    B, T, _ = proj3.shape
    NB = T // SB_BLK
    G = _sb_group(NB)
    KW = G * SB_BLK
    QB = _sb_qrows(T, KW)
    scale = SB_HD ** -0.5

    def body(q_ref, k_ref, v_ref, gq_ref, gk_ref, o_ref, tot_ref, qn_s, kn_s, v_s):
        row_io = lax.broadcasted_iota(jnp.int32, (SB_BLK, SB_BLK), 0)
        col_io = lax.broadcasted_iota(jnp.int32, (SB_BLK, SB_BLK), 1)
        tri = _tri2(row_io > col_io)

        def prep(i, _):
            rows = pl.ds(pl.multiple_of(i * SB_BLK, SB_BLK), SB_BLK)
            for hh in range(2):
                sl = slice(hh * SB_HD, (hh + 1) * SB_HD)
                q = q_ref[0, rows, sl]
                k = k_ref[0, rows, sl]
                qn_s[hh, rows, :] = (q * lax.rsqrt(jnp.mean(q * q, -1, keepdims=True) + EPS) * (gq_ref[...] * scale)).astype(BF16)
                kn_s[hh, rows, :] = (k * lax.rsqrt(jnp.mean(k * k, -1, keepdims=True) + EPS) * gk_ref[...]).astype(BF16)
                v_s[hh, rows, :] = v_ref[0, rows, sl].astype(BF16)
            return 0

        lax.fori_loop(0, NB, prep, 0)

        diff_w = (lax.broadcasted_iota(jnp.int32, (QB, KW), 1)
                  - lax.broadcasted_iota(jnp.int32, (QB, KW), 0))

        def qblock(i, _):
            rows = pl.ds(pl.multiple_of(i * QB, QB), QB)
            qn = [qn_s[hh, rows, :] for hh in range(2)]
            nsj = ((i + 1) * QB - 1) // KW + 1

            thr = i * QB - (nsj - 1) * KW

            def sblock(sj, carry, masked, width=KW):
                g = width // SB_BLK
                cols = pl.ds(pl.multiple_of(sj * KW, KW), width)
                mask = diff_w[:, :width] < thr
                zs = [lax.dot_general(qn[hh], kn_s[hh, cols, :], _NT, preferred_element_type=F32) for hh in range(2)]
                lgs, lss = [], []
                for hh in range(2):
                    sp = _softplus(zs[hh])
                    lgs.append(jnp.where(mask, -sp, 0.0) if masked else -sp)
                    lss.append(zs[hh] - sp)
                blocks = [lgs[hh][:, s * SB_BLK:(s + 1) * SB_BLK] for hh in range(2) for s in range(g)]
                ts_all = _tri_sum(jnp.concatenate(blocks, axis=0), tri)
                atts, css = [], []
                for hh in range(2):
                    cs = carry[hh][1]
                    ps = []
                    for s in reversed(range(g)):
                        n = hh * g + s
                        ts = ts_all[n * QB:(n + 1) * QB]
                        ps.append(lss[hh][:, s * SB_BLK:(s + 1) * SB_BLK] + ts + cs)
                        cs = cs + (ts[:, :1] + blocks[n][:, :1])
                    p = ps[0] if g == 1 else jnp.concatenate(ps[::-1], axis=1)
                    att = jnp.exp(p)
                    atts.append((jnp.where(mask, att, 0.0) if masked else att).astype(BF16))
                    css.append(cs)
                return tuple((carry[hh][0] + lax.dot_general(atts[hh], v_s[hh, cols, :], _NN, preferred_element_type=F32), css[hh])
                             for hh in range(2))

            init = (jnp.zeros((QB, SB_HD), F32), jnp.zeros((QB, 1), F32))
            res = _diag_step(sblock, nsj - 1, (init, init), thr, QB, KW)
            res = lax.fori_loop(0, nsj - 1, lambda jj, c: sblock(nsj - 2 - jj, c, False), res)
            for hh in range(2):
                o_ref[0, rows, hh * SB_HD:(hh + 1) * SB_HD] = res[hh][0].astype(o_ref.dtype)
                tot_ref[0, hh, rows, :] = res[hh][1]
            return 0

        lax.fori_loop(0, T // QB, qblock, 0)

    blk = lambda off: pl.BlockSpec((1, T, 128), lambda b, p: (b, 0, off + p))
    grow = pl.BlockSpec((1, SB_HD), lambda b, p: (0, 0))
    return _hosted_call(
        body, grid=(B, SB_W // 128),
        in_specs=[blk(0), blk(SB_W // 128), blk(2 * SB_W // 128), grow, grow],
        out_specs=[pl.BlockSpec((1, T, 128), lambda b, p: (b, 0, p)), pl.BlockSpec((1, 2, T, 1), lambda b, p: (b, p, 0, 0))],
        out_shape=[_sds((B, T, SB_W), BF16), _sds((B, SB_HEADS, T, 1), F32)],
        scratch_shapes=[pltpu.VMEM((2, T, SB_HD), BF16)] * 3,
        name=name, args=(proj3, proj3, proj3, gq, gk), exch=exch)


def sb_attn_bwd(proj3, gq, gk, tot, dmix3, name, exch=None):
    B, T, _ = proj3.shape
    NB = T // SB_BLK
    G = _sb_group(NB)
    KW = G * SB_BLK
    QB = _sb_qrows(T, KW)
    scale = SB_HD ** -0.5

    def body(q_ref, k_ref, v_ref, gq_ref, gk_ref, tot_ref, do_ref, dq_ref, dk_ref, dv_ref, dgq_ref, dgk_ref,
             qn_s, kn_s, v_s, do_s, dqn_s, dkn_s, dv_s):
        row_io = lax.broadcasted_iota(jnp.int32, (SB_BLK, SB_BLK), 0)
        col_io = lax.broadcasted_iota(jnp.int32, (SB_BLK, SB_BLK), 1)
        tri = _tri2(row_io > col_io)
        trip = _tri2(row_io < col_io)

        @pl.when((pl.program_id(0) == 0) & (pl.program_id(1) == 0))
        def _():
            dgq_ref[...] = jnp.zeros_like(dgq_ref)
            dgk_ref[...] = jnp.zeros_like(dgk_ref)

        def prep(i, _):
            rows = pl.ds(pl.multiple_of(i * SB_BLK, SB_BLK), SB_BLK)
            for hh in range(2):
                sl = slice(hh * SB_HD, (hh + 1) * SB_HD)
                q = q_ref[0, rows, sl]
                k = k_ref[0, rows, sl]
                qn_s[hh, rows, :] = (q * lax.rsqrt(jnp.mean(q * q, -1, keepdims=True) + EPS) * (gq_ref[...] * scale)).astype(BF16)
                kn_s[hh, rows, :] = (k * lax.rsqrt(jnp.mean(k * k, -1, keepdims=True) + EPS) * gk_ref[...]).astype(BF16)
                v_s[hh, rows, :] = v_ref[0, rows, sl].astype(BF16)
                do_s[hh, rows, :] = do_ref[0, rows, sl].astype(BF16)
            return 0

        lax.fori_loop(0, NB, prep, 0)
        dkn_s[...] = jnp.zeros_like(dkn_s)
        dv_s[...] = jnp.zeros_like(dv_s)

        diff_w = (lax.broadcasted_iota(jnp.int32, (QB, KW), 1)
                  - lax.broadcasted_iota(jnp.int32, (QB, KW), 0))

        def qblock(i, _):
            rows = pl.ds(pl.multiple_of(i * QB, QB), QB)
            qn = [qn_s[hh, rows, :] for hh in range(2)]
            dov = [do_s[hh, rows, :] for hh in range(2)]
            tot = [tot_ref[0, hh, rows, :] for hh in range(2)]
            nsj = ((i + 1) * QB - 1) // KW + 1
            qnT = [qn[hh].astype(F32).T.astype(BF16) for hh in range(2)]
            doT = [dov[hh].astype(F32).T.astype(BF16) for hh in range(2)]

            thr = i * QB - (nsj - 1) * KW

            def sblock(sj, carry, masked, width=KW):
                g = width // SB_BLK
                cols = pl.ds(pl.multiple_of(sj * KW, KW), width)
                mask = diff_w[:, :width] < thr
                hs = range(2)
                kns = [kn_s[hh, cols, :] for hh in hs]
                zs = [lax.dot_general(qn[hh], kns[hh], _NT, preferred_element_type=F32) for hh in hs]
                datts = [lax.dot_general(dov[hh], v_s[hh, cols, :], _NT, preferred_element_type=F32) for hh in hs]
                lgs, lss = [], []
                for hh in hs:
                    sp = _softplus(zs[hh])
                    lgs.append(jnp.where(mask, -sp, 0.0) if masked else -sp)
                    lss.append(zs[hh] - sp)
                blocks = [lgs[hh][:, s * SB_BLK:(s + 1) * SB_BLK] for hh in hs for s in range(g)]
                ts_all = _tri_sum(jnp.concatenate(blocks, axis=0), tri)
                atts, dps, cums = [], [], []
                for hh in hs:
                    cum = carry[hh][1]
                    ps = []
                    for s in range(g):
                        n = hh * g + s
                        ts = ts_all[n * QB:(n + 1) * QB]
                        cum = cum + (ts[:, :1] + blocks[n][:, :1])
                        ps.append(lss[hh][:, s * SB_BLK:(s + 1) * SB_BLK] + ts + (tot[hh] - cum))
                    p = ps[0] if g == 1 else jnp.concatenate(ps, axis=1)
                    att = jnp.exp(p)
                    att = jnp.where(mask, att, 0.0) if masked else att
                    atts.append(att.astype(BF16))
                    dps.append(att * datts[hh])
                    cums.append(cum)
                dblocks = [dps[hh][:, s * SB_BLK:(s + 1) * SB_BLK] for hh in hs for s in range(g)]
                tp_all = _tri_sum(jnp.concatenate(dblocks, axis=0), trip)
                dzs, cdps = [], []
                for hh in hs:
                    cdp = carry[hh][2]
                    dls = []
                    for s in range(g):
                        n = hh * g + s
                        tp = tp_all[n * QB:(n + 1) * QB]
                        dls.append(tp + cdp)
                        cdp = cdp + (tp[:, SB_BLK - 1:] + dblocks[n][:, SB_BLK - 1:])
                    dlg = dls[0] if g == 1 else jnp.concatenate(dls, axis=1)
                    dz = dps[hh] - jnp.exp(lss[hh]) * (dps[hh] + dlg)
                    dzs.append((jnp.where(mask, dz, 0.0) if masked else dz).astype(BF16))
                    cdps.append(cdp)
                new = []
                for hh in hs:
                    dq = carry[hh][0] + lax.dot_general(dzs[hh], kns[hh], _NN, preferred_element_type=F32)
                    dkn_s[hh, :, cols] += lax.dot_general(qnT[hh], dzs[hh], _NN, preferred_element_type=F32)
                    dv_s[hh, :, cols] += lax.dot_general(doT[hh], atts[hh], _NN, preferred_element_type=F32)
                    new.append((dq, cums[hh], cdps[hh]))
                return tuple(new)

            z1 = jnp.zeros((QB, 1), F32)
            init = (jnp.zeros((QB, SB_HD), F32), z1, z1)
            res = lax.fori_loop(0, nsj - 1, lambda sj, c: sblock(sj, c, False), (init, init))
            res = _diag_step(sblock, nsj - 1, res, thr, QB, KW)
            for hh in range(2):
                dqn_s[hh, rows, :] = res[hh][0]
            return 0

        lax.fori_loop(0, T // QB, qblock, 0)

        def fin(i, carry):
            aq, ak = carry
            rows = pl.ds(pl.multiple_of(i * SB_BLK, SB_BLK), SB_BLK)
            for hh in range(2):
                sl = slice(hh * SB_HD, (hh + 1) * SB_HD)
                for src_ref, g_ref, out_ref, mult, which in ((q_ref, gq_ref, dq_ref, scale, 0), (k_ref, gk_ref, dk_ref, 1.0, 1)):
                    xr = src_ref[0, rows, sl]
                    r = lax.rsqrt(jnp.mean(xr * xr, -1, keepdims=True) + EPS)
                    dy = (dqn_s[hh, rows, :] if which == 0 else dkn_s[hh, :, rows].T) * mult
                    u = dy * g_ref[...]
                    out_ref[0, rows, sl] = r * u - xr * (r * r * r) * jnp.mean(u * xr, -1, keepdims=True)
                    part = jnp.sum(dy * xr * r, axis=0, keepdims=True)
                    if which == 0:
                        aq = aq + part
                    else:
                        ak = ak + part
                dv_ref[0, rows, sl] = dv_s[hh, :, rows].T
            return aq, ak

        z64 = jnp.zeros((1, SB_HD), F32)
        aq, ak = lax.fori_loop(0, NB, fin, (z64, z64))
        dgq_ref[...] += aq
        dgk_ref[...] += ak

    blk = lambda off: pl.BlockSpec((1, T, 128), lambda b, p: (b, 0, off + p))
    grow = pl.BlockSpec((1, SB_HD), lambda b, p: (0, 0))
    return _hosted_call(
        body, grid=(B, SB_W // 128),
        in_specs=[blk(0), blk(SB_W // 128), blk(2 * SB_W // 128), grow, grow,
                  pl.BlockSpec((1, 2, T, 1), lambda b, p: (b, p, 0, 0)), blk(0)],
        out_specs=[blk(0), blk(0), blk(0), grow, grow],
        out_shape=[_sds((B, T, SB_W), F32)] * 3 + [_sds((1, SB_HD), F32)] * 2,
        scratch_shapes=[pltpu.VMEM((2, T, SB_HD), BF16)] * 4 + [pltpu.VMEM((2, T, SB_HD), F32)] + [pltpu.VMEM((2, SB_HD, T), F32)] * 2,
        name=name, args=(proj3, proj3, proj3, gq, gk, tot, dmix3), exch=exch)


def _exch_copies(kind, src_refs, land_refs, sems, with_arrivals=True):
    x_, y_, c_ = _coords()
    me = 4 * x_ + 2 * y_ + c_
    local, go, arrive = [], [], []
    for a, (src, land) in enumerate(zip(src_refs, land_refs)):
        send, recv, loc = sems[3 * a:3 * a + 3]
        local.append(pltpu.make_async_copy(src if kind == "gather" else src.at[me], land.at[me], loc))
        for k in range(1, N_DEV):
            px = 1 - x_ if k & 4 else x_
            py = 1 - y_ if k & 2 else y_
            pc = 1 - c_ if k & 1 else c_
            peer = 4 * px + 2 * py + pc
            out = src if kind == "gather" else src.at[peer]
            mk = functools.partial(pltpu.make_async_remote_copy, send_sem=send.at[k - 1], recv_sem=recv.at[k - 1],
                                   device_id=(px, py, pc), device_id_type=pl.DeviceIdType.MESH)
            go.append(mk(src_ref=out, dst_ref=land.at[me]))
            if with_arrivals:
                arrive.append(mk(src_ref=out, dst_ref=land.at[peer]))
    return local, go, arrive


def _hosted_call(body, *, grid, in_specs, out_specs, out_shape, scratch_shapes, name, args, exch=None):
    sem = ("arbitrary",) * len(grid)
    if exch is None:
        return pl.pallas_call(body, grid=grid, in_specs=in_specs, out_specs=out_specs, out_shape=out_shape,
                              scratch_shapes=scratch_shapes, compiler_params=_cp(sem), name=name)(*args)
    kind, srcs = exch
    ns, n_in, n_out, n_scr = len(srcs), len(in_specs), len(out_specs), len(scratch_shapes)
    lands = [_sds((N_DEV,) + s.shape if kind == "gather" else s.shape, s.dtype) for s in srcs]

    def wrapped(*refs):
        ins, src_refs = refs[:n_in], refs[n_in:n_in + ns]
        outs, land_refs = refs[n_in + ns:n_in + ns + n_out], refs[n_in + ns + n_out:n_in + 2 * ns + n_out]
        scr, sems = refs[n_in + 2 * ns + n_out:n_in + 2 * ns + n_out + n_scr], refs[n_in + 2 * ns + n_out + n_scr:]
        ids = [pl.program_id(d) for d in range(len(grid))]
        first = functools.reduce(lambda a, b: a & b, [i == 0 for i in ids])
        last = functools.reduce(lambda a, b: a & b, [i == g - 1 for i, g in zip(ids, grid)])

        @pl.when(first)
        def _():
            local, go, _ = _exch_copies(kind, src_refs, land_refs, sems, with_arrivals=False)
            for cp in local + go:
                cp.start()

        body(*ins, *outs, *scr)

        @pl.when(last)
        def _():
            local, go, arrive = _exch_copies(kind, src_refs, land_refs, sems)
            for cp in arrive:
                cp.wait_recv()
            for cp in go:
                cp.wait_send()
            for cp in local:
                cp.wait()

    any_spec = pl.BlockSpec(memory_space=pl.ANY)
    sems = [pltpu.SemaphoreType.DMA((N_DEV - 1,)), pltpu.SemaphoreType.DMA((N_DEV - 1,)), pltpu.SemaphoreType.DMA] * ns
    return pl.pallas_call(
        wrapped, grid=grid, in_specs=list(in_specs) + [any_spec] * ns, out_specs=list(out_specs) + [any_spec] * ns,
        out_shape=list(out_shape) + lands, scratch_shapes=list(scratch_shapes) + sems,
        compiler_params=_cp(sem), name=name)(*args, *srcs)


def _conv_silu(x, w, T):
    t_io = lax.broadcasted_iota(jnp.int32, x.shape, 0)
    xs = [x] + [jnp.where(t_io >= s, pltpu.roll(x, s, 0), 0.0) for s in range(1, CONV_K)]
    y = xs[0] * w[CONV_K - 1:CONV_K, :]
    for s in range(1, CONV_K):
        y = y + xs[s] * w[CONV_K - 1 - s:CONV_K - s, :]
    return y, y * _sigmoid(y), xs


def gdn_pre_fwd(proj3, conv_w, name):
    B, T, _ = proj3.shape
    qs = DN_HD ** -0.5

    def body(x_ref, w_ref, o_ref):
        kind = pl.program_id(1) // DN_HEADS
        _, s, _ = _conv_silu(x_ref[0], w_ref[...], T)
        n = lax.rsqrt(jnp.sum(s * s, axis=-1, keepdims=True) + EPS)
        c = jnp.where(kind == 0, qs, 1.0)
        o_ref[0, 0] = jnp.where(kind < 2, s * (n * c), s)

    return pl.pallas_call(
        body, grid=(B, 3 * DN_HEADS),
        in_specs=[pl.BlockSpec((1, T, 128), lambda b, j: (b, 0, COL_DNQKV + j)), pl.BlockSpec((CONV_K, 128), lambda b, j: (0, j))],
        out_specs=pl.BlockSpec((1, 1, T, 128), lambda b, j: (b, j // DN_HEADS, 0, j % DN_HEADS)),
        out_shape=_sds((B, 3, T, DN_W), F32),
        compiler_params=_cp(("parallel", "parallel")), name=name,
    )(proj3, conv_w)


def gdn_pre_bwd(proj3, conv_w, dqkv, name):
    B, T, _ = proj3.shape
    qs = DN_HD ** -0.5

    def body(x_ref, w_ref, d_ref, dx_ref, dw_ref):
        kind = pl.program_id(1) // DN_HEADS
        w = w_ref[...]
        y, s, xs = _conv_silu(x_ref[0], w, T)
        dout = d_ref[0, 0]
        n = lax.rsqrt(jnp.sum(s * s, axis=-1, keepdims=True) + EPS)
        c = jnp.where(kind == 0, qs, 1.0)
        dsn = c * (n * dout - s * (n * n * n) * jnp.sum(dout * s, axis=-1, keepdims=True))
        ds = jnp.where(kind < 2, dsn, dout)
        sg = _sigmoid(y)
        dy = ds * (sg * (1.0 + y * (1.0 - sg)))
        t_io = lax.broadcasted_iota(jnp.int32, dy.shape, 0)
        dx = dy * w[CONV_K - 1:CONV_K, :]
        dw_ref[0, CONV_K - 1:CONV_K, :] = jnp.sum(dy * xs[0], axis=0, keepdims=True)
        for sft in range(1, CONV_K):
            dx = dx + jnp.where(t_io < T - sft, pltpu.roll(dy, T - sft, 0), 0.0) * w[CONV_K - 1 - sft:CONV_K - sft, :]
            dw_ref[0, CONV_K - 1 - sft:CONV_K - sft, :] = jnp.sum(dy * xs[sft], axis=0, keepdims=True)
        dx_ref[0] = dx

    return pl.pallas_call(
        body, grid=(B, 3 * DN_HEADS),
        in_specs=[pl.BlockSpec((1, T, 128), lambda b, j: (b, 0, COL_DNQKV + j)), pl.BlockSpec((CONV_K, 128), lambda b, j: (0, j)),
                  pl.BlockSpec((1, 1, T, 128), lambda b, j: (b, j // DN_HEADS, 0, j % DN_HEADS))],
        out_specs=[pl.BlockSpec((1, T, 128), lambda b, j: (b, 0, j)), pl.BlockSpec((1, CONV_K, 128), lambda b, j: (b, 0, j))],
        out_shape=[_sds((B, T, 3 * DN_W), F32), _sds((B, CONV_K, 3 * DN_W), F32)],
        compiler_params=_cp(("parallel", "parallel")), name=name,
    )(proj3, conv_w, dqkv)


def gdn_gates_fwd(proj3, alog_row, dtb_row, name):
    B, T, _ = proj3.shape

    def body(x_ref, al_ref, dt_ref, o_ref):
        x = x_ref[0]
        lane = lax.broadcasted_iota(jnp.int32, x.shape, 1)
        g = -jnp.exp(al_ref[...]) * _softplus(x + dt_ref[...])
        o_ref[0] = jnp.where(lane < DN_HEADS, g, jnp.where(lane < 2 * DN_HEADS, _sigmoid(x), 0.0))

    row = pl.BlockSpec((1, 128), lambda b: (0, 0))
    return pl.pallas_call(
        body, grid=(B,),
        in_specs=[pl.BlockSpec((1, T, 128), lambda b: (b, 0, COL_AB)), row, row],
        out_specs=pl.BlockSpec((1, T, 128), lambda b: (b, 0, 0)),
        out_shape=_sds((B, T, 128), F32),
        compiler_params=_cp(("parallel",)), name=name,
    )(proj3, alog_row, dtb_row)


def gdn_gates_bwd(proj3, alog_row, dtb_row, dgates, name):
    B, T, _ = proj3.shape

    def body(x_ref, al_ref, dt_ref, d_ref, dx_ref, dal_ref, ddt_ref):
        @pl.when(pl.program_id(0) == 0)
        def _():
            dal_ref[...] = jnp.zeros_like(dal_ref)
            ddt_ref[...] = jnp.zeros_like(ddt_ref)

        x, d = x_ref[0], d_ref[0]
        lane = lax.broadcasted_iota(jnp.int32, x.shape, 1)
        a = x + dt_ref[...]
        na = -jnp.exp(al_ref[...])
        da = jnp.where(lane < DN_HEADS, d * na * _sigmoid(a), 0.0)
        bt = _sigmoid(x)
        dx_ref[0] = da + jnp.where((lane >= DN_HEADS) & (lane < 2 * DN_HEADS), d * bt * (1.0 - bt), 0.0)
        dal_ref[...] += jnp.sum(jnp.where(lane < DN_HEADS, d * na * _softplus(a), 0.0), axis=0, keepdims=True)
        ddt_ref[...] += jnp.sum(da, axis=0, keepdims=True)

    row = pl.BlockSpec((1, 128), lambda b: (0, 0))
    tile = pl.BlockSpec((1, T, 128), lambda b: (b, 0, 0))
    return pl.pallas_call(
        body, grid=(B,),
        in_specs=[pl.BlockSpec((1, T, 128), lambda b: (b, 0, COL_AB)), row, row, tile],
        out_specs=[tile, row, row],
        out_shape=[_sds((B, T, 128), F32), _sds((1, 128), F32), _sds((1, 128), F32)],
        compiler_params=_cp(("arbitrary",)), name=name,
    )(proj3, alog_row, dtb_row, dgates)


def gdn_post_fwd(ob, proj3, gain, name):
    B, T, _ = ob.shape

    def body(o_ref, z_ref, g_ref, out_ref):
        o, z = o_ref[0], z_ref[0]
        r = lax.rsqrt(jnp.mean(o * o, axis=-1, keepdims=True) + EPS)
        out_ref[0] = ((o * r * g_ref[...]) * (z * _sigmoid(z))).astype(out_ref.dtype)

    tile = pl.BlockSpec((1, T, 128), lambda b, h: (b, 0, h))
    return pl.pallas_call(
        body, grid=(B, DN_HEADS),
        in_specs=[tile, pl.BlockSpec((1, T, 128), lambda b, h: (b, 0, COL_Z + h)), pl.BlockSpec((1, 128), lambda b, h: (0, 0))],
        out_specs=tile, out_shape=_sds((B, T, DN_W), BF16),
        compiler_params=_cp(("parallel", "parallel")), name=name,
    )(ob, proj3, gain)


def gdn_post_bwd(ob, proj3, gain, dmix3, name):
    B, T, _ = ob.shape

    def body(o_ref, z_ref, g_ref, d_ref, do_ref, dz_ref, dg_ref):
        @pl.when((pl.program_id(0) == 0) & (pl.program_id(1) == 0))
        def _():
            dg_ref[...] = jnp.zeros_like(dg_ref)

        o, z, d, g = o_ref[0], z_ref[0], d_ref[0], g_ref[...]
        r = lax.rsqrt(jnp.mean(o * o, axis=-1, keepdims=True) + EPS)
        sg = _sigmoid(z)
        dn = d * (z * sg)
        dz_ref[0] = d * (o * r * g) * (sg * (1.0 + z * (1.0 - sg)))
        dg_ref[...] += jnp.sum(dn * o * r, axis=0, keepdims=True)
        u = dn * g
        do_ref[0] = r * u - o * (r * r * r) * jnp.mean(u * o, axis=-1, keepdims=True)

    tile = pl.BlockSpec((1, T, 128), lambda b, h: (b, 0, h))
    row = pl.BlockSpec((1, 128), lambda b, h: (0, 0))
    return pl.pallas_call(
        body, grid=(B, DN_HEADS),
        in_specs=[tile, pl.BlockSpec((1, T, 128), lambda b, h: (b, 0, COL_Z + h)), row,
                  pl.BlockSpec((1, T, 128), lambda b, h: (b, 0, SB_W // 128 + h))],
        out_specs=[tile, tile, row],
        out_shape=[_sds((B, T, DN_W), F32), _sds((B, T, DN_W), F32), _sds((1, 128), F32)],
        compiler_params=_cp(("arbitrary", "arbitrary")), name=name,
    )(ob, proj3, gain, dmix3)


def _tri_inv(low, ri, ci):
    m = (ri == ci).astype(F32) - jnp.where(((ri >> 1) == (ci >> 1)) & (ri > ci), low, 0.0)
    s = 2
    while s < DN_C:
        sh = s.bit_length()
        off = ((ri >> sh) == (ci >> sh)) & ((ri & (2 * s - 1)) >= s) & ((ci & (2 * s - 1)) < s)
        m = m - _pdot(m, _pdot(jnp.where(off, low, 0.0), m, _BNN), _BNN)
        s *= 2
    return m


_BNN = (((2,), (1,)), ((0,), (0,)))
_BNT = (((2,), (2,)), ((0,), (0,)))
_BTN = (((1,), (1,)), ((0,), (0,)))
DN_G = 16


def _chunk_common(q, k, v, gt, h, tm=None):
    C = DN_C
    G = q.shape[0]
    ri = lax.broadcasted_iota(jnp.int32, (C, C), 0)
    ci = lax.broadcasted_iota(jnp.int32, (C, C), 1)
    lane = lax.broadcasted_iota(jnp.int32, (C, 128), 1)
    incl, strict = ri >= ci, ri > ci
    g = jnp.sum(jnp.where(lane == h, gt, 0.0), axis=2, keepdims=True)
    beta = jnp.sum(jnp.where(lane == h + DN_HEADS, gt, 0.0), axis=2, keepdims=True)
    ones = jnp.ones((G, C, 128), F32)
    inclf = jnp.broadcast_to(incl.astype(F32), (G, C, C))
    gc = _pdot(inclf, g * ones, _BNN)[:, :, :1]
    gcr = _pdot(jnp.ones((G, C, C), F32), jnp.where(ri == ci, gc, 0.0), _BNN)
    decay = jnp.where(incl, jnp.exp(jnp.where(incl, gc - gcr, 0.0)), 0.0)
    e = jnp.exp(gc)
    kb, vb = k * beta, v * beta
    kk = _bdot(kb, k, _BNT)
    if tm is None:
        tm = _tri_inv(jnp.where(strict, kk * decay, 0.0), ri, ci)
    kbe = kb * e
    u = _bdot(tm, vb, _BNN)
    w = _bdot(tm, kbe, _BNN)
    qk = _bdot(q, k, _BNT)
    intra = jnp.where(incl, qk * decay, 0.0)
    gl = gc[:, C - 1:C, :]
    el = jnp.exp(gl)
    r = jnp.exp(gl - gc)
    return dict(lane=lane, incl=incl, inclf=inclf, strict=strict, beta=beta, decay=decay, e=e,
                kb=kb, vb=vb, kk=kk, tm=tm, kbe=kbe, u=u, w=w, qk=qk, intra=intra, el=el, r=r, ones=ones)


def gdn_chunk_fwd(qkv, gates, name):
    B, _, T, _ = qkv.shape
    NC = T // DN_C

    G = DN_G if NC % DN_G == 0 else 1
    GC = G * DN_C

    GS = G * DN_HD

    def body(x_ref, gt_ref, o_ref, st_ref, tm_ref, s_s, p_s, b_s, qp_s, el_s):
        h = pl.program_id(1)

        def group_a(gi, _):
            rows = pl.ds(pl.multiple_of(gi * GC, GC), GC)
            srow = pl.ds(pl.multiple_of(gi * GS, GS), GS)
            q, k, v = [x_ref[0, i, rows, :].reshape(G, DN_C, DN_HD) for i in range(3)]
            c = _chunk_common(q, k, v, gt_ref[0, rows, :].reshape(G, DN_C, 128), h)
            kr = k * c["r"]
            tm_ref[0, 0, rows, :] = c["tm"].reshape(GC, DN_C)
            p_s[srow, :] = _bdot(kr, c["w"], _BTN).reshape(GS, DN_HD)
            b_s[srow, :] = _bdot(kr, c["u"], _BTN).reshape(GS, DN_HD)
            qp_s[rows, :] = (q * c["e"] - _bdot(c["intra"], c["w"], _BNN)).reshape(GC, DN_HD)
            o_ref[0, rows, :] = _bdot(c["intra"], c["u"], _BNN).reshape(GC, DN_HD)
            el_s[pl.ds(gi * G, G), :, :] = c["el"] * jnp.ones((G, 1, 128), F32)
            return 0

        lax.fori_loop(0, NC // G, group_a, 0)
        s_s[...] = jnp.zeros_like(s_s)

        def chunk(n, _):
            srow = pl.ds(pl.multiple_of(n * DN_HD, DN_HD), DN_HD)
            st = s_s[...]
            st_ref[0, 0, srow, :] = st
            s_s[...] = (st * el_s[n] + b_s[srow, :]) - _bdot(p_s[srow, :], st)
            return 0

        lax.fori_loop(0, NC, chunk, 0)

        def group_c(gi, _):
            rows = pl.ds(pl.multiple_of(gi * GC, GC), GC)
            srow = pl.ds(pl.multiple_of(gi * GS, GS), GS)
            st = st_ref[0, 0, srow, :].reshape(G, DN_HD, DN_HD)
            o_ref[0, rows, :] += _bdot(qp_s[rows, :].reshape(G, DN_C, DN_HD), st, _BNN).reshape(GC, DN_HD)
            return 0

        lax.fori_loop(0, NC // G, group_c, 0)

    return pl.pallas_call(
        body, grid=(B, DN_HEADS),
        in_specs=[pl.BlockSpec((1, 3, T, 128), lambda b, h: (b, 0, 0, h)), pl.BlockSpec((1, T, 128), lambda b, h: (b, 0, 0))],
        out_specs=[pl.BlockSpec((1, T, 128), lambda b, h: (b, 0, h)), pl.BlockSpec((1, 1, NC * DN_HD, DN_HD), lambda b, h: (b, h, 0, 0)),
                   pl.BlockSpec((1, 1, T, DN_C), lambda b, h: (b, h, 0, 0))],
        out_shape=[_sds((B, T, DN_W), F32), _sds((B, DN_HEADS, NC * DN_HD, DN_HD), F32), _sds((B, DN_HEADS, T, DN_C), F32)],
        scratch_shapes=[pltpu.VMEM((DN_HD, DN_HD), F32)] + [pltpu.VMEM((NC * DN_HD, DN_HD), F32)] * 2
        + [pltpu.VMEM((T, DN_HD), F32), pltpu.VMEM((NC, 1, 128), F32)],
        compiler_params=_cp(("parallel", "parallel")), name=name,
    )(qkv, gates)


def gdn_chunk_bwd(qkv, gates, states, tms, dob, name):
    B, _, T, _ = qkv.shape
    NC = T // DN_C
    C = DN_C

    G = DN_G if NC % DN_G == 0 else 1
    GC = G * C

    GS = G * DN_HD

    def body(x_ref, gt_ref, st_ref, tm_ref, do_ref, dx_ref, dgt_ref, ds_s, p_s, r_s, el_s, dsa_s):
        h = pl.program_id(1)

        @pl.when(h == 0)
        def _():
            dgt_ref[...] = jnp.zeros_like(dgt_ref)

        def load(gi):
            rows = pl.ds(pl.multiple_of(gi * GC, GC), GC)
            q, k, v = [x_ref[0, i, rows, :].reshape(G, C, DN_HD) for i in range(3)]
            return rows, q, k, v, gt_ref[0, rows, :].reshape(G, C, 128), tm_ref[0, 0, rows, :].reshape(G, C, C)

        def group_a(gi, _):
            rows, q, k, v, gt, tm = load(gi)
            srow = pl.ds(pl.multiple_of(gi * GS, GS), GS)
            c = _chunk_common(q, k, v, gt, h, tm=tm)
            qp = q * c["e"] - _bdot(c["intra"], c["w"], _BNN)
            p_s[srow, :] = _bdot(k * c["r"], c["w"], _BTN).reshape(GS, DN_HD)
            r_s[srow, :] = _bdot(qp, do_ref[0, rows, :].reshape(G, C, DN_HD), _BTN).reshape(GS, DN_HD)
            el_s[pl.ds(gi * G, G), :, :] = c["el"] * jnp.ones((G, 1, 128), F32)
            return 0

        lax.fori_loop(0, NC // G, group_a, 0)
        ds_s[...] = jnp.zeros_like(ds_s)

        def chunk(m, _):
            n = NC - 1 - m
            srow = pl.ds(pl.multiple_of(n * DN_HD, DN_HD), DN_HD)
            dsn = ds_s[...]
            dsa_s[srow, :] = dsn
            ds_s[...] = (dsn * el_s[n] + r_s[srow, :]) - _bdot(p_s[srow, :], dsn, _TN)
            return 0

        lax.fori_loop(0, NC, chunk, 0)

        def group_c(gi, _):
            rows, q, k, v, gt, tm = load(gi)
            c = _chunk_common(q, k, v, gt, h, tm=tm)
            incl, strict, decay, e, r, el, tm = c["incl"], c["strict"], c["decay"], c["e"], c["r"], c["el"], c["tm"]
            srow = pl.ds(pl.multiple_of(gi * GS, GS), GS)
            st = st_ref[0, 0, srow, :].reshape(G, DN_HD, DN_HD)
            dsn = dsa_s[srow, :].reshape(G, DN_HD, DN_HD)
            do = do_ref[0, rows, :].reshape(G, C, DN_HD)
            dvn = _bdot(k * r, dsn, _BNN) + _bdot(c["intra"], do, _BTN)
            v_new = c["u"] - _bdot(c["w"], st, _BNN)
            del_ = jnp.sum(jnp.sum(dsn * st, axis=2, keepdims=True), axis=1, keepdims=True)
            dkr = _bdot(v_new, dsn, _BNT)
            dqe = _bdot(do, st, _BNT)
            dintra = _bdot(do, v_new, _BNT)
            dw = -_bdot(dvn, st, _BNT)
            dqkd = jnp.where(incl, dintra, 0.0)
            dqk = dqkd * decay
            ddecay = dqkd * c["qk"]
            dq = dqe * e + _bdot(dqk, k, _BNN)
            dk = dkr * r + _bdot(dqk, q, _BTN)
            dtm = _bdot(dvn, c["vb"], _BNT) + _bdot(dw, c["kbe"], _BNT)
            dvb = _bdot(tm, dvn, _BTN)
            dkbe = _bdot(tm, dw, _BTN)
            dkb = dkbe * e
            de = jnp.sum(dqe * q, axis=2, keepdims=True) + jnp.sum(dkbe * c["kb"], axis=2, keepdims=True)
            da = -_pdot(tm, _pdot(dtm, tm, _BNT), _BTN)
            dlow = jnp.where(strict, da, 0.0)
            dkk = dlow * decay
            ddecay = ddecay + dlow * c["kk"]
            dkb = dkb + _bdot(dkk, k, _BNN)
            dk = dk + _bdot(dkk, c["kb"], _BTN) + dkb * c["beta"]
            dbeta = jnp.sum(dkb * k, axis=2, keepdims=True) + jnp.sum(dvb * v, axis=2, keepdims=True)
            dv = dvb * c["beta"]
            dd = ddecay * decay
            dgc = jnp.sum(dd, axis=2, keepdims=True) - _pdot(dd, c["ones"], _BTN)[:, :, :1]
            dr = jnp.sum(dkr * k, axis=2, keepdims=True)
            dgc = dgc + de * e - dr * r
            dgl = jnp.sum(dr * r, axis=1, keepdims=True) + del_ * el
            rowc = lax.broadcasted_iota(jnp.int32, (C, 1), 0)
            dgc = dgc + jnp.where(rowc == C - 1, dgl, 0.0)
            dg = _pdot(c["inclf"], dgc * c["ones"], _BTN)[:, :, :1]
            dx_ref[0, 0, rows, :] = dq.reshape(GC, DN_HD)
            dx_ref[0, 1, rows, :] = dk.reshape(GC, DN_HD)
            dx_ref[0, 2, rows, :] = dv.reshape(GC, DN_HD)
            lane = c["lane"]
            dgt_ref[0, rows, :] += (jnp.where(lane == h, dg, 0.0) + jnp.where(lane == h + DN_HEADS, dbeta, 0.0)).reshape(GC, 128)
            return 0

        lax.fori_loop(0, NC // G, group_c, 0)

    return pl.pallas_call(
        body, grid=(B, DN_HEADS),
        in_specs=[pl.BlockSpec((1, 3, T, 128), lambda b, h: (b, 0, 0, h)), pl.BlockSpec((1, T, 128), lambda b, h: (b, 0, 0)),
                  pl.BlockSpec((1, 1, NC * DN_HD, DN_HD), lambda b, h: (b, h, 0, 0)), pl.BlockSpec((1, 1, T, C), lambda b, h: (b, h, 0, 0)),
                  pl.BlockSpec((1, T, 128), lambda b, h: (b, 0, h))],
        out_specs=[pl.BlockSpec((1, 3, T, 128), lambda b, h: (b, 0, 0, h)), pl.BlockSpec((1, T, 128), lambda b, h: (b, 0, 0))],
        out_shape=[_sds((B, 3, T, DN_W), F32), _sds((B, T, 128), F32)],
        scratch_shapes=[pltpu.VMEM((DN_HD, DN_HD), F32)] + [pltpu.VMEM((NC * DN_HD, DN_HD), F32)] * 2
        + [pltpu.VMEM((NC, 1, 128), F32), pltpu.VMEM((NC * DN_HD, DN_HD), F32)],
        compiler_params=_cp(("parallel", "arbitrary")), name=name,
    )(qkv, gates, states, tms, dob)


def ada_fwd(c_all, w_ada, b_sl, name):
    L, D, W = w_ada.shape
    NBt = c_all.shape[0]

    def body(c_ref, w_ref, b_ref, o_ref):
        cv = c_ref[...]
        o_ref[0] = _pdot(cv * _sigmoid(cv), w_ref[0]) + b_ref[0]

    return pl.pallas_call(
        body, grid=(L,),
        in_specs=[pl.BlockSpec((NBt, D), lambda l: (0, 0)), pl.BlockSpec((1, D, W), lambda l: (l, 0, 0)), pl.BlockSpec((1, 1, W), lambda l: (l, 0, 0))],
        out_specs=pl.BlockSpec((1, NBt, W), lambda l: (l, 0, 0)),
        out_shape=_sds((L, NBt, W), F32),
        compiler_params=_cp(("parallel",)), name=name,
    )(c_all, w_ada, b_sl)


def ada_bwd(c_all, dmod_cols, name):
    L, NBt, W = dmod_cols.shape
    D = c_all.shape[1]

    def body(c_ref, d_ref, o_ref):
        cv = c_ref[...]
        o_ref[0] = _pdot(cv * _sigmoid(cv), d_ref[0], _TN)

    return pl.pallas_call(
        body, grid=(L,),
        in_specs=[pl.BlockSpec((NBt, D), lambda l: (0, 0)), pl.BlockSpec((1, NBt, W), lambda l: (l, 0, 0))],
        out_specs=pl.BlockSpec((1, D, W), lambda l: (l, 0, 0)),
        out_shape=_sds((L, D, W), F32),
        compiler_params=_cp(("parallel",)), name=name,
    )(c_all, dmod_cols)


def adamw(partials, w, m, v, name):
    L, R, C = w.shape
    per_layer = isinstance(partials, (list, tuple))
    plist = list(partials) if per_layer else [partials]
    P = plist[0].shape[0]
    tr = _pick(R, 256)

    def body(*refs):
        p_refs = refs[:len(plist)]
        w_ref, m_ref, v_ref, g_ref, d_ref, nm_ref, nv_ref = refs[len(plist):]

        def run(read):
            g = read(0).astype(F32)
            for i in range(1, P):
                g = g + read(i).astype(F32)
            nm = ADAM_B1 * m_ref[0] + (1.0 - ADAM_B1) * g
            nv = ADAM_B2 * v_ref[0] + (1.0 - ADAM_B2) * (g * g)
            m_hat = nm / (1.0 - ADAM_B1 ** ADAM_STEP)
            v_hat = nv / (1.0 - ADAM_B2 ** ADAM_STEP)
            g_ref[0] = g
            d_ref[0] = -ADAM_LR * (m_hat / (jnp.sqrt(v_hat) + ADAM_EPS) + ADAM_WD * w_ref[0])
            nm_ref[0] = nm
            nv_ref[0] = nv

        if per_layer:
            for l in range(L):
                @pl.when(pl.program_id(0) == l)
                def _(l=l):
                    run(lambda i: p_refs[l][i])
        else:
            run(lambda i: p_refs[0][i, 0])

    tile = pl.BlockSpec((1, tr, C), lambda l, i: (l, i, 0))
    if per_layer:
        p_specs = [pl.BlockSpec((P, tr, C), lambda l, i, k=k: (0, jnp.where(l == k, i, 0), 0)) for k in range(L)]
    else:
        p_specs = [pl.BlockSpec((P, 1, tr, C), lambda l, i: (0, l, i, 0))]
    return pl.pallas_call(
        body, grid=(L, R // tr),
        in_specs=p_specs + [tile, tile, tile],
        out_specs=[tile] * 4, out_shape=[_sds((L, R, C), F32)] * 4,
        compiler_params=_cp(("arbitrary", "arbitrary")), name=name,
    )(*plist, w, m, v)


def _coords():
    return lax.axis_index("x"), lax.axis_index("y"), lax.axis_index("c")


def all_gather(x, name):
    return all_gather_many([x], name)[0]


def all_gather_many(xs, name):
    any_spec = pl.BlockSpec(memory_space=pl.ANY)
    n = len(xs)

    def body(*refs):
        x_refs, out_refs = refs[:n], refs[n:2 * n]
        send_sems, recv_sems, local_sems = refs[2 * n:]
        x_, y_, c_ = _coords()
        me, sibling = (x_, y_, c_), (x_, y_, 1 - c_)
        chips = [(1 - x_, y_), (x_, 1 - y_), (1 - x_, 1 - y_)]

        def copy(a, k, block, to, own=False):
            px, py, pc = block
            rows = out_refs[a].at[4 * px + 2 * py + pc]
            return pltpu.make_async_remote_copy(
                src_ref=x_refs[a] if own else rows, dst_ref=rows,
                send_sem=send_sems.at[7 * a + k], recv_sem=recv_sems.at[7 * a + k],
                device_id=to, device_id_type=pl.DeviceIdType.MESH)

        arrays = range(n)
        mine = [pltpu.make_async_copy(x_refs[a], out_refs[a].at[4 * x_ + 2 * y_ + c_], local_sems.at[a]) for a in arrays]
        first = [copy(a, 0, me, sibling, own=True) for a in arrays]
        first += [copy(a, 1 + j, me, (*chip, c_), own=True) for a in arrays for j, chip in enumerate(chips)]
        for cp in mine + first:
            cp.start()
        passed = []
        for a in arrays:
            for j, chip in enumerate(chips):
                copy(a, 1 + j, (*chip, c_), me).wait_recv()
                passed.append(copy(a, 4 + j, (*chip, c_), sibling))
                passed[-1].start()
        for a in arrays:
            copy(a, 0, sibling, me).wait_recv()
            for j, chip in enumerate(chips):
                copy(a, 4 + j, (*chip, 1 - c_), me).wait_recv()
        for cp in first + passed:
            cp.wait_send()
        for cp in mine:
            cp.wait()

    return pl.pallas_call(
        body, out_shape=[_sds((N_DEV,) + x.shape, x.dtype) for x in xs],
        in_specs=[any_spec] * n, out_specs=[any_spec] * n,
        scratch_shapes=[pltpu.SemaphoreType.DMA((7 * n,)), pltpu.SemaphoreType.DMA((7 * n,)), pltpu.SemaphoreType.DMA((n,))],
        name=name,
    )(*xs)


def all_to_all(x, name):
    any_spec = pl.BlockSpec(memory_space=pl.ANY)

    def body(x_ref, out_ref, send_sems, recv_sems, local_sem):
        x_, y_, c_ = _coords()
        me = 4 * x_ + 2 * y_ + c_
        mine = pltpu.make_async_copy(x_ref.at[me], out_ref.at[me], local_sem)
        mine.start()
        copies = []
        for k in range(1, N_DEV):
            px = 1 - x_ if k & 4 else x_
            py = 1 - y_ if k & 2 else y_
            pc = 1 - c_ if k & 1 else c_
            peer = 4 * px + 2 * py + pc
            copies.append((pltpu.make_async_remote_copy(
                src_ref=x_ref.at[peer], dst_ref=out_ref.at[me],
                send_sem=send_sems.at[k - 1], recv_sem=recv_sems.at[k - 1],
                device_id=(px, py, pc), device_id_type=pl.DeviceIdType.MESH), peer))
        for cp, _ in copies:
            cp.start()
        for k, (cp, peer) in enumerate(copies):
            pltpu.make_async_remote_copy(
                src_ref=x_ref.at[peer], dst_ref=out_ref.at[peer],
                send_sem=send_sems.at[k], recv_sem=recv_sems.at[k],
                device_id=(x_, y_, c_), device_id_type=pl.DeviceIdType.MESH).wait_recv()
        for cp, _ in copies:
            cp.wait_send()
        mine.wait()

    return pl.pallas_call(
        body, out_shape=_sds(x.shape, x.dtype),
        in_specs=[any_spec], out_specs=any_spec,
        scratch_shapes=[pltpu.SemaphoreType.DMA((7,)), pltpu.SemaphoreType.DMA((7,)), pltpu.SemaphoreType.DMA],
        name=name,
    )(x)


def _rows128(a):
    return a.reshape(-1, 128)


def _pad_lanes(a):
    return jnp.pad(a, ((0, 0), (0, 128 - a.shape[1])))


def kernel(x, c, w_ada, b_ada, norm_mix, norm_mlp, w_in, sb_q_norm, sb_k_norm, conv_w, a_log, dt_bias, dn_out_norm, w_out, w_ff1, w_ff2, loss_target, m_w_ada, m_b_ada, m_norm_mix, m_norm_mlp, m_w_in, m_sb_q_norm, m_sb_k_norm, m_conv_w, m_a_log, m_dt_bias, m_dn_out_norm, m_w_out, m_w_ff1, m_w_ff2, v_w_ada, v_b_ada, v_norm_mix, v_norm_mlp, v_w_in, v_sb_q_norm, v_sb_k_norm, v_conv_w, v_a_log, v_dt_bias, v_dn_out_norm, v_w_out, v_w_ff1, v_w_ff2):
    B, T, D = x.shape
    L = w_ada.shape[0]
    N = B * T
    FF = w_ff1.shape[2] * N_DEV
    WA = w_ada.shape[2]
    CS = conv_w.shape[2]
    me = 4 * lax.axis_index("x") + 2 * lax.axis_index("y") + lax.axis_index("c")
    tm = _pick(T, 1024)

    wb = [w.astype(BF16) for w in (w_in, w_out, w_ff1, w_ff2)]

    def assemble(lands):
        win_g, wout_g, w1_g, w2_g = lands
        return (jnp.pad(win_g.transpose(1, 0, 2).reshape(D, IN_W), ((0, 0), (0, IN_WP - IN_W))),
                wout_g.reshape(SB_W + DN_W, D), w1_g.transpose(1, 0, 2).reshape(D, FF), w2_g.reshape(FF, D))

    *lands0, conv_g, c_g = all_gather_many([w[0] for w in wb] + [conv_w, c], "comm_gather_first")
    weights = [assemble(lands0)]
    conv_full = conv_g.transpose(1, 2, 0, 3).reshape(L, CONV_K, 3 * DN_W)

    c_all = c_g.reshape(N_DEV * B, D)
    b_sl = lax.dynamic_slice_in_dim(b_ada, me * WA, WA, axis=1).reshape(L, 1, WA)
    mod_sh = ada_fwd(c_all, w_ada, b_sl, "ada_fwd")
    mod_g = all_gather(mod_sh, "comm_gather_mod")
    mod = lax.dynamic_slice_in_dim(mod_g, me * B, B, axis=2).transpose(1, 2, 0, 3).reshape(L, B, 6 * D)

    def mod_part(l, i):
        return mod[l, :, i * D:(i + 1) * D].reshape(B, 1, D)

    alog_row = _pad_lanes(a_log).reshape(L, 1, 128)
    dtb_row = _pad_lanes(dt_bias).reshape(L, 1, 128)

    def gate_epi(acc, xv, g):
        return acc, xv + g[0] * acc

    def relu2(a):
        r = jnp.maximum(a.astype(F32), 0.0)
        return r * r

    def times_gate(a, g):
        return a * g[0]

    tile_ij = lambda i, j, k: (i, j)

    saved = []
    xc = x
    for l in range(L):
        sh_a, sc_a, g_a, sh_m, sc_m, g_m = [mod_part(l, i) for i in range(6)]
        h = ln_mod_fwd(xc, norm_mix[l:l + 1], sc_a, sh_a, "ln_mod_fwd")
        W_in_l, W_out_l, W_1_l, W_2_l = weights[l]
        proj3 = matmul(h.reshape(N, D), W_in_l, mode="nn", name="mm_proj", tm=256)[0].reshape(B, T, IN_WP)
        if l + 1 < L:
            o_a, tot, *lands = sb_attn_fwd(proj3, sb_q_norm[l:l + 1], sb_k_norm[l:l + 1], "sb_attn_fwd_gather",
                                           exch=("gather", [w[l + 1] for w in wb]))
            weights.append(assemble(lands))
        else:
            o_a, tot = sb_attn_fwd(proj3, sb_q_norm[l:l + 1], sb_k_norm[l:l + 1], "sb_attn_fwd")
        qkv = gdn_pre_fwd(proj3, conv_full[l], "gdn_pre_fwd")
        gates = gdn_gates_fwd(proj3, alog_row[l], dtb_row[l], "gdn_gates_fwd")
        ob, states, tms = gdn_chunk_fwd(qkv, gates, "gdn_chunk_fwd")
        o_b = gdn_post_fwd(ob, proj3, dn_out_norm[l:l + 1], "gdn_post_fwd")
        mix = jnp.concatenate([o_a, o_b], axis=-1)
        y1, x_mid = matmul(
            mix.reshape(N, SB_W + DN_W), W_out_l, mode="nn", name="mm_out", out_dtypes=(BF16, F32), tm=tm, epi=gate_epi,
            extras=[(xc.reshape(N, D), (tm, _pick(D, 1024)), tile_ij),
                    (g_a, (1, 1, _pick(D, 1024)), lambda i, j, k: (i * tm // T, 0, j))])
        x_mid = x_mid.reshape(B, T, D)
        h2 = ln_mod_fwd(x_mid, norm_mlp[l:l + 1], sc_m, sh_m, "ln_mod_fwd")
        u = matmul(h2.reshape(N, D), W_1_l, mode="nn", name="mm_ff1", out_dtypes=(BF16,))[0]
        y2, x_out = matmul(
            u, W_2_l, mode="nn", name="mm_ff2", out_dtypes=(BF16, F32), tm=tm, a_fn=relu2, epi=gate_epi,
            extras=[(x_mid.reshape(N, D), (tm, _pick(D, 1024)), tile_ij),
                    (g_m, (1, 1, _pick(D, 1024)), lambda i, j, k: (i * tm // T, 0, j))])
        saved.append(dict(x=xc, h=h, proj3=proj3, tot=tot, qkv=qkv, gates=gates, states=states, tms=tms, ob=ob, mix=mix,
                          y1=y1, x_mid=x_mid, h2=h2, u=u, y2=y2))
        xc = x_out.reshape(B, T, D)

    dx, sq = loss_grad(xc, loss_target, "loss_grad")
    loss = lax.psum((0.5 / D) * jnp.sum(sq), AXES)

    g_win, g_wout, g_w1, g_w2, dmods, smalls, parts = [], [], [], [], [], [], []
    pending = None
    tk_tok = tm
    wi = w_in.shape[2]

    def shard_layer(gin, gout, g1, g2):
        return [None if gin is None else gin[:, :IN_W].reshape(D, N_DEV, wi).transpose(1, 0, 2), gout.reshape(N_DEV, w_out.shape[1], D),
                g1.reshape(D, N_DEV, w_ff1.shape[2]).transpose(1, 0, 2), g2.reshape(N_DEV, w_ff2.shape[1], D)]

    for l in reversed(range(L)):
        s = saved[l]
        W_in_l, W_out_l, W_1_l, W_2_l = weights[l]
        sh_a, sc_a, g_a, sh_m, sc_m, g_m = [mod_part(l, i) for i in range(6)]
        gate_k = lambda g, blk: (g, (1, 1, blk), lambda i, j, k: (i * tm // T, 0, k))
        gate_tok = lambda g, blk: (g, (1, 1, blk), lambda i, j, k: (k * tk_tok // T, 0, j))
        dx2 = dx.reshape(N, D)
        dg_m = rowsum_prod(dx, s["y2"].reshape(B, T, D), "rowsum_prod")
        du = matmul(dx2, W_2_l, mode="nt", name="mm_ff2_da", out_dtypes=(BF16,), tm=tm, a_fn=times_gate,
                    a_extras=[gate_k(g_m, _pick(D, 1024))],
                    epi=lambda acc, uv: (acc * (2.0 * jnp.maximum(uv, 0.0)),),
                    extras=[(s["u"], (tm, _pick(FF, 1024)), tile_ij)])[0]
        g_w2.append(matmul(s["u"], dx2, mode="tn", name="mm_ff2_dw", out_dtypes=(BF16,), tk=tk_tok, a_fn=relu2,
                           b_fn=times_gate, b_extras=[gate_tok(g_m, _pick(D, 1024))])[0])
        g_w1.append(matmul(s["h2"].reshape(N, D), du, mode="tn", name="mm_ff1_dw", out_dtypes=(BF16,))[0])
        dh2 = matmul(du, W_1_l, mode="nt", name="mm_ff1_da")[0]
        dx_mid, dgn_mlp, dsc_m, dsh_m = ln_mod_bwd(s["x_mid"], norm_mlp[l:l + 1], sc_m, dh2.reshape(B, T, D), dx, "ln_mod_bwd")
        dxm2 = dx_mid.reshape(N, D)
        dg_a = rowsum_prod(dx_mid, s["y1"].reshape(B, T, D), "rowsum_prod")
        dmix3 = matmul(dxm2, W_out_l, mode="nt", name="mm_out_da", tm=tm, a_fn=times_gate,
                       a_extras=[gate_k(g_a, _pick(D, 1024))])[0].reshape(B, T, SB_W + DN_W)
        g_wout.append(matmul(s["mix"].reshape(N, SB_W + DN_W), dxm2, mode="tn", name="mm_out_dw", out_dtypes=(BF16,),
                             tk=tk_tok, b_fn=times_gate, b_extras=[gate_tok(g_a, _pick(D, 1024))])[0])
        srcs = (pending or []) + (shard_layer(None, g_wout[-1], g_w1[-1], g_w2[-1])[1:] if l == 0 else [])
        if srcs:
            dq_a, dk_a, dv_a, dgq, dgk, *lands = sb_attn_bwd(
                s["proj3"], sb_q_norm[l:l + 1], sb_k_norm[l:l + 1], s["tot"], dmix3,
                "sb_attn_bwd_scatter" if l else "sb_attn_bwd_scatter0", exch=("scatter", srcs))
            if pending:
                parts.append(lands[:4])
            lands0 = lands[-3:]
        else:
            dq_a, dk_a, dv_a, dgq, dgk = sb_attn_bwd(s["proj3"], sb_q_norm[l:l + 1], sb_k_norm[l:l + 1], s["tot"], dmix3, "sb_attn_bwd")
        dob, dz, dgn_dn = gdn_post_bwd(s["ob"], s["proj3"], dn_out_norm[l:l + 1], dmix3, "gdn_post_bwd")
        dqkv, dgates = gdn_chunk_bwd(s["qkv"], s["gates"], s["states"], s["tms"], dob, "gdn_chunk_bwd")
        d_dnqkv, dconv_b = gdn_pre_bwd(s["proj3"], conv_full[l], dqkv, "gdn_pre_bwd")
        d_ab, dalog, ddtb = gdn_gates_bwd(s["proj3"], alog_row[l], dtb_row[l], dgates, "gdn_gates_bwd")
        dproj = [a.reshape(N, a.shape[-1]) for a in (dq_a, dk_a, dv_a, d_dnqkv, dz, d_ab)]
        g_win.append(proj_bwd_weight(s["h"].reshape(N, D), dproj, "mm_proj_dw"))
        dh = proj_bwd_input(dproj, W_in_l, "mm_proj_da")
        dx, dgn_mix, dsc_a, dsh_a = ln_mod_bwd(s["x"], norm_mix[l:l + 1], sc_a, dh.reshape(B, T, D), dx_mid, "ln_mod_bwd")
        pending = shard_layer(g_win[-1], g_wout[-1], g_w1[-1], g_w2[-1])
        dmods.append(jnp.concatenate([dsh_a, dsc_a, dg_a, dsh_m, dsc_m, dg_m], axis=-1).reshape(B, 6 * D))
        smalls.append(dict(norm_mix=dgn_mix, norm_mlp=dgn_mlp, sbq=dgq, sbk=dgk, alog=dalog, dtb=ddtb, dnorm=dgn_dn,
                           conv=jnp.sum(dconv_b, axis=0)))
    parts.append([all_to_all(pending[0], "comm_scatter_w_in")] + lands0)
    for lst in (dmods, smalls, parts):
        lst.reverse()
    grad_x = dx

    def update2d(parts, w, m, v, name):
        return [o[0] for o in adamw(parts[:, None], w[None], m[None], v[None], name)]

    p_win, p_wout, p_w1, p_w2 = [[parts[l][i] for l in range(L)] for i in range(4)]
    r_win = adamw(p_win, w_in, m_w_in, v_w_in, "adamw_w_in")
    r_wout = adamw(p_wout, w_out, m_w_out, v_w_out, "adamw_w_out")
    r_w1 = adamw(p_w1, w_ff1, m_w_ff1, v_w_ff1, "adamw_w_ff1")
    r_w2 = adamw(p_w2, w_ff2, m_w_ff2, v_w_ff2, "adamw_w_ff2")

    def pack(f):
        return jnp.concatenate([
            _rows128(f("norm_mix")), _rows128(f("norm_mlp")), _rows128(f("sbq")), _rows128(f("sbk")),
            f("alog"), f("dtb"), f("dnorm"), _rows128(f("conv"))], axis=0)

    part = pack(lambda n: jnp.concatenate([sm[n] for sm in smalls], axis=0))
    dmod_g, part_g = all_gather_many([jnp.stack(dmods), part], "comm_gather_last")

    dmod_all = dmod_g.transpose(1, 0, 2, 3).reshape(L, N_DEV * B, 6 * D)
    g_wada = ada_bwd(c_all, lax.dynamic_slice_in_dim(dmod_all, me * WA, WA, axis=2), "ada_bwd")
    r_wada = adamw(g_wada[None], w_ada, m_w_ada, v_w_ada, "adamw_w_ada")
    r_bada = update2d(dmod_g.transpose(0, 2, 1, 3).reshape(N_DEV * B, L, 6 * D), b_ada, m_b_ada, v_b_ada, "adamw_b_ada")

    names = ["norm_mix", "norm_mlp", "sbq", "sbk", "alog", "dtb", "dnorm"]
    n_rep = part.shape[0] - L * CONV_K * 3 * DN_W // 128
    params = dict(norm_mix=(norm_mix, m_norm_mix, v_norm_mix), norm_mlp=(norm_mlp, m_norm_mlp, v_norm_mlp),
                  sbq=(sb_q_norm, m_sb_q_norm, v_sb_q_norm), sbk=(sb_k_norm, m_sb_k_norm, v_sb_k_norm),
                  alog=(a_log, m_a_log, v_a_log), dtb=(dt_bias, m_dt_bias, v_dt_bias),
                  dnorm=(dn_out_norm, m_dn_out_norm, v_dn_out_norm))

    def rows_of(n, a):
        return _pad_lanes(a) if n in ("alog", "dtb") else _rows128(a)

    packed = [jnp.concatenate([rows_of(n, params[n][i]) for n in names], axis=0) for i in range(3)]
    r_small = update2d(part_g[:, :n_rep], packed[0], packed[1], packed[2], "adamw_small")
    small_out = {}
    off = 0
    for n in names:
        w0 = params[n][0]
        nr = rows_of(n, w0).shape[0]
        vals = [o[off:off + nr] for o in r_small]
        small_out[n] = [(vv[:, :w0.shape[1]] if n in ("alog", "dtb") else vv.reshape(w0.shape)) for vv in vals]
        off += nr
    conv_parts = part_g[:, n_rep:].reshape(N_DEV, L, CONV_K, 3 * DN_W)
    r_conv = adamw(lax.dynamic_slice_in_dim(conv_parts, me * CS, CS, axis=3), conv_w, m_conv_w, v_conv_w, "adamw_conv")

    order = [r_wada, r_bada, small_out["norm_mix"], small_out["norm_mlp"], r_win, small_out["sbq"], small_out["sbk"],
             r_conv, small_out["alog"], small_out["dtb"], small_out["dnorm"], r_wout, r_w1, r_w2]
    outs = [loss, grad_x]
    for i in range(4):
        outs += [r[i] for r in order]
    return tuple(outs)
```

```python
import functools
import math

import jax
import jax.numpy as jnp
from jax import lax
from jax.experimental import pallas as pl
from jax.experimental.pallas import tpu as pltpu

F32 = jnp.float32
BF16 = jnp.bfloat16
EPS = 1e-6
N_DEV = 8
AXES = ("x", "y", "c")

SB_HEADS, SB_HD = 8, 64
SB_W = SB_HEADS * SB_HD
SB_BLK = 128
DN_HEADS, DN_HD = 4, 128
DN_W = DN_HEADS * DN_HD
DN_C = 64
CONV_K = 4
IN_W = 3 * SB_W + 4 * DN_W + 2 * DN_HEADS
IN_WP = 3 * SB_W + 4 * DN_W + 128
COL_DNQKV = 3 * SB_W // 128
COL_Z = COL_DNQKV + 3 * DN_W // 128
COL_AB = COL_Z + DN_W // 128

ADAM_LR, ADAM_B1, ADAM_B2, ADAM_EPS, ADAM_WD, ADAM_STEP = 0.001, 0.9, 0.999, 1e-08, 0.01, 10

VMEM_LIMIT = 56 * 1024 * 1024


def _cp(sem):
    return pltpu.CompilerParams(dimension_semantics=sem, vmem_limit_bytes=VMEM_LIMIT)


def _pick(dim, pref):
    return pref if dim % pref == 0 else dim


def _sds(shape, dtype):
    return jax.ShapeDtypeStruct(tuple(shape), dtype)


_NN = (((1,), (0,)), ((), ()))
_NT = (((1,), (1,)), ((), ()))
_TN = (((0,), (0,)), ((), ()))


def _bdot(a, b, dims=_NN):
    return lax.dot_general(a.astype(BF16), b.astype(BF16), dims, preferred_element_type=F32)


def _split(a):
    hi = a.astype(BF16)
    lo = (a - hi.astype(F32)).astype(BF16)
    return hi, lo


def _pdot(a, b, dims=_NN):
    ah, al = _split(a)
    bh, bl = _split(b)
    d = functools.partial(lax.dot_general, dimension_numbers=dims, preferred_element_type=F32)
    return d(ah, bh) + (d(ah, bl) + d(al, bh))


def _sigmoid(x):
    return 1.0 / (1.0 + jnp.exp(-x))


def _softplus(x):
    return jnp.maximum(x, 0.0) + jnp.log(1.0 + jnp.exp(-jnp.abs(x)))


def matmul(a, b, *, mode, name, out_dtypes=(F32,), a_fn=None, a_extras=(), b_fn=None, b_extras=(),
           epi=None, extras=(), tm=1024, tn=1024, tk=1024):
    if mode == "tn":
        K, M = a.shape
    else:
        M, K = a.shape
    N = b.shape[0] if mode == "nt" else b.shape[1]
    tm, tn, tk = _pick(M, tm), _pick(N, tn), _pick(K, tk)
    nk = K // tk
    dims = {"nn": _NN, "nt": _NT, "tn": _TN}[mode]
    a_spec = pl.BlockSpec((tk, tm), lambda i, j, k: (k, i)) if mode == "tn" else pl.BlockSpec((tm, tk), lambda i, j, k: (i, k))
    b_spec = pl.BlockSpec((tn, tk), lambda i, j, k: (j, k)) if mode == "nt" else pl.BlockSpec((tk, tn), lambda i, j, k: (k, j))
    na, nb, ne, no = len(a_extras), len(b_extras), len(extras), len(out_dtypes)

    def body(*refs):
        a_ref, b_ref = refs[0], refs[1]
        ax = refs[2:2 + na]
        bx = refs[2 + na:2 + na + nb]
        ex = refs[2 + na + nb:2 + na + nb + ne]
        outs = refs[2 + na + nb + ne:2 + na + nb + ne + no]
        acc_ref = refs[-1]
        k = pl.program_id(2)

        @pl.when(k == 0)
        def _():
            acc_ref[...] = jnp.zeros_like(acc_ref)

        av = a_ref[...]
        if a_fn is not None:
            av = a_fn(av, *[r[...] for r in ax])
        bv = b_ref[...]
        if b_fn is not None:
            bv = b_fn(bv, *[r[...] for r in bx])
        acc_ref[...] += lax.dot_general(av.astype(BF16), bv.astype(BF16), dims, preferred_element_type=F32)

        @pl.when(k == nk - 1)
        def _():
            res = acc_ref[...]
            res = (res,) if epi is None else epi(res, *[r[...] for r in ex])
            for o_ref, r in zip(outs, res):
                o_ref[...] = r.astype(o_ref.dtype)

    xs = list(a_extras) + list(b_extras) + list(extras)
    return pl.pallas_call(
        body,
        grid=(M // tm, N // tn, nk),
        in_specs=[a_spec, b_spec] + [pl.BlockSpec(bs, im) for _, bs, im in xs],
        out_specs=[pl.BlockSpec((tm, tn), lambda i, j, k: (i, j)) for _ in out_dtypes],
        out_shape=[_sds((M, N), dt) for dt in out_dtypes],
        scratch_shapes=[pltpu.VMEM((tm, tn), F32)],
        compiler_params=_cp(("parallel", "parallel", "arbitrary")),
        name=name,
    )(a, b, *[x for x, _, _ in xs])


def proj_bwd_input(pieces, w, name, tm=512, tn=1024):
    N, D = pieces[0].shape[0], w.shape[0]
    widths = [p.shape[1] for p in pieces]
    offs = [sum(widths[:i]) for i in range(len(widths))]
    tm, tn = _pick(N, tm), _pick(D, tn)

    def body(*refs):
        w_ref, o_ref = refs[len(pieces)], refs[len(pieces) + 1]
        acc = None
        for p_ref, off, wd in zip(refs, offs, widths):
            t = lax.dot_general(p_ref[...].astype(BF16), w_ref[:, off:off + wd], _NT, preferred_element_type=F32)
            acc = t if acc is None else acc + t
        o_ref[...] = acc

    return pl.pallas_call(
        body, grid=(N // tm, D // tn),
        in_specs=[pl.BlockSpec((tm, wd), lambda i, j: (i, 0)) for wd in widths] + [pl.BlockSpec((tn, sum(widths)), lambda i, j: (j, 0))],
        out_specs=pl.BlockSpec((tm, tn), lambda i, j: (i, j)), out_shape=_sds((N, D), F32),
        compiler_params=_cp(("parallel", "parallel")), name=name,
    )(*pieces, w)


def proj_bwd_weight(h, pieces, name, tm=512, tk=512):
    N, D = h.shape
    widths = [p.shape[1] for p in pieces]
    offs = [sum(widths[:i]) for i in range(len(widths))]
    tm, tk = _pick(D, tm), _pick(N, tk)
    nk = N // tk

    def body(*refs):
        h_ref, o_ref, acc_ref = refs[0], refs[len(pieces) + 1], refs[len(pieces) + 2]
        k = pl.program_id(1)

        @pl.when(k == 0)
        def _():
            acc_ref[...] = jnp.zeros_like(acc_ref)

        hv = h_ref[...]
        for p_ref, off, wd in zip(refs[1:], offs, widths):
            acc_ref[:, off:off + wd] += lax.dot_general(hv, p_ref[...].astype(BF16), _TN, preferred_element_type=F32)

        @pl.when(k == nk - 1)
        def _():
            o_ref[...] = acc_ref[...].astype(o_ref.dtype)

    return pl.pallas_call(
        body, grid=(D // tm, nk),
        in_specs=[pl.BlockSpec((tk, tm), lambda i, k: (k, i))] + [pl.BlockSpec((tk, wd), lambda i, k: (k, 0)) for wd in widths],
        out_specs=pl.BlockSpec((tm, sum(widths)), lambda i, k: (i, 0)), out_shape=_sds((D, sum(widths)), BF16),
        scratch_shapes=[pltpu.VMEM((tm, sum(widths)), F32)],
        compiler_params=_cp(("parallel", "arbitrary")), name=name,
    )(h, *pieces)


def ln_mod_fwd(x, gain, sc, sh, name):
    B, T, D = x.shape
    tt = _pick(T, 512)

    def body(x_ref, g_ref, sc_ref, sh_ref, h_ref):
        xv = x_ref[0]
        r = lax.rsqrt(jnp.mean(xv * xv, axis=-1, keepdims=True) + EPS)
        h = (xv * r * g_ref[...]) * (1.0 + sc_ref[0]) + sh_ref[0]
        h_ref[0] = h.astype(h_ref.dtype)

    return pl.pallas_call(
        body, grid=(B, T // tt),
        in_specs=[pl.BlockSpec((1, tt, D), lambda b, t: (b, t, 0)), pl.BlockSpec((1, D), lambda b, t: (0, 0)),
                  pl.BlockSpec((1, 1, D), lambda b, t: (b, 0, 0)), pl.BlockSpec((1, 1, D), lambda b, t: (b, 0, 0))],
        out_specs=pl.BlockSpec((1, tt, D), lambda b, t: (b, t, 0)),
        out_shape=_sds((B, T, D), BF16),
        compiler_params=_cp(("parallel", "parallel")), name=name,
    )(x, gain, sc, sh)


def ln_mod_bwd(x, gain, sc, dh, dres, name):
    B, T, D = x.shape
    tt = _pick(T, 512)

    def body(x_ref, g_ref, sc_ref, dh_ref, dres_ref, dx_ref, dg_ref, dsc_ref, dsh_ref):
        b, t = pl.program_id(0), pl.program_id(1)
        xv, dhv = x_ref[0], dh_ref[0]
        g, s = g_ref[...], sc_ref[0]
        r = lax.rsqrt(jnp.mean(xv * xv, axis=-1, keepdims=True) + EPS)
        xn = xv * r
        dxn = dhv * (g * (1.0 + s))
        dx_ref[0] = dres_ref[0] + r * (dxn - xn * jnp.mean(dxn * xn, axis=-1, keepdims=True))
        s1 = jnp.sum(dhv * xn, axis=0, keepdims=True)
        s2 = jnp.sum(dhv, axis=0, keepdims=True)

        @pl.when(t == 0)
        def _():
            dsc_ref[0] = jnp.zeros_like(s1)
            dsh_ref[0] = jnp.zeros_like(s1)

        @pl.when((t == 0) & (b == 0))
        def _():
            dg_ref[...] = jnp.zeros_like(s1)

        dsc_ref[0] += s1 * g
        dsh_ref[0] += s2
        dg_ref[...] += s1 * (1.0 + s)

    tile = pl.BlockSpec((1, tt, D), lambda b, t: (b, t, 0))
    row = pl.BlockSpec((1, D), lambda b, t: (0, 0))
    brow = pl.BlockSpec((1, 1, D), lambda b, t: (b, 0, 0))
    return pl.pallas_call(
        body, grid=(B, T // tt),
        in_specs=[tile, row, brow, tile, tile],
        out_specs=[tile, row, brow, brow],
        out_shape=[_sds((B, T, D), F32), _sds((1, D), F32), _sds((B, 1, D), F32), _sds((B, 1, D), F32)],
        compiler_params=_cp(("arbitrary", "arbitrary")), name=name,
    )(x, gain, sc, dh, dres)


def rowsum_prod(a, b, name):
    B, T, D = a.shape
    tt = _pick(T, 512)

    def body(a_ref, b_ref, o_ref):
        @pl.when(pl.program_id(1) == 0)
        def _():
            o_ref[...] = jnp.zeros_like(o_ref)

        o_ref[0] += jnp.sum(a_ref[0] * b_ref[0], axis=0, keepdims=True)

    tile = pl.BlockSpec((1, tt, D), lambda b, t: (b, t, 0))
    return pl.pallas_call(
        body, grid=(B, T // tt), in_specs=[tile, tile],
        out_specs=pl.BlockSpec((1, 1, D), lambda b, t: (b, 0, 0)),
        out_shape=_sds((B, 1, D), F32),
        compiler_params=_cp(("parallel", "arbitrary")), name=name,
    )(a, b)


def loss_grad(y, tgt, name):
    B, T, D = y.shape
    tt = _pick(T, 512)

    def body(y_ref, t_ref, dy_ref, s_ref):
        @pl.when((pl.program_id(0) == 0) & (pl.program_id(1) == 0))
        def _():
            s_ref[...] = jnp.zeros_like(s_ref)

        e = y_ref[0] - t_ref[0]
        dy_ref[0] = e * (1.0 / D)
        s_ref[...] += jnp.sum(e * e, axis=0, keepdims=True)

    tile = pl.BlockSpec((1, tt, D), lambda b, t: (b, t, 0))
    return pl.pallas_call(
        body, grid=(B, T // tt), in_specs=[tile, tile],
        out_specs=[tile, pl.BlockSpec((1, D), lambda b, t: (0, 0))],
        out_shape=[_sds((B, T, D), F32), _sds((1, D), F32)],
        compiler_params=_cp(("arbitrary", "arbitrary")), name=name,
    )(y, tgt)


def _sb_group(nb):
    return 4 if nb % 4 == 0 else (2 if nb % 2 == 0 else 1)


def _sb_qrows(t, kw):
    return 256 if (t % 256 == 0 and kw % 256 == 0) else SB_BLK


def _diag_step(sblock, sj, carry, thr, qb, kw):
    half = kw // 2
    if half % SB_BLK or half < qb:
        return sblock(sj, carry, True)
    return lax.cond(thr + qb <= half, lambda c: sblock(sj, c, True, half), lambda c: sblock(sj, c, True), carry)


def _tri_sum(x, tri):
    return lax.dot_general(x.astype(BF16), tri, _NN, preferred_element_type=F32)


def _tri2(cond):
    return cond.astype(BF16)


def sb_attn_fwd(proj3, gq, gk, name, exch=None):
    B, T, _ = proj3.shape
    NB = T // SB_BLK
    G = _sb_group(NB)
    KW = G * SB_BLK
    QB = _sb_qrows(T, KW)
    scale = SB_HD ** -0.5

    def body(q_ref, k_ref, v_ref, gq_ref, gk_ref, o_ref, tot_ref, qn_s, kn_s, v_s):
        row_io = lax.broadcasted_iota(jnp.int32, (SB_BLK, SB_BLK), 0)
        col_io = lax.broadcasted_iota(jnp.int32, (SB_BLK, SB_BLK), 1)
        tri = _tri2(row_io > col_io)

        def prep(i, _):
            rows = pl.ds(pl.multiple_of(i * SB_BLK, SB_BLK), SB_BLK)
            for hh in range(2):
                sl = slice(hh * SB_HD, (hh + 1) * SB_HD)
                q = q_ref[0, rows, sl]
                k = k_ref[0, rows, sl]
                qn_s[hh, rows, :] = (q * lax.rsqrt(jnp.mean(q * q, -1, keepdims=True) + EPS) * (gq_ref[...] * scale)).astype(BF16)
                kn_s[hh, rows, :] = (k * lax.rsqrt(jnp.mean(k * k, -1, keepdims=True) + EPS) * gk_ref[...]).astype(BF16)
                v_s[hh, rows, :] = v_ref[0, rows, sl].astype(BF16)
            return 0

        lax.fori_loop(0, NB, prep, 0)

        diff_w = (lax.broadcasted_iota(jnp.int32, (QB, KW), 1)
                  - lax.broadcasted_iota(jnp.int32, (QB, KW), 0))

        def qblock(i, _):
            rows = pl.ds(pl.multiple_of(i * QB, QB), QB)
            qn = [qn_s[hh, rows, :] for hh in range(2)]
            nsj = ((i + 1) * QB - 1) // KW + 1

            thr = i * QB - (nsj - 1) * KW

            def sblock(sj, carry, masked, width=KW):
                g = width // SB_BLK
                cols = pl.ds(pl.multiple_of(sj * KW, KW), width)
                mask = diff_w[:, :width] < thr
                zs = [lax.dot_general(qn[hh], kn_s[hh, cols, :], _NT, preferred_element_type=F32) for hh in range(2)]
                lgs, lss = [], []
                for hh in range(2):
                    sp = _softplus(zs[hh])
                    lgs.append(jnp.where(mask, -sp, 0.0) if masked else -sp)
                    lss.append(zs[hh] - sp)
                blocks = [lgs[hh][:, s * SB_BLK:(s + 1) * SB_BLK] for hh in range(2) for s in range(g)]
                ts_all = _tri_sum(jnp.concatenate(blocks, axis=0), tri)
                atts, css = [], []
                for hh in range(2):
                    cs = carry[hh][1]
                    ps = []
                    for s in reversed(range(g)):
                        n = hh * g + s
                        ts = ts_all[n * QB:(n + 1) * QB]
                        ps.append(lss[hh][:, s * SB_BLK:(s + 1) * SB_BLK] + ts + cs)
                        cs = cs + (ts[:, :1] + blocks[n][:, :1])
                    p = ps[0] if g == 1 else jnp.concatenate(ps[::-1], axis=1)
                    att = jnp.exp(p)
                    atts.append((jnp.where(mask, att, 0.0) if masked else att).astype(BF16))
                    css.append(cs)
                return tuple((carry[hh][0] + lax.dot_general(atts[hh], v_s[hh, cols, :], _NN, preferred_element_type=F32), css[hh])
                             for hh in range(2))

            init = (jnp.zeros((QB, SB_HD), F32), jnp.zeros((QB, 1), F32))
            res = _diag_step(sblock, nsj - 1, (init, init), thr, QB, KW)
            res = lax.fori_loop(0, nsj - 1, lambda jj, c: sblock(nsj - 2 - jj, c, False), res)
            for hh in range(2):
                o_ref[0, rows, hh * SB_HD:(hh + 1) * SB_HD] = res[hh][0].astype(o_ref.dtype)
                tot_ref[0, hh, rows, :] = res[hh][1]
            return 0

        lax.fori_loop(0, T // QB, qblock, 0)

    blk = lambda off: pl.BlockSpec((1, T, 128), lambda b, p: (b, 0, off + p))
    grow = pl.BlockSpec((1, SB_HD), lambda b, p: (0, 0))
    return _hosted_call(
        body, grid=(B, SB_W // 128),
        in_specs=[blk(0), blk(SB_W // 128), blk(2 * SB_W // 128), grow, grow],
        out_specs=[pl.BlockSpec((1, T, 128), lambda b, p: (b, 0, p)), pl.BlockSpec((1, 2, T, 1), lambda b, p: (b, p, 0, 0))],
        out_shape=[_sds((B, T, SB_W), BF16), _sds((B, SB_HEADS, T, 1), F32)],
        scratch_shapes=[pltpu.VMEM((2, T, SB_HD), BF16)] * 3,
        name=name, args=(proj3, proj3, proj3, gq, gk), exch=exch)


def sb_attn_bwd(proj3, gq, gk, tot, dmix3, name, exch=None):
    B, T, _ = proj3.shape
    NB = T // SB_BLK
    G = _sb_group(NB)
    KW = G * SB_BLK
    QB = _sb_qrows(T, KW)
    scale = SB_HD ** -0.5

    def body(q_ref, k_ref, v_ref, gq_ref, gk_ref, tot_ref, do_ref, dq_ref, dk_ref, dv_ref, dgq_ref, dgk_ref,
             qn_s, kn_s, v_s, do_s, dqn_s, dkn_s, dv_s):
        row_io = lax.broadcasted_iota(jnp.int32, (SB_BLK, SB_BLK), 0)
        col_io = lax.broadcasted_iota(jnp.int32, (SB_BLK, SB_BLK), 1)
        tri = _tri2(row_io > col_io)
        trip = _tri2(row_io < col_io)

        @pl.when((pl.program_id(0) == 0) & (pl.program_id(1) == 0))
        def _():
            dgq_ref[...] = jnp.zeros_like(dgq_ref)
            dgk_ref[...] = jnp.zeros_like(dgk_ref)

        def prep(i, _):
            rows = pl.ds(pl.multiple_of(i * SB_BLK, SB_BLK), SB_BLK)
            for hh in range(2):
                sl = slice(hh * SB_HD, (hh + 1) * SB_HD)
                q = q_ref[0, rows, sl]
                k = k_ref[0, rows, sl]
                qn_s[hh, rows, :] = (q * lax.rsqrt(jnp.mean(q * q, -1, keepdims=True) + EPS) * (gq_ref[...] * scale)).astype(BF16)
                kn_s[hh, rows, :] = (k * lax.rsqrt(jnp.mean(k * k, -1, keepdims=True) + EPS) * gk_ref[...]).astype(BF16)
                v_s[hh, rows, :] = v_ref[0, rows, sl].astype(BF16)
                do_s[hh, rows, :] = do_ref[0, rows, sl].astype(BF16)
            return 0

        lax.fori_loop(0, NB, prep, 0)
        dkn_s[...] = jnp.zeros_like(dkn_s)
        dv_s[...] = jnp.zeros_like(dv_s)

        diff_w = (lax.broadcasted_iota(jnp.int32, (QB, KW), 1)
                  - lax.broadcasted_iota(jnp.int32, (QB, KW), 0))

        def qblock(i, _):
            rows = pl.ds(pl.multiple_of(i * QB, QB), QB)
            qn = [qn_s[hh, rows, :] for hh in range(2)]
            dov = [do_s[hh, rows, :] for hh in range(2)]
            tot = [tot_ref[0, hh, rows, :] for hh in range(2)]
            nsj = ((i + 1) * QB - 1) // KW + 1
            qnT = [qn[hh].astype(F32).T.astype(BF16) for hh in range(2)]
            doT = [dov[hh].astype(F32).T.astype(BF16) for hh in range(2)]

            thr = i * QB - (nsj - 1) * KW

            def sblock(sj, carry, masked, width=KW):
                g = width // SB_BLK
                cols = pl.ds(pl.multiple_of(sj * KW, KW), width)
                mask = diff_w[:, :width] < thr
                hs = range(2)
                kns = [kn_s[hh, cols, :] for hh in hs]
                zs = [lax.dot_general(qn[hh], kns[hh], _NT, preferred_element_type=F32) for hh in hs]
                datts = [lax.dot_general(dov[hh], v_s[hh, cols, :], _NT, preferred_element_type=F32) for hh in hs]
                lgs, lss = [], []
                for hh in hs:
                    sp = _softplus(zs[hh])
                    lgs.append(jnp.where(mask, -sp, 0.0) if masked else -sp)
                    lss.append(zs[hh] - sp)
                blocks = [lgs[hh][:, s * SB_BLK:(s + 1) * SB_BLK] for hh in hs for s in range(g)]
                ts_all = _tri_sum(jnp.concatenate(blocks, axis=0), tri)
                atts, dps, cums = [], [], []
                for hh in hs:
                    cum = carry[hh][1]
                    ps = []
                    for s in range(g):
                        n = hh * g + s
                        ts = ts_all[n * QB:(n + 1) * QB]
                        cum = cum + (ts[:, :1] + blocks[n][:, :1])
                        ps.append(lss[hh][:, s * SB_BLK:(s + 1) * SB_BLK] + ts + (tot[hh] - cum))
                    p = ps[0] if g == 1 else jnp.concatenate(ps, axis=1)
                    att = jnp.exp(p)
                    att = jnp.where(mask, att, 0.0) if masked else att
                    atts.append(att.astype(BF16))
                    dps.append(att * datts[hh])
                    cums.append(cum)
                dblocks = [dps[hh][:, s * SB_BLK:(s + 1) * SB_BLK] for hh in hs for s in range(g)]
                tp_all = _tri_sum(jnp.concatenate(dblocks, axis=0), trip)
                dzs, cdps = [], []
                for hh in hs:
                    cdp = carry[hh][2]
                    dls = []
                    for s in range(g):
                        n = hh * g + s
                        tp = tp_all[n * QB:(n + 1) * QB]
                        dls.append(tp + cdp)
                        cdp = cdp + (tp[:, SB_BLK - 1:] + dblocks[n][:, SB_BLK - 1:])
                    dlg = dls[0] if g == 1 else jnp.concatenate(dls, axis=1)
                    dz = dps[hh] - jnp.exp(lss[hh]) * (dps[hh] + dlg)
                    dzs.append((jnp.where(mask, dz, 0.0) if masked else dz).astype(BF16))
                    cdps.append(cdp)
                new = []
                for hh in hs:
                    dq = carry[hh][0] + lax.dot_general(dzs[hh], kns[hh], _NN, preferred_element_type=F32)
                    dkn_s[hh, :, cols] += lax.dot_general(qnT[hh], dzs[hh], _NN, preferred_element_type=F32)
                    dv_s[hh, :, cols] += lax.dot_general(doT[hh], atts[hh], _NN, preferred_element_type=F32)
                    new.append((dq, cums[hh], cdps[hh]))
                return tuple(new)

            z1 = jnp.zeros((QB, 1), F32)
            init = (jnp.zeros((QB, SB_HD), F32), z1, z1)
            res = lax.fori_loop(0, nsj - 1, lambda sj, c: sblock(sj, c, False), (init, init))
            res = _diag_step(sblock, nsj - 1, res, thr, QB, KW)
            for hh in range(2):
                dqn_s[hh, rows, :] = res[hh][0]
            return 0

        lax.fori_loop(0, T // QB, qblock, 0)

        def fin(i, carry):
            aq, ak = carry
            rows = pl.ds(pl.multiple_of(i * SB_BLK, SB_BLK), SB_BLK)
            for hh in range(2):
                sl = slice(hh * SB_HD, (hh + 1) * SB_HD)
                for src_ref, g_ref, out_ref, mult, which in ((q_ref, gq_ref, dq_ref, scale, 0), (k_ref, gk_ref, dk_ref, 1.0, 1)):
                    xr = src_ref[0, rows, sl]
                    r = lax.rsqrt(jnp.mean(xr * xr, -1, keepdims=True) + EPS)
                    dy = (dqn_s[hh, rows, :] if which == 0 else dkn_s[hh, :, rows].T) * mult
                    u = dy * g_ref[...]
                    out_ref[0, rows, sl] = r * u - xr * (r * r * r) * jnp.mean(u * xr, -1, keepdims=True)
                    part = jnp.sum(dy * xr * r, axis=0, keepdims=True)
                    if which == 0:
                        aq = aq + part
                    else:
                        ak = ak + part
                dv_ref[0, rows, sl] = dv_s[hh, :, rows].T
            return aq, ak

        z64 = jnp.zeros((1, SB_HD), F32)
        aq, ak = lax.fori_loop(0, NB, fin, (z64, z64))
        dgq_ref[...] += aq
        dgk_ref[...] += ak

    blk = lambda off: pl.BlockSpec((1, T, 128), lambda b, p: (b, 0, off + p))
    grow = pl.BlockSpec((1, SB_HD), lambda b, p: (0, 0))
    return _hosted_call(
        body, grid=(B, SB_W // 128),
        in_specs=[blk(0), blk(SB_W // 128), blk(2 * SB_W // 128), grow, grow,
                  pl.BlockSpec((1, 2, T, 1), lambda b, p: (b, p, 0, 0)), blk(0)],
        out_specs=[blk(0), blk(0), blk(0), grow, grow],
        out_shape=[_sds((B, T, SB_W), F32)] * 3 + [_sds((1, SB_HD), F32)] * 2,
        scratch_shapes=[pltpu.VMEM((2, T, SB_HD), BF16)] * 4 + [pltpu.VMEM((2, T, SB_HD), F32)] + [pltpu.VMEM((2, SB_HD, T), F32)] * 2,
        name=name, args=(proj3, proj3, proj3, gq, gk, tot, dmix3), exch=exch)


def _exch_copies(kind, src_refs, land_refs, sems, with_arrivals=True):
    x_, y_, c_ = _coords()
    me = 4 * x_ + 2 * y_ + c_
    local, go, arrive = [], [], []
    for a, (src, land) in enumerate(zip(src_refs, land_refs)):
        send, recv, loc = sems[3 * a:3 * a + 3]
        local.append(pltpu.make_async_copy(src if kind == "gather" else src.at[me], land.at[me], loc))
        for k in range(1, N_DEV):
            px = 1 - x_ if k & 4 else x_
            py = 1 - y_ if k & 2 else y_
            pc = 1 - c_ if k & 1 else c_
            peer = 4 * px + 2 * py + pc
            out = src if kind == "gather" else src.at[peer]
            mk = functools.partial(pltpu.make_async_remote_copy, send_sem=send.at[k - 1], recv_sem=recv.at[k - 1],
                                   device_id=(px, py, pc), device_id_type=pl.DeviceIdType.MESH)
            go.append(mk(src_ref=out, dst_ref=land.at[me]))
            if with_arrivals:
                arrive.append(mk(src_ref=out, dst_ref=land.at[peer]))
    return local, go, arrive


def _hosted_call(body, *, grid, in_specs, out_specs, out_shape, scratch_shapes, name, args, exch=None):
    sem = ("arbitrary",) * len(grid)
    if exch is None:
        return pl.pallas_call(body, grid=grid, in_specs=in_specs, out_specs=out_specs, out_shape=out_shape,
                              scratch_shapes=scratch_shapes, compiler_params=_cp(sem), name=name)(*args)
    kind, srcs = exch
    ns, n_in, n_out, n_scr = len(srcs), len(in_specs), len(out_specs), len(scratch_shapes)
    lands = [_sds((N_DEV,) + s.shape if kind == "gather" else s.shape, s.dtype) for s in srcs]

    def wrapped(*refs):
        ins, src_refs = refs[:n_in], refs[n_in:n_in + ns]
        outs, land_refs = refs[n_in + ns:n_in + ns + n_out], refs[n_in + ns + n_out:n_in + 2 * ns + n_out]
        scr, sems = refs[n_in + 2 * ns + n_out:n_in + 2 * ns + n_out + n_scr], refs[n_in + 2 * ns + n_out + n_scr:]
        ids = [pl.program_id(d) for d in range(len(grid))]
        first = functools.reduce(lambda a, b: a & b, [i == 0 for i in ids])
        last = functools.reduce(lambda a, b: a & b, [i == g - 1 for i, g in zip(ids, grid)])

        @pl.when(first)
        def _():
            local, go, _ = _exch_copies(kind, src_refs, land_refs, sems, with_arrivals=False)
            for cp in local + go:
                cp.start()

        body(*ins, *outs, *scr)

        @pl.when(last)
        def _():
            local, go, arrive = _exch_copies(kind, src_refs, land_refs, sems)
            for cp in arrive:
                cp.wait_recv()
            for cp in go:
                cp.wait_send()
            for cp in local:
                cp.wait()

    any_spec = pl.BlockSpec(memory_space=pl.ANY)
    sems = [pltpu.SemaphoreType.DMA((N_DEV - 1,)), pltpu.SemaphoreType.DMA((N_DEV - 1,)), pltpu.SemaphoreType.DMA] * ns
    return pl.pallas_call(
        wrapped, grid=grid, in_specs=list(in_specs) + [any_spec] * ns, out_specs=list(out_specs) + [any_spec] * ns,
        out_shape=list(out_shape) + lands, scratch_shapes=list(scratch_shapes) + sems,
        compiler_params=_cp(sem), name=name)(*args, *srcs)


def _conv_silu(x, w, T):
    t_io = lax.broadcasted_iota(jnp.int32, x.shape, 0)
    xs = [x] + [jnp.where(t_io >= s, pltpu.roll(x, s, 0), 0.0) for s in range(1, CONV_K)]
    y = xs[0] * w[CONV_K - 1:CONV_K, :]
    for s in range(1, CONV_K):
        y = y + xs[s] * w[CONV_K - 1 - s:CONV_K - s, :]
    return y, y * _sigmoid(y), xs


def gdn_pre_fwd(proj3, conv_w, name):
    B, T, _ = proj3.shape
    qs = DN_HD ** -0.5

    def body(x_ref, w_ref, o_ref):
        kind = pl.program_id(1) // DN_HEADS
        _, s, _ = _conv_silu(x_ref[0], w_ref[...], T)
        n = lax.rsqrt(jnp.sum(s * s, axis=-1, keepdims=True) + EPS)
        c = jnp.where(kind == 0, qs, 1.0)
        o_ref[0, 0] = jnp.where(kind < 2, s * (n * c), s)

    return pl.pallas_call(
        body, grid=(B, 3 * DN_HEADS),
        in_specs=[pl.BlockSpec((1, T, 128), lambda b, j: (b, 0, COL_DNQKV + j)), pl.BlockSpec((CONV_K, 128), lambda b, j: (0, j))],
        out_specs=pl.BlockSpec((1, 1, T, 128), lambda b, j: (b, j // DN_HEADS, 0, j % DN_HEADS)),
        out_shape=_sds((B, 3, T, DN_W), F32),
        compiler_params=_cp(("parallel", "parallel")), name=name,
    )(proj3, conv_w)


def gdn_pre_bwd(proj3, conv_w, dqkv, name):
    B, T, _ = proj3.shape
    qs = DN_HD ** -0.5

    def body(x_ref, w_ref, d_ref, dx_ref, dw_ref):
        kind = pl.program_id(1) // DN_HEADS
        w = w_ref[...]
        y, s, xs = _conv_silu(x_ref[0], w, T)
        dout = d_ref[0, 0]
        n = lax.rsqrt(jnp.sum(s * s, axis=-1, keepdims=True) + EPS)
        c = jnp.where(kind == 0, qs, 1.0)
        dsn = c * (n * dout - s * (n * n * n) * jnp.sum(dout * s, axis=-1, keepdims=True))
        ds = jnp.where(kind < 2, dsn, dout)
        sg = _sigmoid(y)
        dy = ds * (sg * (1.0 + y * (1.0 - sg)))
        t_io = lax.broadcasted_iota(jnp.int32, dy.shape, 0)
        dx = dy * w[CONV_K - 1:CONV_K, :]
        dw_ref[0, CONV_K - 1:CONV_K, :] = jnp.sum(dy * xs[0], axis=0, keepdims=True)
        for sft in range(1, CONV_K):
            dx = dx + jnp.where(t_io < T - sft, pltpu.roll(dy, T - sft, 0), 0.0) * w[CONV_K - 1 - sft:CONV_K - sft, :]
            dw_ref[0, CONV_K - 1 - sft:CONV_K - sft, :] = jnp.sum(dy * xs[sft], axis=0, keepdims=True)
        dx_ref[0] = dx

    return pl.pallas_call(
        body, grid=(B, 3 * DN_HEADS),
        in_specs=[pl.BlockSpec((1, T, 128), lambda b, j: (b, 0, COL_DNQKV + j)), pl.BlockSpec((CONV_K, 128), lambda b, j: (0, j)),
                  pl.BlockSpec((1, 1, T, 128), lambda b, j: (b, j // DN_HEADS, 0, j % DN_HEADS))],
        out_specs=[pl.BlockSpec((1, T, 128), lambda b, j: (b, 0, j)), pl.BlockSpec((1, CONV_K, 128), lambda b, j: (b, 0, j))],
        out_shape=[_sds((B, T, 3 * DN_W), F32), _sds((B, CONV_K, 3 * DN_W), F32)],
        compiler_params=_cp(("parallel", "parallel")), name=name,
    )(proj3, conv_w, dqkv)


def gdn_gates_fwd(proj3, alog_row, dtb_row, name):
    B, T, _ = proj3.shape

    def body(x_ref, al_ref, dt_ref, o_ref):
        x = x_ref[0]
        lane = lax.broadcasted_iota(jnp.int32, x.shape, 1)
        g = -jnp.exp(al_ref[...]) * _softplus(x + dt_ref[...])
        o_ref[0] = jnp.where(lane < DN_HEADS, g, jnp.where(lane < 2 * DN_HEADS, _sigmoid(x), 0.0))

    row = pl.BlockSpec((1, 128), lambda b: (0, 0))
    return pl.pallas_call(
        body, grid=(B,),
        in_specs=[pl.BlockSpec((1, T, 128), lambda b: (b, 0, COL_AB)), row, row],
        out_specs=pl.BlockSpec((1, T, 128), lambda b: (b, 0, 0)),
        out_shape=_sds((B, T, 128), F32),
        compiler_params=_cp(("parallel",)), name=name,
    )(proj3, alog_row, dtb_row)


def gdn_gates_bwd(proj3, alog_row, dtb_row, dgates, name):
    B, T, _ = proj3.shape

    def body(x_ref, al_ref, dt_ref, d_ref, dx_ref, dal_ref, ddt_ref):
        @pl.when(pl.program_id(0) == 0)
        def _():
            dal_ref[...] = jnp.zeros_like(dal_ref)
            ddt_ref[...] = jnp.zeros_like(ddt_ref)

        x, d = x_ref[0], d_ref[0]
        lane = lax.broadcasted_iota(jnp.int32, x.shape, 1)
        a = x + dt_ref[...]
        na = -jnp.exp(al_ref[...])
        da = jnp.where(lane < DN_HEADS, d * na * _sigmoid(a), 0.0)
        bt = _sigmoid(x)
        dx_ref[0] = da + jnp.where((lane >= DN_HEADS) & (lane < 2 * DN_HEADS), d * bt * (1.0 - bt), 0.0)
        dal_ref[...] += jnp.sum(jnp.where(lane < DN_HEADS, d * na * _softplus(a), 0.0), axis=0, keepdims=True)
        ddt_ref[...] += jnp.sum(da, axis=0, keepdims=True)

    row = pl.BlockSpec((1, 128), lambda b: (0, 0))
    tile = pl.BlockSpec((1, T, 128), lambda b: (b, 0, 0))
    return pl.pallas_call(
        body, grid=(B,),
        in_specs=[pl.BlockSpec((1, T, 128), lambda b: (b, 0, COL_AB)), row, row, tile],
        out_specs=[tile, row, row],
        out_shape=[_sds((B, T, 128), F32), _sds((1, 128), F32), _sds((1, 128), F32)],
        compiler_params=_cp(("arbitrary",)), name=name,
    )(proj3, alog_row, dtb_row, dgates)


def gdn_post_fwd(ob, proj3, gain, name):
    B, T, _ = ob.shape

    def body(o_ref, z_ref, g_ref, out_ref):
        o, z = o_ref[0], z_ref[0]
        r = lax.rsqrt(jnp.mean(o * o, axis=-1, keepdims=True) + EPS)
        out_ref[0] = ((o * r * g_ref[...]) * (z * _sigmoid(z))).astype(out_ref.dtype)

    tile = pl.BlockSpec((1, T, 128), lambda b, h: (b, 0, h))
    return pl.pallas_call(
        body, grid=(B, DN_HEADS),
        in_specs=[tile, pl.BlockSpec((1, T, 128), lambda b, h: (b, 0, COL_Z + h)), pl.BlockSpec((1, 128), lambda b, h: (0, 0))],
        out_specs=tile, out_shape=_sds((B, T, DN_W), BF16),
        compiler_params=_cp(("parallel", "parallel")), name=name,
    )(ob, proj3, gain)


def gdn_post_bwd(ob, proj3, gain, dmix3, name):
    B, T, _ = ob.shape

    def body(o_ref, z_ref, g_ref, d_ref, do_ref, dz_ref, dg_ref):
        @pl.when((pl.program_id(0) == 0) & (pl.program_id(1) == 0))
        def _():
            dg_ref[...] = jnp.zeros_like(dg_ref)

        o, z, d, g = o_ref[0], z_ref[0], d_ref[0], g_ref[...]
        r = lax.rsqrt(jnp.mean(o * o, axis=-1, keepdims=True) + EPS)
        sg = _sigmoid(z)
        dn = d * (z * sg)
        dz_ref[0] = d * (o * r * g) * (sg * (1.0 + z * (1.0 - sg)))
        dg_ref[...] += jnp.sum(dn * o * r, axis=0, keepdims=True)
        u = dn * g
        do_ref[0] = r * u - o * (r * r * r) * jnp.mean(u * o, axis=-1, keepdims=True)

    tile = pl.BlockSpec((1, T, 128), lambda b, h: (b, 0, h))
    row = pl.BlockSpec((1, 128), lambda b, h: (0, 0))
    return pl.pallas_call(
        body, grid=(B, DN_HEADS),
        in_specs=[tile, pl.BlockSpec((1, T, 128), lambda b, h: (b, 0, COL_Z + h)), row,
                  pl.BlockSpec((1, T, 128), lambda b, h: (b, 0, SB_W // 128 + h))],
        out_specs=[tile, tile, row],
        out_shape=[_sds((B, T, DN_W), F32), _sds((B, T, DN_W), F32), _sds((1, 128), F32)],
        compiler_params=_cp(("arbitrary", "arbitrary")), name=name,
    )(ob, proj3, gain, dmix3)


def _tri_inv(low, ri, ci):
    m = (ri == ci).astype(F32) - jnp.where(((ri >> 1) == (ci >> 1)) & (ri > ci), low, 0.0)
    s = 2
    while s < DN_C:
        sh = s.bit_length()
        off = ((ri >> sh) == (ci >> sh)) & ((ri & (2 * s - 1)) >= s) & ((ci & (2 * s - 1)) < s)
        m = m - _pdot(m, _pdot(jnp.where(off, low, 0.0), m, _BNN), _BNN)
        s *= 2
    return m


_BNN = (((2,), (1,)), ((0,), (0,)))
_BNT = (((2,), (2,)), ((0,), (0,)))
_BTN = (((1,), (1,)), ((0,), (0,)))
DN_G = 16


def _chunk_common(q, k, v, gt, h, tm=None):
    C = DN_C
    G = q.shape[0]
    ri = lax.broadcasted_iota(jnp.int32, (C, C), 0)
    ci = lax.broadcasted_iota(jnp.int32, (C, C), 1)
    lane = lax.broadcasted_iota(jnp.int32, (C, 128), 1)
    incl, strict = ri >= ci, ri > ci
    g = jnp.sum(jnp.where(lane == h, gt, 0.0), axis=2, keepdims=True)
    beta = jnp.sum(jnp.where(lane == h + DN_HEADS, gt, 0.0), axis=2, keepdims=True)
    ones = jnp.ones((G, C, 128), F32)
    inclf = jnp.broadcast_to(incl.astype(F32), (G, C, C))
    gc = _pdot(inclf, g * ones, _BNN)[:, :, :1]
    gcr = _pdot(jnp.ones((G, C, C), F32), jnp.where(ri == ci, gc, 0.0), _BNN)
    decay = jnp.where(incl, jnp.exp(jnp.where(incl, gc - gcr, 0.0)), 0.0)
    e = jnp.exp(gc)
    kb, vb = k * beta, v * beta
    kk = _bdot(kb, k, _BNT)
    if tm is None:
        tm = _tri_inv(jnp.where(strict, kk * decay, 0.0), ri, ci)
    kbe = kb * e
    u = _bdot(tm, vb, _BNN)
    w = _bdot(tm, kbe, _BNN)
    qk = _bdot(q, k, _BNT)
    intra = jnp.where(incl, qk * decay, 0.0)
    gl = gc[:, C - 1:C, :]
    el = jnp.exp(gl)
    r = jnp.exp(gl - gc)
    return dict(lane=lane, incl=incl, inclf=inclf, strict=strict, beta=beta, decay=decay, e=e,
                kb=kb, vb=vb, kk=kk, tm=tm, kbe=kbe, u=u, w=w, qk=qk, intra=intra, el=el, r=r, ones=ones)


def gdn_chunk_fwd(qkv, gates, name, exch=None):
    B, _, T, _ = qkv.shape
    NC = T // DN_C

    G = DN_G if NC % DN_G == 0 else 1
    GC = G * DN_C

    GS = G * DN_HD

    def body(x_ref, gt_ref, o_ref, st_ref, tm_ref, s_s, p_s, b_s, qp_s, el_s):
        h = pl.program_id(1)

        def group_a(gi, _):
            rows = pl.ds(pl.multiple_of(gi * GC, GC), GC)
            srow = pl.ds(pl.multiple_of(gi * GS, GS), GS)
            q, k, v = [x_ref[0, i, rows, :].reshape(G, DN_C, DN_HD) for i in range(3)]
            c = _chunk_common(q, k, v, gt_ref[0, rows, :].reshape(G, DN_C, 128), h)
            kr = k * c["r"]
            tm_ref[0, 0, rows, :] = c["tm"].reshape(GC, DN_C)
            p_s[srow, :] = _bdot(kr, c["w"], _BTN).reshape(GS, DN_HD)
            b_s[srow, :] = _bdot(kr, c["u"], _BTN).reshape(GS, DN_HD)
            qp_s[rows, :] = (q * c["e"] - _bdot(c["intra"], c["w"], _BNN)).reshape(GC, DN_HD)
            o_ref[0, rows, :] = _bdot(c["intra"], c["u"], _BNN).reshape(GC, DN_HD)
            el_s[pl.ds(gi * G, G), :, :] = c["el"] * jnp.ones((G, 1, 128), F32)
            return 0

        lax.fori_loop(0, NC // G, group_a, 0)
        s_s[...] = jnp.zeros_like(s_s)

        def chunk(n, _):
            srow = pl.ds(pl.multiple_of(n * DN_HD, DN_HD), DN_HD)
            st = s_s[...]
            st_ref[0, 0, srow, :] = st
            s_s[...] = (st * el_s[n] + b_s[srow, :]) - _bdot(p_s[srow, :], st)
            return 0

        lax.fori_loop(0, NC, chunk, 0)

        def group_c(gi, _):
            rows = pl.ds(pl.multiple_of(gi * GC, GC), GC)
            srow = pl.ds(pl.multiple_of(gi * GS, GS), GS)
            st = st_ref[0, 0, srow, :].reshape(G, DN_HD, DN_HD)
            o_ref[0, rows, :] += _bdot(qp_s[rows, :].reshape(G, DN_C, DN_HD), st, _BNN).reshape(GC, DN_HD)
            return 0

        lax.fori_loop(0, NC // G, group_c, 0)

    return _hosted_call(
        body, grid=(B, DN_HEADS),
        in_specs=[pl.BlockSpec((1, 3, T, 128), lambda b, h: (b, 0, 0, h)), pl.BlockSpec((1, T, 128), lambda b, h: (b, 0, 0))],
        out_specs=[pl.BlockSpec((1, T, 128), lambda b, h: (b, 0, h)), pl.BlockSpec((1, 1, NC * DN_HD, DN_HD), lambda b, h: (b, h, 0, 0)),
                   pl.BlockSpec((1, 1, T, DN_C), lambda b, h: (b, h, 0, 0))],
        out_shape=[_sds((B, T, DN_W), F32), _sds((B, DN_HEADS, NC * DN_HD, DN_HD), F32), _sds((B, DN_HEADS, T, DN_C), F32)],
        scratch_shapes=[pltpu.VMEM((DN_HD, DN_HD), F32)] + [pltpu.VMEM((NC * DN_HD, DN_HD), F32)] * 2
        + [pltpu.VMEM((T, DN_HD), F32), pltpu.VMEM((NC, 1, 128), F32)],
        name=name, args=(qkv, gates), exch=exch)


def gdn_chunk_bwd(qkv, gates, states, tms, dob, name):
    B, _, T, _ = qkv.shape
    NC = T // DN_C
    C = DN_C

    G = DN_G if NC % DN_G == 0 else 1
    GC = G * C

    GS = G * DN_HD

    def body(x_ref, gt_ref, st_ref, tm_ref, do_ref, dx_ref, dgt_ref, ds_s, p_s, r_s, el_s, dsa_s):
        h = pl.program_id(1)

        @pl.when(h == 0)
        def _():
            dgt_ref[...] = jnp.zeros_like(dgt_ref)

        def load(gi):
            rows = pl.ds(pl.multiple_of(gi * GC, GC), GC)
            q, k, v = [x_ref[0, i, rows, :].reshape(G, C, DN_HD) for i in range(3)]
            return rows, q, k, v, gt_ref[0, rows, :].reshape(G, C, 128), tm_ref[0, 0, rows, :].reshape(G, C, C)

        def group_a(gi, _):
            rows, q, k, v, gt, tm = load(gi)
            srow = pl.ds(pl.multiple_of(gi * GS, GS), GS)
            c = _chunk_common(q, k, v, gt, h, tm=tm)
            qp = q * c["e"] - _bdot(c["intra"], c["w"], _BNN)
            p_s[srow, :] = _bdot(k * c["r"], c["w"], _BTN).reshape(GS, DN_HD)
            r_s[srow, :] = _bdot(qp, do_ref[0, rows, :].reshape(G, C, DN_HD), _BTN).reshape(GS, DN_HD)
            el_s[pl.ds(gi * G, G), :, :] = c["el"] * jnp.ones((G, 1, 128), F32)
            return 0

        lax.fori_loop(0, NC // G, group_a, 0)
        ds_s[...] = jnp.zeros_like(ds_s)

        def chunk(m, _):
            n = NC - 1 - m
            srow = pl.ds(pl.multiple_of(n * DN_HD, DN_HD), DN_HD)
            dsn = ds_s[...]
            dsa_s[srow, :] = dsn
            ds_s[...] = (dsn * el_s[n] + r_s[srow, :]) - _bdot(p_s[srow, :], dsn, _TN)
            return 0

        lax.fori_loop(0, NC, chunk, 0)

        def group_c(gi, _):
            rows, q, k, v, gt, tm = load(gi)
            c = _chunk_common(q, k, v, gt, h, tm=tm)
            incl, strict, decay, e, r, el, tm = c["incl"], c["strict"], c["decay"], c["e"], c["r"], c["el"], c["tm"]
            srow = pl.ds(pl.multiple_of(gi * GS, GS), GS)
            st = st_ref[0, 0, srow, :].reshape(G, DN_HD, DN_HD)
            dsn = dsa_s[srow, :].reshape(G, DN_HD, DN_HD)
            do = do_ref[0, rows, :].reshape(G, C, DN_HD)
            dvn = _bdot(k * r, dsn, _BNN) + _bdot(c["intra"], do, _BTN)
            v_new = c["u"] - _bdot(c["w"], st, _BNN)
            del_ = jnp.sum(jnp.sum(dsn * st, axis=2, keepdims=True), axis=1, keepdims=True)
            dkr = _bdot(v_new, dsn, _BNT)
            dqe = _bdot(do, st, _BNT)
            dintra = _bdot(do, v_new, _BNT)
            dw = -_bdot(dvn, st, _BNT)
            dqkd = jnp.where(incl, dintra, 0.0)
            dqk = dqkd * decay
            ddecay = dqkd * c["qk"]
            dq = dqe * e + _bdot(dqk, k, _BNN)
            dk = dkr * r + _bdot(dqk, q, _BTN)
            dtm = _bdot(dvn, c["vb"], _BNT) + _bdot(dw, c["kbe"], _BNT)
            dvb = _bdot(tm, dvn, _BTN)
            dkbe = _bdot(tm, dw, _BTN)
            dkb = dkbe * e
            de = jnp.sum(dqe * q, axis=2, keepdims=True) + jnp.sum(dkbe * c["kb"], axis=2, keepdims=True)
            da = -_pdot(tm, _pdot(dtm, tm, _BNT), _BTN)
            dlow = jnp.where(strict, da, 0.0)
            dkk = dlow * decay
            ddecay = ddecay + dlow * c["kk"]
            dkb = dkb + _bdot(dkk, k, _BNN)
            dk = dk + _bdot(dkk, c["kb"], _BTN) + dkb * c["beta"]
            dbeta = jnp.sum(dkb * k, axis=2, keepdims=True) + jnp.sum(dvb * v, axis=2, keepdims=True)
            dv = dvb * c["beta"]
            dd = ddecay * decay
            dgc = jnp.sum(dd, axis=2, keepdims=True) - _pdot(dd, c["ones"], _BTN)[:, :, :1]
            dr = jnp.sum(dkr * k, axis=2, keepdims=True)
            dgc = dgc + de * e - dr * r
            dgl = jnp.sum(dr * r, axis=1, keepdims=True) + del_ * el
            rowc = lax.broadcasted_iota(jnp.int32, (C, 1), 0)
            dgc = dgc + jnp.where(rowc == C - 1, dgl, 0.0)
            dg = _pdot(c["inclf"], dgc * c["ones"], _BTN)[:, :, :1]
            dx_ref[0, 0, rows, :] = dq.reshape(GC, DN_HD)
            dx_ref[0, 1, rows, :] = dk.reshape(GC, DN_HD)
            dx_ref[0, 2, rows, :] = dv.reshape(GC, DN_HD)
            lane = c["lane"]
            dgt_ref[0, rows, :] += (jnp.where(lane == h, dg, 0.0) + jnp.where(lane == h + DN_HEADS, dbeta, 0.0)).reshape(GC, 128)
            return 0

        lax.fori_loop(0, NC // G, group_c, 0)

    return pl.pallas_call(
        body, grid=(B, DN_HEADS),
        in_specs=[pl.BlockSpec((1, 3, T, 128), lambda b, h: (b, 0, 0, h)), pl.BlockSpec((1, T, 128), lambda b, h: (b, 0, 0)),
                  pl.BlockSpec((1, 1, NC * DN_HD, DN_HD), lambda b, h: (b, h, 0, 0)), pl.BlockSpec((1, 1, T, C), lambda b, h: (b, h, 0, 0)),
                  pl.BlockSpec((1, T, 128), lambda b, h: (b, 0, h))],
        out_specs=[pl.BlockSpec((1, 3, T, 128), lambda b, h: (b, 0, 0, h)), pl.BlockSpec((1, T, 128), lambda b, h: (b, 0, 0))],
        out_shape=[_sds((B, 3, T, DN_W), F32), _sds((B, T, 128), F32)],
        scratch_shapes=[pltpu.VMEM((DN_HD, DN_HD), F32)] + [pltpu.VMEM((NC * DN_HD, DN_HD), F32)] * 2
        + [pltpu.VMEM((NC, 1, 128), F32), pltpu.VMEM((NC * DN_HD, DN_HD), F32)],
        compiler_params=_cp(("parallel", "arbitrary")), name=name,
    )(qkv, gates, states, tms, dob)


def ada_fwd(c_all, w_ada, b_sl, name):
    L, D, W = w_ada.shape
    NBt = c_all.shape[0]

    def body(c_ref, w_ref, b_ref, o_ref):
        cv = c_ref[...]
        o_ref[0] = _pdot(cv * _sigmoid(cv), w_ref[0]) + b_ref[0]

    return pl.pallas_call(
        body, grid=(L,),
        in_specs=[pl.BlockSpec((NBt, D), lambda l: (0, 0)), pl.BlockSpec((1, D, W), lambda l: (l, 0, 0)), pl.BlockSpec((1, 1, W), lambda l: (l, 0, 0))],
        out_specs=pl.BlockSpec((1, NBt, W), lambda l: (l, 0, 0)),
        out_shape=_sds((L, NBt, W), F32),
        compiler_params=_cp(("parallel",)), name=name,
    )(c_all, w_ada, b_sl)


def ada_bwd(c_all, dmod_cols, name):
    L, NBt, W = dmod_cols.shape
    D = c_all.shape[1]

    def body(c_ref, d_ref, o_ref):
        cv = c_ref[...]
        o_ref[0] = _pdot(cv * _sigmoid(cv), d_ref[0], _TN)

    return pl.pallas_call(
        body, grid=(L,),
        in_specs=[pl.BlockSpec((NBt, D), lambda l: (0, 0)), pl.BlockSpec((1, NBt, W), lambda l: (l, 0, 0))],
        out_specs=pl.BlockSpec((1, D, W), lambda l: (l, 0, 0)),
        out_shape=_sds((L, D, W), F32),
        compiler_params=_cp(("parallel",)), name=name,
    )(c_all, dmod_cols)


def adamw(partials, w, m, v, name):
    L, R, C = w.shape
    per_layer = isinstance(partials, (list, tuple))
    plist = list(partials) if per_layer else [partials]
    P = plist[0].shape[0]
    tr = _pick(R, 256)

    def body(*refs):
        p_refs = refs[:len(plist)]
        w_ref, m_ref, v_ref, g_ref, d_ref, nm_ref, nv_ref = refs[len(plist):]

        def run(read):
            g = read(0).astype(F32)
            for i in range(1, P):
                g = g + read(i).astype(F32)
            nm = ADAM_B1 * m_ref[0] + (1.0 - ADAM_B1) * g
            nv = ADAM_B2 * v_ref[0] + (1.0 - ADAM_B2) * (g * g)
            m_hat = nm / (1.0 - ADAM_B1 ** ADAM_STEP)
            v_hat = nv / (1.0 - ADAM_B2 ** ADAM_STEP)
            g_ref[0] = g
            d_ref[0] = -ADAM_LR * (m_hat / (jnp.sqrt(v_hat) + ADAM_EPS) + ADAM_WD * w_ref[0])
            nm_ref[0] = nm
            nv_ref[0] = nv

        if per_layer:
            for l in range(L):
                @pl.when(pl.program_id(0) == l)
                def _(l=l):
                    run(lambda i: p_refs[l][i])
        else:
            run(lambda i: p_refs[0][i, 0])

    tile = pl.BlockSpec((1, tr, C), lambda l, i: (l, i, 0))
    if per_layer:
        p_specs = [pl.BlockSpec((P, tr, C), lambda l, i, k=k: (0, jnp.where(l == k, i, 0), 0)) for k in range(L)]
    else:
        p_specs = [pl.BlockSpec((P, 1, tr, C), lambda l, i: (0, l, i, 0))]
    return pl.pallas_call(
        body, grid=(L, R // tr),
        in_specs=p_specs + [tile, tile, tile],
        out_specs=[tile] * 4, out_shape=[_sds((L, R, C), F32)] * 4,
        compiler_params=_cp(("arbitrary", "arbitrary")), name=name,
    )(*plist, w, m, v)


def _coords():
    return lax.axis_index("x"), lax.axis_index("y"), lax.axis_index("c")


def all_gather(x, name):
    return all_gather_many([x], name)[0]


def all_gather_many(xs, name):
    any_spec = pl.BlockSpec(memory_space=pl.ANY)
    n = len(xs)

    def body(*refs):
        x_refs, out_refs = refs[:n], refs[n:2 * n]
        send_sems, recv_sems, local_sems = refs[2 * n:]
        x_, y_, c_ = _coords()
        me, sibling = (x_, y_, c_), (x_, y_, 1 - c_)
        chips = [(1 - x_, y_), (x_, 1 - y_), (1 - x_, 1 - y_)]

        def copy(a, k, block, to, own=False):
            px, py, pc = block
            rows = out_refs[a].at[4 * px + 2 * py + pc]
            return pltpu.make_async_remote_copy(
                src_ref=x_refs[a] if own else rows, dst_ref=rows,
                send_sem=send_sems.at[7 * a + k], recv_sem=recv_sems.at[7 * a + k],
                device_id=to, device_id_type=pl.DeviceIdType.MESH)

        arrays = range(n)
        mine = [pltpu.make_async_copy(x_refs[a], out_refs[a].at[4 * x_ + 2 * y_ + c_], local_sems.at[a]) for a in arrays]
        first = [copy(a, 0, me, sibling, own=True) for a in arrays]
        first += [copy(a, 1 + j, me, (*chip, c_), own=True) for a in arrays for j, chip in enumerate(chips)]
        for cp in mine + first:
            cp.start()
        passed = []
        for a in arrays:
            for j, chip in enumerate(chips):
                copy(a, 1 + j, (*chip, c_), me).wait_recv()
                passed.append(copy(a, 4 + j, (*chip, c_), sibling))
                passed[-1].start()
        for a in arrays:
            copy(a, 0, sibling, me).wait_recv()
            for j, chip in enumerate(chips):
                copy(a, 4 + j, (*chip, 1 - c_), me).wait_recv()
        for cp in first + passed:
            cp.wait_send()
        for cp in mine:
            cp.wait()

    return pl.pallas_call(
        body, out_shape=[_sds((N_DEV,) + x.shape, x.dtype) for x in xs],
        in_specs=[any_spec] * n, out_specs=[any_spec] * n,
        scratch_shapes=[pltpu.SemaphoreType.DMA((7 * n,)), pltpu.SemaphoreType.DMA((7 * n,)), pltpu.SemaphoreType.DMA((n,))],
        name=name,
    )(*xs)


def all_to_all(x, name):
    any_spec = pl.BlockSpec(memory_space=pl.ANY)

    def body(x_ref, out_ref, send_sems, recv_sems, local_sem):
        x_, y_, c_ = _coords()
        me = 4 * x_ + 2 * y_ + c_
        mine = pltpu.make_async_copy(x_ref.at[me], out_ref.at[me], local_sem)
        mine.start()
        copies = []
        for k in range(1, N_DEV):
            px = 1 - x_ if k & 4 else x_
            py = 1 - y_ if k & 2 else y_
            pc = 1 - c_ if k & 1 else c_
            peer = 4 * px + 2 * py + pc
            copies.append((pltpu.make_async_remote_copy(
                src_ref=x_ref.at[peer], dst_ref=out_ref.at[me],
                send_sem=send_sems.at[k - 1], recv_sem=recv_sems.at[k - 1],
                device_id=(px, py, pc), device_id_type=pl.DeviceIdType.MESH), peer))
        for cp, _ in copies:
            cp.start()
        for k, (cp, peer) in enumerate(copies):
            pltpu.make_async_remote_copy(
                src_ref=x_ref.at[peer], dst_ref=out_ref.at[peer],
                send_sem=send_sems.at[k], recv_sem=recv_sems.at[k],
                device_id=(x_, y_, c_), device_id_type=pl.DeviceIdType.MESH).wait_recv()
        for cp, _ in copies:
            cp.wait_send()
        mine.wait()

    return pl.pallas_call(
        body, out_shape=_sds(x.shape, x.dtype),
        in_specs=[any_spec], out_specs=any_spec,
        scratch_shapes=[pltpu.SemaphoreType.DMA((7,)), pltpu.SemaphoreType.DMA((7,)), pltpu.SemaphoreType.DMA],
        name=name,
    )(x)


def _rows128(a):
    return a.reshape(-1, 128)


def _pad_lanes(a):
    return jnp.pad(a, ((0, 0), (0, 128 - a.shape[1])))


def kernel(x, c, w_ada, b_ada, norm_mix, norm_mlp, w_in, sb_q_norm, sb_k_norm, conv_w, a_log, dt_bias, dn_out_norm, w_out, w_ff1, w_ff2, loss_target, m_w_ada, m_b_ada, m_norm_mix, m_norm_mlp, m_w_in, m_sb_q_norm, m_sb_k_norm, m_conv_w, m_a_log, m_dt_bias, m_dn_out_norm, m_w_out, m_w_ff1, m_w_ff2, v_w_ada, v_b_ada, v_norm_mix, v_norm_mlp, v_w_in, v_sb_q_norm, v_sb_k_norm, v_conv_w, v_a_log, v_dt_bias, v_dn_out_norm, v_w_out, v_w_ff1, v_w_ff2):
    B, T, D = x.shape
    L = w_ada.shape[0]
    N = B * T
    FF = w_ff1.shape[2] * N_DEV
    WA = w_ada.shape[2]
    CS = conv_w.shape[2]
    me = 4 * lax.axis_index("x") + 2 * lax.axis_index("y") + lax.axis_index("c")
    tm = _pick(T, 1024)

    wb = [w.astype(BF16) for w in (w_in, w_out, w_ff1, w_ff2)]

    def assemble(lands):
        win_g, wout_g, w1_g, w2_g = lands
        return (jnp.pad(win_g.transpose(1, 0, 2).reshape(D, IN_W), ((0, 0), (0, IN_WP - IN_W))),
                wout_g.reshape(SB_W + DN_W, D), w1_g.transpose(1, 0, 2).reshape(D, FF), w2_g.reshape(FF, D))

    *lands0, conv_g, c_g = all_gather_many([w[0] for w in wb] + [conv_w, c], "comm_gather_first")
    weights = [assemble(lands0)]
    conv_full = conv_g.transpose(1, 2, 0, 3).reshape(L, CONV_K, 3 * DN_W)

    c_all = c_g.reshape(N_DEV * B, D)
    b_sl = lax.dynamic_slice_in_dim(b_ada, me * WA, WA, axis=1).reshape(L, 1, WA)
    mod_sh = ada_fwd(c_all, w_ada, b_sl, "ada_fwd")
    mod_g = all_gather(mod_sh, "comm_gather_mod")
    mod = lax.dynamic_slice_in_dim(mod_g, me * B, B, axis=2).transpose(1, 2, 0, 3).reshape(L, B, 6 * D)

    def mod_part(l, i):
        return mod[l, :, i * D:(i + 1) * D].reshape(B, 1, D)

    alog_row = _pad_lanes(a_log).reshape(L, 1, 128)
    dtb_row = _pad_lanes(dt_bias).reshape(L, 1, 128)

    def gate_epi(acc, xv, g):
        return acc, xv + g[0] * acc

    def relu2(a):
        r = jnp.maximum(a.astype(F32), 0.0)
        return r * r

    def times_gate(a, g):
        return a * g[0]

    tile_ij = lambda i, j, k: (i, j)

    saved = []
    xc = x
    for l in range(L):
        sh_a, sc_a, g_a, sh_m, sc_m, g_m = [mod_part(l, i) for i in range(6)]
        h = ln_mod_fwd(xc, norm_mix[l:l + 1], sc_a, sh_a, "ln_mod_fwd")
        W_in_l, W_out_l, W_1_l, W_2_l = weights[l]
        proj3 = matmul(h.reshape(N, D), W_in_l, mode="nn", name="mm_proj", tm=256)[0].reshape(B, T, IN_WP)
        qkv = gdn_pre_fwd(proj3, conv_full[l], "gdn_pre_fwd")
        gates = gdn_gates_fwd(proj3, alog_row[l], dtb_row[l], "gdn_gates_fwd")
        if l + 1 < L:
            o_a, tot, *lands = sb_attn_fwd(proj3, sb_q_norm[l:l + 1], sb_k_norm[l:l + 1], "sb_attn_fwd_gather",
                                           exch=("gather", [w[l + 1] for w in wb[:3]]))
            ob, states, tms, land_w2 = gdn_chunk_fwd(qkv, gates, "gdn_chunk_fwd_gather", exch=("gather", [wb[3][l + 1]]))
            weights.append(assemble(lands + [land_w2]))
        else:
            o_a, tot = sb_attn_fwd(proj3, sb_q_norm[l:l + 1], sb_k_norm[l:l + 1], "sb_attn_fwd")
            ob, states, tms = gdn_chunk_fwd(qkv, gates, "gdn_chunk_fwd")
        o_b = gdn_post_fwd(ob, proj3, dn_out_norm[l:l + 1], "gdn_post_fwd")
        mix = jnp.concatenate([o_a, o_b], axis=-1)
        y1, x_mid = matmul(
            mix.reshape(N, SB_W + DN_W), W_out_l, mode="nn", name="mm_out", out_dtypes=(BF16, F32), tm=tm, epi=gate_epi,
            extras=[(xc.reshape(N, D), (tm, _pick(D, 1024)), tile_ij),
                    (g_a, (1, 1, _pick(D, 1024)), lambda i, j, k: (i * tm // T, 0, j))])
        x_mid = x_mid.reshape(B, T, D)
        h2 = ln_mod_fwd(x_mid, norm_mlp[l:l + 1], sc_m, sh_m, "ln_mod_fwd")
        u = matmul(h2.reshape(N, D), W_1_l, mode="nn", name="mm_ff1", out_dtypes=(BF16,))[0]
        y2, x_out = matmul(
            u, W_2_l, mode="nn", name="mm_ff2", out_dtypes=(BF16, F32), tm=tm, a_fn=relu2, epi=gate_epi,
            extras=[(x_mid.reshape(N, D), (tm, _pick(D, 1024)), tile_ij),
                    (g_m, (1, 1, _pick(D, 1024)), lambda i, j, k: (i * tm // T, 0, j))])
        saved.append(dict(x=xc, h=h, proj3=proj3, tot=tot, qkv=qkv, gates=gates, states=states, tms=tms, ob=ob, mix=mix,
                          y1=y1, x_mid=x_mid, h2=h2, u=u, y2=y2))
        xc = x_out.reshape(B, T, D)

    dx, sq = loss_grad(xc, loss_target, "loss_grad")
    loss = lax.psum((0.5 / D) * jnp.sum(sq), AXES)

    g_win, g_wout, g_w1, g_w2, dmods, smalls, parts = [], [], [], [], [], [], []
    pending = None
    tk_tok = tm
    wi = w_in.shape[2]

    def shard_layer(gin, gout, g1, g2):
        return [None if gin is None else gin[:, :IN_W].reshape(D, N_DEV, wi).transpose(1, 0, 2), gout.reshape(N_DEV, w_out.shape[1], D),
                g1.reshape(D, N_DEV, w_ff1.shape[2]).transpose(1, 0, 2), g2.reshape(N_DEV, w_ff2.shape[1], D)]

    for l in reversed(range(L)):
        s = saved[l]
        W_in_l, W_out_l, W_1_l, W_2_l = weights[l]
        sh_a, sc_a, g_a, sh_m, sc_m, g_m = [mod_part(l, i) for i in range(6)]
        gate_k = lambda g, blk: (g, (1, 1, blk), lambda i, j, k: (i * tm // T, 0, k))
        gate_tok = lambda g, blk: (g, (1, 1, blk), lambda i, j, k: (k * tk_tok // T, 0, j))
        dx2 = dx.reshape(N, D)
        dg_m = rowsum_prod(dx, s["y2"].reshape(B, T, D), "rowsum_prod")
        du = matmul(dx2, W_2_l, mode="nt", name="mm_ff2_da", out_dtypes=(BF16,), tm=tm, a_fn=times_gate,
                    a_extras=[gate_k(g_m, _pick(D, 1024))],
                    epi=lambda acc, uv: (acc * (2.0 * jnp.maximum(uv, 0.0)),),
                    extras=[(s["u"], (tm, _pick(FF, 1024)), tile_ij)])[0]
        g_w2.append(matmul(s["u"], dx2, mode="tn", name="mm_ff2_dw", out_dtypes=(BF16,), tk=tk_tok, a_fn=relu2,
                           b_fn=times_gate, b_extras=[gate_tok(g_m, _pick(D, 1024))])[0])
        g_w1.append(matmul(s["h2"].reshape(N, D), du, mode="tn", name="mm_ff1_dw", out_dtypes=(BF16,))[0])
        dh2 = matmul(du, W_1_l, mode="nt", name="mm_ff1_da")[0]
        dx_mid, dgn_mlp, dsc_m, dsh_m = ln_mod_bwd(s["x_mid"], norm_mlp[l:l + 1], sc_m, dh2.reshape(B, T, D), dx, "ln_mod_bwd")
        dxm2 = dx_mid.reshape(N, D)
        dg_a = rowsum_prod(dx_mid, s["y1"].reshape(B, T, D), "rowsum_prod")
        dmix3 = matmul(dxm2, W_out_l, mode="nt", name="mm_out_da", tm=tm, a_fn=times_gate,
                       a_extras=[gate_k(g_a, _pick(D, 1024))])[0].reshape(B, T, SB_W + DN_W)
        g_wout.append(matmul(s["mix"].reshape(N, SB_W + DN_W), dxm2, mode="tn", name="mm_out_dw", out_dtypes=(BF16,),
                             tk=tk_tok, b_fn=times_gate, b_extras=[gate_tok(g_a, _pick(D, 1024))])[0])
        srcs = (pending or []) + (shard_layer(None, g_wout[-1], g_w1[-1], g_w2[-1])[1:] if l == 0 else [])
        if srcs:
            dq_a, dk_a, dv_a, dgq, dgk, *lands = sb_attn_bwd(
                s["proj3"], sb_q_norm[l:l + 1], sb_k_norm[l:l + 1], s["tot"], dmix3,
                "sb_attn_bwd_scatter" if l else "sb_attn_bwd_scatter0", exch=("scatter", srcs))
            if pending:
                parts.append(lands[:4])
            lands0 = lands[-3:]
        else:
            dq_a, dk_a, dv_a, dgq, dgk = sb_attn_bwd(s["proj3"], sb_q_norm[l:l + 1], sb_k_norm[l:l + 1], s["tot"], dmix3, "sb_attn_bwd")
        dob, dz, dgn_dn = gdn_post_bwd(s["ob"], s["proj3"], dn_out_norm[l:l + 1], dmix3, "gdn_post_bwd")
        dqkv, dgates = gdn_chunk_bwd(s["qkv"], s["gates"], s["states"], s["tms"], dob, "gdn_chunk_bwd")
        d_dnqkv, dconv_b = gdn_pre_bwd(s["proj3"], conv_full[l], dqkv, "gdn_pre_bwd")
        d_ab, dalog, ddtb = gdn_gates_bwd(s["proj3"], alog_row[l], dtb_row[l], dgates, "gdn_gates_bwd")
        dproj = [a.reshape(N, a.shape[-1]) for a in (dq_a, dk_a, dv_a, d_dnqkv, dz, d_ab)]
        g_win.append(proj_bwd_weight(s["h"].reshape(N, D), dproj, "mm_proj_dw"))
        dh = proj_bwd_input(dproj, W_in_l, "mm_proj_da")
        dx, dgn_mix, dsc_a, dsh_a = ln_mod_bwd(s["x"], norm_mix[l:l + 1], sc_a, dh.reshape(B, T, D), dx_mid, "ln_mod_bwd")
        pending = shard_layer(g_win[-1], g_wout[-1], g_w1[-1], g_w2[-1])
        dmods.append(jnp.concatenate([dsh_a, dsc_a, dg_a, dsh_m, dsc_m, dg_m], axis=-1).reshape(B, 6 * D))
        smalls.append(dict(norm_mix=dgn_mix, norm_mlp=dgn_mlp, sbq=dgq, sbk=dgk, alog=dalog, dtb=ddtb, dnorm=dgn_dn,
                           conv=jnp.sum(dconv_b, axis=0)))
    parts.append([all_to_all(pending[0], "comm_scatter_w_in")] + lands0)
    for lst in (dmods, smalls, parts):
        lst.reverse()
    grad_x = dx

    def update2d(parts, w, m, v, name):
        return [o[0] for o in adamw(parts[:, None], w[None], m[None], v[None], name)]

    p_win, p_wout, p_w1, p_w2 = [[parts[l][i] for l in range(L)] for i in range(4)]
    r_win = adamw(p_win, w_in, m_w_in, v_w_in, "adamw_w_in")
    r_wout = adamw(p_wout, w_out, m_w_out, v_w_out, "adamw_w_out")
    r_w1 = adamw(p_w1, w_ff1, m_w_ff1, v_w_ff1, "adamw_w_ff1")
    r_w2 = adamw(p_w2, w_ff2, m_w_ff2, v_w_ff2, "adamw_w_ff2")

    def pack(f):
        return jnp.concatenate([
            _rows128(f("norm_mix")), _rows128(f("norm_mlp")), _rows128(f("sbq")), _rows128(f("sbk")),
            f("alog"), f("dtb"), f("dnorm"), _rows128(f("conv"))], axis=0)

    part = pack(lambda n: jnp.concatenate([sm[n] for sm in smalls], axis=0))
    dmod_g, part_g = all_gather_many([jnp.stack(dmods), part], "comm_gather_last")

    dmod_all = dmod_g.transpose(1, 0, 2, 3).reshape(L, N_DEV * B, 6 * D)
    g_wada = ada_bwd(c_all, lax.dynamic_slice_in_dim(dmod_all, me * WA, WA, axis=2), "ada_bwd")
    r_wada = adamw(g_wada[None], w_ada, m_w_ada, v_w_ada, "adamw_w_ada")
    r_bada = update2d(dmod_g.transpose(0, 2, 1, 3).reshape(N_DEV * B, L, 6 * D), b_ada, m_b_ada, v_b_ada, "adamw_b_ada")

    names = ["norm_mix", "norm_mlp", "sbq", "sbk", "alog", "dtb", "dnorm"]
    n_rep = part.shape[0] - L * CONV_K * 3 * DN_W // 128
    params = dict(norm_mix=(norm_mix, m_norm_mix, v_norm_mix), norm_mlp=(norm_mlp, m_norm_mlp, v_norm_mlp),
                  sbq=(sb_q_norm, m_sb_q_norm, v_sb_q_norm), sbk=(sb_k_norm, m_sb_k_norm, v_sb_k_norm),
                  alog=(a_log, m_a_log, v_a_log), dtb=(dt_bias, m_dt_bias, v_dt_bias),
                  dnorm=(dn_out_norm, m_dn_out_norm, v_dn_out_norm))

    def rows_of(n, a):
        return _pad_lanes(a) if n in ("alog", "dtb") else _rows128(a)

    packed = [jnp.concatenate([rows_of(n, params[n][i]) for n in names], axis=0) for i in range(3)]
    r_small = update2d(part_g[:, :n_rep], packed[0], packed[1], packed[2], "adamw_small")
    small_out = {}
    off = 0
    for n in names:
        w0 = params[n][0]
        nr = rows_of(n, w0).shape[0]
        vals = [o[off:off + nr] for o in r_small]
        small_out[n] = [(vv[:, :w0.shape[1]] if n in ("alog", "dtb") else vv.reshape(w0.shape)) for vv in vals]
        off += nr
    conv_parts = part_g[:, n_rep:].reshape(N_DEV, L, CONV_K, 3 * DN_W)
    r_conv = adamw(lax.dynamic_slice_in_dim(conv_parts, me * CS, CS, axis=3), conv_w, m_conv_w, v_conv_w, "adamw_conv")

    order = [r_wada, r_bada, small_out["norm_mix"], small_out["norm_mlp"], r_win, small_out["sbq"], small_out["sbk"],
             r_conv, small_out["alog"], small_out["dtb"], small_out["dnorm"], r_wout, r_w1, r_w2]
    outs = [loss, grad_x]
    for i in range(4):
        outs += [r[i] for r in order]
    return tuple(outs)
```

```python
import functools

import jax
import jax.numpy as jnp
from jax import lax
from jax.experimental import pallas as pl
from jax.experimental.pallas import tpu as pltpu

F32 = jnp.float32
BF16 = jnp.bfloat16
EPS = 1e-6
N_DEV = 8
AXES = ("x", "y", "c")

SB_HEADS, SB_HD = 8, 64
SB_W = SB_HEADS * SB_HD
SB_BLK = 128
DN_HEADS, DN_HD = 4, 128
DN_W = DN_HEADS * DN_HD
DN_C = 64
CONV_K = 4
IN_W = 3 * SB_W + 4 * DN_W + 2 * DN_HEADS
IN_WP = 3 * SB_W + 4 * DN_W + 128
COL_DNQKV = 3 * SB_W // 128
COL_Z = COL_DNQKV + 3 * DN_W // 128
COL_AB = COL_Z + DN_W // 128

ADAM_LR, ADAM_B1, ADAM_B2, ADAM_EPS, ADAM_WD, ADAM_STEP = 0.001, 0.9, 0.999, 1e-08, 0.01, 10

VMEM_LIMIT = 56 * 1024 * 1024


def _cp(sem):
    return pltpu.CompilerParams(dimension_semantics=sem, vmem_limit_bytes=VMEM_LIMIT)


def _pick(dim, pref):
    return pref if dim % pref == 0 else dim


def _sds(shape, dtype):
    return jax.ShapeDtypeStruct(tuple(shape), dtype)


_NN = (((1,), (0,)), ((), ()))
_NT = (((1,), (1,)), ((), ()))
_TN = (((0,), (0,)), ((), ()))


def _bdot(a, b, dims=_NN):
    return lax.dot_general(a.astype(BF16), b.astype(BF16), dims, preferred_element_type=F32)


def _split(a):
    hi = a.astype(BF16)
    lo = (a - hi.astype(F32)).astype(BF16)
    return hi, lo


def _pdot(a, b, dims=_NN):
    ah, al = _split(a)
    bh, bl = _split(b)
    d = functools.partial(lax.dot_general, dimension_numbers=dims, preferred_element_type=F32)
    return d(ah, bh) + (d(ah, bl) + d(al, bh))


def _sigmoid(x):
    return 1.0 / (1.0 + jnp.exp(-x))


def _softplus(x):
    return jnp.maximum(x, 0.0) + jnp.log(1.0 + jnp.exp(-jnp.abs(x)))


def matmul(a, b, *, mode, name, out_dtypes=(F32,), a_fn=None, a_extras=(), b_fn=None, b_extras=(),
           epi=None, extras=(), tm=1024, tn=1024, tk=1024):
    if mode == "tn":
        K, M = a.shape
    else:
        M, K = a.shape
    N = b.shape[0] if mode == "nt" else b.shape[1]
    tm, tn, tk = _pick(M, tm), _pick(N, tn), _pick(K, tk)
    nk = K // tk
    dims = {"nn": _NN, "nt": _NT, "tn": _TN}[mode]
    a_spec = pl.BlockSpec((tk, tm), lambda i, j, k: (k, i)) if mode == "tn" else pl.BlockSpec((tm, tk), lambda i, j, k: (i, k))
    b_spec = pl.BlockSpec((tn, tk), lambda i, j, k: (j, k)) if mode == "nt" else pl.BlockSpec((tk, tn), lambda i, j, k: (k, j))
    na, nb, ne, no = len(a_extras), len(b_extras), len(extras), len(out_dtypes)

    def body(*refs):
        a_ref, b_ref = refs[0], refs[1]
        ax = refs[2:2 + na]
        bx = refs[2 + na:2 + na + nb]
        ex = refs[2 + na + nb:2 + na + nb + ne]
        outs = refs[2 + na + nb + ne:2 + na + nb + ne + no]
        acc_ref = refs[-1]
        k = pl.program_id(2)

        @pl.when(k == 0)
        def _():
            acc_ref[...] = jnp.zeros_like(acc_ref)

        av = a_ref[...]
        if a_fn is not None:
            av = a_fn(av, *[r[...] for r in ax])
        bv = b_ref[...]
        if b_fn is not None:
            bv = b_fn(bv, *[r[...] for r in bx])
        acc_ref[...] += lax.dot_general(av.astype(BF16), bv.astype(BF16), dims, preferred_element_type=F32)

        @pl.when(k == nk - 1)
        def _():
            res = acc_ref[...]
            res = (res,) if epi is None else epi(res, *[r[...] for r in ex])
            for o_ref, r in zip(outs, res):
                o_ref[...] = r.astype(o_ref.dtype)

    xs = list(a_extras) + list(b_extras) + list(extras)
    return pl.pallas_call(
        body,
        grid=(M // tm, N // tn, nk),
        in_specs=[a_spec, b_spec] + [pl.BlockSpec(bs, im) for _, bs, im in xs],
        out_specs=[pl.BlockSpec((tm, tn), lambda i, j, k: (i, j)) for _ in out_dtypes],
        out_shape=[_sds((M, N), dt) for dt in out_dtypes],
        scratch_shapes=[pltpu.VMEM((tm, tn), F32)],
        compiler_params=_cp(("parallel", "parallel", "arbitrary")),
        name=name,
    )(a, b, *[x for x, _, _ in xs])


def proj_bwd_input(pieces, w, name, tm=512, tn=1024, exch=None):
    N, D = pieces[0].shape[0], w.shape[0]
    widths = [p.shape[1] for p in pieces]
    offs = [sum(widths[:i]) for i in range(len(widths))]
    tm, tn = _pick(N, tm), _pick(D, tn)

    def body(*refs):
        w_ref, o_ref = refs[len(pieces)], refs[len(pieces) + 1]
        acc = None
        for p_ref, off, wd in zip(refs, offs, widths):
            t = lax.dot_general(p_ref[...].astype(BF16), w_ref[:, off:off + wd], _NT, preferred_element_type=F32)
            acc = t if acc is None else acc + t
        o_ref[...] = acc

    return _hosted_call(
        body, grid=(N // tm, D // tn),
        in_specs=[pl.BlockSpec((tm, wd), lambda i, j: (i, 0)) for wd in widths] + [pl.BlockSpec((tn, sum(widths)), lambda i, j: (j, 0))],
        out_specs=[pl.BlockSpec((tm, tn), lambda i, j: (i, j))], out_shape=[_sds((N, D), F32)],
        scratch_shapes=[], name=name, args=(*pieces, w), exch=exch)


def proj_bwd_weight(h, pieces, name, tm=512, tk=512):
    N, D = h.shape
    widths = [p.shape[1] for p in pieces]
    offs = [sum(widths[:i]) for i in range(len(widths))]
    tm, tk = _pick(D, tm), _pick(N, tk)
    nk = N // tk

    def body(*refs):
        h_ref, o_ref, acc_ref = refs[0], refs[len(pieces) + 1], refs[len(pieces) + 2]
        k = pl.program_id(1)

        @pl.when(k == 0)
        def _():
            acc_ref[...] = jnp.zeros_like(acc_ref)

        hv = h_ref[...]
        for p_ref, off, wd in zip(refs[1:], offs, widths):
            acc_ref[:, off:off + wd] += lax.dot_general(hv, p_ref[...].astype(BF16), _TN, preferred_element_type=F32)

        @pl.when(k == nk - 1)
        def _():
            o_ref[...] = acc_ref[...].astype(o_ref.dtype)

    return pl.pallas_call(
        body, grid=(D // tm, nk),
        in_specs=[pl.BlockSpec((tk, tm), lambda i, k: (k, i))] + [pl.BlockSpec((tk, wd), lambda i, k: (k, 0)) for wd in widths],
        out_specs=pl.BlockSpec((tm, sum(widths)), lambda i, k: (i, 0)), out_shape=_sds((D, sum(widths)), BF16),
        scratch_shapes=[pltpu.VMEM((tm, sum(widths)), F32)],
        compiler_params=_cp(("parallel", "arbitrary")), name=name,
    )(h, *pieces)


def ln_mod_fwd(x, gain, sc, sh, name):
    B, T, D = x.shape
    tt = _pick(T, 512)

    def body(x_ref, g_ref, sc_ref, sh_ref, h_ref):
        xv = x_ref[0]
        r = lax.rsqrt(jnp.mean(xv * xv, axis=-1, keepdims=True) + EPS)
        h = (xv * r * g_ref[...]) * (1.0 + sc_ref[0]) + sh_ref[0]
        h_ref[0] = h.astype(h_ref.dtype)

    return pl.pallas_call(
        body, grid=(B, T // tt),
        in_specs=[pl.BlockSpec((1, tt, D), lambda b, t: (b, t, 0)), pl.BlockSpec((1, D), lambda b, t: (0, 0)),
                  pl.BlockSpec((1, 1, D), lambda b, t: (b, 0, 0)), pl.BlockSpec((1, 1, D), lambda b, t: (b, 0, 0))],
        out_specs=pl.BlockSpec((1, tt, D), lambda b, t: (b, t, 0)),
        out_shape=_sds((B, T, D), BF16),
        compiler_params=_cp(("parallel", "parallel")), name=name,
    )(x, gain, sc, sh)


def ln_mod_bwd(x, gain, sc, dh, dres, name):
    B, T, D = x.shape
    tt = _pick(T, 512)

    def body(x_ref, g_ref, sc_ref, dh_ref, dres_ref, dx_ref, dg_ref, dsc_ref, dsh_ref):
        b, t = pl.program_id(0), pl.program_id(1)
        xv, dhv = x_ref[0], dh_ref[0]
        g, s = g_ref[...], sc_ref[0]
        r = lax.rsqrt(jnp.mean(xv * xv, axis=-1, keepdims=True) + EPS)
        xn = xv * r
        dxn = dhv * (g * (1.0 + s))
        dx_ref[0] = dres_ref[0] + r * (dxn - xn * jnp.mean(dxn * xn, axis=-1, keepdims=True))
        s1 = jnp.sum(dhv * xn, axis=0, keepdims=True)
        s2 = jnp.sum(dhv, axis=0, keepdims=True)

        @pl.when(t == 0)
        def _():
            dsc_ref[0] = jnp.zeros_like(s1)
            dsh_ref[0] = jnp.zeros_like(s1)

        @pl.when((t == 0) & (b == 0))
        def _():
            dg_ref[...] = jnp.zeros_like(s1)

        dsc_ref[0] += s1 * g
        dsh_ref[0] += s2
        dg_ref[...] += s1 * (1.0 + s)

    tile = pl.BlockSpec((1, tt, D), lambda b, t: (b, t, 0))
    row = pl.BlockSpec((1, D), lambda b, t: (0, 0))
    brow = pl.BlockSpec((1, 1, D), lambda b, t: (b, 0, 0))
    return pl.pallas_call(
        body, grid=(B, T // tt),
        in_specs=[tile, row, brow, tile, tile],
        out_specs=[tile, row, brow, brow],
        out_shape=[_sds((B, T, D), F32), _sds((1, D), F32), _sds((B, 1, D), F32), _sds((B, 1, D), F32)],
        compiler_params=_cp(("arbitrary", "arbitrary")), name=name,
    )(x, gain, sc, dh, dres)


def rowsum_prod(a, b, name):
    B, T, D = a.shape
    tt = _pick(T, 512)

    def body(a_ref, b_ref, o_ref):
        @pl.when(pl.program_id(1) == 0)
        def _():
            o_ref[...] = jnp.zeros_like(o_ref)

        o_ref[0] += jnp.sum(a_ref[0] * b_ref[0], axis=0, keepdims=True)

    tile = pl.BlockSpec((1, tt, D), lambda b, t: (b, t, 0))
    return pl.pallas_call(
        body, grid=(B, T // tt), in_specs=[tile, tile],
        out_specs=pl.BlockSpec((1, 1, D), lambda b, t: (b, 0, 0)),
        out_shape=_sds((B, 1, D), F32),
        compiler_params=_cp(("parallel", "arbitrary")), name=name,
    )(a, b)


def loss_grad(y, tgt, name):
    B, T, D = y.shape
    tt = _pick(T, 512)

    def body(y_ref, t_ref, dy_ref, s_ref):
        @pl.when((pl.program_id(0) == 0) & (pl.program_id(1) == 0))
        def _():
            s_ref[...] = jnp.zeros_like(s_ref)

        e = y_ref[0] - t_ref[0]
        dy_ref[0] = e * (1.0 / D)
        s_ref[...] += jnp.sum(e * e, axis=0, keepdims=True)

    tile = pl.BlockSpec((1, tt, D), lambda b, t: (b, t, 0))
    return pl.pallas_call(
        body, grid=(B, T // tt), in_specs=[tile, tile],
        out_specs=[tile, pl.BlockSpec((1, D), lambda b, t: (0, 0))],
        out_shape=[_sds((B, T, D), F32), _sds((1, D), F32)],
        compiler_params=_cp(("arbitrary", "arbitrary")), name=name,
    )(y, tgt)


def _sb_group(nb):
    return 4 if nb % 4 == 0 else (2 if nb % 2 == 0 else 1)


def _sb_qrows(t, kw):
    return 256 if (t % 256 == 0 and kw % 256 == 0) else SB_BLK


def _diag_step(sblock, sj, carry, thr, qb, kw):
    half = kw // 2
    if half % SB_BLK or half < qb:
        return sblock(sj, carry, True)
    return lax.cond(thr + qb <= half, lambda c: sblock(sj, c, True, half), lambda c: sblock(sj, c, True), carry)


def _tri_sum(x, tri):
    return lax.dot_general(x.astype(BF16), tri, _NN, preferred_element_type=F32)


def _tri2(cond):
    return cond.astype(BF16)


def sb_attn_fwd(proj3, gq, gk, name, exch=None):
    B, T, _ = proj3.shape
    NB = T // SB_BLK
    G = _sb_group(NB)
    KW = G * SB_BLK
    QB = _sb_qrows(T, KW)
    scale = SB_HD ** -0.5

    def body(q_ref, k_ref, v_ref, gq_ref, gk_ref, o_ref, tot_ref, qn_s, kn_s, v_s):
        row_io = lax.broadcasted_iota(jnp.int32, (SB_BLK, SB_BLK), 0)
        col_io = lax.broadcasted_iota(jnp.int32, (SB_BLK, SB_BLK), 1)
        tri = _tri2(row_io > col_io)

        def prep(i, _):
            rows = pl.ds(pl.multiple_of(i * SB_BLK, SB_BLK), SB_BLK)
            for hh in range(2):
                sl = slice(hh * SB_HD, (hh + 1) * SB_HD)
                q = q_ref[0, rows, sl]
                k = k_ref[0, rows, sl]
                qn_s[hh, rows, :] = (q * lax.rsqrt(jnp.mean(q * q, -1, keepdims=True) + EPS) * (gq_ref[...] * scale)).astype(BF16)
                kn_s[hh, rows, :] = (k * lax.rsqrt(jnp.mean(k * k, -1, keepdims=True) + EPS) * gk_ref[...]).astype(BF16)
                v_s[hh, rows, :] = v_ref[0, rows, sl].astype(BF16)
            return 0

        lax.fori_loop(0, NB, prep, 0)

        diff_w = (lax.broadcasted_iota(jnp.int32, (QB, KW), 1)
                  - lax.broadcasted_iota(jnp.int32, (QB, KW), 0))

        def qblock(i, _):
            rows = pl.ds(pl.multiple_of(i * QB, QB), QB)
            qn = [qn_s[hh, rows, :] for hh in range(2)]
            nsj = ((i + 1) * QB - 1) // KW + 1

            thr = i * QB - (nsj - 1) * KW

            def sblock(sj, carry, masked, width=KW):
                g = width // SB_BLK
                cols = pl.ds(pl.multiple_of(sj * KW, KW), width)
                mask = diff_w[:, :width] < thr
                zs = [lax.dot_general(qn[hh], kn_s[hh, cols, :], _NT, preferred_element_type=F32) for hh in range(2)]
                lgs, lss = [], []
                for hh in range(2):
                    sp = _softplus(zs[hh])
                    lgs.append(jnp.where(mask, -sp, 0.0) if masked else -sp)
                    lss.append(zs[hh] - sp)
                blocks = [lgs[hh][:, s * SB_BLK:(s + 1) * SB_BLK] for hh in range(2) for s in range(g)]
                ts_all = _tri_sum(jnp.concatenate(blocks, axis=0), tri)
                atts, css = [], []
                for hh in range(2):
                    cs = carry[hh][1]
                    ps = []
                    for s in reversed(range(g)):
                        n = hh * g + s
                        ts = ts_all[n * QB:(n + 1) * QB]
                        ps.append(lss[hh][:, s * SB_BLK:(s + 1) * SB_BLK] + ts + cs)
                        cs = cs + (ts[:, :1] + blocks[n][:, :1])
                    p = ps[0] if g == 1 else jnp.concatenate(ps[::-1], axis=1)
                    att = jnp.exp(p)
                    atts.append((jnp.where(mask, att, 0.0) if masked else att).astype(BF16))
                    css.append(cs)
                return tuple((carry[hh][0] + lax.dot_general(atts[hh], v_s[hh, cols, :], _NN, preferred_element_type=F32), css[hh])
                             for hh in range(2))

            init = (jnp.zeros((QB, SB_HD), F32), jnp.zeros((QB, 1), F32))
            res = _diag_step(sblock, nsj - 1, (init, init), thr, QB, KW)
            res = lax.fori_loop(0, nsj - 1, lambda jj, c: sblock(nsj - 2 - jj, c, False), res)
            for hh in range(2):
                o_ref[0, rows, hh * SB_HD:(hh + 1) * SB_HD] = res[hh][0].astype(o_ref.dtype)
                tot_ref[0, hh, rows, :] = res[hh][1]
            return 0

        lax.fori_loop(0, T // QB, qblock, 0)

    blk = lambda off: pl.BlockSpec((1, T, 128), lambda b, p: (b, 0, off + p))
    grow = pl.BlockSpec((1, SB_HD), lambda b, p: (0, 0))
    return _hosted_call(
        body, grid=(B, SB_W // 128),
        in_specs=[blk(0), blk(SB_W // 128), blk(2 * SB_W // 128), grow, grow],
        out_specs=[pl.BlockSpec((1, T, 128), lambda b, p: (b, 0, p)), pl.BlockSpec((1, 2, T, 1), lambda b, p: (b, p, 0, 0))],
        out_shape=[_sds((B, T, SB_W), BF16), _sds((B, SB_HEADS, T, 1), F32)],
        scratch_shapes=[pltpu.VMEM((2, T, SB_HD), BF16)] * 3,
        name=name, args=(proj3, proj3, proj3, gq, gk), exch=exch)


def sb_attn_bwd(proj3, gq, gk, tot, dmix3, name, exch=None):
    B, T, _ = proj3.shape
    NB = T // SB_BLK
    G = _sb_group(NB)
    KW = G * SB_BLK
    QB = _sb_qrows(T, KW)
    scale = SB_HD ** -0.5

    def body(q_ref, k_ref, v_ref, gq_ref, gk_ref, tot_ref, do_ref, dq_ref, dk_ref, dv_ref, dgq_ref, dgk_ref,
             qn_s, kn_s, v_s, do_s, dqn_s, dkn_s, dv_s):
        row_io = lax.broadcasted_iota(jnp.int32, (SB_BLK, SB_BLK), 0)
        col_io = lax.broadcasted_iota(jnp.int32, (SB_BLK, SB_BLK), 1)
        tri = _tri2(row_io > col_io)
        trip = _tri2(row_io < col_io)

        @pl.when((pl.program_id(0) == 0) & (pl.program_id(1) == 0))
        def _():
            dgq_ref[...] = jnp.zeros_like(dgq_ref)
            dgk_ref[...] = jnp.zeros_like(dgk_ref)

        def prep(i, _):
            rows = pl.ds(pl.multiple_of(i * SB_BLK, SB_BLK), SB_BLK)
            for hh in range(2):
                sl = slice(hh * SB_HD, (hh + 1) * SB_HD)
                q = q_ref[0, rows, sl]
                k = k_ref[0, rows, sl]
                qn_s[hh, rows, :] = (q * lax.rsqrt(jnp.mean(q * q, -1, keepdims=True) + EPS) * (gq_ref[...] * scale)).astype(BF16)
                kn_s[hh, rows, :] = (k * lax.rsqrt(jnp.mean(k * k, -1, keepdims=True) + EPS) * gk_ref[...]).astype(BF16)
                v_s[hh, rows, :] = v_ref[0, rows, sl].astype(BF16)
                do_s[hh, rows, :] = do_ref[0, rows, sl].astype(BF16)
            return 0

        lax.fori_loop(0, NB, prep, 0)
        dkn_s[...] = jnp.zeros_like(dkn_s)
        dv_s[...] = jnp.zeros_like(dv_s)

        diff_w = (lax.broadcasted_iota(jnp.int32, (QB, KW), 1)
                  - lax.broadcasted_iota(jnp.int32, (QB, KW), 0))

        def qblock(i, _):
            rows = pl.ds(pl.multiple_of(i * QB, QB), QB)
            qn = [qn_s[hh, rows, :] for hh in range(2)]
            dov = [do_s[hh, rows, :] for hh in range(2)]
            tot = [tot_ref[0, hh, rows, :] for hh in range(2)]
            nsj = ((i + 1) * QB - 1) // KW + 1
            qnT = [qn[hh].astype(F32).T.astype(BF16) for hh in range(2)]
            doT = [dov[hh].astype(F32).T.astype(BF16) for hh in range(2)]

            thr = i * QB - (nsj - 1) * KW

            def sblock(sj, carry, masked, width=KW):
                g = width // SB_BLK
                cols = pl.ds(pl.multiple_of(sj * KW, KW), width)
                mask = diff_w[:, :width] < thr
                hs = range(2)
                kns = [kn_s[hh, cols, :] for hh in hs]
                zs = [lax.dot_general(qn[hh], kns[hh], _NT, preferred_element_type=F32) for hh in hs]
                datts = [lax.dot_general(dov[hh], v_s[hh, cols, :], _NT, preferred_element_type=F32) for hh in hs]
                lgs, lss = [], []
                for hh in hs:
                    sp = _softplus(zs[hh])
                    lgs.append(jnp.where(mask, -sp, 0.0) if masked else -sp)
                    lss.append(zs[hh] - sp)
                blocks = [lgs[hh][:, s * SB_BLK:(s + 1) * SB_BLK] for hh in hs for s in range(g)]
                ts_all = _tri_sum(jnp.concatenate(blocks, axis=0), tri)
                atts, dps, cums = [], [], []
                for hh in hs:
                    cum = carry[hh][1]
                    ps = []
                    for s in range(g):
                        n = hh * g + s
                        ts = ts_all[n * QB:(n + 1) * QB]
                        cum = cum + (ts[:, :1] + blocks[n][:, :1])
                        ps.append(lss[hh][:, s * SB_BLK:(s + 1) * SB_BLK] + ts + (tot[hh] - cum))
                    p = ps[0] if g == 1 else jnp.concatenate(ps, axis=1)
                    att = jnp.exp(p)
                    att = jnp.where(mask, att, 0.0) if masked else att
                    atts.append(att.astype(BF16))
                    dps.append(att * datts[hh])
                    cums.append(cum)
                dblocks = [dps[hh][:, s * SB_BLK:(s + 1) * SB_BLK] for hh in hs for s in range(g)]
                tp_all = _tri_sum(jnp.concatenate(dblocks, axis=0), trip)
                dzs, cdps = [], []
                for hh in hs:
                    cdp = carry[hh][2]
                    dls = []
                    for s in range(g):
                        n = hh * g + s
                        tp = tp_all[n * QB:(n + 1) * QB]
                        dls.append(tp + cdp)
                        cdp = cdp + (tp[:, SB_BLK - 1:] + dblocks[n][:, SB_BLK - 1:])
                    dlg = dls[0] if g == 1 else jnp.concatenate(dls, axis=1)
                    dz = dps[hh] - jnp.exp(lss[hh]) * (dps[hh] + dlg)
                    dzs.append((jnp.where(mask, dz, 0.0) if masked else dz).astype(BF16))
                    cdps.append(cdp)
                new = []
                for hh in hs:
                    dq = carry[hh][0] + lax.dot_general(dzs[hh], kns[hh], _NN, preferred_element_type=F32)
                    dkn_s[hh, :, cols] += lax.dot_general(qnT[hh], dzs[hh], _NN, preferred_element_type=F32)
                    dv_s[hh, :, cols] += lax.dot_general(doT[hh], atts[hh], _NN, preferred_element_type=F32)
                    new.append((dq, cums[hh], cdps[hh]))
                return tuple(new)

            z1 = jnp.zeros((QB, 1), F32)
            init = (jnp.zeros((QB, SB_HD), F32), z1, z1)
            res = lax.fori_loop(0, nsj - 1, lambda sj, c: sblock(sj, c, False), (init, init))
            res = _diag_step(sblock, nsj - 1, res, thr, QB, KW)
            for hh in range(2):
                dqn_s[hh, rows, :] = res[hh][0]
            return 0

        lax.fori_loop(0, T // QB, qblock, 0)

        def fin(i, carry):
            aq, ak = carry
            rows = pl.ds(pl.multiple_of(i * SB_BLK, SB_BLK), SB_BLK)
            for hh in range(2):
                sl = slice(hh * SB_HD, (hh + 1) * SB_HD)
                for src_ref, g_ref, out_ref, mult, which in ((q_ref, gq_ref, dq_ref, scale, 0), (k_ref, gk_ref, dk_ref, 1.0, 1)):
                    xr = src_ref[0, rows, sl]
                    r = lax.rsqrt(jnp.mean(xr * xr, -1, keepdims=True) + EPS)
                    dy = (dqn_s[hh, rows, :] if which == 0 else dkn_s[hh, :, rows].T) * mult
                    u = dy * g_ref[...]
                    out_ref[0, rows, sl] = r * u - xr * (r * r * r) * jnp.mean(u * xr, -1, keepdims=True)
                    part = jnp.sum(dy * xr * r, axis=0, keepdims=True)
                    if which == 0:
                        aq = aq + part
                    else:
                        ak = ak + part
                dv_ref[0, rows, sl] = dv_s[hh, :, rows].T
            return aq, ak

        z64 = jnp.zeros((1, SB_HD), F32)
        aq, ak = lax.fori_loop(0, NB, fin, (z64, z64))
        dgq_ref[...] += aq
        dgk_ref[...] += ak

    blk = lambda off: pl.BlockSpec((1, T, 128), lambda b, p: (b, 0, off + p))
    grow = pl.BlockSpec((1, SB_HD), lambda b, p: (0, 0))
    return _hosted_call(
        body, grid=(B, SB_W // 128),
        in_specs=[blk(0), blk(SB_W // 128), blk(2 * SB_W // 128), grow, grow,
                  pl.BlockSpec((1, 2, T, 1), lambda b, p: (b, p, 0, 0)), blk(0)],
        out_specs=[blk(0), blk(0), blk(0), grow, grow],
        out_shape=[_sds((B, T, SB_W), F32)] * 3 + [_sds((1, SB_HD), F32)] * 2,
        scratch_shapes=[pltpu.VMEM((2, T, SB_HD), BF16)] * 4 + [pltpu.VMEM((2, T, SB_HD), F32)] + [pltpu.VMEM((2, SB_HD, T), F32)] * 2,
        name=name, args=(proj3, proj3, proj3, gq, gk, tot, dmix3), exch=exch)


def _exch_copies(kind, src_refs, land_refs, sems, with_arrivals=True):
    x_, y_, c_ = _coords()
    me = 4 * x_ + 2 * y_ + c_
    local, go, arrive = [], [], []
    for a, (src, land) in enumerate(zip(src_refs, land_refs)):
        send, recv, loc = sems[3 * a:3 * a + 3]
        local.append(pltpu.make_async_copy(src if kind == "gather" else src.at[me], land.at[me], loc))
        for k in range(1, N_DEV):
            px = 1 - x_ if k & 4 else x_
            py = 1 - y_ if k & 2 else y_
            pc = 1 - c_ if k & 1 else c_
            peer = 4 * px + 2 * py + pc
            out = src if kind == "gather" else src.at[peer]
            mk = functools.partial(pltpu.make_async_remote_copy, send_sem=send.at[k - 1], recv_sem=recv.at[k - 1],
                                   device_id=(px, py, pc), device_id_type=pl.DeviceIdType.MESH)
            go.append(mk(src_ref=out, dst_ref=land.at[me]))
            if with_arrivals:
                arrive.append(mk(src_ref=out, dst_ref=land.at[peer]))
    return local, go, arrive


def _hosted_call(body, *, grid, in_specs, out_specs, out_shape, scratch_shapes, name, args, exch=None):
    sem = ("arbitrary",) * len(grid)
    if exch is None:
        return pl.pallas_call(body, grid=grid, in_specs=in_specs, out_specs=out_specs, out_shape=out_shape,
                              scratch_shapes=scratch_shapes, compiler_params=_cp(sem), name=name)(*args)
    kind, srcs = exch
    ns, n_in, n_out, n_scr = len(srcs), len(in_specs), len(out_specs), len(scratch_shapes)
    lands = [_sds((N_DEV,) + s.shape if kind == "gather" else s.shape, s.dtype) for s in srcs]

    def wrapped(*refs):
        ins, src_refs = refs[:n_in], refs[n_in:n_in + ns]
        outs, land_refs = refs[n_in + ns:n_in + ns + n_out], refs[n_in + ns + n_out:n_in + 2 * ns + n_out]
        scr, sems = refs[n_in + 2 * ns + n_out:n_in + 2 * ns + n_out + n_scr], refs[n_in + 2 * ns + n_out + n_scr:]
        ids = [pl.program_id(d) for d in range(len(grid))]
        first = functools.reduce(lambda a, b: a & b, [i == 0 for i in ids])
        last = functools.reduce(lambda a, b: a & b, [i == g - 1 for i, g in zip(ids, grid)])

        @pl.when(first)
        def _():
            local, go, _ = _exch_copies(kind, src_refs, land_refs, sems, with_arrivals=False)
            for cp in local + go:
                cp.start()

        body(*ins, *outs, *scr)

        @pl.when(last)
        def _():
            local, go, arrive = _exch_copies(kind, src_refs, land_refs, sems)
            for cp in arrive:
                cp.wait_recv()
            for cp in go:
                cp.wait_send()
            for cp in local:
                cp.wait()

    any_spec = pl.BlockSpec(memory_space=pl.ANY)
    sems = [pltpu.SemaphoreType.DMA((N_DEV - 1,)), pltpu.SemaphoreType.DMA((N_DEV - 1,)), pltpu.SemaphoreType.DMA] * ns
    return pl.pallas_call(
        wrapped, grid=grid, in_specs=list(in_specs) + [any_spec] * ns, out_specs=list(out_specs) + [any_spec] * ns,
        out_shape=list(out_shape) + lands, scratch_shapes=list(scratch_shapes) + sems,
        compiler_params=_cp(sem), name=name)(*args, *srcs)


def _conv_silu(x, w, T):
    t_io = lax.broadcasted_iota(jnp.int32, x.shape, 0)
    xs = [x] + [jnp.where(t_io >= s, pltpu.roll(x, s, 0), 0.0) for s in range(1, CONV_K)]
    y = xs[0] * w[CONV_K - 1:CONV_K, :]
    for s in range(1, CONV_K):
        y = y + xs[s] * w[CONV_K - 1 - s:CONV_K - s, :]
    return y, y * _sigmoid(y), xs


def gdn_pre_fwd(proj3, conv_w, name):
    B, T, _ = proj3.shape
    qs = DN_HD ** -0.5

    def body(x_ref, w_ref, o_ref):
        kind = pl.program_id(1) // DN_HEADS
        _, s, _ = _conv_silu(x_ref[0], w_ref[...], T)
        n = lax.rsqrt(jnp.sum(s * s, axis=-1, keepdims=True) + EPS)
        c = jnp.where(kind == 0, qs, 1.0)
        o_ref[0, 0] = jnp.where(kind < 2, s * (n * c), s)

    return pl.pallas_call(
        body, grid=(B, 3 * DN_HEADS),
        in_specs=[pl.BlockSpec((1, T, 128), lambda b, j: (b, 0, COL_DNQKV + j)), pl.BlockSpec((CONV_K, 128), lambda b, j: (0, j))],
        out_specs=pl.BlockSpec((1, 1, T, 128), lambda b, j: (b, j // DN_HEADS, 0, j % DN_HEADS)),
        out_shape=_sds((B, 3, T, DN_W), F32),
        compiler_params=_cp(("parallel", "parallel")), name=name,
    )(proj3, conv_w)


def gdn_pre_bwd(proj3, conv_w, dqkv, name):
    B, T, _ = proj3.shape
    qs = DN_HD ** -0.5

    def body(x_ref, w_ref, d_ref, dx_ref, dw_ref):
        kind = pl.program_id(1) // DN_HEADS
        w = w_ref[...]
        y, s, xs = _conv_silu(x_ref[0], w, T)
        dout = d_ref[0, 0]
        n = lax.rsqrt(jnp.sum(s * s, axis=-1, keepdims=True) + EPS)
        c = jnp.where(kind == 0, qs, 1.0)
        dsn = c * (n * dout - s * (n * n * n) * jnp.sum(dout * s, axis=-1, keepdims=True))
        ds = jnp.where(kind < 2, dsn, dout)
        sg = _sigmoid(y)
        dy = ds * (sg * (1.0 + y * (1.0 - sg)))
        t_io = lax.broadcasted_iota(jnp.int32, dy.shape, 0)
        dx = dy * w[CONV_K - 1:CONV_K, :]
        dw_ref[0, CONV_K - 1:CONV_K, :] = jnp.sum(dy * xs[0], axis=0, keepdims=True)
        for sft in range(1, CONV_K):
            dx = dx + jnp.where(t_io < T - sft, pltpu.roll(dy, T - sft, 0), 0.0) * w[CONV_K - 1 - sft:CONV_K - sft, :]
            dw_ref[0, CONV_K - 1 - sft:CONV_K - sft, :] = jnp.sum(dy * xs[sft], axis=0, keepdims=True)
        dx_ref[0] = dx

    return pl.pallas_call(
        body, grid=(B, 3 * DN_HEADS),
        in_specs=[pl.BlockSpec((1, T, 128), lambda b, j: (b, 0, COL_DNQKV + j)), pl.BlockSpec((CONV_K, 128), lambda b, j: (0, j)),
                  pl.BlockSpec((1, 1, T, 128), lambda b, j: (b, j // DN_HEADS, 0, j % DN_HEADS))],
        out_specs=[pl.BlockSpec((1, T, 128), lambda b, j: (b, 0, j)), pl.BlockSpec((1, CONV_K, 128), lambda b, j: (b, 0, j))],
        out_shape=[_sds((B, T, 3 * DN_W), F32), _sds((B, CONV_K, 3 * DN_W), F32)],
        compiler_params=_cp(("parallel", "parallel")), name=name,
    )(proj3, conv_w, dqkv)


def gdn_gates_fwd(proj3, alog_row, dtb_row, name):
    B, T, _ = proj3.shape

    def body(x_ref, al_ref, dt_ref, o_ref):
        x = x_ref[0]
        lane = lax.broadcasted_iota(jnp.int32, x.shape, 1)
        g = -jnp.exp(al_ref[...]) * _softplus(x + dt_ref[...])
        o_ref[0] = jnp.where(lane < DN_HEADS, g, jnp.where(lane < 2 * DN_HEADS, _sigmoid(x), 0.0))

    row = pl.BlockSpec((1, 128), lambda b: (0, 0))
    return pl.pallas_call(
        body, grid=(B,),
        in_specs=[pl.BlockSpec((1, T, 128), lambda b: (b, 0, COL_AB)), row, row],
        out_specs=pl.BlockSpec((1, T, 128), lambda b: (b, 0, 0)),
        out_shape=_sds((B, T, 128), F32),
        compiler_params=_cp(("parallel",)), name=name,
    )(proj3, alog_row, dtb_row)


def gdn_gates_bwd(proj3, alog_row, dtb_row, dgates, name):
    B, T, _ = proj3.shape

    def body(x_ref, al_ref, dt_ref, d_ref, dx_ref, dal_ref, ddt_ref):
        @pl.when(pl.program_id(0) == 0)
        def _():
            dal_ref[...] = jnp.zeros_like(dal_ref)
            ddt_ref[...] = jnp.zeros_like(ddt_ref)

        x, d = x_ref[0], d_ref[0]
        lane = lax.broadcasted_iota(jnp.int32, x.shape, 1)
        a = x + dt_ref[...]
        na = -jnp.exp(al_ref[...])
        da = jnp.where(lane < DN_HEADS, d * na * _sigmoid(a), 0.0)
        bt = _sigmoid(x)
        dx_ref[0] = da + jnp.where((lane >= DN_HEADS) & (lane < 2 * DN_HEADS), d * bt * (1.0 - bt), 0.0)
        dal_ref[...] += jnp.sum(jnp.where(lane < DN_HEADS, d * na * _softplus(a), 0.0), axis=0, keepdims=True)
        ddt_ref[...] += jnp.sum(da, axis=0, keepdims=True)

    row = pl.BlockSpec((1, 128), lambda b: (0, 0))
    tile = pl.BlockSpec((1, T, 128), lambda b: (b, 0, 0))
    return pl.pallas_call(
        body, grid=(B,),
        in_specs=[pl.BlockSpec((1, T, 128), lambda b: (b, 0, COL_AB)), row, row, tile],
        out_specs=[tile, row, row],
        out_shape=[_sds((B, T, 128), F32), _sds((1, 128), F32), _sds((1, 128), F32)],
        compiler_params=_cp(("arbitrary",)), name=name,
    )(proj3, alog_row, dtb_row, dgates)


def gdn_post_fwd(ob, proj3, gain, name):
    B, T, _ = ob.shape

    def body(o_ref, z_ref, g_ref, out_ref):
        o, z = o_ref[0], z_ref[0]
        r = lax.rsqrt(jnp.mean(o * o, axis=-1, keepdims=True) + EPS)
        out_ref[0] = ((o * r * g_ref[...]) * (z * _sigmoid(z))).astype(out_ref.dtype)

    tile = pl.BlockSpec((1, T, 128), lambda b, h: (b, 0, h))
    return pl.pallas_call(
        body, grid=(B, DN_HEADS),
        in_specs=[tile, pl.BlockSpec((1, T, 128), lambda b, h: (b, 0, COL_Z + h)), pl.BlockSpec((1, 128), lambda b, h: (0, 0))],
        out_specs=tile, out_shape=_sds((B, T, DN_W), BF16),
        compiler_params=_cp(("parallel", "parallel")), name=name,
    )(ob, proj3, gain)


def gdn_post_bwd(ob, proj3, gain, dmix3, name):
    B, T, _ = ob.shape

    def body(o_ref, z_ref, g_ref, d_ref, do_ref, dz_ref, dg_ref):
        @pl.when((pl.program_id(0) == 0) & (pl.program_id(1) == 0))
        def _():
            dg_ref[...] = jnp.zeros_like(dg_ref)

        o, z, d, g = o_ref[0], z_ref[0], d_ref[0], g_ref[...]
        r = lax.rsqrt(jnp.mean(o * o, axis=-1, keepdims=True) + EPS)
        sg = _sigmoid(z)
        dn = d * (z * sg)
        dz_ref[0] = d * (o * r * g) * (sg * (1.0 + z * (1.0 - sg)))
        dg_ref[...] += jnp.sum(dn * o * r, axis=0, keepdims=True)
        u = dn * g
        do_ref[0] = r * u - o * (r * r * r) * jnp.mean(u * o, axis=-1, keepdims=True)

    tile = pl.BlockSpec((1, T, 128), lambda b, h: (b, 0, h))
    row = pl.BlockSpec((1, 128), lambda b, h: (0, 0))
    return pl.pallas_call(
        body, grid=(B, DN_HEADS),
        in_specs=[tile, pl.BlockSpec((1, T, 128), lambda b, h: (b, 0, COL_Z + h)), row,
                  pl.BlockSpec((1, T, 128), lambda b, h: (b, 0, SB_W // 128 + h))],
        out_specs=[tile, tile, row],
        out_shape=[_sds((B, T, DN_W), F32), _sds((B, T, DN_W), F32), _sds((1, 128), F32)],
        compiler_params=_cp(("arbitrary", "arbitrary")), name=name,
    )(ob, proj3, gain, dmix3)


def _tri_inv(low, ri, ci):
    m = (ri == ci).astype(F32) - jnp.where(((ri >> 1) == (ci >> 1)) & (ri > ci), low, 0.0)
    s = 2
    while s < DN_C:
        sh = s.bit_length()
        off = ((ri >> sh) == (ci >> sh)) & ((ri & (2 * s - 1)) >= s) & ((ci & (2 * s - 1)) < s)
        m = m - _pdot(m, _pdot(jnp.where(off, low, 0.0), m, _BNN), _BNN)
        s *= 2
    return m


_BNN = (((2,), (1,)), ((0,), (0,)))
_BNT = (((2,), (2,)), ((0,), (0,)))
_BTN = (((1,), (1,)), ((0,), (0,)))
DN_G = 16


def _chunk_common(q, k, v, gt, h, tm=None):
    C = DN_C
    G = q.shape[0]
    ri = lax.broadcasted_iota(jnp.int32, (C, C), 0)
    ci = lax.broadcasted_iota(jnp.int32, (C, C), 1)
    lane = lax.broadcasted_iota(jnp.int32, (C, 128), 1)
    incl, strict = ri >= ci, ri > ci
    g = jnp.sum(jnp.where(lane == h, gt, 0.0), axis=2, keepdims=True)
    beta = jnp.sum(jnp.where(lane == h + DN_HEADS, gt, 0.0), axis=2, keepdims=True)
    ones = jnp.ones((G, C, 128), F32)
    inclf = jnp.broadcast_to(incl.astype(F32), (G, C, C))
    gc = _pdot(inclf, g * ones, _BNN)[:, :, :1]
    gcr = _pdot(jnp.ones((G, C, C), F32), jnp.where(ri == ci, gc, 0.0), _BNN)
    decay = jnp.where(incl, jnp.exp(jnp.where(incl, gc - gcr, 0.0)), 0.0)
    e = jnp.exp(gc)
    kb, vb = k * beta, v * beta
    kk = _bdot(kb, k, _BNT)
    if tm is None:
        tm = _tri_inv(jnp.where(strict, kk * decay, 0.0), ri, ci)
    kbe = kb * e
    u = _bdot(tm, vb, _BNN)
    w = _bdot(tm, kbe, _BNN)
    qk = _bdot(q, k, _BNT)
    intra = jnp.where(incl, qk * decay, 0.0)
    gl = gc[:, C - 1:C, :]
    el = jnp.exp(gl)
    r = jnp.exp(gl - gc)
    return dict(lane=lane, incl=incl, inclf=inclf, strict=strict, beta=beta, decay=decay, e=e,
                kb=kb, vb=vb, kk=kk, tm=tm, kbe=kbe, u=u, w=w, qk=qk, intra=intra, el=el, r=r, ones=ones)


def gdn_chunk_fwd(qkv, gates, name, exch=None):
    B, _, T, _ = qkv.shape
    NC = T // DN_C

    G = DN_G if NC % DN_G == 0 else 1
    GC = G * DN_C

    GS = G * DN_HD

    def body(x_ref, gt_ref, o_ref, st_ref, tm_ref, s_s, p_s, b_s, qp_s, el_s):
        h = pl.program_id(1)

        def group_a(gi, _):
            rows = pl.ds(pl.multiple_of(gi * GC, GC), GC)
            srow = pl.ds(pl.multiple_of(gi * GS, GS), GS)
            q, k, v = [x_ref[0, i, rows, :].reshape(G, DN_C, DN_HD) for i in range(3)]
            c = _chunk_common(q, k, v, gt_ref[0, rows, :].reshape(G, DN_C, 128), h)
            kr = k * c["r"]
            tm_ref[0, 0, rows, :] = c["tm"].reshape(GC, DN_C)
            p_s[srow, :] = _bdot(kr, c["w"], _BTN).reshape(GS, DN_HD)
            b_s[srow, :] = _bdot(kr, c["u"], _BTN).reshape(GS, DN_HD)
            qp_s[rows, :] = (q * c["e"] - _bdot(c["intra"], c["w"], _BNN)).reshape(GC, DN_HD)
            o_ref[0, rows, :] = _bdot(c["intra"], c["u"], _BNN).reshape(GC, DN_HD)
            el_s[pl.ds(gi * G, G), :, :] = c["el"] * jnp.ones((G, 1, 128), F32)
            return 0

        lax.fori_loop(0, NC // G, group_a, 0)
        s_s[...] = jnp.zeros_like(s_s)

        def chunk(n, _):
            srow = pl.ds(pl.multiple_of(n * DN_HD, DN_HD), DN_HD)
            st = s_s[...]
            st_ref[0, 0, srow, :] = st
            s_s[...] = (st * el_s[n] + b_s[srow, :]) - _bdot(p_s[srow, :], st)
            return 0

        lax.fori_loop(0, NC, chunk, 0)

        def group_c(gi, _):
            rows = pl.ds(pl.multiple_of(gi * GC, GC), GC)
            srow = pl.ds(pl.multiple_of(gi * GS, GS), GS)
            st = st_ref[0, 0, srow, :].reshape(G, DN_HD, DN_HD)
            o_ref[0, rows, :] += _bdot(qp_s[rows, :].reshape(G, DN_C, DN_HD), st, _BNN).reshape(GC, DN_HD)
            return 0

        lax.fori_loop(0, NC // G, group_c, 0)

    return _hosted_call(
        body, grid=(B, DN_HEADS),
        in_specs=[pl.BlockSpec((1, 3, T, 128), lambda b, h: (b, 0, 0, h)), pl.BlockSpec((1, T, 128), lambda b, h: (b, 0, 0))],
        out_specs=[pl.BlockSpec((1, T, 128), lambda b, h: (b, 0, h)), pl.BlockSpec((1, 1, NC * DN_HD, DN_HD), lambda b, h: (b, h, 0, 0)),
                   pl.BlockSpec((1, 1, T, DN_C), lambda b, h: (b, h, 0, 0))],
        out_shape=[_sds((B, T, DN_W), F32), _sds((B, DN_HEADS, NC * DN_HD, DN_HD), F32), _sds((B, DN_HEADS, T, DN_C), F32)],
        scratch_shapes=[pltpu.VMEM((DN_HD, DN_HD), F32)] + [pltpu.VMEM((NC * DN_HD, DN_HD), F32)] * 2
        + [pltpu.VMEM((T, DN_HD), F32), pltpu.VMEM((NC, 1, 128), F32)],
        name=name, args=(qkv, gates), exch=exch)


def gdn_chunk_bwd(qkv, gates, states, tms, dob, name, exch=None):
    B, _, T, _ = qkv.shape
    NC = T // DN_C
    C = DN_C

    G = DN_G if NC % DN_G == 0 else 1
    GC = G * C

    GS = G * DN_HD

    def body(x_ref, gt_ref, st_ref, tm_ref, do_ref, dx_ref, dgt_ref, ds_s, p_s, r_s, el_s, dsa_s):
        h = pl.program_id(1)

        @pl.when(h == 0)
        def _():
            dgt_ref[...] = jnp.zeros_like(dgt_ref)

        def load(gi):
            rows = pl.ds(pl.multiple_of(gi * GC, GC), GC)
            q, k, v = [x_ref[0, i, rows, :].reshape(G, C, DN_HD) for i in range(3)]
            return rows, q, k, v, gt_ref[0, rows, :].reshape(G, C, 128), tm_ref[0, 0, rows, :].reshape(G, C, C)

        def group_a(gi, _):
            rows, q, k, v, gt, tm = load(gi)
            srow = pl.ds(pl.multiple_of(gi * GS, GS), GS)
            c = _chunk_common(q, k, v, gt, h, tm=tm)
            qp = q * c["e"] - _bdot(c["intra"], c["w"], _BNN)
            p_s[srow, :] = _bdot(k * c["r"], c["w"], _BTN).reshape(GS, DN_HD)
            r_s[srow, :] = _bdot(qp, do_ref[0, rows, :].reshape(G, C, DN_HD), _BTN).reshape(GS, DN_HD)
            el_s[pl.ds(gi * G, G), :, :] = c["el"] * jnp.ones((G, 1, 128), F32)
            return 0

        lax.fori_loop(0, NC // G, group_a, 0)
        ds_s[...] = jnp.zeros_like(ds_s)

        def chunk(m, _):
            n = NC - 1 - m
            srow = pl.ds(pl.multiple_of(n * DN_HD, DN_HD), DN_HD)
            dsn = ds_s[...]
            dsa_s[srow, :] = dsn
            ds_s[...] = (dsn * el_s[n] + r_s[srow, :]) - _bdot(p_s[srow, :], dsn, _TN)
            return 0

        lax.fori_loop(0, NC, chunk, 0)

        def group_c(gi, _):
            rows, q, k, v, gt, tm = load(gi)
            c = _chunk_common(q, k, v, gt, h, tm=tm)
            incl, strict, decay, e, r, el, tm = c["incl"], c["strict"], c["decay"], c["e"], c["r"], c["el"], c["tm"]
            srow = pl.ds(pl.multiple_of(gi * GS, GS), GS)
            st = st_ref[0, 0, srow, :].reshape(G, DN_HD, DN_HD)
            dsn = dsa_s[srow, :].reshape(G, DN_HD, DN_HD)
            do = do_ref[0, rows, :].reshape(G, C, DN_HD)
            dvn = _bdot(k * r, dsn, _BNN) + _bdot(c["intra"], do, _BTN)
            v_new = c["u"] - _bdot(c["w"], st, _BNN)
            del_ = jnp.sum(jnp.sum(dsn * st, axis=2, keepdims=True), axis=1, keepdims=True)
            dkr = _bdot(v_new, dsn, _BNT)
            dqe = _bdot(do, st, _BNT)
            dintra = _bdot(do, v_new, _BNT)
            dw = -_bdot(dvn, st, _BNT)
            dqkd = jnp.where(incl, dintra, 0.0)
            dqk = dqkd * decay
            ddecay = dqkd * c["qk"]
            dq = dqe * e + _bdot(dqk, k, _BNN)
            dk = dkr * r + _bdot(dqk, q, _BTN)
            dtm = _bdot(dvn, c["vb"], _BNT) + _bdot(dw, c["kbe"], _BNT)
            dvb = _bdot(tm, dvn, _BTN)
            dkbe = _bdot(tm, dw, _BTN)
            dkb = dkbe * e
            de = jnp.sum(dqe * q, axis=2, keepdims=True) + jnp.sum(dkbe * c["kb"], axis=2, keepdims=True)
            da = -_pdot(tm, _pdot(dtm, tm, _BNT), _BTN)
            dlow = jnp.where(strict, da, 0.0)
            dkk = dlow * decay
            ddecay = ddecay + dlow * c["kk"]
            dkb = dkb + _bdot(dkk, k, _BNN)
            dk = dk + _bdot(dkk, c["kb"], _BTN) + dkb * c["beta"]
            dbeta = jnp.sum(dkb * k, axis=2, keepdims=True) + jnp.sum(dvb * v, axis=2, keepdims=True)
            dv = dvb * c["beta"]
            dd = ddecay * decay
            dgc = jnp.sum(dd, axis=2, keepdims=True) - _pdot(dd, c["ones"], _BTN)[:, :, :1]
            dr = jnp.sum(dkr * k, axis=2, keepdims=True)
            dgc = dgc + de * e - dr * r
            dgl = jnp.sum(dr * r, axis=1, keepdims=True) + del_ * el
            rowc = lax.broadcasted_iota(jnp.int32, (C, 1), 0)
            dgc = dgc + jnp.where(rowc == C - 1, dgl, 0.0)
            dg = _pdot(c["inclf"], dgc * c["ones"], _BTN)[:, :, :1]
            dx_ref[0, 0, rows, :] = dq.reshape(GC, DN_HD)
            dx_ref[0, 1, rows, :] = dk.reshape(GC, DN_HD)
            dx_ref[0, 2, rows, :] = dv.reshape(GC, DN_HD)
            lane = c["lane"]
            dgt_ref[0, rows, :] += (jnp.where(lane == h, dg, 0.0) + jnp.where(lane == h + DN_HEADS, dbeta, 0.0)).reshape(GC, 128)
            return 0

        lax.fori_loop(0, NC // G, group_c, 0)

    return _hosted_call(
        body, grid=(B, DN_HEADS),
        in_specs=[pl.BlockSpec((1, 3, T, 128), lambda b, h: (b, 0, 0, h)), pl.BlockSpec((1, T, 128), lambda b, h: (b, 0, 0)),
                  pl.BlockSpec((1, 1, NC * DN_HD, DN_HD), lambda b, h: (b, h, 0, 0)), pl.BlockSpec((1, 1, T, C), lambda b, h: (b, h, 0, 0)),
                  pl.BlockSpec((1, T, 128), lambda b, h: (b, 0, h))],
        out_specs=[pl.BlockSpec((1, 3, T, 128), lambda b, h: (b, 0, 0, h)), pl.BlockSpec((1, T, 128), lambda b, h: (b, 0, 0))],
        out_shape=[_sds((B, 3, T, DN_W), F32), _sds((B, T, 128), F32)],
        scratch_shapes=[pltpu.VMEM((DN_HD, DN_HD), F32)] + [pltpu.VMEM((NC * DN_HD, DN_HD), F32)] * 2
        + [pltpu.VMEM((NC, 1, 128), F32), pltpu.VMEM((NC * DN_HD, DN_HD), F32)],
        name=name, args=(qkv, gates, states, tms, dob), exch=exch)


def ada_fwd(c_all, w_ada, b_sl, name):
    L, D, W = w_ada.shape
    NBt = c_all.shape[0]

    def body(c_ref, w_ref, b_ref, o_ref):
        cv = c_ref[...]
        o_ref[0] = _pdot(cv * _sigmoid(cv), w_ref[0]) + b_ref[0]

    return pl.pallas_call(
        body, grid=(L,),
        in_specs=[pl.BlockSpec((NBt, D), lambda l: (0, 0)), pl.BlockSpec((1, D, W), lambda l: (l, 0, 0)), pl.BlockSpec((1, 1, W), lambda l: (l, 0, 0))],
        out_specs=pl.BlockSpec((1, NBt, W), lambda l: (l, 0, 0)),
        out_shape=_sds((L, NBt, W), F32),
        compiler_params=_cp(("parallel",)), name=name,
    )(c_all, w_ada, b_sl)


def ada_bwd(c_all, dmod_cols, name):
    L, NBt, W = dmod_cols.shape
    D = c_all.shape[1]

    def body(c_ref, d_ref, o_ref):
        cv = c_ref[...]
        o_ref[0] = _pdot(cv * _sigmoid(cv), d_ref[0], _TN)

    return pl.pallas_call(
        body, grid=(L,),
        in_specs=[pl.BlockSpec((NBt, D), lambda l: (0, 0)), pl.BlockSpec((1, NBt, W), lambda l: (l, 0, 0))],
        out_specs=pl.BlockSpec((1, D, W), lambda l: (l, 0, 0)),
        out_shape=_sds((L, D, W), F32),
        compiler_params=_cp(("parallel",)), name=name,
    )(c_all, dmod_cols)


def adamw(partials, w, m, v, name):
    L, R, C = w.shape
    per_layer = isinstance(partials, (list, tuple))
    plist = list(partials) if per_layer else [partials]
    P = plist[0].shape[0]
    tr = _pick(R, 256)

    def body(*refs):
        p_refs = refs[:len(plist)]
        w_ref, m_ref, v_ref, g_ref, d_ref, nm_ref, nv_ref = refs[len(plist):]

        def run(read):
            g = read(0).astype(F32)
            for i in range(1, P):
                g = g + read(i).astype(F32)
            nm = ADAM_B1 * m_ref[0] + (1.0 - ADAM_B1) * g
            nv = ADAM_B2 * v_ref[0] + (1.0 - ADAM_B2) * (g * g)
            m_hat = nm / (1.0 - ADAM_B1 ** ADAM_STEP)
            v_hat = nv / (1.0 - ADAM_B2 ** ADAM_STEP)
            g_ref[0] = g
            d_ref[0] = -ADAM_LR * (m_hat / (jnp.sqrt(v_hat) + ADAM_EPS) + ADAM_WD * w_ref[0])
            nm_ref[0] = nm
            nv_ref[0] = nv

        if per_layer:
            for l in range(L):
                @pl.when(pl.program_id(0) == l)
                def _(l=l):
                    run(lambda i: p_refs[l][i])
        else:
            run(lambda i: p_refs[0][i, 0])

    tile = pl.BlockSpec((1, tr, C), lambda l, i: (l, i, 0))
    if per_layer:
        p_specs = [pl.BlockSpec((P, tr, C), lambda l, i, k=k: (0, jnp.where(l == k, i, 0), 0)) for k in range(L)]
    else:
        p_specs = [pl.BlockSpec((P, 1, tr, C), lambda l, i: (0, l, i, 0))]
    return pl.pallas_call(
        body, grid=(L, R // tr),
        in_specs=p_specs + [tile, tile, tile],
        out_specs=[tile] * 4, out_shape=[_sds((L, R, C), F32)] * 4,
        compiler_params=_cp(("arbitrary", "arbitrary")), name=name,
    )(*plist, w, m, v)


def _coords():
    return lax.axis_index("x"), lax.axis_index("y"), lax.axis_index("c")


def all_gather(x, name):
    return all_gather_many([x], name)[0]


def all_gather_many(xs, name):
    any_spec = pl.BlockSpec(memory_space=pl.ANY)
    n = len(xs)

    def body(*refs):
        x_refs, out_refs = refs[:n], refs[n:2 * n]
        send_sems, recv_sems, local_sems = refs[2 * n:]
        x_, y_, c_ = _coords()
        me, sibling = (x_, y_, c_), (x_, y_, 1 - c_)
        chips = [(1 - x_, y_), (x_, 1 - y_), (1 - x_, 1 - y_)]

        def copy(a, k, block, to, own=False):
            px, py, pc = block
            rows = out_refs[a].at[4 * px + 2 * py + pc]
            return pltpu.make_async_remote_copy(
                src_ref=x_refs[a] if own else rows, dst_ref=rows,
                send_sem=send_sems.at[7 * a + k], recv_sem=recv_sems.at[7 * a + k],
                device_id=to, device_id_type=pl.DeviceIdType.MESH)

        arrays = range(n)
        mine = [pltpu.make_async_copy(x_refs[a], out_refs[a].at[4 * x_ + 2 * y_ + c_], local_sems.at[a]) for a in arrays]
        first = [copy(a, 0, me, sibling, own=True) for a in arrays]
        first += [copy(a, 1 + j, me, (*chip, c_), own=True) for a in arrays for j, chip in enumerate(chips)]
        for cp in mine + first:
            cp.start()
        passed = []
        for a in arrays:
            for j, chip in enumerate(chips):
                copy(a, 1 + j, (*chip, c_), me).wait_recv()
                passed.append(copy(a, 4 + j, (*chip, c_), sibling))
                passed[-1].start()
        for a in arrays:
            copy(a, 0, sibling, me).wait_recv()
            for j, chip in enumerate(chips):
                copy(a, 4 + j, (*chip, 1 - c_), me).wait_recv()
        for cp in first + passed:
            cp.wait_send()
        for cp in mine:
            cp.wait()

    return pl.pallas_call(
        body, out_shape=[_sds((N_DEV,) + x.shape, x.dtype) for x in xs],
        in_specs=[any_spec] * n, out_specs=[any_spec] * n,
        scratch_shapes=[pltpu.SemaphoreType.DMA((7 * n,)), pltpu.SemaphoreType.DMA((7 * n,)), pltpu.SemaphoreType.DMA((n,))],
        name=name,
    )(*xs)


def all_to_all(x, name):
    any_spec = pl.BlockSpec(memory_space=pl.ANY)

    def body(x_ref, out_ref, send_sems, recv_sems, local_sem):
        x_, y_, c_ = _coords()
        me = 4 * x_ + 2 * y_ + c_
        mine = pltpu.make_async_copy(x_ref.at[me], out_ref.at[me], local_sem)
        mine.start()
        copies = []
        for k in range(1, N_DEV):
            px = 1 - x_ if k & 4 else x_
            py = 1 - y_ if k & 2 else y_
            pc = 1 - c_ if k & 1 else c_
            peer = 4 * px + 2 * py + pc
            copies.append((pltpu.make_async_remote_copy(
                src_ref=x_ref.at[peer], dst_ref=out_ref.at[me],
                send_sem=send_sems.at[k - 1], recv_sem=recv_sems.at[k - 1],
                device_id=(px, py, pc), device_id_type=pl.DeviceIdType.MESH), peer))
        for cp, _ in copies:
            cp.start()
        for k, (cp, peer) in enumerate(copies):
            pltpu.make_async_remote_copy(
                src_ref=x_ref.at[peer], dst_ref=out_ref.at[peer],
                send_sem=send_sems.at[k], recv_sem=recv_sems.at[k],
                device_id=(x_, y_, c_), device_id_type=pl.DeviceIdType.MESH).wait_recv()
        for cp, _ in copies:
            cp.wait_send()
        mine.wait()

    return pl.pallas_call(
        body, out_shape=_sds(x.shape, x.dtype),
        in_specs=[any_spec], out_specs=any_spec,
        scratch_shapes=[pltpu.SemaphoreType.DMA((7,)), pltpu.SemaphoreType.DMA((7,)), pltpu.SemaphoreType.DMA],
        name=name,
    )(x)


def _rows128(a):
    return a.reshape(-1, 128)


def _pad_lanes(a):
    return jnp.pad(a, ((0, 0), (0, 128 - a.shape[1])))


def kernel(x, c, w_ada, b_ada, norm_mix, norm_mlp, w_in, sb_q_norm, sb_k_norm, conv_w, a_log, dt_bias, dn_out_norm, w_out, w_ff1, w_ff2, loss_target, m_w_ada, m_b_ada, m_norm_mix, m_norm_mlp, m_w_in, m_sb_q_norm, m_sb_k_norm, m_conv_w, m_a_log, m_dt_bias, m_dn_out_norm, m_w_out, m_w_ff1, m_w_ff2, v_w_ada, v_b_ada, v_norm_mix, v_norm_mlp, v_w_in, v_sb_q_norm, v_sb_k_norm, v_conv_w, v_a_log, v_dt_bias, v_dn_out_norm, v_w_out, v_w_ff1, v_w_ff2):
    B, T, D = x.shape
    L = w_ada.shape[0]
    N = B * T
    FF = w_ff1.shape[2] * N_DEV
    WA = w_ada.shape[2]
    CS = conv_w.shape[2]
    me = 4 * lax.axis_index("x") + 2 * lax.axis_index("y") + lax.axis_index("c")
    tm = _pick(T, 1024)

    wb = [w.astype(BF16) for w in (w_in, w_out, w_ff1, w_ff2)]

    def assemble(lands):
        win_g, wout_g, w1_g, w2_g = lands
        return (jnp.pad(win_g.transpose(1, 0, 2).reshape(D, IN_W), ((0, 0), (0, IN_WP - IN_W))),
                wout_g.reshape(SB_W + DN_W, D), w1_g.transpose(1, 0, 2).reshape(D, FF), w2_g.reshape(FF, D))

    *lands0, conv_g, c_g = all_gather_many([w[0] for w in wb] + [conv_w, c], "comm_gather_first")
    weights = [assemble(lands0)]
    conv_full = conv_g.transpose(1, 2, 0, 3).reshape(L, CONV_K, 3 * DN_W)

    c_all = c_g.reshape(N_DEV * B, D)
    b_sl = lax.dynamic_slice_in_dim(b_ada, me * WA, WA, axis=1).reshape(L, 1, WA)
    mod_sh = ada_fwd(c_all, w_ada, b_sl, "ada_fwd")
    mod_g = all_gather(mod_sh, "comm_gather_mod")
    mod = lax.dynamic_slice_in_dim(mod_g, me * B, B, axis=2).transpose(1, 2, 0, 3).reshape(L, B, 6 * D)

    def mod_part(l, i):
        return mod[l, :, i * D:(i + 1) * D].reshape(B, 1, D)

    alog_row = _pad_lanes(a_log).reshape(L, 1, 128)
    dtb_row = _pad_lanes(dt_bias).reshape(L, 1, 128)

    def gate_epi(acc, xv, g):
        return acc, xv + g[0] * acc

    def relu2(a):
        r = jnp.maximum(a.astype(F32), 0.0)
        return r * r

    def times_gate(a, g):
        return a * g[0]

    tile_ij = lambda i, j, k: (i, j)

    saved = []
    xc = x
    for l in range(L):
        sh_a, sc_a, g_a, sh_m, sc_m, g_m = [mod_part(l, i) for i in range(6)]
        h = ln_mod_fwd(xc, norm_mix[l:l + 1], sc_a, sh_a, "ln_mod_fwd")
        W_in_l, W_out_l, W_1_l, W_2_l = weights[l]
        proj3 = matmul(h.reshape(N, D), W_in_l, mode="nn", name="mm_proj", tm=256)[0].reshape(B, T, IN_WP)
        qkv = gdn_pre_fwd(proj3, conv_full[l], "gdn_pre_fwd")
        gates = gdn_gates_fwd(proj3, alog_row[l], dtb_row[l], "gdn_gates_fwd")
        if l + 1 < L:
            o_a, tot, *lands = sb_attn_fwd(proj3, sb_q_norm[l:l + 1], sb_k_norm[l:l + 1], "sb_attn_fwd_gather",
                                           exch=("gather", [w[l + 1] for w in wb[:3]]))
            ob, states, tms, land_w2 = gdn_chunk_fwd(qkv, gates, "gdn_chunk_fwd_gather", exch=("gather", [wb[3][l + 1]]))
            weights.append(assemble(lands + [land_w2]))
        else:
            o_a, tot = sb_attn_fwd(proj3, sb_q_norm[l:l + 1], sb_k_norm[l:l + 1], "sb_attn_fwd")
            ob, states, tms = gdn_chunk_fwd(qkv, gates, "gdn_chunk_fwd")
        o_b = gdn_post_fwd(ob, proj3, dn_out_norm[l:l + 1], "gdn_post_fwd")
        mix = jnp.concatenate([o_a, o_b], axis=-1)
        y1, x_mid = matmul(
            mix.reshape(N, SB_W + DN_W), W_out_l, mode="nn", name="mm_out", out_dtypes=(BF16, F32), tm=tm, epi=gate_epi,
            extras=[(xc.reshape(N, D), (tm, _pick(D, 1024)), tile_ij),
                    (g_a, (1, 1, _pick(D, 1024)), lambda i, j, k: (i * tm // T, 0, j))])
        x_mid = x_mid.reshape(B, T, D)
        h2 = ln_mod_fwd(x_mid, norm_mlp[l:l + 1], sc_m, sh_m, "ln_mod_fwd")
        u = matmul(h2.reshape(N, D), W_1_l, mode="nn", name="mm_ff1", out_dtypes=(BF16,))[0]
        y2, x_out = matmul(
            u, W_2_l, mode="nn", name="mm_ff2", out_dtypes=(BF16, F32), tm=tm, a_fn=relu2, epi=gate_epi,
            extras=[(x_mid.reshape(N, D), (tm, _pick(D, 1024)), tile_ij),
                    (g_m, (1, 1, _pick(D, 1024)), lambda i, j, k: (i * tm // T, 0, j))])
        saved.append(dict(x=xc, h=h, proj3=proj3, tot=tot, qkv=qkv, gates=gates, states=states, tms=tms, ob=ob, mix=mix,
                          y1=y1, x_mid=x_mid, h2=h2, u=u, y2=y2))
        xc = x_out.reshape(B, T, D)

    dx, sq = loss_grad(xc, loss_target, "loss_grad")
    loss = lax.psum((0.5 / D) * jnp.sum(sq), AXES)

    g_win, g_wout, g_w1, g_w2, dmods, smalls, parts = [], [], [], [], [], [], []
    pending = None
    tk_tok = tm
    wi = w_in.shape[2]

    def shard_layer(gin, gout, g1, g2):
        return [None if gin is None else gin[:, :IN_W].reshape(D, N_DEV, wi).transpose(1, 0, 2), gout.reshape(N_DEV, w_out.shape[1], D),
                g1.reshape(D, N_DEV, w_ff1.shape[2]).transpose(1, 0, 2), g2.reshape(N_DEV, w_ff2.shape[1], D)]

    for l in reversed(range(L)):
        s = saved[l]
        W_in_l, W_out_l, W_1_l, W_2_l = weights[l]
        sh_a, sc_a, g_a, sh_m, sc_m, g_m = [mod_part(l, i) for i in range(6)]
        gate_k = lambda g, blk: (g, (1, 1, blk), lambda i, j, k: (i * tm // T, 0, k))
        gate_tok = lambda g, blk: (g, (1, 1, blk), lambda i, j, k: (k * tk_tok // T, 0, j))
        dx2 = dx.reshape(N, D)
        dg_m = rowsum_prod(dx, s["y2"].reshape(B, T, D), "rowsum_prod")
        du = matmul(dx2, W_2_l, mode="nt", name="mm_ff2_da", out_dtypes=(BF16,), tm=tm, a_fn=times_gate,
                    a_extras=[gate_k(g_m, _pick(D, 1024))],
                    epi=lambda acc, uv: (acc * (2.0 * jnp.maximum(uv, 0.0)),),
                    extras=[(s["u"], (tm, _pick(FF, 1024)), tile_ij)])[0]
        g_w2.append(matmul(s["u"], dx2, mode="tn", name="mm_ff2_dw", out_dtypes=(BF16,), tk=tk_tok, a_fn=relu2,
                           b_fn=times_gate, b_extras=[gate_tok(g_m, _pick(D, 1024))])[0])
        g_w1.append(matmul(s["h2"].reshape(N, D), du, mode="tn", name="mm_ff1_dw", out_dtypes=(BF16,))[0])
        dh2 = matmul(du, W_1_l, mode="nt", name="mm_ff1_da")[0]
        dx_mid, dgn_mlp, dsc_m, dsh_m = ln_mod_bwd(s["x_mid"], norm_mlp[l:l + 1], sc_m, dh2.reshape(B, T, D), dx, "ln_mod_bwd")
        dxm2 = dx_mid.reshape(N, D)
        dg_a = rowsum_prod(dx_mid, s["y1"].reshape(B, T, D), "rowsum_prod")
        dmix3 = matmul(dxm2, W_out_l, mode="nt", name="mm_out_da", tm=tm, a_fn=times_gate,
                       a_extras=[gate_k(g_a, _pick(D, 1024))])[0].reshape(B, T, SB_W + DN_W)
        g_wout.append(matmul(s["mix"].reshape(N, SB_W + DN_W), dxm2, mode="tn", name="mm_out_dw", out_dtypes=(BF16,),
                             tk=tk_tok, b_fn=times_gate, b_extras=[gate_tok(g_a, _pick(D, 1024))])[0])
        own = shard_layer(None, g_wout[-1], g_w1[-1], g_w2[-1])[1:] if l == 0 else []
        srcs = (pending or []) + own[:1]
        if srcs:
            dq_a, dk_a, dv_a, dgq, dgk, *lands = sb_attn_bwd(
                s["proj3"], sb_q_norm[l:l + 1], sb_k_norm[l:l + 1], s["tot"], dmix3,
                "sb_attn_bwd_scatter" if l else "sb_attn_bwd_scatter0", exch=("scatter", srcs))
            if pending:
                parts.append(lands[:4])
        else:
            dq_a, dk_a, dv_a, dgq, dgk = sb_attn_bwd(s["proj3"], sb_q_norm[l:l + 1], sb_k_norm[l:l + 1], s["tot"], dmix3, "sb_attn_bwd")
        dob, dz, dgn_dn = gdn_post_bwd(s["ob"], s["proj3"], dn_out_norm[l:l + 1], dmix3, "gdn_post_bwd")
        if own:
            dqkv, dgates, *lands_ff = gdn_chunk_bwd(s["qkv"], s["gates"], s["states"], s["tms"], dob, "gdn_chunk_bwd_scatter",
                                                    exch=("scatter", own[1:]))
            lands0 = lands[-1:] + lands_ff
        else:
            dqkv, dgates = gdn_chunk_bwd(s["qkv"], s["gates"], s["states"], s["tms"], dob, "gdn_chunk_bwd")
        d_dnqkv, dconv_b = gdn_pre_bwd(s["proj3"], conv_full[l], dqkv, "gdn_pre_bwd")
        d_ab, dalog, ddtb = gdn_gates_bwd(s["proj3"], alog_row[l], dtb_row[l], dgates, "gdn_gates_bwd")
        dproj = [a.reshape(N, a.shape[-1]) for a in (dq_a, dk_a, dv_a, d_dnqkv, dz, d_ab)]
        g_win.append(proj_bwd_weight(s["h"].reshape(N, D), dproj, "mm_proj_dw"))
        pending = shard_layer(g_win[-1], g_wout[-1], g_w1[-1], g_w2[-1])
        if l == 0:
            dh, land_win0 = proj_bwd_input(dproj, W_in_l, "mm_proj_da_scatter", exch=("scatter", pending[:1]))
        else:
            dh = proj_bwd_input(dproj, W_in_l, "mm_proj_da")[0]
        dx, dgn_mix, dsc_a, dsh_a = ln_mod_bwd(s["x"], norm_mix[l:l + 1], sc_a, dh.reshape(B, T, D), dx_mid, "ln_mod_bwd")
        dmods.append(jnp.concatenate([dsh_a, dsc_a, dg_a, dsh_m, dsc_m, dg_m], axis=-1).reshape(B, 6 * D))
        smalls.append(dict(norm_mix=dgn_mix, norm_mlp=dgn_mlp, sbq=dgq, sbk=dgk, alog=dalog, dtb=ddtb, dnorm=dgn_dn,
                           conv=jnp.sum(dconv_b, axis=0)))
    parts.append([land_win0] + lands0)
    for lst in (dmods, smalls, parts):
        lst.reverse()
    grad_x = dx

    def update2d(parts, w, m, v, name):
        return [o[0] for o in adamw(parts[:, None], w[None], m[None], v[None], name)]

    p_win, p_wout, p_w1, p_w2 = [[parts[l][i] for l in range(L)] for i in range(4)]
    r_win = adamw(p_win, w_in, m_w_in, v_w_in, "adamw_w_in")
    r_wout = adamw(p_wout, w_out, m_w_out, v_w_out, "adamw_w_out")
    r_w1 = adamw(p_w1, w_ff1, m_w_ff1, v_w_ff1, "adamw_w_ff1")
    r_w2 = adamw(p_w2, w_ff2, m_w_ff2, v_w_ff2, "adamw_w_ff2")

    def pack(f):
        return jnp.concatenate([
            _rows128(f("norm_mix")), _rows128(f("norm_mlp")), _rows128(f("sbq")), _rows128(f("sbk")),
            f("alog"), f("dtb"), f("dnorm"), _rows128(f("conv"))], axis=0)

    part = pack(lambda n: jnp.concatenate([sm[n] for sm in smalls], axis=0))
    dmod_g, part_g = all_gather_many([jnp.stack(dmods), part], "comm_gather_last")

    dmod_all = dmod_g.transpose(1, 0, 2, 3).reshape(L, N_DEV * B, 6 * D)
    g_wada = ada_bwd(c_all, lax.dynamic_slice_in_dim(dmod_all, me * WA, WA, axis=2), "ada_bwd")
    r_wada = adamw(g_wada[None], w_ada, m_w_ada, v_w_ada, "adamw_w_ada")
    r_bada = update2d(dmod_g.transpose(0, 2, 1, 3).reshape(N_DEV * B, L, 6 * D), b_ada, m_b_ada, v_b_ada, "adamw_b_ada")

    names = ["norm_mix", "norm_mlp", "sbq", "sbk", "alog", "dtb", "dnorm"]
    n_rep = part.shape[0] - L * CONV_K * 3 * DN_W // 128
    params = dict(norm_mix=(norm_mix, m_norm_mix, v_norm_mix), norm_mlp=(norm_mlp, m_norm_mlp, v_norm_mlp),
                  sbq=(sb_q_norm, m_sb_q_norm, v_sb_q_norm), sbk=(sb_k_norm, m_sb_k_norm, v_sb_k_norm),
                  alog=(a_log, m_a_log, v_a_log), dtb=(dt_bias, m_dt_bias, v_dt_bias),
                  dnorm=(dn_out_norm, m_dn_out_norm, v_dn_out_norm))

    def rows_of(n, a):
        return _pad_lanes(a) if n in ("alog", "dtb") else _rows128(a)

    packed = [jnp.concatenate([rows_of(n, params[n][i]) for n in names], axis=0) for i in range(3)]
    r_small = update2d(part_g[:, :n_rep], packed[0], packed[1], packed[2], "adamw_small")
    small_out = {}
    off = 0
    for n in names:
        w0 = params[n][0]
        nr = rows_of(n, w0).shape[0]
        vals = [o[off:off + nr] for o in r_small]
        small_out[n] = [(vv[:, :w0.shape[1]] if n in ("alog", "dtb") else vv.reshape(w0.shape)) for vv in vals]
        off += nr
    conv_parts = part_g[:, n_rep:].reshape(N_DEV, L, CONV_K, 3 * DN_W)
    r_conv = adamw(lax.dynamic_slice_in_dim(conv_parts, me * CS, CS, axis=3), conv_w, m_conv_w, v_conv_w, "adamw_conv")

    order = [r_wada, r_bada, small_out["norm_mix"], small_out["norm_mlp"], r_win, small_out["sbq"], small_out["sbk"],
             r_conv, small_out["alog"], small_out["dtb"], small_out["dnorm"], r_wout, r_w1, r_w2]
    outs = [loss, grad_x]
    for i in range(4):
        outs += [r[i] for r in order]
    return tuple(outs)
```

```python
import functools

import jax
import jax.numpy as jnp
from jax import lax
from jax.experimental import pallas as pl
from jax.experimental.pallas import tpu as pltpu

F32 = jnp.float32
BF16 = jnp.bfloat16
EPS = 1e-6
N_DEV = 8
AXES = ("x", "y", "c")

SB_HEADS, SB_HD = 8, 64
SB_W = SB_HEADS * SB_HD
SB_BLK = 128
DN_HEADS, DN_HD = 4, 128
DN_W = DN_HEADS * DN_HD
DN_C = 64
CONV_K = 4
IN_W = 3 * SB_W + 4 * DN_W + 2 * DN_HEADS
IN_WP = 3 * SB_W + 4 * DN_W + 128
COL_DNQKV = 3 * SB_W // 128
COL_Z = COL_DNQKV + 3 * DN_W // 128
COL_AB = COL_Z + DN_W // 128

ADAM_LR, ADAM_B1, ADAM_B2, ADAM_EPS, ADAM_WD, ADAM_STEP = 0.001, 0.9, 0.999, 1e-08, 0.01, 10

VMEM_LIMIT = 56 * 1024 * 1024


def _cp(sem):
    return pltpu.CompilerParams(dimension_semantics=sem, vmem_limit_bytes=VMEM_LIMIT)


def _pick(dim, pref):
    return pref if dim % pref == 0 else dim


def _sds(shape, dtype):
    return jax.ShapeDtypeStruct(tuple(shape), dtype)


_NN = (((1,), (0,)), ((), ()))
_NT = (((1,), (1,)), ((), ()))
_TN = (((0,), (0,)), ((), ()))


def _bdot(a, b, dims=_NN):
    return lax.dot_general(a.astype(BF16), b.astype(BF16), dims, preferred_element_type=F32)


def _split(a):
    hi = a.astype(BF16)
    lo = (a - hi.astype(F32)).astype(BF16)
    return hi, lo


def _pdot(a, b, dims=_NN):
    ah, al = _split(a)
    bh, bl = _split(b)
    d = functools.partial(lax.dot_general, dimension_numbers=dims, preferred_element_type=F32)
    return d(ah, bh) + (d(ah, bl) + d(al, bh))


def _sigmoid(x):
    return 1.0 / (1.0 + jnp.exp(-x))


def _softplus(x):
    return jnp.maximum(x, 0.0) + jnp.log(1.0 + jnp.exp(-jnp.abs(x)))


def matmul(a, b, *, mode, name, out_dtypes=(F32,), a_fn=None, a_extras=(), b_fn=None, b_extras=(),
           epi=None, extras=(), tm=1024, tn=1024, tk=1024):
    if mode == "tn":
        K, M = a.shape
    else:
        M, K = a.shape
    N = b.shape[0] if mode == "nt" else b.shape[1]
    tm, tn, tk = _pick(M, tm), _pick(N, tn), _pick(K, tk)
    nk = K // tk
    dims = {"nn": _NN, "nt": _NT, "tn": _TN}[mode]
    a_spec = pl.BlockSpec((tk, tm), lambda i, j, k: (k, i)) if mode == "tn" else pl.BlockSpec((tm, tk), lambda i, j, k: (i, k))
    b_spec = pl.BlockSpec((tn, tk), lambda i, j, k: (j, k)) if mode == "nt" else pl.BlockSpec((tk, tn), lambda i, j, k: (k, j))
    na, nb, ne, no = len(a_extras), len(b_extras), len(extras), len(out_dtypes)

    def body(*refs):
        a_ref, b_ref = refs[0], refs[1]
        ax = refs[2:2 + na]
        bx = refs[2 + na:2 + na + nb]
        ex = refs[2 + na + nb:2 + na + nb + ne]
        outs = refs[2 + na + nb + ne:2 + na + nb + ne + no]
        acc_ref = refs[-1]
        k = pl.program_id(2)

        @pl.when(k == 0)
        def _():
            acc_ref[...] = jnp.zeros_like(acc_ref)

        av = a_ref[...]
        if a_fn is not None:
            av = a_fn(av, *[r[...] for r in ax])
        bv = b_ref[...]
        if b_fn is not None:
            bv = b_fn(bv, *[r[...] for r in bx])
        acc_ref[...] += lax.dot_general(av.astype(BF16), bv.astype(BF16), dims, preferred_element_type=F32)

        @pl.when(k == nk - 1)
        def _():
            res = acc_ref[...]
            res = (res,) if epi is None else epi(res, *[r[...] for r in ex])
            for o_ref, r in zip(outs, res):
                o_ref[...] = r.astype(o_ref.dtype)

    xs = list(a_extras) + list(b_extras) + list(extras)
    return pl.pallas_call(
        body,
        grid=(M // tm, N // tn, nk),
        in_specs=[a_spec, b_spec] + [pl.BlockSpec(bs, im) for _, bs, im in xs],
        out_specs=[pl.BlockSpec((tm, tn), lambda i, j, k: (i, j)) for _ in out_dtypes],
        out_shape=[_sds((M, N), dt) for dt in out_dtypes],
        scratch_shapes=[pltpu.VMEM((tm, tn), F32)],
        compiler_params=_cp(("parallel", "parallel", "arbitrary")),
        name=name,
    )(a, b, *[x for x, _, _ in xs])


def proj_bwd_input(pieces, w, name, tm=512, tn=1024, exch=None):
    N, D = pieces[0].shape[0], w.shape[0]
    widths = [p.shape[1] for p in pieces]
    offs = [sum(widths[:i]) for i in range(len(widths))]
    tm, tn = _pick(N, tm), _pick(D, tn)

    def body(*refs):
        w_ref, o_ref = refs[len(pieces)], refs[len(pieces) + 1]
        acc = None
        for p_ref, off, wd in zip(refs, offs, widths):
            t = lax.dot_general(p_ref[...].astype(BF16), w_ref[:, off:off + wd], _NT, preferred_element_type=F32)
            acc = t if acc is None else acc + t
        o_ref[...] = acc

    return _hosted_call(
        body, grid=(N // tm, D // tn),
        in_specs=[pl.BlockSpec((tm, wd), lambda i, j: (i, 0)) for wd in widths] + [pl.BlockSpec((tn, sum(widths)), lambda i, j: (j, 0))],
        out_specs=[pl.BlockSpec((tm, tn), lambda i, j: (i, j))], out_shape=[_sds((N, D), F32)],
        scratch_shapes=[], name=name, args=(*pieces, w), exch=exch)


def proj_bwd_weight(h, pieces, name, tm=512, tk=512):
    N, D = h.shape
    widths = [p.shape[1] for p in pieces]
    offs = [sum(widths[:i]) for i in range(len(widths))]
    tm, tk = _pick(D, tm), _pick(N, tk)
    nk = N // tk

    def body(*refs):
        h_ref, o_ref, acc_ref = refs[0], refs[len(pieces) + 1], refs[len(pieces) + 2]
        k = pl.program_id(1)

        @pl.when(k == 0)
        def _():
            acc_ref[...] = jnp.zeros_like(acc_ref)

        hv = h_ref[...]
        for p_ref, off, wd in zip(refs[1:], offs, widths):
            acc_ref[:, off:off + wd] += lax.dot_general(hv, p_ref[...].astype(BF16), _TN, preferred_element_type=F32)

        @pl.when(k == nk - 1)
        def _():
            o_ref[...] = acc_ref[...].astype(o_ref.dtype)

    return pl.pallas_call(
        body, grid=(D // tm, nk),
        in_specs=[pl.BlockSpec((tk, tm), lambda i, k: (k, i))] + [pl.BlockSpec((tk, wd), lambda i, k: (k, 0)) for wd in widths],
        out_specs=pl.BlockSpec((tm, sum(widths)), lambda i, k: (i, 0)), out_shape=_sds((D, sum(widths)), BF16),
        scratch_shapes=[pltpu.VMEM((tm, sum(widths)), F32)],
        compiler_params=_cp(("parallel", "arbitrary")), name=name,
    )(h, *pieces)


def ln_mod_fwd(x, gain, sc, sh, name):
    B, T, D = x.shape
    tt = _pick(T, 512)

    def body(x_ref, g_ref, sc_ref, sh_ref, h_ref):
        xv = x_ref[0]
        r = lax.rsqrt(jnp.mean(xv * xv, axis=-1, keepdims=True) + EPS)
        h = (xv * r * g_ref[...]) * (1.0 + sc_ref[0]) + sh_ref[0]
        h_ref[0] = h.astype(h_ref.dtype)

    return pl.pallas_call(
        body, grid=(B, T // tt),
        in_specs=[pl.BlockSpec((1, tt, D), lambda b, t: (b, t, 0)), pl.BlockSpec((1, D), lambda b, t: (0, 0)),
                  pl.BlockSpec((1, 1, D), lambda b, t: (b, 0, 0)), pl.BlockSpec((1, 1, D), lambda b, t: (b, 0, 0))],
        out_specs=pl.BlockSpec((1, tt, D), lambda b, t: (b, t, 0)),
        out_shape=_sds((B, T, D), BF16),
        compiler_params=_cp(("parallel", "parallel")), name=name,
    )(x, gain, sc, sh)


def ln_mod_bwd(x, gain, sc, dh, dres, y, name):
    B, T, D = x.shape
    tt = _pick(T, 512)

    def body(x_ref, g_ref, sc_ref, dh_ref, dres_ref, y_ref, dx_ref, dg_ref, dsc_ref, dsh_ref, dgy_ref):
        b, t = pl.program_id(0), pl.program_id(1)
        xv, dhv = x_ref[0], dh_ref[0]
        g, s = g_ref[...], sc_ref[0]
        r = lax.rsqrt(jnp.mean(xv * xv, axis=-1, keepdims=True) + EPS)
        xn = xv * r
        dxn = dhv * (g * (1.0 + s))
        dx = dres_ref[0] + r * (dxn - xn * jnp.mean(dxn * xn, axis=-1, keepdims=True))
        dx_ref[0] = dx
        s1 = jnp.sum(dhv * xn, axis=0, keepdims=True)
        s2 = jnp.sum(dhv, axis=0, keepdims=True)

        @pl.when(t == 0)
        def _():
            dsc_ref[0] = jnp.zeros_like(s1)
            dsh_ref[0] = jnp.zeros_like(s1)
            dgy_ref[0] = jnp.zeros_like(s1)

        @pl.when((t == 0) & (b == 0))
        def _():
            dg_ref[...] = jnp.zeros_like(s1)

        dsc_ref[0] += s1 * g
        dsh_ref[0] += s2
        dg_ref[...] += s1 * (1.0 + s)
        dgy_ref[0] += jnp.sum(dx * y_ref[0], axis=0, keepdims=True)

    tile = pl.BlockSpec((1, tt, D), lambda b, t: (b, t, 0))
    row = pl.BlockSpec((1, D), lambda b, t: (0, 0))
    brow = pl.BlockSpec((1, 1, D), lambda b, t: (b, 0, 0))
    return pl.pallas_call(
        body, grid=(B, T // tt),
        in_specs=[tile, row, brow, tile, tile, tile],
        out_specs=[tile, row, brow, brow, brow],
        out_shape=[_sds((B, T, D), F32), _sds((1, D), F32)] + [_sds((B, 1, D), F32)] * 3,
        compiler_params=_cp(("arbitrary", "arbitrary")), name=name,
    )(x, gain, sc, dh, dres, y)


def rowsum_prod(a, b, name):
    B, T, D = a.shape
    tt = _pick(T, 512)

    def body(a_ref, b_ref, o_ref):
        @pl.when(pl.program_id(1) == 0)
        def _():
            o_ref[...] = jnp.zeros_like(o_ref)

        o_ref[0] += jnp.sum(a_ref[0] * b_ref[0], axis=0, keepdims=True)

    tile = pl.BlockSpec((1, tt, D), lambda b, t: (b, t, 0))
    return pl.pallas_call(
        body, grid=(B, T // tt), in_specs=[tile, tile],
        out_specs=pl.BlockSpec((1, 1, D), lambda b, t: (b, 0, 0)),
        out_shape=_sds((B, 1, D), F32),
        compiler_params=_cp(("parallel", "arbitrary")), name=name,
    )(a, b)


def loss_grad(y, tgt, name):
    B, T, D = y.shape
    tt = _pick(T, 512)

    def body(y_ref, t_ref, dy_ref, s_ref):
        @pl.when((pl.program_id(0) == 0) & (pl.program_id(1) == 0))
        def _():
            s_ref[...] = jnp.zeros_like(s_ref)

        e = y_ref[0] - t_ref[0]
        dy_ref[0] = e * (1.0 / D)
        s_ref[...] += jnp.sum(e * e, axis=0, keepdims=True)

    tile = pl.BlockSpec((1, tt, D), lambda b, t: (b, t, 0))
    return pl.pallas_call(
        body, grid=(B, T // tt), in_specs=[tile, tile],
        out_specs=[tile, pl.BlockSpec((1, D), lambda b, t: (0, 0))],
        out_shape=[_sds((B, T, D), F32), _sds((1, D), F32)],
        compiler_params=_cp(("arbitrary", "arbitrary")), name=name,
    )(y, tgt)


def _sb_group(nb):
    return 4 if nb % 4 == 0 else (2 if nb % 2 == 0 else 1)


def _sb_qrows(t, kw):
    return 256 if (t % 256 == 0 and kw % 256 == 0) else SB_BLK


def _diag_step(sblock, sj, carry, thr, qb, kw):
    half = kw // 2
    if half % SB_BLK or half < qb:
        return sblock(sj, carry, True)
    return lax.cond(thr + qb <= half, lambda c: sblock(sj, c, True, half), lambda c: sblock(sj, c, True), carry)


def _tri_sum(x, tri):
    return lax.dot_general(x.astype(BF16), tri, _NN, preferred_element_type=F32)


def _tri2(cond):
    return cond.astype(BF16)


def sb_attn_fwd(proj3, gq, gk, name, exch=None):
    B, T, _ = proj3.shape
    NB = T // SB_BLK
    G = _sb_group(NB)
    KW = G * SB_BLK
    QB = _sb_qrows(T, KW)
    scale = SB_HD ** -0.5

    def body(q_ref, k_ref, v_ref, gq_ref, gk_ref, o_ref, tot_ref, qn_s, kn_s, v_s):
        row_io = lax.broadcasted_iota(jnp.int32, (SB_BLK, SB_BLK), 0)
        col_io = lax.broadcasted_iota(jnp.int32, (SB_BLK, SB_BLK), 1)
        tri = _tri2(row_io > col_io)

        def prep(i, _):
            rows = pl.ds(pl.multiple_of(i * SB_BLK, SB_BLK), SB_BLK)
            for hh in range(2):
                sl = slice(hh * SB_HD, (hh + 1) * SB_HD)
                q = q_ref[0, rows, sl]
                k = k_ref[0, rows, sl]
                qn_s[hh, rows, :] = (q * lax.rsqrt(jnp.mean(q * q, -1, keepdims=True) + EPS) * (gq_ref[...] * scale)).astype(BF16)
                kn_s[hh, rows, :] = (k * lax.rsqrt(jnp.mean(k * k, -1, keepdims=True) + EPS) * gk_ref[...]).astype(BF16)
                v_s[hh, rows, :] = v_ref[0, rows, sl].astype(BF16)
            return 0

        lax.fori_loop(0, NB, prep, 0)

        diff_w = (lax.broadcasted_iota(jnp.int32, (QB, KW), 1)
                  - lax.broadcasted_iota(jnp.int32, (QB, KW), 0))

        def qblock(i, _):
            rows = pl.ds(pl.multiple_of(i * QB, QB), QB)
            qn = [qn_s[hh, rows, :] for hh in range(2)]
            nsj = ((i + 1) * QB - 1) // KW + 1

            thr = i * QB - (nsj - 1) * KW

            def sblock(sj, carry, masked, width=KW):
                g = width // SB_BLK
                cols = pl.ds(pl.multiple_of(sj * KW, KW), width)
                mask = diff_w[:, :width] < thr
                zs = [lax.dot_general(qn[hh], kn_s[hh, cols, :], _NT, preferred_element_type=F32) for hh in range(2)]
                lgs, lss = [], []
                for hh in range(2):
                    sp = _softplus(zs[hh])
                    lgs.append(jnp.where(mask, -sp, 0.0) if masked else -sp)
                    lss.append(zs[hh] - sp)
                blocks = [lgs[hh][:, s * SB_BLK:(s + 1) * SB_BLK] for hh in range(2) for s in range(g)]
                ts_all = _tri_sum(jnp.concatenate(blocks, axis=0), tri)
                atts, css = [], []
                for hh in range(2):
                    cs = carry[hh][1]
                    ps = []
                    for s in reversed(range(g)):
                        n = hh * g + s
                        ts = ts_all[n * QB:(n + 1) * QB]
                        ps.append(lss[hh][:, s * SB_BLK:(s + 1) * SB_BLK] + ts + cs)
                        cs = cs + (ts[:, :1] + blocks[n][:, :1])
                    p = ps[0] if g == 1 else jnp.concatenate(ps[::-1], axis=1)
                    att = jnp.exp(p)
                    atts.append((jnp.where(mask, att, 0.0) if masked else att).astype(BF16))
                    css.append(cs)
                return tuple((carry[hh][0] + lax.dot_general(atts[hh], v_s[hh, cols, :], _NN, preferred_element_type=F32), css[hh])
                             for hh in range(2))

            init = (jnp.zeros((QB, SB_HD), F32), jnp.zeros((QB, 1), F32))
            res = _diag_step(sblock, nsj - 1, (init, init), thr, QB, KW)
            res = lax.fori_loop(0, nsj - 1, lambda jj, c: sblock(nsj - 2 - jj, c, False), res)
            for hh in range(2):
                o_ref[0, rows, hh * SB_HD:(hh + 1) * SB_HD] = res[hh][0].astype(o_ref.dtype)
                tot_ref[0, hh, rows, :] = res[hh][1]
            return 0

        lax.fori_loop(0, T // QB, qblock, 0)

    blk = lambda off: pl.BlockSpec((1, T, 128), lambda b, p: (b, 0, off + p))
    grow = pl.BlockSpec((1, SB_HD), lambda b, p: (0, 0))
    return _hosted_call(
        body, grid=(B, SB_W // 128),
        in_specs=[blk(0), blk(SB_W // 128), blk(2 * SB_W // 128), grow, grow],
        out_specs=[pl.BlockSpec((1, T, 128), lambda b, p: (b, 0, p)), pl.BlockSpec((1, 2, T, 1), lambda b, p: (b, p, 0, 0))],
        out_shape=[_sds((B, T, SB_W), BF16), _sds((B, SB_HEADS, T, 1), F32)],
        scratch_shapes=[pltpu.VMEM((2, T, SB_HD), BF16)] * 3,
        name=name, args=(proj3, proj3, proj3, gq, gk), exch=exch)


def sb_attn_bwd(proj3, gq, gk, tot, dmix3, name, exch=None):
    B, T, _ = proj3.shape
    NB = T // SB_BLK
    G = _sb_group(NB)
    KW = G * SB_BLK
    QB = _sb_qrows(T, KW)
    scale = SB_HD ** -0.5

    def body(q_ref, k_ref, v_ref, gq_ref, gk_ref, tot_ref, do_ref, dq_ref, dk_ref, dv_ref, dgq_ref, dgk_ref,
             qn_s, kn_s, v_s, do_s, dqn_s, dkn_s, dv_s):
        row_io = lax.broadcasted_iota(jnp.int32, (SB_BLK, SB_BLK), 0)
        col_io = lax.broadcasted_iota(jnp.int32, (SB_BLK, SB_BLK), 1)
        tri = _tri2(row_io > col_io)
        trip = _tri2(row_io < col_io)

        @pl.when((pl.program_id(0) == 0) & (pl.program_id(1) == 0))
        def _():
            dgq_ref[...] = jnp.zeros_like(dgq_ref)
            dgk_ref[...] = jnp.zeros_like(dgk_ref)

        def prep(i, _):
            rows = pl.ds(pl.multiple_of(i * SB_BLK, SB_BLK), SB_BLK)
            for hh in range(2):
                sl = slice(hh * SB_HD, (hh + 1) * SB_HD)
                q = q_ref[0, rows, sl]
                k = k_ref[0, rows, sl]
                qn_s[hh, rows, :] = (q * lax.rsqrt(jnp.mean(q * q, -1, keepdims=True) + EPS) * (gq_ref[...] * scale)).astype(BF16)
                kn_s[hh, rows, :] = (k * lax.rsqrt(jnp.mean(k * k, -1, keepdims=True) + EPS) * gk_ref[...]).astype(BF16)
                v_s[hh, rows, :] = v_ref[0, rows, sl].astype(BF16)
                do_s[hh, rows, :] = do_ref[0, rows, sl].astype(BF16)
            return 0

        lax.fori_loop(0, NB, prep, 0)
        dkn_s[...] = jnp.zeros_like(dkn_s)
        dv_s[...] = jnp.zeros_like(dv_s)

        diff_w = (lax.broadcasted_iota(jnp.int32, (QB, KW), 1)
                  - lax.broadcasted_iota(jnp.int32, (QB, KW), 0))

        def qblock(i, _):
            rows = pl.ds(pl.multiple_of(i * QB, QB), QB)
            qn = [qn_s[hh, rows, :] for hh in range(2)]
            dov = [do_s[hh, rows, :] for hh in range(2)]
            tot = [tot_ref[0, hh, rows, :] for hh in range(2)]
            nsj = ((i + 1) * QB - 1) // KW + 1
            qnT = [qn[hh].astype(F32).T.astype(BF16) for hh in range(2)]
            doT = [dov[hh].astype(F32).T.astype(BF16) for hh in range(2)]

            thr = i * QB - (nsj - 1) * KW

            def sblock(sj, carry, masked, width=KW):
                g = width // SB_BLK
                cols = pl.ds(pl.multiple_of(sj * KW, KW), width)
                mask = diff_w[:, :width] < thr
                hs = range(2)
                kns = [kn_s[hh, cols, :] for hh in hs]
                zs = [lax.dot_general(qn[hh], kns[hh], _NT, preferred_element_type=F32) for hh in hs]
                datts = [lax.dot_general(dov[hh], v_s[hh, cols, :], _NT, preferred_element_type=F32) for hh in hs]
                lgs, lss = [], []
                for hh in hs:
                    sp = _softplus(zs[hh])
                    lgs.append(jnp.where(mask, -sp, 0.0) if masked else -sp)
                    lss.append(zs[hh] - sp)
                blocks = [lgs[hh][:, s * SB_BLK:(s + 1) * SB_BLK] for hh in hs for s in range(g)]
                ts_all = _tri_sum(jnp.concatenate(blocks, axis=0), tri)
                atts, dps, cums = [], [], []
                for hh in hs:
                    cum = carry[hh][1]
                    ps = []
                    for s in range(g):
                        n = hh * g + s
                        ts = ts_all[n * QB:(n + 1) * QB]
                        cum = cum + (ts[:, :1] + blocks[n][:, :1])
                        ps.append(lss[hh][:, s * SB_BLK:(s + 1) * SB_BLK] + ts + (tot[hh] - cum))
                    p = ps[0] if g == 1 else jnp.concatenate(ps, axis=1)
                    att = jnp.exp(p)
                    att = jnp.where(mask, att, 0.0) if masked else att
                    atts.append(att.astype(BF16))
                    dps.append(att * datts[hh])
                    cums.append(cum)
                dblocks = [dps[hh][:, s * SB_BLK:(s + 1) * SB_BLK] for hh in hs for s in range(g)]
                tp_all = _tri_sum(jnp.concatenate(dblocks, axis=0), trip)
                dzs, cdps = [], []
                for hh in hs:
                    cdp = carry[hh][2]
                    dls = []
                    for s in range(g):
                        n = hh * g + s
                        tp = tp_all[n * QB:(n + 1) * QB]
                        dls.append(tp + cdp)
                        cdp = cdp + (tp[:, SB_BLK - 1:] + dblocks[n][:, SB_BLK - 1:])
                    dlg = dls[0] if g == 1 else jnp.concatenate(dls, axis=1)
                    dz = dps[hh] - jnp.exp(lss[hh]) * (dps[hh] + dlg)
                    dzs.append((jnp.where(mask, dz, 0.0) if masked else dz).astype(BF16))
                    cdps.append(cdp)
                new = []
                for hh in hs:
                    dq = carry[hh][0] + lax.dot_general(dzs[hh], kns[hh], _NN, preferred_element_type=F32)
                    dkn_s[hh, :, cols] += lax.dot_general(qnT[hh], dzs[hh], _NN, preferred_element_type=F32)
                    dv_s[hh, :, cols] += lax.dot_general(doT[hh], atts[hh], _NN, preferred_element_type=F32)
                    new.append((dq, cums[hh], cdps[hh]))
                return tuple(new)

            z1 = jnp.zeros((QB, 1), F32)
            init = (jnp.zeros((QB, SB_HD), F32), z1, z1)
            res = lax.fori_loop(0, nsj - 1, lambda sj, c: sblock(sj, c, False), (init, init))
            res = _diag_step(sblock, nsj - 1, res, thr, QB, KW)
            for hh in range(2):
                dqn_s[hh, rows, :] = res[hh][0]
            return 0

        lax.fori_loop(0, T // QB, qblock, 0)

        def fin(i, carry):
            aq, ak = carry
            rows = pl.ds(pl.multiple_of(i * SB_BLK, SB_BLK), SB_BLK)
            for hh in range(2):
                sl = slice(hh * SB_HD, (hh + 1) * SB_HD)
                for src_ref, g_ref, out_ref, mult, which in ((q_ref, gq_ref, dq_ref, scale, 0), (k_ref, gk_ref, dk_ref, 1.0, 1)):
                    xr = src_ref[0, rows, sl]
                    r = lax.rsqrt(jnp.mean(xr * xr, -1, keepdims=True) + EPS)
                    dy = (dqn_s[hh, rows, :] if which == 0 else dkn_s[hh, :, rows].T) * mult
                    u = dy * g_ref[...]
                    out_ref[0, rows, sl] = r * u - xr * (r * r * r) * jnp.mean(u * xr, -1, keepdims=True)
                    part = jnp.sum(dy * xr * r, axis=0, keepdims=True)
                    if which == 0:
                        aq = aq + part
                    else:
                        ak = ak + part
                dv_ref[0, rows, sl] = dv_s[hh, :, rows].T
            return aq, ak

        z64 = jnp.zeros((1, SB_HD), F32)
        aq, ak = lax.fori_loop(0, NB, fin, (z64, z64))
        dgq_ref[...] += aq
        dgk_ref[...] += ak

    blk = lambda off: pl.BlockSpec((1, T, 128), lambda b, p: (b, 0, off + p))
    grow = pl.BlockSpec((1, SB_HD), lambda b, p: (0, 0))
    return _hosted_call(
        body, grid=(B, SB_W // 128),
        in_specs=[blk(0), blk(SB_W // 128), blk(2 * SB_W // 128), grow, grow,
                  pl.BlockSpec((1, 2, T, 1), lambda b, p: (b, p, 0, 0)), blk(0)],
        out_specs=[blk(0), blk(0), blk(0), grow, grow],
        out_shape=[_sds((B, T, SB_W), F32)] * 3 + [_sds((1, SB_HD), F32)] * 2,
        scratch_shapes=[pltpu.VMEM((2, T, SB_HD), BF16)] * 4 + [pltpu.VMEM((2, T, SB_HD), F32)] + [pltpu.VMEM((2, SB_HD, T), F32)] * 2,
        name=name, args=(proj3, proj3, proj3, gq, gk, tot, dmix3), exch=exch)


def _exch_copies(kind, src_refs, land_refs, sems, with_arrivals=True):
    x_, y_, c_ = _coords()
    me = 4 * x_ + 2 * y_ + c_
    local, go, arrive = [], [], []
    for a, (src, land) in enumerate(zip(src_refs, land_refs)):
        send, recv, loc = sems[3 * a:3 * a + 3]
        local.append(pltpu.make_async_copy(src if kind == "gather" else src.at[me], land.at[me], loc))
        for k in range(1, N_DEV):
            px = 1 - x_ if k & 4 else x_
            py = 1 - y_ if k & 2 else y_
            pc = 1 - c_ if k & 1 else c_
            peer = 4 * px + 2 * py + pc
            out = src if kind == "gather" else src.at[peer]
            mk = functools.partial(pltpu.make_async_remote_copy, send_sem=send.at[k - 1], recv_sem=recv.at[k - 1],
                                   device_id=(px, py, pc), device_id_type=pl.DeviceIdType.MESH)
            go.append(mk(src_ref=out, dst_ref=land.at[me]))
            if with_arrivals:
                arrive.append(mk(src_ref=out, dst_ref=land.at[peer]))
    return local, go, arrive


def _hosted_call(body, *, grid, in_specs, out_specs, out_shape, scratch_shapes, name, args, exch=None):
    sem = ("arbitrary",) * len(grid)
    if exch is None:
        return pl.pallas_call(body, grid=grid, in_specs=in_specs, out_specs=out_specs, out_shape=out_shape,
                              scratch_shapes=scratch_shapes, compiler_params=_cp(sem), name=name)(*args)
    kind, srcs = exch
    ns, n_in, n_out, n_scr = len(srcs), len(in_specs), len(out_specs), len(scratch_shapes)
    lands = [_sds((N_DEV,) + s.shape if kind == "gather" else s.shape, s.dtype) for s in srcs]

    def wrapped(*refs):
        ins, src_refs = refs[:n_in], refs[n_in:n_in + ns]
        outs, land_refs = refs[n_in + ns:n_in + ns + n_out], refs[n_in + ns + n_out:n_in + 2 * ns + n_out]
        scr, sems = refs[n_in + 2 * ns + n_out:n_in + 2 * ns + n_out + n_scr], refs[n_in + 2 * ns + n_out + n_scr:]
        ids = [pl.program_id(d) for d in range(len(grid))]
        first = functools.reduce(lambda a, b: a & b, [i == 0 for i in ids])
        last = functools.reduce(lambda a, b: a & b, [i == g - 1 for i, g in zip(ids, grid)])

        @pl.when(first)
        def _():
            local, go, _ = _exch_copies(kind, src_refs, land_refs, sems, with_arrivals=False)
            for cp in local + go:
                cp.start()

        body(*ins, *outs, *scr)

        @pl.when(last)
        def _():
            local, go, arrive = _exch_copies(kind, src_refs, land_refs, sems)
            for cp in arrive:
                cp.wait_recv()
            for cp in go:
                cp.wait_send()
            for cp in local:
                cp.wait()

    any_spec = pl.BlockSpec(memory_space=pl.ANY)
    sems = [pltpu.SemaphoreType.DMA((N_DEV - 1,)), pltpu.SemaphoreType.DMA((N_DEV - 1,)), pltpu.SemaphoreType.DMA] * ns
    return pl.pallas_call(
        wrapped, grid=grid, in_specs=list(in_specs) + [any_spec] * ns, out_specs=list(out_specs) + [any_spec] * ns,
        out_shape=list(out_shape) + lands, scratch_shapes=list(scratch_shapes) + sems,
        compiler_params=_cp(sem), name=name)(*args, *srcs)


def _conv_silu(x, w, T):
    t_io = lax.broadcasted_iota(jnp.int32, x.shape, 0)
    xs = [x] + [jnp.where(t_io >= s, pltpu.roll(x, s, 0), 0.0) for s in range(1, CONV_K)]
    y = xs[0] * w[CONV_K - 1:CONV_K, :]
    for s in range(1, CONV_K):
        y = y + xs[s] * w[CONV_K - 1 - s:CONV_K - s, :]
    return y, y * _sigmoid(y), xs


def gdn_pre_fwd(proj3, conv_w, name):
    B, T, _ = proj3.shape
    qs = DN_HD ** -0.5

    def body(x_ref, w_ref, o_ref):
        kind = pl.program_id(1) // DN_HEADS
        _, s, _ = _conv_silu(x_ref[0], w_ref[...], T)
        n = lax.rsqrt(jnp.sum(s * s, axis=-1, keepdims=True) + EPS)
        c = jnp.where(kind == 0, qs, 1.0)
        o_ref[0, 0] = jnp.where(kind < 2, s * (n * c), s)

    return pl.pallas_call(
        body, grid=(B, 3 * DN_HEADS),
        in_specs=[pl.BlockSpec((1, T, 128), lambda b, j: (b, 0, COL_DNQKV + j)), pl.BlockSpec((CONV_K, 128), lambda b, j: (0, j))],
        out_specs=pl.BlockSpec((1, 1, T, 128), lambda b, j: (b, j // DN_HEADS, 0, j % DN_HEADS)),
        out_shape=_sds((B, 3, T, DN_W), F32),
        compiler_params=_cp(("parallel", "parallel")), name=name,
    )(proj3, conv_w)


def gdn_pre_bwd(proj3, conv_w, dqkv, name):
    B, T, _ = proj3.shape
    qs = DN_HD ** -0.5

    def body(x_ref, w_ref, d_ref, dx_ref, dw_ref):
        kind = pl.program_id(1) // DN_HEADS
        w = w_ref[...]
        y, s, xs = _conv_silu(x_ref[0], w, T)
        dout = d_ref[0, 0]
        n = lax.rsqrt(jnp.sum(s * s, axis=-1, keepdims=True) + EPS)
        c = jnp.where(kind == 0, qs, 1.0)
        dsn = c * (n * dout - s * (n * n * n) * jnp.sum(dout * s, axis=-1, keepdims=True))
        ds = jnp.where(kind < 2, dsn, dout)
        sg = _sigmoid(y)
        dy = ds * (sg * (1.0 + y * (1.0 - sg)))
        t_io = lax.broadcasted_iota(jnp.int32, dy.shape, 0)
        dx = dy * w[CONV_K - 1:CONV_K, :]
        dw_ref[0, CONV_K - 1:CONV_K, :] = jnp.sum(dy * xs[0], axis=0, keepdims=True)
        for sft in range(1, CONV_K):
            dx = dx + jnp.where(t_io < T - sft, pltpu.roll(dy, T - sft, 0), 0.0) * w[CONV_K - 1 - sft:CONV_K - sft, :]
            dw_ref[0, CONV_K - 1 - sft:CONV_K - sft, :] = jnp.sum(dy * xs[sft], axis=0, keepdims=True)
        dx_ref[0] = dx

    return pl.pallas_call(
        body, grid=(B, 3 * DN_HEADS),
        in_specs=[pl.BlockSpec((1, T, 128), lambda b, j: (b, 0, COL_DNQKV + j)), pl.BlockSpec((CONV_K, 128), lambda b, j: (0, j)),
                  pl.BlockSpec((1, 1, T, 128), lambda b, j: (b, j // DN_HEADS, 0, j % DN_HEADS))],
        out_specs=[pl.BlockSpec((1, T, 128), lambda b, j: (b, 0, j)), pl.BlockSpec((1, CONV_K, 128), lambda b, j: (b, 0, j))],
        out_shape=[_sds((B, T, 3 * DN_W), F32), _sds((B, CONV_K, 3 * DN_W), F32)],
        compiler_params=_cp(("parallel", "parallel")), name=name,
    )(proj3, conv_w, dqkv)


def gdn_gates_fwd(proj3, alog_row, dtb_row, name):
    B, T, _ = proj3.shape

    def body(x_ref, al_ref, dt_ref, o_ref):
        x = x_ref[0]
        lane = lax.broadcasted_iota(jnp.int32, x.shape, 1)
        g = -jnp.exp(al_ref[...]) * _softplus(x + dt_ref[...])
        o_ref[0] = jnp.where(lane < DN_HEADS, g, jnp.where(lane < 2 * DN_HEADS, _sigmoid(x), 0.0))

    row = pl.BlockSpec((1, 128), lambda b: (0, 0))
    return pl.pallas_call(
        body, grid=(B,),
        in_specs=[pl.BlockSpec((1, T, 128), lambda b: (b, 0, COL_AB)), row, row],
        out_specs=pl.BlockSpec((1, T, 128), lambda b: (b, 0, 0)),
        out_shape=_sds((B, T, 128), F32),
        compiler_params=_cp(("parallel",)), name=name,
    )(proj3, alog_row, dtb_row)


def gdn_gates_bwd(proj3, alog_row, dtb_row, dgates, name):
    B, T, _ = proj3.shape

    def body(x_ref, al_ref, dt_ref, d_ref, dx_ref, dal_ref, ddt_ref):
        @pl.when(pl.program_id(0) == 0)
        def _():
            dal_ref[...] = jnp.zeros_like(dal_ref)
            ddt_ref[...] = jnp.zeros_like(ddt_ref)

        x, d = x_ref[0], d_ref[0]
        lane = lax.broadcasted_iota(jnp.int32, x.shape, 1)
        a = x + dt_ref[...]
        na = -jnp.exp(al_ref[...])
        da = jnp.where(lane < DN_HEADS, d * na * _sigmoid(a), 0.0)
        bt = _sigmoid(x)
        dx_ref[0] = da + jnp.where((lane >= DN_HEADS) & (lane < 2 * DN_HEADS), d * bt * (1.0 - bt), 0.0)
        dal_ref[...] += jnp.sum(jnp.where(lane < DN_HEADS, d * na * _softplus(a), 0.0), axis=0, keepdims=True)
        ddt_ref[...] += jnp.sum(da, axis=0, keepdims=True)

    row = pl.BlockSpec((1, 128), lambda b: (0, 0))
    tile = pl.BlockSpec((1, T, 128), lambda b: (b, 0, 0))
    return pl.pallas_call(
        body, grid=(B,),
        in_specs=[pl.BlockSpec((1, T, 128), lambda b: (b, 0, COL_AB)), row, row, tile],
        out_specs=[tile, row, row],
        out_shape=[_sds((B, T, 128), F32), _sds((1, 128), F32), _sds((1, 128), F32)],
        compiler_params=_cp(("arbitrary",)), name=name,
    )(proj3, alog_row, dtb_row, dgates)


def gdn_post_fwd(ob, proj3, gain, name):
    B, T, _ = ob.shape

    def body(o_ref, z_ref, g_ref, out_ref):
        o, z = o_ref[0], z_ref[0]
        r = lax.rsqrt(jnp.mean(o * o, axis=-1, keepdims=True) + EPS)
        out_ref[0] = ((o * r * g_ref[...]) * (z * _sigmoid(z))).astype(out_ref.dtype)

    tile = pl.BlockSpec((1, T, 128), lambda b, h: (b, 0, h))
    return pl.pallas_call(
        body, grid=(B, DN_HEADS),
        in_specs=[tile, pl.BlockSpec((1, T, 128), lambda b, h: (b, 0, COL_Z + h)), pl.BlockSpec((1, 128), lambda b, h: (0, 0))],
        out_specs=tile, out_shape=_sds((B, T, DN_W), BF16),
        compiler_params=_cp(("parallel", "parallel")), name=name,
    )(ob, proj3, gain)


def gdn_post_bwd(ob, proj3, gain, dmix3, name):
    B, T, _ = ob.shape

    def body(o_ref, z_ref, g_ref, d_ref, do_ref, dz_ref, dg_ref):
        @pl.when((pl.program_id(0) == 0) & (pl.program_id(1) == 0))
        def _():
            dg_ref[...] = jnp.zeros_like(dg_ref)

        o, z, d, g = o_ref[0], z_ref[0], d_ref[0], g_ref[...]
        r = lax.rsqrt(jnp.mean(o * o, axis=-1, keepdims=True) + EPS)
        sg = _sigmoid(z)
        dn = d * (z * sg)
        dz_ref[0] = d * (o * r * g) * (sg * (1.0 + z * (1.0 - sg)))
        dg_ref[...] += jnp.sum(dn * o * r, axis=0, keepdims=True)
        u = dn * g
        do_ref[0] = r * u - o * (r * r * r) * jnp.mean(u * o, axis=-1, keepdims=True)

    tile = pl.BlockSpec((1, T, 128), lambda b, h: (b, 0, h))
    row = pl.BlockSpec((1, 128), lambda b, h: (0, 0))
    return pl.pallas_call(
        body, grid=(B, DN_HEADS),
        in_specs=[tile, pl.BlockSpec((1, T, 128), lambda b, h: (b, 0, COL_Z + h)), row,
                  pl.BlockSpec((1, T, 128), lambda b, h: (b, 0, SB_W // 128 + h))],
        out_specs=[tile, tile, row],
        out_shape=[_sds((B, T, DN_W), F32), _sds((B, T, DN_W), F32), _sds((1, 128), F32)],
        compiler_params=_cp(("arbitrary", "arbitrary")), name=name,
    )(ob, proj3, gain, dmix3)


def _tri_inv(low, ri, ci):
    m = (ri == ci).astype(F32) - jnp.where(((ri >> 1) == (ci >> 1)) & (ri > ci), low, 0.0)
    s = 2
    while s < DN_C:
        sh = s.bit_length()
        off = ((ri >> sh) == (ci >> sh)) & ((ri & (2 * s - 1)) >= s) & ((ci & (2 * s - 1)) < s)
        m = m - _pdot(m, _pdot(jnp.where(off, low, 0.0), m, _BNN), _BNN)
        s *= 2
    return m


_BNN = (((2,), (1,)), ((0,), (0,)))
_BNT = (((2,), (2,)), ((0,), (0,)))
_BTN = (((1,), (1,)), ((0,), (0,)))
DN_G = 16


def _chunk_common(q, k, v, gt, h, tm=None):
    C = DN_C
    G = q.shape[0]
    ri = lax.broadcasted_iota(jnp.int32, (C, C), 0)
    ci = lax.broadcasted_iota(jnp.int32, (C, C), 1)
    lane = lax.broadcasted_iota(jnp.int32, (C, 128), 1)
    incl, strict = ri >= ci, ri > ci
    g = jnp.sum(jnp.where(lane == h, gt, 0.0), axis=2, keepdims=True)
    beta = jnp.sum(jnp.where(lane == h + DN_HEADS, gt, 0.0), axis=2, keepdims=True)
    ones = jnp.ones((G, C, 128), F32)
    inclf = jnp.broadcast_to(incl.astype(F32), (G, C, C))
    gc = _pdot(inclf, g * ones, _BNN)[:, :, :1]
    gcr = _pdot(jnp.ones((G, C, C), F32), jnp.where(ri == ci, gc, 0.0), _BNN)
    decay = jnp.where(incl, jnp.exp(jnp.where(incl, gc - gcr, 0.0)), 0.0)
    e = jnp.exp(gc)
    kb, vb = k * beta, v * beta
    kk = _bdot(kb, k, _BNT)
    if tm is None:
        tm = _tri_inv(jnp.where(strict, kk * decay, 0.0), ri, ci)
    kbe = kb * e
    u = _bdot(tm, vb, _BNN)
    w = _bdot(tm, kbe, _BNN)
    qk = _bdot(q, k, _BNT)
    intra = jnp.where(incl, qk * decay, 0.0)
    gl = gc[:, C - 1:C, :]
    el = jnp.exp(gl)
    r = jnp.exp(gl - gc)
    return dict(lane=lane, incl=incl, inclf=inclf, strict=strict, beta=beta, decay=decay, e=e,
                kb=kb, vb=vb, kk=kk, tm=tm, kbe=kbe, u=u, w=w, qk=qk, intra=intra, el=el, r=r, ones=ones)


def gdn_chunk_fwd(qkv, gates, name, exch=None):
    B, _, T, _ = qkv.shape
    NC = T // DN_C

    G = DN_G if NC % DN_G == 0 else 1
    GC = G * DN_C

    GS = G * DN_HD

    def body(x_ref, gt_ref, o_ref, st_ref, tm_ref, s_s, p_s, b_s, qp_s, el_s):
        h = pl.program_id(1)

        def group_a(gi, _):
            rows = pl.ds(pl.multiple_of(gi * GC, GC), GC)
            srow = pl.ds(pl.multiple_of(gi * GS, GS), GS)
            q, k, v = [x_ref[0, i, rows, :].reshape(G, DN_C, DN_HD) for i in range(3)]
            c = _chunk_common(q, k, v, gt_ref[0, rows, :].reshape(G, DN_C, 128), h)
            kr = k * c["r"]
            tm_ref[0, 0, rows, :] = c["tm"].reshape(GC, DN_C)
            p_s[srow, :] = _bdot(kr, c["w"], _BTN).reshape(GS, DN_HD)
            b_s[srow, :] = _bdot(kr, c["u"], _BTN).reshape(GS, DN_HD)
            qp_s[rows, :] = (q * c["e"] - _bdot(c["intra"], c["w"], _BNN)).reshape(GC, DN_HD)
            o_ref[0, rows, :] = _bdot(c["intra"], c["u"], _BNN).reshape(GC, DN_HD)
            el_s[pl.ds(gi * G, G), :, :] = c["el"] * jnp.ones((G, 1, 128), F32)
            return 0

        lax.fori_loop(0, NC // G, group_a, 0)
        s_s[...] = jnp.zeros_like(s_s)

        def chunk(n, _):
            srow = pl.ds(pl.multiple_of(n * DN_HD, DN_HD), DN_HD)
            st = s_s[...]
            st_ref[0, 0, srow, :] = st
            s_s[...] = (st * el_s[n] + b_s[srow, :]) - _bdot(p_s[srow, :], st)
            return 0

        lax.fori_loop(0, NC, chunk, 0)

        def group_c(gi, _):
            rows = pl.ds(pl.multiple_of(gi * GC, GC), GC)
            srow = pl.ds(pl.multiple_of(gi * GS, GS), GS)
            st = st_ref[0, 0, srow, :].reshape(G, DN_HD, DN_HD)
            o_ref[0, rows, :] += _bdot(qp_s[rows, :].reshape(G, DN_C, DN_HD), st, _BNN).reshape(GC, DN_HD)
            return 0

        lax.fori_loop(0, NC // G, group_c, 0)

    return _hosted_call(
        body, grid=(B, DN_HEADS),
        in_specs=[pl.BlockSpec((1, 3, T, 128), lambda b, h: (b, 0, 0, h)), pl.BlockSpec((1, T, 128), lambda b, h: (b, 0, 0))],
        out_specs=[pl.BlockSpec((1, T, 128), lambda b, h: (b, 0, h)), pl.BlockSpec((1, 1, NC * DN_HD, DN_HD), lambda b, h: (b, h, 0, 0)),
                   pl.BlockSpec((1, 1, T, DN_C), lambda b, h: (b, h, 0, 0))],
        out_shape=[_sds((B, T, DN_W), F32), _sds((B, DN_HEADS, NC * DN_HD, DN_HD), F32), _sds((B, DN_HEADS, T, DN_C), F32)],
        scratch_shapes=[pltpu.VMEM((DN_HD, DN_HD), F32)] + [pltpu.VMEM((NC * DN_HD, DN_HD), F32)] * 2
        + [pltpu.VMEM((T, DN_HD), F32), pltpu.VMEM((NC, 1, 128), F32)],
        name=name, args=(qkv, gates), exch=exch)


def gdn_chunk_bwd(qkv, gates, states, tms, dob, name, exch=None):
    B, _, T, _ = qkv.shape
    NC = T // DN_C
    C = DN_C

    G = DN_G if NC % DN_G == 0 else 1
    GC = G * C

    GS = G * DN_HD

    def body(x_ref, gt_ref, st_ref, tm_ref, do_ref, dx_ref, dgt_ref, ds_s, p_s, r_s, el_s, dsa_s):
        h = pl.program_id(1)

        @pl.when(h == 0)
        def _():
            dgt_ref[...] = jnp.zeros_like(dgt_ref)

        def load(gi):
            rows = pl.ds(pl.multiple_of(gi * GC, GC), GC)
            q, k, v = [x_ref[0, i, rows, :].reshape(G, C, DN_HD) for i in range(3)]
            return rows, q, k, v, gt_ref[0, rows, :].reshape(G, C, 128), tm_ref[0, 0, rows, :].reshape(G, C, C)

        def group_a(gi, _):
            rows, q, k, v, gt, tm = load(gi)
            srow = pl.ds(pl.multiple_of(gi * GS, GS), GS)
            c = _chunk_common(q, k, v, gt, h, tm=tm)
            qp = q * c["e"] - _bdot(c["intra"], c["w"], _BNN)
            p_s[srow, :] = _bdot(k * c["r"], c["w"], _BTN).reshape(GS, DN_HD)
            r_s[srow, :] = _bdot(qp, do_ref[0, rows, :].reshape(G, C, DN_HD), _BTN).reshape(GS, DN_HD)
            el_s[pl.ds(gi * G, G), :, :] = c["el"] * jnp.ones((G, 1, 128), F32)
            return 0

        lax.fori_loop(0, NC // G, group_a, 0)
        ds_s[...] = jnp.zeros_like(ds_s)

        def chunk(m, _):
            n = NC - 1 - m
            srow = pl.ds(pl.multiple_of(n * DN_HD, DN_HD), DN_HD)
            dsn = ds_s[...]
            dsa_s[srow, :] = dsn
            ds_s[...] = (dsn * el_s[n] + r_s[srow, :]) - _bdot(p_s[srow, :], dsn, _TN)
            return 0

        lax.fori_loop(0, NC, chunk, 0)

        def group_c(gi, _):
            rows, q, k, v, gt, tm = load(gi)
            c = _chunk_common(q, k, v, gt, h, tm=tm)
            incl, strict, decay, e, r, el, tm = c["incl"], c["strict"], c["decay"], c["e"], c["r"], c["el"], c["tm"]
            srow = pl.ds(pl.multiple_of(gi * GS, GS), GS)
            st = st_ref[0, 0, srow, :].reshape(G, DN_HD, DN_HD)
            dsn = dsa_s[srow, :].reshape(G, DN_HD, DN_HD)
            do = do_ref[0, rows, :].reshape(G, C, DN_HD)
            dvn = _bdot(k * r, dsn, _BNN) + _bdot(c["intra"], do, _BTN)
            v_new = c["u"] - _bdot(c["w"], st, _BNN)
            del_ = jnp.sum(jnp.sum(dsn * st, axis=2, keepdims=True), axis=1, keepdims=True)
            dkr = _bdot(v_new, dsn, _BNT)
            dqe = _bdot(do, st, _BNT)
            dintra = _bdot(do, v_new, _BNT)
            dw = -_bdot(dvn, st, _BNT)
            dqkd = jnp.where(incl, dintra, 0.0)
            dqk = dqkd * decay
            ddecay = dqkd * c["qk"]
            dq = dqe * e + _bdot(dqk, k, _BNN)
            dk = dkr * r + _bdot(dqk, q, _BTN)
            dtm = _bdot(dvn, c["vb"], _BNT) + _bdot(dw, c["kbe"], _BNT)
            dvb = _bdot(tm, dvn, _BTN)
            dkbe = _bdot(tm, dw, _BTN)
            dkb = dkbe * e
            de = jnp.sum(dqe * q, axis=2, keepdims=True) + jnp.sum(dkbe * c["kb"], axis=2, keepdims=True)
            da = -_pdot(tm, _pdot(dtm, tm, _BNT), _BTN)
            dlow = jnp.where(strict, da, 0.0)
            dkk = dlow * decay
            ddecay = ddecay + dlow * c["kk"]
            dkb = dkb + _bdot(dkk, k, _BNN)
            dk = dk + _bdot(dkk, c["kb"], _BTN) + dkb * c["beta"]
            dbeta = jnp.sum(dkb * k, axis=2, keepdims=True) + jnp.sum(dvb * v, axis=2, keepdims=True)
            dv = dvb * c["beta"]
            dd = ddecay * decay
            dgc = jnp.sum(dd, axis=2, keepdims=True) - _pdot(dd, c["ones"], _BTN)[:, :, :1]
            dr = jnp.sum(dkr * k, axis=2, keepdims=True)
            dgc = dgc + de * e - dr * r
            dgl = jnp.sum(dr * r, axis=1, keepdims=True) + del_ * el
            rowc = lax.broadcasted_iota(jnp.int32, (C, 1), 0)
            dgc = dgc + jnp.where(rowc == C - 1, dgl, 0.0)
            dg = _pdot(c["inclf"], dgc * c["ones"], _BTN)[:, :, :1]
            dx_ref[0, 0, rows, :] = dq.reshape(GC, DN_HD)
            dx_ref[0, 1, rows, :] = dk.reshape(GC, DN_HD)
            dx_ref[0, 2, rows, :] = dv.reshape(GC, DN_HD)
            lane = c["lane"]
            dgt_ref[0, rows, :] += (jnp.where(lane == h, dg, 0.0) + jnp.where(lane == h + DN_HEADS, dbeta, 0.0)).reshape(GC, 128)
            return 0

        lax.fori_loop(0, NC // G, group_c, 0)

    return _hosted_call(
        body, grid=(B, DN_HEADS),
        in_specs=[pl.BlockSpec((1, 3, T, 128), lambda b, h: (b, 0, 0, h)), pl.BlockSpec((1, T, 128), lambda b, h: (b, 0, 0)),
                  pl.BlockSpec((1, 1, NC * DN_HD, DN_HD), lambda b, h: (b, h, 0, 0)), pl.BlockSpec((1, 1, T, C), lambda b, h: (b, h, 0, 0)),
                  pl.BlockSpec((1, T, 128), lambda b, h: (b, 0, h))],
        out_specs=[pl.BlockSpec((1, 3, T, 128), lambda b, h: (b, 0, 0, h)), pl.BlockSpec((1, T, 128), lambda b, h: (b, 0, 0))],
        out_shape=[_sds((B, 3, T, DN_W), F32), _sds((B, T, 128), F32)],
        scratch_shapes=[pltpu.VMEM((DN_HD, DN_HD), F32)] + [pltpu.VMEM((NC * DN_HD, DN_HD), F32)] * 2
        + [pltpu.VMEM((NC, 1, 128), F32), pltpu.VMEM((NC * DN_HD, DN_HD), F32)],
        name=name, args=(qkv, gates, states, tms, dob), exch=exch)


def ada_fwd(c_all, w_ada, b_sl, name):
    L, D, W = w_ada.shape
    NBt = c_all.shape[0]

    def body(c_ref, w_ref, b_ref, o_ref):
        cv = c_ref[...]
        o_ref[0] = _pdot(cv * _sigmoid(cv), w_ref[0]) + b_ref[0]

    return pl.pallas_call(
        body, grid=(L,),
        in_specs=[pl.BlockSpec((NBt, D), lambda l: (0, 0)), pl.BlockSpec((1, D, W), lambda l: (l, 0, 0)), pl.BlockSpec((1, 1, W), lambda l: (l, 0, 0))],
        out_specs=pl.BlockSpec((1, NBt, W), lambda l: (l, 0, 0)),
        out_shape=_sds((L, NBt, W), F32),
        compiler_params=_cp(("parallel",)), name=name,
    )(c_all, w_ada, b_sl)


def ada_bwd(c_all, dmod_cols, name):
    L, NBt, W = dmod_cols.shape
    D = c_all.shape[1]

    def body(c_ref, d_ref, o_ref):
        cv = c_ref[...]
        o_ref[0] = _pdot(cv * _sigmoid(cv), d_ref[0], _TN)

    return pl.pallas_call(
        body, grid=(L,),
        in_specs=[pl.BlockSpec((NBt, D), lambda l: (0, 0)), pl.BlockSpec((1, NBt, W), lambda l: (l, 0, 0))],
        out_specs=pl.BlockSpec((1, D, W), lambda l: (l, 0, 0)),
        out_shape=_sds((L, D, W), F32),
        compiler_params=_cp(("parallel",)), name=name,
    )(c_all, dmod_cols)


def adamw(partials, w, m, v, name):
    L, R, C = w.shape
    per_layer = isinstance(partials, (list, tuple))
    plist = list(partials) if per_layer else [partials]
    P = plist[0].shape[0]
    tr = _pick(R, 256)

    def body(*refs):
        p_refs = refs[:len(plist)]
        w_ref, m_ref, v_ref, g_ref, d_ref, nm_ref, nv_ref = refs[len(plist):]

        def run(read):
            g = read(0).astype(F32)
            for i in range(1, P):
                g = g + read(i).astype(F32)
            nm = ADAM_B1 * m_ref[0] + (1.0 - ADAM_B1) * g
            nv = ADAM_B2 * v_ref[0] + (1.0 - ADAM_B2) * (g * g)
            m_hat = nm / (1.0 - ADAM_B1 ** ADAM_STEP)
            v_hat = nv / (1.0 - ADAM_B2 ** ADAM_STEP)
            g_ref[0] = g
            d_ref[0] = -ADAM_LR * (m_hat / (jnp.sqrt(v_hat) + ADAM_EPS) + ADAM_WD * w_ref[0])
            nm_ref[0] = nm
            nv_ref[0] = nv

        if per_layer:
            for l in range(L):
                @pl.when(pl.program_id(0) == l)
                def _(l=l):
                    run(lambda i: p_refs[l][i])
        else:
            run(lambda i: p_refs[0][i, 0])

    tile = pl.BlockSpec((1, tr, C), lambda l, i: (l, i, 0))
    if per_layer:
        p_specs = [pl.BlockSpec((P, tr, C), lambda l, i, k=k: (0, jnp.where(l == k, i, 0), 0)) for k in range(L)]
    else:
        p_specs = [pl.BlockSpec((P, 1, tr, C), lambda l, i: (0, l, i, 0))]
    return pl.pallas_call(
        body, grid=(L, R // tr),
        in_specs=p_specs + [tile, tile, tile],
        out_specs=[tile] * 4, out_shape=[_sds((L, R, C), F32)] * 4,
        compiler_params=_cp(("arbitrary", "arbitrary")), name=name,
    )(*plist, w, m, v)


def _coords():
    return lax.axis_index("x"), lax.axis_index("y"), lax.axis_index("c")


def all_gather(x, name):
    return all_gather_many([x], name)[0]


def all_gather_many(xs, name):
    any_spec = pl.BlockSpec(memory_space=pl.ANY)
    n = len(xs)

    def body(*refs):
        x_refs, out_refs = refs[:n], refs[n:2 * n]
        send_sems, recv_sems, local_sems = refs[2 * n:]
        x_, y_, c_ = _coords()
        me, sibling = (x_, y_, c_), (x_, y_, 1 - c_)
        chips = [(1 - x_, y_), (x_, 1 - y_), (1 - x_, 1 - y_)]

        def copy(a, k, block, to, own=False):
            px, py, pc = block
            rows = out_refs[a].at[4 * px + 2 * py + pc]
            return pltpu.make_async_remote_copy(
                src_ref=x_refs[a] if own else rows, dst_ref=rows,
                send_sem=send_sems.at[7 * a + k], recv_sem=recv_sems.at[7 * a + k],
                device_id=to, device_id_type=pl.DeviceIdType.MESH)

        arrays = range(n)
        mine = [pltpu.make_async_copy(x_refs[a], out_refs[a].at[4 * x_ + 2 * y_ + c_], local_sems.at[a]) for a in arrays]
        first = [copy(a, 0, me, sibling, own=True) for a in arrays]
        first += [copy(a, 1 + j, me, (*chip, c_), own=True) for a in arrays for j, chip in enumerate(chips)]
        for cp in mine + first:
            cp.start()
        passed = []
        for a in arrays:
            for j, chip in enumerate(chips):
                copy(a, 1 + j, (*chip, c_), me).wait_recv()
                passed.append(copy(a, 4 + j, (*chip, c_), sibling))
                passed[-1].start()
        for a in arrays:
            copy(a, 0, sibling, me).wait_recv()
            for j, chip in enumerate(chips):
                copy(a, 4 + j, (*chip, 1 - c_), me).wait_recv()
        for cp in first + passed:
            cp.wait_send()
        for cp in mine:
            cp.wait()

    return pl.pallas_call(
        body, out_shape=[_sds((N_DEV,) + x.shape, x.dtype) for x in xs],
        in_specs=[any_spec] * n, out_specs=[any_spec] * n,
        scratch_shapes=[pltpu.SemaphoreType.DMA((7 * n,)), pltpu.SemaphoreType.DMA((7 * n,)), pltpu.SemaphoreType.DMA((n,))],
        name=name,
    )(*xs)


def all_to_all(x, name):
    any_spec = pl.BlockSpec(memory_space=pl.ANY)

    def body(x_ref, out_ref, send_sems, recv_sems, local_sem):
        x_, y_, c_ = _coords()
        me = 4 * x_ + 2 * y_ + c_
        mine = pltpu.make_async_copy(x_ref.at[me], out_ref.at[me], local_sem)
        mine.start()
        copies = []
        for k in range(1, N_DEV):
            px = 1 - x_ if k & 4 else x_
            py = 1 - y_ if k & 2 else y_
            pc = 1 - c_ if k & 1 else c_
            peer = 4 * px + 2 * py + pc
            copies.append((pltpu.make_async_remote_copy(
                src_ref=x_ref.at[peer], dst_ref=out_ref.at[me],
                send_sem=send_sems.at[k - 1], recv_sem=recv_sems.at[k - 1],
                device_id=(px, py, pc), device_id_type=pl.DeviceIdType.MESH), peer))
        for cp, _ in copies:
            cp.start()
        for k, (cp, peer) in enumerate(copies):
            pltpu.make_async_remote_copy(
                src_ref=x_ref.at[peer], dst_ref=out_ref.at[peer],
                send_sem=send_sems.at[k], recv_sem=recv_sems.at[k],
                device_id=(x_, y_, c_), device_id_type=pl.DeviceIdType.MESH).wait_recv()
        for cp, _ in copies:
            cp.wait_send()
        mine.wait()

    return pl.pallas_call(
        body, out_shape=_sds(x.shape, x.dtype),
        in_specs=[any_spec], out_specs=any_spec,
        scratch_shapes=[pltpu.SemaphoreType.DMA((7,)), pltpu.SemaphoreType.DMA((7,)), pltpu.SemaphoreType.DMA],
        name=name,
    )(x)


def _rows128(a):
    return a.reshape(-1, 128)


def _pad_lanes(a):
    return jnp.pad(a, ((0, 0), (0, 128 - a.shape[1])))


def kernel(x, c, w_ada, b_ada, norm_mix, norm_mlp, w_in, sb_q_norm, sb_k_norm, conv_w, a_log, dt_bias, dn_out_norm, w_out, w_ff1, w_ff2, loss_target, m_w_ada, m_b_ada, m_norm_mix, m_norm_mlp, m_w_in, m_sb_q_norm, m_sb_k_norm, m_conv_w, m_a_log, m_dt_bias, m_dn_out_norm, m_w_out, m_w_ff1, m_w_ff2, v_w_ada, v_b_ada, v_norm_mix, v_norm_mlp, v_w_in, v_sb_q_norm, v_sb_k_norm, v_conv_w, v_a_log, v_dt_bias, v_dn_out_norm, v_w_out, v_w_ff1, v_w_ff2):
    B, T, D = x.shape
    L = w_ada.shape[0]
    N = B * T
    FF = w_ff1.shape[2] * N_DEV
    WA = w_ada.shape[2]
    CS = conv_w.shape[2]
    me = 4 * lax.axis_index("x") + 2 * lax.axis_index("y") + lax.axis_index("c")
    tm = _pick(T, 1024)

    wb = [w.astype(BF16) for w in (w_in, w_out, w_ff1, w_ff2)]

    def assemble(lands):
        win_g, wout_g, w1_g, w2_g = lands
        return (jnp.pad(win_g.transpose(1, 0, 2).reshape(D, IN_W), ((0, 0), (0, IN_WP - IN_W))),
                wout_g.reshape(SB_W + DN_W, D), w1_g.transpose(1, 0, 2).reshape(D, FF), w2_g.reshape(FF, D))

    *lands0, conv_g, c_g = all_gather_many([w[0] for w in wb] + [conv_w, c], "comm_gather_first")
    weights = [assemble(lands0)]
    conv_full = conv_g.transpose(1, 2, 0, 3).reshape(L, CONV_K, 3 * DN_W)

    c_all = c_g.reshape(N_DEV * B, D)
    b_sl = lax.dynamic_slice_in_dim(b_ada, me * WA, WA, axis=1).reshape(L, 1, WA)
    mod_sh = ada_fwd(c_all, w_ada, b_sl, "ada_fwd")
    mod_g = all_gather(mod_sh, "comm_gather_mod")
    mod = lax.dynamic_slice_in_dim(mod_g, me * B, B, axis=2).transpose(1, 2, 0, 3).reshape(L, B, 6 * D)

    def mod_part(l, i):
        return mod[l, :, i * D:(i + 1) * D].reshape(B, 1, D)

    alog_row = _pad_lanes(a_log).reshape(L, 1, 128)
    dtb_row = _pad_lanes(dt_bias).reshape(L, 1, 128)

    def gate_epi(acc, xv, g):
        return acc, xv + g[0] * acc

    def relu2(a):
        r = jnp.maximum(a.astype(F32), 0.0)
        return r * r

    def times_gate(a, g):
        return a * g[0]

    tile_ij = lambda i, j, k: (i, j)

    saved = []
    xc = x
    for l in range(L):
        sh_a, sc_a, g_a, sh_m, sc_m, g_m = [mod_part(l, i) for i in range(6)]
        h = ln_mod_fwd(xc, norm_mix[l:l + 1], sc_a, sh_a, "ln_mod_fwd")
        W_in_l, W_out_l, W_1_l, W_2_l = weights[l]
        proj3 = matmul(h.reshape(N, D), W_in_l, mode="nn", name="mm_proj", tm=256)[0].reshape(B, T, IN_WP)
        qkv = gdn_pre_fwd(proj3, conv_full[l], "gdn_pre_fwd")
        gates = gdn_gates_fwd(proj3, alog_row[l], dtb_row[l], "gdn_gates_fwd")
        if l + 1 < L:
            o_a, tot, *lands = sb_attn_fwd(proj3, sb_q_norm[l:l + 1], sb_k_norm[l:l + 1], "sb_attn_fwd_gather",
                                           exch=("gather", [w[l + 1] for w in wb[:3]]))
            ob, states, tms, land_w2 = gdn_chunk_fwd(qkv, gates, "gdn_chunk_fwd_gather", exch=("gather", [wb[3][l + 1]]))
            weights.append(assemble(lands + [land_w2]))
        else:
            o_a, tot = sb_attn_fwd(proj3, sb_q_norm[l:l + 1], sb_k_norm[l:l + 1], "sb_attn_fwd")
            ob, states, tms = gdn_chunk_fwd(qkv, gates, "gdn_chunk_fwd")
        o_b = gdn_post_fwd(ob, proj3, dn_out_norm[l:l + 1], "gdn_post_fwd")
        mix = jnp.concatenate([o_a, o_b], axis=-1)
        y1, x_mid = matmul(
            mix.reshape(N, SB_W + DN_W), W_out_l, mode="nn", name="mm_out", out_dtypes=(BF16, F32), tm=tm, epi=gate_epi,
            extras=[(xc.reshape(N, D), (tm, _pick(D, 1024)), tile_ij),
                    (g_a, (1, 1, _pick(D, 1024)), lambda i, j, k: (i * tm // T, 0, j))])
        x_mid = x_mid.reshape(B, T, D)
        h2 = ln_mod_fwd(x_mid, norm_mlp[l:l + 1], sc_m, sh_m, "ln_mod_fwd")
        u = matmul(h2.reshape(N, D), W_1_l, mode="nn", name="mm_ff1", out_dtypes=(BF16,))[0]
        y2, x_out = matmul(
            u, W_2_l, mode="nn", name="mm_ff2", out_dtypes=(BF16, F32), tm=tm, a_fn=relu2, epi=gate_epi,
            extras=[(x_mid.reshape(N, D), (tm, _pick(D, 1024)), tile_ij),
                    (g_m, (1, 1, _pick(D, 1024)), lambda i, j, k: (i * tm // T, 0, j))])
        saved.append(dict(x=xc, h=h, proj3=proj3, tot=tot, qkv=qkv, gates=gates, states=states, tms=tms, ob=ob, mix=mix,
                          y1=y1, x_mid=x_mid, h2=h2, u=u, y2=y2))
        xc = x_out.reshape(B, T, D)

    dx, sq = loss_grad(xc, loss_target, "loss_grad")
    loss = lax.psum((0.5 / D) * jnp.sum(sq), AXES)

    g_win, g_wout, g_w1, g_w2, dmods, smalls, parts = [], [], [], [], [], [], []
    pending = None
    tk_tok = tm
    wi = w_in.shape[2]

    def shard_layer(gin, gout, g1, g2):
        return [None if gin is None else gin[:, :IN_W].reshape(D, N_DEV, wi).transpose(1, 0, 2), gout.reshape(N_DEV, w_out.shape[1], D),
                g1.reshape(D, N_DEV, w_ff1.shape[2]).transpose(1, 0, 2), g2.reshape(N_DEV, w_ff2.shape[1], D)]

    for l in reversed(range(L)):
        s = saved[l]
        W_in_l, W_out_l, W_1_l, W_2_l = weights[l]
        sh_a, sc_a, g_a, sh_m, sc_m, g_m = [mod_part(l, i) for i in range(6)]
        gate_k = lambda g, blk: (g, (1, 1, blk), lambda i, j, k: (i * tm // T, 0, k))
        gate_tok = lambda g, blk: (g, (1, 1, blk), lambda i, j, k: (k * tk_tok // T, 0, j))
        dx2 = dx.reshape(N, D)
        dg_m = rowsum_prod(dx, s["y2"].reshape(B, T, D), "rowsum_prod") if l == L - 1 else dg_m_below
        du = matmul(dx2, W_2_l, mode="nt", name="mm_ff2_da", out_dtypes=(BF16,), tm=tm, a_fn=times_gate,
                    a_extras=[gate_k(g_m, _pick(D, 1024))],
                    epi=lambda acc, uv: (acc * (2.0 * jnp.maximum(uv, 0.0)),),
                    extras=[(s["u"], (tm, _pick(FF, 1024)), tile_ij)])[0]
        g_w2.append(matmul(s["u"], dx2, mode="tn", name="mm_ff2_dw", out_dtypes=(BF16,), tk=tk_tok, a_fn=relu2,
                           b_fn=times_gate, b_extras=[gate_tok(g_m, _pick(D, 1024))])[0])
        g_w1.append(matmul(s["h2"].reshape(N, D), du, mode="tn", name="mm_ff1_dw", out_dtypes=(BF16,))[0])
        dh2 = matmul(du, W_1_l, mode="nt", name="mm_ff1_da")[0]
        dx_mid, dgn_mlp, dsc_m, dsh_m, dg_a = ln_mod_bwd(s["x_mid"], norm_mlp[l:l + 1], sc_m, dh2.reshape(B, T, D), dx,
                                                         s["y1"].reshape(B, T, D), "ln_mod_bwd")
        dxm2 = dx_mid.reshape(N, D)
        dmix3 = matmul(dxm2, W_out_l, mode="nt", name="mm_out_da", tm=tm, a_fn=times_gate,
                       a_extras=[gate_k(g_a, _pick(D, 1024))])[0].reshape(B, T, SB_W + DN_W)
        g_wout.append(matmul(s["mix"].reshape(N, SB_W + DN_W), dxm2, mode="tn", name="mm_out_dw", out_dtypes=(BF16,),
                             tk=tk_tok, b_fn=times_gate, b_extras=[gate_tok(g_a, _pick(D, 1024))])[0])
        own = shard_layer(None, g_wout[-1], g_w1[-1], g_w2[-1])[1:] if l == 0 else []
        srcs = (pending or []) + own[:1]
        if srcs:
            dq_a, dk_a, dv_a, dgq, dgk, *lands = sb_attn_bwd(
                s["proj3"], sb_q_norm[l:l + 1], sb_k_norm[l:l + 1], s["tot"], dmix3,
                "sb_attn_bwd_scatter" if l else "sb_attn_bwd_scatter0", exch=("scatter", srcs))
            if pending:
                parts.append(lands[:4])
        else:
            dq_a, dk_a, dv_a, dgq, dgk = sb_attn_bwd(s["proj3"], sb_q_norm[l:l + 1], sb_k_norm[l:l + 1], s["tot"], dmix3, "sb_attn_bwd")
        dob, dz, dgn_dn = gdn_post_bwd(s["ob"], s["proj3"], dn_out_norm[l:l + 1], dmix3, "gdn_post_bwd")
        if own:
            dqkv, dgates, *lands_ff = gdn_chunk_bwd(s["qkv"], s["gates"], s["states"], s["tms"], dob, "gdn_chunk_bwd_scatter",
                                                    exch=("scatter", own[1:]))
            lands0 = lands[-1:] + lands_ff
        else:
            dqkv, dgates = gdn_chunk_bwd(s["qkv"], s["gates"], s["states"], s["tms"], dob, "gdn_chunk_bwd")
        d_dnqkv, dconv_b = gdn_pre_bwd(s["proj3"], conv_full[l], dqkv, "gdn_pre_bwd")
        d_ab, dalog, ddtb = gdn_gates_bwd(s["proj3"], alog_row[l], dtb_row[l], dgates, "gdn_gates_bwd")
        dproj = [a.reshape(N, a.shape[-1]) for a in (dq_a, dk_a, dv_a, d_dnqkv, dz, d_ab)]
        g_win.append(proj_bwd_weight(s["h"].reshape(N, D), dproj, "mm_proj_dw"))
        pending = shard_layer(g_win[-1], g_wout[-1], g_w1[-1], g_w2[-1])
        if l == 0:
            dh, land_win0 = proj_bwd_input(dproj, W_in_l, "mm_proj_da_scatter", exch=("scatter", pending[:1]))
        else:
            dh = proj_bwd_input(dproj, W_in_l, "mm_proj_da")[0]
        y_below = saved[max(l - 1, 0)]["y2"].reshape(B, T, D)
        dx, dgn_mix, dsc_a, dsh_a, dg_m_below = ln_mod_bwd(s["x"], norm_mix[l:l + 1], sc_a, dh.reshape(B, T, D), dx_mid,
                                                           y_below, "ln_mod_bwd")
        dmods.append(jnp.concatenate([dsh_a, dsc_a, dg_a, dsh_m, dsc_m, dg_m], axis=-1).reshape(B, 6 * D))
        smalls.append(dict(norm_mix=dgn_mix, norm_mlp=dgn_mlp, sbq=dgq, sbk=dgk, alog=dalog, dtb=ddtb, dnorm=dgn_dn,
                           conv=jnp.sum(dconv_b, axis=0)))
    parts.append([land_win0] + lands0)
    for lst in (dmods, smalls, parts):
        lst.reverse()
    grad_x = dx

    def update2d(parts, w, m, v, name):
        return [o[0] for o in adamw(parts[:, None], w[None], m[None], v[None], name)]

    p_win, p_wout, p_w1, p_w2 = [[parts[l][i] for l in range(L)] for i in range(4)]
    r_win = adamw(p_win, w_in, m_w_in, v_w_in, "adamw_w_in")
    r_wout = adamw(p_wout, w_out, m_w_out, v_w_out, "adamw_w_out")
    r_w1 = adamw(p_w1, w_ff1, m_w_ff1, v_w_ff1, "adamw_w_ff1")
    r_w2 = adamw(p_w2, w_ff2, m_w_ff2, v_w_ff2, "adamw_w_ff2")

    def pack(f):
        return jnp.concatenate([
            _rows128(f("norm_mix")), _rows128(f("norm_mlp")), _rows128(f("sbq")), _rows128(f("sbk")),
            f("alog"), f("dtb"), f("dnorm"), _rows128(f("conv"))], axis=0)

    part = pack(lambda n: jnp.concatenate([sm[n] for sm in smalls], axis=0))
    dmod_g, part_g = all_gather_many([jnp.stack(dmods), part], "comm_gather_last")

    dmod_all = dmod_g.transpose(1, 0, 2, 3).reshape(L, N_DEV * B, 6 * D)
    g_wada = ada_bwd(c_all, lax.dynamic_slice_in_dim(dmod_all, me * WA, WA, axis=2), "ada_bwd")
    r_wada = adamw(g_wada[None], w_ada, m_w_ada, v_w_ada, "adamw_w_ada")
    r_bada = update2d(dmod_g.transpose(0, 2, 1, 3).reshape(N_DEV * B, L, 6 * D), b_ada, m_b_ada, v_b_ada, "adamw_b_ada")

    names = ["norm_mix", "norm_mlp", "sbq", "sbk", "alog", "dtb", "dnorm"]
    n_rep = part.shape[0] - L * CONV_K * 3 * DN_W // 128
    params = dict(norm_mix=(norm_mix, m_norm_mix, v_norm_mix), norm_mlp=(norm_mlp, m_norm_mlp, v_norm_mlp),
                  sbq=(sb_q_norm, m_sb_q_norm, v_sb_q_norm), sbk=(sb_k_norm, m_sb_k_norm, v_sb_k_norm),
                  alog=(a_log, m_a_log, v_a_log), dtb=(dt_bias, m_dt_bias, v_dt_bias),
                  dnorm=(dn_out_norm, m_dn_out_norm, v_dn_out_norm))

    def rows_of(n, a):
        return _pad_lanes(a) if n in ("alog", "dtb") else _rows128(a)

    packed = [jnp.concatenate([rows_of(n, params[n][i]) for n in names], axis=0) for i in range(3)]
    r_small = update2d(part_g[:, :n_rep], packed[0], packed[1], packed[2], "adamw_small")
    small_out = {}
    off = 0
    for n in names:
        w0 = params[n][0]
        nr = rows_of(n, w0).shape[0]
        vals = [o[off:off + nr] for o in r_small]
        small_out[n] = [(vv[:, :w0.shape[1]] if n in ("alog", "dtb") else vv.reshape(w0.shape)) for vv in vals]
        off += nr
    conv_parts = part_g[:, n_rep:].reshape(N_DEV, L, CONV_K, 3 * DN_W)
    r_conv = adamw(lax.dynamic_slice_in_dim(conv_parts, me * CS, CS, axis=3), conv_w, m_conv_w, v_conv_w, "adamw_conv")

    order = [r_wada, r_bada, small_out["norm_mix"], small_out["norm_mlp"], r_win, small_out["sbq"], small_out["sbk"],
             r_conv, small_out["alog"], small_out["dtb"], small_out["dnorm"], r_wout, r_w1, r_w2]
    outs = [loss, grad_x]
    for i in range(4):
        outs += [r[i] for r in order]
    return tuple(outs)
```
